```python
import math
import jax
import jax.numpy as jnp
from jax import lax
import numpy as np

D_MODEL = 1024
BATCH = 4
SEQ = 8192
DEPTH = 4

BLOCK = 128
HEAD_DIM = 64
N_BRANCH = 4
BRANCH_WIDTH = 512
EPS = 1e-6

A_HEADS = 8
IDX_HEADS = 8
IDX_DIM = 32
TOPK_MAX = 256

B_GROUPS = ((128, 1), (512, 4), (2048, 16))
B_HEADS = 8

C_HEADS = 8
C_NOPE = 64
C_ROPE = 32
C_V = 64
C_Q_LORA = 256
C_KV_LORA = 128
ROPE_THETA = 10000.0

D_HEADS = 8
D_KV_HEADS = 2
D_WINDOW = 128

NUM_BUCKETS = 32
MAX_DISTANCE = 2048
N_BIAS_HEADS = A_HEADS + len(B_GROUPS) * B_HEADS + D_HEADS

A_SIZES = (A_HEADS * HEAD_DIM, A_HEADS * HEAD_DIM, A_HEADS * HEAD_DIM, BRANCH_WIDTH,
           IDX_HEADS * IDX_DIM, IDX_DIM, IDX_HEADS)
B_SIZES = (len(B_GROUPS) * B_HEADS * HEAD_DIM,) * 3 + (BRANCH_WIDTH,)
C_SIZES = (C_Q_LORA, C_KV_LORA, C_ROPE, BRANCH_WIDTH)
D_SIZES = (D_HEADS * HEAD_DIM, D_KV_HEADS * HEAD_DIM, D_KV_HEADS * HEAD_DIM, BRANCH_WIDTH)
GATE_SIZES = (D_MODEL,) * N_BRANCH
IN_GROUPS = (A_SIZES, B_SIZES, C_SIZES, D_SIZES, GATE_SIZES)
N_IN = sum(sum(s) for s in IN_GROUPS)

kernel_name = 'hybrid_parallel_gated_mixers'


def rms_norm(x, gain):
    xf = x.astype(jnp.float32)
    y = xf * lax.rsqrt(jnp.mean(xf * xf, axis=-1, keepdims=True) + EPS)
    return (y * gain.astype(jnp.float32)).astype(x.dtype)


def t5_bucket(dist):
    max_exact = NUM_BUCKETS // 2
    d = jnp.maximum(dist, 0)
    logd = jnp.log(jnp.maximum(d, 1).astype(jnp.float32) / max_exact)
    large = max_exact + (logd / math.log(MAX_DISTANCE / max_exact) * (NUM_BUCKETS - max_exact)).astype(jnp.int32)
    return jnp.where(d < max_exact, d, jnp.minimum(large, NUM_BUCKETS - 1))


def rope(x, pos):
    half = x.shape[-1] // 2
    freq = ROPE_THETA ** (-jnp.arange(half, dtype=jnp.float32) / half)
    ang = pos.astype(jnp.float32)[:, None] * freq[None, :]
    cos, sin = jnp.cos(ang)[:, None, :], jnp.sin(ang)[:, None, :]
    xf = x.astype(jnp.float32)
    x1, x2 = xf[..., :half], xf[..., half:]
    return jnp.concatenate([x1 * cos - x2 * sin, x2 * cos + x1 * sin], axis=-1).astype(x.dtype)


def combined_projection(xn, w):
    groups = []
    start = 0
    for sizes in IN_GROUPS:
        width = sum(sizes)
        h = xn @ w[:, start:start + width]
        cuts, acc = [], 0
        for s in sizes[:-1]:
            acc += s
            cuts.append(acc)
        groups.append(jnp.split(h, cuts, axis=-1))
        start += width
    return groups


def blockify(t, nb):
    return t.reshape(t.shape[0], nb, BLOCK, *t.shape[2:]).swapaxes(0, 1)


def band_bias(table, step):
    rel = BLOCK + jnp.arange(BLOCK)[:, None] - jnp.arange(2 * BLOCK)[None, :]
    return table[t5_bucket(rel * step)].transpose(2, 0, 1)


def banded_attention(q, k, v, max_dist, bias, sink=None):
    b, L, hq, dh = q.shape
    hkv = k.shape[2]
    g = hq // hkv
    nb = L // BLOCK
    qb = q.reshape(b, nb, BLOCK, hkv, g, dh)

    def with_prev(t):
        tb = t.reshape(b, nb, BLOCK, hkv, dh)
        prev = jnp.pad(tb[:, :-1], ((0, 0), (1, 0), (0, 0), (0, 0), (0, 0)))
        return jnp.concatenate([prev, tb], axis=2)

    kb, vb = with_prev(k), with_prev(v)
    s = jnp.einsum('bnqkgd,bnskd->bnkgqs', qb, kb, preferred_element_type=jnp.float32) * dh ** -0.5
    s = s + bias.reshape(hkv, g, BLOCK, 2 * BLOCK).astype(jnp.float32)
    kj = jnp.arange(2 * BLOCK)[None, :]
    rel = BLOCK + jnp.arange(BLOCK)[:, None] - kj
    key_ok = (jnp.arange(nb)[:, None, None] * BLOCK - BLOCK + kj[None]) >= 0
    mask = (rel >= 0)[None] & (rel <= max_dist)[None] & key_ok
    s = jnp.where(mask[None, :, None, None], s, -jnp.inf)
    m = jnp.max(s, axis=-1)
    if sink is not None:
        sk = sink.astype(jnp.float32).reshape(hkv, g)[None, None, :, :, None]
        m = jnp.maximum(m, sk)
    p = jnp.exp(s - m[..., None])
    den = jnp.sum(p, axis=-1)
    if sink is not None:
        den = den + jnp.exp(sk - m)
    den_q = jnp.moveaxis(den, -1, 2)
    o = jnp.einsum('bnkgqs,bnskd->bnqkgd', p, vb.astype(jnp.float32)) / den_q[..., None]
    return (o.reshape(b, L, hq, dh), jnp.moveaxis(m, -1, 2).reshape(b, L, hq), den_q.reshape(b, L, hq))


def dsa_attention(q, k, v, iq, ik, iw, bias_table):
    b, S, h, dh = q.shape
    nb = S // BLOCK
    k_sel = min(TOPK_MAX, S // 4)
    key_pos = jnp.arange(S)

    def one_block(args):
        qb, iqb, iwb, qpos = args
        rel = jax.nn.relu(jnp.einsum('bqhd,bsd->bqhs', iqb, ik, preferred_element_type=jnp.float32))
        score = jnp.einsum('bqhs,bqh->bqs', rel, iwb.astype(jnp.float32))
        score = jnp.where(key_pos[None, None, :] <= qpos[None, :, None], score, -jnp.inf)
        _, idx = lax.top_k(score, k_sel)
        valid = idx <= qpos[None, :, None]
        ks = jax.vmap(lambda kk, ii: kk[ii])(k, idx)
        vs = jax.vmap(lambda vv, ii: vv[ii])(v, idx)
        logits = jnp.einsum('bqhd,bqkhd->bhqk', qb, ks, preferred_element_type=jnp.float32) * dh ** -0.5
        bias = bias_table[t5_bucket(qpos[None, :, None] - idx)]
        logits = logits + bias.transpose(0, 3, 1, 2).astype(jnp.float32)
        logits = jnp.where(valid[:, None], logits, -jnp.inf)
        p = jax.nn.softmax(logits, axis=-1)
        return jnp.einsum('bhqk,bqkhd->bqhd', p, vs.astype(jnp.float32))

    out = lax.map(one_block, (blockify(q, nb), blockify(iq, nb), blockify(iw, nb), key_pos.reshape(nb, BLOCK)))
    return out.swapaxes(0, 1).reshape(b, S, h, dh)


def dilated_mixture(q, k, v, rel_bias):
    b, S, _, h, dh = q.shape
    outs, ms, dens = [], [], []
    for g, (window, dil) in enumerate(B_GROUPS):
        sub_len = S // dil
        pad_len = -(-sub_len // BLOCK) * BLOCK

        def to_sub(t):
            t = t.reshape(b, sub_len, dil, h, dh).transpose(0, 2, 1, 3, 4).reshape(b * dil, sub_len, h, dh)
            return jnp.pad(t, ((0, 0), (0, pad_len - sub_len), (0, 0), (0, 0)))

        def from_sub(t):
            t = t[:, :sub_len]
            return t.reshape(b, dil, sub_len, *t.shape[2:]).swapaxes(1, 2).reshape(b, S, *t.shape[2:])

        table = rel_bias[:, A_HEADS + g * B_HEADS:A_HEADS + (g + 1) * B_HEADS]
        o, m, den = banded_attention(to_sub(q[:, :, g]), to_sub(k[:, :, g]), to_sub(v[:, :, g]),
                                     window // dil, band_bias(table, dil))
        outs.append(from_sub(o))
        ms.append(from_sub(m))
        dens.append(from_sub(den))
    m_all = jnp.stack(ms)
    w = jnp.stack(dens) * jnp.exp(m_all - jnp.max(m_all, axis=0, keepdims=True))
    w = w / jnp.sum(w, axis=0, keepdims=True)
    return sum(w[g][..., None] * outs[g] for g in range(len(B_GROUPS)))


def causal_dense_attention(q, k, v):
    b, S, h, dqk = q.shape
    nb = S // BLOCK
    key_pos = jnp.arange(S)
    vf = v.astype(jnp.float32)

    def one_block(args):
        qb, qpos = args
        s = jnp.einsum('bqhd,bshd->bhqs', qb, k, preferred_element_type=jnp.float32) * dqk ** -0.5
        s = jnp.where(key_pos[None, :] <= qpos[:, None], s, -jnp.inf)
        p = jax.nn.softmax(s, axis=-1)
        return jnp.einsum('bhqs,bshd->bqhd', p, vf)

    out = lax.map(one_block, (blockify(q, nb), key_pos.reshape(nb, BLOCK)))
    return out.swapaxes(0, 1).reshape(b, S, h, v.shape[-1])


def setup_inputs(seed: int = 0) -> dict:
    key = jax.random.key(seed)
    ks = jax.random.split(key, 16)
    f32 = jnp.float32

    def normal(k, shape, scale):
        return jax.random.normal(k, shape, f32) * scale

    def gain(k, shape):
        return 1.0 + 0.05 * jax.random.normal(k, shape, f32)

    return {
        'x': normal(ks[0], (BATCH, SEQ, D_MODEL), 1.0),
        'norm_gain': gain(ks[1], (DEPTH, D_MODEL)),
        'w_in': normal(ks[2], (DEPTH, D_MODEL, N_IN), D_MODEL ** -0.5),
        'qk_gain_a': gain(ks[3], (DEPTH, 2, HEAD_DIM)),
        'qk_gain_b': gain(ks[4], (DEPTH, 2, HEAD_DIM)),
        'qk_gain_c': gain(ks[5], (DEPTH, 2, C_NOPE + C_ROPE)),
        'qk_gain_d': gain(ks[6], (DEPTH, 2, HEAD_DIM)),
        'c_q_gain': gain(ks[7], (DEPTH, C_Q_LORA)),
        'c_kv_gain': gain(ks[8], (DEPTH, C_KV_LORA)),
        'w_q_b': normal(ks[9], (DEPTH, C_Q_LORA, C_HEADS * (C_NOPE + C_ROPE)), C_Q_LORA ** -0.5),
        'w_kv_b': normal(ks[10], (DEPTH, C_KV_LORA, C_HEADS * (C_NOPE + C_V)), C_KV_LORA ** -0.5),
        'sinks': normal(ks[11], (DEPTH, D_HEADS), 0.5),
        'rel_bias': normal(ks[12], (NUM_BUCKETS, N_BIAS_HEADS), 0.1),
        'w_branch': normal(ks[13], (DEPTH, N_BRANCH, BRANCH_WIDTH, D_MODEL), BRANCH_WIDTH ** -0.5),
        'w_out': normal(ks[14], (DEPTH, D_MODEL, D_MODEL), D_MODEL ** -0.5),
    }


def reference(x, norm_gain, w_in, qk_gain_a, qk_gain_b, qk_gain_c, qk_gain_d, c_q_gain, c_kv_gain,
              w_q_b, w_kv_b, sinks, rel_bias, w_branch, w_out):
    b, S, _ = x.shape
    pos = jnp.arange(S)
    bias_a = rel_bias[:, :A_HEADS]
    bias_d = rel_bias[:, N_BIAS_HEADS - D_HEADS:]
    n_g = len(B_GROUPS)
    for l in range(DEPTH):
        xn = rms_norm(x, norm_gain[l])
        ((a_q, a_k, a_v, a_z, a_iq, a_ik, a_iw), (b_q, b_k, b_v, b_z),
         (c_q, c_kv, c_pe, c_z), (d_q, d_k, d_v, d_z), gates) = combined_projection(xn, w_in[l])

        shp_a = (b, S, A_HEADS, HEAD_DIM)
        qa = rms_norm(a_q.reshape(shp_a), qk_gain_a[l, 0])
        ka = rms_norm(a_k.reshape(shp_a), qk_gain_a[l, 1])
        ya = dsa_attention(qa, ka, a_v.reshape(shp_a), a_iq.reshape(b, S, IDX_HEADS, IDX_DIM), a_ik, a_iw, bias_a)

        shp_b = (b, S, n_g, B_HEADS, HEAD_DIM)
        qbm = rms_norm(b_q.reshape(shp_b), qk_gain_b[l, 0])
        kbm = rms_norm(b_k.reshape(shp_b), qk_gain_b[l, 1])
        yb = dilated_mixture(qbm, kbm, b_v.reshape(shp_b), rel_bias)

        qc = (rms_norm(c_q, c_q_gain[l]) @ w_q_b[l]).reshape(b, S, C_HEADS, C_NOPE + C_ROPE)
        kv = (rms_norm(c_kv, c_kv_gain[l]) @ w_kv_b[l]).reshape(b, S, C_HEADS, C_NOPE + C_V)
        kc = jnp.concatenate([kv[..., :C_NOPE],
                              jnp.broadcast_to(c_pe[:, :, None, :], (b, S, C_HEADS, C_ROPE))], axis=-1)
        qc = rms_norm(qc, qk_gain_c[l, 0])
        kc = rms_norm(kc, qk_gain_c[l, 1])
        qc = jnp.concatenate([qc[..., :C_NOPE], rope(qc[..., C_NOPE:], pos)], axis=-1)
        kc = jnp.concatenate([kc[..., :C_NOPE], rope(kc[..., C_NOPE:], pos)], axis=-1)
        yc = causal_dense_attention(qc, kc, kv[..., C_NOPE:])

        qd = rms_norm(d_q.reshape(b, S, D_HEADS, HEAD_DIM), qk_gain_d[l, 0])
        kd = rms_norm(d_k.reshape(b, S, D_KV_HEADS, HEAD_DIM), qk_gain_d[l, 1])
        yd, _, _ = banded_attention(qd, kd, d_v.reshape(b, S, D_KV_HEADS, HEAD_DIM),
                                    D_WINDOW - 1, band_bias(bias_d, 1), sinks[l])

        ys = (ya, yb, yc, yd)
        zs = (a_z, b_z, c_z, d_z)
        merged = sum(jax.nn.sigmoid(gates[n]) *
                     ((ys[n].reshape(b, S, BRANCH_WIDTH).astype(x.dtype) * jax.nn.silu(zs[n])) @ w_branch[l, n])
                     for n in range(N_BRANCH))
        x = x + merged @ w_out[l]
    return x
```

```python
import functools
import math

import numpy as np
import jax
import jax.numpy as jnp
from jax import lax
from jax.experimental import pallas as pl
from jax.experimental.pallas import tpu as pltpu

F32 = jnp.float32
BF16 = jnp.bfloat16

D_MODEL = 1024
BLOCK = 128
HEAD_DIM = 64
N_BRANCH = 4
BRANCH_WIDTH = 512
EPS = 1e-6
A_HEADS = 8
IDX_HEADS = 8
IDX_DIM = 32
TOPK_MAX = 256
B_GROUPS = ((128, 1), (512, 4), (2048, 16))
B_HEADS = 8
C_HEADS = 8
C_NOPE = 64
C_ROPE = 32
C_V = 64
C_Q_LORA = 256
C_KV_LORA = 128
ROPE_THETA = 10000.0
D_HEADS = 8
D_KV_HEADS = 2
D_WINDOW = 128
NUM_BUCKETS = 32
MAX_DISTANCE = 2048
N_BIAS_HEADS = A_HEADS + len(B_GROUPS) * B_HEADS + D_HEADS

LANES = 128
NEG = -1e30
INT_MIN = -(2 ** 31)

_A0 = 0
_B0 = 2344
_C0 = 7464
_D0 = 8392
_G0 = 9672

PN_AQ, PN_AK, PN_BQ, PN_BK, PN_DQ, PN_DK, PN_DV, PN_W = 0, 512, 1024, 2560, 4096, 4608, 4864, 5120
PP_AV, PP_AZ, PP_BV, PP_BZ, PP_CZ, PP_DZ, PP_G, PP_W = 0, 512, 1024, 2560, 3072, 3584, 4096, 8192
PF_W = 1024

A_BIAS_TILES = MAX_DISTANCE // BLOCK + 2


def _cparams(sem, vmem_mb=48):
    return pltpu.CompilerParams(dimension_semantics=sem, vmem_limit_bytes=vmem_mb * 1024 * 1024)


def _t5_bucket(dist):
    max_exact = NUM_BUCKETS // 2
    d = jnp.maximum(dist, 0)
    logd = jnp.log(jnp.maximum(d, 1).astype(F32) / max_exact)
    large = max_exact + (logd / math.log(MAX_DISTANCE / max_exact) * (NUM_BUCKETS - max_exact)).astype(jnp.int32)
    return jnp.where(d < max_exact, d, jnp.minimum(large, NUM_BUCKETS - 1))


def _proj_kernel(x_ref, g_ref, w_ref, *rest, norm):
    if norm:
        hg_ref, flag_ref, bd_ref, o_ref, xn_ref = rest
    else:
        o_ref, xn_ref = rest

    @pl.when(pl.program_id(1) == 0)
    def _():
        x = x_ref[...]
        ms = jnp.mean(x * x, axis=-1, keepdims=True)
        xn_ref[...] = (x * lax.rsqrt(ms + EPS) * g_ref[...]).astype(BF16)

    h = jnp.dot(xn_ref[...], w_ref[...], preferred_element_type=F32)
    if norm:
        ss = jnp.dot((h * h).astype(BF16), bd_ref[...], preferred_element_type=F32)
        scale = lax.rsqrt(ss * (1.0 / HEAD_DIM) + EPS) * hg_ref[...]
        h = h * jnp.where(flag_ref[...] > 0, scale, 1.0)
    o_ref[...] = h.astype(o_ref.dtype)


def _proj(x2, gain, w, out_dtype, head_gain=None, flag=None, tm=1024, tn=512):
    t, d = x2.shape
    n = w.shape[1]
    norm = head_gain is not None
    in_specs = [pl.BlockSpec((tm, d), lambda i, j: (i, 0)),
                pl.BlockSpec((1, d), lambda i, j: (0, 0)),
                pl.BlockSpec((d, tn), lambda i, j: (0, j))]
    args = [x2, gain.reshape(1, d), w]
    if norm:
        lane = np.arange(tn)
        bd = jnp.asarray((lane[:, None] // HEAD_DIM == lane[None, :] // HEAD_DIM), BF16)
        in_specs += [pl.BlockSpec((1, tn), lambda i, j: (0, j)),
                     pl.BlockSpec((1, tn), lambda i, j: (0, j)),
                     pl.BlockSpec((tn, tn), lambda i, j: (0, 0))]
        args += [head_gain.reshape(1, n), flag.reshape(1, n), bd]
    return pl.pallas_call(
        functools.partial(_proj_kernel, norm=norm),
        grid=(t // tm, n // tn),
        in_specs=in_specs,
        out_specs=pl.BlockSpec((tm, tn), lambda i, j: (i, j)),
        out_shape=jax.ShapeDtypeStruct((t, n), out_dtype),
        scratch_shapes=[pltpu.VMEM((tm, d), BF16)],
        compiler_params=_cparams(("parallel", "arbitrary")),
        name="proj_norm" if norm else "proj",
    )(*args)


C_PAIR = 256
C_QK_W = (C_HEADS // 2) * C_PAIR
C_ROPE_HALF = C_ROPE // 2


def _c_lane(h, c):
    base = (h // 2) * C_PAIR
    e = h % 2
    if c < C_NOPE:
        return base + e * C_NOPE + c
    return base + 2 * C_NOPE + e * C_ROPE + (c - C_NOPE)


def _c_layout_tables():
    src_q = np.full(C_QK_W, -1, np.int64)
    src_k = np.full(C_QK_W, -1, np.int64)
    gain_idx = np.full(C_QK_W, -1, np.int64)
    head = np.full(C_QK_W, -1, np.int64)
    pe_src = np.full(C_QK_W, -1, np.int64)
    rope_j = np.full(C_QK_W, -1, np.int64)
    rope_half = np.zeros(C_QK_W, np.int64)
    for h in range(C_HEADS):
        for c in range(C_NOPE + C_ROPE):
            ln = _c_lane(h, c)
            src_q[ln] = h * (C_NOPE + C_ROPE) + c
            gain_idx[ln] = c
            head[ln] = h
            if c < C_NOPE:
                src_k[ln] = h * (C_NOPE + C_V) + c
            else:
                r = c - C_NOPE
                pe_src[ln] = r
                rope_j[ln] = r % C_ROPE_HALF
                rope_half[ln] = r // C_ROPE_HALF
    return src_q, src_k, gain_idx, head, pe_src, rope_j, rope_half


def _c_prep_kernel(pf_ref, cqg_ref, ckvg_ref, wq_ref, wk_ref, wv_ref, ppe_ref, grp_ref,
                   gq_ref, gk_ref, cos_ref, s1_ref, s2_ref, q_ref, k_ref, v_ref):
    blk = pf_ref[...]
    cq = blk[:, :C_Q_LORA]
    ckv = blk[:, C_Q_LORA:C_Q_LORA + C_KV_LORA]
    pe = blk[:, C_Q_LORA + C_KV_LORA:]

    def rms(v, g):
        return v * lax.rsqrt(jnp.mean(v * v, axis=-1, keepdims=True) + EPS) * g

    def head_norm_rope(raw, gain):
        ss = jnp.dot((raw * raw).astype(BF16), grp_ref[...], preferred_element_type=F32)
        y = raw * lax.rsqrt(ss * (1.0 / (C_NOPE + C_ROPE)) + EPS) * gain
        up = pltpu.roll(y, C_QK_W - C_ROPE_HALF, 1)
        dn = pltpu.roll(y, C_ROPE_HALF, 1)
        return y * cos_ref[...] + up * s1_ref[...] + dn * s2_ref[...]

    cqn = rms(cq, cqg_ref[...]).astype(BF16)
    q_raw = jnp.dot(cqn, wq_ref[...], preferred_element_type=F32)
    q_ref[...] = head_norm_rope(q_raw, gq_ref[...]).astype(BF16)

    ckvn = rms(ckv, ckvg_ref[...]).astype(BF16)
    pe_hi = pe.astype(BF16)
    pe_lo = (pe - pe_hi.astype(F32)).astype(BF16)
    k_raw = (jnp.dot(ckvn, wk_ref[...], preferred_element_type=F32)
             + jnp.dot(pe_hi, ppe_ref[...], preferred_element_type=F32)
             + jnp.dot(pe_lo, ppe_ref[...], preferred_element_type=F32))
    k_ref[...] = head_norm_rope(k_raw, gk_ref[...]).astype(BF16)
    v_ref[...] = jnp.dot(ckvn, wv_ref[...], preferred_element_type=F32).astype(BF16)


def _c_prep(pf, cqg, ckvg, wq, wk, wv, ppe, grp, gq, gk, cos, s1, s2, seq, tm=512):
    t = pf.shape[0]
    nsb = seq // tm
    full = lambda shape: pl.BlockSpec(shape, lambda i: (0,) * len(shape))
    tab = pl.BlockSpec((tm, C_QK_W), lambda i: (i % nsb, 0))
    return pl.pallas_call(
        _c_prep_kernel,
        grid=(t // tm,),
        in_specs=[pl.BlockSpec((tm, 512), lambda i: (i, 1)),
                  full((1, C_Q_LORA)), full((1, C_KV_LORA)),
                  full((C_Q_LORA, C_QK_W)), full((C_KV_LORA, C_QK_W)), full((C_KV_LORA, BRANCH_WIDTH)),
                  full((LANES, C_QK_W)), full((C_QK_W, C_QK_W)),
                  full((1, C_QK_W)), full((1, C_QK_W)), tab, tab, tab],
        out_specs=[pl.BlockSpec((tm, C_QK_W), lambda i: (i, 0)),
                   pl.BlockSpec((tm, C_QK_W), lambda i: (i, 0)),
                   pl.BlockSpec((tm, BRANCH_WIDTH), lambda i: (i, 0))],
        out_shape=[jax.ShapeDtypeStruct((t, C_QK_W), BF16),
                   jax.ShapeDtypeStruct((t, C_QK_W), BF16),
                   jax.ShapeDtypeStruct((t, BRANCH_WIDTH), BF16)],
        compiler_params=_cparams(("parallel",)),
        name="c_prep",
    )(pf, cqg, ckvg, wq, wk, wv, ppe, grp, gq, gk, cos, s1, s2)


def _flash_update(e, s, v_chunk, m_ref, l_ref, acc_ref):
    m_prev = m_ref[e]
    m_new = jnp.maximum(m_prev, jnp.max(s, axis=1, keepdims=True))
    alpha = jnp.exp(m_prev - m_new)
    p = jnp.exp(s - m_new)
    l_ref[e] = alpha * l_ref[e] + jnp.sum(p, axis=1, keepdims=True)
    acc_ref[e] = alpha * acc_ref[e] + jnp.dot(p.astype(BF16), v_chunk, preferred_element_type=F32)
    m_ref[e] = m_new


def _flash_init(m_ref, l_ref, acc_ref):
    m_ref[...] = jnp.full(m_ref.shape, NEG, F32)
    l_ref[...] = jnp.zeros(l_ref.shape, F32)
    acc_ref[...] = jnp.zeros(acc_ref.shape, F32)


def _flash_finish(o_ref, l_ref, acc_ref):
    lane = lax.broadcasted_iota(jnp.int32, acc_ref.shape[1:], 1)
    o0 = acc_ref[0] / l_ref[0]
    o1 = acc_ref[1] / l_ref[1]
    o_ref[...] = jnp.where(lane < HEAD_DIM, o0, o1).astype(o_ref.dtype)


_CONTRACT_LANES = (((1,), (1,)), ((), ()))


def _head_masks(width, ranges0, ranges1):
    m = np.zeros((2, 1, width), np.float32)
    for e, ranges in enumerate((ranges0, ranges1)):
        for lo, hi in ranges:
            m[e, 0, lo:hi] = 1
    return jnp.asarray(m, BF16)


def _pair_masks():
    return _head_masks(LANES, [(0, HEAD_DIM)], [(HEAD_DIM, LANES)])


def _c_attn_kernel(q_ref, hm_ref, k_ref, v_ref, o_ref, m_ref, l_ref, acc_ref, *, tq, tk):
    qi = pl.program_id(2)
    q = q_ref[...]
    qs = (q * hm_ref[0], q * hm_ref[1])
    _flash_init(m_ref, l_ref, acc_ref)
    n_full = (qi * tq) // tk

    def chunk(c, masked):
        off = pl.multiple_of(c * tk, tk)
        kc = k_ref[pl.ds(off, tk), :]
        vc = v_ref[pl.ds(off, tk), :]
        for e in range(2):
            s = lax.dot_general(qs[e], kc, _CONTRACT_LANES, preferred_element_type=F32)
            if masked:
                row = qi * tq + lax.broadcasted_iota(jnp.int32, s.shape, 0)
                col = c * tk + lax.broadcasted_iota(jnp.int32, s.shape, 1)
                s = jnp.where(col <= row, s, NEG)
            _flash_update(e, s, vc, m_ref, l_ref, acc_ref)

    def body(c, carry):
        chunk(c, False)
        return carry

    lax.fori_loop(0, n_full, body, 0)
    chunk(n_full, True)
    _flash_finish(o_ref, l_ref, acc_ref)


def _c_attn(qc, kc, vc, batch, seq, tq=256, tk=512):
    t = qc.shape[0]
    nq = seq // tq
    npair = C_HEADS // 2
    return pl.pallas_call(
        functools.partial(_c_attn_kernel, tq=tq, tk=tk),
        grid=(batch, npair, nq),
        in_specs=[pl.BlockSpec((tq, C_PAIR), lambda b, p, i: (b * nq + i, p)),
                  pl.BlockSpec((2, 1, C_PAIR), lambda b, p, i: (0, 0, 0)),
                  pl.BlockSpec((seq, C_PAIR), lambda b, p, i: (b, p)),
                  pl.BlockSpec((seq, LANES), lambda b, p, i: (b, p))],
        out_specs=pl.BlockSpec((tq, LANES), lambda b, p, i: (b * nq + i, p)),
        out_shape=jax.ShapeDtypeStruct((t, BRANCH_WIDTH), BF16),
        scratch_shapes=[pltpu.VMEM((2, tq, 1), F32), pltpu.VMEM((2, tq, 1), F32),
                        pltpu.VMEM((2, tq, LANES), F32)],
        compiler_params=_cparams(("parallel", "parallel", "arbitrary")),
        name="c_attn",
    )(qc, _head_masks(C_PAIR, [(0, C_NOPE), (2 * C_NOPE, 2 * C_NOPE + C_ROPE)],
                      [(C_NOPE, 2 * C_NOPE), (2 * C_NOPE + C_ROPE, 2 * C_NOPE + 2 * C_ROPE)]), kc, vc)


def _a_attn_kernel(q_ref, hm_ref, k_ref, v_ref, sel_ref, bias_ref, o_ref, m_ref, l_ref, acc_ref, *, tq, tk):
    qi = pl.program_id(1)
    q = q_ref[...]
    qs = (q * hm_ref[0], q * hm_ref[1])
    _flash_init(m_ref, l_ref, acc_ref)
    n_chunks = ((qi + 1) * tq + tk - 1) // tk

    def body(c, carry):
        off = pl.multiple_of(c * tk, tk)
        kc = k_ref[pl.ds(off, tk), :]
        vc = v_ref[pl.ds(off, tk), :]
        selmask = sel_ref[:, pl.ds(off, tk)].astype(F32)
        for e in range(2):
            s = lax.dot_general(qs[e], kc, _CONTRACT_LANES, preferred_element_type=F32)
            rows = []
            for i in range(tq // BLOCK):
                tiles = []
                for j in range(tk // BLOCK):
                    d = (qi * (tq // BLOCK) + i) - (c * (tk // BLOCK) + j)
                    d = jnp.clip(d, 0, A_BIAS_TILES - 1)
                    tiles.append(bias_ref[e, d])
                rows.append(jnp.concatenate(tiles, axis=1))
            s = s + jnp.concatenate(rows, axis=0) + selmask
            _flash_update(e, s, vc, m_ref, l_ref, acc_ref)
        return carry

    lax.fori_loop(0, n_chunks, body, 0)
    _flash_finish(o_ref, l_ref, acc_ref)


def _a_attn(pn, pp, sel, bias_tiles, batch, seq, tq=256, tk=512):
    t = pn.shape[0]
    nq = seq // tq
    npair = A_HEADS // 2
    qcol, kcol, vcol = PN_AQ // LANES, PN_AK // LANES, PP_AV // LANES
    return pl.pallas_call(
        functools.partial(_a_attn_kernel, tq=tq, tk=tk),
        grid=(batch, nq, npair),
        in_specs=[pl.BlockSpec((tq, LANES), lambda b, i, p: (b * nq + i, qcol + p)),
                  pl.BlockSpec((2, 1, LANES), lambda b, i, p: (0, 0, 0)),
                  pl.BlockSpec((seq, LANES), lambda b, i, p: (b, kcol + p)),
                  pl.BlockSpec((seq, LANES), lambda b, i, p: (b, vcol + p)),
                  pl.BlockSpec((tq, seq), lambda b, i, p: (b * nq + i, 0)),
                  pl.BlockSpec((2, A_BIAS_TILES, BLOCK, BLOCK), lambda b, i, p: (p, 0, 0, 0))],
        out_specs=pl.BlockSpec((tq, LANES), lambda b, i, p: (b * nq + i, p)),
        out_shape=jax.ShapeDtypeStruct((t, BRANCH_WIDTH), BF16),
        scratch_shapes=[pltpu.VMEM((2, tq, 1), F32), pltpu.VMEM((2, tq, 1), F32),
                        pltpu.VMEM((2, tq, LANES), F32)],
        compiler_params=_cparams(("parallel", "parallel", "arbitrary")),
        name="a_attn",
    )(pn, _pair_masks(), pn, pp, sel, bias_tiles)


def _sortable_key(score):
    bits = pltpu.bitcast(score, jnp.int32)
    return bits ^ ((bits >> 31) & jnp.int32(0x7FFFFFFF))


def _a_select_kernel(qblk_ref, kblk_ref, ph_ref, pl_ref, pkh_ref, pkl_ref, sel_ref,
                     ikx_ref, keys_ref, *, seq, k_sel):
    qi = pl.program_id(1)
    nck = qi + 1

    @pl.when(qi == 0)
    def _():
        def prep(c, carry):
            off = pl.multiple_of(c * 512, 512)
            kv = kblk_ref[pl.ds(off, 512), :]
            hi = kv.astype(BF16)
            lo = (kv - hi.astype(F32)).astype(BF16)
            ikx_ref[pl.ds(off, 512), :] = (
                jnp.dot(hi, pkh_ref[...], preferred_element_type=F32)
                + jnp.dot(lo, pkl_ref[...], preferred_element_type=F32)).astype(BF16)
            return carry
        lax.fori_loop(0, seq // 512, prep, 0)

    qb = qblk_ref[...]
    iq = qb[:, :IDX_HEADS * IDX_DIM]
    iq_hi = iq.astype(BF16)
    iq_lo = (iq - iq_hi.astype(F32)).astype(BF16)
    iqx = (jnp.dot(iq_hi, ph_ref[...], preferred_element_type=F32)
           + jnp.dot(iq_lo, pl_ref[...], preferred_element_type=F32)).astype(BF16)
    iw_t = qb[:, IDX_HEADS * IDX_DIM:IDX_HEADS * IDX_DIM + LANES].T
    q_lane = lax.broadcasted_iota(jnp.int32, (BLOCK, BLOCK), 1)
    k_row = lax.broadcasted_iota(jnp.int32, (BLOCK, BLOCK), 0)

    def score_chunk(c, carry):
        off = pl.multiple_of(c * BLOCK, BLOCK)
        kx = ikx_ref[pl.ds(off, BLOCK), :]
        sc = jnp.zeros((BLOCK, BLOCK), F32)
        for h in range(IDX_HEADS):
            xh = lax.dot_general(kx, iqx[:, h * LANES:(h + 1) * LANES], _CONTRACT_LANES,
                                 preferred_element_type=F32)
            sc = sc + jnp.maximum(xh, 0.0) * iw_t[IDX_DIM + h:IDX_DIM + h + 1, :]
        sc = sc + 0.0
        causal = (c * BLOCK + k_row) <= (qi * BLOCK + q_lane)
        keys_ref[pl.ds(off, BLOCK), :] = jnp.where(causal, _sortable_key(sc), INT_MIN)
        return carry

    lax.fori_loop(0, nck, score_chunk, 0)

    def count_ge(cand):
        def cnt_chunk(c, acc):
            off = pl.multiple_of(c * BLOCK, BLOCK)
            ge = jnp.where(keys_ref[pl.ds(off, BLOCK), :] >= cand, 1, 0)
            return acc + jnp.sum(ge.reshape(BLOCK // 8, 8, BLOCK), axis=0)
        acc = lax.fori_loop(0, nck, cnt_chunk, jnp.zeros((8, BLOCK), jnp.int32))
        return jnp.sum(acc, axis=0, keepdims=True)

    def bisect(it, lo):
        step = lax.shift_left(jnp.int32(1), jnp.int32(31) - it)
        cand = lo + step
        return jnp.where(count_ge(cand) >= k_sel, cand, lo)

    thr = lax.fori_loop(0, 32, bisect, jnp.full((1, BLOCK), INT_MIN, jnp.int32))

    def emit(c, carry):
        off = pl.multiple_of(c * BLOCK, BLOCK)
        keys = keys_ref[pl.ds(off, BLOCK), :]
        causal = (c * BLOCK + k_row) <= (qi * BLOCK + q_lane)
        add = jnp.where((keys >= thr) & causal, 0.0, NEG)
        sel_ref[:, pl.ds(off, BLOCK)] = add.T.astype(BF16)
        return carry

    lax.fori_loop(0, nck, emit, 0)

    def fill(c, carry):
        off = pl.multiple_of(c * BLOCK, BLOCK)
        sel_ref[:, pl.ds(off, BLOCK)] = jnp.full((BLOCK, BLOCK), NEG, BF16)
        return carry

    lax.fori_loop(nck, seq // BLOCK, fill, 0)


def _a_select(pf, batch, seq):
    t = pf.shape[0]
    nq = seq // BLOCK
    k_sel = min(TOPK_MAX, seq // 4)
    nlane = IDX_HEADS * IDX_DIM
    r = np.arange(nlane)
    ph = np.zeros((nlane, IDX_HEADS * LANES), np.float32)
    plo = np.zeros((nlane, IDX_HEADS * LANES), np.float32)
    ph[r, (r // IDX_DIM) * LANES + r % IDX_DIM] = 1
    ph[r, (r // IDX_DIM) * LANES + 2 * IDX_DIM + r % IDX_DIM] = 1
    plo[r, (r // IDX_DIM) * LANES + IDX_DIM + r % IDX_DIM] = 1
    d = np.arange(IDX_DIM)
    pkh = np.zeros((LANES, LANES), np.float32)
    pkl = np.zeros((LANES, LANES), np.float32)
    pkh[d, d] = 1
    pkh[d, IDX_DIM + d] = 1
    pkl[d, 2 * IDX_DIM + d] = 1
    full = lambda shape: pl.BlockSpec(shape, lambda b, i: (0,) * len(shape))
    return pl.pallas_call(
        functools.partial(_a_select_kernel, seq=seq, k_sel=k_sel),
        grid=(batch, nq),
        in_specs=[pl.BlockSpec((BLOCK, 512), lambda b, i: (b * nq + i, 0)),
                  pl.BlockSpec((seq, LANES), lambda b, i: (b, nlane // LANES)),
                  full(ph.shape), full(plo.shape), full(pkh.shape), full(pkl.shape)],
        out_specs=pl.BlockSpec((BLOCK, seq), lambda b, i: (b * nq + i, 0)),
        out_shape=jax.ShapeDtypeStruct((t, seq), BF16),
        scratch_shapes=[pltpu.VMEM((seq, LANES), BF16), pltpu.VMEM((seq, LANES), jnp.int32)],
        compiler_params=_cparams(("parallel", "arbitrary")),
        name="a_select",
    )(pf, pf, jnp.asarray(ph, BF16), jnp.asarray(plo, BF16), jnp.asarray(pkh, BF16),
      jnp.asarray(pkl, BF16))


def _band_kernel(*refs, has_sink, want_lse):
    if has_sink:
        sink_ref, refs = refs[0], refs[1:]
    q_ref, hm_ref, kp_ref, kc_ref, vp_ref, vc_ref, bias_ref = refs[:7]
    o_ref = refs[7]
    n = pl.program_id(2)
    p_idx = pl.program_id(3)
    q = q_ref[...]
    kk = jnp.concatenate([kp_ref[...], kc_ref[...]], axis=0)
    vv = jnp.concatenate([vp_ref[...], vc_ref[...]], axis=0)
    lane = lax.broadcasted_iota(jnp.int32, (BLOCK, LANES), 1)
    col = lax.broadcasted_iota(jnp.int32, (BLOCK, 2 * BLOCK), 1)
    no_prev = (n == 0) & (col < BLOCK)
    outs, lses = [], []
    for e in range(2):
        s = lax.dot_general(q * hm_ref[e], kk, _CONTRACT_LANES, preferred_element_type=F32) + bias_ref[e]
        s = jnp.where(no_prev, NEG, s)
        m = jnp.max(s, axis=1, keepdims=True)
        if has_sink:
            sk = sink_ref[2 * p_idx + e]
            m = jnp.maximum(m, sk)
        p = jnp.exp(s - m)
        den = jnp.sum(p, axis=1, keepdims=True)
        if has_sink:
            den = den + jnp.exp(sk - m)
        outs.append(jnp.dot(p.astype(BF16), vv, preferred_element_type=F32) / den)
        lses.append(m + jnp.log(den))
    o_ref[...] = jnp.where(lane < HEAD_DIM, outs[0], outs[1]).astype(o_ref.dtype)
    if want_lse:
        refs[8][...] = jnp.where(lane < HEAD_DIM, lses[0], lses[1])


def _band_attn(q_arr, k_arr, v_arr, bias, batch, seq, dil, row_w, qcol, kcol, vcol, kv_div,
               npair, sinks=None, want_lse=False):
    t = q_arr.shape[0]
    nb = seq // dil // BLOCK
    qv = q_arr.reshape(t // dil, dil * row_w[0])
    kv = k_arr.reshape(t // dil, dil * row_w[1])
    vv = v_arr.reshape(t // dil, dil * row_w[2])
    qw, kw, vw = (w // LANES for w in row_w)
    qc, kc, vc = qcol // LANES, kcol // LANES, vcol // LANES
    ow = BRANCH_WIDTH // LANES

    def rows(b, n):
        return b * nb + n

    def prev(b, n):
        return b * nb + jnp.maximum(n - 1, 0)

    in_specs = [
        pl.BlockSpec((BLOCK, LANES), lambda b, r, n, p: (rows(b, n), r * qw + qc + p)),
        pl.BlockSpec((2, 1, LANES), lambda b, r, n, p: (0, 0, 0)),
        pl.BlockSpec((BLOCK, LANES), lambda b, r, n, p: (prev(b, n), r * kw + kc + p // kv_div)),
        pl.BlockSpec((BLOCK, LANES), lambda b, r, n, p: (rows(b, n), r * kw + kc + p // kv_div)),
        pl.BlockSpec((BLOCK, LANES), lambda b, r, n, p: (prev(b, n), r * vw + vc + p // kv_div)),
        pl.BlockSpec((BLOCK, LANES), lambda b, r, n, p: (rows(b, n), r * vw + vc + p // kv_div)),
        pl.BlockSpec((2, BLOCK, 2 * BLOCK), lambda b, r, n, p: (p, 0, 0)),
    ]
    args = [qv, _pair_masks(), kv, kv, vv, vv, bias]
    if sinks is not None:
        in_specs = [pl.BlockSpec(memory_space=pltpu.SMEM)] + in_specs
        args = [sinks] + args
    o_spec = pl.BlockSpec((BLOCK, LANES), lambda b, r, n, p: (rows(b, n), r * ow + p))
    out_specs = [o_spec]
    out_shape = [jax.ShapeDtypeStruct((t // dil, dil * BRANCH_WIDTH), BF16)]
    if want_lse:
        out_specs.append(o_spec)
        out_shape.append(jax.ShapeDtypeStruct((t // dil, dil * BRANCH_WIDTH), F32))
    res = pl.pallas_call(
        functools.partial(_band_kernel, has_sink=sinks is not None, want_lse=want_lse),
        grid=(batch, dil, nb, npair),
        in_specs=in_specs,
        out_specs=out_specs,
        out_shape=out_shape,
        compiler_params=_cparams(("parallel", "parallel", "parallel", "arbitrary")),
        name="band_attn",
    )(*args)
    return [a.reshape(t, BRANCH_WIDTH) for a in res]


def _band_bias(table, step, max_dist):
    rel = BLOCK + np.arange(BLOCK)[:, None] - np.arange(2 * BLOCK)[None, :]
    bias = table[_t5_bucket(jnp.asarray(rel * step))].transpose(2, 0, 1).astype(F32)
    ok = (rel >= 0) & (rel <= max_dist)
    return jnp.where(jnp.asarray(ok)[None], bias, NEG)


def _merge_kernel(x_ref, ya_ref, yb0_ref, yb1_ref, yb2_ref, l0_ref, l1_ref, l2_ref, yc_ref, yd_ref,
                  za_ref, zb_ref, zc_ref, zd_ref, ga_ref, gb_ref, gc_ref, gd_ref, wb_ref, wo_ref, o_ref):
    l0, l1, l2 = l0_ref[...], l1_ref[...], l2_ref[...]
    mx = jnp.maximum(jnp.maximum(l0, l1), l2)
    w0, w1, w2 = jnp.exp(l0 - mx), jnp.exp(l1 - mx), jnp.exp(l2 - mx)
    yb = (w0 * yb0_ref[...].astype(F32) + w1 * yb1_ref[...].astype(F32)
          + w2 * yb2_ref[...].astype(F32)) / (w0 + w1 + w2)
    ys = (ya_ref[...].astype(F32), yb, yc_ref[...].astype(F32), yd_ref[...].astype(F32))
    zs = (za_ref, zb_ref, zc_ref, zd_ref)
    gs = (ga_ref, gb_ref, gc_ref, gd_ref)
    merged = jnp.zeros(o_ref.shape, F32)
    for n in range(N_BRANCH):
        z = zs[n][...].astype(F32)
        u = (ys[n] * (z * jax.nn.sigmoid(z))).astype(BF16)
        proj = jnp.dot(u, wb_ref[n], preferred_element_type=F32)
        merged = merged + jax.nn.sigmoid(gs[n][...].astype(F32)) * proj
    o_ref[...] = x_ref[...] + jnp.dot(merged.astype(BF16), wo_ref[...], preferred_element_type=F32)


def _merge(x2, ya, yb, lb, yc, yd, pp, wb, wo, tm=512):
    t = x2.shape[0]
    bw = BRANCH_WIDTH
    row = lambda w, c: pl.BlockSpec((tm, w), lambda i: (i, c))
    in_specs = ([row(D_MODEL, 0), row(bw, 0)] + [row(bw, 0)] * 3 + [row(bw, 0)] * 3 + [row(bw, 0), row(bw, 0)]
                + [row(bw, PP_AZ // bw), row(bw, PP_BZ // bw), row(bw, PP_CZ // bw), row(bw, PP_DZ // bw)]
                + [row(D_MODEL, PP_G // D_MODEL + n) for n in range(N_BRANCH)]
                + [pl.BlockSpec((N_BRANCH, bw, D_MODEL), lambda i: (0, 0, 0)),
                   pl.BlockSpec((D_MODEL, D_MODEL), lambda i: (0, 0))])
    return pl.pallas_call(
        _merge_kernel,
        grid=(t // tm,),
        in_specs=in_specs,
        out_specs=row(D_MODEL, 0),
        out_shape=jax.ShapeDtypeStruct((t, D_MODEL), F32),
        compiler_params=_cparams(("parallel",)),
        name="merge",
    )(x2, ya, yb[0], yb[1], yb[2], lb[0], lb[1], lb[2], yc, yd, pp, pp, pp, pp, pp, pp, pp, pp, wb, wo)


def _layer_weights(w_in, qk_gain_a, qk_gain_b, qk_gain_d):
    def cols(start, width):
        return w_in[:, :, start:start + width]

    n_bq = len(B_GROUPS) * B_HEADS * HEAD_DIM
    dk = cols(_D0 + 512, 128)
    dv = cols(_D0 + 640, 128)
    dup = lambda a: jnp.concatenate([a[:, :, :64], a[:, :, :64], a[:, :, 64:], a[:, :, 64:]], axis=-1)
    w_pn = jnp.concatenate([cols(_A0, 512), cols(_A0 + 512, 512), cols(_B0, n_bq), cols(_B0 + n_bq, n_bq),
                            cols(_D0, 512), dup(dk), dup(dv)], axis=-1).astype(BF16)
    w_pp = jnp.concatenate([cols(_A0 + 1024, 512), cols(_A0 + 1536, 512), cols(_B0 + 2 * n_bq, n_bq),
                            cols(_B0 + 3 * n_bq, 512), cols(_C0 + 416, 512), cols(_D0 + 768, 512),
                            cols(_G0, N_BRANCH * D_MODEL)], axis=-1).astype(BF16)
    depth = w_in.shape[0]
    zeros = lambda w: jnp.zeros((depth, D_MODEL, w), w_in.dtype)
    w_pf = jnp.concatenate([cols(_A0 + 2048, 296), zeros(216), cols(_C0, 416), zeros(96)], axis=-1).astype(BF16)

    scale = HEAD_DIM ** -0.5
    tile = lambda g, reps: jnp.tile(g, (1, reps))
    ones = jnp.ones((depth, 256), F32)
    hg = jnp.concatenate([tile(qk_gain_a[:, 0] * scale, 8), tile(qk_gain_a[:, 1], 8),
                          tile(qk_gain_b[:, 0] * scale, 24), tile(qk_gain_b[:, 1], 24),
                          tile(qk_gain_d[:, 0] * scale, 8), tile(qk_gain_d[:, 1], 4), ones], axis=-1)
    flag = jnp.concatenate([jnp.ones((PN_DV,), F32), jnp.zeros((PN_W - PN_DV,), F32)])
    return w_pn, w_pp, w_pf, hg, flag


def _c_weights(qk_gain_c, w_q_b, w_kv_b, seq):
    src_q, src_k, gain_idx, head, pe_src, rope_j, rope_half = _c_layout_tables()
    take = lambda w, src: jnp.where(jnp.asarray(src >= 0), jnp.take(w, jnp.asarray(np.maximum(src, 0)), axis=-1), 0.0)
    wq = take(w_q_b, src_q).astype(BF16)
    wk = take(w_kv_b, src_k).astype(BF16)
    v_src = np.array([h * (C_NOPE + C_V) + C_NOPE + c for h in range(C_HEADS) for c in range(C_V)])
    wv = jnp.take(w_kv_b, jnp.asarray(v_src), axis=-1).astype(BF16)
    ppe = np.zeros((LANES, C_QK_W), np.float32)
    ln = np.nonzero(pe_src >= 0)[0]
    ppe[pe_src[ln], ln] = 1
    grp = ((head[:, None] == head[None, :]) & (head[:, None] >= 0)).astype(np.float32)
    g_take = lambda g: jnp.where(jnp.asarray(gain_idx >= 0), jnp.take(g, jnp.asarray(np.maximum(gain_idx, 0)), axis=-1), 0.0)
    gq = g_take(qk_gain_c[:, 0]) * (C_NOPE + C_ROPE) ** -0.5
    gk = g_take(qk_gain_c[:, 1])
    freq = ROPE_THETA ** (-jnp.arange(C_ROPE_HALF, dtype=F32) / C_ROPE_HALF)
    ang = jnp.arange(seq).astype(F32)[:, None] * freq[None, :]
    cos_j, sin_j = jnp.cos(ang), jnp.sin(ang)
    is_rope = jnp.asarray(rope_j >= 0)
    jj = jnp.asarray(np.maximum(rope_j, 0))
    cos = jnp.where(is_rope, cos_j[:, jj], 1.0)
    sin = jnp.where(is_rope, sin_j[:, jj], 0.0)
    s1 = jnp.where(jnp.asarray(rope_half == 0), -sin, 0.0)
    s2 = jnp.where(jnp.asarray(rope_half == 1), sin, 0.0)
    return wq, wk, wv, jnp.asarray(ppe, BF16), jnp.asarray(grp, BF16), gq, gk, cos, s1, s2


def _a_bias_tiles(table):
    a = np.arange(BLOCK)
    dist = (np.arange(A_BIAS_TILES)[:, None, None] * BLOCK + a[None, :, None] - a[None, None, :])
    return table[_t5_bucket(jnp.asarray(dist))].transpose(3, 0, 1, 2).astype(F32)


def kernel(x, norm_gain, w_in, qk_gain_a, qk_gain_b, qk_gain_c, qk_gain_d, c_q_gain, c_kv_gain,
           w_q_b, w_kv_b, sinks, rel_bias, w_branch, w_out):
    batch, seq, d_model = x.shape
    depth = w_in.shape[0]
    t = batch * seq
    x2 = x.reshape(t, d_model)

    w_pn, w_pp, w_pf, hg, flag = _layer_weights(w_in, qk_gain_a, qk_gain_b, qk_gain_d)
    wq, wk, wv, ppe, grp, gq, gk, cos, s1, s2 = _c_weights(qk_gain_c, w_q_b, w_kv_b, seq)
    wb = w_branch.astype(BF16)
    wo = w_out.astype(BF16)

    bias_a = _a_bias_tiles(rel_bias[:, :A_HEADS])
    bias_b = [_band_bias(rel_bias[:, A_HEADS + g * B_HEADS:A_HEADS + (g + 1) * B_HEADS], dil, window // dil)
              for g, (window, dil) in enumerate(B_GROUPS)]
    bias_d = _band_bias(rel_bias[:, N_BIAS_HEADS - D_HEADS:], 1, D_WINDOW - 1)

    for l in range(depth):
        pn = _proj(x2, norm_gain[l], w_pn[l], BF16, head_gain=hg[l], flag=flag)
        pp = _proj(x2, norm_gain[l], w_pp[l], BF16)
        pf = _proj(x2, norm_gain[l], w_pf[l], F32)

        sel = _a_select(pf, batch, seq)
        ya = _a_attn(pn, pp, sel, bias_a, batch, seq)

        yb, lb = [], []
        for g, (window, dil) in enumerate(B_GROUPS):
            o, lse = _band_attn(pn, pn, pp, bias_b[g], batch, seq, dil, (PN_W, PN_W, PP_W),
                                PN_BQ + g * 512, PN_BK + g * 512, PP_BV + g * 512, 1,
                                B_HEADS // 2, want_lse=True)
            yb.append(o)
            lb.append(lse)

        qc, kc, vc = _c_prep(pf, c_q_gain[l].reshape(1, -1), c_kv_gain[l].reshape(1, -1), wq[l], wk[l], wv[l],
                             ppe, grp, gq[l].reshape(1, -1), gk[l].reshape(1, -1), cos, s1, s2, seq)
        yc = _c_attn(qc, kc, vc, batch, seq)

        (yd,) = _band_attn(pn, pn, pn, bias_d, batch, seq, 1, (PN_W, PN_W, PN_W),
                           PN_DQ, PN_DK, PN_DV, 2, D_HEADS // 2, sinks=sinks[l])

        x2 = _merge(x2, ya, yb, lb, yc, yd, pp, wb[l], wo[l])
    return x2.reshape(batch, seq, d_model)
```

```python
import functools
import math

import numpy as np
import jax
import jax.numpy as jnp
from jax import lax
from jax.experimental import pallas as pl
from jax.experimental.pallas import tpu as pltpu

F32 = jnp.float32
BF16 = jnp.bfloat16

D_MODEL = 1024
BLOCK = 128
HEAD_DIM = 64
N_BRANCH = 4
BRANCH_WIDTH = 512
EPS = 1e-6
A_HEADS = 8
IDX_HEADS = 8
IDX_DIM = 32
TOPK_MAX = 256
B_GROUPS = ((128, 1), (512, 4), (2048, 16))
B_HEADS = 8
C_HEADS = 8
C_NOPE = 64
C_ROPE = 32
C_V = 64
C_Q_LORA = 256
C_KV_LORA = 128
ROPE_THETA = 10000.0
D_HEADS = 8
D_KV_HEADS = 2
D_WINDOW = 128
NUM_BUCKETS = 32
MAX_DISTANCE = 2048
N_BIAS_HEADS = A_HEADS + len(B_GROUPS) * B_HEADS + D_HEADS

LANES = 128
NEG = -1e30
INT_MIN = -(2 ** 31)

_A0 = 0
_B0 = 2344
_C0 = 7464
_D0 = 8392
_G0 = 9672

PN_AQ, PN_AK, PN_DQ, PN_DK, PN_DV, PN_W = 0, 512, 1024, 1536, 1792, 2048
PP_G, PP_AV, PP_AZ, PP_BZ, PP_CZ, PP_DZ, PP_W = 0, 4096, 4608, 5120, 5632, 6144, 6656
PF_W = 1024
BG_W = 3 * BRANCH_WIDTH

A_BIAS_TILES = MAX_DISTANCE // BLOCK + 2


def _cparams(sem, vmem_mb=48):
    return pltpu.CompilerParams(dimension_semantics=sem, vmem_limit_bytes=vmem_mb * 1024 * 1024)


def _t5_bucket(dist):
    max_exact = NUM_BUCKETS // 2
    d = jnp.maximum(dist, 0)
    logd = jnp.log(jnp.maximum(d, 1).astype(F32) / max_exact)
    large = max_exact + (logd / math.log(MAX_DISTANCE / max_exact) * (NUM_BUCKETS - max_exact)).astype(jnp.int32)
    return jnp.where(d < max_exact, d, jnp.minimum(large, NUM_BUCKETS - 1))


def _proj_kernel(x_ref, g_ref, w_ref, *rest, norm, dil):
    rest = list(rest)
    strided = dil is not None and dil > 1
    xn_ref = rest.pop(-2) if strided else rest.pop(-1)
    res_ref = rest.pop(-1) if strided else None
    if norm:
        hg_ref, flag_ref, bd_ref, o_ref = rest
    else:
        (o_ref,) = rest

    @pl.when(pl.program_id(1) == 0)
    def _():
        x = x_ref[...]
        ms = jnp.mean(x * x, axis=-1, keepdims=True)
        xn_ref[...] = (x * lax.rsqrt(ms + EPS) * g_ref[...]).astype(BF16)

    h = jnp.dot(xn_ref[...], w_ref[...], preferred_element_type=F32)
    if norm:
        ss = jnp.dot((h * h).astype(BF16), bd_ref[...], preferred_element_type=F32)
        scale = lax.rsqrt(ss * (1.0 / HEAD_DIM) + EPS) * hg_ref[...]
        h = h * jnp.where(flag_ref[...] > 0, scale, 1.0)
    if dil is None:
        o_ref[...] = h.astype(o_ref.dtype)
    elif dil == 1:
        o_ref[0, 0] = h.astype(o_ref.dtype)
    else:
        sub = h.shape[0] // dil
        for c in range(h.shape[1] // LANES):
            res_ref[c] = h[:, c * LANES:(c + 1) * LANES]
        for r in range(dil):
            o_ref[0, r] = jnp.concatenate(
                [res_ref[c, pl.ds(r, sub, stride=dil), :] for c in range(h.shape[1] // LANES)],
                axis=1).astype(o_ref.dtype)


def _proj(x2, gain, w, out_dtype, head_gain=None, flag=None, dil=None, seq=None, tm=1024, tn=512):
    t, d = x2.shape
    n = w.shape[1]
    norm = head_gain is not None
    scratch = [pltpu.VMEM((tm, d), BF16)]
    if dil is None:
        out_spec = pl.BlockSpec((tm, tn), lambda i, j: (i, j))
        out_shape = jax.ShapeDtypeStruct((t, n), out_dtype)
        kdil = None
    else:
        nsb = seq // tm
        out_spec = pl.BlockSpec((1, dil, tm // dil, tn), lambda i, j: (i // nsb, 0, i % nsb, j))
        out_shape = jax.ShapeDtypeStruct((t // seq, dil, seq // dil, n), out_dtype)
        kdil = dil
        if dil > 1:
            scratch.append(pltpu.VMEM((tn // LANES, tm, LANES), F32))
    in_specs = [pl.BlockSpec((tm, d), lambda i, j: (i, 0)),
                pl.BlockSpec((1, d), lambda i, j: (0, 0)),
                pl.BlockSpec((d, tn), lambda i, j: (0, j))]
    args = [x2, gain.reshape(1, d), w]
    if norm:
        lane = np.arange(tn)
        bd = jnp.asarray((lane[:, None] // HEAD_DIM == lane[None, :] // HEAD_DIM), BF16)
        in_specs += [pl.BlockSpec((1, tn), lambda i, j: (0, j)),
                     pl.BlockSpec((1, tn), lambda i, j: (0, j)),
                     pl.BlockSpec((tn, tn), lambda i, j: (0, 0))]
        args += [head_gain.reshape(1, n), flag.reshape(1, n), bd]
    out = pl.pallas_call(
        functools.partial(_proj_kernel, norm=norm, dil=kdil),
        grid=(t // tm, n // tn),
        in_specs=in_specs,
        out_specs=out_spec,
        out_shape=out_shape,
        scratch_shapes=scratch,
        compiler_params=_cparams(("parallel", "arbitrary")),
        name=("proj_norm" if norm else "proj") + ("" if dil is None else "_dil%d" % dil),
    )(*args)
    return out.reshape(t, n)


C_PAIR = 256
C_QK_W = (C_HEADS // 2) * C_PAIR
C_ROPE_HALF = C_ROPE // 2


def _c_lane(h, c):
    base = (h // 2) * C_PAIR
    e = h % 2
    if c < C_NOPE:
        return base + e * C_NOPE + c
    return base + 2 * C_NOPE + e * C_ROPE + (c - C_NOPE)


def _c_layout_tables():
    src_q = np.full(C_QK_W, -1, np.int64)
    src_k = np.full(C_QK_W, -1, np.int64)
    gain_idx = np.full(C_QK_W, -1, np.int64)
    head = np.full(C_QK_W, -1, np.int64)
    pe_src = np.full(C_QK_W, -1, np.int64)
    rope_j = np.full(C_QK_W, -1, np.int64)
    rope_half = np.zeros(C_QK_W, np.int64)
    for h in range(C_HEADS):
        for c in range(C_NOPE + C_ROPE):
            ln = _c_lane(h, c)
            src_q[ln] = h * (C_NOPE + C_ROPE) + c
            gain_idx[ln] = c
            head[ln] = h
            if c < C_NOPE:
                src_k[ln] = h * (C_NOPE + C_V) + c
            else:
                r = c - C_NOPE
                pe_src[ln] = r
                rope_j[ln] = r % C_ROPE_HALF
                rope_half[ln] = r // C_ROPE_HALF
    return src_q, src_k, gain_idx, head, pe_src, rope_j, rope_half


def _c_prep_kernel(pf_ref, cqg_ref, ckvg_ref, wq_ref, wk_ref, wv_ref, ppe_ref, grp_ref,
                   gq_ref, gk_ref, cos_ref, s1_ref, s2_ref, q_ref, k_ref, v_ref):
    blk = pf_ref[...]
    cq = blk[:, :C_Q_LORA]
    ckv = blk[:, C_Q_LORA:C_Q_LORA + C_KV_LORA]
    pe = blk[:, C_Q_LORA + C_KV_LORA:]

    def rms(v, g):
        return v * lax.rsqrt(jnp.mean(v * v, axis=-1, keepdims=True) + EPS) * g

    def head_norm_rope(raw, gain):
        ss = jnp.dot((raw * raw).astype(BF16), grp_ref[...], preferred_element_type=F32)
        y = raw * lax.rsqrt(ss * (1.0 / (C_NOPE + C_ROPE)) + EPS) * gain
        up = pltpu.roll(y, C_QK_W - C_ROPE_HALF, 1)
        dn = pltpu.roll(y, C_ROPE_HALF, 1)
        return y * cos_ref[...] + up * s1_ref[...] + dn * s2_ref[...]

    cqn = rms(cq, cqg_ref[...]).astype(BF16)
    q_raw = jnp.dot(cqn, wq_ref[...], preferred_element_type=F32)
    q_ref[...] = head_norm_rope(q_raw, gq_ref[...]).astype(BF16)

    ckvn = rms(ckv, ckvg_ref[...]).astype(BF16)
    pe_hi = pe.astype(BF16)
    pe_lo = (pe - pe_hi.astype(F32)).astype(BF16)
    k_raw = (jnp.dot(ckvn, wk_ref[...], preferred_element_type=F32)
             + jnp.dot(pe_hi, ppe_ref[...], preferred_element_type=F32)
             + jnp.dot(pe_lo, ppe_ref[...], preferred_element_type=F32))
    k_ref[...] = head_norm_rope(k_raw, gk_ref[...]).astype(BF16)
    v_ref[...] = jnp.dot(ckvn, wv_ref[...], preferred_element_type=F32).astype(BF16)


def _c_prep(pf, cqg, ckvg, wq, wk, wv, ppe, grp, gq, gk, cos, s1, s2, seq, tm=512):
    t = pf.shape[0]
    nsb = seq // tm
    full = lambda shape: pl.BlockSpec(shape, lambda i: (0,) * len(shape))
    tab = pl.BlockSpec((tm, C_QK_W), lambda i: (i % nsb, 0))
    return pl.pallas_call(
        _c_prep_kernel,
        grid=(t // tm,),
        in_specs=[pl.BlockSpec((tm, 512), lambda i: (i, 1)),
                  full((1, C_Q_LORA)), full((1, C_KV_LORA)),
                  full((C_Q_LORA, C_QK_W)), full((C_KV_LORA, C_QK_W)), full((C_KV_LORA, BRANCH_WIDTH)),
                  full((LANES, C_QK_W)), full((C_QK_W, C_QK_W)),
                  full((1, C_QK_W)), full((1, C_QK_W)), tab, tab, tab],
        out_specs=[pl.BlockSpec((tm, C_QK_W), lambda i: (i, 0)),
                   pl.BlockSpec((tm, C_QK_W), lambda i: (i, 0)),
                   pl.BlockSpec((tm, BRANCH_WIDTH), lambda i: (i, 0))],
        out_shape=[jax.ShapeDtypeStruct((t, C_QK_W), BF16),
                   jax.ShapeDtypeStruct((t, C_QK_W), BF16),
                   jax.ShapeDtypeStruct((t, BRANCH_WIDTH), BF16)],
        compiler_params=_cparams(("parallel",)),
        name="c_prep",
    )(pf, cqg, ckvg, wq, wk, wv, ppe, grp, gq, gk, cos, s1, s2)


def _flash_update(e, s, v_chunk, m_ref, l_ref, acc_ref):
    m_prev = m_ref[e]
    m_new = jnp.maximum(m_prev, jnp.max(s, axis=1, keepdims=True))
    alpha = jnp.exp(m_prev - m_new)
    p = jnp.exp(s - m_new)
    l_ref[e] = alpha * l_ref[e] + jnp.sum(p, axis=1, keepdims=True)
    acc_ref[e] = alpha * acc_ref[e] + jnp.dot(p.astype(BF16), v_chunk, preferred_element_type=F32)
    m_ref[e] = m_new


def _flash_init(m_ref, l_ref, acc_ref):
    m_ref[...] = jnp.full(m_ref.shape, NEG, F32)
    l_ref[...] = jnp.zeros(l_ref.shape, F32)
    acc_ref[...] = jnp.zeros(acc_ref.shape, F32)


def _flash_finish(o_ref, l_ref, acc_ref):
    lane = lax.broadcasted_iota(jnp.int32, acc_ref.shape[1:], 1)
    o0 = acc_ref[0] / l_ref[0]
    o1 = acc_ref[1] / l_ref[1]
    o_ref[...] = jnp.where(lane < HEAD_DIM, o0, o1).astype(o_ref.dtype)


_CONTRACT_LANES = (((1,), (1,)), ((), ()))


def _head_masks(width, ranges0, ranges1):
    m = np.zeros((2, 1, width), np.float32)
    for e, ranges in enumerate((ranges0, ranges1)):
        for lo, hi in ranges:
            m[e, 0, lo:hi] = 1
    return jnp.asarray(m, BF16)


def _pair_masks():
    return _head_masks(LANES, [(0, HEAD_DIM)], [(HEAD_DIM, LANES)])


def _c_attn_kernel(q_ref, hm_ref, k_ref, v_ref, o_ref, m_ref, l_ref, acc_ref, *, tq, tk):
    qi = pl.program_id(2)
    q = q_ref[...]
    qs = (q * hm_ref[0], q * hm_ref[1])
    _flash_init(m_ref, l_ref, acc_ref)
    n_full = (qi * tq) // tk

    def chunk(c, masked):
        off = pl.multiple_of(c * tk, tk)
        kc = k_ref[pl.ds(off, tk), :]
        vc = v_ref[pl.ds(off, tk), :]
        for e in range(2):
            s = lax.dot_general(qs[e], kc, _CONTRACT_LANES, preferred_element_type=F32)
            if masked:
                row = qi * tq + lax.broadcasted_iota(jnp.int32, s.shape, 0)
                col = c * tk + lax.broadcasted_iota(jnp.int32, s.shape, 1)
                s = jnp.where(col <= row, s, NEG)
            _flash_update(e, s, vc, m_ref, l_ref, acc_ref)

    def body(c, carry):
        chunk(c, False)
        return carry

    lax.fori_loop(0, n_full, body, 0)
    chunk(n_full, True)
    _flash_finish(o_ref, l_ref, acc_ref)


def _c_attn(qc, kc, vc, batch, seq, tq=256, tk=512):
    t = qc.shape[0]
    nq = seq // tq
    npair = C_HEADS // 2
    return pl.pallas_call(
        functools.partial(_c_attn_kernel, tq=tq, tk=tk),
        grid=(batch, npair, nq),
        in_specs=[pl.BlockSpec((tq, C_PAIR), lambda b, p, i: (b * nq + i, p)),
                  pl.BlockSpec((2, 1, C_PAIR), lambda b, p, i: (0, 0, 0)),
                  pl.BlockSpec((seq, C_PAIR), lambda b, p, i: (b, p)),
                  pl.BlockSpec((seq, LANES), lambda b, p, i: (b, p))],
        out_specs=pl.BlockSpec((tq, LANES), lambda b, p, i: (b * nq + i, p)),
        out_shape=jax.ShapeDtypeStruct((t, BRANCH_WIDTH), BF16),
        scratch_shapes=[pltpu.VMEM((2, tq, 1), F32), pltpu.VMEM((2, tq, 1), F32),
                        pltpu.VMEM((2, tq, LANES), F32)],
        compiler_params=_cparams(("parallel", "parallel", "arbitrary")),
        name="c_attn",
    )(qc, _head_masks(C_PAIR, [(0, C_NOPE), (2 * C_NOPE, 2 * C_NOPE + C_ROPE)],
                      [(C_NOPE, 2 * C_NOPE), (2 * C_NOPE + C_ROPE, 2 * C_NOPE + 2 * C_ROPE)]), kc, vc)


def _a_attn_kernel(q_ref, hm_ref, k_ref, v_ref, sel_ref, bias_ref, o_ref, m_ref, l_ref, acc_ref, *, tq, tk):
    qi = pl.program_id(1)
    q = q_ref[...]
    qs = (q * hm_ref[0], q * hm_ref[1])
    _flash_init(m_ref, l_ref, acc_ref)
    n_chunks = ((qi + 1) * tq + tk - 1) // tk

    def body(c, carry):
        off = pl.multiple_of(c * tk, tk)
        kc = k_ref[pl.ds(off, tk), :]
        vc = v_ref[pl.ds(off, tk), :]
        selmask = sel_ref[:, pl.ds(off, tk)].astype(F32)
        for e in range(2):
            s = lax.dot_general(qs[e], kc, _CONTRACT_LANES, preferred_element_type=F32)
            rows = []
            for i in range(tq // BLOCK):
                tiles = []
                for j in range(tk // BLOCK):
                    d = (qi * (tq // BLOCK) + i) - (c * (tk // BLOCK) + j)
                    d = jnp.clip(d, 0, A_BIAS_TILES - 1)
                    tiles.append(bias_ref[e, d])
                rows.append(jnp.concatenate(tiles, axis=1))
            s = s + jnp.concatenate(rows, axis=0) + selmask
            _flash_update(e, s, vc, m_ref, l_ref, acc_ref)
        return carry

    lax.fori_loop(0, n_chunks, body, 0)
    _flash_finish(o_ref, l_ref, acc_ref)


def _a_attn(pn, pp, sel, bias_tiles, batch, seq, tq=256, tk=512):
    t = pn.shape[0]
    nq = seq // tq
    npair = A_HEADS // 2
    qcol, kcol, vcol = PN_AQ // LANES, PN_AK // LANES, PP_AV // LANES
    return pl.pallas_call(
        functools.partial(_a_attn_kernel, tq=tq, tk=tk),
        grid=(batch, nq, npair),
        in_specs=[pl.BlockSpec((tq, LANES), lambda b, i, p: (b * nq + i, qcol + p)),
                  pl.BlockSpec((2, 1, LANES), lambda b, i, p: (0, 0, 0)),
                  pl.BlockSpec((seq, LANES), lambda b, i, p: (b, kcol + p)),
                  pl.BlockSpec((seq, LANES), lambda b, i, p: (b, vcol + p)),
                  pl.BlockSpec((tq, seq), lambda b, i, p: (b * nq + i, 0)),
                  pl.BlockSpec((2, A_BIAS_TILES, BLOCK, BLOCK), lambda b, i, p: (p, 0, 0, 0))],
        out_specs=pl.BlockSpec((tq, LANES), lambda b, i, p: (b * nq + i, p)),
        out_shape=jax.ShapeDtypeStruct((t, BRANCH_WIDTH), BF16),
        scratch_shapes=[pltpu.VMEM((2, tq, 1), F32), pltpu.VMEM((2, tq, 1), F32),
                        pltpu.VMEM((2, tq, LANES), F32)],
        compiler_params=_cparams(("parallel", "parallel", "arbitrary")),
        name="a_attn",
    )(pn, _pair_masks(), pn, pp, sel, bias_tiles)


def _sortable_key(score):
    bits = pltpu.bitcast(score, jnp.int32)
    return bits ^ ((bits >> 31) & jnp.int32(0x7FFFFFFF))


SEL_ROWS = 256
CNT_ROWS = 512


def _a_select_kernel(qblk_ref, kblk_ref, ph_ref, pl_ref, pkh_ref, pkl_ref, sel_ref,
                     ikx_ref, keys_ref, iqt_ref, jcut_ref, *, seq, k_sel):
    qi = pl.program_id(1)
    n_sel = (qi * BLOCK + BLOCK + SEL_ROWS - 1) // SEL_ROWS
    n_cnt = (qi * BLOCK + BLOCK + CNT_ROWS - 1) // CNT_ROWS

    @pl.when(qi == 0)
    def _():
        def prep(c, carry):
            off = pl.multiple_of(c * 512, 512)
            kv = kblk_ref[pl.ds(off, 512), :]
            hi = kv.astype(BF16)
            lo = (kv - hi.astype(F32)).astype(BF16)
            ikx_ref[pl.ds(off, 512), :] = (
                jnp.dot(hi, pkh_ref[...], preferred_element_type=F32)
                + jnp.dot(lo, pkl_ref[...], preferred_element_type=F32)).astype(BF16)
            return carry
        lax.fori_loop(0, seq // 512, prep, 0)

    qb = qblk_ref[...]
    iq = qb[:, :IDX_HEADS * IDX_DIM]
    iq_hi = iq.astype(BF16)
    iq_lo = (iq - iq_hi.astype(F32)).astype(BF16)
    iqx = (jnp.dot(iq_hi, ph_ref[...], preferred_element_type=F32)
           + jnp.dot(iq_lo, pl_ref[...], preferred_element_type=F32))
    for h in range(IDX_HEADS):
        iqt_ref[h // 2, :, (h % 2) * LANES:(h % 2 + 1) * LANES] = iqx[:, h * LANES:(h + 1) * LANES].T.astype(BF16)
    iw_t = qb[:, IDX_HEADS * IDX_DIM:IDX_HEADS * IDX_DIM + LANES].T
    iw_rows = [iw_t[IDX_DIM + h:IDX_DIM + h + 1, :] for h in range(IDX_HEADS)]
    q_pos = qi * BLOCK + lax.broadcasted_iota(jnp.int32, (SEL_ROWS, BLOCK), 1)
    k_row = lax.broadcasted_iota(jnp.int32, (SEL_ROWS, BLOCK), 0)

    def score_chunk(c, masked):
        off = pl.multiple_of(c * SEL_ROWS, SEL_ROWS)
        kx = ikx_ref[pl.ds(off, SEL_ROWS), :]
        sc = jnp.zeros((SEL_ROWS, BLOCK), F32)
        for hp in range(IDX_HEADS // 2):
            xx = jnp.dot(kx, iqt_ref[hp], preferred_element_type=F32)
            sc = (sc + jnp.maximum(xx[:, :LANES], 0.0) * iw_rows[2 * hp]
                  + jnp.maximum(xx[:, LANES:], 0.0) * iw_rows[2 * hp + 1])
        key = _sortable_key(sc + 0.0)
        if masked:
            key = jnp.where(off + k_row <= q_pos, key, INT_MIN)
        keys_ref[pl.ds(off, SEL_ROWS), :] = key

    def score_body(c, carry):
        score_chunk(c, False)
        return carry

    lax.fori_loop(0, n_sel - 1, score_body, 0)
    score_chunk(n_sel - 1, True)

    @pl.when(n_sel * SEL_ROWS < n_cnt * CNT_ROWS)
    def _():
        off = pl.multiple_of(n_sel * SEL_ROWS, SEL_ROWS)
        keys_ref[pl.ds(off, SEL_ROWS), :] = jnp.full((SEL_ROWS, BLOCK), INT_MIN, jnp.int32)

    def count_rows(pred):
        def cnt_chunk(c, acc):
            off = pl.multiple_of(c * CNT_ROWS, CNT_ROWS)
            hit = jnp.where(pred(keys_ref[pl.ds(off, CNT_ROWS), :], off), 1, 0)
            return acc + jnp.sum(hit.reshape(CNT_ROWS // 8, 8, BLOCK), axis=0)
        acc = lax.fori_loop(0, n_cnt, cnt_chunk, jnp.zeros((8, BLOCK), jnp.int32))
        return jnp.sum(acc, axis=0, keepdims=True)

    def count_ge(cand):
        return count_rows(lambda keys, off: keys >= cand)

    def bisect(it, lo):
        step = lax.shift_left(jnp.int32(1), jnp.int32(31) - it)
        cand = lo + step
        return jnp.where(count_ge(cand) >= k_sel, cand, lo)

    thr = lax.fori_loop(0, 32, bisect, jnp.full((1, BLOCK), INT_MIN, jnp.int32))
    has_k = thr > INT_MIN
    thr = jnp.maximum(thr, INT_MIN + 1)

    excess = jnp.where(has_k, count_ge(thr) - k_sel, 0)
    jcut_ref[...] = jnp.full(jcut_ref.shape, 2 ** 31 - 1, jnp.int32)

    @pl.when(jnp.max(excess) > 0)
    def _():
        need = jnp.where(excess > 0, k_sel - count_ge(thr + 1), 1)
        rev_row = (seq - 1) - lax.broadcasted_iota(jnp.int32, (CNT_ROWS, BLOCK), 0)

        def tie_bisect(it, lo):
            cand = lo + lax.shift_left(jnp.int32(1), jnp.int32(seq.bit_length() - 2) - it)
            cnt = count_rows(lambda keys, off: (keys == thr) & (rev_row - off >= cand))
            return jnp.where(cnt >= need, cand, lo)

        rev = lax.fori_loop(0, seq.bit_length() - 1, tie_bisect, jnp.zeros((1, BLOCK), jnp.int32))
        jcut = jnp.where(excess > 0, (seq - 1) - rev, 2 ** 31 - 1)
        jcut_ref[...] = jnp.broadcast_to(jcut, jcut_ref.shape)

    jcut = jcut_ref[0:1, :]

    def emit(c, carry):
        off = pl.multiple_of(c * SEL_ROWS, SEL_ROWS)
        keys = keys_ref[pl.ds(off, SEL_ROWS), :]
        bar = jnp.where(off + k_row > jcut, thr + 1, thr)
        add = jnp.where(keys >= bar, 0.0, NEG)
        sel_ref[:, pl.ds(off, SEL_ROWS)] = jnp.concatenate(
            [add[j * BLOCK:(j + 1) * BLOCK].T for j in range(SEL_ROWS // BLOCK)], axis=1).astype(BF16)
        return carry

    lax.fori_loop(0, n_sel, emit, 0)

    def fill(c, carry):
        off = pl.multiple_of(c * SEL_ROWS, SEL_ROWS)
        sel_ref[:, pl.ds(off, SEL_ROWS)] = jnp.full((BLOCK, SEL_ROWS), NEG, BF16)
        return carry

    lax.fori_loop(n_sel, seq // SEL_ROWS, fill, 0)


def _a_select(pf, batch, seq):
    t = pf.shape[0]
    nq = seq // BLOCK
    k_sel = min(TOPK_MAX, seq // 4)
    nlane = IDX_HEADS * IDX_DIM
    r = np.arange(nlane)
    ph = np.zeros((nlane, IDX_HEADS * LANES), np.float32)
    plo = np.zeros((nlane, IDX_HEADS * LANES), np.float32)
    ph[r, (r // IDX_DIM) * LANES + r % IDX_DIM] = 1
    ph[r, (r // IDX_DIM) * LANES + 2 * IDX_DIM + r % IDX_DIM] = 1
    plo[r, (r // IDX_DIM) * LANES + IDX_DIM + r % IDX_DIM] = 1
    d = np.arange(IDX_DIM)
    pkh = np.zeros((LANES, LANES), np.float32)
    pkl = np.zeros((LANES, LANES), np.float32)
    pkh[d, d] = 1
    pkh[d, IDX_DIM + d] = 1
    pkl[d, 2 * IDX_DIM + d] = 1
    full = lambda shape: pl.BlockSpec(shape, lambda b, i: (0,) * len(shape))
    return pl.pallas_call(
        functools.partial(_a_select_kernel, seq=seq, k_sel=k_sel),
        grid=(batch, nq),
        in_specs=[pl.BlockSpec((BLOCK, 512), lambda b, i: (b * nq + i, 0)),
                  pl.BlockSpec((seq, LANES), lambda b, i: (b, nlane // LANES)),
                  full(ph.shape), full(plo.shape), full(pkh.shape), full(pkl.shape)],
        out_specs=pl.BlockSpec((BLOCK, seq), lambda b, i: (b * nq + i, 0)),
        out_shape=jax.ShapeDtypeStruct((t, seq), BF16),
        scratch_shapes=[pltpu.VMEM((seq, LANES), BF16), pltpu.VMEM((seq, LANES), jnp.int32),
                        pltpu.VMEM((IDX_HEADS // 2, LANES, 2 * LANES), BF16),
                        pltpu.VMEM((8, LANES), jnp.int32)],
        compiler_params=_cparams(("parallel", "arbitrary")),
        name="a_select",
    )(pf, pf, jnp.asarray(ph, BF16), jnp.asarray(plo, BF16), jnp.asarray(pkh, BF16),
      jnp.asarray(pkl, BF16))


BAND_RBLK = 4


def _band_kernel(*refs, has_sink, want_lse, kv_div, rblk):
    if has_sink:
        sink_ref, refs = refs[0], refs[1:]
    q_ref, hm_ref, kp_ref, kc_ref, vp_ref, vc_ref, bias_ref, o_ref = refs[:8]
    first = pl.program_id(1) == 0
    lane = lax.broadcasted_iota(jnp.int32, (BLOCK, LANES), 1)
    col = lax.broadcasted_iota(jnp.int32, (BLOCK, 2 * BLOCK), 1)
    no_prev = jnp.logical_and(first, col < BLOCK)
    for p in range(BRANCH_WIDTH // LANES):
        kcol = (p // kv_div) * LANES
        kk = jnp.concatenate([kp_ref[:, kcol:kcol + LANES], kc_ref[:, kcol:kcol + LANES]], axis=0)
        vv = jnp.concatenate([vp_ref[:, kcol:kcol + LANES], vc_ref[:, kcol:kcol + LANES]], axis=0)
        for i in range(rblk):
            rows = slice(i * BLOCK, (i + 1) * BLOCK)
            q = q_ref[rows, p * LANES:(p + 1) * LANES]
            kblk = kk[i * BLOCK:(i + 2) * BLOCK]
            vblk = vv[i * BLOCK:(i + 2) * BLOCK]
            outs, lses = [], []
            for e in range(2):
                s = lax.dot_general(q * hm_ref[e], kblk, _CONTRACT_LANES, preferred_element_type=F32)
                s = s + bias_ref[2 * p + e]
                if i == 0:
                    s = jnp.where(no_prev, NEG, s)
                m = jnp.max(s, axis=1, keepdims=True)
                if has_sink:
                    sk = sink_ref[2 * p + e]
                    m = jnp.maximum(m, sk)
                pr = jnp.exp(s - m)
                den = jnp.sum(pr, axis=1, keepdims=True)
                if has_sink:
                    den = den + jnp.exp(sk - m)
                outs.append(jnp.dot(pr.astype(BF16), vblk, preferred_element_type=F32) / den)
                lses.append(m + jnp.log(den))
            o_ref[rows, p * LANES:(p + 1) * LANES] = jnp.where(lane < HEAD_DIM, outs[0], outs[1]).astype(o_ref.dtype)
            if want_lse:
                refs[8][rows, p * LANES:(p + 1) * LANES] = jnp.where(lane < HEAD_DIM, lses[0], lses[1])


def _band_attn(q_arr, k_arr, v_arr, bias, nsub, sub_len, qcb, kcb, vcb, kv_w, sinks=None, want_lse=False):
    t = q_arr.shape[0]
    rblk = min(BAND_RBLK, sub_len // BLOCK)
    step_rows = rblk * BLOCK
    nbs = sub_len // step_rows
    kv_div = BRANCH_WIDTH // kv_w

    def cur(u, n):
        return u * nbs + n

    def prev(u, n):
        return jnp.where(n == 0, u * nbs * rblk, (u * nbs + n) * rblk - 1)

    in_specs = [
        pl.BlockSpec((step_rows, BRANCH_WIDTH), lambda u, n: (cur(u, n), qcb)),
        pl.BlockSpec((2, 1, LANES), lambda u, n: (0, 0, 0)),
        pl.BlockSpec((BLOCK, kv_w), lambda u, n: (prev(u, n), kcb)),
        pl.BlockSpec((step_rows, kv_w), lambda u, n: (cur(u, n), kcb)),
        pl.BlockSpec((BLOCK, kv_w), lambda u, n: (prev(u, n), vcb)),
        pl.BlockSpec((step_rows, kv_w), lambda u, n: (cur(u, n), vcb)),
        pl.BlockSpec(bias.shape, lambda u, n: (0, 0, 0)),
    ]
    args = [q_arr, _pair_masks(), k_arr, k_arr, v_arr, v_arr, bias]
    if sinks is not None:
        in_specs = [pl.BlockSpec(memory_space=pltpu.SMEM)] + in_specs
        args = [sinks] + args
    o_spec = pl.BlockSpec((step_rows, BRANCH_WIDTH), lambda u, n: (cur(u, n), 0))
    out_specs = [o_spec]
    out_shape = [jax.ShapeDtypeStruct((t, BRANCH_WIDTH), BF16)]
    if want_lse:
        out_specs.append(o_spec)
        out_shape.append(jax.ShapeDtypeStruct((t, BRANCH_WIDTH), F32))
    return pl.pallas_call(
        functools.partial(_band_kernel, has_sink=sinks is not None, want_lse=want_lse, kv_div=kv_div, rblk=rblk),
        grid=(nsub, nbs),
        in_specs=in_specs,
        out_specs=out_specs,
        out_shape=out_shape,
        compiler_params=_cparams(("parallel", "arbitrary")),
        name="band_attn",
    )(*args)


def _band_bias(table, step, max_dist):
    rel = BLOCK + np.arange(BLOCK)[:, None] - np.arange(2 * BLOCK)[None, :]
    bias = table[_t5_bucket(jnp.asarray(rel * step))].transpose(2, 0, 1).astype(F32)
    ok = (rel >= 0) & (rel <= max_dist)
    return jnp.where(jnp.asarray(ok)[None], bias, NEG)


def _merge_kernel(x_ref, ya_ref, yb0_ref, yb1_ref, yb2_ref, l0_ref, l1_ref, l2_ref, yc_ref, yd_ref,
                  za_ref, zb_ref, zc_ref, zd_ref, ga_ref, gb_ref, gc_ref, gd_ref, wb_ref, wo_ref, o_ref,
                  y1_ref, y2_ref, s1_ref, s2_ref):
    ncol = BRANCH_WIDTH // LANES
    for src, dst in ((yb1_ref, y1_ref), (yb2_ref, y2_ref), (l1_ref, s1_ref), (l2_ref, s2_ref)):
        dil = src.shape[1]
        for r in range(dil):
            blk = src[0, r].astype(F32)
            for c in range(ncol):
                dst[c, pl.ds(r, src.shape[2], stride=dil), :] = blk[:, c * LANES:(c + 1) * LANES]
    wide = lambda ref: jnp.concatenate([ref[c] for c in range(ncol)], axis=1)
    l0, l1, l2 = l0_ref[0, 0], wide(s1_ref), wide(s2_ref)
    mx = jnp.maximum(jnp.maximum(l0, l1), l2)
    w0, w1, w2 = jnp.exp(l0 - mx), jnp.exp(l1 - mx), jnp.exp(l2 - mx)
    yb = (w0 * yb0_ref[0, 0].astype(F32) + w1 * wide(y1_ref) + w2 * wide(y2_ref)) / (w0 + w1 + w2)
    ys = (ya_ref[...].astype(F32), yb, yc_ref[...].astype(F32), yd_ref[...].astype(F32))
    zs = (za_ref, zb_ref, zc_ref, zd_ref)
    gs = (ga_ref, gb_ref, gc_ref, gd_ref)
    merged = jnp.zeros(o_ref.shape, F32)
    for n in range(N_BRANCH):
        z = zs[n][...].astype(F32)
        u = (ys[n] * (z * jax.nn.sigmoid(z))).astype(BF16)
        proj = jnp.dot(u, wb_ref[n], preferred_element_type=F32)
        merged = merged + jax.nn.sigmoid(gs[n][...].astype(F32)) * proj
    o_ref[...] = x_ref[...] + jnp.dot(merged.astype(BF16), wo_ref[...], preferred_element_type=F32)


def _merge(x2, ya, yb, lb, yc, yd, pp, wb, wo, batch, seq, tm=512):
    t = x2.shape[0]
    bw = BRANCH_WIDTH
    nsb = seq // tm
    row = lambda w, c: pl.BlockSpec((tm, w), lambda i: (i, c))

    def sub(dil):
        return pl.BlockSpec((1, dil, tm // dil, bw), lambda i: (i // nsb, 0, i % nsb, 0))

    dils = [dil for _, dil in B_GROUPS]
    yb = [a.reshape(batch, dil, seq // dil, bw) for a, dil in zip(yb, dils)]
    lb = [a.reshape(batch, dil, seq // dil, bw) for a, dil in zip(lb, dils)]
    in_specs = ([row(D_MODEL, 0), row(bw, 0)] + [sub(dil) for dil in dils] * 2 + [row(bw, 0), row(bw, 0)]
                + [row(bw, PP_AZ // bw), row(bw, PP_BZ // bw), row(bw, PP_CZ // bw), row(bw, PP_DZ // bw)]
                + [row(D_MODEL, PP_G // D_MODEL + n) for n in range(N_BRANCH)]
                + [pl.BlockSpec((N_BRANCH, bw, D_MODEL), lambda i: (0, 0, 0)),
                   pl.BlockSpec((D_MODEL, D_MODEL), lambda i: (0, 0))])
    return pl.pallas_call(
        _merge_kernel,
        grid=(t // tm,),
        in_specs=in_specs,
        out_specs=row(D_MODEL, 0),
        out_shape=jax.ShapeDtypeStruct((t, D_MODEL), F32),
        scratch_shapes=[pltpu.VMEM((bw // LANES, tm, LANES), F32)] * 4,
        compiler_params=_cparams(("parallel",)),
        name="merge",
    )(x2, ya, yb[0], yb[1], yb[2], lb[0], lb[1], lb[2], yc, yd, pp, pp, pp, pp, pp, pp, pp, pp, wb, wo)


def _layer_weights(w_in, qk_gain_a, qk_gain_b, qk_gain_d):
    def cols(start, width):
        return w_in[:, :, start:start + width]

    n_bq = len(B_GROUPS) * B_HEADS * HEAD_DIM
    dk = cols(_D0 + 512, 128)
    dv = cols(_D0 + 640, 128)
    dup = lambda a: jnp.concatenate([a[:, :, :64], a[:, :, :64], a[:, :, 64:], a[:, :, 64:]], axis=-1)
    w_pn = jnp.concatenate([cols(_A0, 512), cols(_A0 + 512, 512), cols(_D0, 512), dup(dk), dup(dv)],
                           axis=-1).astype(BF16)
    w_pp = jnp.concatenate([cols(_G0, N_BRANCH * D_MODEL), cols(_A0 + 1024, 512), cols(_A0 + 1536, 512),
                            cols(_B0 + 3 * n_bq, 512), cols(_C0 + 416, 512), cols(_D0 + 768, 512)],
                           axis=-1).astype(BF16)
    w_bg = [jnp.concatenate([cols(_B0 + g * 512, 512), cols(_B0 + n_bq + g * 512, 512),
                             cols(_B0 + 2 * n_bq + g * 512, 512)], axis=-1).astype(BF16)
            for g in range(len(B_GROUPS))]
    depth = w_in.shape[0]
    zeros = lambda w: jnp.zeros((depth, D_MODEL, w), w_in.dtype)
    w_pf = jnp.concatenate([cols(_A0 + 2048, 296), zeros(216), cols(_C0, 416), zeros(96)], axis=-1).astype(BF16)

    scale = HEAD_DIM ** -0.5
    tile = lambda g, reps: jnp.tile(g, (1, reps))
    hg = jnp.concatenate([tile(qk_gain_a[:, 0] * scale, 8), tile(qk_gain_a[:, 1], 8),
                          tile(qk_gain_d[:, 0] * scale, 8), tile(qk_gain_d[:, 1], 4),
                          jnp.ones((depth, 256), F32)], axis=-1)
    flag = jnp.concatenate([jnp.ones((PN_DV,), F32), jnp.zeros((PN_W - PN_DV,), F32)])
    hg_b = jnp.concatenate([tile(qk_gain_b[:, 0] * scale, 8), tile(qk_gain_b[:, 1], 8),
                            jnp.ones((depth, 512), F32)], axis=-1)
    flag_b = jnp.concatenate([jnp.ones((2 * BRANCH_WIDTH,), F32), jnp.zeros((BRANCH_WIDTH,), F32)])
    return w_pn, w_pp, w_bg, w_pf, hg, flag, hg_b, flag_b


def _c_weights(qk_gain_c, w_q_b, w_kv_b, seq):
    src_q, src_k, gain_idx, head, pe_src, rope_j, rope_half = _c_layout_tables()
    take = lambda w, src: jnp.where(jnp.asarray(src >= 0), jnp.take(w, jnp.asarray(np.maximum(src, 0)), axis=-1), 0.0)
    wq = take(w_q_b, src_q).astype(BF16)
    wk = take(w_kv_b, src_k).astype(BF16)
    v_src = np.array([h * (C_NOPE + C_V) + C_NOPE + c for h in range(C_HEADS) for c in range(C_V)])
    wv = jnp.take(w_kv_b, jnp.asarray(v_src), axis=-1).astype(BF16)
    ppe = np.zeros((LANES, C_QK_W), np.float32)
    ln = np.nonzero(pe_src >= 0)[0]
    ppe[pe_src[ln], ln] = 1
    grp = ((head[:, None] == head[None, :]) & (head[:, None] >= 0)).astype(np.float32)
    g_take = lambda g: jnp.where(jnp.asarray(gain_idx >= 0), jnp.take(g, jnp.asarray(np.maximum(gain_idx, 0)), axis=-1), 0.0)
    gq = g_take(qk_gain_c[:, 0]) * (C_NOPE + C_ROPE) ** -0.5
    gk = g_take(qk_gain_c[:, 1])
    freq = ROPE_THETA ** (-jnp.arange(C_ROPE_HALF, dtype=F32) / C_ROPE_HALF)
    ang = jnp.arange(seq).astype(F32)[:, None] * freq[None, :]
    cos_j, sin_j = jnp.cos(ang), jnp.sin(ang)
    is_rope = jnp.asarray(rope_j >= 0)
    jj = jnp.asarray(np.maximum(rope_j, 0))
    cos = jnp.where(is_rope, cos_j[:, jj], 1.0)
    sin = jnp.where(is_rope, sin_j[:, jj], 0.0)
    s1 = jnp.where(jnp.asarray(rope_half == 0), -sin, 0.0)
    s2 = jnp.where(jnp.asarray(rope_half == 1), sin, 0.0)
    return wq, wk, wv, jnp.asarray(ppe, BF16), jnp.asarray(grp, BF16), gq, gk, cos, s1, s2


def _a_bias_tiles(table):
    a = np.arange(BLOCK)
    dist = (np.arange(A_BIAS_TILES)[:, None, None] * BLOCK + a[None, :, None] - a[None, None, :])
    return table[_t5_bucket(jnp.asarray(dist))].transpose(3, 0, 1, 2).astype(F32)


def kernel(x, norm_gain, w_in, qk_gain_a, qk_gain_b, qk_gain_c, qk_gain_d, c_q_gain, c_kv_gain,
           w_q_b, w_kv_b, sinks, rel_bias, w_branch, w_out):
    batch, seq, d_model = x.shape
    depth = w_in.shape[0]
    t = batch * seq
    x2 = x.reshape(t, d_model)

    w_pn, w_pp, w_bg, w_pf, hg, flag, hg_b, flag_b = _layer_weights(w_in, qk_gain_a, qk_gain_b, qk_gain_d)
    wq, wk, wv, ppe, grp, gq, gk, cos, s1, s2 = _c_weights(qk_gain_c, w_q_b, w_kv_b, seq)
    wb = w_branch.astype(BF16)
    wo = w_out.astype(BF16)

    bias_a = _a_bias_tiles(rel_bias[:, :A_HEADS])
    bias_b = [_band_bias(rel_bias[:, A_HEADS + g * B_HEADS:A_HEADS + (g + 1) * B_HEADS], dil, window // dil)
              for g, (window, dil) in enumerate(B_GROUPS)]
    bias_d = _band_bias(rel_bias[:, N_BIAS_HEADS - D_HEADS:], 1, D_WINDOW - 1)

    for l in range(depth):
        pn = _proj(x2, norm_gain[l], w_pn[l], BF16, head_gain=hg[l], flag=flag)
        pp = _proj(x2, norm_gain[l], w_pp[l], BF16)
        pf = _proj(x2, norm_gain[l], w_pf[l], F32)

        sel = _a_select(pf, batch, seq)
        ya = _a_attn(pn, pp, sel, bias_a, batch, seq)

        yb, lb = [], []
        for g, (window, dil) in enumerate(B_GROUPS):
            bg = _proj(x2, norm_gain[l], w_bg[g][l], BF16, head_gain=hg_b[l], flag=flag_b, dil=dil, seq=seq)
            o, lse = _band_attn(bg, bg, bg, bias_b[g], batch * dil, seq // dil, 0, 1, 2, BRANCH_WIDTH,
                                want_lse=True)
            yb.append(o)
            lb.append(lse)

        qc, kc, vc = _c_prep(pf, c_q_gain[l].reshape(1, -1), c_kv_gain[l].reshape(1, -1), wq[l], wk[l], wv[l],
                             ppe, grp, gq[l].reshape(1, -1), gk[l].reshape(1, -1), cos, s1, s2, seq)
        yc = _c_attn(qc, kc, vc, batch, seq)

        (yd,) = _band_attn(pn, pn, pn, bias_d, batch, seq, PN_DQ // BRANCH_WIDTH, PN_DK // 256, PN_DV // 256,
                           256, sinks=sinks[l])

        x2 = _merge(x2, ya, yb, lb, yc, yd, pp, wb[l], wo[l], batch, seq)
    return x2.reshape(batch, seq, d_model)
```

```python
import functools
import math

import numpy as np
import jax
import jax.numpy as jnp
from jax import lax
from jax.experimental import pallas as pl
from jax.experimental.pallas import tpu as pltpu

F32 = jnp.float32
BF16 = jnp.bfloat16

D_MODEL = 1024
BLOCK = 128
HEAD_DIM = 64
N_BRANCH = 4
BRANCH_WIDTH = 512
EPS = 1e-6
A_HEADS = 8
IDX_HEADS = 8
IDX_DIM = 32
TOPK_MAX = 256
B_GROUPS = ((128, 1), (512, 4), (2048, 16))
B_HEADS = 8
C_HEADS = 8
C_NOPE = 64
C_ROPE = 32
C_V = 64
C_Q_LORA = 256
C_KV_LORA = 128
ROPE_THETA = 10000.0
D_HEADS = 8
D_KV_HEADS = 2
D_WINDOW = 128
NUM_BUCKETS = 32
MAX_DISTANCE = 2048
N_BIAS_HEADS = A_HEADS + len(B_GROUPS) * B_HEADS + D_HEADS

LANES = 128
NEG = -1e30
INT_MIN = -(2 ** 31)

_A0 = 0
_B0 = 2344
_C0 = 7464
_D0 = 8392
_G0 = 9672

PN_AQ, PN_AK, PN_DQ, PN_DK, PN_DV, PN_W = 0, 512, 1024, 1536, 1792, 2048
PP_G, PP_AV, PP_AZ, PP_BZ, PP_CZ, PP_DZ, PP_W = 0, 4096, 4608, 5120, 5632, 6144, 6656
PF_W = 1024
BG_W = 3 * BRANCH_WIDTH

A_BIAS_TILES = MAX_DISTANCE // BLOCK + 2


def _cparams(sem, vmem_mb=48):
    return pltpu.CompilerParams(dimension_semantics=sem, vmem_limit_bytes=vmem_mb * 1024 * 1024)


def _t5_bucket(dist):
    max_exact = NUM_BUCKETS // 2
    d = jnp.maximum(dist, 0)
    logd = jnp.log(jnp.maximum(d, 1).astype(F32) / max_exact)
    large = max_exact + (logd / math.log(MAX_DISTANCE / max_exact) * (NUM_BUCKETS - max_exact)).astype(jnp.int32)
    return jnp.where(d < max_exact, d, jnp.minimum(large, NUM_BUCKETS - 1))


def _proj_kernel(x_ref, g_ref, w_ref, *rest, norm, dil):
    rest = list(rest)
    strided = dil is not None and dil > 1
    xn_ref = rest.pop(-2) if strided else rest.pop(-1)
    res_ref = rest.pop(-1) if strided else None
    if norm:
        hg_ref, flag_ref, bd_ref, o_ref = rest
    else:
        (o_ref,) = rest

    @pl.when(pl.program_id(1) == 0)
    def _():
        x = x_ref[...]
        ms = jnp.mean(x * x, axis=-1, keepdims=True)
        xn_ref[...] = (x * lax.rsqrt(ms + EPS) * g_ref[...]).astype(BF16)

    h = jnp.dot(xn_ref[...], w_ref[...], preferred_element_type=F32)
    if norm:
        ss = jnp.dot((h * h).astype(BF16), bd_ref[...], preferred_element_type=F32)
        scale = lax.rsqrt(ss * (1.0 / HEAD_DIM) + EPS) * hg_ref[...]
        h = h * jnp.where(flag_ref[...] > 0, scale, 1.0)
    if dil is None:
        o_ref[...] = h.astype(o_ref.dtype)
    elif dil == 1:
        o_ref[0, 0] = h.astype(o_ref.dtype)
    else:
        sub = h.shape[0] // dil
        for c in range(h.shape[1] // LANES):
            res_ref[c] = h[:, c * LANES:(c + 1) * LANES]
        for r in range(dil):
            o_ref[0, r] = jnp.concatenate(
                [res_ref[c, pl.ds(r, sub, stride=dil), :] for c in range(h.shape[1] // LANES)],
                axis=1).astype(o_ref.dtype)


def _proj(x2, gain, w, out_dtype, head_gain=None, flag=None, dil=None, seq=None, tm=1024, tn=512):
    t, d = x2.shape
    n = w.shape[1]
    norm = head_gain is not None
    scratch = [pltpu.VMEM((tm, d), BF16)]
    if dil is None:
        out_spec = pl.BlockSpec((tm, tn), lambda i, j: (i, j))
        out_shape = jax.ShapeDtypeStruct((t, n), out_dtype)
        kdil = None
    else:
        nsb = seq // tm
        out_spec = pl.BlockSpec((1, dil, tm // dil, tn), lambda i, j: (i // nsb, 0, i % nsb, j))
        out_shape = jax.ShapeDtypeStruct((t // seq, dil, seq // dil, n), out_dtype)
        kdil = dil
        if dil > 1:
            scratch.append(pltpu.VMEM((tn // LANES, tm, LANES), F32))
    in_specs = [pl.BlockSpec((tm, d), lambda i, j: (i, 0)),
                pl.BlockSpec((1, d), lambda i, j: (0, 0)),
                pl.BlockSpec((d, tn), lambda i, j: (0, j))]
    args = [x2, gain.reshape(1, d), w]
    if norm:
        lane = np.arange(tn)
        bd = jnp.asarray((lane[:, None] // HEAD_DIM == lane[None, :] // HEAD_DIM), BF16)
        in_specs += [pl.BlockSpec((1, tn), lambda i, j: (0, j)),
                     pl.BlockSpec((1, tn), lambda i, j: (0, j)),
                     pl.BlockSpec((tn, tn), lambda i, j: (0, 0))]
        args += [head_gain.reshape(1, n), flag.reshape(1, n), bd]
    out = pl.pallas_call(
        functools.partial(_proj_kernel, norm=norm, dil=kdil),
        grid=(t // tm, n // tn),
        in_specs=in_specs,
        out_specs=out_spec,
        out_shape=out_shape,
        scratch_shapes=scratch,
        compiler_params=_cparams(("parallel", "arbitrary")),
        name=("proj_norm" if norm else "proj") + ("" if dil is None else "_dil%d" % dil),
    )(*args)
    return out.reshape(t, n)


C_PAIR = 256
C_QK_W = (C_HEADS // 2) * C_PAIR
C_ROPE_HALF = C_ROPE // 2


def _c_lane(h, c):
    base = (h // 2) * C_PAIR
    e = h % 2
    if c < C_NOPE:
        return base + e * C_NOPE + c
    return base + 2 * C_NOPE + e * C_ROPE + (c - C_NOPE)


def _c_layout_tables():
    src_q = np.full(C_QK_W, -1, np.int64)
    src_k = np.full(C_QK_W, -1, np.int64)
    gain_idx = np.full(C_QK_W, -1, np.int64)
    head = np.full(C_QK_W, -1, np.int64)
    pe_src = np.full(C_QK_W, -1, np.int64)
    rope_j = np.full(C_QK_W, -1, np.int64)
    rope_half = np.zeros(C_QK_W, np.int64)
    for h in range(C_HEADS):
        for c in range(C_NOPE + C_ROPE):
            ln = _c_lane(h, c)
            src_q[ln] = h * (C_NOPE + C_ROPE) + c
            gain_idx[ln] = c
            head[ln] = h
            if c < C_NOPE:
                src_k[ln] = h * (C_NOPE + C_V) + c
            else:
                r = c - C_NOPE
                pe_src[ln] = r
                rope_j[ln] = r % C_ROPE_HALF
                rope_half[ln] = r // C_ROPE_HALF
    return src_q, src_k, gain_idx, head, pe_src, rope_j, rope_half


def _c_prep_kernel(pf_ref, cqg_ref, ckvg_ref, wq_ref, wk_ref, wv_ref, ppe_ref, grp_ref,
                   gq_ref, gk_ref, cos_ref, s1_ref, s2_ref, q_ref, k_ref, v_ref):
    blk = pf_ref[...]
    cq = blk[:, :C_Q_LORA]
    ckv = blk[:, C_Q_LORA:C_Q_LORA + C_KV_LORA]
    pe = blk[:, C_Q_LORA + C_KV_LORA:]

    def rms(v, g):
        return v * lax.rsqrt(jnp.mean(v * v, axis=-1, keepdims=True) + EPS) * g

    def head_norm_rope(raw, gain):
        ss = jnp.dot((raw * raw).astype(BF16), grp_ref[...], preferred_element_type=F32)
        y = raw * lax.rsqrt(ss * (1.0 / (C_NOPE + C_ROPE)) + EPS) * gain
        up = pltpu.roll(y, C_QK_W - C_ROPE_HALF, 1)
        dn = pltpu.roll(y, C_ROPE_HALF, 1)
        return y * cos_ref[...] + up * s1_ref[...] + dn * s2_ref[...]

    cqn = rms(cq, cqg_ref[...]).astype(BF16)
    q_raw = jnp.dot(cqn, wq_ref[...], preferred_element_type=F32)
    q_ref[...] = head_norm_rope(q_raw, gq_ref[...]).astype(BF16)

    ckvn = rms(ckv, ckvg_ref[...]).astype(BF16)
    pe_hi = pe.astype(BF16)
    pe_lo = (pe - pe_hi.astype(F32)).astype(BF16)
    k_raw = (jnp.dot(ckvn, wk_ref[...], preferred_element_type=F32)
             + jnp.dot(pe_hi, ppe_ref[...], preferred_element_type=F32)
             + jnp.dot(pe_lo, ppe_ref[...], preferred_element_type=F32))
    k_ref[...] = head_norm_rope(k_raw, gk_ref[...]).astype(BF16)
    v_ref[...] = jnp.dot(ckvn, wv_ref[...], preferred_element_type=F32).astype(BF16)


def _c_prep(pf, cqg, ckvg, wq, wk, wv, ppe, grp, gq, gk, cos, s1, s2, seq, tm=512):
    t = pf.shape[0]
    nsb = seq // tm
    full = lambda shape: pl.BlockSpec(shape, lambda i: (0,) * len(shape))
    tab = pl.BlockSpec((tm, C_QK_W), lambda i: (i % nsb, 0))
    return pl.pallas_call(
        _c_prep_kernel,
        grid=(t // tm,),
        in_specs=[pl.BlockSpec((tm, 512), lambda i: (i, 1)),
                  full((1, C_Q_LORA)), full((1, C_KV_LORA)),
                  full((C_Q_LORA, C_QK_W)), full((C_KV_LORA, C_QK_W)), full((C_KV_LORA, BRANCH_WIDTH)),
                  full((LANES, C_QK_W)), full((C_QK_W, C_QK_W)),
                  full((1, C_QK_W)), full((1, C_QK_W)), tab, tab, tab],
        out_specs=[pl.BlockSpec((tm, C_QK_W), lambda i: (i, 0)),
                   pl.BlockSpec((tm, C_QK_W), lambda i: (i, 0)),
                   pl.BlockSpec((tm, BRANCH_WIDTH), lambda i: (i, 0))],
        out_shape=[jax.ShapeDtypeStruct((t, C_QK_W), BF16),
                   jax.ShapeDtypeStruct((t, C_QK_W), BF16),
                   jax.ShapeDtypeStruct((t, BRANCH_WIDTH), BF16)],
        compiler_params=_cparams(("parallel",)),
        name="c_prep",
    )(pf, cqg, ckvg, wq, wk, wv, ppe, grp, gq, gk, cos, s1, s2)


def _flash_update(e, s, v_ones, m_ref, acc_ref):
    m_prev = m_ref[e]
    m_new = jnp.maximum(m_prev, jnp.max(s, axis=1, keepdims=True))
    alpha = jnp.exp(m_prev - m_new)
    p = jnp.exp(s - pltpu.repeat(m_new, s.shape[1] // LANES, axis=1))
    acc_ref[e] = (jnp.concatenate([alpha, alpha], axis=1) * acc_ref[e]
                  + jnp.dot(p.astype(BF16), v_ones, preferred_element_type=F32))
    m_ref[e] = m_new


def _flash_init(m_ref, acc_ref):
    m_ref[...] = jnp.full(m_ref.shape, NEG, F32)
    acc_ref[...] = jnp.zeros(acc_ref.shape, F32)


def _flash_finish(o_ref, acc_ref):
    lane = lax.broadcasted_iota(jnp.int32, o_ref.shape, 1)
    o0 = acc_ref[0, :, :LANES] / acc_ref[0, :, LANES:]
    o1 = acc_ref[1, :, :LANES] / acc_ref[1, :, LANES:]
    o_ref[...] = jnp.where(lane < HEAD_DIM, o0, o1).astype(o_ref.dtype)


def _with_ones(v_chunk):
    return jnp.concatenate([v_chunk, jnp.ones(v_chunk.shape, v_chunk.dtype)], axis=1)


_CONTRACT_LANES = (((1,), (1,)), ((), ()))


def _head_masks(width, ranges0, ranges1):
    m = np.zeros((2, 1, width), np.float32)
    for e, ranges in enumerate((ranges0, ranges1)):
        for lo, hi in ranges:
            m[e, 0, lo:hi] = 1
    return jnp.asarray(m, BF16)


def _pair_masks():
    return _head_masks(LANES, [(0, HEAD_DIM)], [(HEAD_DIM, LANES)])


def _transpose_keys(k_ref, kt_ref, rows=512):
    def body(c, carry):
        off = pl.multiple_of(c * rows, rows)
        kt_ref[:, pl.ds(off, rows)] = k_ref[pl.ds(off, rows), :].astype(F32).T.astype(kt_ref.dtype)
        return carry
    lax.fori_loop(0, k_ref.shape[0] // rows, body, 0)


def _c_attn_kernel(q_ref, hm_ref, k_ref, v_ref, o_ref, kt_ref, m_ref, acc_ref, *, tq, tk):
    qi = pl.program_id(2)

    @pl.when(qi == 0)
    def _():
        _transpose_keys(k_ref, kt_ref)

    q = q_ref[...]
    qs = (q * hm_ref[0], q * hm_ref[1])
    _flash_init(m_ref, acc_ref)
    n_full = (qi * tq) // tk

    def chunk(c, masked):
        off = pl.multiple_of(c * tk, tk)
        kc = kt_ref[:, pl.ds(off, tk)]
        vc = _with_ones(v_ref[pl.ds(off, tk), :])
        for e in range(2):
            s = jnp.dot(qs[e], kc, preferred_element_type=F32)
            if masked:
                row = qi * tq + lax.broadcasted_iota(jnp.int32, s.shape, 0)
                col = c * tk + lax.broadcasted_iota(jnp.int32, s.shape, 1)
                s = jnp.where(col <= row, s, NEG)
            _flash_update(e, s, vc, m_ref, acc_ref)

    def body(c, carry):
        chunk(c, False)
        return carry

    lax.fori_loop(0, n_full, body, 0)
    chunk(n_full, True)
    _flash_finish(o_ref, acc_ref)


def _c_attn(qc, kc, vc, batch, seq, tq=512, tk=512):
    t = qc.shape[0]
    nq = seq // tq
    npair = C_HEADS // 2
    return pl.pallas_call(
        functools.partial(_c_attn_kernel, tq=tq, tk=tk),
        grid=(batch, npair, nq),
        in_specs=[pl.BlockSpec((tq, C_PAIR), lambda b, p, i: (b * nq + i, p)),
                  pl.BlockSpec((2, 1, C_PAIR), lambda b, p, i: (0, 0, 0)),
                  pl.BlockSpec((seq, C_PAIR), lambda b, p, i: (b, p)),
                  pl.BlockSpec((seq, LANES), lambda b, p, i: (b, p))],
        out_specs=pl.BlockSpec((tq, LANES), lambda b, p, i: (b * nq + i, p)),
        out_shape=jax.ShapeDtypeStruct((t, BRANCH_WIDTH), BF16),
        scratch_shapes=[pltpu.VMEM((C_PAIR, seq), BF16),
                        pltpu.VMEM((2, tq, LANES), F32), pltpu.VMEM((2, tq, 2 * LANES), F32)],
        compiler_params=_cparams(("parallel", "parallel", "arbitrary")),
        name="c_attn",
    )(qc, _head_masks(C_PAIR, [(0, C_NOPE), (2 * C_NOPE, 2 * C_NOPE + C_ROPE)],
                      [(C_NOPE, 2 * C_NOPE), (2 * C_NOPE + C_ROPE, 2 * C_NOPE + 2 * C_ROPE)]), kc, vc)


def _a_attn_kernel(q_ref, hm_ref, k_ref, v_ref, sel_ref, bias_ref, o_ref, kt_ref, m_ref, acc_ref,
                   *, tq, tk):
    qi = pl.program_id(2)

    @pl.when(qi == 0)
    def _():
        _transpose_keys(k_ref, kt_ref)

    q = q_ref[...]
    qs = (q * hm_ref[0], q * hm_ref[1])
    _flash_init(m_ref, acc_ref)
    n_chunks = ((qi + 1) * tq + tk - 1) // tk

    def body(c, carry):
        off = pl.multiple_of(c * tk, tk)
        kc = kt_ref[:, pl.ds(off, tk)]
        vc = _with_ones(v_ref[pl.ds(off, tk), :])
        selmask = sel_ref[:, pl.ds(off, tk)].astype(F32)
        for e in range(2):
            s = jnp.dot(qs[e], kc, preferred_element_type=F32)
            rows = []
            for i in range(tq // BLOCK):
                tiles = []
                for j in range(tk // BLOCK):
                    d = (qi * (tq // BLOCK) + i) - (c * (tk // BLOCK) + j)
                    d = jnp.clip(d, 0, A_BIAS_TILES - 1)
                    tiles.append(bias_ref[e, d])
                rows.append(jnp.concatenate(tiles, axis=1))
            s = s + jnp.concatenate(rows, axis=0) + selmask
            _flash_update(e, s, vc, m_ref, acc_ref)
        return carry

    lax.fori_loop(0, n_chunks, body, 0)
    _flash_finish(o_ref, acc_ref)


def _a_attn(pn, pp, sel, bias_tiles, batch, seq, tq=512, tk=1024):
    t = pn.shape[0]
    nq = seq // tq
    npair = A_HEADS // 2
    qcol, kcol, vcol = PN_AQ // LANES, PN_AK // LANES, PP_AV // LANES
    return pl.pallas_call(
        functools.partial(_a_attn_kernel, tq=tq, tk=tk),
        grid=(batch, npair, nq),
        in_specs=[pl.BlockSpec((tq, LANES), lambda b, p, i: (b * nq + i, qcol + p)),
                  pl.BlockSpec((2, 1, LANES), lambda b, p, i: (0, 0, 0)),
                  pl.BlockSpec((seq, LANES), lambda b, p, i: (b, kcol + p)),
                  pl.BlockSpec((seq, LANES), lambda b, p, i: (b, vcol + p)),
                  pl.BlockSpec((tq, seq), lambda b, p, i: (b * nq + i, 0)),
                  pl.BlockSpec((2, A_BIAS_TILES, BLOCK, BLOCK), lambda b, p, i: (p, 0, 0, 0))],
        out_specs=pl.BlockSpec((tq, LANES), lambda b, p, i: (b * nq + i, p)),
        out_shape=jax.ShapeDtypeStruct((t, BRANCH_WIDTH), BF16),
        scratch_shapes=[pltpu.VMEM((LANES, seq), BF16),
                        pltpu.VMEM((2, tq, LANES), F32), pltpu.VMEM((2, tq, 2 * LANES), F32)],
        compiler_params=_cparams(("parallel", "parallel", "arbitrary")),
        name="a_attn",
    )(pn, _pair_masks(), pn, pp, sel, bias_tiles)


def _sortable_key(score):
    bits = pltpu.bitcast(score, jnp.int32)
    return bits ^ ((bits >> 31) & jnp.int32(0x7FFFFFFF))


SEL_ROWS = 256
CNT_ROWS = 512


def _a_select_kernel(qblk_ref, kblk_ref, ph_ref, pl_ref, pkh_ref, pkl_ref, sel_ref,
                     ikx_ref, keys_ref, iqt_ref, jcut_ref, *, seq, k_sel):
    qi = pl.program_id(1)
    n_sel = (qi * BLOCK + BLOCK + SEL_ROWS - 1) // SEL_ROWS
    n_cnt = (qi * BLOCK + BLOCK + CNT_ROWS - 1) // CNT_ROWS

    @pl.when(qi == 0)
    def _():
        def prep(c, carry):
            off = pl.multiple_of(c * 512, 512)
            kv = kblk_ref[pl.ds(off, 512), :]
            hi = kv.astype(BF16)
            lo = (kv - hi.astype(F32)).astype(BF16)
            ikx_ref[pl.ds(off, 512), :] = (
                jnp.dot(hi, pkh_ref[...], preferred_element_type=F32)
                + jnp.dot(lo, pkl_ref[...], preferred_element_type=F32)).astype(BF16)
            return carry
        lax.fori_loop(0, seq // 512, prep, 0)

    qb = qblk_ref[...]
    iq = qb[:, :IDX_HEADS * IDX_DIM]
    iq_hi = iq.astype(BF16)
    iq_lo = (iq - iq_hi.astype(F32)).astype(BF16)
    iqx = (jnp.dot(iq_hi, ph_ref[...], preferred_element_type=F32)
           + jnp.dot(iq_lo, pl_ref[...], preferred_element_type=F32))
    for h in range(IDX_HEADS):
        iqt_ref[h // 2, :, (h % 2) * LANES:(h % 2 + 1) * LANES] = iqx[:, h * LANES:(h + 1) * LANES].T.astype(BF16)
    iw_t = qb[:, IDX_HEADS * IDX_DIM:IDX_HEADS * IDX_DIM + LANES].T
    iw_rows = [iw_t[IDX_DIM + h:IDX_DIM + h + 1, :] for h in range(IDX_HEADS)]
    q_pos = qi * BLOCK + lax.broadcasted_iota(jnp.int32, (SEL_ROWS, BLOCK), 1)
    k_row = lax.broadcasted_iota(jnp.int32, (SEL_ROWS, BLOCK), 0)

    def score_chunk(c, masked):
        off = pl.multiple_of(c * SEL_ROWS, SEL_ROWS)
        kx = ikx_ref[pl.ds(off, SEL_ROWS), :]
        sc = jnp.zeros((SEL_ROWS, BLOCK), F32)
        for hp in range(IDX_HEADS // 2):
            xx = jnp.dot(kx, iqt_ref[hp], preferred_element_type=F32)
            sc = (sc + jnp.maximum(xx[:, :LANES], 0.0) * iw_rows[2 * hp]
                  + jnp.maximum(xx[:, LANES:], 0.0) * iw_rows[2 * hp + 1])
        key = _sortable_key(sc + 0.0)
        if masked:
            key = jnp.where(off + k_row <= q_pos, key, INT_MIN)
        keys_ref[pl.ds(off, SEL_ROWS), :] = key

    def score_body(c, carry):
        score_chunk(c, False)
        return carry

    lax.fori_loop(0, n_sel - 1, score_body, 0)
    score_chunk(n_sel - 1, True)

    @pl.when(n_sel * SEL_ROWS < n_cnt * CNT_ROWS)
    def _():
        off = pl.multiple_of(n_sel * SEL_ROWS, SEL_ROWS)
        keys_ref[pl.ds(off, SEL_ROWS), :] = jnp.full((SEL_ROWS, BLOCK), INT_MIN, jnp.int32)

    def count_rows(pred):
        def cnt_chunk(c, acc):
            off = pl.multiple_of(c * CNT_ROWS, CNT_ROWS)
            hit = jnp.where(pred(keys_ref[pl.ds(off, CNT_ROWS), :], off), 1, 0)
            return acc + jnp.sum(hit.reshape(CNT_ROWS // 8, 8, BLOCK), axis=0)
        acc = lax.fori_loop(0, n_cnt, cnt_chunk, jnp.zeros((8, BLOCK), jnp.int32))
        return jnp.sum(acc, axis=0, keepdims=True)

    def count_ge(cand):
        return count_rows(lambda keys, off: keys >= cand)

    def bisect(it, lo):
        step = lax.shift_left(jnp.int32(1), jnp.int32(31) - it)
        cand = lo + step
        return jnp.where(count_ge(cand) >= k_sel, cand, lo)

    thr = lax.fori_loop(0, 32, bisect, jnp.full((1, BLOCK), INT_MIN, jnp.int32))
    has_k = thr > INT_MIN
    thr = jnp.maximum(thr, INT_MIN + 1)

    excess = jnp.where(has_k, count_ge(thr) - k_sel, 0)
    jcut_ref[...] = jnp.full(jcut_ref.shape, 2 ** 31 - 1, jnp.int32)

    @pl.when(jnp.max(excess) > 0)
    def _():
        need = jnp.where(excess > 0, k_sel - count_ge(thr + 1), 1)
        rev_row = (seq - 1) - lax.broadcasted_iota(jnp.int32, (CNT_ROWS, BLOCK), 0)

        def tie_bisect(it, lo):
            cand = lo + lax.shift_left(jnp.int32(1), jnp.int32(seq.bit_length() - 2) - it)
            cnt = count_rows(lambda keys, off: (keys == thr) & (rev_row - off >= cand))
            return jnp.where(cnt >= need, cand, lo)

        rev = lax.fori_loop(0, seq.bit_length() - 1, tie_bisect, jnp.zeros((1, BLOCK), jnp.int32))
        jcut = jnp.where(excess > 0, (seq - 1) - rev, 2 ** 31 - 1)
        jcut_ref[...] = jnp.broadcast_to(jcut, jcut_ref.shape)

    jcut = jcut_ref[0:1, :]

    def emit(c, carry):
        off = pl.multiple_of(c * SEL_ROWS, SEL_ROWS)
        keys = keys_ref[pl.ds(off, SEL_ROWS), :]
        bar = jnp.where(off + k_row > jcut, thr + 1, thr)
        add = jnp.where(keys >= bar, 0.0, NEG)
        sel_ref[:, pl.ds(off, SEL_ROWS)] = jnp.concatenate(
            [add[j * BLOCK:(j + 1) * BLOCK].T for j in range(SEL_ROWS // BLOCK)], axis=1).astype(BF16)
        return carry

    lax.fori_loop(0, n_sel, emit, 0)

    def fill(c, carry):
        off = pl.multiple_of(c * SEL_ROWS, SEL_ROWS)
        sel_ref[:, pl.ds(off, SEL_ROWS)] = jnp.full((BLOCK, SEL_ROWS), NEG, BF16)
        return carry

    lax.fori_loop(n_sel, seq // SEL_ROWS, fill, 0)


def _a_select(pf, batch, seq):
    t = pf.shape[0]
    nq = seq // BLOCK
    k_sel = min(TOPK_MAX, seq // 4)
    nlane = IDX_HEADS * IDX_DIM
    r = np.arange(nlane)
    ph = np.zeros((nlane, IDX_HEADS * LANES), np.float32)
    plo = np.zeros((nlane, IDX_HEADS * LANES), np.float32)
    ph[r, (r // IDX_DIM) * LANES + r % IDX_DIM] = 1
    ph[r, (r // IDX_DIM) * LANES + 2 * IDX_DIM + r % IDX_DIM] = 1
    plo[r, (r // IDX_DIM) * LANES + IDX_DIM + r % IDX_DIM] = 1
    d = np.arange(IDX_DIM)
    pkh = np.zeros((LANES, LANES), np.float32)
    pkl = np.zeros((LANES, LANES), np.float32)
    pkh[d, d] = 1
    pkh[d, IDX_DIM + d] = 1
    pkl[d, 2 * IDX_DIM + d] = 1
    full = lambda shape: pl.BlockSpec(shape, lambda b, i: (0,) * len(shape))
    return pl.pallas_call(
        functools.partial(_a_select_kernel, seq=seq, k_sel=k_sel),
        grid=(batch, nq),
        in_specs=[pl.BlockSpec((BLOCK, 512), lambda b, i: (b * nq + i, 0)),
                  pl.BlockSpec((seq, LANES), lambda b, i: (b, nlane // LANES)),
                  full(ph.shape), full(plo.shape), full(pkh.shape), full(pkl.shape)],
        out_specs=pl.BlockSpec((BLOCK, seq), lambda b, i: (b * nq + i, 0)),
        out_shape=jax.ShapeDtypeStruct((t, seq), BF16),
        scratch_shapes=[pltpu.VMEM((seq, LANES), BF16), pltpu.VMEM((seq, LANES), jnp.int32),
                        pltpu.VMEM((IDX_HEADS // 2, LANES, 2 * LANES), BF16),
                        pltpu.VMEM((8, LANES), jnp.int32)],
        compiler_params=_cparams(("parallel", "arbitrary")),
        name="a_select",
    )(pf, pf, jnp.asarray(ph, BF16), jnp.asarray(plo, BF16), jnp.asarray(pkh, BF16),
      jnp.asarray(pkl, BF16))


BAND_RBLK = 4


def _band_kernel(*refs, has_sink, want_lse, kv_div, rblk):
    if has_sink:
        sink_ref, refs = refs[0], refs[1:]
    q_ref, hm_ref, kp_ref, kc_ref, vp_ref, vc_ref, bias_ref, o_ref = refs[:8]
    first = pl.program_id(1) == 0
    lane = lax.broadcasted_iota(jnp.int32, (BLOCK, LANES), 1)
    col = lax.broadcasted_iota(jnp.int32, (BLOCK, 2 * BLOCK), 1)
    no_prev = jnp.logical_and(first, col < BLOCK)
    for p in range(BRANCH_WIDTH // LANES):
        kcol = (p // kv_div) * LANES
        kk = jnp.concatenate([kp_ref[:, kcol:kcol + LANES], kc_ref[:, kcol:kcol + LANES]], axis=0)
        vv = jnp.concatenate([vp_ref[:, kcol:kcol + LANES], vc_ref[:, kcol:kcol + LANES]], axis=0)
        for i in range(rblk):
            rows = slice(i * BLOCK, (i + 1) * BLOCK)
            q = q_ref[rows, p * LANES:(p + 1) * LANES]
            kblk = kk[i * BLOCK:(i + 2) * BLOCK]
            vblk = vv[i * BLOCK:(i + 2) * BLOCK]
            outs, lses = [], []
            for e in range(2):
                s = lax.dot_general(q * hm_ref[e], kblk, _CONTRACT_LANES, preferred_element_type=F32)
                s = s + bias_ref[2 * p + e]
                if i == 0:
                    s = jnp.where(no_prev, NEG, s)
                m = jnp.max(s, axis=1, keepdims=True)
                if has_sink:
                    sk = sink_ref[2 * p + e]
                    m = jnp.maximum(m, sk)
                pr = jnp.exp(s - m)
                den = jnp.sum(pr, axis=1, keepdims=True)
                if has_sink:
                    den = den + jnp.exp(sk - m)
                outs.append(jnp.dot(pr.astype(BF16), vblk, preferred_element_type=F32) / den)
                lses.append(m + jnp.log(den))
            o_ref[rows, p * LANES:(p + 1) * LANES] = jnp.where(lane < HEAD_DIM, outs[0], outs[1]).astype(o_ref.dtype)
            if want_lse:
                refs[8][rows, p * LANES:(p + 1) * LANES] = jnp.where(lane < HEAD_DIM, lses[0], lses[1])


def _band_attn(q_arr, k_arr, v_arr, bias, nsub, sub_len, qcb, kcb, vcb, kv_w, sinks=None, want_lse=False):
    t = q_arr.shape[0]
    rblk = min(BAND_RBLK, sub_len // BLOCK)
    step_rows = rblk * BLOCK
    nbs = sub_len // step_rows
    kv_div = BRANCH_WIDTH // kv_w

    def cur(u, n):
        return u * nbs + n

    def prev(u, n):
        return jnp.where(n == 0, u * nbs * rblk, (u * nbs + n) * rblk - 1)

    in_specs = [
        pl.BlockSpec((step_rows, BRANCH_WIDTH), lambda u, n: (cur(u, n), qcb)),
        pl.BlockSpec((2, 1, LANES), lambda u, n: (0, 0, 0)),
        pl.BlockSpec((BLOCK, kv_w), lambda u, n: (prev(u, n), kcb)),
        pl.BlockSpec((step_rows, kv_w), lambda u, n: (cur(u, n), kcb)),
        pl.BlockSpec((BLOCK, kv_w), lambda u, n: (prev(u, n), vcb)),
        pl.BlockSpec((step_rows, kv_w), lambda u, n: (cur(u, n), vcb)),
        pl.BlockSpec(bias.shape, lambda u, n: (0, 0, 0)),
    ]
    args = [q_arr, _pair_masks(), k_arr, k_arr, v_arr, v_arr, bias]
    if sinks is not None:
        in_specs = [pl.BlockSpec(memory_space=pltpu.SMEM)] + in_specs
        args = [sinks] + args
    o_spec = pl.BlockSpec((step_rows, BRANCH_WIDTH), lambda u, n: (cur(u, n), 0))
    out_specs = [o_spec]
    out_shape = [jax.ShapeDtypeStruct((t, BRANCH_WIDTH), BF16)]
    if want_lse:
        out_specs.append(o_spec)
        out_shape.append(jax.ShapeDtypeStruct((t, BRANCH_WIDTH), F32))
    return pl.pallas_call(
        functools.partial(_band_kernel, has_sink=sinks is not None, want_lse=want_lse, kv_div=kv_div, rblk=rblk),
        grid=(nsub, nbs),
        in_specs=in_specs,
        out_specs=out_specs,
        out_shape=out_shape,
        compiler_params=_cparams(("parallel", "arbitrary")),
        name="band_attn",
    )(*args)


def _band_bias(table, step, max_dist):
    rel = BLOCK + np.arange(BLOCK)[:, None] - np.arange(2 * BLOCK)[None, :]
    bias = table[_t5_bucket(jnp.asarray(rel * step))].transpose(2, 0, 1).astype(F32)
    ok = (rel >= 0) & (rel <= max_dist)
    return jnp.where(jnp.asarray(ok)[None], bias, NEG)


def _merge_kernel(x_ref, ya_ref, yb0_ref, yb1_ref, yb2_ref, l0_ref, l1_ref, l2_ref, yc_ref, yd_ref,
                  za_ref, zb_ref, zc_ref, zd_ref, ga_ref, gb_ref, gc_ref, gd_ref, wb_ref, wo_ref, o_ref,
                  y1_ref, y2_ref, s1_ref, s2_ref):
    ncol = BRANCH_WIDTH // LANES
    for src, dst in ((yb1_ref, y1_ref), (yb2_ref, y2_ref), (l1_ref, s1_ref), (l2_ref, s2_ref)):
        dil = src.shape[1]
        for r in range(dil):
            blk = src[0, r].astype(F32)
            for c in range(ncol):
                dst[c, pl.ds(r, src.shape[2], stride=dil), :] = blk[:, c * LANES:(c + 1) * LANES]
    wide = lambda ref: jnp.concatenate([ref[c] for c in range(ncol)], axis=1)
    l0, l1, l2 = l0_ref[0, 0], wide(s1_ref), wide(s2_ref)
    mx = jnp.maximum(jnp.maximum(l0, l1), l2)
    w0, w1, w2 = jnp.exp(l0 - mx), jnp.exp(l1 - mx), jnp.exp(l2 - mx)
    yb = (w0 * yb0_ref[0, 0].astype(F32) + w1 * wide(y1_ref) + w2 * wide(y2_ref)) / (w0 + w1 + w2)
    ys = (ya_ref[...].astype(F32), yb, yc_ref[...].astype(F32), yd_ref[...].astype(F32))
    zs = (za_ref, zb_ref, zc_ref, zd_ref)
    gs = (ga_ref, gb_ref, gc_ref, gd_ref)
    merged = jnp.zeros(o_ref.shape, F32)
    for n in range(N_BRANCH):
        z = zs[n][...].astype(F32)
        u = (ys[n] * (z * jax.nn.sigmoid(z))).astype(BF16)
        proj = jnp.dot(u, wb_ref[n], preferred_element_type=F32)
        merged = merged + jax.nn.sigmoid(gs[n][...].astype(F32)) * proj
    o_ref[...] = x_ref[...] + jnp.dot(merged.astype(BF16), wo_ref[...], preferred_element_type=F32)


def _merge(x2, ya, yb, lb, yc, yd, pp, wb, wo, batch, seq, tm=512):
    t = x2.shape[0]
    bw = BRANCH_WIDTH
    nsb = seq // tm
    row = lambda w, c: pl.BlockSpec((tm, w), lambda i: (i, c))

    def sub(dil):
        return pl.BlockSpec((1, dil, tm // dil, bw), lambda i: (i // nsb, 0, i % nsb, 0))

    dils = [dil for _, dil in B_GROUPS]
    yb = [a.reshape(batch, dil, seq // dil, bw) for a, dil in zip(yb, dils)]
    lb = [a.reshape(batch, dil, seq // dil, bw) for a, dil in zip(lb, dils)]
    in_specs = ([row(D_MODEL, 0), row(bw, 0)] + [sub(dil) for dil in dils] * 2 + [row(bw, 0), row(bw, 0)]
                + [row(bw, PP_AZ // bw), row(bw, PP_BZ // bw), row(bw, PP_CZ // bw), row(bw, PP_DZ // bw)]
                + [row(D_MODEL, PP_G // D_MODEL + n) for n in range(N_BRANCH)]
                + [pl.BlockSpec((N_BRANCH, bw, D_MODEL), lambda i: (0, 0, 0)),
                   pl.BlockSpec((D_MODEL, D_MODEL), lambda i: (0, 0))])
    return pl.pallas_call(
        _merge_kernel,
        grid=(t // tm,),
        in_specs=in_specs,
        out_specs=row(D_MODEL, 0),
        out_shape=jax.ShapeDtypeStruct((t, D_MODEL), F32),
        scratch_shapes=[pltpu.VMEM((bw // LANES, tm, LANES), F32)] * 4,
        compiler_params=_cparams(("parallel",)),
        name="merge",
    )(x2, ya, yb[0], yb[1], yb[2], lb[0], lb[1], lb[2], yc, yd, pp, pp, pp, pp, pp, pp, pp, pp, wb, wo)


def _layer_weights(w_in, qk_gain_a, qk_gain_b, qk_gain_d):
    def cols(start, width):
        return w_in[:, :, start:start + width]

    n_bq = len(B_GROUPS) * B_HEADS * HEAD_DIM
    dk = cols(_D0 + 512, 128)
    dv = cols(_D0 + 640, 128)
    dup = lambda a: jnp.concatenate([a[:, :, :64], a[:, :, :64], a[:, :, 64:], a[:, :, 64:]], axis=-1)
    w_pn = jnp.concatenate([cols(_A0, 512), cols(_A0 + 512, 512), cols(_D0, 512), dup(dk), dup(dv)],
                           axis=-1).astype(BF16)
    w_pp = jnp.concatenate([cols(_G0, N_BRANCH * D_MODEL), cols(_A0 + 1024, 512), cols(_A0 + 1536, 512),
                            cols(_B0 + 3 * n_bq, 512), cols(_C0 + 416, 512), cols(_D0 + 768, 512)],
                           axis=-1).astype(BF16)
    w_bg = [jnp.concatenate([cols(_B0 + g * 512, 512), cols(_B0 + n_bq + g * 512, 512),
                             cols(_B0 + 2 * n_bq + g * 512, 512)], axis=-1).astype(BF16)
            for g in range(len(B_GROUPS))]
    depth = w_in.shape[0]
    zeros = lambda w: jnp.zeros((depth, D_MODEL, w), w_in.dtype)
    w_pf = jnp.concatenate([cols(_A0 + 2048, 296), zeros(216), cols(_C0, 416), zeros(96)], axis=-1).astype(BF16)

    scale = HEAD_DIM ** -0.5
    tile = lambda g, reps: jnp.tile(g, (1, reps))
    hg = jnp.concatenate([tile(qk_gain_a[:, 0] * scale, 8), tile(qk_gain_a[:, 1], 8),
                          tile(qk_gain_d[:, 0] * scale, 8), tile(qk_gain_d[:, 1], 4),
                          jnp.ones((depth, 256), F32)], axis=-1)
    flag = jnp.concatenate([jnp.ones((PN_DV,), F32), jnp.zeros((PN_W - PN_DV,), F32)])
    hg_b = jnp.concatenate([tile(qk_gain_b[:, 0] * scale, 8), tile(qk_gain_b[:, 1], 8),
                            jnp.ones((depth, 512), F32)], axis=-1)
    flag_b = jnp.concatenate([jnp.ones((2 * BRANCH_WIDTH,), F32), jnp.zeros((BRANCH_WIDTH,), F32)])
    return w_pn, w_pp, w_bg, w_pf, hg, flag, hg_b, flag_b


def _c_weights(qk_gain_c, w_q_b, w_kv_b, seq):
    src_q, src_k, gain_idx, head, pe_src, rope_j, rope_half = _c_layout_tables()
    take = lambda w, src: jnp.where(jnp.asarray(src >= 0), jnp.take(w, jnp.asarray(np.maximum(src, 0)), axis=-1), 0.0)
    wq = take(w_q_b, src_q).astype(BF16)
    wk = take(w_kv_b, src_k).astype(BF16)
    v_src = np.array([h * (C_NOPE + C_V) + C_NOPE + c for h in range(C_HEADS) for c in range(C_V)])
    wv = jnp.take(w_kv_b, jnp.asarray(v_src), axis=-1).astype(BF16)
    ppe = np.zeros((LANES, C_QK_W), np.float32)
    ln = np.nonzero(pe_src >= 0)[0]
    ppe[pe_src[ln], ln] = 1
    grp = ((head[:, None] == head[None, :]) & (head[:, None] >= 0)).astype(np.float32)
    g_take = lambda g: jnp.where(jnp.asarray(gain_idx >= 0), jnp.take(g, jnp.asarray(np.maximum(gain_idx, 0)), axis=-1), 0.0)
    gq = g_take(qk_gain_c[:, 0]) * (C_NOPE + C_ROPE) ** -0.5
    gk = g_take(qk_gain_c[:, 1])
    freq = ROPE_THETA ** (-jnp.arange(C_ROPE_HALF, dtype=F32) / C_ROPE_HALF)
    ang = jnp.arange(seq).astype(F32)[:, None] * freq[None, :]
    cos_j, sin_j = jnp.cos(ang), jnp.sin(ang)
    is_rope = jnp.asarray(rope_j >= 0)
    jj = jnp.asarray(np.maximum(rope_j, 0))
    cos = jnp.where(is_rope, cos_j[:, jj], 1.0)
    sin = jnp.where(is_rope, sin_j[:, jj], 0.0)
    s1 = jnp.where(jnp.asarray(rope_half == 0), -sin, 0.0)
    s2 = jnp.where(jnp.asarray(rope_half == 1), sin, 0.0)
    return wq, wk, wv, jnp.asarray(ppe, BF16), jnp.asarray(grp, BF16), gq, gk, cos, s1, s2


def _a_bias_tiles(table):
    a = np.arange(BLOCK)
    dist = (np.arange(A_BIAS_TILES)[:, None, None] * BLOCK + a[None, :, None] - a[None, None, :])
    return table[_t5_bucket(jnp.asarray(dist))].transpose(3, 0, 1, 2).astype(F32)


def kernel(x, norm_gain, w_in, qk_gain_a, qk_gain_b, qk_gain_c, qk_gain_d, c_q_gain, c_kv_gain,
           w_q_b, w_kv_b, sinks, rel_bias, w_branch, w_out):
    batch, seq, d_model = x.shape
    depth = w_in.shape[0]
    t = batch * seq
    x2 = x.reshape(t, d_model)

    w_pn, w_pp, w_bg, w_pf, hg, flag, hg_b, flag_b = _layer_weights(w_in, qk_gain_a, qk_gain_b, qk_gain_d)
    wq, wk, wv, ppe, grp, gq, gk, cos, s1, s2 = _c_weights(qk_gain_c, w_q_b, w_kv_b, seq)
    wb = w_branch.astype(BF16)
    wo = w_out.astype(BF16)

    bias_a = _a_bias_tiles(rel_bias[:, :A_HEADS])
    bias_b = [_band_bias(rel_bias[:, A_HEADS + g * B_HEADS:A_HEADS + (g + 1) * B_HEADS], dil, window // dil)
              for g, (window, dil) in enumerate(B_GROUPS)]
    bias_d = _band_bias(rel_bias[:, N_BIAS_HEADS - D_HEADS:], 1, D_WINDOW - 1)

    for l in range(depth):
        pn = _proj(x2, norm_gain[l], w_pn[l], BF16, head_gain=hg[l], flag=flag)
        pp = _proj(x2, norm_gain[l], w_pp[l], BF16)
        pf = _proj(x2, norm_gain[l], w_pf[l], F32)

        sel = _a_select(pf, batch, seq)
        ya = _a_attn(pn, pp, sel, bias_a, batch, seq)

        yb, lb = [], []
        for g, (window, dil) in enumerate(B_GROUPS):
            bg = _proj(x2, norm_gain[l], w_bg[g][l], BF16, head_gain=hg_b[l], flag=flag_b, dil=dil, seq=seq)
            o, lse = _band_attn(bg, bg, bg, bias_b[g], batch * dil, seq // dil, 0, 1, 2, BRANCH_WIDTH,
                                want_lse=True)
            yb.append(o)
            lb.append(lse)

        qc, kc, vc = _c_prep(pf, c_q_gain[l].reshape(1, -1), c_kv_gain[l].reshape(1, -1), wq[l], wk[l], wv[l],
                             ppe, grp, gq[l].reshape(1, -1), gk[l].reshape(1, -1), cos, s1, s2, seq)
        yc = _c_attn(qc, kc, vc, batch, seq)

        (yd,) = _band_attn(pn, pn, pn, bias_d, batch, seq, PN_DQ // BRANCH_WIDTH, PN_DK // 256, PN_DV // 256,
                           256, sinks=sinks[l])

        x2 = _merge(x2, ya, yb, lb, yc, yd, pp, wb[l], wo[l], batch, seq)
    return x2.reshape(batch, seq, d_model)
```

```python
import functools
import math

import numpy as np
import jax
import jax.numpy as jnp
from jax import lax
from jax.experimental import pallas as pl
from jax.experimental.pallas import tpu as pltpu

F32 = jnp.float32
BF16 = jnp.bfloat16

D_MODEL = 1024
BLOCK = 128
HEAD_DIM = 64
N_BRANCH = 4
BRANCH_WIDTH = 512
EPS = 1e-6
A_HEADS = 8
IDX_HEADS = 8
IDX_DIM = 32
TOPK_MAX = 256
B_GROUPS = ((128, 1), (512, 4), (2048, 16))
B_HEADS = 8
C_HEADS = 8
C_NOPE = 64
C_ROPE = 32
C_V = 64
C_Q_LORA = 256
C_KV_LORA = 128
ROPE_THETA = 10000.0
D_HEADS = 8
D_KV_HEADS = 2
D_WINDOW = 128
NUM_BUCKETS = 32
MAX_DISTANCE = 2048
N_BIAS_HEADS = A_HEADS + len(B_GROUPS) * B_HEADS + D_HEADS

LANES = 128
NEG = -1e30
INT_MIN = -(2 ** 31)

_A0 = 0
_B0 = 2344
_C0 = 7464
_D0 = 8392
_G0 = 9672

PN_AQ, PN_AK, PN_DQ, PN_DK, PN_DV, PN_W = 0, 512, 1024, 1536, 1792, 2048
PP_G, PP_AV, PP_AZ, PP_BZ, PP_CZ, PP_DZ, PP_W = 0, 4096, 4608, 5120, 5632, 6144, 6656
PF_W = 1024
BG_W = 3 * BRANCH_WIDTH

A_BIAS_TILES = MAX_DISTANCE // BLOCK + 2


def _cparams(sem, vmem_mb=48):
    return pltpu.CompilerParams(dimension_semantics=sem, vmem_limit_bytes=vmem_mb * 1024 * 1024)


def _t5_bucket(dist):
    max_exact = NUM_BUCKETS // 2
    d = jnp.maximum(dist, 0)
    logd = jnp.log(jnp.maximum(d, 1).astype(F32) / max_exact)
    large = max_exact + (logd / math.log(MAX_DISTANCE / max_exact) * (NUM_BUCKETS - max_exact)).astype(jnp.int32)
    return jnp.where(d < max_exact, d, jnp.minimum(large, NUM_BUCKETS - 1))


def _proj_kernel(x_ref, g_ref, w_ref, *rest, norm, dil):
    rest = list(rest)
    strided = dil is not None and dil > 1
    xn_ref = rest.pop(-2) if strided else rest.pop(-1)
    res_ref = rest.pop(-1) if strided else None
    if norm:
        hg_ref, flag_ref, bd_ref, o_ref = rest
    else:
        (o_ref,) = rest

    @pl.when(pl.program_id(1) == 0)
    def _():
        x = x_ref[...]
        ms = jnp.mean(x * x, axis=-1, keepdims=True)
        xn_ref[...] = (x * lax.rsqrt(ms + EPS) * g_ref[...]).astype(BF16)

    h = jnp.dot(xn_ref[...], w_ref[...], preferred_element_type=F32)
    if norm:
        ss = jnp.dot((h * h).astype(BF16), bd_ref[...], preferred_element_type=F32)
        scale = lax.rsqrt(ss * (1.0 / HEAD_DIM) + EPS) * hg_ref[...]
        h = h * jnp.where(flag_ref[...] > 0, scale, 1.0)
    if dil is None:
        o_ref[...] = h.astype(o_ref.dtype)
    elif dil == 1:
        o_ref[0, 0] = h.astype(o_ref.dtype)
    else:
        sub = h.shape[0] // dil
        for c in range(h.shape[1] // LANES):
            res_ref[c] = h[:, c * LANES:(c + 1) * LANES]
        for r in range(dil):
            o_ref[0, r] = jnp.concatenate(
                [res_ref[c, pl.ds(r, sub, stride=dil), :] for c in range(h.shape[1] // LANES)],
                axis=1).astype(o_ref.dtype)


def _proj(x2, gain, w, out_dtype, head_gain=None, flag=None, dil=None, seq=None, tm=1024, tn=512):
    t, d = x2.shape
    n = w.shape[1]
    norm = head_gain is not None
    scratch = [pltpu.VMEM((tm, d), BF16)]
    if dil is None:
        out_spec = pl.BlockSpec((tm, tn), lambda i, j: (i, j))
        out_shape = jax.ShapeDtypeStruct((t, n), out_dtype)
        kdil = None
    else:
        nsb = seq // tm
        out_spec = pl.BlockSpec((1, dil, tm // dil, tn), lambda i, j: (i // nsb, 0, i % nsb, j))
        out_shape = jax.ShapeDtypeStruct((t // seq, dil, seq // dil, n), out_dtype)
        kdil = dil
        if dil > 1:
            scratch.append(pltpu.VMEM((tn // LANES, tm, LANES), F32))
    in_specs = [pl.BlockSpec((tm, d), lambda i, j: (i, 0)),
                pl.BlockSpec((1, d), lambda i, j: (0, 0)),
                pl.BlockSpec((d, tn), lambda i, j: (0, j))]
    args = [x2, gain.reshape(1, d), w]
    if norm:
        lane = np.arange(tn)
        bd = jnp.asarray((lane[:, None] // HEAD_DIM == lane[None, :] // HEAD_DIM), BF16)
        in_specs += [pl.BlockSpec((1, tn), lambda i, j: (0, j)),
                     pl.BlockSpec((1, tn), lambda i, j: (0, j)),
                     pl.BlockSpec((tn, tn), lambda i, j: (0, 0))]
        args += [head_gain.reshape(1, n), flag.reshape(1, n), bd]
    out = pl.pallas_call(
        functools.partial(_proj_kernel, norm=norm, dil=kdil),
        grid=(t // tm, n // tn),
        in_specs=in_specs,
        out_specs=out_spec,
        out_shape=out_shape,
        scratch_shapes=scratch,
        compiler_params=_cparams(("parallel", "arbitrary")),
        name=("proj_norm" if norm else "proj") + ("" if dil is None else "_dil%d" % dil),
    )(*args)
    return out.reshape(t, n)


C_PAIR = 256
C_QK_W = (C_HEADS // 2) * C_PAIR
C_ROPE_HALF = C_ROPE // 2


def _c_lane(h, c):
    base = (h // 2) * C_PAIR
    e = h % 2
    if c < C_NOPE:
        return base + e * C_NOPE + c
    return base + 2 * C_NOPE + e * C_ROPE + (c - C_NOPE)


def _c_layout_tables():
    src_q = np.full(C_QK_W, -1, np.int64)
    src_k = np.full(C_QK_W, -1, np.int64)
    gain_idx = np.full(C_QK_W, -1, np.int64)
    head = np.full(C_QK_W, -1, np.int64)
    pe_src = np.full(C_QK_W, -1, np.int64)
    rope_j = np.full(C_QK_W, -1, np.int64)
    rope_half = np.zeros(C_QK_W, np.int64)
    for h in range(C_HEADS):
        for c in range(C_NOPE + C_ROPE):
            ln = _c_lane(h, c)
            src_q[ln] = h * (C_NOPE + C_ROPE) + c
            gain_idx[ln] = c
            head[ln] = h
            if c < C_NOPE:
                src_k[ln] = h * (C_NOPE + C_V) + c
            else:
                r = c - C_NOPE
                pe_src[ln] = r
                rope_j[ln] = r % C_ROPE_HALF
                rope_half[ln] = r // C_ROPE_HALF
    return src_q, src_k, gain_idx, head, pe_src, rope_j, rope_half


def _c_prep_kernel(pf_ref, cqg_ref, ckvg_ref, wq_ref, wk_ref, wv_ref, ppe_ref, grp_ref,
                   gq_ref, gk_ref, cos_ref, s1_ref, s2_ref, q_ref, k_ref, v_ref):
    blk = pf_ref[...]
    cq = blk[:, :C_Q_LORA]
    ckv = blk[:, C_Q_LORA:C_Q_LORA + C_KV_LORA]
    pe = blk[:, C_Q_LORA + C_KV_LORA:]

    def rms(v, g):
        return v * lax.rsqrt(jnp.mean(v * v, axis=-1, keepdims=True) + EPS) * g

    def head_norm_rope(raw, gain):
        ss = jnp.dot((raw * raw).astype(BF16), grp_ref[...], preferred_element_type=F32)
        y = raw * lax.rsqrt(ss * (1.0 / (C_NOPE + C_ROPE)) + EPS) * gain
        up = pltpu.roll(y, C_QK_W - C_ROPE_HALF, 1)
        dn = pltpu.roll(y, C_ROPE_HALF, 1)
        return y * cos_ref[...] + up * s1_ref[...] + dn * s2_ref[...]

    cqn = rms(cq, cqg_ref[...]).astype(BF16)
    q_raw = jnp.dot(cqn, wq_ref[...], preferred_element_type=F32)
    q_ref[...] = head_norm_rope(q_raw, gq_ref[...]).astype(BF16)

    ckvn = rms(ckv, ckvg_ref[...]).astype(BF16)
    pe_hi = pe.astype(BF16)
    pe_lo = (pe - pe_hi.astype(F32)).astype(BF16)
    k_raw = (jnp.dot(ckvn, wk_ref[...], preferred_element_type=F32)
             + jnp.dot(pe_hi, ppe_ref[...], preferred_element_type=F32)
             + jnp.dot(pe_lo, ppe_ref[...], preferred_element_type=F32))
    k_ref[...] = head_norm_rope(k_raw, gk_ref[...]).astype(BF16)
    v_ref[...] = jnp.dot(ckvn, wv_ref[...], preferred_element_type=F32).astype(BF16)


def _c_prep(pf, cqg, ckvg, wq, wk, wv, ppe, grp, gq, gk, cos, s1, s2, seq, tm=512):
    t = pf.shape[0]
    nsb = seq // tm
    full = lambda shape: pl.BlockSpec(shape, lambda i: (0,) * len(shape))
    tab = pl.BlockSpec((tm, C_QK_W), lambda i: (i % nsb, 0))
    return pl.pallas_call(
        _c_prep_kernel,
        grid=(t // tm,),
        in_specs=[pl.BlockSpec((tm, 512), lambda i: (i, 1)),
                  full((1, C_Q_LORA)), full((1, C_KV_LORA)),
                  full((C_Q_LORA, C_QK_W)), full((C_KV_LORA, C_QK_W)), full((C_KV_LORA, BRANCH_WIDTH)),
                  full((LANES, C_QK_W)), full((C_QK_W, C_QK_W)),
                  full((1, C_QK_W)), full((1, C_QK_W)), tab, tab, tab],
        out_specs=[pl.BlockSpec((tm, C_QK_W), lambda i: (i, 0)),
                   pl.BlockSpec((tm, C_QK_W), lambda i: (i, 0)),
                   pl.BlockSpec((tm, BRANCH_WIDTH), lambda i: (i, 0))],
        out_shape=[jax.ShapeDtypeStruct((t, C_QK_W), BF16),
                   jax.ShapeDtypeStruct((t, C_QK_W), BF16),
                   jax.ShapeDtypeStruct((t, BRANCH_WIDTH), BF16)],
        compiler_params=_cparams(("parallel",)),
        name="c_prep",
    )(pf, cqg, ckvg, wq, wk, wv, ppe, grp, gq, gk, cos, s1, s2)


def _flash_update(e, s, v_ones, m_ref, acc_ref):
    m_prev = m_ref[e]
    m_new = jnp.maximum(m_prev, jnp.max(s, axis=1, keepdims=True))
    alpha = jnp.exp(m_prev - m_new)
    p = jnp.exp(s - jnp.concatenate([m_new] * (s.shape[1] // LANES), axis=1))
    acc_ref[e] = (jnp.concatenate([alpha, alpha], axis=1) * acc_ref[e]
                  + jnp.dot(p.astype(BF16), v_ones, preferred_element_type=F32))
    m_ref[e] = m_new


def _flash_init(m_ref, acc_ref):
    m_ref[...] = jnp.full(m_ref.shape, NEG, F32)
    acc_ref[...] = jnp.zeros(acc_ref.shape, F32)


def _flash_finish(o_ref, acc_ref):
    lane = lax.broadcasted_iota(jnp.int32, o_ref.shape, 1)
    o0 = acc_ref[0, :, :LANES] / acc_ref[0, :, LANES:]
    o1 = acc_ref[1, :, :LANES] / acc_ref[1, :, LANES:]
    o_ref[...] = jnp.where(lane < HEAD_DIM, o0, o1).astype(o_ref.dtype)


def _with_ones(v_chunk):
    return jnp.concatenate([v_chunk, jnp.ones(v_chunk.shape, v_chunk.dtype)], axis=1)


_CONTRACT_LANES = (((1,), (1,)), ((), ()))


def _head_masks(width, ranges0, ranges1):
    m = np.zeros((2, 1, width), np.float32)
    for e, ranges in enumerate((ranges0, ranges1)):
        for lo, hi in ranges:
            m[e, 0, lo:hi] = 1
    return jnp.asarray(m, BF16)


def _pair_masks():
    return _head_masks(LANES, [(0, HEAD_DIM)], [(HEAD_DIM, LANES)])


def _transpose_keys(k_ref, kt_ref, rows=512):
    def body(c, carry):
        off = pl.multiple_of(c * rows, rows)
        kt_ref[:, pl.ds(off, rows)] = k_ref[pl.ds(off, rows), :].astype(F32).T.astype(kt_ref.dtype)
        return carry
    lax.fori_loop(0, k_ref.shape[0] // rows, body, 0)


def _c_attn_kernel(q_ref, hm_ref, k_ref, v_ref, o_ref, kt_ref, m_ref, acc_ref, *, tq, tk):
    qi = pl.program_id(2)

    @pl.when(qi == 0)
    def _():
        _transpose_keys(k_ref, kt_ref)

    q = q_ref[...]
    qs = (q * hm_ref[0], q * hm_ref[1])
    _flash_init(m_ref, acc_ref)
    n_full = (qi * tq) // tk

    def chunk(c, masked):
        off = pl.multiple_of(c * tk, tk)
        kc = kt_ref[:, pl.ds(off, tk)]
        vc = _with_ones(v_ref[pl.ds(off, tk), :])
        for e in range(2):
            s = jnp.dot(qs[e], kc, preferred_element_type=F32)
            if masked:
                row = qi * tq + lax.broadcasted_iota(jnp.int32, s.shape, 0)
                col = c * tk + lax.broadcasted_iota(jnp.int32, s.shape, 1)
                s = jnp.where(col <= row, s, NEG)
            _flash_update(e, s, vc, m_ref, acc_ref)

    def body(c, carry):
        chunk(c, False)
        return carry

    lax.fori_loop(0, n_full, body, 0)
    chunk(n_full, True)
    _flash_finish(o_ref, acc_ref)


def _c_attn(qc, kc, vc, batch, seq, tq=512, tk=512):
    t = qc.shape[0]
    nq = seq // tq
    npair = C_HEADS // 2
    return pl.pallas_call(
        functools.partial(_c_attn_kernel, tq=tq, tk=tk),
        grid=(batch, npair, nq),
        in_specs=[pl.BlockSpec((tq, C_PAIR), lambda b, p, i: (b * nq + i, p)),
                  pl.BlockSpec((2, 1, C_PAIR), lambda b, p, i: (0, 0, 0)),
                  pl.BlockSpec((seq, C_PAIR), lambda b, p, i: (b, p)),
                  pl.BlockSpec((seq, LANES), lambda b, p, i: (b, p))],
        out_specs=pl.BlockSpec((tq, LANES), lambda b, p, i: (b * nq + i, p)),
        out_shape=jax.ShapeDtypeStruct((t, BRANCH_WIDTH), BF16),
        scratch_shapes=[pltpu.VMEM((C_PAIR, seq), BF16),
                        pltpu.VMEM((2, tq, LANES), F32), pltpu.VMEM((2, tq, 2 * LANES), F32)],
        compiler_params=_cparams(("parallel", "parallel", "arbitrary")),
        name="c_attn",
    )(qc, _head_masks(C_PAIR, [(0, C_NOPE), (2 * C_NOPE, 2 * C_NOPE + C_ROPE)],
                      [(C_NOPE, 2 * C_NOPE), (2 * C_NOPE + C_ROPE, 2 * C_NOPE + 2 * C_ROPE)]), kc, vc)


def _a_attn_kernel(q_ref, hm_ref, k_ref, v_ref, sel_ref, bias_ref, o_ref, kt_ref, m_ref, acc_ref,
                   *, tq, tk):
    qi = pl.program_id(2)

    @pl.when(qi == 0)
    def _():
        _transpose_keys(k_ref, kt_ref)

    q = q_ref[...]
    qs = (q * hm_ref[0], q * hm_ref[1])
    _flash_init(m_ref, acc_ref)
    n_chunks = ((qi + 1) * tq + tk - 1) // tk

    def body(c, carry):
        off = pl.multiple_of(c * tk, tk)
        kc = kt_ref[:, pl.ds(off, tk)]
        vc = _with_ones(v_ref[pl.ds(off, tk), :])
        selmask = sel_ref[:, pl.ds(off, tk)].astype(F32)
        for e in range(2):
            s = jnp.dot(qs[e], kc, preferred_element_type=F32)
            rows = []
            for i in range(tq // BLOCK):
                tiles = []
                for j in range(tk // BLOCK):
                    d = (qi * (tq // BLOCK) + i) - (c * (tk // BLOCK) + j)
                    d = jnp.clip(d, 0, A_BIAS_TILES - 1)
                    tiles.append(bias_ref[e, d])
                rows.append(jnp.concatenate(tiles, axis=1))
            s = s + jnp.concatenate(rows, axis=0) + selmask
            _flash_update(e, s, vc, m_ref, acc_ref)
        return carry

    lax.fori_loop(0, n_chunks, body, 0)
    _flash_finish(o_ref, acc_ref)


def _a_attn(pn, pp, sel, bias_tiles, batch, seq, tq=512, tk=1024):
    t = pn.shape[0]
    nq = seq // tq
    npair = A_HEADS // 2
    qcol, kcol, vcol = PN_AQ // LANES, PN_AK // LANES, PP_AV // LANES
    return pl.pallas_call(
        functools.partial(_a_attn_kernel, tq=tq, tk=tk),
        grid=(batch, npair, nq),
        in_specs=[pl.BlockSpec((tq, LANES), lambda b, p, i: (b * nq + i, qcol + p)),
                  pl.BlockSpec((2, 1, LANES), lambda b, p, i: (0, 0, 0)),
                  pl.BlockSpec((seq, LANES), lambda b, p, i: (b, kcol + p)),
                  pl.BlockSpec((seq, LANES), lambda b, p, i: (b, vcol + p)),
                  pl.BlockSpec((tq, seq), lambda b, p, i: (b * nq + i, 0)),
                  pl.BlockSpec((2, A_BIAS_TILES, BLOCK, BLOCK), lambda b, p, i: (p, 0, 0, 0))],
        out_specs=pl.BlockSpec((tq, LANES), lambda b, p, i: (b * nq + i, p)),
        out_shape=jax.ShapeDtypeStruct((t, BRANCH_WIDTH), BF16),
        scratch_shapes=[pltpu.VMEM((LANES, seq), BF16),
                        pltpu.VMEM((2, tq, LANES), F32), pltpu.VMEM((2, tq, 2 * LANES), F32)],
        compiler_params=_cparams(("parallel", "parallel", "arbitrary")),
        name="a_attn",
    )(pn, _pair_masks(), pn, pp, sel, bias_tiles)


def _sortable_key(score):
    bits = pltpu.bitcast(score, jnp.int32)
    return bits ^ ((bits >> 31) & jnp.int32(0x7FFFFFFF))


SEL_ROWS = 512
CNT_ROWS = SEL_ROWS
GMAX_ROWS = 256
I16_MIN, I16_MAX = -(2 ** 15), 2 ** 15 - 1


def _a_select_kernel(qblk_ref, kblk_ref, ph_ref, pl_ref, pkh_ref, pkl_ref, sel_ref,
                     ikx_ref, keys_ref, khi_ref, klo_ref, gmax_ref, iqt_ref, jcut_ref, *, seq, k_sel):
    qi = pl.program_id(1)
    n_sel = (qi * BLOCK + BLOCK + SEL_ROWS - 1) // SEL_ROWS
    n_cnt = n_sel

    @pl.when(qi == 0)
    def _():
        def prep(c, carry):
            off = pl.multiple_of(c * 512, 512)
            kv = kblk_ref[pl.ds(off, 512), :]
            hi = kv.astype(BF16)
            lo = (kv - hi.astype(F32)).astype(BF16)
            ikx_ref[pl.ds(off, 512), :] = (
                jnp.dot(hi, pkh_ref[...], preferred_element_type=F32)
                + jnp.dot(lo, pkl_ref[...], preferred_element_type=F32)).astype(BF16)
            return carry
        lax.fori_loop(0, seq // 512, prep, 0)

    qb = qblk_ref[...]
    iq = qb[:, :IDX_HEADS * IDX_DIM]
    iq_hi = iq.astype(BF16)
    iq_lo = (iq - iq_hi.astype(F32)).astype(BF16)
    iqx = (jnp.dot(iq_hi, ph_ref[...], preferred_element_type=F32)
           + jnp.dot(iq_lo, pl_ref[...], preferred_element_type=F32))
    for h in range(IDX_HEADS):
        iqt_ref[h // 2, :, (h % 2) * LANES:(h % 2 + 1) * LANES] = iqx[:, h * LANES:(h + 1) * LANES].T.astype(BF16)
    iw_t = qb[:, IDX_HEADS * IDX_DIM:IDX_HEADS * IDX_DIM + LANES].T
    iw_rows = [iw_t[IDX_DIM + h:IDX_DIM + h + 1, :] for h in range(IDX_HEADS)]
    q_pos = qi * BLOCK + lax.broadcasted_iota(jnp.int32, (SEL_ROWS, BLOCK), 1)
    k_row = lax.broadcasted_iota(jnp.int32, (SEL_ROWS, BLOCK), 0)

    def score_chunk(c, masked):
        off = pl.multiple_of(c * SEL_ROWS, SEL_ROWS)
        kx = ikx_ref[pl.ds(off, SEL_ROWS), :]
        sc = jnp.zeros((SEL_ROWS, BLOCK), F32)
        for hp in range(IDX_HEADS // 2):
            xx = jnp.dot(kx, iqt_ref[hp], preferred_element_type=F32)
            sc = (sc + jnp.maximum(xx[:, :LANES], 0.0) * iw_rows[2 * hp]
                  + jnp.maximum(xx[:, LANES:], 0.0) * iw_rows[2 * hp + 1])
        key = _sortable_key(sc + 0.0)
        if masked:
            key = jnp.where(off + k_row <= q_pos, key, INT_MIN)
        keys_ref[pl.ds(off, SEL_ROWS), :] = key
        khi_ref[pl.ds(off, SEL_ROWS), :] = (key >> 16).astype(jnp.int16)
        klo_ref[pl.ds(off, SEL_ROWS), :] = ((key & 0xFFFF) + I16_MIN).astype(jnp.int16)
        gm = gmax_ref[...]
        for j in range(SEL_ROWS // GMAX_ROWS):
            gm = jnp.maximum(gm, key[j * GMAX_ROWS:(j + 1) * GMAX_ROWS])
        gmax_ref[...] = gm

    def score_body(c, carry):
        score_chunk(c, False)
        return carry

    gmax_ref[...] = jnp.full(gmax_ref.shape, INT_MIN, jnp.int32)
    lax.fori_loop(0, n_sel - 1, score_body, 0)
    score_chunk(n_sel - 1, True)

    def count16(ref, cand):
        cand16 = cand.astype(jnp.int16)

        def cnt_chunk(c, acc):
            off = pl.multiple_of(c * CNT_ROWS, CNT_ROWS)
            hit = jnp.where(ref[pl.ds(off, CNT_ROWS), :] >= cand16, jnp.int16(1), jnp.int16(0))
            parts = [hit[i * 16:(i + 1) * 16] for i in range(CNT_ROWS // 16)]
            while len(parts) > 1:
                parts = [a + b for a, b in zip(parts[::2], parts[1::2])]
            return acc + parts[0]
        acc = lax.fori_loop(0, n_cnt, cnt_chunk, jnp.zeros((16, BLOCK), jnp.int16))
        return jnp.sum(acc.astype(jnp.int32), axis=0, keepdims=True)

    gm = gmax_ref[...]
    h_lo = jnp.min(gm, axis=0, keepdims=True) >> 16
    h_hi = jnp.max(gm, axis=0, keepdims=True) >> 16
    nbits_hi = jnp.max(32 - lax.clz(h_hi - h_lo))

    def bisect_hi(it, lo):
        cand = lo + lax.shift_left(jnp.int32(1), nbits_hi - 1 - it)
        cnt = count16(khi_ref, jnp.minimum(cand, I16_MAX))
        return jnp.where((cand <= h_hi) & (cnt >= k_sel), cand, lo)

    top_hi = lax.fori_loop(0, nbits_hi, bisect_hi, h_lo)
    above = jnp.where(top_hi < I16_MAX, count16(khi_ref, jnp.minimum(top_hi + 1, I16_MAX)), 0)
    need_lo = k_sel - above
    top_hi16 = top_hi.astype(jnp.int16)

    def bucket(c, carry):
        off = pl.multiple_of(c * CNT_ROWS, CNT_ROWS)
        rows = pl.ds(off, CNT_ROWS)
        klo_ref[rows, :] = jnp.where(khi_ref[rows, :] == top_hi16, klo_ref[rows, :], jnp.int16(I16_MIN))
        return carry

    lax.fori_loop(0, n_cnt, bucket, 0)

    def bisect_lo(it, lo):
        cand = lo + lax.shift_left(jnp.int32(1), jnp.int32(15) - it)
        return jnp.where(count16(klo_ref, cand) >= need_lo, cand, lo)

    top_lo = lax.fori_loop(0, 16, bisect_lo, jnp.full((1, BLOCK), I16_MIN, jnp.int32))
    thr = top_hi * 65536 + (top_lo - I16_MIN)

    def count_rows(pred):
        def cnt_chunk(c, acc):
            off = pl.multiple_of(c * CNT_ROWS, CNT_ROWS)
            hit = jnp.where(pred(keys_ref[pl.ds(off, CNT_ROWS), :], off), 1, 0)
            return acc + jnp.sum(hit.reshape(CNT_ROWS // 8, 8, BLOCK), axis=0)
        acc = lax.fori_loop(0, n_cnt, cnt_chunk, jnp.zeros((8, BLOCK), jnp.int32))
        return jnp.sum(acc, axis=0, keepdims=True)

    def count_ge(cand):
        return count_rows(lambda keys, off: keys >= cand)

    has_k = thr > INT_MIN
    thr = jnp.maximum(thr, INT_MIN + 1)

    excess = jnp.where(has_k, count_ge(thr) - k_sel, 0)
    jcut_ref[...] = jnp.full(jcut_ref.shape, 2 ** 31 - 1, jnp.int32)

    @pl.when(jnp.max(excess) > 0)
    def _():
        need = jnp.where(excess > 0, k_sel - count_ge(thr + 1), 1)
        rev_row = (seq - 1) - lax.broadcasted_iota(jnp.int32, (CNT_ROWS, BLOCK), 0)

        def tie_bisect(it, lo):
            cand = lo + lax.shift_left(jnp.int32(1), jnp.int32(seq.bit_length() - 2) - it)
            cnt = count_rows(lambda keys, off: (keys == thr) & (rev_row - off >= cand))
            return jnp.where(cnt >= need, cand, lo)

        rev = lax.fori_loop(0, seq.bit_length() - 1, tie_bisect, jnp.zeros((1, BLOCK), jnp.int32))
        jcut = jnp.where(excess > 0, (seq - 1) - rev, 2 ** 31 - 1)
        jcut_ref[...] = jnp.broadcast_to(jcut, jcut_ref.shape)

    jcut = jcut_ref[0:1, :]

    def emit(c, carry):
        off = pl.multiple_of(c * SEL_ROWS, SEL_ROWS)
        keys = keys_ref[pl.ds(off, SEL_ROWS), :]
        bar = jnp.where(off + k_row > jcut, thr + 1, thr)
        add = jnp.where(keys >= bar, 0.0, NEG)
        sel_ref[:, pl.ds(off, SEL_ROWS)] = jnp.concatenate(
            [add[j * BLOCK:(j + 1) * BLOCK].T for j in range(SEL_ROWS // BLOCK)], axis=1).astype(BF16)
        return carry

    lax.fori_loop(0, n_sel, emit, 0)

    def fill(c, carry):
        off = pl.multiple_of(c * SEL_ROWS, SEL_ROWS)
        sel_ref[:, pl.ds(off, SEL_ROWS)] = jnp.full((BLOCK, SEL_ROWS), NEG, BF16)
        return carry

    lax.fori_loop(n_sel, seq // SEL_ROWS, fill, 0)


def _a_select(pf, batch, seq):
    t = pf.shape[0]
    nq = seq // BLOCK
    k_sel = min(TOPK_MAX, seq // 4)
    nlane = IDX_HEADS * IDX_DIM
    r = np.arange(nlane)
    ph = np.zeros((nlane, IDX_HEADS * LANES), np.float32)
    plo = np.zeros((nlane, IDX_HEADS * LANES), np.float32)
    ph[r, (r // IDX_DIM) * LANES + r % IDX_DIM] = 1
    ph[r, (r // IDX_DIM) * LANES + 2 * IDX_DIM + r % IDX_DIM] = 1
    plo[r, (r // IDX_DIM) * LANES + IDX_DIM + r % IDX_DIM] = 1
    d = np.arange(IDX_DIM)
    pkh = np.zeros((LANES, LANES), np.float32)
    pkl = np.zeros((LANES, LANES), np.float32)
    pkh[d, d] = 1
    pkh[d, IDX_DIM + d] = 1
    pkl[d, 2 * IDX_DIM + d] = 1
    full = lambda shape: pl.BlockSpec(shape, lambda b, i: (0,) * len(shape))
    return pl.pallas_call(
        functools.partial(_a_select_kernel, seq=seq, k_sel=k_sel),
        grid=(batch, nq),
        in_specs=[pl.BlockSpec((BLOCK, 512), lambda b, i: (b * nq + i, 0)),
                  pl.BlockSpec((seq, LANES), lambda b, i: (b, nlane // LANES)),
                  full(ph.shape), full(plo.shape), full(pkh.shape), full(pkl.shape)],
        out_specs=pl.BlockSpec((BLOCK, seq), lambda b, i: (b * nq + i, 0)),
        out_shape=jax.ShapeDtypeStruct((t, seq), BF16),
        scratch_shapes=[pltpu.VMEM((seq, LANES), BF16), pltpu.VMEM((seq, LANES), jnp.int32),
                        pltpu.VMEM((seq, LANES), jnp.int16), pltpu.VMEM((seq, LANES), jnp.int16),
                        pltpu.VMEM((GMAX_ROWS, LANES), jnp.int32),
                        pltpu.VMEM((IDX_HEADS // 2, LANES, 2 * LANES), BF16),
                        pltpu.VMEM((8, LANES), jnp.int32)],
        compiler_params=_cparams(("parallel", "arbitrary")),
        name="a_select",
    )(pf, pf, jnp.asarray(ph, BF16), jnp.asarray(plo, BF16), jnp.asarray(pkh, BF16),
      jnp.asarray(pkl, BF16))


BAND_RBLK = 4


def _band_kernel(*refs, has_sink, want_lse, kv_div, rblk):
    if has_sink:
        sink_ref, refs = refs[0], refs[1:]
    q_ref, hm_ref, kp_ref, kc_ref, vp_ref, vc_ref, bias_ref, o_ref = refs[:8]
    first = pl.program_id(1) == 0
    lane = lax.broadcasted_iota(jnp.int32, (BLOCK, LANES), 1)
    col = lax.broadcasted_iota(jnp.int32, (BLOCK, 2 * BLOCK), 1)
    no_prev = jnp.logical_and(first, col < BLOCK)
    for p in range(BRANCH_WIDTH // LANES):
        kcol = (p // kv_div) * LANES
        kk = jnp.concatenate([kp_ref[:, kcol:kcol + LANES], kc_ref[:, kcol:kcol + LANES]], axis=0)
        vv = jnp.concatenate([vp_ref[:, kcol:kcol + LANES], vc_ref[:, kcol:kcol + LANES]], axis=0)
        for i in range(rblk):
            rows = slice(i * BLOCK, (i + 1) * BLOCK)
            q = q_ref[rows, p * LANES:(p + 1) * LANES]
            kblk = kk[i * BLOCK:(i + 2) * BLOCK]
            vblk = vv[i * BLOCK:(i + 2) * BLOCK]
            outs, lses = [], []
            for e in range(2):
                s = lax.dot_general(q * hm_ref[e], kblk, _CONTRACT_LANES, preferred_element_type=F32)
                s = s + bias_ref[2 * p + e]
                if i == 0:
                    s = jnp.where(no_prev, NEG, s)
                m = jnp.max(s, axis=1, keepdims=True)
                if has_sink:
                    sk = sink_ref[2 * p + e]
                    m = jnp.maximum(m, sk)
                pr = jnp.exp(s - m)
                den = jnp.sum(pr, axis=1, keepdims=True)
                if has_sink:
                    den = den + jnp.exp(sk - m)
                outs.append(jnp.dot(pr.astype(BF16), vblk, preferred_element_type=F32) / den)
                lses.append(m + jnp.log(den))
            o_ref[rows, p * LANES:(p + 1) * LANES] = jnp.where(lane < HEAD_DIM, outs[0], outs[1]).astype(o_ref.dtype)
            if want_lse:
                refs[8][rows, p * LANES:(p + 1) * LANES] = jnp.where(lane < HEAD_DIM, lses[0], lses[1])


def _band_attn(q_arr, k_arr, v_arr, bias, nsub, sub_len, qcb, kcb, vcb, kv_w, sinks=None, want_lse=False):
    t = q_arr.shape[0]
    rblk = min(BAND_RBLK, sub_len // BLOCK)
    step_rows = rblk * BLOCK
    nbs = sub_len // step_rows
    kv_div = BRANCH_WIDTH // kv_w

    def cur(u, n):
        return u * nbs + n

    def prev(u, n):
        return jnp.where(n == 0, u * nbs * rblk, (u * nbs + n) * rblk - 1)

    in_specs = [
        pl.BlockSpec((step_rows, BRANCH_WIDTH), lambda u, n: (cur(u, n), qcb)),
        pl.BlockSpec((2, 1, LANES), lambda u, n: (0, 0, 0)),
        pl.BlockSpec((BLOCK, kv_w), lambda u, n: (prev(u, n), kcb)),
        pl.BlockSpec((step_rows, kv_w), lambda u, n: (cur(u, n), kcb)),
        pl.BlockSpec((BLOCK, kv_w), lambda u, n: (prev(u, n), vcb)),
        pl.BlockSpec((step_rows, kv_w), lambda u, n: (cur(u, n), vcb)),
        pl.BlockSpec(bias.shape, lambda u, n: (0, 0, 0)),
    ]
    args = [q_arr, _pair_masks(), k_arr, k_arr, v_arr, v_arr, bias]
    if sinks is not None:
        in_specs = [pl.BlockSpec(memory_space=pltpu.SMEM)] + in_specs
        args = [sinks] + args
    o_spec = pl.BlockSpec((step_rows, BRANCH_WIDTH), lambda u, n: (cur(u, n), 0))
    out_specs = [o_spec]
    out_shape = [jax.ShapeDtypeStruct((t, BRANCH_WIDTH), BF16)]
    if want_lse:
        out_specs.append(o_spec)
        out_shape.append(jax.ShapeDtypeStruct((t, BRANCH_WIDTH), F32))
    return pl.pallas_call(
        functools.partial(_band_kernel, has_sink=sinks is not None, want_lse=want_lse, kv_div=kv_div, rblk=rblk),
        grid=(nsub, nbs),
        in_specs=in_specs,
        out_specs=out_specs,
        out_shape=out_shape,
        compiler_params=_cparams(("parallel", "arbitrary")),
        name="band_attn",
    )(*args)


def _band_bias(table, step, max_dist):
    rel = BLOCK + np.arange(BLOCK)[:, None] - np.arange(2 * BLOCK)[None, :]
    bias = table[_t5_bucket(jnp.asarray(rel * step))].transpose(2, 0, 1).astype(F32)
    ok = (rel >= 0) & (rel <= max_dist)
    return jnp.where(jnp.asarray(ok)[None], bias, NEG)


def _merge_kernel(x_ref, ya_ref, yb0_ref, yb1_ref, yb2_ref, l0_ref, l1_ref, l2_ref, yc_ref, yd_ref,
                  za_ref, zb_ref, zc_ref, zd_ref, ga_ref, gb_ref, gc_ref, gd_ref, wb_ref, wo_ref, o_ref,
                  y1_ref, y2_ref, s1_ref, s2_ref):
    ncol = BRANCH_WIDTH // LANES
    for src, dst in ((yb1_ref, y1_ref), (yb2_ref, y2_ref), (l1_ref, s1_ref), (l2_ref, s2_ref)):
        dil = src.shape[1]
        for r in range(dil):
            blk = src[0, r].astype(F32)
            for c in range(ncol):
                dst[c, pl.ds(r, src.shape[2], stride=dil), :] = blk[:, c * LANES:(c + 1) * LANES]
    wide = lambda ref: jnp.concatenate([ref[c] for c in range(ncol)], axis=1)
    l0, l1, l2 = l0_ref[0, 0], wide(s1_ref), wide(s2_ref)
    mx = jnp.maximum(jnp.maximum(l0, l1), l2)
    w0, w1, w2 = jnp.exp(l0 - mx), jnp.exp(l1 - mx), jnp.exp(l2 - mx)
    yb = (w0 * yb0_ref[0, 0].astype(F32) + w1 * wide(y1_ref) + w2 * wide(y2_ref)) / (w0 + w1 + w2)
    ys = (ya_ref[...].astype(F32), yb, yc_ref[...].astype(F32), yd_ref[...].astype(F32))
    zs = (za_ref, zb_ref, zc_ref, zd_ref)
    gs = (ga_ref, gb_ref, gc_ref, gd_ref)
    merged = jnp.zeros(o_ref.shape, F32)
    for n in range(N_BRANCH):
        z = zs[n][...].astype(F32)
        u = (ys[n] * (z * jax.nn.sigmoid(z))).astype(BF16)
        proj = jnp.dot(u, wb_ref[n], preferred_element_type=F32)
        merged = merged + jax.nn.sigmoid(gs[n][...].astype(F32)) * proj
    o_ref[...] = x_ref[...] + jnp.dot(merged.astype(BF16), wo_ref[...], preferred_element_type=F32)


def _merge(x2, ya, yb, lb, yc, yd, pp, wb, wo, batch, seq, tm=512):
    t = x2.shape[0]
    bw = BRANCH_WIDTH
    nsb = seq // tm
    row = lambda w, c: pl.BlockSpec((tm, w), lambda i: (i, c))

    def sub(dil):
        return pl.BlockSpec((1, dil, tm // dil, bw), lambda i: (i // nsb, 0, i % nsb, 0))

    dils = [dil for _, dil in B_GROUPS]
    yb = [a.reshape(batch, dil, seq // dil, bw) for a, dil in zip(yb, dils)]
    lb = [a.reshape(batch, dil, seq // dil, bw) for a, dil in zip(lb, dils)]
    in_specs = ([row(D_MODEL, 0), row(bw, 0)] + [sub(dil) for dil in dils] * 2 + [row(bw, 0), row(bw, 0)]
                + [row(bw, PP_AZ // bw), row(bw, PP_BZ // bw), row(bw, PP_CZ // bw), row(bw, PP_DZ // bw)]
                + [row(D_MODEL, PP_G // D_MODEL + n) for n in range(N_BRANCH)]
                + [pl.BlockSpec((N_BRANCH, bw, D_MODEL), lambda i: (0, 0, 0)),
                   pl.BlockSpec((D_MODEL, D_MODEL), lambda i: (0, 0))])
    return pl.pallas_call(
        _merge_kernel,
        grid=(t // tm,),
        in_specs=in_specs,
        out_specs=row(D_MODEL, 0),
        out_shape=jax.ShapeDtypeStruct((t, D_MODEL), F32),
        scratch_shapes=[pltpu.VMEM((bw // LANES, tm, LANES), F32)] * 4,
        compiler_params=_cparams(("parallel",)),
        name="merge",
    )(x2, ya, yb[0], yb[1], yb[2], lb[0], lb[1], lb[2], yc, yd, pp, pp, pp, pp, pp, pp, pp, pp, wb, wo)


def _layer_weights(w_in, qk_gain_a, qk_gain_b, qk_gain_d):
    def cols(start, width):
        return w_in[:, :, start:start + width]

    n_bq = len(B_GROUPS) * B_HEADS * HEAD_DIM
    dk = cols(_D0 + 512, 128)
    dv = cols(_D0 + 640, 128)
    dup = lambda a: jnp.concatenate([a[:, :, :64], a[:, :, :64], a[:, :, 64:], a[:, :, 64:]], axis=-1)
    w_pn = jnp.concatenate([cols(_A0, 512), cols(_A0 + 512, 512), cols(_D0, 512), dup(dk), dup(dv)],
                           axis=-1).astype(BF16)
    w_pp = jnp.concatenate([cols(_G0, N_BRANCH * D_MODEL), cols(_A0 + 1024, 512), cols(_A0 + 1536, 512),
                            cols(_B0 + 3 * n_bq, 512), cols(_C0 + 416, 512), cols(_D0 + 768, 512)],
                           axis=-1).astype(BF16)
    w_bg = [jnp.concatenate([cols(_B0 + g * 512, 512), cols(_B0 + n_bq + g * 512, 512),
                             cols(_B0 + 2 * n_bq + g * 512, 512)], axis=-1).astype(BF16)
            for g in range(len(B_GROUPS))]
    depth = w_in.shape[0]
    zeros = lambda w: jnp.zeros((depth, D_MODEL, w), w_in.dtype)
    w_pf = jnp.concatenate([cols(_A0 + 2048, 296), zeros(216), cols(_C0, 416), zeros(96)], axis=-1).astype(BF16)

    scale = HEAD_DIM ** -0.5
    tile = lambda g, reps: jnp.tile(g, (1, reps))
    hg = jnp.concatenate([tile(qk_gain_a[:, 0] * scale, 8), tile(qk_gain_a[:, 1], 8),
                          tile(qk_gain_d[:, 0] * scale, 8), tile(qk_gain_d[:, 1], 4),
                          jnp.ones((depth, 256), F32)], axis=-1)
    flag = jnp.concatenate([jnp.ones((PN_DV,), F32), jnp.zeros((PN_W - PN_DV,), F32)])
    hg_b = jnp.concatenate([tile(qk_gain_b[:, 0] * scale, 8), tile(qk_gain_b[:, 1], 8),
                            jnp.ones((depth, 512), F32)], axis=-1)
    flag_b = jnp.concatenate([jnp.ones((2 * BRANCH_WIDTH,), F32), jnp.zeros((BRANCH_WIDTH,), F32)])
    return w_pn, w_pp, w_bg, w_pf, hg, flag, hg_b, flag_b


def _c_weights(qk_gain_c, w_q_b, w_kv_b, seq):
    src_q, src_k, gain_idx, head, pe_src, rope_j, rope_half = _c_layout_tables()
    take = lambda w, src: jnp.where(jnp.asarray(src >= 0), jnp.take(w, jnp.asarray(np.maximum(src, 0)), axis=-1), 0.0)
    wq = take(w_q_b, src_q).astype(BF16)
    wk = take(w_kv_b, src_k).astype(BF16)
    v_src = np.array([h * (C_NOPE + C_V) + C_NOPE + c for h in range(C_HEADS) for c in range(C_V)])
    wv = jnp.take(w_kv_b, jnp.asarray(v_src), axis=-1).astype(BF16)
    ppe = np.zeros((LANES, C_QK_W), np.float32)
    ln = np.nonzero(pe_src >= 0)[0]
    ppe[pe_src[ln], ln] = 1
    grp = ((head[:, None] == head[None, :]) & (head[:, None] >= 0)).astype(np.float32)
    g_take = lambda g: jnp.where(jnp.asarray(gain_idx >= 0), jnp.take(g, jnp.asarray(np.maximum(gain_idx, 0)), axis=-1), 0.0)
    gq = g_take(qk_gain_c[:, 0]) * (C_NOPE + C_ROPE) ** -0.5
    gk = g_take(qk_gain_c[:, 1])
    freq = ROPE_THETA ** (-jnp.arange(C_ROPE_HALF, dtype=F32) / C_ROPE_HALF)
    ang = jnp.arange(seq).astype(F32)[:, None] * freq[None, :]
    cos_j, sin_j = jnp.cos(ang), jnp.sin(ang)
    is_rope = jnp.asarray(rope_j >= 0)
    jj = jnp.asarray(np.maximum(rope_j, 0))
    cos = jnp.where(is_rope, cos_j[:, jj], 1.0)
    sin = jnp.where(is_rope, sin_j[:, jj], 0.0)
    s1 = jnp.where(jnp.asarray(rope_half == 0), -sin, 0.0)
    s2 = jnp.where(jnp.asarray(rope_half == 1), sin, 0.0)
    return wq, wk, wv, jnp.asarray(ppe, BF16), jnp.asarray(grp, BF16), gq, gk, cos, s1, s2


def _a_bias_tiles(table):
    a = np.arange(BLOCK)
    dist = (np.arange(A_BIAS_TILES)[:, None, None] * BLOCK + a[None, :, None] - a[None, None, :])
    return table[_t5_bucket(jnp.asarray(dist))].transpose(3, 0, 1, 2).astype(F32)


def kernel(x, norm_gain, w_in, qk_gain_a, qk_gain_b, qk_gain_c, qk_gain_d, c_q_gain, c_kv_gain,
           w_q_b, w_kv_b, sinks, rel_bias, w_branch, w_out):
    batch, seq, d_model = x.shape
    depth = w_in.shape[0]
    t = batch * seq
    x2 = x.reshape(t, d_model)

    w_pn, w_pp, w_bg, w_pf, hg, flag, hg_b, flag_b = _layer_weights(w_in, qk_gain_a, qk_gain_b, qk_gain_d)
    wq, wk, wv, ppe, grp, gq, gk, cos, s1, s2 = _c_weights(qk_gain_c, w_q_b, w_kv_b, seq)
    wb = w_branch.astype(BF16)
    wo = w_out.astype(BF16)

    bias_a = _a_bias_tiles(rel_bias[:, :A_HEADS])
    bias_b = [_band_bias(rel_bias[:, A_HEADS + g * B_HEADS:A_HEADS + (g + 1) * B_HEADS], dil, window // dil)
              for g, (window, dil) in enumerate(B_GROUPS)]
    bias_d = _band_bias(rel_bias[:, N_BIAS_HEADS - D_HEADS:], 1, D_WINDOW - 1)

    for l in range(depth):
        pn = _proj(x2, norm_gain[l], w_pn[l], BF16, head_gain=hg[l], flag=flag)
        pp = _proj(x2, norm_gain[l], w_pp[l], BF16)
        pf = _proj(x2, norm_gain[l], w_pf[l], F32)

        sel = _a_select(pf, batch, seq)
        ya = _a_attn(pn, pp, sel, bias_a, batch, seq)

        yb, lb = [], []
        for g, (window, dil) in enumerate(B_GROUPS):
            bg = _proj(x2, norm_gain[l], w_bg[g][l], BF16, head_gain=hg_b[l], flag=flag_b, dil=dil, seq=seq)
            o, lse = _band_attn(bg, bg, bg, bias_b[g], batch * dil, seq // dil, 0, 1, 2, BRANCH_WIDTH,
                                want_lse=True)
            yb.append(o)
            lb.append(lse)

        qc, kc, vc = _c_prep(pf, c_q_gain[l].reshape(1, -1), c_kv_gain[l].reshape(1, -1), wq[l], wk[l], wv[l],
                             ppe, grp, gq[l].reshape(1, -1), gk[l].reshape(1, -1), cos, s1, s2, seq)
        yc = _c_attn(qc, kc, vc, batch, seq)

        (yd,) = _band_attn(pn, pn, pn, bias_d, batch, seq, PN_DQ // BRANCH_WIDTH, PN_DK // 256, PN_DV // 256,
                           256, sinks=sinks[l])

        x2 = _merge(x2, ya, yb, lb, yc, yd, pp, wb[l], wo[l], batch, seq)
    return x2.reshape(batch, seq, d_model)
```

```python
import functools
import math

import numpy as np
import jax
import jax.numpy as jnp
from jax import lax
from jax.experimental import pallas as pl
from jax.experimental.pallas import tpu as pltpu

F32 = jnp.float32
BF16 = jnp.bfloat16

D_MODEL = 1024
BLOCK = 128
HEAD_DIM = 64
N_BRANCH = 4
BRANCH_WIDTH = 512
EPS = 1e-6
A_HEADS = 8
IDX_HEADS = 8
IDX_DIM = 32
TOPK_MAX = 256
B_GROUPS = ((128, 1), (512, 4), (2048, 16))
B_HEADS = 8
C_HEADS = 8
C_NOPE = 64
C_ROPE = 32
C_V = 64
C_Q_LORA = 256
C_KV_LORA = 128
ROPE_THETA = 10000.0
D_HEADS = 8
D_KV_HEADS = 2
D_WINDOW = 128
NUM_BUCKETS = 32
MAX_DISTANCE = 2048
N_BIAS_HEADS = A_HEADS + len(B_GROUPS) * B_HEADS + D_HEADS

LANES = 128
NEG = -1e30
INT_MIN = -(2 ** 31)

_A0 = 0
_B0 = 2344
_C0 = 7464
_D0 = 8392
_G0 = 9672

PN_AQ, PN_AK, PN_DQ, PN_DK, PN_DV, PN_W = 0, 512, 1024, 1536, 1792, 2048
PP_G, PP_AV, PP_AZ, PP_BZ, PP_CZ, PP_DZ, PP_W = 0, 4096, 4608, 5120, 5632, 6144, 6656
PF_W = 1024
BG_W = 3 * BRANCH_WIDTH

A_BIAS_TILES = MAX_DISTANCE // BLOCK + 2


def _cparams(sem, vmem_mb=48):
    return pltpu.CompilerParams(dimension_semantics=sem, vmem_limit_bytes=vmem_mb * 1024 * 1024)


def _t5_bucket(dist):
    max_exact = NUM_BUCKETS // 2
    d = jnp.maximum(dist, 0)
    logd = jnp.log(jnp.maximum(d, 1).astype(F32) / max_exact)
    large = max_exact + (logd / math.log(MAX_DISTANCE / max_exact) * (NUM_BUCKETS - max_exact)).astype(jnp.int32)
    return jnp.where(d < max_exact, d, jnp.minimum(large, NUM_BUCKETS - 1))


def _proj_kernel(x_ref, g_ref, w_ref, *rest, norm, dil):
    rest = list(rest)
    strided = dil is not None and dil > 1
    xn_ref = rest.pop(-2) if strided else rest.pop(-1)
    res_ref = rest.pop(-1) if strided else None
    if norm:
        hg_ref, flag_ref, bd_ref, o_ref = rest
    else:
        (o_ref,) = rest

    @pl.when(pl.program_id(1) == 0)
    def _():
        x = x_ref[...]
        ms = jnp.mean(x * x, axis=-1, keepdims=True)
        xn_ref[...] = (x * lax.rsqrt(ms + EPS) * g_ref[...]).astype(BF16)

    h = jnp.dot(xn_ref[...], w_ref[...], preferred_element_type=F32)
    if norm:
        ss = jnp.dot((h * h).astype(BF16), bd_ref[...], preferred_element_type=F32)
        scale = lax.rsqrt(ss * (1.0 / HEAD_DIM) + EPS) * hg_ref[...]
        h = h * jnp.where(flag_ref[...] > 0, scale, 1.0)
    if dil is None:
        o_ref[...] = h.astype(o_ref.dtype)
    elif dil == 1:
        o_ref[0, 0] = h.astype(o_ref.dtype)
    else:
        sub = h.shape[0] // dil
        for c in range(h.shape[1] // LANES):
            res_ref[c] = h[:, c * LANES:(c + 1) * LANES]
        for r in range(dil):
            o_ref[0, r] = jnp.concatenate(
                [res_ref[c, pl.ds(r, sub, stride=dil), :] for c in range(h.shape[1] // LANES)],
                axis=1).astype(o_ref.dtype)


def _proj(x2, gain, w, out_dtype, head_gain=None, flag=None, dil=None, seq=None, tm=1024, tn=512):
    t, d = x2.shape
    n = w.shape[1]
    norm = head_gain is not None
    scratch = [pltpu.VMEM((tm, d), BF16)]
    if dil is None:
        out_spec = pl.BlockSpec((tm, tn), lambda i, j: (i, j))
        out_shape = jax.ShapeDtypeStruct((t, n), out_dtype)
        kdil = None
    else:
        nsb = seq // tm
        out_spec = pl.BlockSpec((1, dil, tm // dil, tn), lambda i, j: (i // nsb, 0, i % nsb, j))
        out_shape = jax.ShapeDtypeStruct((t // seq, dil, seq // dil, n), out_dtype)
        kdil = dil
        if dil > 1:
            scratch.append(pltpu.VMEM((tn // LANES, tm, LANES), F32))
    in_specs = [pl.BlockSpec((tm, d), lambda i, j: (i, 0)),
                pl.BlockSpec((1, d), lambda i, j: (0, 0)),
                pl.BlockSpec((d, tn), lambda i, j: (0, j))]
    args = [x2, gain.reshape(1, d), w]
    if norm:
        lane = np.arange(tn)
        bd = jnp.asarray((lane[:, None] // HEAD_DIM == lane[None, :] // HEAD_DIM), BF16)
        in_specs += [pl.BlockSpec((1, tn), lambda i, j: (0, j)),
                     pl.BlockSpec((1, tn), lambda i, j: (0, j)),
                     pl.BlockSpec((tn, tn), lambda i, j: (0, 0))]
        args += [head_gain.reshape(1, n), flag.reshape(1, n), bd]
    out = pl.pallas_call(
        functools.partial(_proj_kernel, norm=norm, dil=kdil),
        grid=(t // tm, n // tn),
        in_specs=in_specs,
        out_specs=out_spec,
        out_shape=out_shape,
        scratch_shapes=scratch,
        compiler_params=_cparams(("parallel", "arbitrary")),
        name=("proj_norm" if norm else "proj") + ("" if dil is None else "_dil%d" % dil),
    )(*args)
    return out.reshape(t, n)


C_PAIR = 256
C_QK_W = (C_HEADS // 2) * C_PAIR
C_ROPE_HALF = C_ROPE // 2


def _c_lane(h, c):
    base = (h // 2) * C_PAIR
    e = h % 2
    if c < C_NOPE:
        return base + e * C_NOPE + c
    return base + 2 * C_NOPE + e * C_ROPE + (c - C_NOPE)


def _c_layout_tables():
    src_q = np.full(C_QK_W, -1, np.int64)
    src_k = np.full(C_QK_W, -1, np.int64)
    gain_idx = np.full(C_QK_W, -1, np.int64)
    head = np.full(C_QK_W, -1, np.int64)
    pe_src = np.full(C_QK_W, -1, np.int64)
    rope_j = np.full(C_QK_W, -1, np.int64)
    rope_half = np.zeros(C_QK_W, np.int64)
    for h in range(C_HEADS):
        for c in range(C_NOPE + C_ROPE):
            ln = _c_lane(h, c)
            src_q[ln] = h * (C_NOPE + C_ROPE) + c
            gain_idx[ln] = c
            head[ln] = h
            if c < C_NOPE:
                src_k[ln] = h * (C_NOPE + C_V) + c
            else:
                r = c - C_NOPE
                pe_src[ln] = r
                rope_j[ln] = r % C_ROPE_HALF
                rope_half[ln] = r // C_ROPE_HALF
    return src_q, src_k, gain_idx, head, pe_src, rope_j, rope_half


def _c_prep_kernel(pf_ref, cqg_ref, ckvg_ref, wq_ref, wk_ref, wv_ref, ppe_ref, grp_ref,
                   gq_ref, gk_ref, cos_ref, s1_ref, s2_ref, q_ref, k_ref, v_ref):
    blk = pf_ref[...]
    cq = blk[:, :C_Q_LORA]
    ckv = blk[:, C_Q_LORA:C_Q_LORA + C_KV_LORA]
    pe = blk[:, C_Q_LORA + C_KV_LORA:]

    def rms(v, g):
        return v * lax.rsqrt(jnp.mean(v * v, axis=-1, keepdims=True) + EPS) * g

    def head_norm_rope(raw, gain):
        ss = jnp.dot((raw * raw).astype(BF16), grp_ref[...], preferred_element_type=F32)
        y = raw * lax.rsqrt(ss * (1.0 / (C_NOPE + C_ROPE)) + EPS) * gain
        up = pltpu.roll(y, C_QK_W - C_ROPE_HALF, 1)
        dn = pltpu.roll(y, C_ROPE_HALF, 1)
        return y * cos_ref[...] + up * s1_ref[...] + dn * s2_ref[...]

    cqn = rms(cq, cqg_ref[...]).astype(BF16)
    q_raw = jnp.dot(cqn, wq_ref[...], preferred_element_type=F32)
    q_ref[...] = head_norm_rope(q_raw, gq_ref[...]).astype(BF16)

    ckvn = rms(ckv, ckvg_ref[...]).astype(BF16)
    pe_hi = pe.astype(BF16)
    pe_lo = (pe - pe_hi.astype(F32)).astype(BF16)
    k_raw = (jnp.dot(ckvn, wk_ref[...], preferred_element_type=F32)
             + jnp.dot(pe_hi, ppe_ref[...], preferred_element_type=F32)
             + jnp.dot(pe_lo, ppe_ref[...], preferred_element_type=F32))
    k_ref[...] = head_norm_rope(k_raw, gk_ref[...]).astype(BF16)
    v_ref[...] = jnp.dot(ckvn, wv_ref[...], preferred_element_type=F32).astype(BF16)


def _c_prep(pf, cqg, ckvg, wq, wk, wv, ppe, grp, gq, gk, cos, s1, s2, seq, tm=512):
    t = pf.shape[0]
    nsb = seq // tm
    full = lambda shape: pl.BlockSpec(shape, lambda i: (0,) * len(shape))
    tab = pl.BlockSpec((tm, C_QK_W), lambda i: (i % nsb, 0))
    return pl.pallas_call(
        _c_prep_kernel,
        grid=(t // tm,),
        in_specs=[pl.BlockSpec((tm, 512), lambda i: (i, 1)),
                  full((1, C_Q_LORA)), full((1, C_KV_LORA)),
                  full((C_Q_LORA, C_QK_W)), full((C_KV_LORA, C_QK_W)), full((C_KV_LORA, BRANCH_WIDTH)),
                  full((LANES, C_QK_W)), full((C_QK_W, C_QK_W)),
                  full((1, C_QK_W)), full((1, C_QK_W)), tab, tab, tab],
        out_specs=[pl.BlockSpec((tm, C_QK_W), lambda i: (i, 0)),
                   pl.BlockSpec((tm, C_QK_W), lambda i: (i, 0)),
                   pl.BlockSpec((tm, BRANCH_WIDTH), lambda i: (i, 0))],
        out_shape=[jax.ShapeDtypeStruct((t, C_QK_W), BF16),
                   jax.ShapeDtypeStruct((t, C_QK_W), BF16),
                   jax.ShapeDtypeStruct((t, BRANCH_WIDTH), BF16)],
        compiler_params=_cparams(("parallel",)),
        name="c_prep",
    )(pf, cqg, ckvg, wq, wk, wv, ppe, grp, gq, gk, cos, s1, s2)


def _flash_update(e, s, v_ones, m_ref, acc_ref):
    m_prev = m_ref[e]
    m_new = jnp.maximum(m_prev, jnp.max(s, axis=1, keepdims=True))
    alpha = jnp.exp(m_prev - m_new)
    p = jnp.exp(s - jnp.concatenate([m_new] * (s.shape[1] // LANES), axis=1))
    acc_ref[e] = (jnp.concatenate([alpha, alpha], axis=1) * acc_ref[e]
                  + jnp.dot(p.astype(BF16), v_ones, preferred_element_type=F32))
    m_ref[e] = m_new


def _flash_init(m_ref, acc_ref):
    m_ref[...] = jnp.full(m_ref.shape, NEG, F32)
    acc_ref[...] = jnp.zeros(acc_ref.shape, F32)


def _flash_finish(o_ref, acc_ref):
    lane = lax.broadcasted_iota(jnp.int32, o_ref.shape, 1)
    o0 = acc_ref[0, :, :LANES] / acc_ref[0, :, LANES:]
    o1 = acc_ref[1, :, :LANES] / acc_ref[1, :, LANES:]
    o_ref[...] = jnp.where(lane < HEAD_DIM, o0, o1).astype(o_ref.dtype)


def _with_ones(v_chunk):
    return jnp.concatenate([v_chunk, jnp.ones(v_chunk.shape, v_chunk.dtype)], axis=1)


_CONTRACT_LANES = (((1,), (1,)), ((), ()))


def _head_masks(width, ranges0, ranges1):
    m = np.zeros((2, 1, width), np.float32)
    for e, ranges in enumerate((ranges0, ranges1)):
        for lo, hi in ranges:
            m[e, 0, lo:hi] = 1
    return jnp.asarray(m, BF16)


def _pair_masks():
    return _head_masks(LANES, [(0, HEAD_DIM)], [(HEAD_DIM, LANES)])


def _transpose_keys(k_ref, kt_ref, rows=512):
    def body(c, carry):
        off = pl.multiple_of(c * rows, rows)
        kt_ref[:, pl.ds(off, rows)] = k_ref[pl.ds(off, rows), :].astype(F32).T.astype(kt_ref.dtype)
        return carry
    lax.fori_loop(0, k_ref.shape[0] // rows, body, 0)


def _c_attn_kernel(q_ref, hm_ref, k_ref, v_ref, o_ref, kt_ref, m_ref, acc_ref, *, tq, tk):
    qi = pl.program_id(2)

    @pl.when(qi == 0)
    def _():
        _transpose_keys(k_ref, kt_ref)

    q = q_ref[...]
    qs = (q * hm_ref[0], q * hm_ref[1])
    _flash_init(m_ref, acc_ref)
    n_full = (qi * tq) // tk

    def chunk(c, masked):
        off = pl.multiple_of(c * tk, tk)
        kc = kt_ref[:, pl.ds(off, tk)]
        vc = _with_ones(v_ref[pl.ds(off, tk), :])
        for e in range(2):
            s = jnp.dot(qs[e], kc, preferred_element_type=F32)
            if masked:
                row = qi * tq + lax.broadcasted_iota(jnp.int32, s.shape, 0)
                col = c * tk + lax.broadcasted_iota(jnp.int32, s.shape, 1)
                s = jnp.where(col <= row, s, NEG)
            _flash_update(e, s, vc, m_ref, acc_ref)

    def body(c, carry):
        chunk(c, False)
        return carry

    lax.fori_loop(0, n_full, body, 0)
    chunk(n_full, True)
    _flash_finish(o_ref, acc_ref)


def _c_attn(qc, kc, vc, batch, seq, tq=512, tk=512):
    t = qc.shape[0]
    nq = seq // tq
    npair = C_HEADS // 2
    return pl.pallas_call(
        functools.partial(_c_attn_kernel, tq=tq, tk=tk),
        grid=(batch, npair, nq),
        in_specs=[pl.BlockSpec((tq, C_PAIR), lambda b, p, i: (b * nq + i, p)),
                  pl.BlockSpec((2, 1, C_PAIR), lambda b, p, i: (0, 0, 0)),
                  pl.BlockSpec((seq, C_PAIR), lambda b, p, i: (b, p)),
                  pl.BlockSpec((seq, LANES), lambda b, p, i: (b, p))],
        out_specs=pl.BlockSpec((tq, LANES), lambda b, p, i: (b * nq + i, p)),
        out_shape=jax.ShapeDtypeStruct((t, BRANCH_WIDTH), BF16),
        scratch_shapes=[pltpu.VMEM((C_PAIR, seq), BF16),
                        pltpu.VMEM((2, tq, LANES), F32), pltpu.VMEM((2, tq, 2 * LANES), F32)],
        compiler_params=_cparams(("parallel", "parallel", "arbitrary")),
        name="c_attn",
    )(qc, _head_masks(C_PAIR, [(0, C_NOPE), (2 * C_NOPE, 2 * C_NOPE + C_ROPE)],
                      [(C_NOPE, 2 * C_NOPE), (2 * C_NOPE + C_ROPE, 2 * C_NOPE + 2 * C_ROPE)]), kc, vc)


def _a_attn_kernel(q_ref, hm_ref, k_ref, v_ref, sel_ref, bias_ref, o_ref, kt_ref, m_ref, acc_ref,
                   *, tq, tk):
    qi = pl.program_id(2)

    @pl.when(qi == 0)
    def _():
        _transpose_keys(k_ref, kt_ref)

    q = q_ref[...]
    qs = (q * hm_ref[0], q * hm_ref[1])
    _flash_init(m_ref, acc_ref)
    n_chunks = ((qi + 1) * tq + tk - 1) // tk

    def body(c, carry):
        off = pl.multiple_of(c * tk, tk)
        kc = kt_ref[:, pl.ds(off, tk)]
        vc = _with_ones(v_ref[pl.ds(off, tk), :])
        selmask = sel_ref[:, pl.ds(off, tk)].astype(F32)
        for e in range(2):
            s = jnp.dot(qs[e], kc, preferred_element_type=F32)
            rows = []
            for i in range(tq // BLOCK):
                tiles = []
                for j in range(tk // BLOCK):
                    d = (qi * (tq // BLOCK) + i) - (c * (tk // BLOCK) + j)
                    d = jnp.clip(d, 0, A_BIAS_TILES - 1)
                    tiles.append(bias_ref[e, d])
                rows.append(jnp.concatenate(tiles, axis=1))
            s = s + jnp.concatenate(rows, axis=0) + selmask
            _flash_update(e, s, vc, m_ref, acc_ref)
        return carry

    lax.fori_loop(0, n_chunks, body, 0)
    _flash_finish(o_ref, acc_ref)


def _a_attn(pn, pp, sel, bias_tiles, batch, seq, tq=512, tk=1024):
    t = pn.shape[0]
    nq = seq // tq
    npair = A_HEADS // 2
    qcol, kcol, vcol = PN_AQ // LANES, PN_AK // LANES, PP_AV // LANES
    return pl.pallas_call(
        functools.partial(_a_attn_kernel, tq=tq, tk=tk),
        grid=(batch, npair, nq),
        in_specs=[pl.BlockSpec((tq, LANES), lambda b, p, i: (b * nq + i, qcol + p)),
                  pl.BlockSpec((2, 1, LANES), lambda b, p, i: (0, 0, 0)),
                  pl.BlockSpec((seq, LANES), lambda b, p, i: (b, kcol + p)),
                  pl.BlockSpec((seq, LANES), lambda b, p, i: (b, vcol + p)),
                  pl.BlockSpec((tq, seq), lambda b, p, i: (b * nq + i, 0)),
                  pl.BlockSpec((2, A_BIAS_TILES, BLOCK, BLOCK), lambda b, p, i: (p, 0, 0, 0))],
        out_specs=pl.BlockSpec((tq, LANES), lambda b, p, i: (b * nq + i, p)),
        out_shape=jax.ShapeDtypeStruct((t, BRANCH_WIDTH), BF16),
        scratch_shapes=[pltpu.VMEM((LANES, seq), BF16),
                        pltpu.VMEM((2, tq, LANES), F32), pltpu.VMEM((2, tq, 2 * LANES), F32)],
        compiler_params=_cparams(("parallel", "parallel", "arbitrary")),
        name="a_attn",
    )(pn, _pair_masks(), pn, pp, sel, bias_tiles)


def _sortable_key(score):
    bits = pltpu.bitcast(score, jnp.int32)
    return bits ^ ((bits >> 31) & jnp.int32(0x7FFFFFFF))


SEL_ROWS = 512
CNT_ROWS = SEL_ROWS
GMAX_ROWS = 256


def _a_select_kernel(qblk_ref, kblk_ref, ph_ref, pl_ref, pkh_ref, pkl_ref, sel_ref,
                     ikx_ref, keys_ref, gmax_ref, iqt_ref, jcut_ref, *, seq, k_sel):
    qi = pl.program_id(1)
    n_sel = (qi * BLOCK + BLOCK + SEL_ROWS - 1) // SEL_ROWS
    n_cnt = n_sel

    @pl.when(qi == 0)
    def _():
        def prep(c, carry):
            off = pl.multiple_of(c * 512, 512)
            kv = kblk_ref[pl.ds(off, 512), :]
            hi = kv.astype(BF16)
            lo = (kv - hi.astype(F32)).astype(BF16)
            ikx_ref[pl.ds(off, 512), :] = (
                jnp.dot(hi, pkh_ref[...], preferred_element_type=F32)
                + jnp.dot(lo, pkl_ref[...], preferred_element_type=F32)).astype(BF16)
            return carry
        lax.fori_loop(0, seq // 512, prep, 0)

    qb = qblk_ref[...]
    iq = qb[:, :IDX_HEADS * IDX_DIM]
    iq_hi = iq.astype(BF16)
    iq_lo = (iq - iq_hi.astype(F32)).astype(BF16)
    iqx = (jnp.dot(iq_hi, ph_ref[...], preferred_element_type=F32)
           + jnp.dot(iq_lo, pl_ref[...], preferred_element_type=F32))
    for h in range(IDX_HEADS):
        iqt_ref[h // 2, :, (h % 2) * LANES:(h % 2 + 1) * LANES] = iqx[:, h * LANES:(h + 1) * LANES].T.astype(BF16)
    iw_t = qb[:, IDX_HEADS * IDX_DIM:IDX_HEADS * IDX_DIM + LANES].T
    iw_rows = [iw_t[IDX_DIM + h:IDX_DIM + h + 1, :] for h in range(IDX_HEADS)]
    q_pos = qi * BLOCK + lax.broadcasted_iota(jnp.int32, (SEL_ROWS, BLOCK), 1)
    k_row = lax.broadcasted_iota(jnp.int32, (SEL_ROWS, BLOCK), 0)

    def score_chunk(c, masked):
        off = pl.multiple_of(c * SEL_ROWS, SEL_ROWS)
        kx = ikx_ref[pl.ds(off, SEL_ROWS), :]
        sc = jnp.zeros((SEL_ROWS, BLOCK), F32)
        for hp in range(IDX_HEADS // 2):
            xx = jnp.dot(kx, iqt_ref[hp], preferred_element_type=F32)
            sc = (sc + jnp.maximum(xx[:, :LANES], 0.0) * iw_rows[2 * hp]
                  + jnp.maximum(xx[:, LANES:], 0.0) * iw_rows[2 * hp + 1])
        key = _sortable_key(sc + 0.0)
        if masked:
            key = jnp.where(off + k_row <= q_pos, key, INT_MIN)
        keys_ref[pl.ds(off, SEL_ROWS), :] = key
        gm = gmax_ref[...]
        for j in range(SEL_ROWS // GMAX_ROWS):
            gm = jnp.maximum(gm, key[j * GMAX_ROWS:(j + 1) * GMAX_ROWS])
        gmax_ref[...] = gm

    def score_body(c, carry):
        score_chunk(c, False)
        return carry

    gmax_ref[...] = jnp.full(gmax_ref.shape, INT_MIN, jnp.int32)
    lax.fori_loop(0, n_sel - 1, score_body, 0)
    score_chunk(n_sel - 1, True)

    def count_rows(pred):
        def cnt_chunk(c, acc):
            off = pl.multiple_of(c * CNT_ROWS, CNT_ROWS)
            hit = jnp.where(pred(keys_ref[pl.ds(off, CNT_ROWS), :], off), 1, 0)
            return acc + jnp.sum(hit.reshape(CNT_ROWS // 8, 8, BLOCK), axis=0)
        acc = lax.fori_loop(0, n_cnt, cnt_chunk, jnp.zeros((8, BLOCK), jnp.int32))
        return jnp.sum(acc, axis=0, keepdims=True)

    def count_ge(cand):
        return count_rows(lambda keys, off: keys >= cand)

    gm = gmax_ref[...]
    k_lo = jnp.min(gm, axis=0, keepdims=True)
    k_hi = jnp.max(gm, axis=0, keepdims=True)
    span = k_hi - k_lo
    nbits = jnp.max(jnp.where(span < 0, 32, 32 - lax.clz(span)))

    def bisect(it, lo):
        cand = lo + lax.shift_left(jnp.int32(1), nbits - 1 - it)
        ok = (cand > lo) & (cand <= k_hi)
        return jnp.where(ok & (count_ge(cand) >= k_sel), cand, lo)

    thr = lax.fori_loop(0, nbits, bisect, k_lo)

    has_k = thr > INT_MIN
    thr = jnp.maximum(thr, INT_MIN + 1)

    excess = jnp.where(has_k, count_ge(thr) - k_sel, 0)
    jcut_ref[...] = jnp.full(jcut_ref.shape, 2 ** 31 - 1, jnp.int32)

    @pl.when(jnp.max(excess) > 0)
    def _():
        need = jnp.where(excess > 0, k_sel - count_ge(thr + 1), 1)
        rev_row = (seq - 1) - lax.broadcasted_iota(jnp.int32, (CNT_ROWS, BLOCK), 0)

        def tie_bisect(it, lo):
            cand = lo + lax.shift_left(jnp.int32(1), jnp.int32(seq.bit_length() - 2) - it)
            cnt = count_rows(lambda keys, off: (keys == thr) & (rev_row - off >= cand))
            return jnp.where(cnt >= need, cand, lo)

        rev = lax.fori_loop(0, seq.bit_length() - 1, tie_bisect, jnp.zeros((1, BLOCK), jnp.int32))
        jcut = jnp.where(excess > 0, (seq - 1) - rev, 2 ** 31 - 1)
        jcut_ref[...] = jnp.broadcast_to(jcut, jcut_ref.shape)

    jcut = jcut_ref[0:1, :]

    def emit(c, carry):
        off = pl.multiple_of(c * SEL_ROWS, SEL_ROWS)
        keys = keys_ref[pl.ds(off, SEL_ROWS), :]
        bar = jnp.where(off + k_row > jcut, thr + 1, thr)
        add = jnp.where(keys >= bar, 0.0, NEG)
        sel_ref[:, pl.ds(off, SEL_ROWS)] = jnp.concatenate(
            [add[j * BLOCK:(j + 1) * BLOCK].T for j in range(SEL_ROWS // BLOCK)], axis=1).astype(BF16)
        return carry

    lax.fori_loop(0, n_sel, emit, 0)

    def fill(c, carry):
        off = pl.multiple_of(c * SEL_ROWS, SEL_ROWS)
        sel_ref[:, pl.ds(off, SEL_ROWS)] = jnp.full((BLOCK, SEL_ROWS), NEG, BF16)
        return carry

    lax.fori_loop(n_sel, seq // SEL_ROWS, fill, 0)


def _a_select(pf, batch, seq):
    t = pf.shape[0]
    nq = seq // BLOCK
    k_sel = min(TOPK_MAX, seq // 4)
    nlane = IDX_HEADS * IDX_DIM
    r = np.arange(nlane)
    ph = np.zeros((nlane, IDX_HEADS * LANES), np.float32)
    plo = np.zeros((nlane, IDX_HEADS * LANES), np.float32)
    ph[r, (r // IDX_DIM) * LANES + r % IDX_DIM] = 1
    ph[r, (r // IDX_DIM) * LANES + 2 * IDX_DIM + r % IDX_DIM] = 1
    plo[r, (r // IDX_DIM) * LANES + IDX_DIM + r % IDX_DIM] = 1
    d = np.arange(IDX_DIM)
    pkh = np.zeros((LANES, LANES), np.float32)
    pkl = np.zeros((LANES, LANES), np.float32)
    pkh[d, d] = 1
    pkh[d, IDX_DIM + d] = 1
    pkl[d, 2 * IDX_DIM + d] = 1
    full = lambda shape: pl.BlockSpec(shape, lambda b, i: (0,) * len(shape))
    return pl.pallas_call(
        functools.partial(_a_select_kernel, seq=seq, k_sel=k_sel),
        grid=(batch, nq),
        in_specs=[pl.BlockSpec((BLOCK, 512), lambda b, i: (b * nq + i, 0)),
                  pl.BlockSpec((seq, LANES), lambda b, i: (b, nlane // LANES)),
                  full(ph.shape), full(plo.shape), full(pkh.shape), full(pkl.shape)],
        out_specs=pl.BlockSpec((BLOCK, seq), lambda b, i: (b * nq + i, 0)),
        out_shape=jax.ShapeDtypeStruct((t, seq), BF16),
        scratch_shapes=[pltpu.VMEM((seq, LANES), BF16), pltpu.VMEM((seq, LANES), jnp.int32),
                        pltpu.VMEM((GMAX_ROWS, LANES), jnp.int32),
                        pltpu.VMEM((IDX_HEADS // 2, LANES, 2 * LANES), BF16),
                        pltpu.VMEM((8, LANES), jnp.int32)],
        compiler_params=_cparams(("parallel", "arbitrary")),
        name="a_select",
    )(pf, pf, jnp.asarray(ph, BF16), jnp.asarray(plo, BF16), jnp.asarray(pkh, BF16),
      jnp.asarray(pkl, BF16))


BAND_RBLK = 4


def _band_kernel(*refs, has_sink, want_lse, kv_div, rblk):
    if has_sink:
        sink_ref, refs = refs[0], refs[1:]
    q_ref, hm_ref, kp_ref, kc_ref, vp_ref, vc_ref, bias_ref, o_ref = refs[:8]
    first = pl.program_id(1) == 0
    lane = lax.broadcasted_iota(jnp.int32, (BLOCK, LANES), 1)
    col = lax.broadcasted_iota(jnp.int32, (BLOCK, 2 * BLOCK), 1)
    no_prev = jnp.logical_and(first, col < BLOCK)
    for p in range(BRANCH_WIDTH // LANES):
        kcol = (p // kv_div) * LANES
        kk = jnp.concatenate([kp_ref[:, kcol:kcol + LANES], kc_ref[:, kcol:kcol + LANES]], axis=0)
        vv = jnp.concatenate([vp_ref[:, kcol:kcol + LANES], vc_ref[:, kcol:kcol + LANES]], axis=0)
        for i in range(rblk):
            rows = slice(i * BLOCK, (i + 1) * BLOCK)
            q = q_ref[rows, p * LANES:(p + 1) * LANES]
            kblk = kk[i * BLOCK:(i + 2) * BLOCK]
            vblk = vv[i * BLOCK:(i + 2) * BLOCK]
            outs, lses = [], []
            for e in range(2):
                s = lax.dot_general(q * hm_ref[e], kblk, _CONTRACT_LANES, preferred_element_type=F32)
                s = s + bias_ref[2 * p + e]
                if i == 0:
                    s = jnp.where(no_prev, NEG, s)
                m = jnp.max(s, axis=1, keepdims=True)
                if has_sink:
                    sk = sink_ref[2 * p + e]
                    m = jnp.maximum(m, sk)
                pr = jnp.exp(s - m)
                den = jnp.sum(pr, axis=1, keepdims=True)
                if has_sink:
                    den = den + jnp.exp(sk - m)
                outs.append(jnp.dot(pr.astype(BF16), vblk, preferred_element_type=F32) / den)
                lses.append(m + jnp.log(den))
            o_ref[rows, p * LANES:(p + 1) * LANES] = jnp.where(lane < HEAD_DIM, outs[0], outs[1]).astype(o_ref.dtype)
            if want_lse:
                refs[8][rows, p * LANES:(p + 1) * LANES] = jnp.where(lane < HEAD_DIM, lses[0], lses[1])


def _band_attn(q_arr, k_arr, v_arr, bias, nsub, sub_len, qcb, kcb, vcb, kv_w, sinks=None, want_lse=False):
    t = q_arr.shape[0]
    rblk = min(BAND_RBLK, sub_len // BLOCK)
    step_rows = rblk * BLOCK
    nbs = sub_len // step_rows
    kv_div = BRANCH_WIDTH // kv_w

    def cur(u, n):
        return u * nbs + n

    def prev(u, n):
        return jnp.where(n == 0, u * nbs * rblk, (u * nbs + n) * rblk - 1)

    in_specs = [
        pl.BlockSpec((step_rows, BRANCH_WIDTH), lambda u, n: (cur(u, n), qcb)),
        pl.BlockSpec((2, 1, LANES), lambda u, n: (0, 0, 0)),
        pl.BlockSpec((BLOCK, kv_w), lambda u, n: (prev(u, n), kcb)),
        pl.BlockSpec((step_rows, kv_w), lambda u, n: (cur(u, n), kcb)),
        pl.BlockSpec((BLOCK, kv_w), lambda u, n: (prev(u, n), vcb)),
        pl.BlockSpec((step_rows, kv_w), lambda u, n: (cur(u, n), vcb)),
        pl.BlockSpec(bias.shape, lambda u, n: (0, 0, 0)),
    ]
    args = [q_arr, _pair_masks(), k_arr, k_arr, v_arr, v_arr, bias]
    if sinks is not None:
        in_specs = [pl.BlockSpec(memory_space=pltpu.SMEM)] + in_specs
        args = [sinks] + args
    o_spec = pl.BlockSpec((step_rows, BRANCH_WIDTH), lambda u, n: (cur(u, n), 0))
    out_specs = [o_spec]
    out_shape = [jax.ShapeDtypeStruct((t, BRANCH_WIDTH), BF16)]
    if want_lse:
        out_specs.append(o_spec)
        out_shape.append(jax.ShapeDtypeStruct((t, BRANCH_WIDTH), F32))
    return pl.pallas_call(
        functools.partial(_band_kernel, has_sink=sinks is not None, want_lse=want_lse, kv_div=kv_div, rblk=rblk),
        grid=(nsub, nbs),
        in_specs=in_specs,
        out_specs=out_specs,
        out_shape=out_shape,
        compiler_params=_cparams(("parallel", "arbitrary")),
        name="band_attn",
    )(*args)


def _band_bias(table, step, max_dist):
    rel = BLOCK + np.arange(BLOCK)[:, None] - np.arange(2 * BLOCK)[None, :]
    bias = table[_t5_bucket(jnp.asarray(rel * step))].transpose(2, 0, 1).astype(F32)
    ok = (rel >= 0) & (rel <= max_dist)
    return jnp.where(jnp.asarray(ok)[None], bias, NEG)


def _merge_kernel(x_ref, ya_ref, yb0_ref, yb1_ref, yb2_ref, l0_ref, l1_ref, l2_ref, yc_ref, yd_ref,
                  za_ref, zb_ref, zc_ref, zd_ref, ga_ref, gb_ref, gc_ref, gd_ref, wb_ref, wo_ref, o_ref,
                  y1_ref, y2_ref, s1_ref, s2_ref):
    ncol = BRANCH_WIDTH // LANES
    for src, dst in ((yb1_ref, y1_ref), (yb2_ref, y2_ref), (l1_ref, s1_ref), (l2_ref, s2_ref)):
        dil = src.shape[1]
        for r in range(dil):
            blk = src[0, r].astype(F32)
            for c in range(ncol):
                dst[c, pl.ds(r, src.shape[2], stride=dil), :] = blk[:, c * LANES:(c + 1) * LANES]
    wide = lambda ref: jnp.concatenate([ref[c] for c in range(ncol)], axis=1)
    l0, l1, l2 = l0_ref[0, 0], wide(s1_ref), wide(s2_ref)
    mx = jnp.maximum(jnp.maximum(l0, l1), l2)
    w0, w1, w2 = jnp.exp(l0 - mx), jnp.exp(l1 - mx), jnp.exp(l2 - mx)
    yb = (w0 * yb0_ref[0, 0].astype(F32) + w1 * wide(y1_ref) + w2 * wide(y2_ref)) / (w0 + w1 + w2)
    ys = (ya_ref[...].astype(F32), yb, yc_ref[...].astype(F32), yd_ref[...].astype(F32))
    zs = (za_ref, zb_ref, zc_ref, zd_ref)
    gs = (ga_ref, gb_ref, gc_ref, gd_ref)
    merged = jnp.zeros(o_ref.shape, F32)
    for n in range(N_BRANCH):
        z = zs[n][...].astype(F32)
        u = (ys[n] * (z * jax.nn.sigmoid(z))).astype(BF16)
        proj = jnp.dot(u, wb_ref[n], preferred_element_type=F32)
        merged = merged + jax.nn.sigmoid(gs[n][...].astype(F32)) * proj
    o_ref[...] = x_ref[...] + jnp.dot(merged.astype(BF16), wo_ref[...], preferred_element_type=F32)


def _merge(x2, ya, yb, lb, yc, yd, pp, wb, wo, batch, seq, tm=512):
    t = x2.shape[0]
    bw = BRANCH_WIDTH
    nsb = seq // tm
    row = lambda w, c: pl.BlockSpec((tm, w), lambda i: (i, c))

    def sub(dil):
        return pl.BlockSpec((1, dil, tm // dil, bw), lambda i: (i // nsb, 0, i % nsb, 0))

    dils = [dil for _, dil in B_GROUPS]
    yb = [a.reshape(batch, dil, seq // dil, bw) for a, dil in zip(yb, dils)]
    lb = [a.reshape(batch, dil, seq // dil, bw) for a, dil in zip(lb, dils)]
    in_specs = ([row(D_MODEL, 0), row(bw, 0)] + [sub(dil) for dil in dils] * 2 + [row(bw, 0), row(bw, 0)]
                + [row(bw, PP_AZ // bw), row(bw, PP_BZ // bw), row(bw, PP_CZ // bw), row(bw, PP_DZ // bw)]
                + [row(D_MODEL, PP_G // D_MODEL + n) for n in range(N_BRANCH)]
                + [pl.BlockSpec((N_BRANCH, bw, D_MODEL), lambda i: (0, 0, 0)),
                   pl.BlockSpec((D_MODEL, D_MODEL), lambda i: (0, 0))])
    return pl.pallas_call(
        _merge_kernel,
        grid=(t // tm,),
        in_specs=in_specs,
        out_specs=row(D_MODEL, 0),
        out_shape=jax.ShapeDtypeStruct((t, D_MODEL), F32),
        scratch_shapes=[pltpu.VMEM((bw // LANES, tm, LANES), F32)] * 4,
        compiler_params=_cparams(("parallel",)),
        name="merge",
    )(x2, ya, yb[0], yb[1], yb[2], lb[0], lb[1], lb[2], yc, yd, pp, pp, pp, pp, pp, pp, pp, pp, wb, wo)


def _layer_weights(w_in, qk_gain_a, qk_gain_b, qk_gain_d):
    def cols(start, width):
        return w_in[:, :, start:start + width]

    n_bq = len(B_GROUPS) * B_HEADS * HEAD_DIM
    dk = cols(_D0 + 512, 128)
    dv = cols(_D0 + 640, 128)
    dup = lambda a: jnp.concatenate([a[:, :, :64], a[:, :, :64], a[:, :, 64:], a[:, :, 64:]], axis=-1)
    w_pn = jnp.concatenate([cols(_A0, 512), cols(_A0 + 512, 512), cols(_D0, 512), dup(dk), dup(dv)],
                           axis=-1).astype(BF16)
    w_pp = jnp.concatenate([cols(_G0, N_BRANCH * D_MODEL), cols(_A0 + 1024, 512), cols(_A0 + 1536, 512),
                            cols(_B0 + 3 * n_bq, 512), cols(_C0 + 416, 512), cols(_D0 + 768, 512)],
                           axis=-1).astype(BF16)
    w_bg = [jnp.concatenate([cols(_B0 + g * 512, 512), cols(_B0 + n_bq + g * 512, 512),
                             cols(_B0 + 2 * n_bq + g * 512, 512)], axis=-1).astype(BF16)
            for g in range(len(B_GROUPS))]
    depth = w_in.shape[0]
    zeros = lambda w: jnp.zeros((depth, D_MODEL, w), w_in.dtype)
    w_pf = jnp.concatenate([cols(_A0 + 2048, 296), zeros(216), cols(_C0, 416), zeros(96)], axis=-1).astype(BF16)

    scale = HEAD_DIM ** -0.5
    tile = lambda g, reps: jnp.tile(g, (1, reps))
    hg = jnp.concatenate([tile(qk_gain_a[:, 0] * scale, 8), tile(qk_gain_a[:, 1], 8),
                          tile(qk_gain_d[:, 0] * scale, 8), tile(qk_gain_d[:, 1], 4),
                          jnp.ones((depth, 256), F32)], axis=-1)
    flag = jnp.concatenate([jnp.ones((PN_DV,), F32), jnp.zeros((PN_W - PN_DV,), F32)])
    hg_b = jnp.concatenate([tile(qk_gain_b[:, 0] * scale, 8), tile(qk_gain_b[:, 1], 8),
                            jnp.ones((depth, 512), F32)], axis=-1)
    flag_b = jnp.concatenate([jnp.ones((2 * BRANCH_WIDTH,), F32), jnp.zeros((BRANCH_WIDTH,), F32)])
    return w_pn, w_pp, w_bg, w_pf, hg, flag, hg_b, flag_b


def _c_weights(qk_gain_c, w_q_b, w_kv_b, seq):
    src_q, src_k, gain_idx, head, pe_src, rope_j, rope_half = _c_layout_tables()
    take = lambda w, src: jnp.where(jnp.asarray(src >= 0), jnp.take(w, jnp.asarray(np.maximum(src, 0)), axis=-1), 0.0)
    wq = take(w_q_b, src_q).astype(BF16)
    wk = take(w_kv_b, src_k).astype(BF16)
    v_src = np.array([h * (C_NOPE + C_V) + C_NOPE + c for h in range(C_HEADS) for c in range(C_V)])
    wv = jnp.take(w_kv_b, jnp.asarray(v_src), axis=-1).astype(BF16)
    ppe = np.zeros((LANES, C_QK_W), np.float32)
    ln = np.nonzero(pe_src >= 0)[0]
    ppe[pe_src[ln], ln] = 1
    grp = ((head[:, None] == head[None, :]) & (head[:, None] >= 0)).astype(np.float32)
    g_take = lambda g: jnp.where(jnp.asarray(gain_idx >= 0), jnp.take(g, jnp.asarray(np.maximum(gain_idx, 0)), axis=-1), 0.0)
    gq = g_take(qk_gain_c[:, 0]) * (C_NOPE + C_ROPE) ** -0.5
    gk = g_take(qk_gain_c[:, 1])
    freq = ROPE_THETA ** (-jnp.arange(C_ROPE_HALF, dtype=F32) / C_ROPE_HALF)
    ang = jnp.arange(seq).astype(F32)[:, None] * freq[None, :]
    cos_j, sin_j = jnp.cos(ang), jnp.sin(ang)
    is_rope = jnp.asarray(rope_j >= 0)
    jj = jnp.asarray(np.maximum(rope_j, 0))
    cos = jnp.where(is_rope, cos_j[:, jj], 1.0)
    sin = jnp.where(is_rope, sin_j[:, jj], 0.0)
    s1 = jnp.where(jnp.asarray(rope_half == 0), -sin, 0.0)
    s2 = jnp.where(jnp.asarray(rope_half == 1), sin, 0.0)
    return wq, wk, wv, jnp.asarray(ppe, BF16), jnp.asarray(grp, BF16), gq, gk, cos, s1, s2


def _a_bias_tiles(table):
    a = np.arange(BLOCK)
    dist = (np.arange(A_BIAS_TILES)[:, None, None] * BLOCK + a[None, :, None] - a[None, None, :])
    return table[_t5_bucket(jnp.asarray(dist))].transpose(3, 0, 1, 2).astype(F32)


def kernel(x, norm_gain, w_in, qk_gain_a, qk_gain_b, qk_gain_c, qk_gain_d, c_q_gain, c_kv_gain,
           w_q_b, w_kv_b, sinks, rel_bias, w_branch, w_out):
    batch, seq, d_model = x.shape
    depth = w_in.shape[0]
    t = batch * seq
    x2 = x.reshape(t, d_model)

    w_pn, w_pp, w_bg, w_pf, hg, flag, hg_b, flag_b = _layer_weights(w_in, qk_gain_a, qk_gain_b, qk_gain_d)
    wq, wk, wv, ppe, grp, gq, gk, cos, s1, s2 = _c_weights(qk_gain_c, w_q_b, w_kv_b, seq)
    wb = w_branch.astype(BF16)
    wo = w_out.astype(BF16)

    bias_a = _a_bias_tiles(rel_bias[:, :A_HEADS])
    bias_b = [_band_bias(rel_bias[:, A_HEADS + g * B_HEADS:A_HEADS + (g + 1) * B_HEADS], dil, window // dil)
              for g, (window, dil) in enumerate(B_GROUPS)]
    bias_d = _band_bias(rel_bias[:, N_BIAS_HEADS - D_HEADS:], 1, D_WINDOW - 1)

    for l in range(depth):
        pn = _proj(x2, norm_gain[l], w_pn[l], BF16, head_gain=hg[l], flag=flag)
        pp = _proj(x2, norm_gain[l], w_pp[l], BF16)
        pf = _proj(x2, norm_gain[l], w_pf[l], F32)

        sel = _a_select(pf, batch, seq)
        ya = _a_attn(pn, pp, sel, bias_a, batch, seq)

        yb, lb = [], []
        for g, (window, dil) in enumerate(B_GROUPS):
            bg = _proj(x2, norm_gain[l], w_bg[g][l], BF16, head_gain=hg_b[l], flag=flag_b, dil=dil, seq=seq)
            o, lse = _band_attn(bg, bg, bg, bias_b[g], batch * dil, seq // dil, 0, 1, 2, BRANCH_WIDTH,
                                want_lse=True)
            yb.append(o)
            lb.append(lse)

        qc, kc, vc = _c_prep(pf, c_q_gain[l].reshape(1, -1), c_kv_gain[l].reshape(1, -1), wq[l], wk[l], wv[l],
                             ppe, grp, gq[l].reshape(1, -1), gk[l].reshape(1, -1), cos, s1, s2, seq)
        yc = _c_attn(qc, kc, vc, batch, seq)

        (yd,) = _band_attn(pn, pn, pn, bias_d, batch, seq, PN_DQ // BRANCH_WIDTH, PN_DK // 256, PN_DV // 256,
                           256, sinks=sinks[l])

        x2 = _merge(x2, ya, yb, lb, yc, yd, pp, wb[l], wo[l], batch, seq)
    return x2.reshape(batch, seq, d_model)
```

```python
import functools
import math

import numpy as np
import jax
import jax.numpy as jnp
from jax import lax
from jax.experimental import pallas as pl
from jax.experimental.pallas import tpu as pltpu

F32 = jnp.float32
BF16 = jnp.bfloat16

D_MODEL = 1024
BLOCK = 128
HEAD_DIM = 64
N_BRANCH = 4
BRANCH_WIDTH = 512
EPS = 1e-6
A_HEADS = 8
IDX_HEADS = 8
IDX_DIM = 32
TOPK_MAX = 256
B_GROUPS = ((128, 1), (512, 4), (2048, 16))
B_HEADS = 8
C_HEADS = 8
C_NOPE = 64
C_ROPE = 32
C_V = 64
C_Q_LORA = 256
C_KV_LORA = 128
ROPE_THETA = 10000.0
D_HEADS = 8
D_KV_HEADS = 2
D_WINDOW = 128
NUM_BUCKETS = 32
MAX_DISTANCE = 2048
N_BIAS_HEADS = A_HEADS + len(B_GROUPS) * B_HEADS + D_HEADS

LANES = 128
NEG = -1e30
INT_MIN = -(2 ** 31)

_A0 = 0
_B0 = 2344
_C0 = 7464
_D0 = 8392
_G0 = 9672

PN_AQ, PN_AK, PN_DQ, PN_DK, PN_DV, PN_W = 0, 512, 1024, 1536, 1792, 2048
PP_G, PP_AV, PP_AZ, PP_BZ, PP_CZ, PP_DZ, PP_W = 0, 4096, 4608, 5120, 5632, 6144, 6656
PF_W = 1024
BG_W = 3 * BRANCH_WIDTH

A_BIAS_TILES = MAX_DISTANCE // BLOCK + 2


def _cparams(sem, vmem_mb=48):
    return pltpu.CompilerParams(dimension_semantics=sem, vmem_limit_bytes=vmem_mb * 1024 * 1024)


def _t5_bucket(dist):
    max_exact = NUM_BUCKETS // 2
    d = jnp.maximum(dist, 0)
    logd = jnp.log(jnp.maximum(d, 1).astype(F32) / max_exact)
    large = max_exact + (logd / math.log(MAX_DISTANCE / max_exact) * (NUM_BUCKETS - max_exact)).astype(jnp.int32)
    return jnp.where(d < max_exact, d, jnp.minimum(large, NUM_BUCKETS - 1))


def _proj_kernel(x_ref, g_ref, w_ref, *rest, norm, dil, norm_tiles):
    rest = list(rest)
    strided = dil is not None and dil > 1
    xn_ref = rest.pop(-2) if strided else rest.pop(-1)
    res_ref = rest.pop(-1) if strided else None
    if norm:
        hg_ref, flag_ref, bd_ref, o_ref = rest
    else:
        (o_ref,) = rest

    @pl.when(pl.program_id(1) == 0)
    def _():
        x = x_ref[...]
        ms = jnp.mean(x * x, axis=-1, keepdims=True)
        xn_ref[...] = (x * lax.rsqrt(ms + EPS) * g_ref[...]).astype(BF16)

    def emit(h):
        if dil is None:
            o_ref[...] = h.astype(o_ref.dtype)
        elif dil == 1:
            o_ref[0, 0] = h.astype(o_ref.dtype)
        else:
            sub = h.shape[0] // dil
            for c in range(h.shape[1] // LANES):
                res_ref[c] = h[:, c * LANES:(c + 1) * LANES]
            for r in range(dil):
                o_ref[0, r] = jnp.concatenate(
                    [res_ref[c, pl.ds(r, sub, stride=dil), :] for c in range(h.shape[1] // LANES)],
                    axis=1).astype(o_ref.dtype)

    h = jnp.dot(xn_ref[...], w_ref[...], preferred_element_type=F32)
    if not norm:
        emit(h)
    else:
        @pl.when(pl.program_id(1) < norm_tiles)
        def _():
            ss = jnp.dot((h * h).astype(BF16), bd_ref[...], preferred_element_type=F32)
            scale = lax.rsqrt(ss * (1.0 / HEAD_DIM) + EPS) * hg_ref[...]
            emit(h * jnp.where(flag_ref[...] > 0, scale, 1.0))

        @pl.when(pl.program_id(1) >= norm_tiles)
        def _():
            emit(h)


def _proj(x2, gain, w, out_dtype, head_gain=None, flag=None, norm_cols=0, dil=None, seq=None, tm=1024, tn=512):
    norm_tiles = -(-norm_cols // tn)
    t, d = x2.shape
    n = w.shape[1]
    norm = head_gain is not None
    scratch = [pltpu.VMEM((tm, d), BF16)]
    if dil is None:
        out_spec = pl.BlockSpec((tm, tn), lambda i, j: (i, j))
        out_shape = jax.ShapeDtypeStruct((t, n), out_dtype)
        kdil = None
    else:
        nsb = seq // tm
        out_spec = pl.BlockSpec((1, dil, tm // dil, tn), lambda i, j: (i // nsb, 0, i % nsb, j))
        out_shape = jax.ShapeDtypeStruct((t // seq, dil, seq // dil, n), out_dtype)
        kdil = dil
        if dil > 1:
            scratch.append(pltpu.VMEM((tn // LANES, tm, LANES), F32))
    in_specs = [pl.BlockSpec((tm, d), lambda i, j: (i, 0)),
                pl.BlockSpec((1, d), lambda i, j: (0, 0)),
                pl.BlockSpec((d, tn), lambda i, j: (0, j))]
    args = [x2, gain.reshape(1, d), w]
    if norm:
        lane = np.arange(tn)
        bd = jnp.asarray((lane[:, None] // HEAD_DIM == lane[None, :] // HEAD_DIM), BF16)
        in_specs += [pl.BlockSpec((1, tn), lambda i, j: (0, j)),
                     pl.BlockSpec((1, tn), lambda i, j: (0, j)),
                     pl.BlockSpec((tn, tn), lambda i, j: (0, 0))]
        args += [head_gain.reshape(1, n), flag.reshape(1, n), bd]
    out = pl.pallas_call(
        functools.partial(_proj_kernel, norm=norm, dil=kdil, norm_tiles=norm_tiles),
        grid=(t // tm, n // tn),
        in_specs=in_specs,
        out_specs=out_spec,
        out_shape=out_shape,
        scratch_shapes=scratch,
        compiler_params=_cparams(("parallel", "arbitrary")),
        name=("proj_norm" if norm else "proj") + ("" if dil is None else "_dil%d" % dil),
    )(*args)
    return out.reshape(t, n)


C_PAIR = 256
C_QK_W = (C_HEADS // 2) * C_PAIR
C_ROPE_HALF = C_ROPE // 2


def _c_lane(h, c):
    base = (h // 2) * C_PAIR
    e = h % 2
    if c < C_NOPE:
        return base + e * C_NOPE + c
    return base + 2 * C_NOPE + e * C_ROPE + (c - C_NOPE)


def _c_layout_tables():
    src_q = np.full(C_QK_W, -1, np.int64)
    src_k = np.full(C_QK_W, -1, np.int64)
    gain_idx = np.full(C_QK_W, -1, np.int64)
    head = np.full(C_QK_W, -1, np.int64)
    pe_src = np.full(C_QK_W, -1, np.int64)
    rope_j = np.full(C_QK_W, -1, np.int64)
    rope_half = np.zeros(C_QK_W, np.int64)
    for h in range(C_HEADS):
        for c in range(C_NOPE + C_ROPE):
            ln = _c_lane(h, c)
            src_q[ln] = h * (C_NOPE + C_ROPE) + c
            gain_idx[ln] = c
            head[ln] = h
            if c < C_NOPE:
                src_k[ln] = h * (C_NOPE + C_V) + c
            else:
                r = c - C_NOPE
                pe_src[ln] = r
                rope_j[ln] = r % C_ROPE_HALF
                rope_half[ln] = r // C_ROPE_HALF
    return src_q, src_k, gain_idx, head, pe_src, rope_j, rope_half


def _c_prep_kernel(pf_ref, cqg_ref, ckvg_ref, wq_ref, wk_ref, wv_ref, ppe_ref, grp_ref,
                   gq_ref, gk_ref, cos_ref, s1_ref, s2_ref, q_ref, k_ref, v_ref):
    blk = pf_ref[...]
    cq = blk[:, :C_Q_LORA]
    ckv = blk[:, C_Q_LORA:C_Q_LORA + C_KV_LORA]
    pe = blk[:, C_Q_LORA + C_KV_LORA:]

    def rms(v, g):
        return v * lax.rsqrt(jnp.mean(v * v, axis=-1, keepdims=True) + EPS) * g

    def head_norm_rope(raw, gain):
        ss = jnp.dot((raw * raw).astype(BF16), grp_ref[...], preferred_element_type=F32)
        y = raw * lax.rsqrt(ss * (1.0 / (C_NOPE + C_ROPE)) + EPS) * gain
        up = pltpu.roll(y, C_QK_W - C_ROPE_HALF, 1)
        dn = pltpu.roll(y, C_ROPE_HALF, 1)
        wide = lambda ref: jnp.concatenate([ref[...]] * (C_QK_W // C_PAIR), axis=1)
        return y * wide(cos_ref) + up * wide(s1_ref) + dn * wide(s2_ref)

    cqn = rms(cq, cqg_ref[...]).astype(BF16)
    q_raw = jnp.dot(cqn, wq_ref[...], preferred_element_type=F32)
    q_ref[...] = head_norm_rope(q_raw, gq_ref[...]).astype(BF16)

    ckvn = rms(ckv, ckvg_ref[...]).astype(BF16)
    pe_hi = pe.astype(BF16)
    pe_lo = (pe - pe_hi.astype(F32)).astype(BF16)
    k_raw = (jnp.dot(ckvn, wk_ref[...], preferred_element_type=F32)
             + jnp.dot(pe_hi, ppe_ref[...], preferred_element_type=F32)
             + jnp.dot(pe_lo, ppe_ref[...], preferred_element_type=F32))
    k_ref[...] = head_norm_rope(k_raw, gk_ref[...]).astype(BF16)
    v_ref[...] = jnp.dot(ckvn, wv_ref[...], preferred_element_type=F32).astype(BF16)


def _c_prep(pf, cqg, ckvg, wq, wk, wv, ppe, grp, gq, gk, cos, s1, s2, seq, tm=512):
    t = pf.shape[0]
    nsb = seq // tm
    full = lambda shape: pl.BlockSpec(shape, lambda i: (0,) * len(shape))
    tab = pl.BlockSpec((tm, C_PAIR), lambda i: (i % nsb, 0))
    return pl.pallas_call(
        _c_prep_kernel,
        grid=(t // tm,),
        in_specs=[pl.BlockSpec((tm, 512), lambda i: (i, 1)),
                  full((1, C_Q_LORA)), full((1, C_KV_LORA)),
                  full((C_Q_LORA, C_QK_W)), full((C_KV_LORA, C_QK_W)), full((C_KV_LORA, BRANCH_WIDTH)),
                  full((LANES, C_QK_W)), full((C_QK_W, C_QK_W)),
                  full((1, C_QK_W)), full((1, C_QK_W)), tab, tab, tab],
        out_specs=[pl.BlockSpec((tm, C_QK_W), lambda i: (i, 0)),
                   pl.BlockSpec((tm, C_QK_W), lambda i: (i, 0)),
                   pl.BlockSpec((tm, BRANCH_WIDTH), lambda i: (i, 0))],
        out_shape=[jax.ShapeDtypeStruct((t, C_QK_W), BF16),
                   jax.ShapeDtypeStruct((t, C_QK_W), BF16),
                   jax.ShapeDtypeStruct((t, BRANCH_WIDTH), BF16)],
        compiler_params=_cparams(("parallel",)),
        name="c_prep",
    )(pf, cqg, ckvg, wq, wk, wv, ppe, grp, gq, gk, cos, s1, s2)


def _flash_update(e, s, v_ones, m_ref, acc_ref):
    m_prev = m_ref[e]
    m_new = jnp.maximum(m_prev, jnp.max(s, axis=1, keepdims=True))
    alpha = jnp.exp(m_prev - m_new)
    p = jnp.exp(s - jnp.concatenate([m_new] * (s.shape[1] // LANES), axis=1))
    acc_ref[e] = (jnp.concatenate([alpha, alpha], axis=1) * acc_ref[e]
                  + jnp.dot(p.astype(BF16), v_ones, preferred_element_type=F32))
    m_ref[e] = m_new


def _flash_init(m_ref, acc_ref):
    m_ref[...] = jnp.full(m_ref.shape, NEG, F32)
    acc_ref[...] = jnp.zeros(acc_ref.shape, F32)


def _flash_finish(o_ref, acc_ref):
    lane = lax.broadcasted_iota(jnp.int32, o_ref.shape, 1)
    o0 = acc_ref[0, :, :LANES] / acc_ref[0, :, LANES:]
    o1 = acc_ref[1, :, :LANES] / acc_ref[1, :, LANES:]
    o_ref[...] = jnp.where(lane < HEAD_DIM, o0, o1).astype(o_ref.dtype)


def _with_ones(v_chunk):
    return jnp.concatenate([v_chunk, jnp.ones(v_chunk.shape, v_chunk.dtype)], axis=1)


_CONTRACT_LANES = (((1,), (1,)), ((), ()))


def _head_masks(width, ranges0, ranges1):
    m = np.zeros((2, 1, width), np.float32)
    for e, ranges in enumerate((ranges0, ranges1)):
        for lo, hi in ranges:
            m[e, 0, lo:hi] = 1
    return jnp.asarray(m, BF16)


def _pair_masks():
    return _head_masks(LANES, [(0, HEAD_DIM)], [(HEAD_DIM, LANES)])


def _transpose_keys(k_ref, kt_ref, rows=512):
    def body(c, carry):
        off = pl.multiple_of(c * rows, rows)
        kt_ref[:, pl.ds(off, rows)] = k_ref[pl.ds(off, rows), :].astype(F32).T.astype(kt_ref.dtype)
        return carry
    lax.fori_loop(0, k_ref.shape[0] // rows, body, 0)


def _c_attn_kernel(q_ref, hm_ref, k_ref, v_ref, o_ref, kt_ref, m_ref, acc_ref, *, tq, tk):
    qi = pl.program_id(2)

    @pl.when(qi == 0)
    def _():
        _transpose_keys(k_ref, kt_ref)

    q = q_ref[...]
    qs = (q * hm_ref[0], q * hm_ref[1])
    _flash_init(m_ref, acc_ref)
    n_full = (qi * tq) // tk

    def chunk(c, masked):
        off = pl.multiple_of(c * tk, tk)
        kc = kt_ref[:, pl.ds(off, tk)]
        vc = _with_ones(v_ref[pl.ds(off, tk), :])
        for e in range(2):
            s = jnp.dot(qs[e], kc, preferred_element_type=F32)
            if masked:
                row = qi * tq + lax.broadcasted_iota(jnp.int32, s.shape, 0)
                col = c * tk + lax.broadcasted_iota(jnp.int32, s.shape, 1)
                s = jnp.where(col <= row, s, NEG)
            _flash_update(e, s, vc, m_ref, acc_ref)

    def body(c, carry):
        chunk(c, False)
        return carry

    lax.fori_loop(0, n_full, body, 0)
    chunk(n_full, True)
    _flash_finish(o_ref, acc_ref)


def _c_attn(qc, kc, vc, batch, seq, tq=512, tk=512):
    t = qc.shape[0]
    nq = seq // tq
    npair = C_HEADS // 2
    return pl.pallas_call(
        functools.partial(_c_attn_kernel, tq=tq, tk=tk),
        grid=(batch, npair, nq),
        in_specs=[pl.BlockSpec((tq, C_PAIR), lambda b, p, i: (b * nq + i, p)),
                  pl.BlockSpec((2, 1, C_PAIR), lambda b, p, i: (0, 0, 0)),
                  pl.BlockSpec((seq, C_PAIR), lambda b, p, i: (b, p)),
                  pl.BlockSpec((seq, LANES), lambda b, p, i: (b, p))],
        out_specs=pl.BlockSpec((tq, LANES), lambda b, p, i: (b * nq + i, p)),
        out_shape=jax.ShapeDtypeStruct((t, BRANCH_WIDTH), BF16),
        scratch_shapes=[pltpu.VMEM((C_PAIR, seq), BF16),
                        pltpu.VMEM((2, tq, LANES), F32), pltpu.VMEM((2, tq, 2 * LANES), F32)],
        compiler_params=_cparams(("parallel", "parallel", "arbitrary")),
        name="c_attn",
    )(qc, _head_masks(C_PAIR, [(0, C_NOPE), (2 * C_NOPE, 2 * C_NOPE + C_ROPE)],
                      [(C_NOPE, 2 * C_NOPE), (2 * C_NOPE + C_ROPE, 2 * C_NOPE + 2 * C_ROPE)]), kc, vc)


def _a_attn_kernel(q_ref, hm_ref, k_ref, v_ref, sel_ref, bias_ref, o_ref, kt_ref, m_ref, acc_ref,
                   *, tq, tk):
    qi = pl.program_id(2)

    @pl.when(qi == 0)
    def _():
        _transpose_keys(k_ref, kt_ref)

    q = q_ref[...]
    qs = (q * hm_ref[0], q * hm_ref[1])
    _flash_init(m_ref, acc_ref)
    n_chunks = ((qi + 1) * tq + tk - 1) // tk

    def body(c, carry):
        off = pl.multiple_of(c * tk, tk)
        kc = kt_ref[:, pl.ds(off, tk)]
        vc = _with_ones(v_ref[pl.ds(off, tk), :])
        selmask = sel_ref[:, pl.ds(off, tk)].astype(F32)
        for e in range(2):
            s = jnp.dot(qs[e], kc, preferred_element_type=F32)
            rows = []
            for i in range(tq // BLOCK):
                tiles = []
                for j in range(tk // BLOCK):
                    d = (qi * (tq // BLOCK) + i) - (c * (tk // BLOCK) + j)
                    d = jnp.clip(d, 0, A_BIAS_TILES - 1)
                    tiles.append(bias_ref[e, d])
                rows.append(jnp.concatenate(tiles, axis=1))
            s = s + jnp.concatenate(rows, axis=0) + selmask
            _flash_update(e, s, vc, m_ref, acc_ref)
        return carry

    lax.fori_loop(0, n_chunks, body, 0)
    _flash_finish(o_ref, acc_ref)


def _a_attn(pn, pp, sel, bias_tiles, batch, seq, tq=512, tk=1024):
    t = pn.shape[0]
    nq = seq // tq
    npair = A_HEADS // 2
    qcol, kcol, vcol = PN_AQ // LANES, PN_AK // LANES, PP_AV // LANES
    return pl.pallas_call(
        functools.partial(_a_attn_kernel, tq=tq, tk=tk),
        grid=(batch, npair, nq),
        in_specs=[pl.BlockSpec((tq, LANES), lambda b, p, i: (b * nq + i, qcol + p)),
                  pl.BlockSpec((2, 1, LANES), lambda b, p, i: (0, 0, 0)),
                  pl.BlockSpec((seq, LANES), lambda b, p, i: (b, kcol + p)),
                  pl.BlockSpec((seq, LANES), lambda b, p, i: (b, vcol + p)),
                  pl.BlockSpec((tq, seq), lambda b, p, i: (b * nq + i, 0)),
                  pl.BlockSpec((2, A_BIAS_TILES, BLOCK, BLOCK), lambda b, p, i: (p, 0, 0, 0))],
        out_specs=pl.BlockSpec((tq, LANES), lambda b, p, i: (b * nq + i, p)),
        out_shape=jax.ShapeDtypeStruct((t, BRANCH_WIDTH), BF16),
        scratch_shapes=[pltpu.VMEM((LANES, seq), BF16),
                        pltpu.VMEM((2, tq, LANES), F32), pltpu.VMEM((2, tq, 2 * LANES), F32)],
        compiler_params=_cparams(("parallel", "parallel", "arbitrary")),
        name="a_attn",
    )(pn, _pair_masks(), pn, pp, sel, bias_tiles)


def _sortable_key(score):
    bits = pltpu.bitcast(score, jnp.int32)
    return bits ^ ((bits >> 31) & jnp.int32(0x7FFFFFFF))


SEL_ROWS = 512
CNT_ROWS = SEL_ROWS
GMAX_ROWS = 256


def _a_select_kernel(qblk_ref, kblk_ref, ph_ref, pl_ref, pkh_ref, pkl_ref, sel_ref,
                     ikx_ref, keys_ref, gmax_ref, iqt_ref, jcut_ref, *, seq, k_sel):
    qi = pl.program_id(1)
    n_sel = (qi * BLOCK + BLOCK + SEL_ROWS - 1) // SEL_ROWS
    n_cnt = n_sel

    @pl.when(qi == 0)
    def _():
        def prep(c, carry):
            off = pl.multiple_of(c * 512, 512)
            kv = kblk_ref[pl.ds(off, 512), :]
            hi = kv.astype(BF16)
            lo = (kv - hi.astype(F32)).astype(BF16)
            ikx_ref[pl.ds(off, 512), :] = (
                jnp.dot(hi, pkh_ref[...], preferred_element_type=F32)
                + jnp.dot(lo, pkl_ref[...], preferred_element_type=F32)).astype(BF16)
            return carry
        lax.fori_loop(0, seq // 512, prep, 0)

    qb = qblk_ref[...]
    iq = qb[:, :IDX_HEADS * IDX_DIM]
    iq_hi = iq.astype(BF16)
    iq_lo = (iq - iq_hi.astype(F32)).astype(BF16)
    iqx = (jnp.dot(iq_hi, ph_ref[...], preferred_element_type=F32)
           + jnp.dot(iq_lo, pl_ref[...], preferred_element_type=F32))
    for h in range(IDX_HEADS):
        iqt_ref[h // 2, :, (h % 2) * LANES:(h % 2 + 1) * LANES] = iqx[:, h * LANES:(h + 1) * LANES].T.astype(BF16)
    iw_t = qb[:, IDX_HEADS * IDX_DIM:IDX_HEADS * IDX_DIM + LANES].T
    iw_rows = [iw_t[IDX_DIM + h:IDX_DIM + h + 1, :] for h in range(IDX_HEADS)]
    q_pos = qi * BLOCK + lax.broadcasted_iota(jnp.int32, (SEL_ROWS, BLOCK), 1)
    k_row = lax.broadcasted_iota(jnp.int32, (SEL_ROWS, BLOCK), 0)

    def score_chunk(c, masked):
        off = pl.multiple_of(c * SEL_ROWS, SEL_ROWS)
        kx = ikx_ref[pl.ds(off, SEL_ROWS), :]
        sc = jnp.zeros((SEL_ROWS, BLOCK), F32)
        for hp in range(IDX_HEADS // 2):
            xx = jnp.dot(kx, iqt_ref[hp], preferred_element_type=F32)
            sc = (sc + jnp.maximum(xx[:, :LANES], 0.0) * iw_rows[2 * hp]
                  + jnp.maximum(xx[:, LANES:], 0.0) * iw_rows[2 * hp + 1])
        key = _sortable_key(sc + 0.0)
        if masked:
            key = jnp.where(off + k_row <= q_pos, key, INT_MIN)
        keys_ref[pl.ds(off, SEL_ROWS), :] = key
        gm = gmax_ref[...]
        for j in range(SEL_ROWS // GMAX_ROWS):
            gm = jnp.maximum(gm, key[j * GMAX_ROWS:(j + 1) * GMAX_ROWS])
        gmax_ref[...] = gm

    def score_body(c, carry):
        score_chunk(c, False)
        return carry

    gmax_ref[...] = jnp.full(gmax_ref.shape, INT_MIN, jnp.int32)
    lax.fori_loop(0, n_sel - 1, score_body, 0)
    score_chunk(n_sel - 1, True)

    def count_rows(pred):
        def cnt_chunk(c, acc):
            off = pl.multiple_of(c * CNT_ROWS, CNT_ROWS)
            hit = jnp.where(pred(keys_ref[pl.ds(off, CNT_ROWS), :], off), 1, 0)
            return acc + jnp.sum(hit.reshape(CNT_ROWS // 8, 8, BLOCK), axis=0)
        acc = lax.fori_loop(0, n_cnt, cnt_chunk, jnp.zeros((8, BLOCK), jnp.int32))
        return jnp.sum(acc, axis=0, keepdims=True)

    def count_ge(cand):
        return count_rows(lambda keys, off: keys >= cand)

    gm = gmax_ref[...]
    k_lo = jnp.min(gm, axis=0, keepdims=True)
    k_hi = jnp.max(gm, axis=0, keepdims=True)
    span = k_hi - k_lo
    nbits = jnp.max(jnp.where(span < 0, 32, 32 - lax.clz(span)))

    unknown = jnp.int32(2 ** 30)

    def bisect_cond(state):
        it, _, cnt_lo = state
        return (it < nbits) & (jnp.max(jnp.abs(cnt_lo - k_sel)) > 0)

    def bisect(state):
        it, lo, cnt_lo = state
        cand = lo + lax.shift_left(jnp.int32(1), nbits - 1 - it)
        cnt = count_ge(cand)
        ok = (cand > lo) & (cand <= k_hi) & (cnt >= k_sel)
        return it + 1, jnp.where(ok, cand, lo), jnp.where(ok, cnt, cnt_lo)

    _, thr, cnt_thr = lax.while_loop(bisect_cond, bisect,
                                     (jnp.int32(0), k_lo, jnp.full((1, BLOCK), unknown, jnp.int32)))

    has_k = thr > INT_MIN
    thr = jnp.maximum(thr, INT_MIN + 1)

    jcut_ref[...] = jnp.full(jcut_ref.shape, 2 ** 31 - 1, jnp.int32)

    @pl.when(jnp.max(jnp.where(has_k, cnt_thr, 0)) > k_sel)
    def _():
        excess = jnp.where(has_k, count_ge(thr) - k_sel, 0)
        need = jnp.where(excess > 0, k_sel - count_ge(thr + 1), 1)
        rev_row = (seq - 1) - lax.broadcasted_iota(jnp.int32, (CNT_ROWS, BLOCK), 0)

        def tie_bisect(it, lo):
            cand = lo + lax.shift_left(jnp.int32(1), jnp.int32(seq.bit_length() - 2) - it)
            cnt = count_rows(lambda keys, off: (keys == thr) & (rev_row - off >= cand))
            return jnp.where(cnt >= need, cand, lo)

        rev = lax.fori_loop(0, seq.bit_length() - 1, tie_bisect, jnp.zeros((1, BLOCK), jnp.int32))
        jcut = jnp.where(excess > 0, (seq - 1) - rev, 2 ** 31 - 1)
        jcut_ref[...] = jnp.broadcast_to(jcut, jcut_ref.shape)

    jcut = jcut_ref[0:1, :]

    def emit(c, carry):
        off = pl.multiple_of(c * SEL_ROWS, SEL_ROWS)
        keys = keys_ref[pl.ds(off, SEL_ROWS), :]
        bar = jnp.where(off + k_row > jcut, thr + 1, thr)
        add = jnp.where(keys >= bar, 0.0, NEG)
        sel_ref[:, pl.ds(off, SEL_ROWS)] = jnp.concatenate(
            [add[j * BLOCK:(j + 1) * BLOCK].T for j in range(SEL_ROWS // BLOCK)], axis=1).astype(BF16)
        return carry

    lax.fori_loop(0, n_sel, emit, 0)

    def fill(c, carry):
        off = pl.multiple_of(c * SEL_ROWS, SEL_ROWS)
        sel_ref[:, pl.ds(off, SEL_ROWS)] = jnp.full((BLOCK, SEL_ROWS), NEG, BF16)
        return carry

    lax.fori_loop(n_sel, seq // SEL_ROWS, fill, 0)


def _a_select(pf, batch, seq):
    t = pf.shape[0]
    nq = seq // BLOCK
    k_sel = min(TOPK_MAX, seq // 4)
    nlane = IDX_HEADS * IDX_DIM
    r = np.arange(nlane)
    ph = np.zeros((nlane, IDX_HEADS * LANES), np.float32)
    plo = np.zeros((nlane, IDX_HEADS * LANES), np.float32)
    ph[r, (r // IDX_DIM) * LANES + r % IDX_DIM] = 1
    ph[r, (r // IDX_DIM) * LANES + 2 * IDX_DIM + r % IDX_DIM] = 1
    plo[r, (r // IDX_DIM) * LANES + IDX_DIM + r % IDX_DIM] = 1
    d = np.arange(IDX_DIM)
    pkh = np.zeros((LANES, LANES), np.float32)
    pkl = np.zeros((LANES, LANES), np.float32)
    pkh[d, d] = 1
    pkh[d, IDX_DIM + d] = 1
    pkl[d, 2 * IDX_DIM + d] = 1
    full = lambda shape: pl.BlockSpec(shape, lambda b, i: (0,) * len(shape))
    return pl.pallas_call(
        functools.partial(_a_select_kernel, seq=seq, k_sel=k_sel),
        grid=(batch, nq),
        in_specs=[pl.BlockSpec((BLOCK, 512), lambda b, i: (b * nq + i, 0)),
                  pl.BlockSpec((seq, LANES), lambda b, i: (b, nlane // LANES)),
                  full(ph.shape), full(plo.shape), full(pkh.shape), full(pkl.shape)],
        out_specs=pl.BlockSpec((BLOCK, seq), lambda b, i: (b * nq + i, 0)),
        out_shape=jax.ShapeDtypeStruct((t, seq), BF16),
        scratch_shapes=[pltpu.VMEM((seq, LANES), BF16), pltpu.VMEM((seq, LANES), jnp.int32),
                        pltpu.VMEM((GMAX_ROWS, LANES), jnp.int32),
                        pltpu.VMEM((IDX_HEADS // 2, LANES, 2 * LANES), BF16),
                        pltpu.VMEM((8, LANES), jnp.int32)],
        compiler_params=_cparams(("parallel", "arbitrary")),
        name="a_select",
    )(pf, pf, jnp.asarray(ph, BF16), jnp.asarray(plo, BF16), jnp.asarray(pkh, BF16),
      jnp.asarray(pkl, BF16))


BAND_RBLK = 4


def _band_kernel(*refs, has_sink, want_lse, kv_div, rblk):
    if has_sink:
        sink_ref, refs = refs[0], refs[1:]
    q_ref, hm_ref, kp_ref, kc_ref, vp_ref, vc_ref, bias_ref, o_ref = refs[:8]
    first = pl.program_id(1) == 0
    lane = lax.broadcasted_iota(jnp.int32, (BLOCK, LANES), 1)
    col = lax.broadcasted_iota(jnp.int32, (BLOCK, 2 * BLOCK), 1)
    no_prev = jnp.logical_and(first, col < BLOCK)
    for p in range(BRANCH_WIDTH // LANES):
        kcol = (p // kv_div) * LANES
        kk = jnp.concatenate([kp_ref[:, kcol:kcol + LANES], kc_ref[:, kcol:kcol + LANES]], axis=0)
        vv = jnp.concatenate([vp_ref[:, kcol:kcol + LANES], vc_ref[:, kcol:kcol + LANES]], axis=0)
        for i in range(rblk):
            rows = slice(i * BLOCK, (i + 1) * BLOCK)
            q = q_ref[rows, p * LANES:(p + 1) * LANES]
            kblk = kk[i * BLOCK:(i + 2) * BLOCK]
            vblk = vv[i * BLOCK:(i + 2) * BLOCK]
            outs, lses = [], []
            for e in range(2):
                s = lax.dot_general(q * hm_ref[e], kblk, _CONTRACT_LANES, preferred_element_type=F32)
                s = s + bias_ref[2 * p + e]
                if i == 0:
                    s = jnp.where(no_prev, NEG, s)
                m = jnp.max(s, axis=1, keepdims=True)
                if has_sink:
                    sk = sink_ref[2 * p + e]
                    m = jnp.maximum(m, sk)
                pr = jnp.exp(s - m)
                den = jnp.sum(pr, axis=1, keepdims=True)
                if has_sink:
                    den = den + jnp.exp(sk - m)
                outs.append(jnp.dot(pr.astype(BF16), vblk, preferred_element_type=F32) / den)
                lses.append(m + jnp.log(den))
            o_ref[rows, p * LANES:(p + 1) * LANES] = jnp.where(lane < HEAD_DIM, outs[0], outs[1]).astype(o_ref.dtype)
            if want_lse:
                refs[8][rows, p * LANES:(p + 1) * LANES] = jnp.where(lane < HEAD_DIM, lses[0], lses[1])


def _band_attn(q_arr, k_arr, v_arr, bias, nsub, sub_len, qcb, kcb, vcb, kv_w, sinks=None, want_lse=False):
    t = q_arr.shape[0]
    rblk = min(BAND_RBLK, sub_len // BLOCK)
    step_rows = rblk * BLOCK
    nbs = sub_len // step_rows
    kv_div = BRANCH_WIDTH // kv_w

    def cur(u, n):
        return u * nbs + n

    def prev(u, n):
        return jnp.where(n == 0, u * nbs * rblk, (u * nbs + n) * rblk - 1)

    in_specs = [
        pl.BlockSpec((step_rows, BRANCH_WIDTH), lambda u, n: (cur(u, n), qcb)),
        pl.BlockSpec((2, 1, LANES), lambda u, n: (0, 0, 0)),
        pl.BlockSpec((BLOCK, kv_w), lambda u, n: (prev(u, n), kcb)),
        pl.BlockSpec((step_rows, kv_w), lambda u, n: (cur(u, n), kcb)),
        pl.BlockSpec((BLOCK, kv_w), lambda u, n: (prev(u, n), vcb)),
        pl.BlockSpec((step_rows, kv_w), lambda u, n: (cur(u, n), vcb)),
        pl.BlockSpec(bias.shape, lambda u, n: (0, 0, 0)),
    ]
    args = [q_arr, _pair_masks(), k_arr, k_arr, v_arr, v_arr, bias]
    if sinks is not None:
        in_specs = [pl.BlockSpec(memory_space=pltpu.SMEM)] + in_specs
        args = [sinks] + args
    o_spec = pl.BlockSpec((step_rows, BRANCH_WIDTH), lambda u, n: (cur(u, n), 0))
    out_specs = [o_spec]
    out_shape = [jax.ShapeDtypeStruct((t, BRANCH_WIDTH), BF16)]
    if want_lse:
        out_specs.append(o_spec)
        out_shape.append(jax.ShapeDtypeStruct((t, BRANCH_WIDTH), F32))
    return pl.pallas_call(
        functools.partial(_band_kernel, has_sink=sinks is not None, want_lse=want_lse, kv_div=kv_div, rblk=rblk),
        grid=(nsub, nbs),
        in_specs=in_specs,
        out_specs=out_specs,
        out_shape=out_shape,
        compiler_params=_cparams(("parallel", "arbitrary")),
        name="band_attn",
    )(*args)


def _band_bias(table, step, max_dist):
    rel = BLOCK + np.arange(BLOCK)[:, None] - np.arange(2 * BLOCK)[None, :]
    bias = table[_t5_bucket(jnp.asarray(rel * step))].transpose(2, 0, 1).astype(F32)
    ok = (rel >= 0) & (rel <= max_dist)
    return jnp.where(jnp.asarray(ok)[None], bias, NEG)


def _merge_kernel(x_ref, ya_ref, yb0_ref, yb1_ref, yb2_ref, l0_ref, l1_ref, l2_ref, yc_ref, yd_ref,
                  za_ref, zb_ref, zc_ref, zd_ref, ga_ref, gb_ref, gc_ref, gd_ref, wb_ref, wo_ref, o_ref,
                  y1_ref, y2_ref, s1_ref, s2_ref):
    ncol = BRANCH_WIDTH // LANES
    for src, dst in ((yb1_ref, y1_ref), (yb2_ref, y2_ref), (l1_ref, s1_ref), (l2_ref, s2_ref)):
        dil = src.shape[1]
        for r in range(dil):
            blk = src[0, r].astype(F32)
            for c in range(ncol):
                dst[c, pl.ds(r, src.shape[2], stride=dil), :] = blk[:, c * LANES:(c + 1) * LANES]
    wide = lambda ref: jnp.concatenate([ref[c] for c in range(ncol)], axis=1)
    l0, l1, l2 = l0_ref[0, 0], wide(s1_ref), wide(s2_ref)
    mx = jnp.maximum(jnp.maximum(l0, l1), l2)
    w0, w1, w2 = jnp.exp(l0 - mx), jnp.exp(l1 - mx), jnp.exp(l2 - mx)
    yb = (w0 * yb0_ref[0, 0].astype(F32) + w1 * wide(y1_ref) + w2 * wide(y2_ref)) / (w0 + w1 + w2)
    ys = (ya_ref[...].astype(F32), yb, yc_ref[...].astype(F32), yd_ref[...].astype(F32))
    zs = (za_ref, zb_ref, zc_ref, zd_ref)
    gs = (ga_ref, gb_ref, gc_ref, gd_ref)
    merged = jnp.zeros(o_ref.shape, F32)
    for n in range(N_BRANCH):
        z = zs[n][...].astype(F32)
        u = (ys[n] * (z * jax.nn.sigmoid(z))).astype(BF16)
        proj = jnp.dot(u, wb_ref[n], preferred_element_type=F32)
        merged = merged + jax.nn.sigmoid(gs[n][...].astype(F32)) * proj
    o_ref[...] = x_ref[...] + jnp.dot(merged.astype(BF16), wo_ref[...], preferred_element_type=F32)


def _merge(x2, ya, yb, lb, yc, yd, pp, wb, wo, batch, seq, tm=512):
    t = x2.shape[0]
    bw = BRANCH_WIDTH
    nsb = seq // tm
    row = lambda w, c: pl.BlockSpec((tm, w), lambda i: (i, c))

    def sub(dil):
        return pl.BlockSpec((1, dil, tm // dil, bw), lambda i: (i // nsb, 0, i % nsb, 0))

    dils = [dil for _, dil in B_GROUPS]
    yb = [a.reshape(batch, dil, seq // dil, bw) for a, dil in zip(yb, dils)]
    lb = [a.reshape(batch, dil, seq // dil, bw) for a, dil in zip(lb, dils)]
    in_specs = ([row(D_MODEL, 0), row(bw, 0)] + [sub(dil) for dil in dils] * 2 + [row(bw, 0), row(bw, 0)]
                + [row(bw, PP_AZ // bw), row(bw, PP_BZ // bw), row(bw, PP_CZ // bw), row(bw, PP_DZ // bw)]
                + [row(D_MODEL, PP_G // D_MODEL + n) for n in range(N_BRANCH)]
                + [pl.BlockSpec((N_BRANCH, bw, D_MODEL), lambda i: (0, 0, 0)),
                   pl.BlockSpec((D_MODEL, D_MODEL), lambda i: (0, 0))])
    return pl.pallas_call(
        _merge_kernel,
        grid=(t // tm,),
        in_specs=in_specs,
        out_specs=row(D_MODEL, 0),
        out_shape=jax.ShapeDtypeStruct((t, D_MODEL), F32),
        scratch_shapes=[pltpu.VMEM((bw // LANES, tm, LANES), F32)] * 4,
        compiler_params=_cparams(("parallel",)),
        name="merge",
    )(x2, ya, yb[0], yb[1], yb[2], lb[0], lb[1], lb[2], yc, yd, pp, pp, pp, pp, pp, pp, pp, pp, wb, wo)


def _layer_weights(w_in, qk_gain_a, qk_gain_b, qk_gain_d):
    def cols(start, width):
        return w_in[:, :, start:start + width]

    n_bq = len(B_GROUPS) * B_HEADS * HEAD_DIM
    dk = cols(_D0 + 512, 128)
    dv = cols(_D0 + 640, 128)
    dup = lambda a: jnp.concatenate([a[:, :, :64], a[:, :, :64], a[:, :, 64:], a[:, :, 64:]], axis=-1)
    w_pn = jnp.concatenate([cols(_A0, 512), cols(_A0 + 512, 512), cols(_D0, 512), dup(dk), dup(dv)],
                           axis=-1).astype(BF16)
    w_pp = jnp.concatenate([cols(_G0, N_BRANCH * D_MODEL), cols(_A0 + 1024, 512), cols(_A0 + 1536, 512),
                            cols(_B0 + 3 * n_bq, 512), cols(_C0 + 416, 512), cols(_D0 + 768, 512)],
                           axis=-1).astype(BF16)
    w_bg = [jnp.concatenate([cols(_B0 + g * 512, 512), cols(_B0 + n_bq + g * 512, 512),
                             cols(_B0 + 2 * n_bq + g * 512, 512)], axis=-1).astype(BF16)
            for g in range(len(B_GROUPS))]
    depth = w_in.shape[0]
    zeros = lambda w: jnp.zeros((depth, D_MODEL, w), w_in.dtype)
    w_pf = jnp.concatenate([cols(_A0 + 2048, 296), zeros(216), cols(_C0, 416), zeros(96)], axis=-1).astype(BF16)

    scale = HEAD_DIM ** -0.5
    tile = lambda g, reps: jnp.tile(g, (1, reps))
    hg = jnp.concatenate([tile(qk_gain_a[:, 0] * scale, 8), tile(qk_gain_a[:, 1], 8),
                          tile(qk_gain_d[:, 0] * scale, 8), tile(qk_gain_d[:, 1], 4),
                          jnp.ones((depth, 256), F32)], axis=-1)
    flag = jnp.concatenate([jnp.ones((PN_DV,), F32), jnp.zeros((PN_W - PN_DV,), F32)])
    hg_b = jnp.concatenate([tile(qk_gain_b[:, 0] * scale, 8), tile(qk_gain_b[:, 1], 8),
                            jnp.ones((depth, 512), F32)], axis=-1)
    flag_b = jnp.concatenate([jnp.ones((2 * BRANCH_WIDTH,), F32), jnp.zeros((BRANCH_WIDTH,), F32)])
    return w_pn, w_pp, w_bg, w_pf, hg, flag, hg_b, flag_b


def _c_weights(qk_gain_c, w_q_b, w_kv_b, seq):
    src_q, src_k, gain_idx, head, pe_src, rope_j, rope_half = _c_layout_tables()
    take = lambda w, src: jnp.where(jnp.asarray(src >= 0), jnp.take(w, jnp.asarray(np.maximum(src, 0)), axis=-1), 0.0)
    wq = take(w_q_b, src_q).astype(BF16)
    wk = take(w_kv_b, src_k).astype(BF16)
    v_src = np.array([h * (C_NOPE + C_V) + C_NOPE + c for h in range(C_HEADS) for c in range(C_V)])
    wv = jnp.take(w_kv_b, jnp.asarray(v_src), axis=-1).astype(BF16)
    ppe = np.zeros((LANES, C_QK_W), np.float32)
    ln = np.nonzero(pe_src >= 0)[0]
    ppe[pe_src[ln], ln] = 1
    grp = ((head[:, None] == head[None, :]) & (head[:, None] >= 0)).astype(np.float32)
    g_take = lambda g: jnp.where(jnp.asarray(gain_idx >= 0), jnp.take(g, jnp.asarray(np.maximum(gain_idx, 0)), axis=-1), 0.0)
    gq = g_take(qk_gain_c[:, 0]) * (C_NOPE + C_ROPE) ** -0.5
    gk = g_take(qk_gain_c[:, 1])
    freq = ROPE_THETA ** (-jnp.arange(C_ROPE_HALF, dtype=F32) / C_ROPE_HALF)
    ang = jnp.arange(seq).astype(F32)[:, None] * freq[None, :]
    cos_j, sin_j = jnp.cos(ang), jnp.sin(ang)
    place = np.zeros((C_ROPE_HALF, C_PAIR), np.float32)
    lanes = np.nonzero(rope_j[:C_PAIR] >= 0)[0]
    place[rope_j[lanes], lanes] = 1
    is_rope = jnp.asarray(rope_j[:C_PAIR] >= 0)
    hi = lax.Precision.HIGHEST
    cos = jnp.where(is_rope, jnp.dot(cos_j, jnp.asarray(place), precision=hi), 1.0)
    sin = jnp.dot(sin_j, jnp.asarray(place), precision=hi)
    s1 = jnp.where(jnp.asarray(rope_half[:C_PAIR] == 0), -sin, 0.0)
    s2 = jnp.where(jnp.asarray(rope_half[:C_PAIR] == 1), sin, 0.0)
    return wq, wk, wv, jnp.asarray(ppe, BF16), jnp.asarray(grp, BF16), gq, gk, cos, s1, s2


def _a_bias_tiles(table):
    a = np.arange(BLOCK)
    dist = (np.arange(A_BIAS_TILES)[:, None, None] * BLOCK + a[None, :, None] - a[None, None, :])
    return table[_t5_bucket(jnp.asarray(dist))].transpose(3, 0, 1, 2).astype(F32)


def kernel(x, norm_gain, w_in, qk_gain_a, qk_gain_b, qk_gain_c, qk_gain_d, c_q_gain, c_kv_gain,
           w_q_b, w_kv_b, sinks, rel_bias, w_branch, w_out):
    batch, seq, d_model = x.shape
    depth = w_in.shape[0]
    t = batch * seq
    x2 = x.reshape(t, d_model)

    w_pn, w_pp, w_bg, w_pf, hg, flag, hg_b, flag_b = _layer_weights(w_in, qk_gain_a, qk_gain_b, qk_gain_d)
    wq, wk, wv, ppe, grp, gq, gk, cos, s1, s2 = _c_weights(qk_gain_c, w_q_b, w_kv_b, seq)
    wb = w_branch.astype(BF16)
    wo = w_out.astype(BF16)

    bias_a = _a_bias_tiles(rel_bias[:, :A_HEADS])
    bias_b = [_band_bias(rel_bias[:, A_HEADS + g * B_HEADS:A_HEADS + (g + 1) * B_HEADS], dil, window // dil)
              for g, (window, dil) in enumerate(B_GROUPS)]
    bias_d = _band_bias(rel_bias[:, N_BIAS_HEADS - D_HEADS:], 1, D_WINDOW - 1)

    for l in range(depth):
        pn = _proj(x2, norm_gain[l], w_pn[l], BF16, head_gain=hg[l], flag=flag, norm_cols=PN_DV)
        pp = _proj(x2, norm_gain[l], w_pp[l], BF16)
        pf = _proj(x2, norm_gain[l], w_pf[l], F32)

        sel = _a_select(pf, batch, seq)
        ya = _a_attn(pn, pp, sel, bias_a, batch, seq)

        yb, lb = [], []
        for g, (window, dil) in enumerate(B_GROUPS):
            bg = _proj(x2, norm_gain[l], w_bg[g][l], BF16, head_gain=hg_b[l], flag=flag_b,
                       norm_cols=2 * BRANCH_WIDTH, dil=dil, seq=seq)
            o, lse = _band_attn(bg, bg, bg, bias_b[g], batch * dil, seq // dil, 0, 1, 2, BRANCH_WIDTH,
                                want_lse=True)
            yb.append(o)
            lb.append(lse)

        qc, kc, vc = _c_prep(pf, c_q_gain[l].reshape(1, -1), c_kv_gain[l].reshape(1, -1), wq[l], wk[l], wv[l],
                             ppe, grp, gq[l].reshape(1, -1), gk[l].reshape(1, -1), cos, s1, s2, seq)
        yc = _c_attn(qc, kc, vc, batch, seq)

        (yd,) = _band_attn(pn, pn, pn, bias_d, batch, seq, PN_DQ // BRANCH_WIDTH, PN_DK // 256, PN_DV // 256,
                           256, sinks=sinks[l])

        x2 = _merge(x2, ya, yb, lb, yc, yd, pp, wb[l], wo[l], batch, seq)
    return x2.reshape(batch, seq, d_model)
```

```python
import functools
import math

import numpy as np
import jax
import jax.numpy as jnp
from jax import lax
from jax.experimental import pallas as pl
from jax.experimental.pallas import tpu as pltpu

F32 = jnp.float32
BF16 = jnp.bfloat16

D_MODEL = 1024
BLOCK = 128
HEAD_DIM = 64
N_BRANCH = 4
BRANCH_WIDTH = 512
EPS = 1e-6
A_HEADS = 8
IDX_HEADS = 8
IDX_DIM = 32
TOPK_MAX = 256
B_GROUPS = ((128, 1), (512, 4), (2048, 16))
B_HEADS = 8
C_HEADS = 8
C_NOPE = 64
C_ROPE = 32
C_V = 64
C_Q_LORA = 256
C_KV_LORA = 128
ROPE_THETA = 10000.0
D_HEADS = 8
D_KV_HEADS = 2
D_WINDOW = 128
NUM_BUCKETS = 32
MAX_DISTANCE = 2048
N_BIAS_HEADS = A_HEADS + len(B_GROUPS) * B_HEADS + D_HEADS

LANES = 128
NEG = -1e30
INT_MIN = -(2 ** 31)

_A0 = 0
_B0 = 2344
_C0 = 7464
_D0 = 8392
_G0 = 9672

PN_AQ, PN_AK, PN_DQ, PN_DK, PN_DV, PN_W = 0, 512, 1024, 1536, 1792, 2048
PP_G, PP_AV, PP_AZ, PP_BZ, PP_CZ, PP_DZ, PP_W = 0, 4096, 4608, 5120, 5632, 6144, 6656
PF_W = 1024
BG_W = 3 * BRANCH_WIDTH

A_BIAS_TILES = MAX_DISTANCE // BLOCK + 2


def _cparams(sem, vmem_mb=48):
    return pltpu.CompilerParams(dimension_semantics=sem, vmem_limit_bytes=vmem_mb * 1024 * 1024)


def _t5_bucket(dist):
    max_exact = NUM_BUCKETS // 2
    d = jnp.maximum(dist, 0)
    logd = jnp.log(jnp.maximum(d, 1).astype(F32) / max_exact)
    large = max_exact + (logd / math.log(MAX_DISTANCE / max_exact) * (NUM_BUCKETS - max_exact)).astype(jnp.int32)
    return jnp.where(d < max_exact, d, jnp.minimum(large, NUM_BUCKETS - 1))


def _proj_kernel(x_ref, g_ref, w_ref, *rest, norm, dil, norm_tiles):
    rest = list(rest)
    strided = dil is not None and dil > 1
    xn_ref = rest.pop(-2) if strided else rest.pop(-1)
    res_ref = rest.pop(-1) if strided else None
    if norm:
        hg_ref, flag_ref, bd_ref, o_ref = rest
    else:
        (o_ref,) = rest

    @pl.when(pl.program_id(1) == 0)
    def _():
        x = x_ref[...]
        ms = jnp.mean(x * x, axis=-1, keepdims=True)
        xn_ref[...] = (x * lax.rsqrt(ms + EPS) * g_ref[...]).astype(BF16)

    def emit(h):
        if dil is None:
            o_ref[...] = h.astype(o_ref.dtype)
        elif dil == 1:
            o_ref[0, 0] = h.astype(o_ref.dtype)
        else:
            sub = h.shape[0] // dil
            for c in range(h.shape[1] // LANES):
                res_ref[c] = h[:, c * LANES:(c + 1) * LANES]
            for r in range(dil):
                o_ref[0, r] = jnp.concatenate(
                    [res_ref[c, pl.ds(r, sub, stride=dil), :] for c in range(h.shape[1] // LANES)],
                    axis=1).astype(o_ref.dtype)

    h = jnp.dot(xn_ref[...], w_ref[...], preferred_element_type=F32)
    if not norm:
        emit(h)
    else:
        @pl.when(pl.program_id(1) < norm_tiles)
        def _():
            ss = jnp.dot((h * h).astype(BF16), bd_ref[...], preferred_element_type=F32)
            scale = lax.rsqrt(ss * (1.0 / HEAD_DIM) + EPS) * hg_ref[...]
            emit(h * jnp.where(flag_ref[...] > 0, scale, 1.0))

        @pl.when(pl.program_id(1) >= norm_tiles)
        def _():
            emit(h)


def _proj(x2, gain, w, out_dtype, head_gain=None, flag=None, norm_cols=0, dil=None, seq=None, tm=1024, tn=512):
    norm_tiles = -(-norm_cols // tn)
    t, d = x2.shape
    n = w.shape[1]
    norm = head_gain is not None
    scratch = [pltpu.VMEM((tm, d), BF16)]
    if dil is None:
        out_spec = pl.BlockSpec((tm, tn), lambda i, j: (i, j))
        out_shape = jax.ShapeDtypeStruct((t, n), out_dtype)
        kdil = None
    else:
        nsb = seq // tm
        out_spec = pl.BlockSpec((1, dil, tm // dil, tn), lambda i, j: (i // nsb, 0, i % nsb, j))
        out_shape = jax.ShapeDtypeStruct((t // seq, dil, seq // dil, n), out_dtype)
        kdil = dil
        if dil > 1:
            scratch.append(pltpu.VMEM((tn // LANES, tm, LANES), F32))
    in_specs = [pl.BlockSpec((tm, d), lambda i, j: (i, 0)),
                pl.BlockSpec((1, d), lambda i, j: (0, 0)),
                pl.BlockSpec((d, tn), lambda i, j: (0, j))]
    args = [x2, gain.reshape(1, d), w]
    if norm:
        lane = np.arange(tn)
        bd = jnp.asarray((lane[:, None] // HEAD_DIM == lane[None, :] // HEAD_DIM), BF16)
        in_specs += [pl.BlockSpec((1, tn), lambda i, j: (0, j)),
                     pl.BlockSpec((1, tn), lambda i, j: (0, j)),
                     pl.BlockSpec((tn, tn), lambda i, j: (0, 0))]
        args += [head_gain.reshape(1, n), flag.reshape(1, n), bd]
    out = pl.pallas_call(
        functools.partial(_proj_kernel, norm=norm, dil=kdil, norm_tiles=norm_tiles),
        grid=(t // tm, n // tn),
        in_specs=in_specs,
        out_specs=out_spec,
        out_shape=out_shape,
        scratch_shapes=scratch,
        compiler_params=_cparams(("parallel", "arbitrary")),
        name=("proj_norm" if norm else "proj") + ("" if dil is None else "_dil%d" % dil),
    )(*args)
    return out.reshape(t, n)


C_PAIR = 256
C_QK_W = (C_HEADS // 2) * C_PAIR
C_ROPE_HALF = C_ROPE // 2


def _c_lane(h, c):
    base = (h // 2) * C_PAIR
    e = h % 2
    if c < C_NOPE:
        return base + e * C_NOPE + c
    return base + 2 * C_NOPE + e * C_ROPE + (c - C_NOPE)


def _c_layout_tables():
    src_q = np.full(C_QK_W, -1, np.int64)
    src_k = np.full(C_QK_W, -1, np.int64)
    gain_idx = np.full(C_QK_W, -1, np.int64)
    head = np.full(C_QK_W, -1, np.int64)
    pe_src = np.full(C_QK_W, -1, np.int64)
    rope_j = np.full(C_QK_W, -1, np.int64)
    rope_half = np.zeros(C_QK_W, np.int64)
    for h in range(C_HEADS):
        for c in range(C_NOPE + C_ROPE):
            ln = _c_lane(h, c)
            src_q[ln] = h * (C_NOPE + C_ROPE) + c
            gain_idx[ln] = c
            head[ln] = h
            if c < C_NOPE:
                src_k[ln] = h * (C_NOPE + C_V) + c
            else:
                r = c - C_NOPE
                pe_src[ln] = r
                rope_j[ln] = r % C_ROPE_HALF
                rope_half[ln] = r // C_ROPE_HALF
    return src_q, src_k, gain_idx, head, pe_src, rope_j, rope_half


def _c_prep_kernel(pf_ref, cqg_ref, ckvg_ref, wq_ref, wk_ref, wv_ref, ppe_ref, grp_ref,
                   gq_ref, gk_ref, cos_ref, s1_ref, s2_ref, q_ref, k_ref, v_ref):
    blk = pf_ref[...]
    cq = blk[:, :C_Q_LORA]
    ckv = blk[:, C_Q_LORA:C_Q_LORA + C_KV_LORA]
    pe = blk[:, C_Q_LORA + C_KV_LORA:]

    def rms(v, g):
        return v * lax.rsqrt(jnp.mean(v * v, axis=-1, keepdims=True) + EPS) * g

    def head_norm_rope(raw, gain):
        ss = jnp.dot((raw * raw).astype(BF16), grp_ref[...], preferred_element_type=F32)
        y = raw * lax.rsqrt(ss * (1.0 / (C_NOPE + C_ROPE)) + EPS) * gain
        up = pltpu.roll(y, C_QK_W - C_ROPE_HALF, 1)
        dn = pltpu.roll(y, C_ROPE_HALF, 1)
        wide = lambda ref: jnp.concatenate([ref[...]] * (C_QK_W // C_PAIR), axis=1)
        return y * wide(cos_ref) + up * wide(s1_ref) + dn * wide(s2_ref)

    cqn = rms(cq, cqg_ref[...]).astype(BF16)
    q_raw = jnp.dot(cqn, wq_ref[...], preferred_element_type=F32)
    q_ref[...] = head_norm_rope(q_raw, gq_ref[...]).astype(BF16)

    ckvn = rms(ckv, ckvg_ref[...]).astype(BF16)
    pe_hi = pe.astype(BF16)
    pe_lo = (pe - pe_hi.astype(F32)).astype(BF16)
    k_raw = (jnp.dot(ckvn, wk_ref[...], preferred_element_type=F32)
             + jnp.dot(pe_hi, ppe_ref[...], preferred_element_type=F32)
             + jnp.dot(pe_lo, ppe_ref[...], preferred_element_type=F32))
    k_ref[...] = head_norm_rope(k_raw, gk_ref[...]).astype(BF16)
    v_ref[...] = jnp.dot(ckvn, wv_ref[...], preferred_element_type=F32).astype(BF16)


def _c_prep(pf, cqg, ckvg, wq, wk, wv, ppe, grp, gq, gk, cos, s1, s2, seq, tm=512):
    t = pf.shape[0]
    nsb = seq // tm
    full = lambda shape: pl.BlockSpec(shape, lambda i: (0,) * len(shape))
    tab = pl.BlockSpec((tm, C_PAIR), lambda i: (i % nsb, 0))
    return pl.pallas_call(
        _c_prep_kernel,
        grid=(t // tm,),
        in_specs=[pl.BlockSpec((tm, 512), lambda i: (i, 1)),
                  full((1, C_Q_LORA)), full((1, C_KV_LORA)),
                  full((C_Q_LORA, C_QK_W)), full((C_KV_LORA, C_QK_W)), full((C_KV_LORA, BRANCH_WIDTH)),
                  full((LANES, C_QK_W)), full((C_QK_W, C_QK_W)),
                  full((1, C_QK_W)), full((1, C_QK_W)), tab, tab, tab],
        out_specs=[pl.BlockSpec((tm, C_QK_W), lambda i: (i, 0)),
                   pl.BlockSpec((tm, C_QK_W), lambda i: (i, 0)),
                   pl.BlockSpec((tm, BRANCH_WIDTH), lambda i: (i, 0))],
        out_shape=[jax.ShapeDtypeStruct((t, C_QK_W), BF16),
                   jax.ShapeDtypeStruct((t, C_QK_W), BF16),
                   jax.ShapeDtypeStruct((t, BRANCH_WIDTH), BF16)],
        compiler_params=_cparams(("parallel",)),
        name="c_prep",
    )(pf, cqg, ckvg, wq, wk, wv, ppe, grp, gq, gk, cos, s1, s2)


def _flash_update(e, s, v_ones, m_ref, acc_ref):
    m_prev = m_ref[e]
    m_new = jnp.maximum(m_prev, jnp.max(s, axis=1, keepdims=True))
    alpha = jnp.exp(m_prev - m_new)
    p = jnp.exp(s - jnp.concatenate([m_new] * (s.shape[1] // LANES), axis=1))
    acc_ref[e] = (jnp.concatenate([alpha, alpha], axis=1) * acc_ref[e]
                  + jnp.dot(p.astype(BF16), v_ones, preferred_element_type=F32))
    m_ref[e] = m_new


def _flash_init(m_ref, acc_ref):
    m_ref[...] = jnp.full(m_ref.shape, NEG, F32)
    acc_ref[...] = jnp.zeros(acc_ref.shape, F32)


def _flash_finish(o_ref, acc_ref):
    lane = lax.broadcasted_iota(jnp.int32, o_ref.shape, 1)
    o0 = acc_ref[0, :, :LANES] / acc_ref[0, :, LANES:]
    o1 = acc_ref[1, :, :LANES] / acc_ref[1, :, LANES:]
    o_ref[...] = jnp.where(lane < HEAD_DIM, o0, o1).astype(o_ref.dtype)


def _with_ones(v_chunk):
    return jnp.concatenate([v_chunk, jnp.ones(v_chunk.shape, v_chunk.dtype)], axis=1)


_CONTRACT_LANES = (((1,), (1,)), ((), ()))


def _head_masks(width, ranges0, ranges1):
    m = np.zeros((2, 1, width), np.float32)
    for e, ranges in enumerate((ranges0, ranges1)):
        for lo, hi in ranges:
            m[e, 0, lo:hi] = 1
    return jnp.asarray(m, BF16)


def _pair_masks():
    return _head_masks(LANES, [(0, HEAD_DIM)], [(HEAD_DIM, LANES)])


def _transpose_keys(k_ref, kt_ref, rows=512):
    def body(c, carry):
        off = pl.multiple_of(c * rows, rows)
        kt_ref[:, pl.ds(off, rows)] = k_ref[pl.ds(off, rows), :].astype(F32).T.astype(kt_ref.dtype)
        return carry
    lax.fori_loop(0, k_ref.shape[0] // rows, body, 0)


def _c_attn_kernel(q_ref, hm_ref, k_ref, v_ref, o_ref, kt_ref, m_ref, acc_ref, *, tq, tk):
    qi = pl.program_id(2)

    @pl.when(qi == 0)
    def _():
        _transpose_keys(k_ref, kt_ref)

    q = q_ref[...]
    qs = (q * hm_ref[0], q * hm_ref[1])
    _flash_init(m_ref, acc_ref)
    n_full = (qi * tq) // tk

    def chunk(c, masked):
        off = pl.multiple_of(c * tk, tk)
        kc = kt_ref[:, pl.ds(off, tk)]
        vc = _with_ones(v_ref[pl.ds(off, tk), :])
        for e in range(2):
            s = jnp.dot(qs[e], kc, preferred_element_type=F32)
            if masked:
                row = qi * tq + lax.broadcasted_iota(jnp.int32, s.shape, 0)
                col = c * tk + lax.broadcasted_iota(jnp.int32, s.shape, 1)
                s = jnp.where(col <= row, s, NEG)
            _flash_update(e, s, vc, m_ref, acc_ref)

    def body(c, carry):
        chunk(c, False)
        return carry

    lax.fori_loop(0, n_full, body, 0)
    chunk(n_full, True)
    _flash_finish(o_ref, acc_ref)


def _c_attn(qc, kc, vc, batch, seq, tq=512, tk=512):
    t = qc.shape[0]
    nq = seq // tq
    npair = C_HEADS // 2
    return pl.pallas_call(
        functools.partial(_c_attn_kernel, tq=tq, tk=tk),
        grid=(batch, npair, nq),
        in_specs=[pl.BlockSpec((tq, C_PAIR), lambda b, p, i: (b * nq + i, p)),
                  pl.BlockSpec((2, 1, C_PAIR), lambda b, p, i: (0, 0, 0)),
                  pl.BlockSpec((seq, C_PAIR), lambda b, p, i: (b, p)),
                  pl.BlockSpec((seq, LANES), lambda b, p, i: (b, p))],
        out_specs=pl.BlockSpec((tq, LANES), lambda b, p, i: (b * nq + i, p)),
        out_shape=jax.ShapeDtypeStruct((t, BRANCH_WIDTH), BF16),
        scratch_shapes=[pltpu.VMEM((C_PAIR, seq), BF16),
                        pltpu.VMEM((2, tq, LANES), F32), pltpu.VMEM((2, tq, 2 * LANES), F32)],
        compiler_params=_cparams(("parallel", "parallel", "arbitrary")),
        name="c_attn",
    )(qc, _head_masks(C_PAIR, [(0, C_NOPE), (2 * C_NOPE, 2 * C_NOPE + C_ROPE)],
                      [(C_NOPE, 2 * C_NOPE), (2 * C_NOPE + C_ROPE, 2 * C_NOPE + 2 * C_ROPE)]), kc, vc)


def _a_attn_kernel(q_ref, hm_ref, k_ref, v_ref, sel_ref, bias_ref, o_ref, kt_ref, m_ref, acc_ref,
                   *, tq, tk):
    qi = pl.program_id(2)

    @pl.when(qi == 0)
    def _():
        _transpose_keys(k_ref, kt_ref)

    q = q_ref[...]
    qs = (q * hm_ref[0], q * hm_ref[1])
    _flash_init(m_ref, acc_ref)
    n_chunks = ((qi + 1) * tq + tk - 1) // tk

    def body(c, carry):
        off = pl.multiple_of(c * tk, tk)
        kc = kt_ref[:, pl.ds(off, tk)]
        vc = _with_ones(v_ref[pl.ds(off, tk), :])
        selmask = sel_ref[:, pl.ds(off, tk)].astype(F32)
        for e in range(2):
            s = jnp.dot(qs[e], kc, preferred_element_type=F32)
            rows = []
            for i in range(tq // BLOCK):
                tiles = []
                for j in range(tk // BLOCK):
                    d = (qi * (tq // BLOCK) + i) - (c * (tk // BLOCK) + j)
                    d = jnp.clip(d, 0, A_BIAS_TILES - 1)
                    tiles.append(bias_ref[e, d])
                rows.append(jnp.concatenate(tiles, axis=1))
            s = s + jnp.concatenate(rows, axis=0) + selmask
            _flash_update(e, s, vc, m_ref, acc_ref)
        return carry

    lax.fori_loop(0, n_chunks, body, 0)
    _flash_finish(o_ref, acc_ref)


def _a_attn(pn, pp, sel, bias_tiles, batch, seq, tq=512, tk=1024):
    t = pn.shape[0]
    nq = seq // tq
    npair = A_HEADS // 2
    qcol, kcol, vcol = PN_AQ // LANES, PN_AK // LANES, PP_AV // LANES
    return pl.pallas_call(
        functools.partial(_a_attn_kernel, tq=tq, tk=tk),
        grid=(batch, npair, nq),
        in_specs=[pl.BlockSpec((tq, LANES), lambda b, p, i: (b * nq + i, qcol + p)),
                  pl.BlockSpec((2, 1, LANES), lambda b, p, i: (0, 0, 0)),
                  pl.BlockSpec((seq, LANES), lambda b, p, i: (b, kcol + p)),
                  pl.BlockSpec((seq, LANES), lambda b, p, i: (b, vcol + p)),
                  pl.BlockSpec((tq, seq), lambda b, p, i: (b * nq + i, 0)),
                  pl.BlockSpec((2, A_BIAS_TILES, BLOCK, BLOCK), lambda b, p, i: (p, 0, 0, 0))],
        out_specs=pl.BlockSpec((tq, LANES), lambda b, p, i: (b * nq + i, p)),
        out_shape=jax.ShapeDtypeStruct((t, BRANCH_WIDTH), BF16),
        scratch_shapes=[pltpu.VMEM((LANES, seq), BF16),
                        pltpu.VMEM((2, tq, LANES), F32), pltpu.VMEM((2, tq, 2 * LANES), F32)],
        compiler_params=_cparams(("parallel", "parallel", "arbitrary")),
        name="a_attn",
    )(pn, _pair_masks(), pn, pp, sel, bias_tiles)


def _sortable_key(score):
    bits = pltpu.bitcast(score, jnp.int32)
    return bits ^ ((bits >> 31) & jnp.int32(0x7FFFFFFF))


SEL_ROWS = 512
CNT_ROWS = SEL_ROWS
GMAX_ROWS = 256


def _a_select_kernel(qblk_ref, kblk_ref, ph_ref, pl_ref, pkh_ref, pkl_ref, sel_ref,
                     ikx_ref, keys_ref, gmax_ref, iqt_ref, jcut_ref, *, seq, k_sel):
    qi = pl.program_id(1)
    n_sel = (qi * BLOCK + BLOCK + SEL_ROWS - 1) // SEL_ROWS
    n_cnt = n_sel

    @pl.when(qi == 0)
    def _():
        def prep(c, carry):
            off = pl.multiple_of(c * 512, 512)
            kv = kblk_ref[pl.ds(off, 512), :]
            hi = kv.astype(BF16)
            lo = (kv - hi.astype(F32)).astype(BF16)
            ikx_ref[pl.ds(off, 512), :] = (
                jnp.dot(hi, pkh_ref[...], preferred_element_type=F32)
                + jnp.dot(lo, pkl_ref[...], preferred_element_type=F32)).astype(BF16)
            return carry
        lax.fori_loop(0, seq // 512, prep, 0)

    qb = qblk_ref[...]
    iq = qb[:, :IDX_HEADS * IDX_DIM]
    iq_hi = iq.astype(BF16)
    iq_lo = (iq - iq_hi.astype(F32)).astype(BF16)
    iqx = (jnp.dot(iq_hi, ph_ref[...], preferred_element_type=F32)
           + jnp.dot(iq_lo, pl_ref[...], preferred_element_type=F32))
    for h in range(IDX_HEADS):
        iqt_ref[h // 2, :, (h % 2) * LANES:(h % 2 + 1) * LANES] = iqx[:, h * LANES:(h + 1) * LANES].T.astype(BF16)
    iw_t = qb[:, IDX_HEADS * IDX_DIM:IDX_HEADS * IDX_DIM + LANES].T
    iw_rows = [iw_t[IDX_DIM + h:IDX_DIM + h + 1, :] for h in range(IDX_HEADS)]
    q_pos = qi * BLOCK + lax.broadcasted_iota(jnp.int32, (SEL_ROWS, BLOCK), 1)
    k_row = lax.broadcasted_iota(jnp.int32, (SEL_ROWS, BLOCK), 0)

    def score_chunk(c, masked):
        off = pl.multiple_of(c * SEL_ROWS, SEL_ROWS)
        kx = ikx_ref[pl.ds(off, SEL_ROWS), :]
        sc = jnp.zeros((SEL_ROWS, BLOCK), F32)
        for hp in range(IDX_HEADS // 2):
            xx = jnp.dot(kx, iqt_ref[hp], preferred_element_type=F32)
            sc = (sc + jnp.maximum(xx[:, :LANES], 0.0) * iw_rows[2 * hp]
                  + jnp.maximum(xx[:, LANES:], 0.0) * iw_rows[2 * hp + 1])
        key = _sortable_key(sc + 0.0)
        if masked:
            key = jnp.where(off + k_row <= q_pos, key, INT_MIN)
        keys_ref[pl.ds(off, SEL_ROWS), :] = key
        gm = gmax_ref[...]
        for j in range(SEL_ROWS // GMAX_ROWS):
            gm = jnp.maximum(gm, key[j * GMAX_ROWS:(j + 1) * GMAX_ROWS])
        gmax_ref[...] = gm

    def score_body(c, carry):
        score_chunk(c, False)
        return carry

    gmax_ref[...] = jnp.full(gmax_ref.shape, INT_MIN, jnp.int32)
    lax.fori_loop(0, n_sel - 1, score_body, 0)
    score_chunk(n_sel - 1, True)

    def count_rows(pred):
        def cnt_chunk(c, acc):
            off = pl.multiple_of(c * CNT_ROWS, CNT_ROWS)
            hit = jnp.where(pred(keys_ref[pl.ds(off, CNT_ROWS), :], off), 1, 0)
            return acc + jnp.sum(hit.reshape(CNT_ROWS // 8, 8, BLOCK), axis=0)
        acc = lax.fori_loop(0, n_cnt, cnt_chunk, jnp.zeros((8, BLOCK), jnp.int32))
        return jnp.sum(acc, axis=0, keepdims=True)

    def count_ge(cand):
        return count_rows(lambda keys, off: keys >= cand)

    gm = gmax_ref[...]
    k_lo = jnp.min(gm, axis=0, keepdims=True)
    k_hi = jnp.max(gm, axis=0, keepdims=True)
    span = k_hi - k_lo
    nbits = jnp.max(jnp.where(span < 0, 32, 32 - lax.clz(span)))

    unknown = jnp.int32(2 ** 30)

    def bisect(it, state):
        lo, cnt_lo = state
        cand = lo + lax.shift_left(jnp.int32(1), nbits - 1 - it)
        cnt = count_ge(cand)
        ok = (cand > lo) & (cand <= k_hi) & (cnt >= k_sel)
        return jnp.where(ok, cand, lo), jnp.where(ok, cnt, cnt_lo)

    thr, cnt_thr = lax.fori_loop(0, nbits, bisect, (k_lo, jnp.full((1, BLOCK), unknown, jnp.int32)))

    has_k = thr > INT_MIN
    thr = jnp.maximum(thr, INT_MIN + 1)

    jcut_ref[...] = jnp.full(jcut_ref.shape, 2 ** 31 - 1, jnp.int32)

    @pl.when(jnp.max(jnp.where(has_k, cnt_thr, 0)) > k_sel)
    def _():
        excess = jnp.where(has_k, count_ge(thr) - k_sel, 0)
        need = jnp.where(excess > 0, k_sel - count_ge(thr + 1), 1)
        rev_row = (seq - 1) - lax.broadcasted_iota(jnp.int32, (CNT_ROWS, BLOCK), 0)

        def tie_bisect(it, lo):
            cand = lo + lax.shift_left(jnp.int32(1), jnp.int32(seq.bit_length() - 2) - it)
            cnt = count_rows(lambda keys, off: (keys == thr) & (rev_row - off >= cand))
            return jnp.where(cnt >= need, cand, lo)

        rev = lax.fori_loop(0, seq.bit_length() - 1, tie_bisect, jnp.zeros((1, BLOCK), jnp.int32))
        jcut = jnp.where(excess > 0, (seq - 1) - rev, 2 ** 31 - 1)
        jcut_ref[...] = jnp.broadcast_to(jcut, jcut_ref.shape)

    jcut = jcut_ref[0:1, :]

    def emit(c, carry):
        off = pl.multiple_of(c * SEL_ROWS, SEL_ROWS)
        keys = keys_ref[pl.ds(off, SEL_ROWS), :]
        bar = jnp.where(off + k_row > jcut, thr + 1, thr)
        add = jnp.where(keys >= bar, 0.0, NEG)
        sel_ref[:, pl.ds(off, SEL_ROWS)] = jnp.concatenate(
            [add[j * BLOCK:(j + 1) * BLOCK].T for j in range(SEL_ROWS // BLOCK)], axis=1).astype(BF16)
        return carry

    lax.fori_loop(0, n_sel, emit, 0)

    def fill(c, carry):
        off = pl.multiple_of(c * SEL_ROWS, SEL_ROWS)
        sel_ref[:, pl.ds(off, SEL_ROWS)] = jnp.full((BLOCK, SEL_ROWS), NEG, BF16)
        return carry

    lax.fori_loop(n_sel, seq // SEL_ROWS, fill, 0)


def _a_select(pf, batch, seq):
    t = pf.shape[0]
    nq = seq // BLOCK
    k_sel = min(TOPK_MAX, seq // 4)
    nlane = IDX_HEADS * IDX_DIM
    r = np.arange(nlane)
    ph = np.zeros((nlane, IDX_HEADS * LANES), np.float32)
    plo = np.zeros((nlane, IDX_HEADS * LANES), np.float32)
    ph[r, (r // IDX_DIM) * LANES + r % IDX_DIM] = 1
    ph[r, (r // IDX_DIM) * LANES + 2 * IDX_DIM + r % IDX_DIM] = 1
    plo[r, (r // IDX_DIM) * LANES + IDX_DIM + r % IDX_DIM] = 1
    d = np.arange(IDX_DIM)
    pkh = np.zeros((LANES, LANES), np.float32)
    pkl = np.zeros((LANES, LANES), np.float32)
    pkh[d, d] = 1
    pkh[d, IDX_DIM + d] = 1
    pkl[d, 2 * IDX_DIM + d] = 1
    full = lambda shape: pl.BlockSpec(shape, lambda b, i: (0,) * len(shape))
    return pl.pallas_call(
        functools.partial(_a_select_kernel, seq=seq, k_sel=k_sel),
        grid=(batch, nq),
        in_specs=[pl.BlockSpec((BLOCK, 512), lambda b, i: (b * nq + i, 0)),
                  pl.BlockSpec((seq, LANES), lambda b, i: (b, nlane // LANES)),
                  full(ph.shape), full(plo.shape), full(pkh.shape), full(pkl.shape)],
        out_specs=pl.BlockSpec((BLOCK, seq), lambda b, i: (b * nq + i, 0)),
        out_shape=jax.ShapeDtypeStruct((t, seq), BF16),
        scratch_shapes=[pltpu.VMEM((seq, LANES), BF16), pltpu.VMEM((seq, LANES), jnp.int32),
                        pltpu.VMEM((GMAX_ROWS, LANES), jnp.int32),
                        pltpu.VMEM((IDX_HEADS // 2, LANES, 2 * LANES), BF16),
                        pltpu.VMEM((8, LANES), jnp.int32)],
        compiler_params=_cparams(("parallel", "arbitrary")),
        name="a_select",
    )(pf, pf, jnp.asarray(ph, BF16), jnp.asarray(plo, BF16), jnp.asarray(pkh, BF16),
      jnp.asarray(pkl, BF16))


BAND_RBLK = 4


def _band_kernel(*refs, has_sink, want_lse, kv_div, rblk):
    if has_sink:
        sink_ref, refs = refs[0], refs[1:]
    q_ref, hm_ref, kp_ref, kc_ref, vp_ref, vc_ref, bias_ref, o_ref = refs[:8]
    first = pl.program_id(1) == 0
    lane = lax.broadcasted_iota(jnp.int32, (BLOCK, LANES), 1)
    col = lax.broadcasted_iota(jnp.int32, (BLOCK, 2 * BLOCK), 1)
    no_prev = jnp.logical_and(first, col < BLOCK)
    for p in range(BRANCH_WIDTH // LANES):
        kcol = (p // kv_div) * LANES
        kk = jnp.concatenate([kp_ref[:, kcol:kcol + LANES], kc_ref[:, kcol:kcol + LANES]], axis=0)
        vv = jnp.concatenate([vp_ref[:, kcol:kcol + LANES], vc_ref[:, kcol:kcol + LANES]], axis=0)
        for i in range(rblk):
            rows = slice(i * BLOCK, (i + 1) * BLOCK)
            q = q_ref[rows, p * LANES:(p + 1) * LANES]
            kblk = kk[i * BLOCK:(i + 2) * BLOCK]
            vblk = vv[i * BLOCK:(i + 2) * BLOCK]
            outs, lses = [], []
            for e in range(2):
                s = lax.dot_general(q * hm_ref[e], kblk, _CONTRACT_LANES, preferred_element_type=F32)
                s = s + bias_ref[2 * p + e]
                if i == 0:
                    s = jnp.where(no_prev, NEG, s)
                m = jnp.max(s, axis=1, keepdims=True)
                if has_sink:
                    sk = sink_ref[2 * p + e]
                    m = jnp.maximum(m, sk)
                pr = jnp.exp(s - m)
                den = jnp.sum(pr, axis=1, keepdims=True)
                if has_sink:
                    den = den + jnp.exp(sk - m)
                outs.append(jnp.dot(pr.astype(BF16), vblk, preferred_element_type=F32) / den)
                lses.append(m + jnp.log(den))
            o_ref[rows, p * LANES:(p + 1) * LANES] = jnp.where(lane < HEAD_DIM, outs[0], outs[1]).astype(o_ref.dtype)
            if want_lse:
                refs[8][rows, p * LANES:(p + 1) * LANES] = jnp.where(lane < HEAD_DIM, lses[0], lses[1])


def _band_attn(q_arr, k_arr, v_arr, bias, nsub, sub_len, qcb, kcb, vcb, kv_w, sinks=None, want_lse=False):
    t = q_arr.shape[0]
    rblk = min(BAND_RBLK, sub_len // BLOCK)
    step_rows = rblk * BLOCK
    nbs = sub_len // step_rows
    kv_div = BRANCH_WIDTH // kv_w

    def cur(u, n):
        return u * nbs + n

    def prev(u, n):
        return jnp.where(n == 0, u * nbs * rblk, (u * nbs + n) * rblk - 1)

    in_specs = [
        pl.BlockSpec((step_rows, BRANCH_WIDTH), lambda u, n: (cur(u, n), qcb)),
        pl.BlockSpec((2, 1, LANES), lambda u, n: (0, 0, 0)),
        pl.BlockSpec((BLOCK, kv_w), lambda u, n: (prev(u, n), kcb)),
        pl.BlockSpec((step_rows, kv_w), lambda u, n: (cur(u, n), kcb)),
        pl.BlockSpec((BLOCK, kv_w), lambda u, n: (prev(u, n), vcb)),
        pl.BlockSpec((step_rows, kv_w), lambda u, n: (cur(u, n), vcb)),
        pl.BlockSpec(bias.shape, lambda u, n: (0, 0, 0)),
    ]
    args = [q_arr, _pair_masks(), k_arr, k_arr, v_arr, v_arr, bias]
    if sinks is not None:
        in_specs = [pl.BlockSpec(memory_space=pltpu.SMEM)] + in_specs
        args = [sinks] + args
    o_spec = pl.BlockSpec((step_rows, BRANCH_WIDTH), lambda u, n: (cur(u, n), 0))
    out_specs = [o_spec]
    out_shape = [jax.ShapeDtypeStruct((t, BRANCH_WIDTH), BF16)]
    if want_lse:
        out_specs.append(o_spec)
        out_shape.append(jax.ShapeDtypeStruct((t, BRANCH_WIDTH), F32))
    return pl.pallas_call(
        functools.partial(_band_kernel, has_sink=sinks is not None, want_lse=want_lse, kv_div=kv_div, rblk=rblk),
        grid=(nsub, nbs),
        in_specs=in_specs,
        out_specs=out_specs,
        out_shape=out_shape,
        compiler_params=_cparams(("parallel", "arbitrary")),
        name="band_attn",
    )(*args)


def _toeplitz(w, rows, cols):
    length = rows + cols - 1
    u = jnp.pad(w[..., ::-1], [(0, 0)] * (w.ndim - 1) + [(0, 1)])
    flat = jnp.broadcast_to(u[..., None, :], w.shape[:-1] + (rows, length + 1)).reshape(w.shape[:-1] + (-1,))
    skew = flat[..., :rows * length].reshape(w.shape[:-1] + (rows, length))
    return skew[..., rows - 1:rows - 1 + cols]


def _band_bias(table, step, max_dist):
    rel = BLOCK + np.arange(BLOCK)[:, None] - np.arange(2 * BLOCK)[None, :]
    rel_vec = np.arange(BLOCK - (2 * BLOCK - 1), 2 * BLOCK)
    bias = _toeplitz(table[_t5_bucket(jnp.asarray(rel_vec * step))].T.astype(F32), BLOCK, 2 * BLOCK)
    ok = (rel >= 0) & (rel <= max_dist)
    return jnp.where(jnp.asarray(ok)[None], bias, NEG)


def _merge_kernel(x_ref, ya_ref, yb0_ref, yb1_ref, yb2_ref, l0_ref, l1_ref, l2_ref, yc_ref, yd_ref,
                  za_ref, zb_ref, zc_ref, zd_ref, ga_ref, gb_ref, gc_ref, gd_ref, wb_ref, wo_ref, o_ref,
                  y1_ref, y2_ref, s1_ref, s2_ref):
    ncol = BRANCH_WIDTH // LANES
    for src, dst in ((yb1_ref, y1_ref), (yb2_ref, y2_ref), (l1_ref, s1_ref), (l2_ref, s2_ref)):
        dil = src.shape[1]
        for r in range(dil):
            blk = src[0, r].astype(F32)
            for c in range(ncol):
                dst[c, pl.ds(r, src.shape[2], stride=dil), :] = blk[:, c * LANES:(c + 1) * LANES]
    wide = lambda ref: jnp.concatenate([ref[c] for c in range(ncol)], axis=1)
    l0, l1, l2 = l0_ref[0, 0], wide(s1_ref), wide(s2_ref)
    mx = jnp.maximum(jnp.maximum(l0, l1), l2)
    w0, w1, w2 = jnp.exp(l0 - mx), jnp.exp(l1 - mx), jnp.exp(l2 - mx)
    yb = (w0 * yb0_ref[0, 0].astype(F32) + w1 * wide(y1_ref) + w2 * wide(y2_ref)) / (w0 + w1 + w2)
    ys = (ya_ref[...].astype(F32), yb, yc_ref[...].astype(F32), yd_ref[...].astype(F32))
    zs = (za_ref, zb_ref, zc_ref, zd_ref)
    gs = (ga_ref, gb_ref, gc_ref, gd_ref)
    merged = jnp.zeros(o_ref.shape, F32)
    for n in range(N_BRANCH):
        z = zs[n][...].astype(F32)
        u = (ys[n] * (z * jax.nn.sigmoid(z))).astype(BF16)
        proj = jnp.dot(u, wb_ref[n], preferred_element_type=F32)
        merged = merged + jax.nn.sigmoid(gs[n][...].astype(F32)) * proj
    o_ref[...] = x_ref[...] + jnp.dot(merged.astype(BF16), wo_ref[...], preferred_element_type=F32)


def _merge(x2, ya, yb, lb, yc, yd, pp, wb, wo, batch, seq, tm=512):
    t = x2.shape[0]
    bw = BRANCH_WIDTH
    nsb = seq // tm
    row = lambda w, c: pl.BlockSpec((tm, w), lambda i: (i, c))

    def sub(dil):
        return pl.BlockSpec((1, dil, tm // dil, bw), lambda i: (i // nsb, 0, i % nsb, 0))

    dils = [dil for _, dil in B_GROUPS]
    yb = [a.reshape(batch, dil, seq // dil, bw) for a, dil in zip(yb, dils)]
    lb = [a.reshape(batch, dil, seq // dil, bw) for a, dil in zip(lb, dils)]
    in_specs = ([row(D_MODEL, 0), row(bw, 0)] + [sub(dil) for dil in dils] * 2 + [row(bw, 0), row(bw, 0)]
                + [row(bw, PP_AZ // bw), row(bw, PP_BZ // bw), row(bw, PP_CZ // bw), row(bw, PP_DZ // bw)]
                + [row(D_MODEL, PP_G // D_MODEL + n) for n in range(N_BRANCH)]
                + [pl.BlockSpec((N_BRANCH, bw, D_MODEL), lambda i: (0, 0, 0)),
                   pl.BlockSpec((D_MODEL, D_MODEL), lambda i: (0, 0))])
    return pl.pallas_call(
        _merge_kernel,
        grid=(t // tm,),
        in_specs=in_specs,
        out_specs=row(D_MODEL, 0),
        out_shape=jax.ShapeDtypeStruct((t, D_MODEL), F32),
        scratch_shapes=[pltpu.VMEM((bw // LANES, tm, LANES), F32)] * 4,
        compiler_params=_cparams(("parallel",)),
        name="merge",
    )(x2, ya, yb[0], yb[1], yb[2], lb[0], lb[1], lb[2], yc, yd, pp, pp, pp, pp, pp, pp, pp, pp, wb, wo)


def _layer_weights(w_in, qk_gain_a, qk_gain_b, qk_gain_d):
    def cols(start, width):
        return w_in[:, :, start:start + width]

    n_bq = len(B_GROUPS) * B_HEADS * HEAD_DIM
    dk = cols(_D0 + 512, 128)
    dv = cols(_D0 + 640, 128)
    dup = lambda a: jnp.concatenate([a[:, :, :64], a[:, :, :64], a[:, :, 64:], a[:, :, 64:]], axis=-1)
    w_pn = jnp.concatenate([cols(_A0, 512), cols(_A0 + 512, 512), cols(_D0, 512), dup(dk), dup(dv)],
                           axis=-1).astype(BF16)
    w_pp = jnp.concatenate([cols(_G0, N_BRANCH * D_MODEL), cols(_A0 + 1024, 512), cols(_A0 + 1536, 512),
                            cols(_B0 + 3 * n_bq, 512), cols(_C0 + 416, 512), cols(_D0 + 768, 512)],
                           axis=-1).astype(BF16)
    w_bg = [jnp.concatenate([cols(_B0 + g * 512, 512), cols(_B0 + n_bq + g * 512, 512),
                             cols(_B0 + 2 * n_bq + g * 512, 512)], axis=-1).astype(BF16)
            for g in range(len(B_GROUPS))]
    depth = w_in.shape[0]
    zeros = lambda w: jnp.zeros((depth, D_MODEL, w), w_in.dtype)
    w_pf = jnp.concatenate([cols(_A0 + 2048, 296), zeros(216), cols(_C0, 416), zeros(96)], axis=-1).astype(BF16)

    scale = HEAD_DIM ** -0.5
    tile = lambda g, reps: jnp.tile(g, (1, reps))
    hg = jnp.concatenate([tile(qk_gain_a[:, 0] * scale, 8), tile(qk_gain_a[:, 1], 8),
                          tile(qk_gain_d[:, 0] * scale, 8), tile(qk_gain_d[:, 1], 4),
                          jnp.ones((depth, 256), F32)], axis=-1)
    flag = jnp.concatenate([jnp.ones((PN_DV,), F32), jnp.zeros((PN_W - PN_DV,), F32)])
    hg_b = jnp.concatenate([tile(qk_gain_b[:, 0] * scale, 8), tile(qk_gain_b[:, 1], 8),
                            jnp.ones((depth, 512), F32)], axis=-1)
    flag_b = jnp.concatenate([jnp.ones((2 * BRANCH_WIDTH,), F32), jnp.zeros((BRANCH_WIDTH,), F32)])
    return w_pn, w_pp, w_bg, w_pf, hg, flag, hg_b, flag_b


def _c_weights(qk_gain_c, w_q_b, w_kv_b, seq):
    src_q, src_k, gain_idx, head, pe_src, rope_j, rope_half = _c_layout_tables()
    take = lambda w, src: jnp.where(jnp.asarray(src >= 0), jnp.take(w, jnp.asarray(np.maximum(src, 0)), axis=-1), 0.0)
    wq = take(w_q_b, src_q).astype(BF16)
    wk = take(w_kv_b, src_k).astype(BF16)
    v_src = np.array([h * (C_NOPE + C_V) + C_NOPE + c for h in range(C_HEADS) for c in range(C_V)])
    wv = jnp.take(w_kv_b, jnp.asarray(v_src), axis=-1).astype(BF16)
    ppe = np.zeros((LANES, C_QK_W), np.float32)
    ln = np.nonzero(pe_src >= 0)[0]
    ppe[pe_src[ln], ln] = 1
    grp = ((head[:, None] == head[None, :]) & (head[:, None] >= 0)).astype(np.float32)
    g_take = lambda g: jnp.where(jnp.asarray(gain_idx >= 0), jnp.take(g, jnp.asarray(np.maximum(gain_idx, 0)), axis=-1), 0.0)
    gq = g_take(qk_gain_c[:, 0]) * (C_NOPE + C_ROPE) ** -0.5
    gk = g_take(qk_gain_c[:, 1])
    freq = ROPE_THETA ** (-jnp.arange(C_ROPE_HALF, dtype=F32) / C_ROPE_HALF)
    ang = jnp.arange(seq).astype(F32)[:, None] * freq[None, :]
    cos_j, sin_j = jnp.cos(ang), jnp.sin(ang)
    place = np.zeros((C_ROPE_HALF, C_PAIR), np.float32)
    lanes = np.nonzero(rope_j[:C_PAIR] >= 0)[0]
    place[rope_j[lanes], lanes] = 1
    is_rope = jnp.asarray(rope_j[:C_PAIR] >= 0)
    hi = lax.Precision.HIGHEST
    cos = jnp.where(is_rope, jnp.dot(cos_j, jnp.asarray(place), precision=hi), 1.0)
    sin = jnp.dot(sin_j, jnp.asarray(place), precision=hi)
    s1 = jnp.where(jnp.asarray(rope_half[:C_PAIR] == 0), -sin, 0.0)
    s2 = jnp.where(jnp.asarray(rope_half[:C_PAIR] == 1), sin, 0.0)
    return wq, wk, wv, jnp.asarray(ppe, BF16), jnp.asarray(grp, BF16), gq, gk, cos, s1, s2


def _a_bias_tiles(table):
    dist_vec = np.arange(-(BLOCK - 1), A_BIAS_TILES * BLOCK)
    vec = table[_t5_bucket(jnp.asarray(dist_vec))].T.astype(F32)
    windows = jnp.stack([vec[:, d * BLOCK:d * BLOCK + 2 * BLOCK - 1] for d in range(A_BIAS_TILES)], axis=1)
    return _toeplitz(windows, BLOCK, BLOCK)


def kernel(x, norm_gain, w_in, qk_gain_a, qk_gain_b, qk_gain_c, qk_gain_d, c_q_gain, c_kv_gain,
           w_q_b, w_kv_b, sinks, rel_bias, w_branch, w_out):
    batch, seq, d_model = x.shape
    depth = w_in.shape[0]
    t = batch * seq
    x2 = x.reshape(t, d_model)

    w_pn, w_pp, w_bg, w_pf, hg, flag, hg_b, flag_b = _layer_weights(w_in, qk_gain_a, qk_gain_b, qk_gain_d)
    wq, wk, wv, ppe, grp, gq, gk, cos, s1, s2 = _c_weights(qk_gain_c, w_q_b, w_kv_b, seq)
    wb = w_branch.astype(BF16)
    wo = w_out.astype(BF16)

    bias_a = _a_bias_tiles(rel_bias[:, :A_HEADS])
    bias_b = [_band_bias(rel_bias[:, A_HEADS + g * B_HEADS:A_HEADS + (g + 1) * B_HEADS], dil, window // dil)
              for g, (window, dil) in enumerate(B_GROUPS)]
    bias_d = _band_bias(rel_bias[:, N_BIAS_HEADS - D_HEADS:], 1, D_WINDOW - 1)

    for l in range(depth):
        pn = _proj(x2, norm_gain[l], w_pn[l], BF16, head_gain=hg[l], flag=flag, norm_cols=PN_DV)
        pp = _proj(x2, norm_gain[l], w_pp[l], BF16)
        pf = _proj(x2, norm_gain[l], w_pf[l], F32)

        sel = _a_select(pf, batch, seq)
        ya = _a_attn(pn, pp, sel, bias_a, batch, seq)

        yb, lb = [], []
        for g, (window, dil) in enumerate(B_GROUPS):
            bg = _proj(x2, norm_gain[l], w_bg[g][l], BF16, head_gain=hg_b[l], flag=flag_b,
                       norm_cols=2 * BRANCH_WIDTH, dil=dil, seq=seq)
            o, lse = _band_attn(bg, bg, bg, bias_b[g], batch * dil, seq // dil, 0, 1, 2, BRANCH_WIDTH,
                                want_lse=True)
            yb.append(o)
            lb.append(lse)

        qc, kc, vc = _c_prep(pf, c_q_gain[l].reshape(1, -1), c_kv_gain[l].reshape(1, -1), wq[l], wk[l], wv[l],
                             ppe, grp, gq[l].reshape(1, -1), gk[l].reshape(1, -1), cos, s1, s2, seq)
        yc = _c_attn(qc, kc, vc, batch, seq)

        (yd,) = _band_attn(pn, pn, pn, bias_d, batch, seq, PN_DQ // BRANCH_WIDTH, PN_DK // 256, PN_DV // 256,
                           256, sinks=sinks[l])

        x2 = _merge(x2, ya, yb, lb, yc, yd, pp, wb[l], wo[l], batch, seq)
    return x2.reshape(batch, seq, d_model)
```

```python
import functools
import math

import numpy as np
import jax
import jax.numpy as jnp
from jax import lax
from jax.experimental import pallas as pl
from jax.experimental.pallas import tpu as pltpu

F32 = jnp.float32
BF16 = jnp.bfloat16

D_MODEL = 1024
BLOCK = 128
HEAD_DIM = 64
N_BRANCH = 4
BRANCH_WIDTH = 512
EPS = 1e-6
A_HEADS = 8
IDX_HEADS = 8
IDX_DIM = 32
TOPK_MAX = 256
B_GROUPS = ((128, 1), (512, 4), (2048, 16))
B_HEADS = 8
C_HEADS = 8
C_NOPE = 64
C_ROPE = 32
C_V = 64
C_Q_LORA = 256
C_KV_LORA = 128
ROPE_THETA = 10000.0
D_HEADS = 8
D_KV_HEADS = 2
D_WINDOW = 128
NUM_BUCKETS = 32
MAX_DISTANCE = 2048
N_BIAS_HEADS = A_HEADS + len(B_GROUPS) * B_HEADS + D_HEADS

LANES = 128
NEG = -1e30
INT_MIN = -(2 ** 31)

_A0 = 0
_B0 = 2344
_C0 = 7464
_D0 = 8392
_G0 = 9672

PN_AQ, PN_AK, PN_DQ, PN_DK, PN_DV, PN_W = 0, 512, 1024, 1536, 1792, 2048
PP_G, PP_AV, PP_AZ, PP_BZ, PP_CZ, PP_DZ, PP_W = 0, 4096, 4608, 5120, 5632, 6144, 6656
PF_W = 1024
BG_W = 3 * BRANCH_WIDTH

A_BIAS_TILES = MAX_DISTANCE // BLOCK + 2


def _cparams(sem, vmem_mb=48):
    return pltpu.CompilerParams(dimension_semantics=sem, vmem_limit_bytes=vmem_mb * 1024 * 1024)


def _t5_bucket(dist):
    max_exact = NUM_BUCKETS // 2
    d = jnp.maximum(dist, 0)
    logd = jnp.log(jnp.maximum(d, 1).astype(F32) / max_exact)
    large = max_exact + (logd / math.log(MAX_DISTANCE / max_exact) * (NUM_BUCKETS - max_exact)).astype(jnp.int32)
    return jnp.where(d < max_exact, d, jnp.minimum(large, NUM_BUCKETS - 1))


def _rms_bf16(x, gain):
    return (x * lax.rsqrt(jnp.mean(x * x, axis=-1, keepdims=True) + EPS) * gain).astype(BF16)


def _norm_kernel(x_ref, g_ref, o_ref):
    o_ref[...] = _rms_bf16(x_ref[...], g_ref[...])


def _norm(x2, gain, tm=1024):
    t, d = x2.shape
    return pl.pallas_call(
        _norm_kernel,
        grid=(t // tm,),
        in_specs=[pl.BlockSpec((tm, d), lambda i: (i, 0)), pl.BlockSpec((1, d), lambda i: (0, 0))],
        out_specs=pl.BlockSpec((tm, d), lambda i: (i, 0)),
        out_shape=jax.ShapeDtypeStruct((t, d), BF16),
        compiler_params=_cparams(("parallel",)),
        name="norm",
    )(x2, gain.reshape(1, d))


def _proj_kernel(xn_ref, w_ref, *rest, norm, dil):
    rest = list(rest)
    res_ref = rest.pop(-1) if dil is not None and dil > 1 else None
    if norm:
        hg_ref, flag_ref, bd_ref, o_ref = rest
    else:
        (o_ref,) = rest

    def emit(h):
        if dil is None:
            o_ref[...] = h.astype(o_ref.dtype)
        elif dil == 1:
            o_ref[0, 0] = h.astype(o_ref.dtype)
        else:
            sub = h.shape[0] // dil
            for c in range(h.shape[1] // LANES):
                res_ref[c] = h[:, c * LANES:(c + 1) * LANES]
            for r in range(dil):
                o_ref[0, r] = jnp.concatenate(
                    [res_ref[c, pl.ds(r, sub, stride=dil), :] for c in range(h.shape[1] // LANES)],
                    axis=1).astype(o_ref.dtype)

    h = jnp.dot(xn_ref[...], w_ref[...], preferred_element_type=F32)
    if norm:
        ss = jnp.dot((h * h).astype(BF16), bd_ref[...], preferred_element_type=F32)
        scale = lax.rsqrt(ss * (1.0 / HEAD_DIM) + EPS) * hg_ref[...]
        h = h * jnp.where(flag_ref[...] > 0, scale, 1.0)
    emit(h)


def _proj(xn, w, out_dtype, head_gain=None, flag=None, dil=None, seq=None, tm=1024, tn=512):
    t, d = xn.shape
    n = w.shape[1]
    norm = head_gain is not None
    scratch = []
    if dil is None:
        out_spec = pl.BlockSpec((tm, tn), lambda i, j: (i, j))
        out_shape = jax.ShapeDtypeStruct((t, n), out_dtype)
        kdil = None
    else:
        nsb = seq // tm
        out_spec = pl.BlockSpec((1, dil, tm // dil, tn), lambda i, j: (i // nsb, 0, i % nsb, j))
        out_shape = jax.ShapeDtypeStruct((t // seq, dil, seq // dil, n), out_dtype)
        kdil = dil
        if dil > 1:
            scratch.append(pltpu.VMEM((tn // LANES, tm, LANES), F32))
    in_specs = [pl.BlockSpec((tm, d), lambda i, j: (i, 0)),
                pl.BlockSpec((d, tn), lambda i, j: (0, j))]
    args = [xn, w]
    if norm:
        lane = np.arange(tn)
        bd = jnp.asarray((lane[:, None] // HEAD_DIM == lane[None, :] // HEAD_DIM), BF16)
        in_specs += [pl.BlockSpec((1, tn), lambda i, j: (0, j)),
                     pl.BlockSpec((1, tn), lambda i, j: (0, j)),
                     pl.BlockSpec((tn, tn), lambda i, j: (0, 0))]
        args += [head_gain.reshape(1, n), flag.reshape(1, n), bd]
    out = pl.pallas_call(
        functools.partial(_proj_kernel, norm=norm, dil=kdil),
        grid=(t // tm, n // tn),
        in_specs=in_specs,
        out_specs=out_spec,
        out_shape=out_shape,
        scratch_shapes=scratch,
        compiler_params=_cparams(("parallel", "parallel")),
        name=("proj_norm" if norm else "proj") + ("" if dil is None else "_dil%d" % dil),
    )(*args)
    return out.reshape(t, n)


C_PAIR = 256
C_QK_W = (C_HEADS // 2) * C_PAIR
C_ROPE_HALF = C_ROPE // 2


def _c_lane(h, c):
    base = (h // 2) * C_PAIR
    e = h % 2
    if c < C_NOPE:
        return base + e * C_NOPE + c
    return base + 2 * C_NOPE + e * C_ROPE + (c - C_NOPE)


def _c_layout_tables():
    src_q = np.full(C_QK_W, -1, np.int64)
    src_k = np.full(C_QK_W, -1, np.int64)
    gain_idx = np.full(C_QK_W, -1, np.int64)
    head = np.full(C_QK_W, -1, np.int64)
    pe_src = np.full(C_QK_W, -1, np.int64)
    rope_j = np.full(C_QK_W, -1, np.int64)
    rope_half = np.zeros(C_QK_W, np.int64)
    for h in range(C_HEADS):
        for c in range(C_NOPE + C_ROPE):
            ln = _c_lane(h, c)
            src_q[ln] = h * (C_NOPE + C_ROPE) + c
            gain_idx[ln] = c
            head[ln] = h
            if c < C_NOPE:
                src_k[ln] = h * (C_NOPE + C_V) + c
            else:
                r = c - C_NOPE
                pe_src[ln] = r
                rope_j[ln] = r % C_ROPE_HALF
                rope_half[ln] = r // C_ROPE_HALF
    return src_q, src_k, gain_idx, head, pe_src, rope_j, rope_half


def _c_prep_kernel(pf_ref, cqg_ref, ckvg_ref, wq_ref, wk_ref, wv_ref, ppe_ref, grp_ref,
                   gq_ref, gk_ref, cos_ref, s1_ref, s2_ref, q_ref, k_ref, v_ref):
    blk = pf_ref[...]
    cq = blk[:, :C_Q_LORA]
    ckv = blk[:, C_Q_LORA:C_Q_LORA + C_KV_LORA]
    pe = blk[:, C_Q_LORA + C_KV_LORA:]

    def rms(v, g):
        return v * lax.rsqrt(jnp.mean(v * v, axis=-1, keepdims=True) + EPS) * g

    def head_norm_rope(raw, gain):
        ss = jnp.dot((raw * raw).astype(BF16), grp_ref[...], preferred_element_type=F32)
        y = raw * lax.rsqrt(ss * (1.0 / (C_NOPE + C_ROPE)) + EPS) * gain
        up = pltpu.roll(y, C_QK_W - C_ROPE_HALF, 1)
        dn = pltpu.roll(y, C_ROPE_HALF, 1)
        wide = lambda ref: jnp.concatenate([ref[...]] * (C_QK_W // C_PAIR), axis=1)
        return y * wide(cos_ref) + up * wide(s1_ref) + dn * wide(s2_ref)

    cqn = rms(cq, cqg_ref[...]).astype(BF16)
    q_raw = jnp.dot(cqn, wq_ref[...], preferred_element_type=F32)
    q_ref[...] = head_norm_rope(q_raw, gq_ref[...]).astype(BF16)

    ckvn = rms(ckv, ckvg_ref[...]).astype(BF16)
    pe_hi = pe.astype(BF16)
    pe_lo = (pe - pe_hi.astype(F32)).astype(BF16)
    k_raw = (jnp.dot(ckvn, wk_ref[...], preferred_element_type=F32)
             + jnp.dot(pe_hi, ppe_ref[...], preferred_element_type=F32)
             + jnp.dot(pe_lo, ppe_ref[...], preferred_element_type=F32))
    k_ref[...] = head_norm_rope(k_raw, gk_ref[...]).astype(BF16)
    v_ref[...] = jnp.dot(ckvn, wv_ref[...], preferred_element_type=F32).astype(BF16)


def _c_prep(pf, cqg, ckvg, wq, wk, wv, ppe, grp, gq, gk, cos, s1, s2, seq, tm=512):
    t = pf.shape[0]
    nsb = seq // tm
    full = lambda shape: pl.BlockSpec(shape, lambda i: (0,) * len(shape))
    tab = pl.BlockSpec((tm, C_PAIR), lambda i: (i % nsb, 0))
    return pl.pallas_call(
        _c_prep_kernel,
        grid=(t // tm,),
        in_specs=[pl.BlockSpec((tm, 512), lambda i: (i, 1)),
                  full((1, C_Q_LORA)), full((1, C_KV_LORA)),
                  full((C_Q_LORA, C_QK_W)), full((C_KV_LORA, C_QK_W)), full((C_KV_LORA, BRANCH_WIDTH)),
                  full((LANES, C_QK_W)), full((C_QK_W, C_QK_W)),
                  full((1, C_QK_W)), full((1, C_QK_W)), tab, tab, tab],
        out_specs=[pl.BlockSpec((tm, C_QK_W), lambda i: (i, 0)),
                   pl.BlockSpec((tm, C_QK_W), lambda i: (i, 0)),
                   pl.BlockSpec((tm, BRANCH_WIDTH), lambda i: (i, 0))],
        out_shape=[jax.ShapeDtypeStruct((t, C_QK_W), BF16),
                   jax.ShapeDtypeStruct((t, C_QK_W), BF16),
                   jax.ShapeDtypeStruct((t, BRANCH_WIDTH), BF16)],
        compiler_params=_cparams(("parallel",)),
        name="c_prep",
    )(pf, cqg, ckvg, wq, wk, wv, ppe, grp, gq, gk, cos, s1, s2)


def _flash_update(e, s, v_ones, m_ref, acc_ref):
    m_prev = m_ref[e]
    m_new = jnp.maximum(m_prev, jnp.max(s, axis=1, keepdims=True))
    alpha = jnp.exp(m_prev - m_new)
    p = jnp.exp(s - jnp.concatenate([m_new] * (s.shape[1] // LANES), axis=1))
    acc_ref[e] = (jnp.concatenate([alpha, alpha], axis=1) * acc_ref[e]
                  + jnp.dot(p.astype(BF16), v_ones, preferred_element_type=F32))
    m_ref[e] = m_new


def _flash_init(m_ref, acc_ref):
    m_ref[...] = jnp.full(m_ref.shape, NEG, F32)
    acc_ref[...] = jnp.zeros(acc_ref.shape, F32)


def _flash_finish(o_ref, acc_ref):
    lane = lax.broadcasted_iota(jnp.int32, o_ref.shape, 1)
    o0 = acc_ref[0, :, :LANES] / acc_ref[0, :, LANES:]
    o1 = acc_ref[1, :, :LANES] / acc_ref[1, :, LANES:]
    o_ref[...] = jnp.where(lane < HEAD_DIM, o0, o1).astype(o_ref.dtype)


def _with_ones(v_chunk):
    return jnp.concatenate([v_chunk, jnp.ones(v_chunk.shape, v_chunk.dtype)], axis=1)


_CONTRACT_LANES = (((1,), (1,)), ((), ()))


def _head_masks(width, ranges0, ranges1):
    m = np.zeros((2, 1, width), np.float32)
    for e, ranges in enumerate((ranges0, ranges1)):
        for lo, hi in ranges:
            m[e, 0, lo:hi] = 1
    return jnp.asarray(m, BF16)


def _pair_masks():
    return _head_masks(LANES, [(0, HEAD_DIM)], [(HEAD_DIM, LANES)])


def _transpose_keys(k_ref, kt_ref, rows=512):
    def body(c, carry):
        off = pl.multiple_of(c * rows, rows)
        kt_ref[:, pl.ds(off, rows)] = k_ref[pl.ds(off, rows), :].astype(F32).T.astype(kt_ref.dtype)
        return carry
    lax.fori_loop(0, k_ref.shape[0] // rows, body, 0)


def _c_attn_kernel(q_ref, hm_ref, k_ref, v_ref, o_ref, kt_ref, m_ref, acc_ref, *, tq, tk):
    qi = pl.program_id(2)

    @pl.when(qi == 0)
    def _():
        _transpose_keys(k_ref, kt_ref)

    q = q_ref[...]
    qs = (q * hm_ref[0], q * hm_ref[1])
    _flash_init(m_ref, acc_ref)
    n_full = (qi * tq) // tk

    def chunk(c, masked):
        off = pl.multiple_of(c * tk, tk)
        kc = kt_ref[:, pl.ds(off, tk)]
        vc = _with_ones(v_ref[pl.ds(off, tk), :])
        for e in range(2):
            s = jnp.dot(qs[e], kc, preferred_element_type=F32)
            if masked:
                row = qi * tq + lax.broadcasted_iota(jnp.int32, s.shape, 0)
                col = c * tk + lax.broadcasted_iota(jnp.int32, s.shape, 1)
                s = jnp.where(col <= row, s, NEG)
            _flash_update(e, s, vc, m_ref, acc_ref)

    def body(c, carry):
        chunk(c, False)
        return carry

    lax.fori_loop(0, n_full, body, 0)
    for j in range(max(tq // tk, 1)):
        chunk(n_full + j, True)
    _flash_finish(o_ref, acc_ref)


def _c_attn(qc, kc, vc, batch, seq, tq=1024, tk=512):
    t = qc.shape[0]
    nq = seq // tq
    npair = C_HEADS // 2
    return pl.pallas_call(
        functools.partial(_c_attn_kernel, tq=tq, tk=tk),
        grid=(batch, npair, nq),
        in_specs=[pl.BlockSpec((tq, C_PAIR), lambda b, p, i: (b * nq + i, p)),
                  pl.BlockSpec((2, 1, C_PAIR), lambda b, p, i: (0, 0, 0)),
                  pl.BlockSpec((seq, C_PAIR), lambda b, p, i: (b, p)),
                  pl.BlockSpec((seq, LANES), lambda b, p, i: (b, p))],
        out_specs=pl.BlockSpec((tq, LANES), lambda b, p, i: (b * nq + i, p)),
        out_shape=jax.ShapeDtypeStruct((t, BRANCH_WIDTH), BF16),
        scratch_shapes=[pltpu.VMEM((C_PAIR, seq), BF16),
                        pltpu.VMEM((2, tq, LANES), F32), pltpu.VMEM((2, tq, 2 * LANES), F32)],
        compiler_params=_cparams(("parallel", "parallel", "arbitrary")),
        name="c_attn",
    )(qc, _head_masks(C_PAIR, [(0, C_NOPE), (2 * C_NOPE, 2 * C_NOPE + C_ROPE)],
                      [(C_NOPE, 2 * C_NOPE), (2 * C_NOPE + C_ROPE, 2 * C_NOPE + 2 * C_ROPE)]), kc, vc)


def _a_attn_kernel(q_ref, hm_ref, k_ref, v_ref, sel_ref, bias_ref, o_ref, kt_ref, m_ref, acc_ref,
                   *, tq, tk):
    qi = pl.program_id(2)

    @pl.when(qi == 0)
    def _():
        _transpose_keys(k_ref, kt_ref)

    q = q_ref[...]
    qs = (q * hm_ref[0], q * hm_ref[1])
    _flash_init(m_ref, acc_ref)
    n_chunks = ((qi + 1) * tq + tk - 1) // tk

    def body(c, carry):
        off = pl.multiple_of(c * tk, tk)
        kc = kt_ref[:, pl.ds(off, tk)]
        vc = _with_ones(v_ref[pl.ds(off, tk), :])
        selmask = sel_ref[:, pl.ds(off, tk)].astype(F32)
        for e in range(2):
            s = jnp.dot(qs[e], kc, preferred_element_type=F32)
            rows = []
            for i in range(tq // BLOCK):
                tiles = []
                for j in range(tk // BLOCK):
                    d = (qi * (tq // BLOCK) + i) - (c * (tk // BLOCK) + j)
                    d = jnp.clip(d, 0, A_BIAS_TILES - 1)
                    tiles.append(bias_ref[e, d])
                rows.append(jnp.concatenate(tiles, axis=1))
            s = s + jnp.concatenate(rows, axis=0) + selmask
            _flash_update(e, s, vc, m_ref, acc_ref)
        return carry

    lax.fori_loop(0, n_chunks, body, 0)
    _flash_finish(o_ref, acc_ref)


def _a_attn(pn, pp, sel, bias_tiles, batch, seq, tq=512, tk=1024):
    t = pn.shape[0]
    nq = seq // tq
    npair = A_HEADS // 2
    qcol, kcol, vcol = PN_AQ // LANES, PN_AK // LANES, PP_AV // LANES
    return pl.pallas_call(
        functools.partial(_a_attn_kernel, tq=tq, tk=tk),
        grid=(batch, npair, nq),
        in_specs=[pl.BlockSpec((tq, LANES), lambda b, p, i: (b * nq + i, qcol + p)),
                  pl.BlockSpec((2, 1, LANES), lambda b, p, i: (0, 0, 0)),
                  pl.BlockSpec((seq, LANES), lambda b, p, i: (b, kcol + p)),
                  pl.BlockSpec((seq, LANES), lambda b, p, i: (b, vcol + p)),
                  pl.BlockSpec((tq, seq), lambda b, p, i: (b * nq + i, 0)),
                  pl.BlockSpec((2, A_BIAS_TILES, BLOCK, BLOCK), lambda b, p, i: (p, 0, 0, 0))],
        out_specs=pl.BlockSpec((tq, LANES), lambda b, p, i: (b * nq + i, p)),
        out_shape=jax.ShapeDtypeStruct((t, BRANCH_WIDTH), BF16),
        scratch_shapes=[pltpu.VMEM((LANES, seq), BF16),
                        pltpu.VMEM((2, tq, LANES), F32), pltpu.VMEM((2, tq, 2 * LANES), F32)],
        compiler_params=_cparams(("parallel", "parallel", "arbitrary")),
        name="a_attn",
    )(pn, _pair_masks(), pn, pp, sel, bias_tiles)


def _sortable_key(score):
    bits = pltpu.bitcast(score, jnp.int32)
    return bits ^ ((bits >> 31) & jnp.int32(0x7FFFFFFF))


SEL_ROWS = 512
CNT_ROWS = SEL_ROWS
GMAX_ROWS = 256


def _a_select_kernel(qblk_ref, kblk_ref, ph_ref, pl_ref, pkh_ref, pkl_ref, sel_ref,
                     ikx_ref, keys_ref, gmax_ref, iqt_ref, jcut_ref, *, seq, k_sel):
    qi = pl.program_id(1)
    n_sel = (qi * BLOCK + BLOCK + SEL_ROWS - 1) // SEL_ROWS
    n_cnt = n_sel

    @pl.when(qi == 0)
    def _():
        def prep(c, carry):
            off = pl.multiple_of(c * 512, 512)
            kv = kblk_ref[pl.ds(off, 512), :]
            hi = kv.astype(BF16)
            lo = (kv - hi.astype(F32)).astype(BF16)
            ikx_ref[pl.ds(off, 512), :] = (
                jnp.dot(hi, pkh_ref[...], preferred_element_type=F32)
                + jnp.dot(lo, pkl_ref[...], preferred_element_type=F32)).astype(BF16)
            return carry
        lax.fori_loop(0, seq // 512, prep, 0)

    qb = qblk_ref[...]
    iq = qb[:, :IDX_HEADS * IDX_DIM]
    iq_hi = iq.astype(BF16)
    iq_lo = (iq - iq_hi.astype(F32)).astype(BF16)
    iqx = (jnp.dot(iq_hi, ph_ref[...], preferred_element_type=F32)
           + jnp.dot(iq_lo, pl_ref[...], preferred_element_type=F32))
    for h in range(IDX_HEADS):
        iqt_ref[h // 2, :, (h % 2) * LANES:(h % 2 + 1) * LANES] = iqx[:, h * LANES:(h + 1) * LANES].T.astype(BF16)
    iw_t = qb[:, IDX_HEADS * IDX_DIM:IDX_HEADS * IDX_DIM + LANES].T
    iw_rows = [iw_t[IDX_DIM + h:IDX_DIM + h + 1, :] for h in range(IDX_HEADS)]
    q_pos = qi * BLOCK + lax.broadcasted_iota(jnp.int32, (SEL_ROWS, BLOCK), 1)
    k_row = lax.broadcasted_iota(jnp.int32, (SEL_ROWS, BLOCK), 0)

    def score_chunk(c, masked):
        off = pl.multiple_of(c * SEL_ROWS, SEL_ROWS)
        kx = ikx_ref[pl.ds(off, SEL_ROWS), :]
        sc = jnp.zeros((SEL_ROWS, BLOCK), F32)
        for hp in range(IDX_HEADS // 2):
            xx = jnp.dot(kx, iqt_ref[hp], preferred_element_type=F32)
            sc = (sc + jnp.maximum(xx[:, :LANES], 0.0) * iw_rows[2 * hp]
                  + jnp.maximum(xx[:, LANES:], 0.0) * iw_rows[2 * hp + 1])
        key = _sortable_key(sc + 0.0)
        if masked:
            key = jnp.where(off + k_row <= q_pos, key, INT_MIN)
        keys_ref[pl.ds(off, SEL_ROWS), :] = key
        gm = gmax_ref[...]
        for j in range(SEL_ROWS // GMAX_ROWS):
            gm = jnp.maximum(gm, key[j * GMAX_ROWS:(j + 1) * GMAX_ROWS])
        gmax_ref[...] = gm

    def score_body(c, carry):
        score_chunk(c, False)
        return carry

    gmax_ref[...] = jnp.full(gmax_ref.shape, INT_MIN, jnp.int32)
    lax.fori_loop(0, n_sel - 1, score_body, 0)
    score_chunk(n_sel - 1, True)

    def count_rows(pred):
        def cnt_chunk(c, acc):
            off = pl.multiple_of(c * CNT_ROWS, CNT_ROWS)
            hit = jnp.where(pred(keys_ref[pl.ds(off, CNT_ROWS), :], off), 1, 0)
            return acc + jnp.sum(hit.reshape(CNT_ROWS // 8, 8, BLOCK), axis=0)
        acc = lax.fori_loop(0, n_cnt, cnt_chunk, jnp.zeros((8, BLOCK), jnp.int32))
        return jnp.sum(acc, axis=0, keepdims=True)

    def count_ge(cand):
        return count_rows(lambda keys, off: keys >= cand)

    gm = gmax_ref[...]
    k_lo = jnp.min(gm, axis=0, keepdims=True)
    k_hi = jnp.max(gm, axis=0, keepdims=True)
    span = k_hi - k_lo
    nbits = jnp.max(jnp.where(span < 0, 32, 32 - lax.clz(span)))

    unknown = jnp.int32(2 ** 30)

    def bisect(it, state):
        lo, cnt_lo = state
        cand = lo + lax.shift_left(jnp.int32(1), nbits - 1 - it)
        cnt = count_ge(cand)
        ok = (cand > lo) & (cand <= k_hi) & (cnt >= k_sel)
        return jnp.where(ok, cand, lo), jnp.where(ok, cnt, cnt_lo)

    thr, cnt_thr = lax.fori_loop(0, nbits, bisect, (k_lo, jnp.full((1, BLOCK), unknown, jnp.int32)))

    has_k = thr > INT_MIN
    thr = jnp.maximum(thr, INT_MIN + 1)

    jcut_ref[...] = jnp.full(jcut_ref.shape, 2 ** 31 - 1, jnp.int32)

    @pl.when(jnp.max(jnp.where(has_k, cnt_thr, 0)) > k_sel)
    def _():
        excess = jnp.where(has_k, count_ge(thr) - k_sel, 0)
        need = jnp.where(excess > 0, k_sel - count_ge(thr + 1), 1)
        rev_row = (seq - 1) - lax.broadcasted_iota(jnp.int32, (CNT_ROWS, BLOCK), 0)

        def tie_bisect(it, lo):
            cand = lo + lax.shift_left(jnp.int32(1), jnp.int32(seq.bit_length() - 2) - it)
            cnt = count_rows(lambda keys, off: (keys == thr) & (rev_row - off >= cand))
            return jnp.where(cnt >= need, cand, lo)

        rev = lax.fori_loop(0, seq.bit_length() - 1, tie_bisect, jnp.zeros((1, BLOCK), jnp.int32))
        jcut = jnp.where(excess > 0, (seq - 1) - rev, 2 ** 31 - 1)
        jcut_ref[...] = jnp.broadcast_to(jcut, jcut_ref.shape)

    jcut = jcut_ref[0:1, :]

    def emit(c, carry):
        off = pl.multiple_of(c * SEL_ROWS, SEL_ROWS)
        keys = keys_ref[pl.ds(off, SEL_ROWS), :]
        bar = jnp.where(off + k_row > jcut, thr + 1, thr)
        add = jnp.where(keys >= bar, 0.0, NEG)
        sel_ref[:, pl.ds(off, SEL_ROWS)] = jnp.concatenate(
            [add[j * BLOCK:(j + 1) * BLOCK].T for j in range(SEL_ROWS // BLOCK)], axis=1).astype(BF16)
        return carry

    lax.fori_loop(0, n_sel, emit, 0)

    def fill(c, carry):
        off = pl.multiple_of(c * SEL_ROWS, SEL_ROWS)
        sel_ref[:, pl.ds(off, SEL_ROWS)] = jnp.full((BLOCK, SEL_ROWS), NEG, BF16)
        return carry

    lax.fori_loop(n_sel, seq // SEL_ROWS, fill, 0)


def _a_select(pf, batch, seq):
    t = pf.shape[0]
    nq = seq // BLOCK
    k_sel = min(TOPK_MAX, seq // 4)
    nlane = IDX_HEADS * IDX_DIM
    r = np.arange(nlane)
    ph = np.zeros((nlane, IDX_HEADS * LANES), np.float32)
    plo = np.zeros((nlane, IDX_HEADS * LANES), np.float32)
    ph[r, (r // IDX_DIM) * LANES + r % IDX_DIM] = 1
    ph[r, (r // IDX_DIM) * LANES + 2 * IDX_DIM + r % IDX_DIM] = 1
    plo[r, (r // IDX_DIM) * LANES + IDX_DIM + r % IDX_DIM] = 1
    d = np.arange(IDX_DIM)
    pkh = np.zeros((LANES, LANES), np.float32)
    pkl = np.zeros((LANES, LANES), np.float32)
    pkh[d, d] = 1
    pkh[d, IDX_DIM + d] = 1
    pkl[d, 2 * IDX_DIM + d] = 1
    full = lambda shape: pl.BlockSpec(shape, lambda b, i: (0,) * len(shape))
    return pl.pallas_call(
        functools.partial(_a_select_kernel, seq=seq, k_sel=k_sel),
        grid=(batch, nq),
        in_specs=[pl.BlockSpec((BLOCK, 512), lambda b, i: (b * nq + i, 0)),
                  pl.BlockSpec((seq, LANES), lambda b, i: (b, nlane // LANES)),
                  full(ph.shape), full(plo.shape), full(pkh.shape), full(pkl.shape)],
        out_specs=pl.BlockSpec((BLOCK, seq), lambda b, i: (b * nq + i, 0)),
        out_shape=jax.ShapeDtypeStruct((t, seq), BF16),
        scratch_shapes=[pltpu.VMEM((seq, LANES), BF16), pltpu.VMEM((seq, LANES), jnp.int32),
                        pltpu.VMEM((GMAX_ROWS, LANES), jnp.int32),
                        pltpu.VMEM((IDX_HEADS // 2, LANES, 2 * LANES), BF16),
                        pltpu.VMEM((8, LANES), jnp.int32)],
        compiler_params=_cparams(("parallel", "arbitrary")),
        name="a_select",
    )(pf, pf, jnp.asarray(ph, BF16), jnp.asarray(plo, BF16), jnp.asarray(pkh, BF16),
      jnp.asarray(pkl, BF16))


BAND_RBLK = 4


def _band_kernel(*refs, has_sink, want_lse, kv_div, rblk):
    if has_sink:
        sink_ref, refs = refs[0], refs[1:]
    q_ref, hm_ref, kp_ref, kc_ref, vp_ref, vc_ref, bias_ref, o_ref = refs[:8]
    first = pl.program_id(1) == 0
    lane = lax.broadcasted_iota(jnp.int32, (BLOCK, LANES), 1)
    col = lax.broadcasted_iota(jnp.int32, (BLOCK, 2 * BLOCK), 1)
    no_prev = jnp.logical_and(first, col < BLOCK)
    for p in range(BRANCH_WIDTH // LANES):
        kcol = (p // kv_div) * LANES
        kk = jnp.concatenate([kp_ref[:, kcol:kcol + LANES], kc_ref[:, kcol:kcol + LANES]], axis=0)
        vv = jnp.concatenate([vp_ref[:, kcol:kcol + LANES], vc_ref[:, kcol:kcol + LANES]], axis=0)
        for i in range(rblk):
            rows = slice(i * BLOCK, (i + 1) * BLOCK)
            q = q_ref[rows, p * LANES:(p + 1) * LANES]
            kblk = kk[i * BLOCK:(i + 2) * BLOCK]
            vblk = vv[i * BLOCK:(i + 2) * BLOCK]
            outs, lses = [], []
            for e in range(2):
                s = lax.dot_general(q * hm_ref[e], kblk, _CONTRACT_LANES, preferred_element_type=F32)
                s = s + bias_ref[2 * p + e]
                if i == 0:
                    s = jnp.where(no_prev, NEG, s)
                m = jnp.max(s, axis=1, keepdims=True)
                if has_sink:
                    sk = sink_ref[2 * p + e]
                    m = jnp.maximum(m, sk)
                pr = jnp.exp(s - m)
                den = jnp.sum(pr, axis=1, keepdims=True)
                if has_sink:
                    den = den + jnp.exp(sk - m)
                outs.append(jnp.dot(pr.astype(BF16), vblk, preferred_element_type=F32) / den)
                lses.append(m + jnp.log(den))
            o_ref[rows, p * LANES:(p + 1) * LANES] = jnp.where(lane < HEAD_DIM, outs[0], outs[1]).astype(o_ref.dtype)
            if want_lse:
                refs[8][rows, p * LANES:(p + 1) * LANES] = jnp.where(lane < HEAD_DIM, lses[0], lses[1])


def _band_attn(q_arr, k_arr, v_arr, bias, nsub, sub_len, qcb, kcb, vcb, kv_w, sinks=None, want_lse=False):
    t = q_arr.shape[0]
    rblk = min(BAND_RBLK, sub_len // BLOCK)
    step_rows = rblk * BLOCK
    nbs = sub_len // step_rows
    kv_div = BRANCH_WIDTH // kv_w

    def cur(u, n):
        return u * nbs + n

    def prev(u, n):
        return jnp.where(n == 0, u * nbs * rblk, (u * nbs + n) * rblk - 1)

    in_specs = [
        pl.BlockSpec((step_rows, BRANCH_WIDTH), lambda u, n: (cur(u, n), qcb)),
        pl.BlockSpec((2, 1, LANES), lambda u, n: (0, 0, 0)),
        pl.BlockSpec((BLOCK, kv_w), lambda u, n: (prev(u, n), kcb)),
        pl.BlockSpec((step_rows, kv_w), lambda u, n: (cur(u, n), kcb)),
        pl.BlockSpec((BLOCK, kv_w), lambda u, n: (prev(u, n), vcb)),
        pl.BlockSpec((step_rows, kv_w), lambda u, n: (cur(u, n), vcb)),
        pl.BlockSpec(bias.shape, lambda u, n: (0, 0, 0)),
    ]
    args = [q_arr, _pair_masks(), k_arr, k_arr, v_arr, v_arr, bias]
    if sinks is not None:
        in_specs = [pl.BlockSpec(memory_space=pltpu.SMEM)] + in_specs
        args = [sinks] + args
    o_spec = pl.BlockSpec((step_rows, BRANCH_WIDTH), lambda u, n: (cur(u, n), 0))
    out_specs = [o_spec]
    out_shape = [jax.ShapeDtypeStruct((t, BRANCH_WIDTH), BF16)]
    if want_lse:
        out_specs.append(o_spec)
        out_shape.append(jax.ShapeDtypeStruct((t, BRANCH_WIDTH), F32))
    return pl.pallas_call(
        functools.partial(_band_kernel, has_sink=sinks is not None, want_lse=want_lse, kv_div=kv_div, rblk=rblk),
        grid=(nsub, nbs),
        in_specs=in_specs,
        out_specs=out_specs,
        out_shape=out_shape,
        compiler_params=_cparams(("parallel", "arbitrary")),
        name="band_attn",
    )(*args)


def _toeplitz(w, rows, cols):
    length = rows + cols - 1
    u = jnp.pad(w[..., ::-1], [(0, 0)] * (w.ndim - 1) + [(0, 1)])
    flat = jnp.broadcast_to(u[..., None, :], w.shape[:-1] + (rows, length + 1)).reshape(w.shape[:-1] + (-1,))
    skew = flat[..., :rows * length].reshape(w.shape[:-1] + (rows, length))
    return skew[..., rows - 1:rows - 1 + cols]


def _band_bias(table, step, max_dist):
    rel = BLOCK + np.arange(BLOCK)[:, None] - np.arange(2 * BLOCK)[None, :]
    rel_vec = np.arange(BLOCK - (2 * BLOCK - 1), 2 * BLOCK)
    bias = _toeplitz(table[_t5_bucket(jnp.asarray(rel_vec * step))].T.astype(F32), BLOCK, 2 * BLOCK)
    ok = (rel >= 0) & (rel <= max_dist)
    return jnp.where(jnp.asarray(ok)[None], bias, NEG)


def _merge_kernel(x_ref, ya_ref, yb0_ref, yb1_ref, yb2_ref, l0_ref, l1_ref, l2_ref, yc_ref, yd_ref,
                  za_ref, zb_ref, zc_ref, zd_ref, ga_ref, gb_ref, gc_ref, gd_ref, wb_ref, wo_ref, *rest):
    if len(rest) == 7:
        ng_ref, o_ref, xn_ref, y1_ref, y2_ref, s1_ref, s2_ref = rest
    else:
        ng_ref = xn_ref = None
        o_ref, y1_ref, y2_ref, s1_ref, s2_ref = rest
    ncol = BRANCH_WIDTH // LANES
    for src, dst in ((yb1_ref, y1_ref), (yb2_ref, y2_ref), (l1_ref, s1_ref), (l2_ref, s2_ref)):
        dil = src.shape[1]
        for r in range(dil):
            blk = src[0, r].astype(F32)
            for c in range(ncol):
                dst[c, pl.ds(r, src.shape[2], stride=dil), :] = blk[:, c * LANES:(c + 1) * LANES]
    wide = lambda ref: jnp.concatenate([ref[c] for c in range(ncol)], axis=1)
    l0, l1, l2 = l0_ref[0, 0], wide(s1_ref), wide(s2_ref)
    mx = jnp.maximum(jnp.maximum(l0, l1), l2)
    w0, w1, w2 = jnp.exp(l0 - mx), jnp.exp(l1 - mx), jnp.exp(l2 - mx)
    yb = (w0 * yb0_ref[0, 0].astype(F32) + w1 * wide(y1_ref) + w2 * wide(y2_ref)) / (w0 + w1 + w2)
    ys = (ya_ref[...].astype(F32), yb, yc_ref[...].astype(F32), yd_ref[...].astype(F32))
    zs = (za_ref, zb_ref, zc_ref, zd_ref)
    gs = (ga_ref, gb_ref, gc_ref, gd_ref)
    merged = jnp.zeros(o_ref.shape, F32)
    for n in range(N_BRANCH):
        z = zs[n][...].astype(F32)
        u = (ys[n] * (z * jax.nn.sigmoid(z))).astype(BF16)
        proj = jnp.dot(u, wb_ref[n], preferred_element_type=F32)
        merged = merged + jax.nn.sigmoid(gs[n][...].astype(F32)) * proj
    x_new = x_ref[...] + jnp.dot(merged.astype(BF16), wo_ref[...], preferred_element_type=F32)
    o_ref[...] = x_new
    if xn_ref is not None:
        xn_ref[...] = _rms_bf16(x_new, ng_ref[...])


def _merge(x2, ya, yb, lb, yc, yd, pp, wb, wo, batch, seq, next_gain=None, tm=512):
    t = x2.shape[0]
    bw = BRANCH_WIDTH
    nsb = seq // tm
    row = lambda w, c: pl.BlockSpec((tm, w), lambda i: (i, c))

    def sub(dil):
        return pl.BlockSpec((1, dil, tm // dil, bw), lambda i: (i // nsb, 0, i % nsb, 0))

    dils = [dil for _, dil in B_GROUPS]
    yb = [a.reshape(batch, dil, seq // dil, bw) for a, dil in zip(yb, dils)]
    lb = [a.reshape(batch, dil, seq // dil, bw) for a, dil in zip(lb, dils)]
    in_specs = ([row(D_MODEL, 0), row(bw, 0)] + [sub(dil) for dil in dils] * 2 + [row(bw, 0), row(bw, 0)]
                + [row(bw, PP_AZ // bw), row(bw, PP_BZ // bw), row(bw, PP_CZ // bw), row(bw, PP_DZ // bw)]
                + [row(D_MODEL, PP_G // D_MODEL + n) for n in range(N_BRANCH)]
                + [pl.BlockSpec((N_BRANCH, bw, D_MODEL), lambda i: (0, 0, 0)),
                   pl.BlockSpec((D_MODEL, D_MODEL), lambda i: (0, 0))])
    args = [x2, ya, yb[0], yb[1], yb[2], lb[0], lb[1], lb[2], yc, yd, pp, pp, pp, pp, pp, pp, pp, pp, wb, wo]
    out_specs = [row(D_MODEL, 0)]
    out_shape = [jax.ShapeDtypeStruct((t, D_MODEL), F32)]
    if next_gain is not None:
        in_specs.append(pl.BlockSpec((1, D_MODEL), lambda i: (0, 0)))
        args.append(next_gain.reshape(1, D_MODEL))
        out_specs.append(row(D_MODEL, 0))
        out_shape.append(jax.ShapeDtypeStruct((t, D_MODEL), BF16))
    return pl.pallas_call(
        _merge_kernel,
        grid=(t // tm,),
        in_specs=in_specs,
        out_specs=out_specs,
        out_shape=out_shape,
        scratch_shapes=[pltpu.VMEM((bw // LANES, tm, LANES), F32)] * 4,
        compiler_params=_cparams(("parallel",)),
        name="merge",
    )(*args)


def _layer_weights(w_in, qk_gain_a, qk_gain_b, qk_gain_d):
    def cols(start, width):
        return w_in[:, :, start:start + width]

    n_bq = len(B_GROUPS) * B_HEADS * HEAD_DIM
    dk = cols(_D0 + 512, 128)
    dv = cols(_D0 + 640, 128)
    dup = lambda a: jnp.concatenate([a[:, :, :64], a[:, :, :64], a[:, :, 64:], a[:, :, 64:]], axis=-1)
    w_pn = jnp.concatenate([cols(_A0, 512), cols(_A0 + 512, 512), cols(_D0, 512), dup(dk), dup(dv)],
                           axis=-1).astype(BF16)
    w_pp = jnp.concatenate([cols(_G0, N_BRANCH * D_MODEL), cols(_A0 + 1024, 512), cols(_A0 + 1536, 512),
                            cols(_B0 + 3 * n_bq, 512), cols(_C0 + 416, 512), cols(_D0 + 768, 512)],
                           axis=-1).astype(BF16)
    w_bg = [jnp.concatenate([cols(_B0 + g * 512, 512), cols(_B0 + n_bq + g * 512, 512),
                             cols(_B0 + 2 * n_bq + g * 512, 512)], axis=-1).astype(BF16)
            for g in range(len(B_GROUPS))]
    depth = w_in.shape[0]
    zeros = lambda w: jnp.zeros((depth, D_MODEL, w), w_in.dtype)
    w_pf = jnp.concatenate([cols(_A0 + 2048, 296), zeros(216), cols(_C0, 416), zeros(96)], axis=-1).astype(BF16)

    scale = HEAD_DIM ** -0.5
    tile = lambda g, reps: jnp.tile(g, (1, reps))
    hg = jnp.concatenate([tile(qk_gain_a[:, 0] * scale, 8), tile(qk_gain_a[:, 1], 8),
                          tile(qk_gain_d[:, 0] * scale, 8), tile(qk_gain_d[:, 1], 4),
                          jnp.ones((depth, 256), F32)], axis=-1)
    flag = jnp.concatenate([jnp.ones((PN_DV,), F32), jnp.zeros((PN_W - PN_DV,), F32)])
    hg_b = jnp.concatenate([tile(qk_gain_b[:, 0] * scale, 8), tile(qk_gain_b[:, 1], 8),
                            jnp.ones((depth, 512), F32)], axis=-1)
    flag_b = jnp.concatenate([jnp.ones((2 * BRANCH_WIDTH,), F32), jnp.zeros((BRANCH_WIDTH,), F32)])
    return w_pn, w_pp, w_bg, w_pf, hg, flag, hg_b, flag_b


def _c_weights(qk_gain_c, w_q_b, w_kv_b, seq):
    src_q, src_k, gain_idx, head, pe_src, rope_j, rope_half = _c_layout_tables()
    take = lambda w, src: jnp.where(jnp.asarray(src >= 0), jnp.take(w, jnp.asarray(np.maximum(src, 0)), axis=-1), 0.0)
    wq = take(w_q_b, src_q).astype(BF16)
    wk = take(w_kv_b, src_k).astype(BF16)
    v_src = np.array([h * (C_NOPE + C_V) + C_NOPE + c for h in range(C_HEADS) for c in range(C_V)])
    wv = jnp.take(w_kv_b, jnp.asarray(v_src), axis=-1).astype(BF16)
    ppe = np.zeros((LANES, C_QK_W), np.float32)
    ln = np.nonzero(pe_src >= 0)[0]
    ppe[pe_src[ln], ln] = 1
    grp = ((head[:, None] == head[None, :]) & (head[:, None] >= 0)).astype(np.float32)
    g_take = lambda g: jnp.where(jnp.asarray(gain_idx >= 0), jnp.take(g, jnp.asarray(np.maximum(gain_idx, 0)), axis=-1), 0.0)
    gq = g_take(qk_gain_c[:, 0]) * (C_NOPE + C_ROPE) ** -0.5
    gk = g_take(qk_gain_c[:, 1])
    freq = ROPE_THETA ** (-jnp.arange(C_ROPE_HALF, dtype=F32) / C_ROPE_HALF)
    ang = jnp.arange(seq).astype(F32)[:, None] * freq[None, :]
    cos_j, sin_j = jnp.cos(ang), jnp.sin(ang)
    place = np.zeros((C_ROPE_HALF, C_PAIR), np.float32)
    lanes = np.nonzero(rope_j[:C_PAIR] >= 0)[0]
    place[rope_j[lanes], lanes] = 1
    is_rope = jnp.asarray(rope_j[:C_PAIR] >= 0)
    hi = lax.Precision.HIGHEST
    cos = jnp.where(is_rope, jnp.dot(cos_j, jnp.asarray(place), precision=hi), 1.0)
    sin = jnp.dot(sin_j, jnp.asarray(place), precision=hi)
    s1 = jnp.where(jnp.asarray(rope_half[:C_PAIR] == 0), -sin, 0.0)
    s2 = jnp.where(jnp.asarray(rope_half[:C_PAIR] == 1), sin, 0.0)
    return wq, wk, wv, jnp.asarray(ppe, BF16), jnp.asarray(grp, BF16), gq, gk, cos, s1, s2


def _a_bias_tiles(table):
    dist_vec = np.arange(-(BLOCK - 1), A_BIAS_TILES * BLOCK)
    vec = table[_t5_bucket(jnp.asarray(dist_vec))].T.astype(F32)
    windows = jnp.stack([vec[:, d * BLOCK:d * BLOCK + 2 * BLOCK - 1] for d in range(A_BIAS_TILES)], axis=1)
    return _toeplitz(windows, BLOCK, BLOCK)


def kernel(x, norm_gain, w_in, qk_gain_a, qk_gain_b, qk_gain_c, qk_gain_d, c_q_gain, c_kv_gain,
           w_q_b, w_kv_b, sinks, rel_bias, w_branch, w_out):
    batch, seq, d_model = x.shape
    depth = w_in.shape[0]
    t = batch * seq
    x2 = x.reshape(t, d_model)

    w_pn, w_pp, w_bg, w_pf, hg, flag, hg_b, flag_b = _layer_weights(w_in, qk_gain_a, qk_gain_b, qk_gain_d)
    wq, wk, wv, ppe, grp, gq, gk, cos, s1, s2 = _c_weights(qk_gain_c, w_q_b, w_kv_b, seq)
    wb = w_branch.astype(BF16)
    wo = w_out.astype(BF16)

    bias_a = _a_bias_tiles(rel_bias[:, :A_HEADS])
    bias_b = [_band_bias(rel_bias[:, A_HEADS + g * B_HEADS:A_HEADS + (g + 1) * B_HEADS], dil, window // dil)
              for g, (window, dil) in enumerate(B_GROUPS)]
    bias_d = _band_bias(rel_bias[:, N_BIAS_HEADS - D_HEADS:], 1, D_WINDOW - 1)

    xn = _norm(x2, norm_gain[0])
    for l in range(depth):
        pn = _proj(xn, w_pn[l], BF16, head_gain=hg[l], flag=flag)
        pp = _proj(xn, w_pp[l], BF16)
        pf = _proj(xn, w_pf[l], F32)

        sel = _a_select(pf, batch, seq)
        ya = _a_attn(pn, pp, sel, bias_a, batch, seq)

        yb, lb = [], []
        for g, (window, dil) in enumerate(B_GROUPS):
            bg = _proj(xn, w_bg[g][l], BF16, head_gain=hg_b[l], flag=flag_b, dil=dil, seq=seq)
            o, lse = _band_attn(bg, bg, bg, bias_b[g], batch * dil, seq // dil, 0, 1, 2, BRANCH_WIDTH,
                                want_lse=True)
            yb.append(o)
            lb.append(lse)

        qc, kc, vc = _c_prep(pf, c_q_gain[l].reshape(1, -1), c_kv_gain[l].reshape(1, -1), wq[l], wk[l], wv[l],
                             ppe, grp, gq[l].reshape(1, -1), gk[l].reshape(1, -1), cos, s1, s2, seq)
        yc = _c_attn(qc, kc, vc, batch, seq)

        (yd,) = _band_attn(pn, pn, pn, bias_d, batch, seq, PN_DQ // BRANCH_WIDTH, PN_DK // 256, PN_DV // 256,
                           256, sinks=sinks[l])

        if l + 1 < depth:
            x2, xn = _merge(x2, ya, yb, lb, yc, yd, pp, wb[l], wo[l], batch, seq, next_gain=norm_gain[l + 1])
        else:
            (x2,) = _merge(x2, ya, yb, lb, yc, yd, pp, wb[l], wo[l], batch, seq)
    return x2.reshape(batch, seq, d_model)
```

```python
import functools
import math

import numpy as np
import jax
import jax.numpy as jnp
from jax import lax
from jax.experimental import pallas as pl
from jax.experimental.pallas import tpu as pltpu

F32 = jnp.float32
BF16 = jnp.bfloat16

D_MODEL = 1024
BLOCK = 128
HEAD_DIM = 64
N_BRANCH = 4
BRANCH_WIDTH = 512
EPS = 1e-6
A_HEADS = 8
IDX_HEADS = 8
IDX_DIM = 32
TOPK_MAX = 256
B_GROUPS = ((128, 1), (512, 4), (2048, 16))
B_HEADS = 8
C_HEADS = 8
C_NOPE = 64
C_ROPE = 32
C_V = 64
C_Q_LORA = 256
C_KV_LORA = 128
ROPE_THETA = 10000.0
D_HEADS = 8
D_KV_HEADS = 2
D_WINDOW = 128
NUM_BUCKETS = 32
MAX_DISTANCE = 2048
N_BIAS_HEADS = A_HEADS + len(B_GROUPS) * B_HEADS + D_HEADS

LANES = 128
NEG = -1e30
INT_MIN = -(2 ** 31)

_A0 = 0
_B0 = 2344
_C0 = 7464
_D0 = 8392
_G0 = 9672

PN_AQ, PN_AK, PN_DQ, PN_DK, PN_DV, PN_W = 0, 512, 1024, 1536, 1792, 2048
PP_G, PP_AV, PP_AZ, PP_BZ, PP_CZ, PP_DZ, PP_W = 0, 4096, 4608, 5120, 5632, 6144, 6656
PF_W = 1024
BG_W = 3 * BRANCH_WIDTH

A_BIAS_TILES = MAX_DISTANCE // BLOCK + 2


def _cparams(sem, vmem_mb=48):
    return pltpu.CompilerParams(dimension_semantics=sem, vmem_limit_bytes=vmem_mb * 1024 * 1024)


def _t5_bucket(dist):
    max_exact = NUM_BUCKETS // 2
    d = jnp.maximum(dist, 0)
    logd = jnp.log(jnp.maximum(d, 1).astype(F32) / max_exact)
    large = max_exact + (logd / math.log(MAX_DISTANCE / max_exact) * (NUM_BUCKETS - max_exact)).astype(jnp.int32)
    return jnp.where(d < max_exact, d, jnp.minimum(large, NUM_BUCKETS - 1))


def _rms_bf16(x, gain):
    return (x * lax.rsqrt(jnp.mean(x * x, axis=-1, keepdims=True) + EPS) * gain).astype(BF16)


def _norm_kernel(x_ref, g_ref, o_ref):
    o_ref[...] = _rms_bf16(x_ref[...], g_ref[...])


def _norm(x2, gain, tm=1024):
    t, d = x2.shape
    return pl.pallas_call(
        _norm_kernel,
        grid=(t // tm,),
        in_specs=[pl.BlockSpec((tm, d), lambda i: (i, 0)), pl.BlockSpec((1, d), lambda i: (0, 0))],
        out_specs=pl.BlockSpec((tm, d), lambda i: (i, 0)),
        out_shape=jax.ShapeDtypeStruct((t, d), BF16),
        compiler_params=_cparams(("parallel",)),
        name="norm",
    )(x2, gain.reshape(1, d))


def _proj_kernel(xn_ref, w_ref, *rest, norm, dil):
    rest = list(rest)
    res_ref = rest.pop(-1) if dil is not None and dil > 1 else None
    if norm:
        hg_ref, flag_ref, bd_ref, o_ref = rest
    else:
        (o_ref,) = rest

    def emit(h):
        if dil is None:
            o_ref[...] = h.astype(o_ref.dtype)
        elif dil == 1:
            o_ref[0, 0] = h.astype(o_ref.dtype)
        else:
            sub = h.shape[0] // dil
            for c in range(h.shape[1] // LANES):
                res_ref[c] = h[:, c * LANES:(c + 1) * LANES]
            for r in range(dil):
                o_ref[0, r] = jnp.concatenate(
                    [res_ref[c, pl.ds(r, sub, stride=dil), :] for c in range(h.shape[1] // LANES)],
                    axis=1).astype(o_ref.dtype)

    h = jnp.dot(xn_ref[...], w_ref[...], preferred_element_type=F32)
    if norm:
        ss = jnp.dot((h * h).astype(BF16), bd_ref[...], preferred_element_type=F32)
        scale = lax.rsqrt(ss * (1.0 / HEAD_DIM) + EPS) * hg_ref[...]
        h = h * jnp.where(flag_ref[...] > 0, scale, 1.0)
    emit(h)


def _proj(xn, w, out_dtype, head_gain=None, flag=None, dil=None, seq=None, tm=1024, tn=512):
    t, d = xn.shape
    n = w.shape[1]
    norm = head_gain is not None
    scratch = []
    if dil is None:
        out_spec = pl.BlockSpec((tm, tn), lambda i, j: (i, j))
        out_shape = jax.ShapeDtypeStruct((t, n), out_dtype)
        kdil = None
    else:
        nsb = seq // tm
        out_spec = pl.BlockSpec((1, dil, tm // dil, tn), lambda i, j: (i // nsb, 0, i % nsb, j))
        out_shape = jax.ShapeDtypeStruct((t // seq, dil, seq // dil, n), out_dtype)
        kdil = dil
        if dil > 1:
            scratch.append(pltpu.VMEM((tn // LANES, tm, LANES), F32))
    in_specs = [pl.BlockSpec((tm, d), lambda i, j: (i, 0)),
                pl.BlockSpec((d, tn), lambda i, j: (0, j))]
    args = [xn, w]
    if norm:
        lane = np.arange(tn)
        bd = jnp.asarray((lane[:, None] // HEAD_DIM == lane[None, :] // HEAD_DIM), BF16)
        in_specs += [pl.BlockSpec((1, tn), lambda i, j: (0, j)),
                     pl.BlockSpec((1, tn), lambda i, j: (0, j)),
                     pl.BlockSpec((tn, tn), lambda i, j: (0, 0))]
        args += [head_gain.reshape(1, n), flag.reshape(1, n), bd]
    out = pl.pallas_call(
        functools.partial(_proj_kernel, norm=norm, dil=kdil),
        grid=(t // tm, n // tn),
        in_specs=in_specs,
        out_specs=out_spec,
        out_shape=out_shape,
        scratch_shapes=scratch,
        compiler_params=_cparams(("parallel", "parallel")),
        name=("proj_norm" if norm else "proj") + ("" if dil is None else "_dil%d" % dil),
    )(*args)
    return out.reshape(t, n)


C_PAIR = 256
C_QK_W = (C_HEADS // 2) * C_PAIR
C_ROPE_HALF = C_ROPE // 2


def _c_lane(h, c):
    base = (h // 2) * C_PAIR
    e = h % 2
    if c < C_NOPE:
        return base + e * C_NOPE + c
    return base + 2 * C_NOPE + e * C_ROPE + (c - C_NOPE)


def _c_layout_tables():
    src_q = np.full(C_QK_W, -1, np.int64)
    src_k = np.full(C_QK_W, -1, np.int64)
    gain_idx = np.full(C_QK_W, -1, np.int64)
    head = np.full(C_QK_W, -1, np.int64)
    pe_src = np.full(C_QK_W, -1, np.int64)
    rope_j = np.full(C_QK_W, -1, np.int64)
    rope_half = np.zeros(C_QK_W, np.int64)
    for h in range(C_HEADS):
        for c in range(C_NOPE + C_ROPE):
            ln = _c_lane(h, c)
            src_q[ln] = h * (C_NOPE + C_ROPE) + c
            gain_idx[ln] = c
            head[ln] = h
            if c < C_NOPE:
                src_k[ln] = h * (C_NOPE + C_V) + c
            else:
                r = c - C_NOPE
                pe_src[ln] = r
                rope_j[ln] = r % C_ROPE_HALF
                rope_half[ln] = r // C_ROPE_HALF
    return src_q, src_k, gain_idx, head, pe_src, rope_j, rope_half


def _c_prep_kernel(pf_ref, cqg_ref, ckvg_ref, wq_ref, wk_ref, wv_ref, ppe_ref, grp_ref,
                   gq_ref, gk_ref, cos_ref, s1_ref, s2_ref, q_ref, k_ref, v_ref):
    blk = pf_ref[...]
    cq = blk[:, :C_Q_LORA]
    ckv = blk[:, C_Q_LORA:C_Q_LORA + C_KV_LORA]
    pe = blk[:, C_Q_LORA + C_KV_LORA:]

    def rms(v, g):
        return v * lax.rsqrt(jnp.mean(v * v, axis=-1, keepdims=True) + EPS) * g

    def head_norm_rope(raw, gain):
        sq = (raw * raw).astype(BF16)
        ss = jnp.concatenate([jnp.dot(sq[:, p * C_PAIR:(p + 1) * C_PAIR], grp_ref[...], preferred_element_type=F32)
                              for p in range(C_QK_W // C_PAIR)], axis=1)
        y = raw * lax.rsqrt(ss * (1.0 / (C_NOPE + C_ROPE)) + EPS) * gain
        up = pltpu.roll(y, C_QK_W - C_ROPE_HALF, 1)
        dn = pltpu.roll(y, C_ROPE_HALF, 1)
        wide = lambda ref: jnp.concatenate([ref[...]] * (C_QK_W // C_PAIR), axis=1)
        return y * wide(cos_ref) + up * wide(s1_ref) + dn * wide(s2_ref)

    cqn = rms(cq, cqg_ref[...]).astype(BF16)
    q_raw = jnp.dot(cqn, wq_ref[...], preferred_element_type=F32)
    q_ref[...] = head_norm_rope(q_raw, gq_ref[...]).astype(BF16)

    ckvn = rms(ckv, ckvg_ref[...]).astype(BF16)
    pe_hi = pe.astype(BF16)
    pe_lo = (pe - pe_hi.astype(F32)).astype(BF16)
    k_raw = (jnp.dot(ckvn, wk_ref[...], preferred_element_type=F32)
             + jnp.dot(pe_hi, ppe_ref[...], preferred_element_type=F32)
             + jnp.dot(pe_lo, ppe_ref[...], preferred_element_type=F32))
    k_ref[...] = head_norm_rope(k_raw, gk_ref[...]).astype(BF16)
    v_ref[...] = jnp.dot(ckvn, wv_ref[...], preferred_element_type=F32).astype(BF16)


def _c_prep(pf, cqg, ckvg, wq, wk, wv, ppe, grp, gq, gk, cos, s1, s2, seq, tm=512):
    t = pf.shape[0]
    nsb = seq // tm
    full = lambda shape: pl.BlockSpec(shape, lambda i: (0,) * len(shape))
    tab = pl.BlockSpec((tm, C_PAIR), lambda i: (i % nsb, 0))
    return pl.pallas_call(
        _c_prep_kernel,
        grid=(t // tm,),
        in_specs=[pl.BlockSpec((tm, 512), lambda i: (i, 1)),
                  full((1, C_Q_LORA)), full((1, C_KV_LORA)),
                  full((C_Q_LORA, C_QK_W)), full((C_KV_LORA, C_QK_W)), full((C_KV_LORA, BRANCH_WIDTH)),
                  full((LANES, C_QK_W)), full((C_PAIR, C_PAIR)),
                  full((1, C_QK_W)), full((1, C_QK_W)), tab, tab, tab],
        out_specs=[pl.BlockSpec((tm, C_QK_W), lambda i: (i, 0)),
                   pl.BlockSpec((tm, C_QK_W), lambda i: (i, 0)),
                   pl.BlockSpec((tm, BRANCH_WIDTH), lambda i: (i, 0))],
        out_shape=[jax.ShapeDtypeStruct((t, C_QK_W), BF16),
                   jax.ShapeDtypeStruct((t, C_QK_W), BF16),
                   jax.ShapeDtypeStruct((t, BRANCH_WIDTH), BF16)],
        compiler_params=_cparams(("parallel",)),
        name="c_prep",
    )(pf, cqg, ckvg, wq, wk, wv, ppe, grp, gq, gk, cos, s1, s2)


def _flash_update(e, s, v_ones, m_ref, acc_ref):
    m_prev = m_ref[e]
    m_new = jnp.maximum(m_prev, jnp.max(s, axis=1, keepdims=True))
    alpha = jnp.exp(m_prev - m_new)
    p = jnp.exp(s - jnp.concatenate([m_new] * (s.shape[1] // LANES), axis=1))
    acc_ref[e] = (jnp.concatenate([alpha, alpha], axis=1) * acc_ref[e]
                  + jnp.dot(p.astype(BF16), v_ones, preferred_element_type=F32))
    m_ref[e] = m_new


def _flash_init(m_ref, acc_ref):
    m_ref[...] = jnp.full(m_ref.shape, NEG, F32)
    acc_ref[...] = jnp.zeros(acc_ref.shape, F32)


def _flash_finish(o_ref, acc_ref):
    lane = lax.broadcasted_iota(jnp.int32, o_ref.shape, 1)
    o0 = acc_ref[0, :, :LANES] / acc_ref[0, :, LANES:]
    o1 = acc_ref[1, :, :LANES] / acc_ref[1, :, LANES:]
    o_ref[...] = jnp.where(lane < HEAD_DIM, o0, o1).astype(o_ref.dtype)


def _with_ones(v_chunk):
    return jnp.concatenate([v_chunk, jnp.ones(v_chunk.shape, v_chunk.dtype)], axis=1)


_CONTRACT_LANES = (((1,), (1,)), ((), ()))


def _head_masks(width, ranges0, ranges1):
    m = np.zeros((2, 1, width), np.float32)
    for e, ranges in enumerate((ranges0, ranges1)):
        for lo, hi in ranges:
            m[e, 0, lo:hi] = 1
    return jnp.asarray(m, BF16)


def _pair_masks():
    return _head_masks(LANES, [(0, HEAD_DIM)], [(HEAD_DIM, LANES)])


def _transpose_keys(k_ref, kt_ref, rows=512):
    def body(c, carry):
        off = pl.multiple_of(c * rows, rows)
        kt_ref[:, pl.ds(off, rows)] = k_ref[pl.ds(off, rows), :].astype(F32).T.astype(kt_ref.dtype)
        return carry
    lax.fori_loop(0, k_ref.shape[0] // rows, body, 0)


def _c_attn_kernel(q_ref, hm_ref, k_ref, v_ref, o_ref, kt_ref, m_ref, acc_ref, *, tq, tk):
    qi = pl.program_id(2)

    @pl.when(qi == 0)
    def _():
        _transpose_keys(k_ref, kt_ref)

    q = q_ref[...]
    qs = (q * hm_ref[0], q * hm_ref[1])
    _flash_init(m_ref, acc_ref)
    n_full = (qi * tq) // tk

    def chunk(c, masked):
        off = pl.multiple_of(c * tk, tk)
        kc = kt_ref[:, pl.ds(off, tk)]
        vc = _with_ones(v_ref[pl.ds(off, tk), :])
        for e in range(2):
            s = jnp.dot(qs[e], kc, preferred_element_type=F32)
            if masked:
                row = qi * tq + lax.broadcasted_iota(jnp.int32, s.shape, 0)
                col = c * tk + lax.broadcasted_iota(jnp.int32, s.shape, 1)
                s = jnp.where(col <= row, s, NEG)
            _flash_update(e, s, vc, m_ref, acc_ref)

    def body(c, carry):
        chunk(c, False)
        return carry

    lax.fori_loop(0, n_full, body, 0)
    for j in range(max(tq // tk, 1)):
        chunk(n_full + j, True)
    _flash_finish(o_ref, acc_ref)


def _c_attn(qc, kc, vc, batch, seq, tq=1024, tk=512):
    t = qc.shape[0]
    nq = seq // tq
    npair = C_HEADS // 2
    return pl.pallas_call(
        functools.partial(_c_attn_kernel, tq=tq, tk=tk),
        grid=(batch, npair, nq),
        in_specs=[pl.BlockSpec((tq, C_PAIR), lambda b, p, i: (b * nq + i, p)),
                  pl.BlockSpec((2, 1, C_PAIR), lambda b, p, i: (0, 0, 0)),
                  pl.BlockSpec((seq, C_PAIR), lambda b, p, i: (b, p)),
                  pl.BlockSpec((seq, LANES), lambda b, p, i: (b, p))],
        out_specs=pl.BlockSpec((tq, LANES), lambda b, p, i: (b * nq + i, p)),
        out_shape=jax.ShapeDtypeStruct((t, BRANCH_WIDTH), BF16),
        scratch_shapes=[pltpu.VMEM((C_PAIR, seq), BF16),
                        pltpu.VMEM((2, tq, LANES), F32), pltpu.VMEM((2, tq, 2 * LANES), F32)],
        compiler_params=_cparams(("parallel", "parallel", "arbitrary")),
        name="c_attn",
    )(qc, _head_masks(C_PAIR, [(0, C_NOPE), (2 * C_NOPE, 2 * C_NOPE + C_ROPE)],
                      [(C_NOPE, 2 * C_NOPE), (2 * C_NOPE + C_ROPE, 2 * C_NOPE + 2 * C_ROPE)]), kc, vc)


def _a_attn_kernel(q_ref, hm_ref, k_ref, v_ref, sel_ref, bias_ref, o_ref, kt_ref, m_ref, acc_ref,
                   *, tq, tk):
    qi = pl.program_id(2)

    @pl.when(qi == 0)
    def _():
        _transpose_keys(k_ref, kt_ref)

    q = q_ref[...]
    qs = (q * hm_ref[0], q * hm_ref[1])
    _flash_init(m_ref, acc_ref)
    n_chunks = ((qi + 1) * tq + tk - 1) // tk

    def body(c, carry):
        off = pl.multiple_of(c * tk, tk)
        kc = kt_ref[:, pl.ds(off, tk)]
        vc = _with_ones(v_ref[pl.ds(off, tk), :])
        selmask = sel_ref[:, pl.ds(off, tk)].astype(F32)
        for e in range(2):
            s = jnp.dot(qs[e], kc, preferred_element_type=F32)
            rows = []
            for i in range(tq // BLOCK):
                tiles = []
                for j in range(tk // BLOCK):
                    d = (qi * (tq // BLOCK) + i) - (c * (tk // BLOCK) + j)
                    d = jnp.clip(d, 0, A_BIAS_TILES - 1)
                    tiles.append(bias_ref[e, d])
                rows.append(jnp.concatenate(tiles, axis=1))
            s = s + jnp.concatenate(rows, axis=0) + selmask
            _flash_update(e, s, vc, m_ref, acc_ref)
        return carry

    lax.fori_loop(0, n_chunks, body, 0)
    _flash_finish(o_ref, acc_ref)


def _a_attn(pn, pp, sel, bias_tiles, batch, seq, tq=512, tk=1024):
    t = pn.shape[0]
    nq = seq // tq
    npair = A_HEADS // 2
    qcol, kcol, vcol = PN_AQ // LANES, PN_AK // LANES, PP_AV // LANES
    return pl.pallas_call(
        functools.partial(_a_attn_kernel, tq=tq, tk=tk),
        grid=(batch, npair, nq),
        in_specs=[pl.BlockSpec((tq, LANES), lambda b, p, i: (b * nq + i, qcol + p)),
                  pl.BlockSpec((2, 1, LANES), lambda b, p, i: (0, 0, 0)),
                  pl.BlockSpec((seq, LANES), lambda b, p, i: (b, kcol + p)),
                  pl.BlockSpec((seq, LANES), lambda b, p, i: (b, vcol + p)),
                  pl.BlockSpec((tq, seq), lambda b, p, i: (b * nq + i, 0)),
                  pl.BlockSpec((2, A_BIAS_TILES, BLOCK, BLOCK), lambda b, p, i: (p, 0, 0, 0))],
        out_specs=pl.BlockSpec((tq, LANES), lambda b, p, i: (b * nq + i, p)),
        out_shape=jax.ShapeDtypeStruct((t, BRANCH_WIDTH), BF16),
        scratch_shapes=[pltpu.VMEM((LANES, seq), BF16),
                        pltpu.VMEM((2, tq, LANES), F32), pltpu.VMEM((2, tq, 2 * LANES), F32)],
        compiler_params=_cparams(("parallel", "parallel", "arbitrary")),
        name="a_attn",
    )(pn, _pair_masks(), pn, pp, sel, bias_tiles)


def _sortable_key(score):
    bits = pltpu.bitcast(score, jnp.int32)
    return bits ^ ((bits >> 31) & jnp.int32(0x7FFFFFFF))


SEL_ROWS = 512
CNT_ROWS = SEL_ROWS


def _a_select_kernel(qblk_ref, kblk_ref, ph_ref, pl_ref, pkh_ref, pkl_ref, sel_ref,
                     ikx_ref, keys_ref, iqt_ref, jcut_ref, *, seq, k_sel):
    qi = pl.program_id(1)
    n_sel = (qi * BLOCK + BLOCK + SEL_ROWS - 1) // SEL_ROWS
    n_cnt = n_sel

    @pl.when(qi == 0)
    def _():
        def prep(c, carry):
            off = pl.multiple_of(c * 512, 512)
            kv = kblk_ref[pl.ds(off, 512), :]
            hi = kv.astype(BF16)
            lo = (kv - hi.astype(F32)).astype(BF16)
            ikx_ref[pl.ds(off, 512), :] = (
                jnp.dot(hi, pkh_ref[...], preferred_element_type=F32)
                + jnp.dot(lo, pkl_ref[...], preferred_element_type=F32)).astype(BF16)
            return carry
        lax.fori_loop(0, seq // 512, prep, 0)

    qb = qblk_ref[...]
    iq = qb[:, :IDX_HEADS * IDX_DIM]
    iq_hi = iq.astype(BF16)
    iq_lo = (iq - iq_hi.astype(F32)).astype(BF16)
    iqx = (jnp.dot(iq_hi, ph_ref[...], preferred_element_type=F32)
           + jnp.dot(iq_lo, pl_ref[...], preferred_element_type=F32))
    for h in range(IDX_HEADS):
        iqt_ref[h // 2, :, (h % 2) * LANES:(h % 2 + 1) * LANES] = iqx[:, h * LANES:(h + 1) * LANES].T.astype(BF16)
    iw_t = qb[:, IDX_HEADS * IDX_DIM:IDX_HEADS * IDX_DIM + LANES].T
    iw_rows = [iw_t[IDX_DIM + h:IDX_DIM + h + 1, :] for h in range(IDX_HEADS)]
    q_pos = qi * BLOCK + lax.broadcasted_iota(jnp.int32, (SEL_ROWS, BLOCK), 1)
    k_row = lax.broadcasted_iota(jnp.int32, (SEL_ROWS, BLOCK), 0)

    def score_chunk(c, masked):
        off = pl.multiple_of(c * SEL_ROWS, SEL_ROWS)
        kx = ikx_ref[pl.ds(off, SEL_ROWS), :]
        sc = jnp.zeros((SEL_ROWS, BLOCK), F32)
        for hp in range(IDX_HEADS // 2):
            xx = jnp.dot(kx, iqt_ref[hp], preferred_element_type=F32)
            sc = (sc + jnp.maximum(xx[:, :LANES], 0.0) * iw_rows[2 * hp]
                  + jnp.maximum(xx[:, LANES:], 0.0) * iw_rows[2 * hp + 1])
        key = _sortable_key(sc + 0.0)
        if masked:
            key = jnp.where(off + k_row <= q_pos, key, INT_MIN)
        keys_ref[pl.ds(off, SEL_ROWS), :] = key

    def score_body(c, carry):
        score_chunk(c, False)
        return carry

    lax.fori_loop(0, n_sel - 1, score_body, 0)
    score_chunk(n_sel - 1, True)

    def count_rows(pred):
        def cnt_chunk(c, acc):
            off = pl.multiple_of(c * CNT_ROWS, CNT_ROWS)
            hit = jnp.where(pred(keys_ref[pl.ds(off, CNT_ROWS), :], off), 1, 0)
            return acc + jnp.sum(hit.reshape(CNT_ROWS // 8, 8, BLOCK), axis=0)
        acc = lax.fori_loop(0, n_cnt, cnt_chunk, jnp.zeros((8, BLOCK), jnp.int32))
        return jnp.sum(acc, axis=0, keepdims=True)

    def count_ge(cand):
        return count_rows(lambda keys, off: keys >= cand)

    unknown = jnp.int32(2 ** 30)

    def bisect(it, state):
        lo, cnt_lo = state
        cand = lo + lax.shift_left(jnp.int32(1), jnp.int32(31) - it)
        cnt = count_ge(cand)
        ok = cnt >= k_sel
        return jnp.where(ok, cand, lo), jnp.where(ok, cnt, cnt_lo)

    thr, cnt_thr = lax.fori_loop(0, 32, bisect, (jnp.full((1, BLOCK), INT_MIN, jnp.int32),
                                                 jnp.full((1, BLOCK), unknown, jnp.int32)))

    has_k = thr > INT_MIN
    thr = jnp.maximum(thr, INT_MIN + 1)

    jcut_ref[...] = jnp.full(jcut_ref.shape, 2 ** 31 - 1, jnp.int32)

    @pl.when(jnp.max(jnp.where(has_k, cnt_thr, 0)) > k_sel)
    def _():
        excess = jnp.where(has_k, count_ge(thr) - k_sel, 0)
        need = jnp.where(excess > 0, k_sel - count_ge(thr + 1), 1)
        rev_row = (seq - 1) - lax.broadcasted_iota(jnp.int32, (CNT_ROWS, BLOCK), 0)

        def tie_bisect(it, lo):
            cand = lo + lax.shift_left(jnp.int32(1), jnp.int32(seq.bit_length() - 2) - it)
            cnt = count_rows(lambda keys, off: (keys == thr) & (rev_row - off >= cand))
            return jnp.where(cnt >= need, cand, lo)

        rev = lax.fori_loop(0, seq.bit_length() - 1, tie_bisect, jnp.zeros((1, BLOCK), jnp.int32))
        jcut = jnp.where(excess > 0, (seq - 1) - rev, 2 ** 31 - 1)
        jcut_ref[...] = jnp.broadcast_to(jcut, jcut_ref.shape)

    jcut = jcut_ref[0:1, :]

    def emit(c, carry):
        off = pl.multiple_of(c * SEL_ROWS, SEL_ROWS)
        keys = keys_ref[pl.ds(off, SEL_ROWS), :]
        bar = jnp.where(off + k_row > jcut, thr + 1, thr)
        add = jnp.where(keys >= bar, 0.0, NEG)
        sel_ref[:, pl.ds(off, SEL_ROWS)] = jnp.concatenate(
            [add[j * BLOCK:(j + 1) * BLOCK].T for j in range(SEL_ROWS // BLOCK)], axis=1).astype(BF16)
        return carry

    lax.fori_loop(0, n_sel, emit, 0)

    def fill(c, carry):
        off = pl.multiple_of(c * SEL_ROWS, SEL_ROWS)
        sel_ref[:, pl.ds(off, SEL_ROWS)] = jnp.full((BLOCK, SEL_ROWS), NEG, BF16)
        return carry

    lax.fori_loop(n_sel, seq // SEL_ROWS, fill, 0)


def _a_select(pf, batch, seq):
    t = pf.shape[0]
    nq = seq // BLOCK
    k_sel = min(TOPK_MAX, seq // 4)
    nlane = IDX_HEADS * IDX_DIM
    r = np.arange(nlane)
    ph = np.zeros((nlane, IDX_HEADS * LANES), np.float32)
    plo = np.zeros((nlane, IDX_HEADS * LANES), np.float32)
    ph[r, (r // IDX_DIM) * LANES + r % IDX_DIM] = 1
    ph[r, (r // IDX_DIM) * LANES + 2 * IDX_DIM + r % IDX_DIM] = 1
    plo[r, (r // IDX_DIM) * LANES + IDX_DIM + r % IDX_DIM] = 1
    d = np.arange(IDX_DIM)
    pkh = np.zeros((LANES, LANES), np.float32)
    pkl = np.zeros((LANES, LANES), np.float32)
    pkh[d, d] = 1
    pkh[d, IDX_DIM + d] = 1
    pkl[d, 2 * IDX_DIM + d] = 1
    full = lambda shape: pl.BlockSpec(shape, lambda b, i: (0,) * len(shape))
    return pl.pallas_call(
        functools.partial(_a_select_kernel, seq=seq, k_sel=k_sel),
        grid=(batch, nq),
        in_specs=[pl.BlockSpec((BLOCK, 512), lambda b, i: (b * nq + i, 0)),
                  pl.BlockSpec((seq, LANES), lambda b, i: (b, nlane // LANES)),
                  full(ph.shape), full(plo.shape), full(pkh.shape), full(pkl.shape)],
        out_specs=pl.BlockSpec((BLOCK, seq), lambda b, i: (b * nq + i, 0)),
        out_shape=jax.ShapeDtypeStruct((t, seq), BF16),
        scratch_shapes=[pltpu.VMEM((seq, LANES), BF16), pltpu.VMEM((seq, LANES), jnp.int32),
                        pltpu.VMEM((IDX_HEADS // 2, LANES, 2 * LANES), BF16),
                        pltpu.VMEM((8, LANES), jnp.int32)],
        compiler_params=_cparams(("parallel", "arbitrary")),
        name="a_select",
    )(pf, pf, jnp.asarray(ph, BF16), jnp.asarray(plo, BF16), jnp.asarray(pkh, BF16),
      jnp.asarray(pkl, BF16))


BAND_RBLK = 4


def _band_kernel(*refs, has_sink, want_lse, kv_div, rblk):
    if has_sink:
        sink_ref, refs = refs[0], refs[1:]
    q_ref, hm_ref, kp_ref, kc_ref, vp_ref, vc_ref, bias_ref, o_ref = refs[:8]
    first = pl.program_id(1) == 0
    lane = lax.broadcasted_iota(jnp.int32, (BLOCK, LANES), 1)
    col = lax.broadcasted_iota(jnp.int32, (BLOCK, 2 * BLOCK), 1)
    no_prev = jnp.logical_and(first, col < BLOCK)
    for p in range(BRANCH_WIDTH // LANES):
        kcol = (p // kv_div) * LANES
        kk = jnp.concatenate([kp_ref[:, kcol:kcol + LANES], kc_ref[:, kcol:kcol + LANES]], axis=0)
        vv = jnp.concatenate([vp_ref[:, kcol:kcol + LANES], vc_ref[:, kcol:kcol + LANES]], axis=0)
        for i in range(rblk):
            rows = slice(i * BLOCK, (i + 1) * BLOCK)
            q = q_ref[rows, p * LANES:(p + 1) * LANES]
            kblk = kk[i * BLOCK:(i + 2) * BLOCK]
            vblk = vv[i * BLOCK:(i + 2) * BLOCK]
            outs, lses = [], []
            for e in range(2):
                s = lax.dot_general(q * hm_ref[e], kblk, _CONTRACT_LANES, preferred_element_type=F32)
                s = s + bias_ref[2 * p + e]
                if i == 0:
                    s = jnp.where(no_prev, NEG, s)
                m = jnp.max(s, axis=1, keepdims=True)
                if has_sink:
                    sk = sink_ref[2 * p + e]
                    m = jnp.maximum(m, sk)
                pr = jnp.exp(s - m)
                den = jnp.sum(pr, axis=1, keepdims=True)
                if has_sink:
                    den = den + jnp.exp(sk - m)
                outs.append(jnp.dot(pr.astype(BF16), vblk, preferred_element_type=F32) / den)
                lses.append(m + jnp.log(den))
            o_ref[rows, p * LANES:(p + 1) * LANES] = jnp.where(lane < HEAD_DIM, outs[0], outs[1]).astype(o_ref.dtype)
            if want_lse:
                refs[8][rows, p * LANES:(p + 1) * LANES] = jnp.where(lane < HEAD_DIM, lses[0], lses[1])


def _band_attn(q_arr, k_arr, v_arr, bias, nsub, sub_len, qcb, kcb, vcb, kv_w, sinks=None, want_lse=False):
    t = q_arr.shape[0]
    rblk = min(BAND_RBLK, sub_len // BLOCK)
    step_rows = rblk * BLOCK
    nbs = sub_len // step_rows
    kv_div = BRANCH_WIDTH // kv_w

    def cur(u, n):
        return u * nbs + n

    def prev(u, n):
        return jnp.where(n == 0, u * nbs * rblk, (u * nbs + n) * rblk - 1)

    in_specs = [
        pl.BlockSpec((step_rows, BRANCH_WIDTH), lambda u, n: (cur(u, n), qcb)),
        pl.BlockSpec((2, 1, LANES), lambda u, n: (0, 0, 0)),
        pl.BlockSpec((BLOCK, kv_w), lambda u, n: (prev(u, n), kcb)),
        pl.BlockSpec((step_rows, kv_w), lambda u, n: (cur(u, n), kcb)),
        pl.BlockSpec((BLOCK, kv_w), lambda u, n: (prev(u, n), vcb)),
        pl.BlockSpec((step_rows, kv_w), lambda u, n: (cur(u, n), vcb)),
        pl.BlockSpec(bias.shape, lambda u, n: (0, 0, 0)),
    ]
    args = [q_arr, _pair_masks(), k_arr, k_arr, v_arr, v_arr, bias]
    if sinks is not None:
        in_specs = [pl.BlockSpec(memory_space=pltpu.SMEM)] + in_specs
        args = [sinks] + args
    o_spec = pl.BlockSpec((step_rows, BRANCH_WIDTH), lambda u, n: (cur(u, n), 0))
    out_specs = [o_spec]
    out_shape = [jax.ShapeDtypeStruct((t, BRANCH_WIDTH), BF16)]
    if want_lse:
        out_specs.append(o_spec)
        out_shape.append(jax.ShapeDtypeStruct((t, BRANCH_WIDTH), F32))
    return pl.pallas_call(
        functools.partial(_band_kernel, has_sink=sinks is not None, want_lse=want_lse, kv_div=kv_div, rblk=rblk),
        grid=(nsub, nbs),
        in_specs=in_specs,
        out_specs=out_specs,
        out_shape=out_shape,
        compiler_params=_cparams(("parallel", "arbitrary")),
        name="band_attn",
    )(*args)


def _toeplitz(w, rows, cols):
    length = rows + cols - 1
    u = jnp.pad(w[..., ::-1], [(0, 0)] * (w.ndim - 1) + [(0, 1)])
    flat = jnp.broadcast_to(u[..., None, :], w.shape[:-1] + (rows, length + 1)).reshape(w.shape[:-1] + (-1,))
    skew = flat[..., :rows * length].reshape(w.shape[:-1] + (rows, length))
    return skew[..., rows - 1:rows - 1 + cols]


def _band_bias(table, step, max_dist):
    rel = BLOCK + np.arange(BLOCK)[:, None] - np.arange(2 * BLOCK)[None, :]
    rel_vec = np.arange(BLOCK - (2 * BLOCK - 1), 2 * BLOCK)
    bias = _toeplitz(table[_t5_bucket(jnp.asarray(rel_vec * step))].T.astype(F32), BLOCK, 2 * BLOCK)
    ok = (rel >= 0) & (rel <= max_dist)
    return jnp.where(jnp.asarray(ok)[None], bias, NEG)


def _merge_kernel(x_ref, ya_ref, yb0_ref, yb1_ref, yb2_ref, l0_ref, l1_ref, l2_ref, yc_ref, yd_ref,
                  za_ref, zb_ref, zc_ref, zd_ref, ga_ref, gb_ref, gc_ref, gd_ref, wb_ref, wo_ref, *rest):
    if len(rest) == 7:
        ng_ref, o_ref, xn_ref, y1_ref, y2_ref, s1_ref, s2_ref = rest
    else:
        ng_ref = xn_ref = None
        o_ref, y1_ref, y2_ref, s1_ref, s2_ref = rest
    ncol = BRANCH_WIDTH // LANES
    for src, dst in ((yb1_ref, y1_ref), (yb2_ref, y2_ref), (l1_ref, s1_ref), (l2_ref, s2_ref)):
        dil = src.shape[1]
        for r in range(dil):
            blk = src[0, r].astype(F32)
            for c in range(ncol):
                dst[c, pl.ds(r, src.shape[2], stride=dil), :] = blk[:, c * LANES:(c + 1) * LANES]
    wide = lambda ref: jnp.concatenate([ref[c] for c in range(ncol)], axis=1)
    l0, l1, l2 = l0_ref[0, 0], wide(s1_ref), wide(s2_ref)
    mx = jnp.maximum(jnp.maximum(l0, l1), l2)
    w0, w1, w2 = jnp.exp(l0 - mx), jnp.exp(l1 - mx), jnp.exp(l2 - mx)
    yb = (w0 * yb0_ref[0, 0].astype(F32) + w1 * wide(y1_ref) + w2 * wide(y2_ref)) / (w0 + w1 + w2)
    ys = (ya_ref[...].astype(F32), yb, yc_ref[...].astype(F32), yd_ref[...].astype(F32))
    zs = (za_ref, zb_ref, zc_ref, zd_ref)
    gs = (ga_ref, gb_ref, gc_ref, gd_ref)
    merged = jnp.zeros(o_ref.shape, F32)
    for n in range(N_BRANCH):
        z = zs[n][...].astype(F32)
        u = (ys[n] * (z * jax.nn.sigmoid(z))).astype(BF16)
        proj = jnp.dot(u, wb_ref[n], preferred_element_type=F32)
        merged = merged + jax.nn.sigmoid(gs[n][...].astype(F32)) * proj
    x_new = x_ref[...] + jnp.dot(merged.astype(BF16), wo_ref[...], preferred_element_type=F32)
    o_ref[...] = x_new
    if xn_ref is not None:
        xn_ref[...] = _rms_bf16(x_new, ng_ref[...])


def _merge(x2, ya, yb, lb, yc, yd, pp, wb, wo, batch, seq, next_gain=None, tm=512):
    t = x2.shape[0]
    bw = BRANCH_WIDTH
    nsb = seq // tm
    row = lambda w, c: pl.BlockSpec((tm, w), lambda i: (i, c))

    def sub(dil):
        return pl.BlockSpec((1, dil, tm // dil, bw), lambda i: (i // nsb, 0, i % nsb, 0))

    dils = [dil for _, dil in B_GROUPS]
    yb = [a.reshape(batch, dil, seq // dil, bw) for a, dil in zip(yb, dils)]
    lb = [a.reshape(batch, dil, seq // dil, bw) for a, dil in zip(lb, dils)]
    in_specs = ([row(D_MODEL, 0), row(bw, 0)] + [sub(dil) for dil in dils] * 2 + [row(bw, 0), row(bw, 0)]
                + [row(bw, PP_AZ // bw), row(bw, PP_BZ // bw), row(bw, PP_CZ // bw), row(bw, PP_DZ // bw)]
                + [row(D_MODEL, PP_G // D_MODEL + n) for n in range(N_BRANCH)]
                + [pl.BlockSpec((N_BRANCH, bw, D_MODEL), lambda i: (0, 0, 0)),
                   pl.BlockSpec((D_MODEL, D_MODEL), lambda i: (0, 0))])
    args = [x2, ya, yb[0], yb[1], yb[2], lb[0], lb[1], lb[2], yc, yd, pp, pp, pp, pp, pp, pp, pp, pp, wb, wo]
    out_specs = [row(D_MODEL, 0)]
    out_shape = [jax.ShapeDtypeStruct((t, D_MODEL), F32)]
    if next_gain is not None:
        in_specs.append(pl.BlockSpec((1, D_MODEL), lambda i: (0, 0)))
        args.append(next_gain.reshape(1, D_MODEL))
        out_specs.append(row(D_MODEL, 0))
        out_shape.append(jax.ShapeDtypeStruct((t, D_MODEL), BF16))
    return pl.pallas_call(
        _merge_kernel,
        grid=(t // tm,),
        in_specs=in_specs,
        out_specs=out_specs,
        out_shape=out_shape,
        scratch_shapes=[pltpu.VMEM((bw // LANES, tm, LANES), F32)] * 4,
        compiler_params=_cparams(("parallel",)),
        name="merge",
    )(*args)


def _layer_weights(w_in, qk_gain_a, qk_gain_b, qk_gain_d):
    def cols(start, width):
        return w_in[:, :, start:start + width]

    n_bq = len(B_GROUPS) * B_HEADS * HEAD_DIM
    dk = cols(_D0 + 512, 128)
    dv = cols(_D0 + 640, 128)
    dup = lambda a: jnp.concatenate([a[:, :, :64], a[:, :, :64], a[:, :, 64:], a[:, :, 64:]], axis=-1)
    w_pn = jnp.concatenate([cols(_A0, 512), cols(_A0 + 512, 512), cols(_D0, 512), dup(dk), dup(dv)],
                           axis=-1).astype(BF16)
    w_pp = jnp.concatenate([cols(_G0, N_BRANCH * D_MODEL), cols(_A0 + 1024, 512), cols(_A0 + 1536, 512),
                            cols(_B0 + 3 * n_bq, 512), cols(_C0 + 416, 512), cols(_D0 + 768, 512)],
                           axis=-1).astype(BF16)
    w_bg = [jnp.concatenate([cols(_B0 + g * 512, 512), cols(_B0 + n_bq + g * 512, 512),
                             cols(_B0 + 2 * n_bq + g * 512, 512)], axis=-1).astype(BF16)
            for g in range(len(B_GROUPS))]
    depth = w_in.shape[0]
    zeros = lambda w: jnp.zeros((depth, D_MODEL, w), w_in.dtype)
    w_pf = jnp.concatenate([cols(_A0 + 2048, 296), zeros(216), cols(_C0, 416), zeros(96)], axis=-1).astype(BF16)

    scale = HEAD_DIM ** -0.5
    tile = lambda g, reps: jnp.tile(g, (1, reps))
    hg = jnp.concatenate([tile(qk_gain_a[:, 0] * scale, 8), tile(qk_gain_a[:, 1], 8),
                          tile(qk_gain_d[:, 0] * scale, 8), tile(qk_gain_d[:, 1], 4),
                          jnp.ones((depth, 256), F32)], axis=-1)
    flag = jnp.concatenate([jnp.ones((PN_DV,), F32), jnp.zeros((PN_W - PN_DV,), F32)])
    hg_b = jnp.concatenate([tile(qk_gain_b[:, 0] * scale, 8), tile(qk_gain_b[:, 1], 8),
                            jnp.ones((depth, 512), F32)], axis=-1)
    flag_b = jnp.concatenate([jnp.ones((2 * BRANCH_WIDTH,), F32), jnp.zeros((BRANCH_WIDTH,), F32)])
    return w_pn, w_pp, w_bg, w_pf, hg, flag, hg_b, flag_b


def _c_weights(qk_gain_c, w_q_b, w_kv_b, seq):
    src_q, src_k, gain_idx, head, pe_src, rope_j, rope_half = _c_layout_tables()
    take = lambda w, src: jnp.where(jnp.asarray(src >= 0), jnp.take(w, jnp.asarray(np.maximum(src, 0)), axis=-1), 0.0)
    wq = take(w_q_b, src_q).astype(BF16)
    wk = take(w_kv_b, src_k).astype(BF16)
    v_src = np.array([h * (C_NOPE + C_V) + C_NOPE + c for h in range(C_HEADS) for c in range(C_V)])
    wv = jnp.take(w_kv_b, jnp.asarray(v_src), axis=-1).astype(BF16)
    ppe = np.zeros((LANES, C_QK_W), np.float32)
    ln = np.nonzero(pe_src >= 0)[0]
    ppe[pe_src[ln], ln] = 1
    hp = head[:C_PAIR]
    grp = ((hp[:, None] == hp[None, :]) & (hp[:, None] >= 0)).astype(np.float32)
    g_take = lambda g: jnp.where(jnp.asarray(gain_idx >= 0), jnp.take(g, jnp.asarray(np.maximum(gain_idx, 0)), axis=-1), 0.0)
    gq = g_take(qk_gain_c[:, 0]) * (C_NOPE + C_ROPE) ** -0.5
    gk = g_take(qk_gain_c[:, 1])
    freq = ROPE_THETA ** (-jnp.arange(C_ROPE_HALF, dtype=F32) / C_ROPE_HALF)
    ang = jnp.arange(seq).astype(F32)[:, None] * freq[None, :]
    cos_j, sin_j = jnp.cos(ang), jnp.sin(ang)
    place = np.zeros((C_ROPE_HALF, C_PAIR), np.float32)
    lanes = np.nonzero(rope_j[:C_PAIR] >= 0)[0]
    place[rope_j[lanes], lanes] = 1
    is_rope = jnp.asarray(rope_j[:C_PAIR] >= 0)
    hi = lax.Precision.HIGHEST
    cos = jnp.where(is_rope, jnp.dot(cos_j, jnp.asarray(place), precision=hi), 1.0)
    sin = jnp.dot(sin_j, jnp.asarray(place), precision=hi)
    s1 = jnp.where(jnp.asarray(rope_half[:C_PAIR] == 0), -sin, 0.0)
    s2 = jnp.where(jnp.asarray(rope_half[:C_PAIR] == 1), sin, 0.0)
    return wq, wk, wv, jnp.asarray(ppe, BF16), jnp.asarray(grp, BF16), gq, gk, cos, s1, s2


def _a_bias_tiles(table):
    dist_vec = np.arange(-(BLOCK - 1), A_BIAS_TILES * BLOCK)
    vec = table[_t5_bucket(jnp.asarray(dist_vec))].T.astype(F32)
    windows = jnp.stack([vec[:, d * BLOCK:d * BLOCK + 2 * BLOCK - 1] for d in range(A_BIAS_TILES)], axis=1)
    return _toeplitz(windows, BLOCK, BLOCK)


def kernel(x, norm_gain, w_in, qk_gain_a, qk_gain_b, qk_gain_c, qk_gain_d, c_q_gain, c_kv_gain,
           w_q_b, w_kv_b, sinks, rel_bias, w_branch, w_out):
    batch, seq, d_model = x.shape
    depth = w_in.shape[0]
    t = batch * seq
    x2 = x.reshape(t, d_model)

    w_pn, w_pp, w_bg, w_pf, hg, flag, hg_b, flag_b = _layer_weights(w_in, qk_gain_a, qk_gain_b, qk_gain_d)
    wq, wk, wv, ppe, grp, gq, gk, cos, s1, s2 = _c_weights(qk_gain_c, w_q_b, w_kv_b, seq)
    wb = w_branch.astype(BF16)
    wo = w_out.astype(BF16)

    bias_a = _a_bias_tiles(rel_bias[:, :A_HEADS])
    bias_b = [_band_bias(rel_bias[:, A_HEADS + g * B_HEADS:A_HEADS + (g + 1) * B_HEADS], dil, window // dil)
              for g, (window, dil) in enumerate(B_GROUPS)]
    bias_d = _band_bias(rel_bias[:, N_BIAS_HEADS - D_HEADS:], 1, D_WINDOW - 1)

    xn = _norm(x2, norm_gain[0])
    for l in range(depth):
        pn = _proj(xn, w_pn[l], BF16, head_gain=hg[l], flag=flag)
        pp = _proj(xn, w_pp[l], BF16)
        pf = _proj(xn, w_pf[l], F32)

        sel = _a_select(pf, batch, seq)
        ya = _a_attn(pn, pp, sel, bias_a, batch, seq)

        yb, lb = [], []
        for g, (window, dil) in enumerate(B_GROUPS):
            bg = _proj(xn, w_bg[g][l], BF16, head_gain=hg_b[l], flag=flag_b, dil=dil, seq=seq)
            o, lse = _band_attn(bg, bg, bg, bias_b[g], batch * dil, seq // dil, 0, 1, 2, BRANCH_WIDTH,
                                want_lse=True)
            yb.append(o)
            lb.append(lse)

        qc, kc, vc = _c_prep(pf, c_q_gain[l].reshape(1, -1), c_kv_gain[l].reshape(1, -1), wq[l], wk[l], wv[l],
                             ppe, grp, gq[l].reshape(1, -1), gk[l].reshape(1, -1), cos, s1, s2, seq)
        yc = _c_attn(qc, kc, vc, batch, seq)

        (yd,) = _band_attn(pn, pn, pn, bias_d, batch, seq, PN_DQ // BRANCH_WIDTH, PN_DK // 256, PN_DV // 256,
                           256, sinks=sinks[l])

        if l + 1 < depth:
            x2, xn = _merge(x2, ya, yb, lb, yc, yd, pp, wb[l], wo[l], batch, seq, next_gain=norm_gain[l + 1])
        else:
            (x2,) = _merge(x2, ya, yb, lb, yc, yd, pp, wb[l], wo[l], batch, seq)
    return x2.reshape(batch, seq, d_model)
```

```python
import functools
import math

import numpy as np
import jax
import jax.numpy as jnp
from jax import lax
from jax.experimental import pallas as pl
from jax.experimental.pallas import tpu as pltpu

F32 = jnp.float32
BF16 = jnp.bfloat16

D_MODEL = 1024
BLOCK = 128
HEAD_DIM = 64
N_BRANCH = 4
BRANCH_WIDTH = 512
EPS = 1e-6
A_HEADS = 8
IDX_HEADS = 8
IDX_DIM = 32
TOPK_MAX = 256
B_GROUPS = ((128, 1), (512, 4), (2048, 16))
B_HEADS = 8
C_HEADS = 8
C_NOPE = 64
C_ROPE = 32
C_V = 64
C_Q_LORA = 256
C_KV_LORA = 128
ROPE_THETA = 10000.0
D_HEADS = 8
D_KV_HEADS = 2
D_WINDOW = 128
NUM_BUCKETS = 32
MAX_DISTANCE = 2048
N_BIAS_HEADS = A_HEADS + len(B_GROUPS) * B_HEADS + D_HEADS

LANES = 128
NEG = -1e30
INT_MIN = -(2 ** 31)

_A0 = 0
_B0 = 2344
_C0 = 7464
_D0 = 8392
_G0 = 9672

PN_AQ, PN_AK, PN_DQ, PN_DK, PN_DV, PN_W = 0, 512, 1024, 1536, 1792, 2048
PP_G, PP_AV, PP_AZ, PP_BZ, PP_CZ, PP_DZ, PP_W = 0, 4096, 4608, 5120, 5632, 6144, 6656
PF_W = 1024
BG_W = 3 * BRANCH_WIDTH

A_BIAS_TILES = MAX_DISTANCE // BLOCK + 2


def _cparams(sem, vmem_mb=48):
    return pltpu.CompilerParams(dimension_semantics=sem, vmem_limit_bytes=vmem_mb * 1024 * 1024)


def _t5_bucket(dist):
    max_exact = NUM_BUCKETS // 2
    d = jnp.maximum(dist, 0)
    logd = jnp.log(jnp.maximum(d, 1).astype(F32) / max_exact)
    large = max_exact + (logd / math.log(MAX_DISTANCE / max_exact) * (NUM_BUCKETS - max_exact)).astype(jnp.int32)
    return jnp.where(d < max_exact, d, jnp.minimum(large, NUM_BUCKETS - 1))


def _rms_bf16(x, gain):
    return (x * lax.rsqrt(jnp.mean(x * x, axis=-1, keepdims=True) + EPS) * gain).astype(BF16)


def _norm_kernel(x_ref, g_ref, o_ref):
    o_ref[...] = _rms_bf16(x_ref[...], g_ref[...])


def _norm(x2, gain, tm=1024):
    t, d = x2.shape
    return pl.pallas_call(
        _norm_kernel,
        grid=(t // tm,),
        in_specs=[pl.BlockSpec((tm, d), lambda i: (i, 0)), pl.BlockSpec((1, d), lambda i: (0, 0))],
        out_specs=pl.BlockSpec((tm, d), lambda i: (i, 0)),
        out_shape=jax.ShapeDtypeStruct((t, d), BF16),
        compiler_params=_cparams(("parallel",)),
        name="norm",
    )(x2, gain.reshape(1, d))


def _proj_kernel(xn_ref, w_ref, *rest, norm, dil):
    rest = list(rest)
    res_ref = rest.pop(-1) if dil is not None and dil > 1 else None
    if norm:
        hg_ref, flag_ref, bd_ref, o_ref = rest
    else:
        (o_ref,) = rest

    def emit(h):
        if dil is None:
            o_ref[...] = h.astype(o_ref.dtype)
        elif dil == 1:
            o_ref[0, 0] = h.astype(o_ref.dtype)
        else:
            sub = h.shape[0] // dil
            for c in range(h.shape[1] // LANES):
                res_ref[c] = h[:, c * LANES:(c + 1) * LANES]
            for r in range(dil):
                o_ref[0, r] = jnp.concatenate(
                    [res_ref[c, pl.ds(r, sub, stride=dil), :] for c in range(h.shape[1] // LANES)],
                    axis=1).astype(o_ref.dtype)

    h = jnp.dot(xn_ref[...], w_ref[...], preferred_element_type=F32)
    if norm:
        ss = jnp.dot((h * h).astype(BF16), bd_ref[...], preferred_element_type=F32)
        scale = lax.rsqrt(ss * (1.0 / HEAD_DIM) + EPS) * hg_ref[...]
        h = h * jnp.where(flag_ref[...] > 0, scale, 1.0)
    emit(h)


def _proj(xn, w, out_dtype, head_gain=None, flag=None, dil=None, seq=None, tm=1024, tn=512):
    t, d = xn.shape
    n = w.shape[1]
    norm = head_gain is not None
    scratch = []
    if dil is None:
        out_spec = pl.BlockSpec((tm, tn), lambda i, j: (i, j))
        out_shape = jax.ShapeDtypeStruct((t, n), out_dtype)
        kdil = None
    else:
        nsb = seq // tm
        out_spec = pl.BlockSpec((1, dil, tm // dil, tn), lambda i, j: (i // nsb, 0, i % nsb, j))
        out_shape = jax.ShapeDtypeStruct((t // seq, dil, seq // dil, n), out_dtype)
        kdil = dil
        if dil > 1:
            scratch.append(pltpu.VMEM((tn // LANES, tm, LANES), F32))
    in_specs = [pl.BlockSpec((tm, d), lambda i, j: (i, 0)),
                pl.BlockSpec((d, tn), lambda i, j: (0, j))]
    args = [xn, w]
    if norm:
        lane = np.arange(tn)
        bd = jnp.asarray((lane[:, None] // HEAD_DIM == lane[None, :] // HEAD_DIM), BF16)
        in_specs += [pl.BlockSpec((1, tn), lambda i, j: (0, j)),
                     pl.BlockSpec((1, tn), lambda i, j: (0, j)),
                     pl.BlockSpec((tn, tn), lambda i, j: (0, 0))]
        args += [head_gain.reshape(1, n), flag.reshape(1, n), bd]
    out = pl.pallas_call(
        functools.partial(_proj_kernel, norm=norm, dil=kdil),
        grid=(t // tm, n // tn),
        in_specs=in_specs,
        out_specs=out_spec,
        out_shape=out_shape,
        scratch_shapes=scratch,
        compiler_params=_cparams(("parallel", "parallel")),
        name=("proj_norm" if norm else "proj") + ("" if dil is None else "_dil%d" % dil),
    )(*args)
    return out.reshape(t, n)


C_PAIR = 256
C_QK_W = (C_HEADS // 2) * C_PAIR
C_ROPE_HALF = C_ROPE // 2


def _c_lane(h, c):
    base = (h // 2) * C_PAIR
    e = h % 2
    if c < C_NOPE:
        return base + e * C_NOPE + c
    return base + 2 * C_NOPE + e * C_ROPE + (c - C_NOPE)


def _c_layout_tables():
    src_q = np.full(C_QK_W, -1, np.int64)
    src_k = np.full(C_QK_W, -1, np.int64)
    gain_idx = np.full(C_QK_W, -1, np.int64)
    head = np.full(C_QK_W, -1, np.int64)
    pe_src = np.full(C_QK_W, -1, np.int64)
    rope_j = np.full(C_QK_W, -1, np.int64)
    rope_half = np.zeros(C_QK_W, np.int64)
    for h in range(C_HEADS):
        for c in range(C_NOPE + C_ROPE):
            ln = _c_lane(h, c)
            src_q[ln] = h * (C_NOPE + C_ROPE) + c
            gain_idx[ln] = c
            head[ln] = h
            if c < C_NOPE:
                src_k[ln] = h * (C_NOPE + C_V) + c
            else:
                r = c - C_NOPE
                pe_src[ln] = r
                rope_j[ln] = r % C_ROPE_HALF
                rope_half[ln] = r // C_ROPE_HALF
    return src_q, src_k, gain_idx, head, pe_src, rope_j, rope_half


def _c_prep_kernel(pf_ref, cqg_ref, ckvg_ref, wq_ref, wk_ref, wv_ref, ppe_ref, grp_ref,
                   gq_ref, gk_ref, cos_ref, s1_ref, s2_ref, q_ref, k_ref, v_ref):
    blk = pf_ref[...]
    cq = blk[:, :C_Q_LORA]
    ckv = blk[:, C_Q_LORA:C_Q_LORA + C_KV_LORA]
    pe = blk[:, C_Q_LORA + C_KV_LORA:]

    def rms(v, g):
        return v * lax.rsqrt(jnp.mean(v * v, axis=-1, keepdims=True) + EPS) * g

    def head_norm_rope(raw, gain):
        sq = (raw * raw).astype(BF16)
        ss = jnp.concatenate([jnp.dot(sq[:, p * C_PAIR:(p + 1) * C_PAIR], grp_ref[...], preferred_element_type=F32)
                              for p in range(C_QK_W // C_PAIR)], axis=1)
        y = raw * lax.rsqrt(ss * (1.0 / (C_NOPE + C_ROPE)) + EPS) * gain
        up = pltpu.roll(y, C_QK_W - C_ROPE_HALF, 1)
        dn = pltpu.roll(y, C_ROPE_HALF, 1)
        wide = lambda ref: jnp.concatenate([ref[...]] * (C_QK_W // C_PAIR), axis=1)
        return y * wide(cos_ref) + up * wide(s1_ref) + dn * wide(s2_ref)

    cqn = rms(cq, cqg_ref[...]).astype(BF16)
    q_raw = jnp.dot(cqn, wq_ref[...], preferred_element_type=F32)
    q_ref[...] = head_norm_rope(q_raw, gq_ref[...]).astype(BF16)

    ckvn = rms(ckv, ckvg_ref[...]).astype(BF16)
    pe_hi = pe.astype(BF16)
    pe_lo = (pe - pe_hi.astype(F32)).astype(BF16)
    k_raw = (jnp.dot(ckvn, wk_ref[...], preferred_element_type=F32)
             + jnp.dot(pe_hi, ppe_ref[...], preferred_element_type=F32)
             + jnp.dot(pe_lo, ppe_ref[...], preferred_element_type=F32))
    k_ref[...] = head_norm_rope(k_raw, gk_ref[...]).astype(BF16)
    v_ref[...] = jnp.dot(ckvn, wv_ref[...], preferred_element_type=F32).astype(BF16)


def _c_prep(pf, cqg, ckvg, wq, wk, wv, ppe, grp, gq, gk, cos, s1, s2, seq, tm=512):
    t = pf.shape[0]
    nsb = seq // tm
    full = lambda shape: pl.BlockSpec(shape, lambda i: (0,) * len(shape))
    tab = pl.BlockSpec((tm, C_PAIR), lambda i: (i % nsb, 0))
    return pl.pallas_call(
        _c_prep_kernel,
        grid=(t // tm,),
        in_specs=[pl.BlockSpec((tm, 512), lambda i: (i, 1)),
                  full((1, C_Q_LORA)), full((1, C_KV_LORA)),
                  full((C_Q_LORA, C_QK_W)), full((C_KV_LORA, C_QK_W)), full((C_KV_LORA, BRANCH_WIDTH)),
                  full((LANES, C_QK_W)), full((C_PAIR, C_PAIR)),
                  full((1, C_QK_W)), full((1, C_QK_W)), tab, tab, tab],
        out_specs=[pl.BlockSpec((tm, C_QK_W), lambda i: (i, 0)),
                   pl.BlockSpec((tm, C_QK_W), lambda i: (i, 0)),
                   pl.BlockSpec((tm, BRANCH_WIDTH), lambda i: (i, 0))],
        out_shape=[jax.ShapeDtypeStruct((t, C_QK_W), BF16),
                   jax.ShapeDtypeStruct((t, C_QK_W), BF16),
                   jax.ShapeDtypeStruct((t, BRANCH_WIDTH), BF16)],
        compiler_params=_cparams(("parallel",)),
        name="c_prep",
    )(pf, cqg, ckvg, wq, wk, wv, ppe, grp, gq, gk, cos, s1, s2)


def _flash_update(e, s, v_ones, m_ref, acc_ref, rows=slice(None)):
    m_prev = m_ref[e, rows]
    m_new = jnp.maximum(m_prev, jnp.max(s, axis=1, keepdims=True))
    alpha = jnp.exp(m_prev - m_new)
    p = jnp.exp(s - jnp.concatenate([m_new] * (s.shape[1] // LANES), axis=1))
    acc_ref[e, rows] = (jnp.concatenate([alpha, alpha], axis=1) * acc_ref[e, rows]
                        + jnp.dot(p.astype(BF16), v_ones, preferred_element_type=F32))
    m_ref[e, rows] = m_new


def _flash_init(m_ref, acc_ref):
    m_ref[...] = jnp.full(m_ref.shape, NEG, F32)
    acc_ref[...] = jnp.zeros(acc_ref.shape, F32)


def _flash_finish(o_ref, acc_ref):
    lane = lax.broadcasted_iota(jnp.int32, o_ref.shape, 1)
    o0 = acc_ref[0, :, :LANES] / acc_ref[0, :, LANES:]
    o1 = acc_ref[1, :, :LANES] / acc_ref[1, :, LANES:]
    o_ref[...] = jnp.where(lane < HEAD_DIM, o0, o1).astype(o_ref.dtype)


def _with_ones(v_chunk):
    return jnp.concatenate([v_chunk, jnp.ones(v_chunk.shape, v_chunk.dtype)], axis=1)


_CONTRACT_LANES = (((1,), (1,)), ((), ()))


def _head_masks(width, ranges0, ranges1):
    m = np.zeros((2, 1, width), np.float32)
    for e, ranges in enumerate((ranges0, ranges1)):
        for lo, hi in ranges:
            m[e, 0, lo:hi] = 1
    return jnp.asarray(m, BF16)


def _pair_masks():
    return _head_masks(LANES, [(0, HEAD_DIM)], [(HEAD_DIM, LANES)])


def _transpose_keys(k_ref, kt_ref, rows=512):
    def body(c, carry):
        off = pl.multiple_of(c * rows, rows)
        kt_ref[:, pl.ds(off, rows)] = k_ref[pl.ds(off, rows), :].astype(F32).T.astype(kt_ref.dtype)
        return carry
    lax.fori_loop(0, k_ref.shape[0] // rows, body, 0)


def _chunk_loop(chunk, n, unroll):
    start = 0
    width = unroll
    while width >= 1:
        def body(g, carry, width=width, start=start):
            for j in range(width):
                chunk(start + g * width + j)
            return carry
        groups = (n - start) // width
        lax.fori_loop(0, groups, body, 0)
        start = start + groups * width
        width //= 2


def _c_attn_kernel(q_ref, hm_ref, k_ref, v_ref, o_ref, kt_ref, m_ref, acc_ref, *, tq, tk, unroll):
    qi = pl.program_id(2)

    @pl.when(qi == 0)
    def _():
        _transpose_keys(k_ref, kt_ref)

    q = q_ref[...]
    qs = (q * hm_ref[0], q * hm_ref[1])
    _flash_init(m_ref, acc_ref)
    n_full = (qi * tq) // tk

    def chunk(c, row0=0, diagonal=False):
        off = pl.multiple_of(c * tk, tk)
        kc = kt_ref[:, pl.ds(off, tk)]
        vc = _with_ones(v_ref[pl.ds(off, tk), :])
        for e in range(2):
            if not diagonal:
                rows = slice(row0, tq)
                _flash_update(e, jnp.dot(qs[e][rows], kc, preferred_element_type=F32), vc, m_ref, acc_ref, rows)
                continue
            rows = slice(row0, row0 + tk)
            s = jnp.dot(qs[e][rows], kc, preferred_element_type=F32)
            s = jnp.where(lax.broadcasted_iota(jnp.int32, s.shape, 1) <= lax.broadcasted_iota(jnp.int32, s.shape, 0),
                          s, NEG)
            _flash_update(e, s, vc, m_ref, acc_ref, rows)
            if row0 + tk < tq:
                rows = slice(row0 + tk, tq)
                _flash_update(e, jnp.dot(qs[e][rows], kc, preferred_element_type=F32), vc, m_ref, acc_ref, rows)

    _chunk_loop(chunk, n_full, unroll)
    for j in range(tq // tk):
        chunk(n_full + j, row0=j * tk, diagonal=True)
    _flash_finish(o_ref, acc_ref)


def _c_attn(qc, kc, vc, batch, seq, tq=1024, tk=512, unroll=4):
    t = qc.shape[0]
    nq = seq // tq
    npair = C_HEADS // 2
    return pl.pallas_call(
        functools.partial(_c_attn_kernel, tq=tq, tk=tk, unroll=unroll),
        grid=(batch, npair, nq),
        in_specs=[pl.BlockSpec((tq, C_PAIR), lambda b, p, i: (b * nq + i, p)),
                  pl.BlockSpec((2, 1, C_PAIR), lambda b, p, i: (0, 0, 0)),
                  pl.BlockSpec((seq, C_PAIR), lambda b, p, i: (b, p)),
                  pl.BlockSpec((seq, LANES), lambda b, p, i: (b, p))],
        out_specs=pl.BlockSpec((tq, LANES), lambda b, p, i: (b * nq + i, p)),
        out_shape=jax.ShapeDtypeStruct((t, BRANCH_WIDTH), BF16),
        scratch_shapes=[pltpu.VMEM((C_PAIR, seq), BF16),
                        pltpu.VMEM((2, tq, LANES), F32), pltpu.VMEM((2, tq, 2 * LANES), F32)],
        compiler_params=_cparams(("parallel", "parallel", "arbitrary")),
        name="c_attn",
    )(qc, _head_masks(C_PAIR, [(0, C_NOPE), (2 * C_NOPE, 2 * C_NOPE + C_ROPE)],
                      [(C_NOPE, 2 * C_NOPE), (2 * C_NOPE + C_ROPE, 2 * C_NOPE + 2 * C_ROPE)]), kc, vc)


def _a_attn_kernel(q_ref, hm_ref, k_ref, v_ref, sel_ref, bias_ref, o_ref, kt_ref, m_ref, acc_ref,
                   *, tq, tk, unroll):
    qi = pl.program_id(2)

    @pl.when(qi == 0)
    def _():
        _transpose_keys(k_ref, kt_ref)

    q = q_ref[...]
    qs = (q * hm_ref[0], q * hm_ref[1])
    _flash_init(m_ref, acc_ref)
    n_chunks = ((qi + 1) * tq + tk - 1) // tk

    def chunk(c):
        off = pl.multiple_of(c * tk, tk)
        kc = kt_ref[:, pl.ds(off, tk)]
        vc = _with_ones(v_ref[pl.ds(off, tk), :])
        selmask = sel_ref[:, pl.ds(off, tk)].astype(F32)
        for e in range(2):
            s = jnp.dot(qs[e], kc, preferred_element_type=F32)
            rows = []
            for i in range(tq // BLOCK):
                tiles = []
                for j in range(tk // BLOCK):
                    d = (qi * (tq // BLOCK) + i) - (c * (tk // BLOCK) + j)
                    d = jnp.clip(d, 0, A_BIAS_TILES - 1)
                    tiles.append(bias_ref[e, d])
                rows.append(jnp.concatenate(tiles, axis=1))
            s = s + jnp.concatenate(rows, axis=0) + selmask
            _flash_update(e, s, vc, m_ref, acc_ref)

    _chunk_loop(chunk, n_chunks, unroll)
    _flash_finish(o_ref, acc_ref)


def _a_attn(pn, pp, sel, bias_tiles, batch, seq, tq=512, tk=512, unroll=4):
    t = pn.shape[0]
    nq = seq // tq
    npair = A_HEADS // 2
    qcol, kcol, vcol = PN_AQ // LANES, PN_AK // LANES, PP_AV // LANES
    return pl.pallas_call(
        functools.partial(_a_attn_kernel, tq=tq, tk=tk, unroll=unroll),
        grid=(batch, npair, nq),
        in_specs=[pl.BlockSpec((tq, LANES), lambda b, p, i: (b * nq + i, qcol + p)),
                  pl.BlockSpec((2, 1, LANES), lambda b, p, i: (0, 0, 0)),
                  pl.BlockSpec((seq, LANES), lambda b, p, i: (b, kcol + p)),
                  pl.BlockSpec((seq, LANES), lambda b, p, i: (b, vcol + p)),
                  pl.BlockSpec((tq, seq), lambda b, p, i: (b * nq + i, 0)),
                  pl.BlockSpec((2, A_BIAS_TILES, BLOCK, BLOCK), lambda b, p, i: (p, 0, 0, 0))],
        out_specs=pl.BlockSpec((tq, LANES), lambda b, p, i: (b * nq + i, p)),
        out_shape=jax.ShapeDtypeStruct((t, BRANCH_WIDTH), BF16),
        scratch_shapes=[pltpu.VMEM((LANES, seq), BF16),
                        pltpu.VMEM((2, tq, LANES), F32), pltpu.VMEM((2, tq, 2 * LANES), F32)],
        compiler_params=_cparams(("parallel", "parallel", "arbitrary")),
        name="a_attn",
    )(pn, _pair_masks(), pn, pp, sel, bias_tiles)


def _sortable_key(score):
    bits = pltpu.bitcast(score, jnp.int32)
    return bits ^ ((bits >> 31) & jnp.int32(0x7FFFFFFF))


SEL_ROWS = 512
CNT_ROWS = SEL_ROWS


def _a_select_kernel(qblk_ref, kblk_ref, ph_ref, pl_ref, pkh_ref, pkl_ref, sel_ref,
                     ikx_ref, keys_ref, iqt_ref, jcut_ref, *, seq, k_sel):
    qi = pl.program_id(1)
    n_sel = (qi * BLOCK + BLOCK + SEL_ROWS - 1) // SEL_ROWS
    n_cnt = n_sel

    @pl.when(qi == 0)
    def _():
        def prep(c, carry):
            off = pl.multiple_of(c * 512, 512)
            kv = kblk_ref[pl.ds(off, 512), :]
            hi = kv.astype(BF16)
            lo = (kv - hi.astype(F32)).astype(BF16)
            ikx_ref[pl.ds(off, 512), :] = (
                jnp.dot(hi, pkh_ref[...], preferred_element_type=F32)
                + jnp.dot(lo, pkl_ref[...], preferred_element_type=F32)).astype(BF16)
            return carry
        lax.fori_loop(0, seq // 512, prep, 0)

    qb = qblk_ref[...]
    iq = qb[:, :IDX_HEADS * IDX_DIM]
    iq_hi = iq.astype(BF16)
    iq_lo = (iq - iq_hi.astype(F32)).astype(BF16)
    iqx = (jnp.dot(iq_hi, ph_ref[...], preferred_element_type=F32)
           + jnp.dot(iq_lo, pl_ref[...], preferred_element_type=F32))
    for h in range(IDX_HEADS):
        iqt_ref[h // 2, :, (h % 2) * LANES:(h % 2 + 1) * LANES] = iqx[:, h * LANES:(h + 1) * LANES].T.astype(BF16)
    iw_t = qb[:, IDX_HEADS * IDX_DIM:IDX_HEADS * IDX_DIM + LANES].T
    iw_rows = [iw_t[IDX_DIM + h:IDX_DIM + h + 1, :] for h in range(IDX_HEADS)]
    q_pos = qi * BLOCK + lax.broadcasted_iota(jnp.int32, (SEL_ROWS, BLOCK), 1)
    k_row = lax.broadcasted_iota(jnp.int32, (SEL_ROWS, BLOCK), 0)

    def score_chunk(c, masked):
        off = pl.multiple_of(c * SEL_ROWS, SEL_ROWS)
        kx = ikx_ref[pl.ds(off, SEL_ROWS), :]
        sc = jnp.zeros((SEL_ROWS, BLOCK), F32)
        for hp in range(IDX_HEADS // 2):
            xx = jnp.dot(kx, iqt_ref[hp], preferred_element_type=F32)
            sc = (sc + jnp.maximum(xx[:, :LANES], 0.0) * iw_rows[2 * hp]
                  + jnp.maximum(xx[:, LANES:], 0.0) * iw_rows[2 * hp + 1])
        key = _sortable_key(sc + 0.0)
        if masked:
            key = jnp.where(off + k_row <= q_pos, key, INT_MIN)
        keys_ref[pl.ds(off, SEL_ROWS), :] = key

    def score_body(c, carry):
        score_chunk(c, False)
        return carry

    lax.fori_loop(0, n_sel - 1, score_body, 0)
    score_chunk(n_sel - 1, True)

    def count_rows(pred):
        def cnt_chunk(c, acc):
            off = pl.multiple_of(c * CNT_ROWS, CNT_ROWS)
            hit = jnp.where(pred(keys_ref[pl.ds(off, CNT_ROWS), :], off), 1, 0)
            return acc + jnp.sum(hit.reshape(CNT_ROWS // 8, 8, BLOCK), axis=0)
        acc = lax.fori_loop(0, n_cnt, cnt_chunk, jnp.zeros((8, BLOCK), jnp.int32))
        return jnp.sum(acc, axis=0, keepdims=True)

    def count_ge(cand):
        return count_rows(lambda keys, off: keys >= cand)

    unknown = jnp.int32(2 ** 30)

    def bisect(it, state):
        lo, cnt_lo = state
        cand = lo + lax.shift_left(jnp.int32(1), jnp.int32(31) - it)
        cnt = count_ge(cand)
        ok = cnt >= k_sel
        return jnp.where(ok, cand, lo), jnp.where(ok, cnt, cnt_lo)

    thr, cnt_thr = lax.fori_loop(0, 32, bisect, (jnp.full((1, BLOCK), INT_MIN, jnp.int32),
                                                 jnp.full((1, BLOCK), unknown, jnp.int32)))

    has_k = thr > INT_MIN
    thr = jnp.maximum(thr, INT_MIN + 1)

    jcut_ref[...] = jnp.full(jcut_ref.shape, 2 ** 31 - 1, jnp.int32)

    @pl.when(jnp.max(jnp.where(has_k, cnt_thr, 0)) > k_sel)
    def _():
        excess = jnp.where(has_k, count_ge(thr) - k_sel, 0)
        need = jnp.where(excess > 0, k_sel - count_ge(thr + 1), 1)
        rev_row = (seq - 1) - lax.broadcasted_iota(jnp.int32, (CNT_ROWS, BLOCK), 0)

        def tie_bisect(it, lo):
            cand = lo + lax.shift_left(jnp.int32(1), jnp.int32(seq.bit_length() - 2) - it)
            cnt = count_rows(lambda keys, off: (keys == thr) & (rev_row - off >= cand))
            return jnp.where(cnt >= need, cand, lo)

        rev = lax.fori_loop(0, seq.bit_length() - 1, tie_bisect, jnp.zeros((1, BLOCK), jnp.int32))
        jcut = jnp.where(excess > 0, (seq - 1) - rev, 2 ** 31 - 1)
        jcut_ref[...] = jnp.broadcast_to(jcut, jcut_ref.shape)

    jcut = jcut_ref[0:1, :]

    def emit(c, carry):
        off = pl.multiple_of(c * SEL_ROWS, SEL_ROWS)
        keys = keys_ref[pl.ds(off, SEL_ROWS), :]
        bar = jnp.where(off + k_row > jcut, thr + 1, thr)
        add = jnp.where(keys >= bar, 0.0, NEG)
        sel_ref[:, pl.ds(off, SEL_ROWS)] = jnp.concatenate(
            [add[j * BLOCK:(j + 1) * BLOCK].T for j in range(SEL_ROWS // BLOCK)], axis=1).astype(BF16)
        return carry

    lax.fori_loop(0, n_sel, emit, 0)

    def fill(c, carry):
        off = pl.multiple_of(c * SEL_ROWS, SEL_ROWS)
        sel_ref[:, pl.ds(off, SEL_ROWS)] = jnp.full((BLOCK, SEL_ROWS), NEG, BF16)
        return carry

    lax.fori_loop(n_sel, seq // SEL_ROWS, fill, 0)


def _a_select(pf, batch, seq):
    t = pf.shape[0]
    nq = seq // BLOCK
    k_sel = min(TOPK_MAX, seq // 4)
    nlane = IDX_HEADS * IDX_DIM
    r = np.arange(nlane)
    ph = np.zeros((nlane, IDX_HEADS * LANES), np.float32)
    plo = np.zeros((nlane, IDX_HEADS * LANES), np.float32)
    ph[r, (r // IDX_DIM) * LANES + r % IDX_DIM] = 1
    ph[r, (r // IDX_DIM) * LANES + 2 * IDX_DIM + r % IDX_DIM] = 1
    plo[r, (r // IDX_DIM) * LANES + IDX_DIM + r % IDX_DIM] = 1
    d = np.arange(IDX_DIM)
    pkh = np.zeros((LANES, LANES), np.float32)
    pkl = np.zeros((LANES, LANES), np.float32)
    pkh[d, d] = 1
    pkh[d, IDX_DIM + d] = 1
    pkl[d, 2 * IDX_DIM + d] = 1
    full = lambda shape: pl.BlockSpec(shape, lambda b, i: (0,) * len(shape))
    return pl.pallas_call(
        functools.partial(_a_select_kernel, seq=seq, k_sel=k_sel),
        grid=(batch, nq),
        in_specs=[pl.BlockSpec((BLOCK, 512), lambda b, i: (b * nq + i, 0)),
                  pl.BlockSpec((seq, LANES), lambda b, i: (b, nlane // LANES)),
                  full(ph.shape), full(plo.shape), full(pkh.shape), full(pkl.shape)],
        out_specs=pl.BlockSpec((BLOCK, seq), lambda b, i: (b * nq + i, 0)),
        out_shape=jax.ShapeDtypeStruct((t, seq), BF16),
        scratch_shapes=[pltpu.VMEM((seq, LANES), BF16), pltpu.VMEM((seq, LANES), jnp.int32),
                        pltpu.VMEM((IDX_HEADS // 2, LANES, 2 * LANES), BF16),
                        pltpu.VMEM((8, LANES), jnp.int32)],
        compiler_params=_cparams(("parallel", "arbitrary")),
        name="a_select",
    )(pf, pf, jnp.asarray(ph, BF16), jnp.asarray(plo, BF16), jnp.asarray(pkh, BF16),
      jnp.asarray(pkl, BF16))


BAND_RBLK = 4


def _band_kernel(*refs, has_sink, want_lse, kv_div, rblk):
    if has_sink:
        sink_ref, refs = refs[0], refs[1:]
    q_ref, hm_ref, kp_ref, kc_ref, vp_ref, vc_ref, bias_ref, o_ref = refs[:8]
    first = pl.program_id(1) == 0
    lane = lax.broadcasted_iota(jnp.int32, (BLOCK, LANES), 1)
    col = lax.broadcasted_iota(jnp.int32, (BLOCK, 2 * BLOCK), 1)
    no_prev = jnp.logical_and(first, col < BLOCK)
    for p in range(BRANCH_WIDTH // LANES):
        kcol = (p // kv_div) * LANES
        kk = jnp.concatenate([kp_ref[:, kcol:kcol + LANES], kc_ref[:, kcol:kcol + LANES]], axis=0)
        vv = jnp.concatenate([vp_ref[:, kcol:kcol + LANES], vc_ref[:, kcol:kcol + LANES]], axis=0)
        for i in range(rblk):
            rows = slice(i * BLOCK, (i + 1) * BLOCK)
            q = q_ref[rows, p * LANES:(p + 1) * LANES]
            kblk = kk[i * BLOCK:(i + 2) * BLOCK]
            vblk = vv[i * BLOCK:(i + 2) * BLOCK]
            outs, lses = [], []
            for e in range(2):
                s = lax.dot_general(q * hm_ref[e], kblk, _CONTRACT_LANES, preferred_element_type=F32)
                s = s + bias_ref[2 * p + e]
                if i == 0:
                    s = jnp.where(no_prev, NEG, s)
                m = jnp.max(s, axis=1, keepdims=True)
                if has_sink:
                    sk = sink_ref[2 * p + e]
                    m = jnp.maximum(m, sk)
                pr = jnp.exp(s - m)
                den = jnp.sum(pr, axis=1, keepdims=True)
                if has_sink:
                    den = den + jnp.exp(sk - m)
                outs.append(jnp.dot(pr.astype(BF16), vblk, preferred_element_type=F32) / den)
                lses.append(m + jnp.log(den))
            o_ref[rows, p * LANES:(p + 1) * LANES] = jnp.where(lane < HEAD_DIM, outs[0], outs[1]).astype(o_ref.dtype)
            if want_lse:
                refs[8][rows, p * LANES:(p + 1) * LANES] = jnp.where(lane < HEAD_DIM, lses[0], lses[1])


def _band_attn(q_arr, k_arr, v_arr, bias, nsub, sub_len, qcb, kcb, vcb, kv_w, sinks=None, want_lse=False):
    t = q_arr.shape[0]
    rblk = min(BAND_RBLK, sub_len // BLOCK)
    step_rows = rblk * BLOCK
    nbs = sub_len // step_rows
    kv_div = BRANCH_WIDTH // kv_w

    def cur(u, n):
        return u * nbs + n

    def prev(u, n):
        return jnp.where(n == 0, u * nbs * rblk, (u * nbs + n) * rblk - 1)

    in_specs = [
        pl.BlockSpec((step_rows, BRANCH_WIDTH), lambda u, n: (cur(u, n), qcb)),
        pl.BlockSpec((2, 1, LANES), lambda u, n: (0, 0, 0)),
        pl.BlockSpec((BLOCK, kv_w), lambda u, n: (prev(u, n), kcb)),
        pl.BlockSpec((step_rows, kv_w), lambda u, n: (cur(u, n), kcb)),
        pl.BlockSpec((BLOCK, kv_w), lambda u, n: (prev(u, n), vcb)),
        pl.BlockSpec((step_rows, kv_w), lambda u, n: (cur(u, n), vcb)),
        pl.BlockSpec(bias.shape, lambda u, n: (0, 0, 0)),
    ]
    args = [q_arr, _pair_masks(), k_arr, k_arr, v_arr, v_arr, bias]
    if sinks is not None:
        in_specs = [pl.BlockSpec(memory_space=pltpu.SMEM)] + in_specs
        args = [sinks] + args
    o_spec = pl.BlockSpec((step_rows, BRANCH_WIDTH), lambda u, n: (cur(u, n), 0))
    out_specs = [o_spec]
    out_shape = [jax.ShapeDtypeStruct((t, BRANCH_WIDTH), BF16)]
    if want_lse:
        out_specs.append(o_spec)
        out_shape.append(jax.ShapeDtypeStruct((t, BRANCH_WIDTH), F32))
    return pl.pallas_call(
        functools.partial(_band_kernel, has_sink=sinks is not None, want_lse=want_lse, kv_div=kv_div, rblk=rblk),
        grid=(nsub, nbs),
        in_specs=in_specs,
        out_specs=out_specs,
        out_shape=out_shape,
        compiler_params=_cparams(("parallel", "arbitrary")),
        name="band_attn",
    )(*args)


def _toeplitz(w, rows, cols):
    length = rows + cols - 1
    u = jnp.pad(w[..., ::-1], [(0, 0)] * (w.ndim - 1) + [(0, 1)])
    flat = jnp.broadcast_to(u[..., None, :], w.shape[:-1] + (rows, length + 1)).reshape(w.shape[:-1] + (-1,))
    skew = flat[..., :rows * length].reshape(w.shape[:-1] + (rows, length))
    return skew[..., rows - 1:rows - 1 + cols]


def _band_bias(table, step, max_dist):
    rel = BLOCK + np.arange(BLOCK)[:, None] - np.arange(2 * BLOCK)[None, :]
    rel_vec = np.arange(BLOCK - (2 * BLOCK - 1), 2 * BLOCK)
    bias = _toeplitz(table[_t5_bucket(jnp.asarray(rel_vec * step))].T.astype(F32), BLOCK, 2 * BLOCK)
    ok = (rel >= 0) & (rel <= max_dist)
    return jnp.where(jnp.asarray(ok)[None], bias, NEG)


def _merge_kernel(x_ref, ya_ref, yb0_ref, yb1_ref, yb2_ref, l0_ref, l1_ref, l2_ref, yc_ref, yd_ref,
                  za_ref, zb_ref, zc_ref, zd_ref, ga_ref, gb_ref, gc_ref, gd_ref, wb_ref, wo_ref, *rest):
    if len(rest) == 7:
        ng_ref, o_ref, xn_ref, y1_ref, y2_ref, s1_ref, s2_ref = rest
    else:
        ng_ref = xn_ref = None
        o_ref, y1_ref, y2_ref, s1_ref, s2_ref = rest
    ncol = BRANCH_WIDTH // LANES
    for src, dst in ((yb1_ref, y1_ref), (yb2_ref, y2_ref), (l1_ref, s1_ref), (l2_ref, s2_ref)):
        dil = src.shape[1]
        for r in range(dil):
            blk = src[0, r].astype(F32)
            for c in range(ncol):
                dst[c, pl.ds(r, src.shape[2], stride=dil), :] = blk[:, c * LANES:(c + 1) * LANES]
    wide = lambda ref: jnp.concatenate([ref[c] for c in range(ncol)], axis=1)
    l0, l1, l2 = l0_ref[0, 0], wide(s1_ref), wide(s2_ref)
    mx = jnp.maximum(jnp.maximum(l0, l1), l2)
    w0, w1, w2 = jnp.exp(l0 - mx), jnp.exp(l1 - mx), jnp.exp(l2 - mx)
    yb = (w0 * yb0_ref[0, 0].astype(F32) + w1 * wide(y1_ref) + w2 * wide(y2_ref)) / (w0 + w1 + w2)
    ys = (ya_ref[...].astype(F32), yb, yc_ref[...].astype(F32), yd_ref[...].astype(F32))
    zs = (za_ref, zb_ref, zc_ref, zd_ref)
    gs = (ga_ref, gb_ref, gc_ref, gd_ref)
    merged = jnp.zeros(o_ref.shape, F32)
    for n in range(N_BRANCH):
        z = zs[n][...].astype(F32)
        u = (ys[n] * (z * jax.nn.sigmoid(z))).astype(BF16)
        proj = jnp.dot(u, wb_ref[n], preferred_element_type=F32)
        merged = merged + jax.nn.sigmoid(gs[n][...].astype(F32)) * proj
    x_new = x_ref[...] + jnp.dot(merged.astype(BF16), wo_ref[...], preferred_element_type=F32)
    o_ref[...] = x_new
    if xn_ref is not None:
        xn_ref[...] = _rms_bf16(x_new, ng_ref[...])


def _merge(x2, ya, yb, lb, yc, yd, pp, wb, wo, batch, seq, next_gain=None, tm=512):
    t = x2.shape[0]
    bw = BRANCH_WIDTH
    nsb = seq // tm
    row = lambda w, c: pl.BlockSpec((tm, w), lambda i: (i, c))

    def sub(dil):
        return pl.BlockSpec((1, dil, tm // dil, bw), lambda i: (i // nsb, 0, i % nsb, 0))

    dils = [dil for _, dil in B_GROUPS]
    yb = [a.reshape(batch, dil, seq // dil, bw) for a, dil in zip(yb, dils)]
    lb = [a.reshape(batch, dil, seq // dil, bw) for a, dil in zip(lb, dils)]
    in_specs = ([row(D_MODEL, 0), row(bw, 0)] + [sub(dil) for dil in dils] * 2 + [row(bw, 0), row(bw, 0)]
                + [row(bw, PP_AZ // bw), row(bw, PP_BZ // bw), row(bw, PP_CZ // bw), row(bw, PP_DZ // bw)]
                + [row(D_MODEL, PP_G // D_MODEL + n) for n in range(N_BRANCH)]
                + [pl.BlockSpec((N_BRANCH, bw, D_MODEL), lambda i: (0, 0, 0)),
                   pl.BlockSpec((D_MODEL, D_MODEL), lambda i: (0, 0))])
    args = [x2, ya, yb[0], yb[1], yb[2], lb[0], lb[1], lb[2], yc, yd, pp, pp, pp, pp, pp, pp, pp, pp, wb, wo]
    out_specs = [row(D_MODEL, 0)]
    out_shape = [jax.ShapeDtypeStruct((t, D_MODEL), F32)]
    if next_gain is not None:
        in_specs.append(pl.BlockSpec((1, D_MODEL), lambda i: (0, 0)))
        args.append(next_gain.reshape(1, D_MODEL))
        out_specs.append(row(D_MODEL, 0))
        out_shape.append(jax.ShapeDtypeStruct((t, D_MODEL), BF16))
    return pl.pallas_call(
        _merge_kernel,
        grid=(t // tm,),
        in_specs=in_specs,
        out_specs=out_specs,
        out_shape=out_shape,
        scratch_shapes=[pltpu.VMEM((bw // LANES, tm, LANES), F32)] * 4,
        compiler_params=_cparams(("parallel",)),
        name="merge",
    )(*args)


def _layer_weights(w_in, qk_gain_a, qk_gain_b, qk_gain_d):
    def cols(start, width):
        return w_in[:, :, start:start + width]

    n_bq = len(B_GROUPS) * B_HEADS * HEAD_DIM
    dk = cols(_D0 + 512, 128)
    dv = cols(_D0 + 640, 128)
    dup = lambda a: jnp.concatenate([a[:, :, :64], a[:, :, :64], a[:, :, 64:], a[:, :, 64:]], axis=-1)
    w_pn = jnp.concatenate([cols(_A0, 512), cols(_A0 + 512, 512), cols(_D0, 512), dup(dk), dup(dv)],
                           axis=-1).astype(BF16)
    w_pp = jnp.concatenate([cols(_G0, N_BRANCH * D_MODEL), cols(_A0 + 1024, 512), cols(_A0 + 1536, 512),
                            cols(_B0 + 3 * n_bq, 512), cols(_C0 + 416, 512), cols(_D0 + 768, 512)],
                           axis=-1).astype(BF16)
    w_bg = [jnp.concatenate([cols(_B0 + g * 512, 512), cols(_B0 + n_bq + g * 512, 512),
                             cols(_B0 + 2 * n_bq + g * 512, 512)], axis=-1).astype(BF16)
            for g in range(len(B_GROUPS))]
    depth = w_in.shape[0]
    zeros = lambda w: jnp.zeros((depth, D_MODEL, w), w_in.dtype)
    w_pf = jnp.concatenate([cols(_A0 + 2048, 296), zeros(216), cols(_C0, 416), zeros(96)], axis=-1).astype(BF16)

    scale = HEAD_DIM ** -0.5
    tile = lambda g, reps: jnp.tile(g, (1, reps))
    hg = jnp.concatenate([tile(qk_gain_a[:, 0] * scale, 8), tile(qk_gain_a[:, 1], 8),
                          tile(qk_gain_d[:, 0] * scale, 8), tile(qk_gain_d[:, 1], 4),
                          jnp.ones((depth, 256), F32)], axis=-1)
    flag = jnp.concatenate([jnp.ones((PN_DV,), F32), jnp.zeros((PN_W - PN_DV,), F32)])
    hg_b = jnp.concatenate([tile(qk_gain_b[:, 0] * scale, 8), tile(qk_gain_b[:, 1], 8),
                            jnp.ones((depth, 512), F32)], axis=-1)
    flag_b = jnp.concatenate([jnp.ones((2 * BRANCH_WIDTH,), F32), jnp.zeros((BRANCH_WIDTH,), F32)])
    return w_pn, w_pp, w_bg, w_pf, hg, flag, hg_b, flag_b


def _c_weights(qk_gain_c, w_q_b, w_kv_b, seq):
    src_q, src_k, gain_idx, head, pe_src, rope_j, rope_half = _c_layout_tables()
    take = lambda w, src: jnp.where(jnp.asarray(src >= 0), jnp.take(w, jnp.asarray(np.maximum(src, 0)), axis=-1), 0.0)
    wq = take(w_q_b, src_q).astype(BF16)
    wk = take(w_kv_b, src_k).astype(BF16)
    v_src = np.array([h * (C_NOPE + C_V) + C_NOPE + c for h in range(C_HEADS) for c in range(C_V)])
    wv = jnp.take(w_kv_b, jnp.asarray(v_src), axis=-1).astype(BF16)
    ppe = np.zeros((LANES, C_QK_W), np.float32)
    ln = np.nonzero(pe_src >= 0)[0]
    ppe[pe_src[ln], ln] = 1
    hp = head[:C_PAIR]
    grp = ((hp[:, None] == hp[None, :]) & (hp[:, None] >= 0)).astype(np.float32)
    g_take = lambda g: jnp.where(jnp.asarray(gain_idx >= 0), jnp.take(g, jnp.asarray(np.maximum(gain_idx, 0)), axis=-1), 0.0)
    gq = g_take(qk_gain_c[:, 0]) * (C_NOPE + C_ROPE) ** -0.5
    gk = g_take(qk_gain_c[:, 1])
    freq = ROPE_THETA ** (-jnp.arange(C_ROPE_HALF, dtype=F32) / C_ROPE_HALF)
    ang = jnp.arange(seq).astype(F32)[:, None] * freq[None, :]
    cos_j, sin_j = jnp.cos(ang), jnp.sin(ang)
    place = np.zeros((C_ROPE_HALF, C_PAIR), np.float32)
    lanes = np.nonzero(rope_j[:C_PAIR] >= 0)[0]
    place[rope_j[lanes], lanes] = 1
    is_rope = jnp.asarray(rope_j[:C_PAIR] >= 0)
    hi = lax.Precision.HIGHEST
    cos = jnp.where(is_rope, jnp.dot(cos_j, jnp.asarray(place), precision=hi), 1.0)
    sin = jnp.dot(sin_j, jnp.asarray(place), precision=hi)
    s1 = jnp.where(jnp.asarray(rope_half[:C_PAIR] == 0), -sin, 0.0)
    s2 = jnp.where(jnp.asarray(rope_half[:C_PAIR] == 1), sin, 0.0)
    return wq, wk, wv, jnp.asarray(ppe, BF16), jnp.asarray(grp, BF16), gq, gk, cos, s1, s2


def _a_bias_tiles(table):
    dist_vec = np.arange(-(BLOCK - 1), A_BIAS_TILES * BLOCK)
    vec = table[_t5_bucket(jnp.asarray(dist_vec))].T.astype(F32)
    windows = jnp.stack([vec[:, d * BLOCK:d * BLOCK + 2 * BLOCK - 1] for d in range(A_BIAS_TILES)], axis=1)
    return _toeplitz(windows, BLOCK, BLOCK)


def kernel(x, norm_gain, w_in, qk_gain_a, qk_gain_b, qk_gain_c, qk_gain_d, c_q_gain, c_kv_gain,
           w_q_b, w_kv_b, sinks, rel_bias, w_branch, w_out):
    batch, seq, d_model = x.shape
    depth = w_in.shape[0]
    t = batch * seq
    x2 = x.reshape(t, d_model)

    w_pn, w_pp, w_bg, w_pf, hg, flag, hg_b, flag_b = _layer_weights(w_in, qk_gain_a, qk_gain_b, qk_gain_d)
    wq, wk, wv, ppe, grp, gq, gk, cos, s1, s2 = _c_weights(qk_gain_c, w_q_b, w_kv_b, seq)
    wb = w_branch.astype(BF16)
    wo = w_out.astype(BF16)

    bias_a = _a_bias_tiles(rel_bias[:, :A_HEADS])
    bias_b = [_band_bias(rel_bias[:, A_HEADS + g * B_HEADS:A_HEADS + (g + 1) * B_HEADS], dil, window // dil)
              for g, (window, dil) in enumerate(B_GROUPS)]
    bias_d = _band_bias(rel_bias[:, N_BIAS_HEADS - D_HEADS:], 1, D_WINDOW - 1)

    xn = _norm(x2, norm_gain[0])
    for l in range(depth):
        pn = _proj(xn, w_pn[l], BF16, head_gain=hg[l], flag=flag)
        pp = _proj(xn, w_pp[l], BF16)
        pf = _proj(xn, w_pf[l], F32)

        sel = _a_select(pf, batch, seq)
        ya = _a_attn(pn, pp, sel, bias_a, batch, seq)

        yb, lb = [], []
        for g, (window, dil) in enumerate(B_GROUPS):
            bg = _proj(xn, w_bg[g][l], BF16, head_gain=hg_b[l], flag=flag_b, dil=dil, seq=seq)
            o, lse = _band_attn(bg, bg, bg, bias_b[g], batch * dil, seq // dil, 0, 1, 2, BRANCH_WIDTH,
                                want_lse=True)
            yb.append(o)
            lb.append(lse)

        qc, kc, vc = _c_prep(pf, c_q_gain[l].reshape(1, -1), c_kv_gain[l].reshape(1, -1), wq[l], wk[l], wv[l],
                             ppe, grp, gq[l].reshape(1, -1), gk[l].reshape(1, -1), cos, s1, s2, seq)
        yc = _c_attn(qc, kc, vc, batch, seq)

        (yd,) = _band_attn(pn, pn, pn, bias_d, batch, seq, PN_DQ // BRANCH_WIDTH, PN_DK // 256, PN_DV // 256,
                           256, sinks=sinks[l])

        if l + 1 < depth:
            x2, xn = _merge(x2, ya, yb, lb, yc, yd, pp, wb[l], wo[l], batch, seq, next_gain=norm_gain[l + 1])
        else:
            (x2,) = _merge(x2, ya, yb, lb, yc, yd, pp, wb[l], wo[l], batch, seq)
    return x2.reshape(batch, seq, d_model)
```

```python
import functools
import math

import numpy as np
import jax
import jax.numpy as jnp
from jax import lax
from jax.experimental import pallas as pl
from jax.experimental.pallas import tpu as pltpu

F32 = jnp.float32
BF16 = jnp.bfloat16

D_MODEL = 1024
BLOCK = 128
HEAD_DIM = 64
N_BRANCH = 4
BRANCH_WIDTH = 512
EPS = 1e-6
A_HEADS = 8
IDX_HEADS = 8
IDX_DIM = 32
TOPK_MAX = 256
B_GROUPS = ((128, 1), (512, 4), (2048, 16))
B_HEADS = 8
C_HEADS = 8
C_NOPE = 64
C_ROPE = 32
C_V = 64
C_Q_LORA = 256
C_KV_LORA = 128
ROPE_THETA = 10000.0
D_HEADS = 8
D_KV_HEADS = 2
D_WINDOW = 128
NUM_BUCKETS = 32
MAX_DISTANCE = 2048
N_BIAS_HEADS = A_HEADS + len(B_GROUPS) * B_HEADS + D_HEADS

LANES = 128
NEG = -1e30
INT_MIN = -(2 ** 31)

_A0 = 0
_B0 = 2344
_C0 = 7464
_D0 = 8392
_G0 = 9672

PN_AQ, PN_AK, PN_DQ, PN_DK, PN_DV, PN_W = 0, 512, 1024, 1536, 1792, 2048
PP_G, PP_AV, PP_AZ, PP_BZ, PP_CZ, PP_DZ, PP_W = 0, 4096, 4608, 5120, 5632, 6144, 6656
PF_W = 1024
BG_W = 3 * BRANCH_WIDTH

A_BIAS_TILES = MAX_DISTANCE // BLOCK + 2


def _cparams(sem, vmem_mb=48):
    return pltpu.CompilerParams(dimension_semantics=sem, vmem_limit_bytes=vmem_mb * 1024 * 1024)


def _t5_bucket(dist):
    max_exact = NUM_BUCKETS // 2
    d = jnp.maximum(dist, 0)
    logd = jnp.log(jnp.maximum(d, 1).astype(F32) / max_exact)
    large = max_exact + (logd / math.log(MAX_DISTANCE / max_exact) * (NUM_BUCKETS - max_exact)).astype(jnp.int32)
    return jnp.where(d < max_exact, d, jnp.minimum(large, NUM_BUCKETS - 1))


def _rms_bf16(x, gain):
    return (x * lax.rsqrt(jnp.mean(x * x, axis=-1, keepdims=True) + EPS) * gain).astype(BF16)


def _norm_kernel(x_ref, g_ref, o_ref):
    o_ref[...] = _rms_bf16(x_ref[...], g_ref[...])


def _norm(x2, gain, tm=1024):
    t, d = x2.shape
    return pl.pallas_call(
        _norm_kernel,
        grid=(t // tm,),
        in_specs=[pl.BlockSpec((tm, d), lambda i: (i, 0)), pl.BlockSpec((1, d), lambda i: (0, 0))],
        out_specs=pl.BlockSpec((tm, d), lambda i: (i, 0)),
        out_shape=jax.ShapeDtypeStruct((t, d), BF16),
        compiler_params=_cparams(("parallel",)),
        name="norm",
    )(x2, gain.reshape(1, d))


def _proj_kernel(xn_ref, w_ref, *rest, norm, dil):
    rest = list(rest)
    res_ref = rest.pop(-1) if dil is not None and dil > 1 else None
    if norm:
        hg_ref, flag_ref, bd_ref, o_ref = rest
    else:
        (o_ref,) = rest

    def emit(h):
        if dil is None:
            o_ref[...] = h.astype(o_ref.dtype)
        elif dil == 1:
            o_ref[0, 0] = h.astype(o_ref.dtype)
        else:
            sub = h.shape[0] // dil
            for c in range(h.shape[1] // LANES):
                res_ref[c] = h[:, c * LANES:(c + 1) * LANES]
            for r in range(dil):
                o_ref[0, r] = jnp.concatenate(
                    [res_ref[c, pl.ds(r, sub, stride=dil), :] for c in range(h.shape[1] // LANES)],
                    axis=1).astype(o_ref.dtype)

    h = jnp.dot(xn_ref[...], w_ref[...], preferred_element_type=F32)
    if norm:
        ss = jnp.dot((h * h).astype(BF16), bd_ref[...], preferred_element_type=F32)
        scale = lax.rsqrt(ss * (1.0 / HEAD_DIM) + EPS) * hg_ref[...]
        h = h * jnp.where(flag_ref[...] > 0, scale, 1.0)
    emit(h)


def _proj(xn, w, out_dtype, head_gain=None, flag=None, dil=None, seq=None, tm=1024, tn=512):
    t, d = xn.shape
    n = w.shape[1]
    norm = head_gain is not None
    scratch = []
    if dil is None:
        out_spec = pl.BlockSpec((tm, tn), lambda i, j: (i, j))
        out_shape = jax.ShapeDtypeStruct((t, n), out_dtype)
        kdil = None
    else:
        nsb = seq // tm
        out_spec = pl.BlockSpec((1, dil, tm // dil, tn), lambda i, j: (i // nsb, 0, i % nsb, j))
        out_shape = jax.ShapeDtypeStruct((t // seq, dil, seq // dil, n), out_dtype)
        kdil = dil
        if dil > 1:
            scratch.append(pltpu.VMEM((tn // LANES, tm, LANES), F32))
    in_specs = [pl.BlockSpec((tm, d), lambda i, j: (i, 0)),
                pl.BlockSpec((d, tn), lambda i, j: (0, j))]
    args = [xn, w]
    if norm:
        lane = np.arange(tn)
        bd = jnp.asarray((lane[:, None] // HEAD_DIM == lane[None, :] // HEAD_DIM), BF16)
        in_specs += [pl.BlockSpec((1, tn), lambda i, j: (0, j)),
                     pl.BlockSpec((1, tn), lambda i, j: (0, j)),
                     pl.BlockSpec((tn, tn), lambda i, j: (0, 0))]
        args += [head_gain.reshape(1, n), flag.reshape(1, n), bd]
    out = pl.pallas_call(
        functools.partial(_proj_kernel, norm=norm, dil=kdil),
        grid=(t // tm, n // tn),
        in_specs=in_specs,
        out_specs=out_spec,
        out_shape=out_shape,
        scratch_shapes=scratch,
        compiler_params=_cparams(("parallel", "parallel")),
        name=("proj_norm" if norm else "proj") + ("" if dil is None else "_dil%d" % dil),
    )(*args)
    return out.reshape(t, n)


C_PAIR = 256
C_QK_W = (C_HEADS // 2) * C_PAIR
C_ROPE_HALF = C_ROPE // 2


def _c_lane(h, c):
    base = (h // 2) * C_PAIR
    e = h % 2
    if c < C_NOPE:
        return base + e * C_NOPE + c
    return base + 2 * C_NOPE + e * C_ROPE + (c - C_NOPE)


def _c_layout_tables():
    src_q = np.full(C_QK_W, -1, np.int64)
    src_k = np.full(C_QK_W, -1, np.int64)
    gain_idx = np.full(C_QK_W, -1, np.int64)
    head = np.full(C_QK_W, -1, np.int64)
    pe_src = np.full(C_QK_W, -1, np.int64)
    rope_j = np.full(C_QK_W, -1, np.int64)
    rope_half = np.zeros(C_QK_W, np.int64)
    for h in range(C_HEADS):
        for c in range(C_NOPE + C_ROPE):
            ln = _c_lane(h, c)
            src_q[ln] = h * (C_NOPE + C_ROPE) + c
            gain_idx[ln] = c
            head[ln] = h
            if c < C_NOPE:
                src_k[ln] = h * (C_NOPE + C_V) + c
            else:
                r = c - C_NOPE
                pe_src[ln] = r
                rope_j[ln] = r % C_ROPE_HALF
                rope_half[ln] = r // C_ROPE_HALF
    return src_q, src_k, gain_idx, head, pe_src, rope_j, rope_half


def _c_prep_kernel(pf_ref, cqg_ref, ckvg_ref, wq_ref, wk_ref, wv_ref, ppe_ref, grp_ref,
                   gq_ref, gk_ref, cos_ref, s1_ref, s2_ref, q_ref, k_ref, v_ref):
    blk = pf_ref[...]
    cq = blk[:, :C_Q_LORA]
    ckv = blk[:, C_Q_LORA:C_Q_LORA + C_KV_LORA]
    pe = blk[:, C_Q_LORA + C_KV_LORA:]

    def rms(v, g):
        return v * lax.rsqrt(jnp.mean(v * v, axis=-1, keepdims=True) + EPS) * g

    def head_norm_rope(raw, gain):
        sq = (raw * raw).astype(BF16)
        ss = jnp.concatenate([jnp.dot(sq[:, p * C_PAIR:(p + 1) * C_PAIR], grp_ref[...], preferred_element_type=F32)
                              for p in range(C_QK_W // C_PAIR)], axis=1)
        y = raw * lax.rsqrt(ss * (1.0 / (C_NOPE + C_ROPE)) + EPS) * gain
        up = pltpu.roll(y, C_QK_W - C_ROPE_HALF, 1)
        dn = pltpu.roll(y, C_ROPE_HALF, 1)
        wide = lambda ref: jnp.concatenate([ref[...]] * (C_QK_W // C_PAIR), axis=1)
        return y * wide(cos_ref) + up * wide(s1_ref) + dn * wide(s2_ref)

    cqn = rms(cq, cqg_ref[...]).astype(BF16)
    q_raw = jnp.dot(cqn, wq_ref[...], preferred_element_type=F32)
    q_ref[...] = head_norm_rope(q_raw, gq_ref[...]).astype(BF16)

    ckvn = rms(ckv, ckvg_ref[...]).astype(BF16)
    pe_hi = pe.astype(BF16)
    pe_lo = (pe - pe_hi.astype(F32)).astype(BF16)
    k_raw = (jnp.dot(ckvn, wk_ref[...], preferred_element_type=F32)
             + jnp.dot(pe_hi, ppe_ref[...], preferred_element_type=F32)
             + jnp.dot(pe_lo, ppe_ref[...], preferred_element_type=F32))
    k_ref[...] = head_norm_rope(k_raw, gk_ref[...]).astype(BF16)
    v_ref[...] = jnp.dot(ckvn, wv_ref[...], preferred_element_type=F32).astype(BF16)


def _c_prep(pf, cqg, ckvg, wq, wk, wv, ppe, grp, gq, gk, cos, s1, s2, seq, tm=512):
    t = pf.shape[0]
    nsb = seq // tm
    full = lambda shape: pl.BlockSpec(shape, lambda i: (0,) * len(shape))
    tab = pl.BlockSpec((tm, C_PAIR), lambda i: (i % nsb, 0))
    return pl.pallas_call(
        _c_prep_kernel,
        grid=(t // tm,),
        in_specs=[pl.BlockSpec((tm, 512), lambda i: (i, 1)),
                  full((1, C_Q_LORA)), full((1, C_KV_LORA)),
                  full((C_Q_LORA, C_QK_W)), full((C_KV_LORA, C_QK_W)), full((C_KV_LORA, BRANCH_WIDTH)),
                  full((LANES, C_QK_W)), full((C_PAIR, C_PAIR)),
                  full((1, C_QK_W)), full((1, C_QK_W)), tab, tab, tab],
        out_specs=[pl.BlockSpec((tm, C_QK_W), lambda i: (i, 0)),
                   pl.BlockSpec((tm, C_QK_W), lambda i: (i, 0)),
                   pl.BlockSpec((tm, BRANCH_WIDTH), lambda i: (i, 0))],
        out_shape=[jax.ShapeDtypeStruct((t, C_QK_W), BF16),
                   jax.ShapeDtypeStruct((t, C_QK_W), BF16),
                   jax.ShapeDtypeStruct((t, BRANCH_WIDTH), BF16)],
        compiler_params=_cparams(("parallel",)),
        name="c_prep",
    )(pf, cqg, ckvg, wq, wk, wv, ppe, grp, gq, gk, cos, s1, s2)


def _flash_update(e, s, v_ones, m_ref, acc_ref, rows=slice(None)):
    m_prev = m_ref[e, rows]
    m_new = jnp.maximum(m_prev, jnp.max(s, axis=1, keepdims=True))
    alpha = jnp.exp(m_prev - m_new)
    p = jnp.exp(s - jnp.concatenate([m_new] * (s.shape[1] // LANES), axis=1))
    acc_ref[e, rows] = (jnp.concatenate([alpha, alpha], axis=1) * acc_ref[e, rows]
                        + jnp.dot(p.astype(BF16), v_ones, preferred_element_type=F32))
    m_ref[e, rows] = m_new


def _flash_init(m_ref, acc_ref):
    m_ref[...] = jnp.full(m_ref.shape, NEG, F32)
    acc_ref[...] = jnp.zeros(acc_ref.shape, F32)


def _flash_finish(o_ref, acc_ref):
    lane = lax.broadcasted_iota(jnp.int32, o_ref.shape, 1)
    o0 = acc_ref[0, :, :LANES] / acc_ref[0, :, LANES:]
    o1 = acc_ref[1, :, :LANES] / acc_ref[1, :, LANES:]
    o_ref[...] = jnp.where(lane < HEAD_DIM, o0, o1).astype(o_ref.dtype)


def _with_ones(v_chunk):
    return jnp.concatenate([v_chunk, jnp.ones(v_chunk.shape, v_chunk.dtype)], axis=1)


_CONTRACT_LANES = (((1,), (1,)), ((), ()))


def _head_masks(width, ranges0, ranges1):
    m = np.zeros((2, 1, width), np.float32)
    for e, ranges in enumerate((ranges0, ranges1)):
        for lo, hi in ranges:
            m[e, 0, lo:hi] = 1
    return jnp.asarray(m, BF16)


def _pair_masks():
    return _head_masks(LANES, [(0, HEAD_DIM)], [(HEAD_DIM, LANES)])


def _transpose_keys(k_ref, kt_ref, rows=512):
    def body(c, carry):
        off = pl.multiple_of(c * rows, rows)
        kt_ref[:, pl.ds(off, rows)] = k_ref[pl.ds(off, rows), :].astype(F32).T.astype(kt_ref.dtype)
        return carry
    lax.fori_loop(0, k_ref.shape[0] // rows, body, 0)


def _chunk_loop(chunk, n, unroll, carry=None):
    start = 0
    width = unroll
    while width >= 1:
        def body(g, carry, width=width, start=start):
            for j in range(width):
                carry = chunk(start + g * width + j, carry)
            return carry
        groups = (n - start) // width
        carry = lax.fori_loop(0, groups, body, carry)
        start = start + groups * width
        width //= 2
    return carry


def _c_attn_kernel(q_ref, hm_ref, k_ref, v_ref, o_ref, kt_ref, m_ref, acc_ref, *, tq, tk, unroll):
    qi = pl.program_id(2)

    @pl.when(qi == 0)
    def _():
        _transpose_keys(k_ref, kt_ref)

    q = q_ref[...]
    qs = (q * hm_ref[0], q * hm_ref[1])
    _flash_init(m_ref, acc_ref)
    n_full = (qi * tq) // tk

    def chunk(c, row0=0, diagonal=False):
        off = pl.multiple_of(c * tk, tk)
        kc = kt_ref[:, pl.ds(off, tk)]
        vc = _with_ones(v_ref[pl.ds(off, tk), :])
        for e in range(2):
            if not diagonal:
                rows = slice(row0, tq)
                _flash_update(e, jnp.dot(qs[e][rows], kc, preferred_element_type=F32), vc, m_ref, acc_ref, rows)
                continue
            rows = slice(row0, row0 + tk)
            s = jnp.dot(qs[e][rows], kc, preferred_element_type=F32)
            s = jnp.where(lax.broadcasted_iota(jnp.int32, s.shape, 1) <= lax.broadcasted_iota(jnp.int32, s.shape, 0),
                          s, NEG)
            _flash_update(e, s, vc, m_ref, acc_ref, rows)
            if row0 + tk < tq:
                rows = slice(row0 + tk, tq)
                _flash_update(e, jnp.dot(qs[e][rows], kc, preferred_element_type=F32), vc, m_ref, acc_ref, rows)

    _chunk_loop(lambda c, _: chunk(c), n_full, unroll)
    for j in range(tq // tk):
        chunk(n_full + j, row0=j * tk, diagonal=True)
    _flash_finish(o_ref, acc_ref)


def _c_attn(qc, kc, vc, batch, seq, tq=1024, tk=512, unroll=4):
    t = qc.shape[0]
    nq = seq // tq
    npair = C_HEADS // 2
    return pl.pallas_call(
        functools.partial(_c_attn_kernel, tq=tq, tk=tk, unroll=unroll),
        grid=(batch, npair, nq),
        in_specs=[pl.BlockSpec((tq, C_PAIR), lambda b, p, i: (b * nq + i, p)),
                  pl.BlockSpec((2, 1, C_PAIR), lambda b, p, i: (0, 0, 0)),
                  pl.BlockSpec((seq, C_PAIR), lambda b, p, i: (b, p)),
                  pl.BlockSpec((seq, LANES), lambda b, p, i: (b, p))],
        out_specs=pl.BlockSpec((tq, LANES), lambda b, p, i: (b * nq + i, p)),
        out_shape=jax.ShapeDtypeStruct((t, BRANCH_WIDTH), BF16),
        scratch_shapes=[pltpu.VMEM((C_PAIR, seq), BF16),
                        pltpu.VMEM((2, tq, LANES), F32), pltpu.VMEM((2, tq, 2 * LANES), F32)],
        compiler_params=_cparams(("parallel", "parallel", "arbitrary")),
        name="c_attn",
    )(qc, _head_masks(C_PAIR, [(0, C_NOPE), (2 * C_NOPE, 2 * C_NOPE + C_ROPE)],
                      [(C_NOPE, 2 * C_NOPE), (2 * C_NOPE + C_ROPE, 2 * C_NOPE + 2 * C_ROPE)]), kc, vc)


def _a_attn_kernel(q_ref, hm_ref, k_ref, v_ref, sel_ref, bias_ref, o_ref, kt_ref, m_ref, acc_ref,
                   *, tq, tk, unroll):
    qi = pl.program_id(2)

    @pl.when(qi == 0)
    def _():
        _transpose_keys(k_ref, kt_ref)

    q = q_ref[...]
    qs = (q * hm_ref[0], q * hm_ref[1])
    _flash_init(m_ref, acc_ref)
    n_chunks = ((qi + 1) * tq + tk - 1) // tk

    def chunk(c):
        off = pl.multiple_of(c * tk, tk)
        kc = kt_ref[:, pl.ds(off, tk)]
        vc = _with_ones(v_ref[pl.ds(off, tk), :])
        selmask = sel_ref[:, pl.ds(off, tk)].astype(F32)
        for e in range(2):
            s = jnp.dot(qs[e], kc, preferred_element_type=F32)
            rows = []
            for i in range(tq // BLOCK):
                tiles = []
                for j in range(tk // BLOCK):
                    d = (qi * (tq // BLOCK) + i) - (c * (tk // BLOCK) + j)
                    d = jnp.clip(d, 0, A_BIAS_TILES - 1)
                    tiles.append(bias_ref[e, d])
                rows.append(jnp.concatenate(tiles, axis=1))
            s = s + jnp.concatenate(rows, axis=0) + selmask
            _flash_update(e, s, vc, m_ref, acc_ref)

    _chunk_loop(lambda c, _: chunk(c), n_chunks, unroll)
    _flash_finish(o_ref, acc_ref)


def _a_attn(pn, pp, sel, bias_tiles, batch, seq, tq=512, tk=512, unroll=4):
    t = pn.shape[0]
    nq = seq // tq
    npair = A_HEADS // 2
    qcol, kcol, vcol = PN_AQ // LANES, PN_AK // LANES, PP_AV // LANES
    return pl.pallas_call(
        functools.partial(_a_attn_kernel, tq=tq, tk=tk, unroll=unroll),
        grid=(batch, npair, nq),
        in_specs=[pl.BlockSpec((tq, LANES), lambda b, p, i: (b * nq + i, qcol + p)),
                  pl.BlockSpec((2, 1, LANES), lambda b, p, i: (0, 0, 0)),
                  pl.BlockSpec((seq, LANES), lambda b, p, i: (b, kcol + p)),
                  pl.BlockSpec((seq, LANES), lambda b, p, i: (b, vcol + p)),
                  pl.BlockSpec((tq, seq), lambda b, p, i: (b * nq + i, 0)),
                  pl.BlockSpec((2, A_BIAS_TILES, BLOCK, BLOCK), lambda b, p, i: (p, 0, 0, 0))],
        out_specs=pl.BlockSpec((tq, LANES), lambda b, p, i: (b * nq + i, p)),
        out_shape=jax.ShapeDtypeStruct((t, BRANCH_WIDTH), BF16),
        scratch_shapes=[pltpu.VMEM((LANES, seq), BF16),
                        pltpu.VMEM((2, tq, LANES), F32), pltpu.VMEM((2, tq, 2 * LANES), F32)],
        compiler_params=_cparams(("parallel", "parallel", "arbitrary")),
        name="a_attn",
    )(pn, _pair_masks(), pn, pp, sel, bias_tiles)


def _sortable_key(score):
    bits = pltpu.bitcast(score, jnp.int32)
    return bits ^ ((bits >> 31) & jnp.int32(0x7FFFFFFF))


SEL_ROWS = 512
CNT_ROWS = SEL_ROWS
SCORE_UNROLL = 2
COUNT_UNROLL = 1


def _a_select_kernel(qblk_ref, kblk_ref, ph_ref, pl_ref, pkh_ref, pkl_ref, sel_ref,
                     ikx_ref, keys_ref, iqt_ref, jcut_ref, *, seq, k_sel):
    qi = pl.program_id(1)
    n_sel = (qi * BLOCK + BLOCK + SEL_ROWS - 1) // SEL_ROWS
    n_cnt = n_sel

    @pl.when(qi == 0)
    def _():
        def prep(c, carry):
            off = pl.multiple_of(c * 512, 512)
            kv = kblk_ref[pl.ds(off, 512), :]
            hi = kv.astype(BF16)
            lo = (kv - hi.astype(F32)).astype(BF16)
            ikx_ref[pl.ds(off, 512), :] = (
                jnp.dot(hi, pkh_ref[...], preferred_element_type=F32)
                + jnp.dot(lo, pkl_ref[...], preferred_element_type=F32)).astype(BF16)
            return carry
        lax.fori_loop(0, seq // 512, prep, 0)

    qb = qblk_ref[...]
    iq = qb[:, :IDX_HEADS * IDX_DIM]
    iq_hi = iq.astype(BF16)
    iq_lo = (iq - iq_hi.astype(F32)).astype(BF16)
    iqx = (jnp.dot(iq_hi, ph_ref[...], preferred_element_type=F32)
           + jnp.dot(iq_lo, pl_ref[...], preferred_element_type=F32))
    for h in range(IDX_HEADS):
        iqt_ref[h // 2, :, (h % 2) * LANES:(h % 2 + 1) * LANES] = iqx[:, h * LANES:(h + 1) * LANES].T.astype(BF16)
    iw_t = qb[:, IDX_HEADS * IDX_DIM:IDX_HEADS * IDX_DIM + LANES].T
    iw_rows = [iw_t[IDX_DIM + h:IDX_DIM + h + 1, :] for h in range(IDX_HEADS)]
    q_pos = qi * BLOCK + lax.broadcasted_iota(jnp.int32, (SEL_ROWS, BLOCK), 1)
    k_row = lax.broadcasted_iota(jnp.int32, (SEL_ROWS, BLOCK), 0)

    def score_chunk(c, masked):
        off = pl.multiple_of(c * SEL_ROWS, SEL_ROWS)
        kx = ikx_ref[pl.ds(off, SEL_ROWS), :]
        sc = jnp.zeros((SEL_ROWS, BLOCK), F32)
        for hp in range(IDX_HEADS // 2):
            xx = jnp.dot(kx, iqt_ref[hp], preferred_element_type=F32)
            sc = (sc + jnp.maximum(xx[:, :LANES], 0.0) * iw_rows[2 * hp]
                  + jnp.maximum(xx[:, LANES:], 0.0) * iw_rows[2 * hp + 1])
        key = _sortable_key(sc + 0.0)
        if masked:
            key = jnp.where(off + k_row <= q_pos, key, INT_MIN)
        keys_ref[pl.ds(off, SEL_ROWS), :] = key

    _chunk_loop(lambda c, _: score_chunk(c, False), n_sel - 1, SCORE_UNROLL)
    score_chunk(n_sel - 1, True)

    def count_rows(pred):
        def cnt_chunk(c, acc):
            off = pl.multiple_of(c * CNT_ROWS, CNT_ROWS)
            hit = jnp.where(pred(keys_ref[pl.ds(off, CNT_ROWS), :], off), 1, 0)
            return acc + jnp.sum(hit.reshape(CNT_ROWS // 8, 8, BLOCK), axis=0)
        acc = _chunk_loop(cnt_chunk, n_cnt, COUNT_UNROLL, jnp.zeros((8, BLOCK), jnp.int32))
        return jnp.sum(acc, axis=0, keepdims=True)

    def count_ge(cand):
        return count_rows(lambda keys, off: keys >= cand)

    unknown = jnp.int32(2 ** 30)

    def bisect(it, state):
        lo, cnt_lo = state
        cand = lo + lax.shift_left(jnp.int32(1), jnp.int32(31) - it)
        cnt = count_ge(cand)
        ok = cnt >= k_sel
        return jnp.where(ok, cand, lo), jnp.where(ok, cnt, cnt_lo)

    thr, cnt_thr = lax.fori_loop(0, 32, bisect, (jnp.full((1, BLOCK), INT_MIN, jnp.int32),
                                                 jnp.full((1, BLOCK), unknown, jnp.int32)))

    has_k = thr > INT_MIN
    thr = jnp.maximum(thr, INT_MIN + 1)

    jcut_ref[...] = jnp.full(jcut_ref.shape, 2 ** 31 - 1, jnp.int32)

    @pl.when(jnp.max(jnp.where(has_k, cnt_thr, 0)) > k_sel)
    def _():
        excess = jnp.where(has_k, count_ge(thr) - k_sel, 0)
        need = jnp.where(excess > 0, k_sel - count_ge(thr + 1), 1)
        rev_row = (seq - 1) - lax.broadcasted_iota(jnp.int32, (CNT_ROWS, BLOCK), 0)

        def tie_bisect(it, lo):
            cand = lo + lax.shift_left(jnp.int32(1), jnp.int32(seq.bit_length() - 2) - it)
            cnt = count_rows(lambda keys, off: (keys == thr) & (rev_row - off >= cand))
            return jnp.where(cnt >= need, cand, lo)

        rev = lax.fori_loop(0, seq.bit_length() - 1, tie_bisect, jnp.zeros((1, BLOCK), jnp.int32))
        jcut = jnp.where(excess > 0, (seq - 1) - rev, 2 ** 31 - 1)
        jcut_ref[...] = jnp.broadcast_to(jcut, jcut_ref.shape)

    jcut = jcut_ref[0:1, :]

    def emit(c, carry):
        off = pl.multiple_of(c * SEL_ROWS, SEL_ROWS)
        keys = keys_ref[pl.ds(off, SEL_ROWS), :]
        bar = jnp.where(off + k_row > jcut, thr + 1, thr)
        add = jnp.where(keys >= bar, 0.0, NEG)
        sel_ref[:, pl.ds(off, SEL_ROWS)] = jnp.concatenate(
            [add[j * BLOCK:(j + 1) * BLOCK].T for j in range(SEL_ROWS // BLOCK)], axis=1).astype(BF16)
        return carry

    _chunk_loop(emit, n_sel, SCORE_UNROLL)

    def fill(c, carry):
        off = pl.multiple_of(c * SEL_ROWS, SEL_ROWS)
        sel_ref[:, pl.ds(off, SEL_ROWS)] = jnp.full((BLOCK, SEL_ROWS), NEG, BF16)
        return carry

    lax.fori_loop(n_sel, seq // SEL_ROWS, fill, 0)


def _a_select(pf, batch, seq):
    t = pf.shape[0]
    nq = seq // BLOCK
    k_sel = min(TOPK_MAX, seq // 4)
    nlane = IDX_HEADS * IDX_DIM
    r = np.arange(nlane)
    ph = np.zeros((nlane, IDX_HEADS * LANES), np.float32)
    plo = np.zeros((nlane, IDX_HEADS * LANES), np.float32)
    ph[r, (r // IDX_DIM) * LANES + r % IDX_DIM] = 1
    ph[r, (r // IDX_DIM) * LANES + 2 * IDX_DIM + r % IDX_DIM] = 1
    plo[r, (r // IDX_DIM) * LANES + IDX_DIM + r % IDX_DIM] = 1
    d = np.arange(IDX_DIM)
    pkh = np.zeros((LANES, LANES), np.float32)
    pkl = np.zeros((LANES, LANES), np.float32)
    pkh[d, d] = 1
    pkh[d, IDX_DIM + d] = 1
    pkl[d, 2 * IDX_DIM + d] = 1
    full = lambda shape: pl.BlockSpec(shape, lambda b, i: (0,) * len(shape))
    return pl.pallas_call(
        functools.partial(_a_select_kernel, seq=seq, k_sel=k_sel),
        grid=(batch, nq),
        in_specs=[pl.BlockSpec((BLOCK, 512), lambda b, i: (b * nq + i, 0)),
                  pl.BlockSpec((seq, LANES), lambda b, i: (b, nlane // LANES)),
                  full(ph.shape), full(plo.shape), full(pkh.shape), full(pkl.shape)],
        out_specs=pl.BlockSpec((BLOCK, seq), lambda b, i: (b * nq + i, 0)),
        out_shape=jax.ShapeDtypeStruct((t, seq), BF16),
        scratch_shapes=[pltpu.VMEM((seq, LANES), BF16), pltpu.VMEM((seq, LANES), jnp.int32),
                        pltpu.VMEM((IDX_HEADS // 2, LANES, 2 * LANES), BF16),
                        pltpu.VMEM((8, LANES), jnp.int32)],
        compiler_params=_cparams(("parallel", "arbitrary")),
        name="a_select",
    )(pf, pf, jnp.asarray(ph, BF16), jnp.asarray(plo, BF16), jnp.asarray(pkh, BF16),
      jnp.asarray(pkl, BF16))


BAND_RBLK = 4


def _band_kernel(*refs, has_sink, want_lse, kv_div, rblk):
    if has_sink:
        sink_ref, refs = refs[0], refs[1:]
    q_ref, hm_ref, kp_ref, kc_ref, vp_ref, vc_ref, bias_ref, o_ref = refs[:8]
    first = pl.program_id(1) == 0
    lane = lax.broadcasted_iota(jnp.int32, (BLOCK, LANES), 1)
    col = lax.broadcasted_iota(jnp.int32, (BLOCK, 2 * BLOCK), 1)
    no_prev = jnp.logical_and(first, col < BLOCK)
    for p in range(BRANCH_WIDTH // LANES):
        kcol = (p // kv_div) * LANES
        kk = jnp.concatenate([kp_ref[:, kcol:kcol + LANES], kc_ref[:, kcol:kcol + LANES]], axis=0)
        vv = jnp.concatenate([vp_ref[:, kcol:kcol + LANES], vc_ref[:, kcol:kcol + LANES]], axis=0)
        for i in range(rblk):
            rows = slice(i * BLOCK, (i + 1) * BLOCK)
            q = q_ref[rows, p * LANES:(p + 1) * LANES]
            kblk = kk[i * BLOCK:(i + 2) * BLOCK]
            vblk = vv[i * BLOCK:(i + 2) * BLOCK]
            outs, lses = [], []
            for e in range(2):
                s = lax.dot_general(q * hm_ref[e], kblk, _CONTRACT_LANES, preferred_element_type=F32)
                s = s + bias_ref[2 * p + e]
                if i == 0:
                    s = jnp.where(no_prev, NEG, s)
                m = jnp.max(s, axis=1, keepdims=True)
                if has_sink:
                    sk = sink_ref[2 * p + e]
                    m = jnp.maximum(m, sk)
                pr = jnp.exp(s - m)
                den = jnp.sum(pr, axis=1, keepdims=True)
                if has_sink:
                    den = den + jnp.exp(sk - m)
                outs.append(jnp.dot(pr.astype(BF16), vblk, preferred_element_type=F32) / den)
                lses.append(m + jnp.log(den))
            o_ref[rows, p * LANES:(p + 1) * LANES] = jnp.where(lane < HEAD_DIM, outs[0], outs[1]).astype(o_ref.dtype)
            if want_lse:
                refs[8][rows, p * LANES:(p + 1) * LANES] = jnp.where(lane < HEAD_DIM, lses[0], lses[1])


def _band_attn(q_arr, k_arr, v_arr, bias, nsub, sub_len, qcb, kcb, vcb, kv_w, sinks=None, want_lse=False):
    t = q_arr.shape[0]
    rblk = min(BAND_RBLK, sub_len // BLOCK)
    step_rows = rblk * BLOCK
    nbs = sub_len // step_rows
    kv_div = BRANCH_WIDTH // kv_w

    def cur(u, n):
        return u * nbs + n

    def prev(u, n):
        return jnp.where(n == 0, u * nbs * rblk, (u * nbs + n) * rblk - 1)

    in_specs = [
        pl.BlockSpec((step_rows, BRANCH_WIDTH), lambda u, n: (cur(u, n), qcb)),
        pl.BlockSpec((2, 1, LANES), lambda u, n: (0, 0, 0)),
        pl.BlockSpec((BLOCK, kv_w), lambda u, n: (prev(u, n), kcb)),
        pl.BlockSpec((step_rows, kv_w), lambda u, n: (cur(u, n), kcb)),
        pl.BlockSpec((BLOCK, kv_w), lambda u, n: (prev(u, n), vcb)),
        pl.BlockSpec((step_rows, kv_w), lambda u, n: (cur(u, n), vcb)),
        pl.BlockSpec(bias.shape, lambda u, n: (0, 0, 0)),
    ]
    args = [q_arr, _pair_masks(), k_arr, k_arr, v_arr, v_arr, bias]
    if sinks is not None:
        in_specs = [pl.BlockSpec(memory_space=pltpu.SMEM)] + in_specs
        args = [sinks] + args
    o_spec = pl.BlockSpec((step_rows, BRANCH_WIDTH), lambda u, n: (cur(u, n), 0))
    out_specs = [o_spec]
    out_shape = [jax.ShapeDtypeStruct((t, BRANCH_WIDTH), BF16)]
    if want_lse:
        out_specs.append(o_spec)
        out_shape.append(jax.ShapeDtypeStruct((t, BRANCH_WIDTH), F32))
    return pl.pallas_call(
        functools.partial(_band_kernel, has_sink=sinks is not None, want_lse=want_lse, kv_div=kv_div, rblk=rblk),
        grid=(nsub, nbs),
        in_specs=in_specs,
        out_specs=out_specs,
        out_shape=out_shape,
        compiler_params=_cparams(("parallel", "arbitrary")),
        name="band_attn",
    )(*args)


def _toeplitz(w, rows, cols):
    length = rows + cols - 1
    u = jnp.pad(w[..., ::-1], [(0, 0)] * (w.ndim - 1) + [(0, 1)])
    flat = jnp.broadcast_to(u[..., None, :], w.shape[:-1] + (rows, length + 1)).reshape(w.shape[:-1] + (-1,))
    skew = flat[..., :rows * length].reshape(w.shape[:-1] + (rows, length))
    return skew[..., rows - 1:rows - 1 + cols]


def _band_bias(table, step, max_dist):
    rel = BLOCK + np.arange(BLOCK)[:, None] - np.arange(2 * BLOCK)[None, :]
    rel_vec = np.arange(BLOCK - (2 * BLOCK - 1), 2 * BLOCK)
    bias = _toeplitz(table[_t5_bucket(jnp.asarray(rel_vec * step))].T.astype(F32), BLOCK, 2 * BLOCK)
    ok = (rel >= 0) & (rel <= max_dist)
    return jnp.where(jnp.asarray(ok)[None], bias, NEG)


def _merge_kernel(x_ref, ya_ref, yb0_ref, yb1_ref, yb2_ref, l0_ref, l1_ref, l2_ref, yc_ref, yd_ref,
                  za_ref, zb_ref, zc_ref, zd_ref, ga_ref, gb_ref, gc_ref, gd_ref, wb_ref, wo_ref, *rest):
    if len(rest) == 7:
        ng_ref, o_ref, xn_ref, y1_ref, y2_ref, s1_ref, s2_ref = rest
    else:
        ng_ref = xn_ref = None
        o_ref, y1_ref, y2_ref, s1_ref, s2_ref = rest
    ncol = BRANCH_WIDTH // LANES
    for src, dst in ((yb1_ref, y1_ref), (yb2_ref, y2_ref), (l1_ref, s1_ref), (l2_ref, s2_ref)):
        dil = src.shape[1]
        for r in range(dil):
            blk = src[0, r].astype(F32)
            for c in range(ncol):
                dst[c, pl.ds(r, src.shape[2], stride=dil), :] = blk[:, c * LANES:(c + 1) * LANES]
    wide = lambda ref: jnp.concatenate([ref[c] for c in range(ncol)], axis=1)
    l0, l1, l2 = l0_ref[0, 0], wide(s1_ref), wide(s2_ref)
    mx = jnp.maximum(jnp.maximum(l0, l1), l2)
    w0, w1, w2 = jnp.exp(l0 - mx), jnp.exp(l1 - mx), jnp.exp(l2 - mx)
    yb = (w0 * yb0_ref[0, 0].astype(F32) + w1 * wide(y1_ref) + w2 * wide(y2_ref)) / (w0 + w1 + w2)
    ys = (ya_ref[...].astype(F32), yb, yc_ref[...].astype(F32), yd_ref[...].astype(F32))
    zs = (za_ref, zb_ref, zc_ref, zd_ref)
    gs = (ga_ref, gb_ref, gc_ref, gd_ref)
    merged = jnp.zeros(o_ref.shape, F32)
    for n in range(N_BRANCH):
        z = zs[n][...].astype(F32)
        u = (ys[n] * (z * jax.nn.sigmoid(z))).astype(BF16)
        proj = jnp.dot(u, wb_ref[n], preferred_element_type=F32)
        merged = merged + jax.nn.sigmoid(gs[n][...].astype(F32)) * proj
    x_new = x_ref[...] + jnp.dot(merged.astype(BF16), wo_ref[...], preferred_element_type=F32)
    o_ref[...] = x_new
    if xn_ref is not None:
        xn_ref[...] = _rms_bf16(x_new, ng_ref[...])


def _merge(x2, ya, yb, lb, yc, yd, pp, wb, wo, batch, seq, next_gain=None, tm=512):
    t = x2.shape[0]
    bw = BRANCH_WIDTH
    nsb = seq // tm
    row = lambda w, c: pl.BlockSpec((tm, w), lambda i: (i, c))

    def sub(dil):
        return pl.BlockSpec((1, dil, tm // dil, bw), lambda i: (i // nsb, 0, i % nsb, 0))

    dils = [dil for _, dil in B_GROUPS]
    yb = [a.reshape(batch, dil, seq // dil, bw) for a, dil in zip(yb, dils)]
    lb = [a.reshape(batch, dil, seq // dil, bw) for a, dil in zip(lb, dils)]
    in_specs = ([row(D_MODEL, 0), row(bw, 0)] + [sub(dil) for dil in dils] * 2 + [row(bw, 0), row(bw, 0)]
                + [row(bw, PP_AZ // bw), row(bw, PP_BZ // bw), row(bw, PP_CZ // bw), row(bw, PP_DZ // bw)]
                + [row(D_MODEL, PP_G // D_MODEL + n) for n in range(N_BRANCH)]
                + [pl.BlockSpec((N_BRANCH, bw, D_MODEL), lambda i: (0, 0, 0)),
                   pl.BlockSpec((D_MODEL, D_MODEL), lambda i: (0, 0))])
    args = [x2, ya, yb[0], yb[1], yb[2], lb[0], lb[1], lb[2], yc, yd, pp, pp, pp, pp, pp, pp, pp, pp, wb, wo]
    out_specs = [row(D_MODEL, 0)]
    out_shape = [jax.ShapeDtypeStruct((t, D_MODEL), F32)]
    if next_gain is not None:
        in_specs.append(pl.BlockSpec((1, D_MODEL), lambda i: (0, 0)))
        args.append(next_gain.reshape(1, D_MODEL))
        out_specs.append(row(D_MODEL, 0))
        out_shape.append(jax.ShapeDtypeStruct((t, D_MODEL), BF16))
    return pl.pallas_call(
        _merge_kernel,
        grid=(t // tm,),
        in_specs=in_specs,
        out_specs=out_specs,
        out_shape=out_shape,
        scratch_shapes=[pltpu.VMEM((bw // LANES, tm, LANES), F32)] * 4,
        compiler_params=_cparams(("parallel",)),
        name="merge",
    )(*args)


def _layer_weights(w_in, qk_gain_a, qk_gain_b, qk_gain_d):
    def cols(start, width):
        return w_in[:, :, start:start + width]

    n_bq = len(B_GROUPS) * B_HEADS * HEAD_DIM
    dk = cols(_D0 + 512, 128)
    dv = cols(_D0 + 640, 128)
    dup = lambda a: jnp.concatenate([a[:, :, :64], a[:, :, :64], a[:, :, 64:], a[:, :, 64:]], axis=-1)
    w_pn = jnp.concatenate([cols(_A0, 512), cols(_A0 + 512, 512), cols(_D0, 512), dup(dk), dup(dv)],
                           axis=-1).astype(BF16)
    w_pp = jnp.concatenate([cols(_G0, N_BRANCH * D_MODEL), cols(_A0 + 1024, 512), cols(_A0 + 1536, 512),
                            cols(_B0 + 3 * n_bq, 512), cols(_C0 + 416, 512), cols(_D0 + 768, 512)],
                           axis=-1).astype(BF16)
    w_bg = [jnp.concatenate([cols(_B0 + g * 512, 512), cols(_B0 + n_bq + g * 512, 512),
                             cols(_B0 + 2 * n_bq + g * 512, 512)], axis=-1).astype(BF16)
            for g in range(len(B_GROUPS))]
    depth = w_in.shape[0]
    zeros = lambda w: jnp.zeros((depth, D_MODEL, w), w_in.dtype)
    w_pf = jnp.concatenate([cols(_A0 + 2048, 296), zeros(216), cols(_C0, 416), zeros(96)], axis=-1).astype(BF16)

    scale = HEAD_DIM ** -0.5
    tile = lambda g, reps: jnp.tile(g, (1, reps))
    hg = jnp.concatenate([tile(qk_gain_a[:, 0] * scale, 8), tile(qk_gain_a[:, 1], 8),
                          tile(qk_gain_d[:, 0] * scale, 8), tile(qk_gain_d[:, 1], 4),
                          jnp.ones((depth, 256), F32)], axis=-1)
    flag = jnp.concatenate([jnp.ones((PN_DV,), F32), jnp.zeros((PN_W - PN_DV,), F32)])
    hg_b = jnp.concatenate([tile(qk_gain_b[:, 0] * scale, 8), tile(qk_gain_b[:, 1], 8),
                            jnp.ones((depth, 512), F32)], axis=-1)
    flag_b = jnp.concatenate([jnp.ones((2 * BRANCH_WIDTH,), F32), jnp.zeros((BRANCH_WIDTH,), F32)])
    return w_pn, w_pp, w_bg, w_pf, hg, flag, hg_b, flag_b


def _c_weights(qk_gain_c, w_q_b, w_kv_b, seq):
    src_q, src_k, gain_idx, head, pe_src, rope_j, rope_half = _c_layout_tables()
    take = lambda w, src: jnp.where(jnp.asarray(src >= 0), jnp.take(w, jnp.asarray(np.maximum(src, 0)), axis=-1), 0.0)
    wq = take(w_q_b, src_q).astype(BF16)
    wk = take(w_kv_b, src_k).astype(BF16)
    v_src = np.array([h * (C_NOPE + C_V) + C_NOPE + c for h in range(C_HEADS) for c in range(C_V)])
    wv = jnp.take(w_kv_b, jnp.asarray(v_src), axis=-1).astype(BF16)
    ppe = np.zeros((LANES, C_QK_W), np.float32)
    ln = np.nonzero(pe_src >= 0)[0]
    ppe[pe_src[ln], ln] = 1
    hp = head[:C_PAIR]
    grp = ((hp[:, None] == hp[None, :]) & (hp[:, None] >= 0)).astype(np.float32)
    g_take = lambda g: jnp.where(jnp.asarray(gain_idx >= 0), jnp.take(g, jnp.asarray(np.maximum(gain_idx, 0)), axis=-1), 0.0)
    gq = g_take(qk_gain_c[:, 0]) * (C_NOPE + C_ROPE) ** -0.5
    gk = g_take(qk_gain_c[:, 1])
    freq = ROPE_THETA ** (-jnp.arange(C_ROPE_HALF, dtype=F32) / C_ROPE_HALF)
    ang = jnp.arange(seq).astype(F32)[:, None] * freq[None, :]
    cos_j, sin_j = jnp.cos(ang), jnp.sin(ang)
    place = np.zeros((C_ROPE_HALF, C_PAIR), np.float32)
    lanes = np.nonzero(rope_j[:C_PAIR] >= 0)[0]
    place[rope_j[lanes], lanes] = 1
    is_rope = jnp.asarray(rope_j[:C_PAIR] >= 0)
    hi = lax.Precision.HIGHEST
    cos = jnp.where(is_rope, jnp.dot(cos_j, jnp.asarray(place), precision=hi), 1.0)
    sin = jnp.dot(sin_j, jnp.asarray(place), precision=hi)
    s1 = jnp.where(jnp.asarray(rope_half[:C_PAIR] == 0), -sin, 0.0)
    s2 = jnp.where(jnp.asarray(rope_half[:C_PAIR] == 1), sin, 0.0)
    return wq, wk, wv, jnp.asarray(ppe, BF16), jnp.asarray(grp, BF16), gq, gk, cos, s1, s2


def _a_bias_tiles(table):
    dist_vec = np.arange(-(BLOCK - 1), A_BIAS_TILES * BLOCK)
    vec = table[_t5_bucket(jnp.asarray(dist_vec))].T.astype(F32)
    windows = jnp.stack([vec[:, d * BLOCK:d * BLOCK + 2 * BLOCK - 1] for d in range(A_BIAS_TILES)], axis=1)
    return _toeplitz(windows, BLOCK, BLOCK)


def kernel(x, norm_gain, w_in, qk_gain_a, qk_gain_b, qk_gain_c, qk_gain_d, c_q_gain, c_kv_gain,
           w_q_b, w_kv_b, sinks, rel_bias, w_branch, w_out):
    batch, seq, d_model = x.shape
    depth = w_in.shape[0]
    t = batch * seq
    x2 = x.reshape(t, d_model)

    w_pn, w_pp, w_bg, w_pf, hg, flag, hg_b, flag_b = _layer_weights(w_in, qk_gain_a, qk_gain_b, qk_gain_d)
    wq, wk, wv, ppe, grp, gq, gk, cos, s1, s2 = _c_weights(qk_gain_c, w_q_b, w_kv_b, seq)
    wb = w_branch.astype(BF16)
    wo = w_out.astype(BF16)

    bias_a = _a_bias_tiles(rel_bias[:, :A_HEADS])
    bias_b = [_band_bias(rel_bias[:, A_HEADS + g * B_HEADS:A_HEADS + (g + 1) * B_HEADS], dil, window // dil)
              for g, (window, dil) in enumerate(B_GROUPS)]
    bias_d = _band_bias(rel_bias[:, N_BIAS_HEADS - D_HEADS:], 1, D_WINDOW - 1)

    xn = _norm(x2, norm_gain[0])
    for l in range(depth):
        pn = _proj(xn, w_pn[l], BF16, head_gain=hg[l], flag=flag)
        pp = _proj(xn, w_pp[l], BF16)
        pf = _proj(xn, w_pf[l], F32)

        sel = _a_select(pf, batch, seq)
        ya = _a_attn(pn, pp, sel, bias_a, batch, seq)

        yb, lb = [], []
        for g, (window, dil) in enumerate(B_GROUPS):
            bg = _proj(xn, w_bg[g][l], BF16, head_gain=hg_b[l], flag=flag_b, dil=dil, seq=seq)
            o, lse = _band_attn(bg, bg, bg, bias_b[g], batch * dil, seq // dil, 0, 1, 2, BRANCH_WIDTH,
                                want_lse=True)
            yb.append(o)
            lb.append(lse)

        qc, kc, vc = _c_prep(pf, c_q_gain[l].reshape(1, -1), c_kv_gain[l].reshape(1, -1), wq[l], wk[l], wv[l],
                             ppe, grp, gq[l].reshape(1, -1), gk[l].reshape(1, -1), cos, s1, s2, seq)
        yc = _c_attn(qc, kc, vc, batch, seq)

        (yd,) = _band_attn(pn, pn, pn, bias_d, batch, seq, PN_DQ // BRANCH_WIDTH, PN_DK // 256, PN_DV // 256,
                           256, sinks=sinks[l])

        if l + 1 < depth:
            x2, xn = _merge(x2, ya, yb, lb, yc, yd, pp, wb[l], wo[l], batch, seq, next_gain=norm_gain[l + 1])
        else:
            (x2,) = _merge(x2, ya, yb, lb, yc, yd, pp, wb[l], wo[l], batch, seq)
    return x2.reshape(batch, seq, d_model)
```

```python
import functools
import math

import numpy as np
import jax
import jax.numpy as jnp
from jax import lax
from jax.experimental import pallas as pl
from jax.experimental.pallas import tpu as pltpu

F32 = jnp.float32
BF16 = jnp.bfloat16

D_MODEL = 1024
BLOCK = 128
HEAD_DIM = 64
N_BRANCH = 4
BRANCH_WIDTH = 512
EPS = 1e-6
A_HEADS = 8
IDX_HEADS = 8
IDX_DIM = 32
TOPK_MAX = 256
B_GROUPS = ((128, 1), (512, 4), (2048, 16))
B_HEADS = 8
C_HEADS = 8
C_NOPE = 64
C_ROPE = 32
C_V = 64
C_Q_LORA = 256
C_KV_LORA = 128
ROPE_THETA = 10000.0
D_HEADS = 8
D_KV_HEADS = 2
D_WINDOW = 128
NUM_BUCKETS = 32
MAX_DISTANCE = 2048
N_BIAS_HEADS = A_HEADS + len(B_GROUPS) * B_HEADS + D_HEADS

LANES = 128
NEG = -1e30
INT_MIN = -(2 ** 31)

_A0 = 0
_B0 = 2344
_C0 = 7464
_D0 = 8392
_G0 = 9672

PN_AQ, PN_AK, PN_DQ, PN_DK, PN_DV, PN_W = 0, 512, 1024, 1536, 1792, 2048
PP_G, PP_AV, PP_AZ, PP_BZ, PP_CZ, PP_DZ, PP_W = 0, 4096, 4608, 5120, 5632, 6144, 6656
PF_W = 1024
BG_W = 3 * BRANCH_WIDTH

A_BIAS_TILES = MAX_DISTANCE // BLOCK + 2


def _cparams(sem, vmem_mb=48):
    return pltpu.CompilerParams(dimension_semantics=sem, vmem_limit_bytes=vmem_mb * 1024 * 1024)


def _t5_bucket(dist):
    max_exact = NUM_BUCKETS // 2
    d = jnp.maximum(dist, 0)
    logd = jnp.log(jnp.maximum(d, 1).astype(F32) / max_exact)
    large = max_exact + (logd / math.log(MAX_DISTANCE / max_exact) * (NUM_BUCKETS - max_exact)).astype(jnp.int32)
    return jnp.where(d < max_exact, d, jnp.minimum(large, NUM_BUCKETS - 1))


def _rms_bf16(x, gain):
    return (x * lax.rsqrt(jnp.mean(x * x, axis=-1, keepdims=True) + EPS) * gain).astype(BF16)


def _norm_kernel(x_ref, g_ref, o_ref):
    o_ref[...] = _rms_bf16(x_ref[...], g_ref[...])


def _norm(x2, gain, tm=1024):
    t, d = x2.shape
    return pl.pallas_call(
        _norm_kernel,
        grid=(t // tm,),
        in_specs=[pl.BlockSpec((tm, d), lambda i: (i, 0)), pl.BlockSpec((1, d), lambda i: (0, 0))],
        out_specs=pl.BlockSpec((tm, d), lambda i: (i, 0)),
        out_shape=jax.ShapeDtypeStruct((t, d), BF16),
        compiler_params=_cparams(("parallel",)),
        name="norm",
    )(x2, gain.reshape(1, d))


def _proj_kernel(xn_ref, w_ref, *rest, norm, dil):
    rest = list(rest)
    res_ref = rest.pop(-1) if dil is not None and dil > 1 else None
    if norm:
        hg_ref, flag_ref, bd_ref, o_ref = rest
    else:
        (o_ref,) = rest

    def emit(h):
        if dil is None:
            o_ref[...] = h.astype(o_ref.dtype)
        elif dil == 1:
            o_ref[0, 0] = h.astype(o_ref.dtype)
        else:
            sub = h.shape[0] // dil
            for c in range(h.shape[1] // LANES):
                res_ref[c] = h[:, c * LANES:(c + 1) * LANES]
            for r in range(dil):
                o_ref[0, r] = jnp.concatenate(
                    [res_ref[c, pl.ds(r, sub, stride=dil), :] for c in range(h.shape[1] // LANES)],
                    axis=1).astype(o_ref.dtype)

    h = jnp.dot(xn_ref[...], w_ref[...], preferred_element_type=F32)
    if norm:
        ss = jnp.dot((h * h).astype(BF16), bd_ref[...], preferred_element_type=F32)
        scale = lax.rsqrt(ss * (1.0 / HEAD_DIM) + EPS) * hg_ref[...]
        h = h * jnp.where(flag_ref[...] > 0, scale, 1.0)
    emit(h)


def _proj(xn, w, out_dtype, head_gain=None, flag=None, dil=None, seq=None, tm=1024, tn=512):
    t, d = xn.shape
    n = w.shape[1]
    norm = head_gain is not None
    scratch = []
    if dil is None:
        out_spec = pl.BlockSpec((tm, tn), lambda i, j: (i, j))
        out_shape = jax.ShapeDtypeStruct((t, n), out_dtype)
        kdil = None
    else:
        nsb = seq // tm
        out_spec = pl.BlockSpec((1, dil, tm // dil, tn), lambda i, j: (i // nsb, 0, i % nsb, j))
        out_shape = jax.ShapeDtypeStruct((t // seq, dil, seq // dil, n), out_dtype)
        kdil = dil
        if dil > 1:
            scratch.append(pltpu.VMEM((tn // LANES, tm, LANES), F32))
    in_specs = [pl.BlockSpec((tm, d), lambda i, j: (i, 0)),
                pl.BlockSpec((d, tn), lambda i, j: (0, j))]
    args = [xn, w]
    if norm:
        lane = np.arange(tn)
        bd = jnp.asarray((lane[:, None] // HEAD_DIM == lane[None, :] // HEAD_DIM), BF16)
        in_specs += [pl.BlockSpec((1, tn), lambda i, j: (0, j)),
                     pl.BlockSpec((1, tn), lambda i, j: (0, j)),
                     pl.BlockSpec((tn, tn), lambda i, j: (0, 0))]
        args += [head_gain.reshape(1, n), flag.reshape(1, n), bd]
    out = pl.pallas_call(
        functools.partial(_proj_kernel, norm=norm, dil=kdil),
        grid=(t // tm, n // tn),
        in_specs=in_specs,
        out_specs=out_spec,
        out_shape=out_shape,
        scratch_shapes=scratch,
        compiler_params=_cparams(("parallel", "parallel")),
        name=("proj_norm" if norm else "proj") + ("" if dil is None else "_dil%d" % dil),
    )(*args)
    return out.reshape(t, n)


C_PAIR = 256
C_QK_W = (C_HEADS // 2) * C_PAIR
C_ROPE_HALF = C_ROPE // 2


def _c_lane(h, c):
    base = (h // 2) * C_PAIR
    e = h % 2
    if c < C_NOPE:
        return base + e * C_NOPE + c
    return base + 2 * C_NOPE + e * C_ROPE + (c - C_NOPE)


def _c_layout_tables():
    src_q = np.full(C_QK_W, -1, np.int64)
    src_k = np.full(C_QK_W, -1, np.int64)
    gain_idx = np.full(C_QK_W, -1, np.int64)
    head = np.full(C_QK_W, -1, np.int64)
    pe_src = np.full(C_QK_W, -1, np.int64)
    rope_j = np.full(C_QK_W, -1, np.int64)
    rope_half = np.zeros(C_QK_W, np.int64)
    for h in range(C_HEADS):
        for c in range(C_NOPE + C_ROPE):
            ln = _c_lane(h, c)
            src_q[ln] = h * (C_NOPE + C_ROPE) + c
            gain_idx[ln] = c
            head[ln] = h
            if c < C_NOPE:
                src_k[ln] = h * (C_NOPE + C_V) + c
            else:
                r = c - C_NOPE
                pe_src[ln] = r
                rope_j[ln] = r % C_ROPE_HALF
                rope_half[ln] = r // C_ROPE_HALF
    return src_q, src_k, gain_idx, head, pe_src, rope_j, rope_half


def _c_prep_kernel(pf_ref, cqg_ref, ckvg_ref, wq_ref, wk_ref, wv_ref, ppe_ref, grp_ref,
                   gq_ref, gk_ref, cos_ref, s1_ref, s2_ref, q_ref, k_ref, v_ref):
    blk = pf_ref[...]
    cq = blk[:, :C_Q_LORA]
    ckv = blk[:, C_Q_LORA:C_Q_LORA + C_KV_LORA]
    pe = blk[:, C_Q_LORA + C_KV_LORA:]

    def rms(v, g):
        return v * lax.rsqrt(jnp.mean(v * v, axis=-1, keepdims=True) + EPS) * g

    def head_norm_rope(raw, gain):
        sq = (raw * raw).astype(BF16)
        ss = jnp.concatenate([jnp.dot(sq[:, p * C_PAIR:(p + 1) * C_PAIR], grp_ref[...], preferred_element_type=F32)
                              for p in range(C_QK_W // C_PAIR)], axis=1)
        y = raw * lax.rsqrt(ss * (1.0 / (C_NOPE + C_ROPE)) + EPS) * gain
        up = pltpu.roll(y, C_QK_W - C_ROPE_HALF, 1)
        dn = pltpu.roll(y, C_ROPE_HALF, 1)
        wide = lambda ref: jnp.concatenate([ref[...]] * (C_QK_W // C_PAIR), axis=1)
        return y * wide(cos_ref) + up * wide(s1_ref) + dn * wide(s2_ref)

    cqn = rms(cq, cqg_ref[...]).astype(BF16)
    q_raw = jnp.dot(cqn, wq_ref[...], preferred_element_type=F32)
    q_ref[...] = head_norm_rope(q_raw, gq_ref[...]).astype(BF16)

    ckvn = rms(ckv, ckvg_ref[...]).astype(BF16)
    pe_hi = pe.astype(BF16)
    pe_lo = (pe - pe_hi.astype(F32)).astype(BF16)
    k_raw = (jnp.dot(ckvn, wk_ref[...], preferred_element_type=F32)
             + jnp.dot(pe_hi, ppe_ref[...], preferred_element_type=F32)
             + jnp.dot(pe_lo, ppe_ref[...], preferred_element_type=F32))
    k_ref[...] = head_norm_rope(k_raw, gk_ref[...]).astype(BF16)
    v_ref[...] = jnp.dot(ckvn, wv_ref[...], preferred_element_type=F32).astype(BF16)


def _c_prep(pf, cqg, ckvg, wq, wk, wv, ppe, grp, gq, gk, cos, s1, s2, seq, tm=512):
    t = pf.shape[0]
    nsb = seq // tm
    full = lambda shape: pl.BlockSpec(shape, lambda i: (0,) * len(shape))
    tab = pl.BlockSpec((tm, C_PAIR), lambda i: (i % nsb, 0))
    return pl.pallas_call(
        _c_prep_kernel,
        grid=(t // tm,),
        in_specs=[pl.BlockSpec((tm, 512), lambda i: (i, 1)),
                  full((1, C_Q_LORA)), full((1, C_KV_LORA)),
                  full((C_Q_LORA, C_QK_W)), full((C_KV_LORA, C_QK_W)), full((C_KV_LORA, BRANCH_WIDTH)),
                  full((LANES, C_QK_W)), full((C_PAIR, C_PAIR)),
                  full((1, C_QK_W)), full((1, C_QK_W)), tab, tab, tab],
        out_specs=[pl.BlockSpec((tm, C_QK_W), lambda i: (i, 0)),
                   pl.BlockSpec((tm, C_QK_W), lambda i: (i, 0)),
                   pl.BlockSpec((tm, BRANCH_WIDTH), lambda i: (i, 0))],
        out_shape=[jax.ShapeDtypeStruct((t, C_QK_W), BF16),
                   jax.ShapeDtypeStruct((t, C_QK_W), BF16),
                   jax.ShapeDtypeStruct((t, BRANCH_WIDTH), BF16)],
        compiler_params=_cparams(("parallel",)),
        name="c_prep",
    )(pf, cqg, ckvg, wq, wk, wv, ppe, grp, gq, gk, cos, s1, s2)


def _flash_update(e, s, v_ones, m_ref, acc_ref, rows=slice(None)):
    m_prev = m_ref[e, rows]
    m_new = jnp.maximum(m_prev, jnp.max(s, axis=1, keepdims=True))
    alpha = jnp.exp(m_prev - m_new)
    p = jnp.exp(s - jnp.concatenate([m_new] * (s.shape[1] // LANES), axis=1))
    acc_ref[e, rows] = (jnp.concatenate([alpha, alpha], axis=1) * acc_ref[e, rows]
                        + jnp.dot(p.astype(BF16), v_ones, preferred_element_type=F32))
    m_ref[e, rows] = m_new


def _flash_init(m_ref, acc_ref):
    m_ref[...] = jnp.full(m_ref.shape, NEG, F32)
    acc_ref[...] = jnp.zeros(acc_ref.shape, F32)


def _flash_finish(o_ref, acc_ref):
    lane = lax.broadcasted_iota(jnp.int32, o_ref.shape, 1)
    o0 = acc_ref[0, :, :LANES] / acc_ref[0, :, LANES:]
    o1 = acc_ref[1, :, :LANES] / acc_ref[1, :, LANES:]
    o_ref[...] = jnp.where(lane < HEAD_DIM, o0, o1).astype(o_ref.dtype)


def _with_ones(v_chunk):
    return jnp.concatenate([v_chunk, jnp.ones(v_chunk.shape, v_chunk.dtype)], axis=1)


_CONTRACT_LANES = (((1,), (1,)), ((), ()))


def _head_masks(width, ranges0, ranges1):
    m = np.zeros((2, 1, width), np.float32)
    for e, ranges in enumerate((ranges0, ranges1)):
        for lo, hi in ranges:
            m[e, 0, lo:hi] = 1
    return jnp.asarray(m, BF16)


def _pair_masks():
    return _head_masks(LANES, [(0, HEAD_DIM)], [(HEAD_DIM, LANES)])


def _transpose_keys(k_ref, kt_ref, rows=512):
    def body(c, carry):
        off = pl.multiple_of(c * rows, rows)
        kt_ref[:, pl.ds(off, rows)] = k_ref[pl.ds(off, rows), :].astype(F32).T.astype(kt_ref.dtype)
        return carry
    lax.fori_loop(0, k_ref.shape[0] // rows, body, 0)


def _chunk_loop(chunk, n, unroll, carry=None):
    start = 0
    width = unroll
    while width >= 1:
        def body(g, carry, width=width, start=start):
            for j in range(width):
                carry = chunk(start + g * width + j, carry)
            return carry
        groups = (n - start) // width
        carry = lax.fori_loop(0, groups, body, carry)
        start = start + groups * width
        width //= 2
    return carry


def _c_attn_kernel(q_ref, hm_ref, k_ref, v_ref, o_ref, kt_ref, m_ref, acc_ref, *, tq, tk, unroll):
    qi = pl.program_id(2)

    @pl.when(qi == 0)
    def _():
        _transpose_keys(k_ref, kt_ref)

    q = q_ref[...]
    qs = (q * hm_ref[0], q * hm_ref[1])
    _flash_init(m_ref, acc_ref)
    n_full = (qi * tq) // tk

    def chunk(c, row0=0, diagonal=False):
        off = pl.multiple_of(c * tk, tk)
        kc = kt_ref[:, pl.ds(off, tk)]
        vc = _with_ones(v_ref[pl.ds(off, tk), :])
        for e in range(2):
            if not diagonal:
                rows = slice(row0, tq)
                _flash_update(e, jnp.dot(qs[e][rows], kc, preferred_element_type=F32), vc, m_ref, acc_ref, rows)
                continue
            rows = slice(row0, row0 + tk)
            s = jnp.dot(qs[e][rows], kc, preferred_element_type=F32)
            s = jnp.where(lax.broadcasted_iota(jnp.int32, s.shape, 1) <= lax.broadcasted_iota(jnp.int32, s.shape, 0),
                          s, NEG)
            _flash_update(e, s, vc, m_ref, acc_ref, rows)
            if row0 + tk < tq:
                rows = slice(row0 + tk, tq)
                _flash_update(e, jnp.dot(qs[e][rows], kc, preferred_element_type=F32), vc, m_ref, acc_ref, rows)

    _chunk_loop(lambda c, _: chunk(c), n_full, unroll)
    for j in range(tq // tk):
        chunk(n_full + j, row0=j * tk, diagonal=True)
    _flash_finish(o_ref, acc_ref)


def _c_attn(qc, kc, vc, batch, seq, tq=1024, tk=512, unroll=4):
    t = qc.shape[0]
    nq = seq // tq
    npair = C_HEADS // 2
    return pl.pallas_call(
        functools.partial(_c_attn_kernel, tq=tq, tk=tk, unroll=unroll),
        grid=(batch, npair, nq),
        in_specs=[pl.BlockSpec((tq, C_PAIR), lambda b, p, i: (b * nq + i, p)),
                  pl.BlockSpec((2, 1, C_PAIR), lambda b, p, i: (0, 0, 0)),
                  pl.BlockSpec((seq, C_PAIR), lambda b, p, i: (b, p)),
                  pl.BlockSpec((seq, LANES), lambda b, p, i: (b, p))],
        out_specs=pl.BlockSpec((tq, LANES), lambda b, p, i: (b * nq + i, p)),
        out_shape=jax.ShapeDtypeStruct((t, BRANCH_WIDTH), BF16),
        scratch_shapes=[pltpu.VMEM((C_PAIR, seq), BF16),
                        pltpu.VMEM((2, tq, LANES), F32), pltpu.VMEM((2, tq, 2 * LANES), F32)],
        compiler_params=_cparams(("parallel", "parallel", "arbitrary")),
        name="c_attn",
    )(qc, _head_masks(C_PAIR, [(0, C_NOPE), (2 * C_NOPE, 2 * C_NOPE + C_ROPE)],
                      [(C_NOPE, 2 * C_NOPE), (2 * C_NOPE + C_ROPE, 2 * C_NOPE + 2 * C_ROPE)]), kc, vc)


def _a_attn_kernel(q_ref, hm_ref, k_ref, v_ref, sel_ref, bias_ref, o_ref, kt_ref, m_ref, acc_ref,
                   *, tq, tk, unroll):
    qi = pl.program_id(2)

    @pl.when(qi == 0)
    def _():
        _transpose_keys(k_ref, kt_ref)

    q = q_ref[...]
    qs = (q * hm_ref[0], q * hm_ref[1])
    _flash_init(m_ref, acc_ref)
    n_chunks = ((qi + 1) * tq + tk - 1) // tk

    def chunk(c):
        off = pl.multiple_of(c * tk, tk)
        kc = kt_ref[:, pl.ds(off, tk)]
        vc = _with_ones(v_ref[pl.ds(off, tk), :])
        selmask = sel_ref[:, pl.ds(off, tk)].astype(F32)
        for e in range(2):
            s = jnp.dot(qs[e], kc, preferred_element_type=F32)
            rows = []
            for i in range(tq // BLOCK):
                tiles = []
                for j in range(tk // BLOCK):
                    d = (qi * (tq // BLOCK) + i) - (c * (tk // BLOCK) + j)
                    d = jnp.clip(d, 0, A_BIAS_TILES - 1)
                    tiles.append(bias_ref[e, d])
                rows.append(jnp.concatenate(tiles, axis=1))
            s = s + jnp.concatenate(rows, axis=0) + selmask
            _flash_update(e, s, vc, m_ref, acc_ref)

    _chunk_loop(lambda c, _: chunk(c), n_chunks, unroll)
    _flash_finish(o_ref, acc_ref)


def _a_attn(pn, pp, sel, bias_tiles, batch, seq, tq=512, tk=512, unroll=4):
    t = pn.shape[0]
    nq = seq // tq
    npair = A_HEADS // 2
    qcol, kcol, vcol = PN_AQ // LANES, PN_AK // LANES, PP_AV // LANES
    return pl.pallas_call(
        functools.partial(_a_attn_kernel, tq=tq, tk=tk, unroll=unroll),
        grid=(batch, npair, nq),
        in_specs=[pl.BlockSpec((tq, LANES), lambda b, p, i: (b * nq + i, qcol + p)),
                  pl.BlockSpec((2, 1, LANES), lambda b, p, i: (0, 0, 0)),
                  pl.BlockSpec((seq, LANES), lambda b, p, i: (b, kcol + p)),
                  pl.BlockSpec((seq, LANES), lambda b, p, i: (b, vcol + p)),
                  pl.BlockSpec((tq, seq), lambda b, p, i: (b * nq + i, 0)),
                  pl.BlockSpec((2, A_BIAS_TILES, BLOCK, BLOCK), lambda b, p, i: (p, 0, 0, 0))],
        out_specs=pl.BlockSpec((tq, LANES), lambda b, p, i: (b * nq + i, p)),
        out_shape=jax.ShapeDtypeStruct((t, BRANCH_WIDTH), BF16),
        scratch_shapes=[pltpu.VMEM((LANES, seq), BF16),
                        pltpu.VMEM((2, tq, LANES), F32), pltpu.VMEM((2, tq, 2 * LANES), F32)],
        compiler_params=_cparams(("parallel", "parallel", "arbitrary")),
        name="a_attn",
    )(pn, _pair_masks(), pn, pp, sel, bias_tiles)


def _sortable_key(score):
    bits = pltpu.bitcast(score, jnp.int32)
    return bits ^ ((bits >> 31) & jnp.int32(0x7FFFFFFF))


def _bit_transpose32(words):
    a = list(words)
    j, m = 16, 0x0000FFFF
    while j:
        for k in range(32):
            if k & j:
                continue
            t = (a[k] ^ lax.shift_right_logical(a[k + j], jnp.int32(j))) & m
            a[k] = a[k] ^ t
            a[k + j] = a[k + j] ^ lax.shift_left(t, jnp.int32(j))
        j >>= 1
        m ^= (m << j) & 0xFFFFFFFF
    return a


SEL_ROWS = 512
CNT_ROWS = SEL_ROWS
PLANE_ROWS = 256
GROUP_UNROLL = 8
SCORE_UNROLL = 2
COUNT_UNROLL = 1


def _a_select_kernel(qblk_ref, kblk_ref, ph_ref, pl_ref, pkh_ref, pkl_ref, sel_ref,
                     ikx_ref, keys_ref, planes_ref, eq_ref, iqt_ref, jcut_ref, *, seq, k_sel):
    qi = pl.program_id(1)
    n_sel = (qi * BLOCK + BLOCK + SEL_ROWS - 1) // SEL_ROWS
    n_cnt = n_sel

    @pl.when(qi == 0)
    def _():
        def prep(c, carry):
            off = pl.multiple_of(c * 512, 512)
            kv = kblk_ref[pl.ds(off, 512), :]
            hi = kv.astype(BF16)
            lo = (kv - hi.astype(F32)).astype(BF16)
            ikx_ref[pl.ds(off, 512), :] = (
                jnp.dot(hi, pkh_ref[...], preferred_element_type=F32)
                + jnp.dot(lo, pkl_ref[...], preferred_element_type=F32)).astype(BF16)
            return carry
        lax.fori_loop(0, seq // 512, prep, 0)

        def clear(g, carry):
            planes_ref[g] = jnp.zeros(planes_ref.shape[1:], jnp.int32)
            return carry
        lax.fori_loop(0, planes_ref.shape[0], clear, 0)

    qb = qblk_ref[...]
    iq = qb[:, :IDX_HEADS * IDX_DIM]
    iq_hi = iq.astype(BF16)
    iq_lo = (iq - iq_hi.astype(F32)).astype(BF16)
    iqx = (jnp.dot(iq_hi, ph_ref[...], preferred_element_type=F32)
           + jnp.dot(iq_lo, pl_ref[...], preferred_element_type=F32))
    for h in range(IDX_HEADS):
        iqt_ref[h // 2, :, (h % 2) * LANES:(h % 2 + 1) * LANES] = iqx[:, h * LANES:(h + 1) * LANES].T.astype(BF16)
    iw_t = qb[:, IDX_HEADS * IDX_DIM:IDX_HEADS * IDX_DIM + LANES].T
    iw_rows = [iw_t[IDX_DIM + h:IDX_DIM + h + 1, :] for h in range(IDX_HEADS)]
    q_pos = qi * BLOCK + lax.broadcasted_iota(jnp.int32, (SEL_ROWS, BLOCK), 1)
    k_row = lax.broadcasted_iota(jnp.int32, (SEL_ROWS, BLOCK), 0)

    def score_chunk(c, masked):
        off = pl.multiple_of(c * SEL_ROWS, SEL_ROWS)
        kx = ikx_ref[pl.ds(off, SEL_ROWS), :]
        sc = jnp.zeros((SEL_ROWS, BLOCK), F32)
        for hp in range(IDX_HEADS // 2):
            xx = jnp.dot(kx, iqt_ref[hp], preferred_element_type=F32)
            sc = (sc + jnp.maximum(xx[:, :LANES], 0.0) * iw_rows[2 * hp]
                  + jnp.maximum(xx[:, LANES:], 0.0) * iw_rows[2 * hp + 1])
        key = _sortable_key(sc + 0.0)
        if masked:
            key = jnp.where(off + k_row <= q_pos, key, INT_MIN)
        keys_ref[pl.ds(off, SEL_ROWS), :] = key
        u = key ^ INT_MIN
        for g in range(SEL_ROWS // PLANE_ROWS):
            words = [u[g * PLANE_ROWS + 8 * j:g * PLANE_ROWS + 8 * j + 8] for j in range(32)]
            for i, plane in enumerate(_bit_transpose32(words)):
                planes_ref[c * (SEL_ROWS // PLANE_ROWS) + g, i] = plane

    _chunk_loop(lambda c, _: score_chunk(c, False), n_sel - 1, SCORE_UNROLL)
    score_chunk(n_sel - 1, True)

    def count_rows(pred):
        def cnt_chunk(c, acc):
            off = pl.multiple_of(c * CNT_ROWS, CNT_ROWS)
            hit = jnp.where(pred(keys_ref[pl.ds(off, CNT_ROWS), :], off), 1, 0)
            return acc + jnp.sum(hit.reshape(CNT_ROWS // 8, 8, BLOCK), axis=0)
        acc = _chunk_loop(cnt_chunk, n_cnt, COUNT_UNROLL, jnp.zeros((8, BLOCK), jnp.int32))
        return jnp.sum(acc, axis=0, keepdims=True)

    def count_ge(cand):
        return count_rows(lambda keys, off: keys >= cand)

    n_grp = n_sel * (SEL_ROWS // PLANE_ROWS)
    n_blk = (n_grp + GROUP_UNROLL - 1) // GROUP_UNROLL

    def init_eq(g, carry):
        eq_ref[g] = jnp.where(g < n_grp, jnp.full((8, BLOCK), -1, jnp.int32), jnp.zeros((8, BLOCK), jnp.int32))
        return carry

    lax.fori_loop(0, n_blk * GROUP_UNROLL, init_eq, 0)

    def sweep(cur_plane, prev_plane, prev_keep):
        def blk(b, acc):
            for j in range(GROUP_UNROLL):
                g = b * GROUP_UNROLL + j
                eq = eq_ref[g]
                if prev_plane is not None:
                    eq = eq & (planes_ref[g, prev_plane] ^ ~prev_keep)
                    eq_ref[g] = eq
                ones = eq if cur_plane is None else eq & planes_ref[g, cur_plane]
                acc = acc + lax.population_count(ones)
            return acc
        acc = lax.fori_loop(0, n_blk, blk, jnp.zeros((8, BLOCK), jnp.int32))
        return jnp.sum(acc, axis=0, keepdims=True)

    def decide(bit, cnt_ones, thr_u, above):
        total = above + cnt_ones
        keep = jnp.where(total >= k_sel, -1, 0)
        return thr_u | (keep & bit), jnp.where(keep < 0, above, total), keep

    zeros = jnp.zeros((1, BLOCK), jnp.int32)
    state = decide(INT_MIN, sweep(0, None, None), zeros, zeros)

    def radix_step(it, state):
        thr_u, above, keep = state
        bit = lax.shift_left(jnp.int32(1), jnp.int32(31) - it)
        return decide(bit, sweep(it, it - 1, keep), thr_u, above)

    thr_u, above, keep = lax.fori_loop(1, 32, radix_step, state)
    cnt_thr = above + sweep(None, 31, keep)
    thr = thr_u ^ INT_MIN

    has_k = thr > INT_MIN
    thr = jnp.maximum(thr, INT_MIN + 1)

    jcut_ref[...] = jnp.full(jcut_ref.shape, 2 ** 31 - 1, jnp.int32)

    @pl.when(jnp.max(jnp.where(has_k, cnt_thr, 0)) > k_sel)
    def _():
        excess = jnp.where(has_k, count_ge(thr) - k_sel, 0)
        need = jnp.where(excess > 0, k_sel - count_ge(thr + 1), 1)
        rev_row = (seq - 1) - lax.broadcasted_iota(jnp.int32, (CNT_ROWS, BLOCK), 0)

        def tie_bisect(it, lo):
            cand = lo + lax.shift_left(jnp.int32(1), jnp.int32(seq.bit_length() - 2) - it)
            cnt = count_rows(lambda keys, off: (keys == thr) & (rev_row - off >= cand))
            return jnp.where(cnt >= need, cand, lo)

        rev = lax.fori_loop(0, seq.bit_length() - 1, tie_bisect, jnp.zeros((1, BLOCK), jnp.int32))
        jcut = jnp.where(excess > 0, (seq - 1) - rev, 2 ** 31 - 1)
        jcut_ref[...] = jnp.broadcast_to(jcut, jcut_ref.shape)

    jcut = jcut_ref[0:1, :]

    def emit(c, carry):
        off = pl.multiple_of(c * SEL_ROWS, SEL_ROWS)
        keys = keys_ref[pl.ds(off, SEL_ROWS), :]
        bar = jnp.where(off + k_row > jcut, thr + 1, thr)
        add = jnp.where(keys >= bar, 0.0, NEG)
        sel_ref[:, pl.ds(off, SEL_ROWS)] = jnp.concatenate(
            [add[j * BLOCK:(j + 1) * BLOCK].T for j in range(SEL_ROWS // BLOCK)], axis=1).astype(BF16)
        return carry

    _chunk_loop(emit, n_sel, SCORE_UNROLL)

    def fill(c, carry):
        off = pl.multiple_of(c * SEL_ROWS, SEL_ROWS)
        sel_ref[:, pl.ds(off, SEL_ROWS)] = jnp.full((BLOCK, SEL_ROWS), NEG, BF16)
        return carry

    lax.fori_loop(n_sel, seq // SEL_ROWS, fill, 0)


def _a_select(pf, batch, seq):
    t = pf.shape[0]
    nq = seq // BLOCK
    k_sel = min(TOPK_MAX, seq // 4)
    n_groups = -(-(seq // PLANE_ROWS) // GROUP_UNROLL) * GROUP_UNROLL
    nlane = IDX_HEADS * IDX_DIM
    r = np.arange(nlane)
    ph = np.zeros((nlane, IDX_HEADS * LANES), np.float32)
    plo = np.zeros((nlane, IDX_HEADS * LANES), np.float32)
    ph[r, (r // IDX_DIM) * LANES + r % IDX_DIM] = 1
    ph[r, (r // IDX_DIM) * LANES + 2 * IDX_DIM + r % IDX_DIM] = 1
    plo[r, (r // IDX_DIM) * LANES + IDX_DIM + r % IDX_DIM] = 1
    d = np.arange(IDX_DIM)
    pkh = np.zeros((LANES, LANES), np.float32)
    pkl = np.zeros((LANES, LANES), np.float32)
    pkh[d, d] = 1
    pkh[d, IDX_DIM + d] = 1
    pkl[d, 2 * IDX_DIM + d] = 1
    full = lambda shape: pl.BlockSpec(shape, lambda b, i: (0,) * len(shape))
    return pl.pallas_call(
        functools.partial(_a_select_kernel, seq=seq, k_sel=k_sel),
        grid=(batch, nq),
        in_specs=[pl.BlockSpec((BLOCK, 512), lambda b, i: (b * nq + i, 0)),
                  pl.BlockSpec((seq, LANES), lambda b, i: (b, nlane // LANES)),
                  full(ph.shape), full(plo.shape), full(pkh.shape), full(pkl.shape)],
        out_specs=pl.BlockSpec((BLOCK, seq), lambda b, i: (b * nq + i, 0)),
        out_shape=jax.ShapeDtypeStruct((t, seq), BF16),
        scratch_shapes=[pltpu.VMEM((seq, LANES), BF16), pltpu.VMEM((seq, LANES), jnp.int32),
                        pltpu.VMEM((n_groups, 32, 8, LANES), jnp.int32),
                        pltpu.VMEM((n_groups, 8, LANES), jnp.int32),
                        pltpu.VMEM((IDX_HEADS // 2, LANES, 2 * LANES), BF16),
                        pltpu.VMEM((8, LANES), jnp.int32)],
        compiler_params=_cparams(("parallel", "arbitrary")),
        name="a_select",
    )(pf, pf, jnp.asarray(ph, BF16), jnp.asarray(plo, BF16), jnp.asarray(pkh, BF16),
      jnp.asarray(pkl, BF16))


BAND_RBLK = 4


def _band_kernel(*refs, has_sink, want_lse, kv_div, rblk):
    if has_sink:
        sink_ref, refs = refs[0], refs[1:]
    q_ref, hm_ref, kp_ref, kc_ref, vp_ref, vc_ref, bias_ref, o_ref = refs[:8]
    first = pl.program_id(1) == 0
    lane = lax.broadcasted_iota(jnp.int32, (BLOCK, LANES), 1)
    col = lax.broadcasted_iota(jnp.int32, (BLOCK, 2 * BLOCK), 1)
    no_prev = jnp.logical_and(first, col < BLOCK)
    for p in range(BRANCH_WIDTH // LANES):
        kcol = (p // kv_div) * LANES
        kk = jnp.concatenate([kp_ref[:, kcol:kcol + LANES], kc_ref[:, kcol:kcol + LANES]], axis=0)
        vv = jnp.concatenate([vp_ref[:, kcol:kcol + LANES], vc_ref[:, kcol:kcol + LANES]], axis=0)
        for i in range(rblk):
            rows = slice(i * BLOCK, (i + 1) * BLOCK)
            q = q_ref[rows, p * LANES:(p + 1) * LANES]
            kblk = kk[i * BLOCK:(i + 2) * BLOCK]
            vblk = vv[i * BLOCK:(i + 2) * BLOCK]
            outs, lses = [], []
            for e in range(2):
                s = lax.dot_general(q * hm_ref[e], kblk, _CONTRACT_LANES, preferred_element_type=F32)
                s = s + bias_ref[2 * p + e]
                if i == 0:
                    s = jnp.where(no_prev, NEG, s)
                m = jnp.max(s, axis=1, keepdims=True)
                if has_sink:
                    sk = sink_ref[2 * p + e]
                    m = jnp.maximum(m, sk)
                pr = jnp.exp(s - m)
                den = jnp.sum(pr, axis=1, keepdims=True)
                if has_sink:
                    den = den + jnp.exp(sk - m)
                outs.append(jnp.dot(pr.astype(BF16), vblk, preferred_element_type=F32) / den)
                lses.append(m + jnp.log(den))
            o_ref[rows, p * LANES:(p + 1) * LANES] = jnp.where(lane < HEAD_DIM, outs[0], outs[1]).astype(o_ref.dtype)
            if want_lse:
                refs[8][rows, p * LANES:(p + 1) * LANES] = jnp.where(lane < HEAD_DIM, lses[0], lses[1])


def _band_attn(q_arr, k_arr, v_arr, bias, nsub, sub_len, qcb, kcb, vcb, kv_w, sinks=None, want_lse=False):
    t = q_arr.shape[0]
    rblk = min(BAND_RBLK, sub_len // BLOCK)
    step_rows = rblk * BLOCK
    nbs = sub_len // step_rows
    kv_div = BRANCH_WIDTH // kv_w

    def cur(u, n):
        return u * nbs + n

    def prev(u, n):
        return jnp.where(n == 0, u * nbs * rblk, (u * nbs + n) * rblk - 1)

    in_specs = [
        pl.BlockSpec((step_rows, BRANCH_WIDTH), lambda u, n: (cur(u, n), qcb)),
        pl.BlockSpec((2, 1, LANES), lambda u, n: (0, 0, 0)),
        pl.BlockSpec((BLOCK, kv_w), lambda u, n: (prev(u, n), kcb)),
        pl.BlockSpec((step_rows, kv_w), lambda u, n: (cur(u, n), kcb)),
        pl.BlockSpec((BLOCK, kv_w), lambda u, n: (prev(u, n), vcb)),
        pl.BlockSpec((step_rows, kv_w), lambda u, n: (cur(u, n), vcb)),
        pl.BlockSpec(bias.shape, lambda u, n: (0, 0, 0)),
    ]
    args = [q_arr, _pair_masks(), k_arr, k_arr, v_arr, v_arr, bias]
    if sinks is not None:
        in_specs = [pl.BlockSpec(memory_space=pltpu.SMEM)] + in_specs
        args = [sinks] + args
    o_spec = pl.BlockSpec((step_rows, BRANCH_WIDTH), lambda u, n: (cur(u, n), 0))
    out_specs = [o_spec]
    out_shape = [jax.ShapeDtypeStruct((t, BRANCH_WIDTH), BF16)]
    if want_lse:
        out_specs.append(o_spec)
        out_shape.append(jax.ShapeDtypeStruct((t, BRANCH_WIDTH), F32))
    return pl.pallas_call(
        functools.partial(_band_kernel, has_sink=sinks is not None, want_lse=want_lse, kv_div=kv_div, rblk=rblk),
        grid=(nsub, nbs),
        in_specs=in_specs,
        out_specs=out_specs,
        out_shape=out_shape,
        compiler_params=_cparams(("parallel", "arbitrary")),
        name="band_attn",
    )(*args)


def _toeplitz(w, rows, cols):
    length = rows + cols - 1
    u = jnp.pad(w[..., ::-1], [(0, 0)] * (w.ndim - 1) + [(0, 1)])
    flat = jnp.broadcast_to(u[..., None, :], w.shape[:-1] + (rows, length + 1)).reshape(w.shape[:-1] + (-1,))
    skew = flat[..., :rows * length].reshape(w.shape[:-1] + (rows, length))
    return skew[..., rows - 1:rows - 1 + cols]


def _band_bias(table, step, max_dist):
    rel = BLOCK + np.arange(BLOCK)[:, None] - np.arange(2 * BLOCK)[None, :]
    rel_vec = np.arange(BLOCK - (2 * BLOCK - 1), 2 * BLOCK)
    bias = _toeplitz(table[_t5_bucket(jnp.asarray(rel_vec * step))].T.astype(F32), BLOCK, 2 * BLOCK)
    ok = (rel >= 0) & (rel <= max_dist)
    return jnp.where(jnp.asarray(ok)[None], bias, NEG)


def _merge_kernel(x_ref, ya_ref, yb0_ref, yb1_ref, yb2_ref, l0_ref, l1_ref, l2_ref, yc_ref, yd_ref,
                  za_ref, zb_ref, zc_ref, zd_ref, ga_ref, gb_ref, gc_ref, gd_ref, wb_ref, wo_ref, *rest):
    if len(rest) == 7:
        ng_ref, o_ref, xn_ref, y1_ref, y2_ref, s1_ref, s2_ref = rest
    else:
        ng_ref = xn_ref = None
        o_ref, y1_ref, y2_ref, s1_ref, s2_ref = rest
    ncol = BRANCH_WIDTH // LANES
    for src, dst in ((yb1_ref, y1_ref), (yb2_ref, y2_ref), (l1_ref, s1_ref), (l2_ref, s2_ref)):
        dil = src.shape[1]
        for r in range(dil):
            blk = src[0, r].astype(F32)
            for c in range(ncol):
                dst[c, pl.ds(r, src.shape[2], stride=dil), :] = blk[:, c * LANES:(c + 1) * LANES]
    wide = lambda ref: jnp.concatenate([ref[c] for c in range(ncol)], axis=1)
    l0, l1, l2 = l0_ref[0, 0], wide(s1_ref), wide(s2_ref)
    mx = jnp.maximum(jnp.maximum(l0, l1), l2)
    w0, w1, w2 = jnp.exp(l0 - mx), jnp.exp(l1 - mx), jnp.exp(l2 - mx)
    yb = (w0 * yb0_ref[0, 0].astype(F32) + w1 * wide(y1_ref) + w2 * wide(y2_ref)) / (w0 + w1 + w2)
    ys = (ya_ref[...].astype(F32), yb, yc_ref[...].astype(F32), yd_ref[...].astype(F32))
    zs = (za_ref, zb_ref, zc_ref, zd_ref)
    gs = (ga_ref, gb_ref, gc_ref, gd_ref)
    merged = jnp.zeros(o_ref.shape, F32)
    for n in range(N_BRANCH):
        z = zs[n][...].astype(F32)
        u = (ys[n] * (z * jax.nn.sigmoid(z))).astype(BF16)
        proj = jnp.dot(u, wb_ref[n], preferred_element_type=F32)
        merged = merged + jax.nn.sigmoid(gs[n][...].astype(F32)) * proj
    x_new = x_ref[...] + jnp.dot(merged.astype(BF16), wo_ref[...], preferred_element_type=F32)
    o_ref[...] = x_new
    if xn_ref is not None:
        xn_ref[...] = _rms_bf16(x_new, ng_ref[...])


def _merge(x2, ya, yb, lb, yc, yd, pp, wb, wo, batch, seq, next_gain=None, tm=512):
    t = x2.shape[0]
    bw = BRANCH_WIDTH
    nsb = seq // tm
    row = lambda w, c: pl.BlockSpec((tm, w), lambda i: (i, c))

    def sub(dil):
        return pl.BlockSpec((1, dil, tm // dil, bw), lambda i: (i // nsb, 0, i % nsb, 0))

    dils = [dil for _, dil in B_GROUPS]
    yb = [a.reshape(batch, dil, seq // dil, bw) for a, dil in zip(yb, dils)]
    lb = [a.reshape(batch, dil, seq // dil, bw) for a, dil in zip(lb, dils)]
    in_specs = ([row(D_MODEL, 0), row(bw, 0)] + [sub(dil) for dil in dils] * 2 + [row(bw, 0), row(bw, 0)]
                + [row(bw, PP_AZ // bw), row(bw, PP_BZ // bw), row(bw, PP_CZ // bw), row(bw, PP_DZ // bw)]
                + [row(D_MODEL, PP_G // D_MODEL + n) for n in range(N_BRANCH)]
                + [pl.BlockSpec((N_BRANCH, bw, D_MODEL), lambda i: (0, 0, 0)),
                   pl.BlockSpec((D_MODEL, D_MODEL), lambda i: (0, 0))])
    args = [x2, ya, yb[0], yb[1], yb[2], lb[0], lb[1], lb[2], yc, yd, pp, pp, pp, pp, pp, pp, pp, pp, wb, wo]
    out_specs = [row(D_MODEL, 0)]
    out_shape = [jax.ShapeDtypeStruct((t, D_MODEL), F32)]
    if next_gain is not None:
        in_specs.append(pl.BlockSpec((1, D_MODEL), lambda i: (0, 0)))
        args.append(next_gain.reshape(1, D_MODEL))
        out_specs.append(row(D_MODEL, 0))
        out_shape.append(jax.ShapeDtypeStruct((t, D_MODEL), BF16))
    return pl.pallas_call(
        _merge_kernel,
        grid=(t // tm,),
        in_specs=in_specs,
        out_specs=out_specs,
        out_shape=out_shape,
        scratch_shapes=[pltpu.VMEM((bw // LANES, tm, LANES), F32)] * 4,
        compiler_params=_cparams(("parallel",)),
        name="merge",
    )(*args)


def _layer_weights(w_in, qk_gain_a, qk_gain_b, qk_gain_d):
    def cols(start, width):
        return w_in[:, :, start:start + width]

    n_bq = len(B_GROUPS) * B_HEADS * HEAD_DIM
    dk = cols(_D0 + 512, 128)
    dv = cols(_D0 + 640, 128)
    dup = lambda a: jnp.concatenate([a[:, :, :64], a[:, :, :64], a[:, :, 64:], a[:, :, 64:]], axis=-1)
    w_pn = jnp.concatenate([cols(_A0, 512), cols(_A0 + 512, 512), cols(_D0, 512), dup(dk), dup(dv)],
                           axis=-1).astype(BF16)
    w_pp = jnp.concatenate([cols(_G0, N_BRANCH * D_MODEL), cols(_A0 + 1024, 512), cols(_A0 + 1536, 512),
                            cols(_B0 + 3 * n_bq, 512), cols(_C0 + 416, 512), cols(_D0 + 768, 512)],
                           axis=-1).astype(BF16)
    w_bg = [jnp.concatenate([cols(_B0 + g * 512, 512), cols(_B0 + n_bq + g * 512, 512),
                             cols(_B0 + 2 * n_bq + g * 512, 512)], axis=-1).astype(BF16)
            for g in range(len(B_GROUPS))]
    depth = w_in.shape[0]
    zeros = lambda w: jnp.zeros((depth, D_MODEL, w), w_in.dtype)
    w_pf = jnp.concatenate([cols(_A0 + 2048, 296), zeros(216), cols(_C0, 416), zeros(96)], axis=-1).astype(BF16)

    scale = HEAD_DIM ** -0.5
    tile = lambda g, reps: jnp.tile(g, (1, reps))
    hg = jnp.concatenate([tile(qk_gain_a[:, 0] * scale, 8), tile(qk_gain_a[:, 1], 8),
                          tile(qk_gain_d[:, 0] * scale, 8), tile(qk_gain_d[:, 1], 4),
                          jnp.ones((depth, 256), F32)], axis=-1)
    flag = jnp.concatenate([jnp.ones((PN_DV,), F32), jnp.zeros((PN_W - PN_DV,), F32)])
    hg_b = jnp.concatenate([tile(qk_gain_b[:, 0] * scale, 8), tile(qk_gain_b[:, 1], 8),
                            jnp.ones((depth, 512), F32)], axis=-1)
    flag_b = jnp.concatenate([jnp.ones((2 * BRANCH_WIDTH,), F32), jnp.zeros((BRANCH_WIDTH,), F32)])
    return w_pn, w_pp, w_bg, w_pf, hg, flag, hg_b, flag_b


def _c_weights(qk_gain_c, w_q_b, w_kv_b, seq):
    src_q, src_k, gain_idx, head, pe_src, rope_j, rope_half = _c_layout_tables()
    take = lambda w, src: jnp.where(jnp.asarray(src >= 0), jnp.take(w, jnp.asarray(np.maximum(src, 0)), axis=-1), 0.0)
    wq = take(w_q_b, src_q).astype(BF16)
    wk = take(w_kv_b, src_k).astype(BF16)
    v_src = np.array([h * (C_NOPE + C_V) + C_NOPE + c for h in range(C_HEADS) for c in range(C_V)])
    wv = jnp.take(w_kv_b, jnp.asarray(v_src), axis=-1).astype(BF16)
    ppe = np.zeros((LANES, C_QK_W), np.float32)
    ln = np.nonzero(pe_src >= 0)[0]
    ppe[pe_src[ln], ln] = 1
    hp = head[:C_PAIR]
    grp = ((hp[:, None] == hp[None, :]) & (hp[:, None] >= 0)).astype(np.float32)
    g_take = lambda g: jnp.where(jnp.asarray(gain_idx >= 0), jnp.take(g, jnp.asarray(np.maximum(gain_idx, 0)), axis=-1), 0.0)
    gq = g_take(qk_gain_c[:, 0]) * (C_NOPE + C_ROPE) ** -0.5
    gk = g_take(qk_gain_c[:, 1])
    freq = ROPE_THETA ** (-jnp.arange(C_ROPE_HALF, dtype=F32) / C_ROPE_HALF)
    ang = jnp.arange(seq).astype(F32)[:, None] * freq[None, :]
    cos_j, sin_j = jnp.cos(ang), jnp.sin(ang)
    place = np.zeros((C_ROPE_HALF, C_PAIR), np.float32)
    lanes = np.nonzero(rope_j[:C_PAIR] >= 0)[0]
    place[rope_j[lanes], lanes] = 1
    is_rope = jnp.asarray(rope_j[:C_PAIR] >= 0)
    hi = lax.Precision.HIGHEST
    cos = jnp.where(is_rope, jnp.dot(cos_j, jnp.asarray(place), precision=hi), 1.0)
    sin = jnp.dot(sin_j, jnp.asarray(place), precision=hi)
    s1 = jnp.where(jnp.asarray(rope_half[:C_PAIR] == 0), -sin, 0.0)
    s2 = jnp.where(jnp.asarray(rope_half[:C_PAIR] == 1), sin, 0.0)
    return wq, wk, wv, jnp.asarray(ppe, BF16), jnp.asarray(grp, BF16), gq, gk, cos, s1, s2


def _a_bias_tiles(table):
    dist_vec = np.arange(-(BLOCK - 1), A_BIAS_TILES * BLOCK)
    vec = table[_t5_bucket(jnp.asarray(dist_vec))].T.astype(F32)
    windows = jnp.stack([vec[:, d * BLOCK:d * BLOCK + 2 * BLOCK - 1] for d in range(A_BIAS_TILES)], axis=1)
    return _toeplitz(windows, BLOCK, BLOCK)


def kernel(x, norm_gain, w_in, qk_gain_a, qk_gain_b, qk_gain_c, qk_gain_d, c_q_gain, c_kv_gain,
           w_q_b, w_kv_b, sinks, rel_bias, w_branch, w_out):
    batch, seq, d_model = x.shape
    depth = w_in.shape[0]
    t = batch * seq
    x2 = x.reshape(t, d_model)

    w_pn, w_pp, w_bg, w_pf, hg, flag, hg_b, flag_b = _layer_weights(w_in, qk_gain_a, qk_gain_b, qk_gain_d)
    wq, wk, wv, ppe, grp, gq, gk, cos, s1, s2 = _c_weights(qk_gain_c, w_q_b, w_kv_b, seq)
    wb = w_branch.astype(BF16)
    wo = w_out.astype(BF16)

    bias_a = _a_bias_tiles(rel_bias[:, :A_HEADS])
    bias_b = [_band_bias(rel_bias[:, A_HEADS + g * B_HEADS:A_HEADS + (g + 1) * B_HEADS], dil, window // dil)
              for g, (window, dil) in enumerate(B_GROUPS)]
    bias_d = _band_bias(rel_bias[:, N_BIAS_HEADS - D_HEADS:], 1, D_WINDOW - 1)

    xn = _norm(x2, norm_gain[0])
    for l in range(depth):
        pn = _proj(xn, w_pn[l], BF16, head_gain=hg[l], flag=flag)
        pp = _proj(xn, w_pp[l], BF16)
        pf = _proj(xn, w_pf[l], F32)

        sel = _a_select(pf, batch, seq)
        ya = _a_attn(pn, pp, sel, bias_a, batch, seq)

        yb, lb = [], []
        for g, (window, dil) in enumerate(B_GROUPS):
            bg = _proj(xn, w_bg[g][l], BF16, head_gain=hg_b[l], flag=flag_b, dil=dil, seq=seq)
            o, lse = _band_attn(bg, bg, bg, bias_b[g], batch * dil, seq // dil, 0, 1, 2, BRANCH_WIDTH,
                                want_lse=True)
            yb.append(o)
            lb.append(lse)

        qc, kc, vc = _c_prep(pf, c_q_gain[l].reshape(1, -1), c_kv_gain[l].reshape(1, -1), wq[l], wk[l], wv[l],
                             ppe, grp, gq[l].reshape(1, -1), gk[l].reshape(1, -1), cos, s1, s2, seq)
        yc = _c_attn(qc, kc, vc, batch, seq)

        (yd,) = _band_attn(pn, pn, pn, bias_d, batch, seq, PN_DQ // BRANCH_WIDTH, PN_DK // 256, PN_DV // 256,
                           256, sinks=sinks[l])

        if l + 1 < depth:
            x2, xn = _merge(x2, ya, yb, lb, yc, yd, pp, wb[l], wo[l], batch, seq, next_gain=norm_gain[l + 1])
        else:
            (x2,) = _merge(x2, ya, yb, lb, yc, yd, pp, wb[l], wo[l], batch, seq)
    return x2.reshape(batch, seq, d_model)
```

```python
import functools
import math

import numpy as np
import jax
import jax.numpy as jnp
from jax import lax
from jax.experimental import pallas as pl
from jax.experimental.pallas import tpu as pltpu

F32 = jnp.float32
BF16 = jnp.bfloat16

D_MODEL = 1024
BLOCK = 128
HEAD_DIM = 64
N_BRANCH = 4
BRANCH_WIDTH = 512
EPS = 1e-6
A_HEADS = 8
IDX_HEADS = 8
IDX_DIM = 32
TOPK_MAX = 256
B_GROUPS = ((128, 1), (512, 4), (2048, 16))
B_HEADS = 8
C_HEADS = 8
C_NOPE = 64
C_ROPE = 32
C_V = 64
C_Q_LORA = 256
C_KV_LORA = 128
ROPE_THETA = 10000.0
D_HEADS = 8
D_KV_HEADS = 2
D_WINDOW = 128
NUM_BUCKETS = 32
MAX_DISTANCE = 2048
N_BIAS_HEADS = A_HEADS + len(B_GROUPS) * B_HEADS + D_HEADS

LANES = 128
NEG = -1e30
LOG2E = math.log2(math.e)
INT_MIN = -(2 ** 31)

_A0 = 0
_B0 = 2344
_C0 = 7464
_D0 = 8392
_G0 = 9672

PN_AQ, PN_AK, PN_DQ, PN_DK, PN_DV, PN_W = 0, 512, 1024, 1536, 1792, 2048
PP_G, PP_AV, PP_AZ, PP_BZ, PP_CZ, PP_DZ, PP_W = 0, 4096, 4608, 5120, 5632, 6144, 6656
PF_W = 1024
BG_W = 3 * BRANCH_WIDTH

A_BIAS_TILES = MAX_DISTANCE // BLOCK + 2


def _cparams(sem, vmem_mb=48):
    return pltpu.CompilerParams(dimension_semantics=sem, vmem_limit_bytes=vmem_mb * 1024 * 1024)


def _t5_bucket(dist):
    max_exact = NUM_BUCKETS // 2
    d = jnp.maximum(dist, 0)
    logd = jnp.log(jnp.maximum(d, 1).astype(F32) / max_exact)
    large = max_exact + (logd / math.log(MAX_DISTANCE / max_exact) * (NUM_BUCKETS - max_exact)).astype(jnp.int32)
    return jnp.where(d < max_exact, d, jnp.minimum(large, NUM_BUCKETS - 1))


def _rms_bf16(x, gain):
    return (x * lax.rsqrt(jnp.mean(x * x, axis=-1, keepdims=True) + EPS) * gain).astype(BF16)


def _norm_kernel(x_ref, g_ref, o_ref):
    o_ref[...] = _rms_bf16(x_ref[...], g_ref[...])


def _norm(x2, gain, tm=1024):
    t, d = x2.shape
    return pl.pallas_call(
        _norm_kernel,
        grid=(t // tm,),
        in_specs=[pl.BlockSpec((tm, d), lambda i: (i, 0)), pl.BlockSpec((1, d), lambda i: (0, 0))],
        out_specs=pl.BlockSpec((tm, d), lambda i: (i, 0)),
        out_shape=jax.ShapeDtypeStruct((t, d), BF16),
        compiler_params=_cparams(("parallel",)),
        name="norm",
    )(x2, gain.reshape(1, d))


def _proj_kernel(xn_ref, w_ref, *rest, norm, dil):
    rest = list(rest)
    res_ref = rest.pop(-1) if dil is not None and dil > 1 else None
    if norm:
        hg_ref, flag_ref, bd_ref, o_ref = rest
    else:
        (o_ref,) = rest

    def emit(h):
        if dil is None:
            o_ref[...] = h.astype(o_ref.dtype)
        elif dil == 1:
            o_ref[0, 0] = h.astype(o_ref.dtype)
        else:
            sub = h.shape[0] // dil
            for c in range(h.shape[1] // LANES):
                res_ref[c] = h[:, c * LANES:(c + 1) * LANES]
            for r in range(dil):
                o_ref[0, r] = jnp.concatenate(
                    [res_ref[c, pl.ds(r, sub, stride=dil), :] for c in range(h.shape[1] // LANES)],
                    axis=1).astype(o_ref.dtype)

    h = jnp.dot(xn_ref[...], w_ref[...], preferred_element_type=F32)
    if norm:
        ss = jnp.dot((h * h).astype(BF16), bd_ref[...], preferred_element_type=F32)
        scale = lax.rsqrt(ss * (1.0 / HEAD_DIM) + EPS) * hg_ref[...]
        h = h * jnp.where(flag_ref[...] > 0, scale, 1.0)
    emit(h)


def _proj(xn, w, out_dtype, head_gain=None, flag=None, dil=None, seq=None, tm=1024, tn=512):
    t, d = xn.shape
    n = w.shape[1]
    norm = head_gain is not None
    scratch = []
    if dil is None:
        out_spec = pl.BlockSpec((tm, tn), lambda i, j: (i, j))
        out_shape = jax.ShapeDtypeStruct((t, n), out_dtype)
        kdil = None
    else:
        nsb = seq // tm
        out_spec = pl.BlockSpec((1, dil, tm // dil, tn), lambda i, j: (i // nsb, 0, i % nsb, j))
        out_shape = jax.ShapeDtypeStruct((t // seq, dil, seq // dil, n), out_dtype)
        kdil = dil
        if dil > 1:
            scratch.append(pltpu.VMEM((tn // LANES, tm, LANES), F32))
    in_specs = [pl.BlockSpec((tm, d), lambda i, j: (i, 0)),
                pl.BlockSpec((d, tn), lambda i, j: (0, j))]
    args = [xn, w]
    if norm:
        lane = np.arange(tn)
        bd = jnp.asarray((lane[:, None] // HEAD_DIM == lane[None, :] // HEAD_DIM), BF16)
        in_specs += [pl.BlockSpec((1, tn), lambda i, j: (0, j)),
                     pl.BlockSpec((1, tn), lambda i, j: (0, j)),
                     pl.BlockSpec((tn, tn), lambda i, j: (0, 0))]
        args += [head_gain.reshape(1, n), flag.reshape(1, n), bd]
    out = pl.pallas_call(
        functools.partial(_proj_kernel, norm=norm, dil=kdil),
        grid=(t // tm, n // tn),
        in_specs=in_specs,
        out_specs=out_spec,
        out_shape=out_shape,
        scratch_shapes=scratch,
        compiler_params=_cparams(("parallel", "parallel")),
        name=("proj_norm" if norm else "proj") + ("" if dil is None else "_dil%d" % dil),
    )(*args)
    return out.reshape(t, n)


C_PAIR = 256
C_QK_W = (C_HEADS // 2) * C_PAIR
C_ROPE_HALF = C_ROPE // 2


def _c_lane(h, c):
    base = (h // 2) * C_PAIR
    e = h % 2
    if c < C_NOPE:
        return base + e * C_NOPE + c
    return base + 2 * C_NOPE + e * C_ROPE + (c - C_NOPE)


def _c_layout_tables():
    src_q = np.full(C_QK_W, -1, np.int64)
    src_k = np.full(C_QK_W, -1, np.int64)
    gain_idx = np.full(C_QK_W, -1, np.int64)
    head = np.full(C_QK_W, -1, np.int64)
    pe_src = np.full(C_QK_W, -1, np.int64)
    rope_j = np.full(C_QK_W, -1, np.int64)
    rope_half = np.zeros(C_QK_W, np.int64)
    for h in range(C_HEADS):
        for c in range(C_NOPE + C_ROPE):
            ln = _c_lane(h, c)
            src_q[ln] = h * (C_NOPE + C_ROPE) + c
            gain_idx[ln] = c
            head[ln] = h
            if c < C_NOPE:
                src_k[ln] = h * (C_NOPE + C_V) + c
            else:
                r = c - C_NOPE
                pe_src[ln] = r
                rope_j[ln] = r % C_ROPE_HALF
                rope_half[ln] = r // C_ROPE_HALF
    return src_q, src_k, gain_idx, head, pe_src, rope_j, rope_half


def _c_prep_kernel(pf_ref, cqg_ref, ckvg_ref, wq_ref, wk_ref, wv_ref, ppe_ref, grp_ref,
                   gq_ref, gk_ref, cos_ref, s1_ref, s2_ref, q_ref, k_ref, v_ref):
    blk = pf_ref[...]
    cq = blk[:, :C_Q_LORA]
    ckv = blk[:, C_Q_LORA:C_Q_LORA + C_KV_LORA]
    pe = blk[:, C_Q_LORA + C_KV_LORA:]

    def rms(v, g):
        return v * lax.rsqrt(jnp.mean(v * v, axis=-1, keepdims=True) + EPS) * g

    def head_norm_rope(raw, gain):
        sq = (raw * raw).astype(BF16)
        ss = jnp.concatenate([jnp.dot(sq[:, p * C_PAIR:(p + 1) * C_PAIR], grp_ref[...], preferred_element_type=F32)
                              for p in range(C_QK_W // C_PAIR)], axis=1)
        y = raw * lax.rsqrt(ss * (1.0 / (C_NOPE + C_ROPE)) + EPS) * gain
        up = pltpu.roll(y, C_QK_W - C_ROPE_HALF, 1)
        dn = pltpu.roll(y, C_ROPE_HALF, 1)
        wide = lambda ref: jnp.concatenate([ref[...]] * (C_QK_W // C_PAIR), axis=1)
        return y * wide(cos_ref) + up * wide(s1_ref) + dn * wide(s2_ref)

    cqn = rms(cq, cqg_ref[...]).astype(BF16)
    q_raw = jnp.dot(cqn, wq_ref[...], preferred_element_type=F32)
    q_ref[...] = head_norm_rope(q_raw, gq_ref[...]).astype(BF16)

    ckvn = rms(ckv, ckvg_ref[...]).astype(BF16)
    pe_hi = pe.astype(BF16)
    pe_lo = (pe - pe_hi.astype(F32)).astype(BF16)
    k_raw = (jnp.dot(ckvn, wk_ref[...], preferred_element_type=F32)
             + jnp.dot(pe_hi, ppe_ref[...], preferred_element_type=F32)
             + jnp.dot(pe_lo, ppe_ref[...], preferred_element_type=F32))
    k_ref[...] = head_norm_rope(k_raw, gk_ref[...]).astype(BF16)
    v_ref[...] = jnp.dot(ckvn, wv_ref[...], preferred_element_type=F32).astype(BF16)


def _c_prep(pf, cqg, ckvg, wq, wk, wv, ppe, grp, gq, gk, cos, s1, s2, seq, tm=512):
    t = pf.shape[0]
    nsb = seq // tm
    full = lambda shape: pl.BlockSpec(shape, lambda i: (0,) * len(shape))
    tab = pl.BlockSpec((tm, C_PAIR), lambda i: (i % nsb, 0))
    return pl.pallas_call(
        _c_prep_kernel,
        grid=(t // tm,),
        in_specs=[pl.BlockSpec((tm, 512), lambda i: (i, 1)),
                  full((1, C_Q_LORA)), full((1, C_KV_LORA)),
                  full((C_Q_LORA, C_QK_W)), full((C_KV_LORA, C_QK_W)), full((C_KV_LORA, BRANCH_WIDTH)),
                  full((LANES, C_QK_W)), full((C_PAIR, C_PAIR)),
                  full((1, C_QK_W)), full((1, C_QK_W)), tab, tab, tab],
        out_specs=[pl.BlockSpec((tm, C_QK_W), lambda i: (i, 0)),
                   pl.BlockSpec((tm, C_QK_W), lambda i: (i, 0)),
                   pl.BlockSpec((tm, BRANCH_WIDTH), lambda i: (i, 0))],
        out_shape=[jax.ShapeDtypeStruct((t, C_QK_W), BF16),
                   jax.ShapeDtypeStruct((t, C_QK_W), BF16),
                   jax.ShapeDtypeStruct((t, BRANCH_WIDTH), BF16)],
        compiler_params=_cparams(("parallel",)),
        name="c_prep",
    )(pf, cqg, ckvg, wq, wk, wv, ppe, grp, gq, gk, cos, s1, s2)


def _flash_update(e, s, v_ones, m_ref, acc_ref, rows=slice(None)):
    m_prev = m_ref[e, rows]
    m_new = jnp.maximum(m_prev, jnp.max(s, axis=1, keepdims=True))
    alpha = jnp.exp2(m_prev - m_new)
    p = jnp.exp2(s - jnp.concatenate([m_new] * (s.shape[1] // LANES), axis=1))
    acc_ref[e, rows] = (jnp.concatenate([alpha, alpha], axis=1) * acc_ref[e, rows]
                        + jnp.dot(p.astype(BF16), v_ones, preferred_element_type=F32))
    m_ref[e, rows] = m_new


def _flash_init(m_ref, acc_ref):
    m_ref[...] = jnp.full(m_ref.shape, NEG, F32)
    acc_ref[...] = jnp.zeros(acc_ref.shape, F32)


def _flash_finish(o_ref, acc_ref):
    lane = lax.broadcasted_iota(jnp.int32, o_ref.shape, 1)
    o0 = acc_ref[0, :, :LANES] / acc_ref[0, :, LANES:]
    o1 = acc_ref[1, :, :LANES] / acc_ref[1, :, LANES:]
    o_ref[...] = jnp.where(lane < HEAD_DIM, o0, o1).astype(o_ref.dtype)


def _with_ones(v_chunk):
    return jnp.concatenate([v_chunk, jnp.ones(v_chunk.shape, v_chunk.dtype)], axis=1)


_CONTRACT_LANES = (((1,), (1,)), ((), ()))


def _head_masks(width, ranges0, ranges1):
    m = np.zeros((2, 1, width), np.float32)
    for e, ranges in enumerate((ranges0, ranges1)):
        for lo, hi in ranges:
            m[e, 0, lo:hi] = 1
    return jnp.asarray(m, BF16)


def _pair_masks():
    return _head_masks(LANES, [(0, HEAD_DIM)], [(HEAD_DIM, LANES)])


def _transpose_keys(k_ref, kt_ref, rows=512):
    def body(c, carry):
        off = pl.multiple_of(c * rows, rows)
        kt_ref[:, pl.ds(off, rows)] = k_ref[pl.ds(off, rows), :].astype(F32).T.astype(kt_ref.dtype)
        return carry
    lax.fori_loop(0, k_ref.shape[0] // rows, body, 0)


def _chunk_loop(chunk, n, unroll, carry=None, slots=None):
    start = 0
    width = unroll
    while width >= 1:
        def body(g, carry, width=width, start=start):
            for j in range(width):
                carry = chunk(start + g * width + j, carry) if slots is None else \
                    chunk(start + g * width + j, carry, j)
            return carry
        groups = (n - start) // width
        carry = lax.fori_loop(0, groups, body, carry)
        start = start + groups * width
        width //= 2
    return carry


def _c_attn_kernel(q_ref, hm_ref, k_ref, v_ref, o_ref, kt_ref, m_ref, acc_ref, *, tq, tk, unroll):
    qi = pl.program_id(2)

    @pl.when(qi == 0)
    def _():
        _transpose_keys(k_ref, kt_ref)

    q = q_ref[...]
    qs = (q * hm_ref[0], q * hm_ref[1])
    _flash_init(m_ref, acc_ref)
    n_full = (qi * tq) // tk

    def chunk(c, row0=0, diagonal=False):
        off = pl.multiple_of(c * tk, tk)
        kc = kt_ref[:, pl.ds(off, tk)]
        vc = _with_ones(v_ref[pl.ds(off, tk), :])
        for e in range(2):
            if not diagonal:
                rows = slice(row0, tq)
                _flash_update(e, jnp.dot(qs[e][rows], kc, preferred_element_type=F32), vc, m_ref, acc_ref, rows)
                continue
            rows = slice(row0, row0 + tk)
            s = jnp.dot(qs[e][rows], kc, preferred_element_type=F32)
            s = jnp.where(lax.broadcasted_iota(jnp.int32, s.shape, 1) <= lax.broadcasted_iota(jnp.int32, s.shape, 0),
                          s, NEG)
            _flash_update(e, s, vc, m_ref, acc_ref, rows)
            if row0 + tk < tq:
                rows = slice(row0 + tk, tq)
                _flash_update(e, jnp.dot(qs[e][rows], kc, preferred_element_type=F32), vc, m_ref, acc_ref, rows)

    _chunk_loop(lambda c, _: chunk(c), n_full, unroll)
    for j in range(tq // tk):
        chunk(n_full + j, row0=j * tk, diagonal=True)
    _flash_finish(o_ref, acc_ref)


def _c_attn(qc, kc, vc, batch, seq, tq=1024, tk=512, unroll=4):
    t = qc.shape[0]
    nq = seq // tq
    npair = C_HEADS // 2
    return pl.pallas_call(
        functools.partial(_c_attn_kernel, tq=tq, tk=tk, unroll=unroll),
        grid=(batch, npair, nq),
        in_specs=[pl.BlockSpec((tq, C_PAIR), lambda b, p, i: (b * nq + i, p)),
                  pl.BlockSpec((2, 1, C_PAIR), lambda b, p, i: (0, 0, 0)),
                  pl.BlockSpec((seq, C_PAIR), lambda b, p, i: (b, p)),
                  pl.BlockSpec((seq, LANES), lambda b, p, i: (b, p))],
        out_specs=pl.BlockSpec((tq, LANES), lambda b, p, i: (b * nq + i, p)),
        out_shape=jax.ShapeDtypeStruct((t, BRANCH_WIDTH), BF16),
        scratch_shapes=[pltpu.VMEM((C_PAIR, seq), BF16),
                        pltpu.VMEM((2, tq, LANES), F32), pltpu.VMEM((2, tq, 2 * LANES), F32)],
        compiler_params=_cparams(("parallel", "parallel", "arbitrary")),
        name="c_attn",
    )(qc, _head_masks(C_PAIR, [(0, C_NOPE), (2 * C_NOPE, 2 * C_NOPE + C_ROPE)],
                      [(C_NOPE, 2 * C_NOPE), (2 * C_NOPE + C_ROPE, 2 * C_NOPE + 2 * C_ROPE)]), kc, vc)


def _a_attn_kernel(q_ref, hm_ref, k_ref, v_ref, sel_ref, bias_ref, o_ref, kt_ref, m_ref, acc_ref, msk_ref,
                   *, tq, tk, unroll):
    qi = pl.program_id(2)

    @pl.when(qi == 0)
    def _():
        _transpose_keys(k_ref, kt_ref)

    q = q_ref[...]
    qs = (q * hm_ref[0], q * hm_ref[1])
    _flash_init(m_ref, acc_ref)
    n_chunks = ((qi + 1) * tq + tk - 1) // tk

    def chunk(c, carry, slot):
        off = pl.multiple_of(c * tk, tk)
        kc = kt_ref[:, pl.ds(off, tk)]
        vc = _with_ones(v_ref[pl.ds(off, tk), :])
        msk_ref[slot] = sel_ref[:, pl.ds(off, tk)].astype(F32)
        for e in range(2):
            s = jnp.dot(qs[e], kc, preferred_element_type=F32)
            rows = []
            for i in range(tq // BLOCK):
                tiles = []
                for j in range(tk // BLOCK):
                    d = (qi * (tq // BLOCK) + i) - (c * (tk // BLOCK) + j)
                    d = jnp.clip(d, 0, A_BIAS_TILES - 1)
                    tiles.append(bias_ref[e, d])
                rows.append(jnp.concatenate(tiles, axis=1))
            s = s + jnp.concatenate(rows, axis=0) + msk_ref[slot]
            _flash_update(e, s, vc, m_ref, acc_ref)
        return carry

    _chunk_loop(chunk, n_chunks, unroll, slots=unroll)
    _flash_finish(o_ref, acc_ref)


def _a_attn(pn, pp, sel, bias_tiles, batch, seq, tq=512, tk=512, unroll=4):
    t = pn.shape[0]
    nq = seq // tq
    npair = A_HEADS // 2
    qcol, kcol, vcol = PN_AQ // LANES, PN_AK // LANES, PP_AV // LANES
    return pl.pallas_call(
        functools.partial(_a_attn_kernel, tq=tq, tk=tk, unroll=unroll),
        grid=(batch, npair, nq),
        in_specs=[pl.BlockSpec((tq, LANES), lambda b, p, i: (b * nq + i, qcol + p)),
                  pl.BlockSpec((2, 1, LANES), lambda b, p, i: (0, 0, 0)),
                  pl.BlockSpec((seq, LANES), lambda b, p, i: (b, kcol + p)),
                  pl.BlockSpec((seq, LANES), lambda b, p, i: (b, vcol + p)),
                  pl.BlockSpec((tq, seq), lambda b, p, i: (b * nq + i, 0)),
                  pl.BlockSpec((2, A_BIAS_TILES, BLOCK, BLOCK), lambda b, p, i: (p, 0, 0, 0))],
        out_specs=pl.BlockSpec((tq, LANES), lambda b, p, i: (b * nq + i, p)),
        out_shape=jax.ShapeDtypeStruct((t, BRANCH_WIDTH), BF16),
        scratch_shapes=[pltpu.VMEM((LANES, seq), BF16),
                        pltpu.VMEM((2, tq, LANES), F32), pltpu.VMEM((2, tq, 2 * LANES), F32),
                        pltpu.VMEM((unroll, tq, tk), F32)],
        compiler_params=_cparams(("parallel", "parallel", "arbitrary")),
        name="a_attn",
    )(pn, _pair_masks(), pn, pp, sel, bias_tiles)


def _sortable_key(score):
    bits = pltpu.bitcast(score, jnp.int32)
    return bits ^ ((bits >> 31) & jnp.int32(0x7FFFFFFF))


def _bit_transpose32(words):
    a = list(words)
    j, m = 16, 0x0000FFFF
    while j:
        for k in range(32):
            if k & j:
                continue
            t = (a[k] ^ lax.shift_right_logical(a[k + j], jnp.int32(j))) & m
            a[k] = a[k] ^ t
            a[k + j] = a[k + j] ^ lax.shift_left(t, jnp.int32(j))
        j >>= 1
        m ^= (m << j) & 0xFFFFFFFF
    return a


SEL_ROWS = 512
CNT_ROWS = SEL_ROWS
PLANE_ROWS = 256
GROUP_UNROLL = 8
SCORE_UNROLL = 2
COUNT_UNROLL = 1


def _a_select_kernel(qblk_ref, kblk_ref, ph_ref, pl_ref, pkh_ref, pkl_ref, sel_ref,
                     ikx_ref, keys_ref, planes_ref, eq_ref, iqt_ref, jcut_ref, *, seq, k_sel):
    qi = pl.program_id(1)
    n_sel = (qi * BLOCK + BLOCK + SEL_ROWS - 1) // SEL_ROWS
    n_cnt = n_sel

    @pl.when(qi == 0)
    def _():
        def prep(c, carry):
            off = pl.multiple_of(c * 512, 512)
            kv = kblk_ref[pl.ds(off, 512), :]
            hi = kv.astype(BF16)
            lo = (kv - hi.astype(F32)).astype(BF16)
            ikx_ref[pl.ds(off, 512), :] = (
                jnp.dot(hi, pkh_ref[...], preferred_element_type=F32)
                + jnp.dot(lo, pkl_ref[...], preferred_element_type=F32)).astype(BF16)
            return carry
        lax.fori_loop(0, seq // 512, prep, 0)

        def clear(g, carry):
            planes_ref[g] = jnp.zeros(planes_ref.shape[1:], jnp.int32)
            return carry
        lax.fori_loop(0, planes_ref.shape[0], clear, 0)

    qb = qblk_ref[...]
    iq = qb[:, :IDX_HEADS * IDX_DIM]
    iq_hi = iq.astype(BF16)
    iq_lo = (iq - iq_hi.astype(F32)).astype(BF16)
    iqx = (jnp.dot(iq_hi, ph_ref[...], preferred_element_type=F32)
           + jnp.dot(iq_lo, pl_ref[...], preferred_element_type=F32))
    for h in range(IDX_HEADS):
        iqt_ref[h // 2, :, (h % 2) * LANES:(h % 2 + 1) * LANES] = iqx[:, h * LANES:(h + 1) * LANES].T.astype(BF16)
    iw_t = qb[:, IDX_HEADS * IDX_DIM:IDX_HEADS * IDX_DIM + LANES].T
    iw_rows = [iw_t[IDX_DIM + h:IDX_DIM + h + 1, :] for h in range(IDX_HEADS)]
    q_pos = qi * BLOCK + lax.broadcasted_iota(jnp.int32, (SEL_ROWS, BLOCK), 1)
    k_row = lax.broadcasted_iota(jnp.int32, (SEL_ROWS, BLOCK), 0)

    def score_chunk(c, masked):
        off = pl.multiple_of(c * SEL_ROWS, SEL_ROWS)
        kx = ikx_ref[pl.ds(off, SEL_ROWS), :]
        sc = jnp.zeros((SEL_ROWS, BLOCK), F32)
        for hp in range(IDX_HEADS // 2):
            xx = jnp.dot(kx, iqt_ref[hp], preferred_element_type=F32)
            sc = (sc + jnp.maximum(xx[:, :LANES], 0.0) * iw_rows[2 * hp]
                  + jnp.maximum(xx[:, LANES:], 0.0) * iw_rows[2 * hp + 1])
        key = _sortable_key(sc + 0.0)
        if masked:
            key = jnp.where(off + k_row <= q_pos, key, INT_MIN)
        keys_ref[pl.ds(off, SEL_ROWS), :] = key
        u = key ^ INT_MIN
        for g in range(SEL_ROWS // PLANE_ROWS):
            words = [u[g * PLANE_ROWS + 8 * j:g * PLANE_ROWS + 8 * j + 8] for j in range(32)]
            for i, plane in enumerate(_bit_transpose32(words)):
                planes_ref[c * (SEL_ROWS // PLANE_ROWS) + g, i] = plane

    _chunk_loop(lambda c, _: score_chunk(c, False), n_sel - 1, SCORE_UNROLL)
    score_chunk(n_sel - 1, True)

    def count_rows(pred):
        def cnt_chunk(c, acc):
            off = pl.multiple_of(c * CNT_ROWS, CNT_ROWS)
            hit = jnp.where(pred(keys_ref[pl.ds(off, CNT_ROWS), :], off), 1, 0)
            return acc + jnp.sum(hit.reshape(CNT_ROWS // 8, 8, BLOCK), axis=0)
        acc = _chunk_loop(cnt_chunk, n_cnt, COUNT_UNROLL, jnp.zeros((8, BLOCK), jnp.int32))
        return jnp.sum(acc, axis=0, keepdims=True)

    def count_ge(cand):
        return count_rows(lambda keys, off: keys >= cand)

    n_grp = n_sel * (SEL_ROWS // PLANE_ROWS)
    n_blk = (n_grp + GROUP_UNROLL - 1) // GROUP_UNROLL

    def init_eq(g, carry):
        eq_ref[g] = jnp.where(g < n_grp, jnp.full((8, BLOCK), -1, jnp.int32), jnp.zeros((8, BLOCK), jnp.int32))
        return carry

    lax.fori_loop(0, n_blk * GROUP_UNROLL, init_eq, 0)

    def sweep(cur_plane, prev_plane, prev_keep):
        def blk(b, acc):
            for j in range(GROUP_UNROLL):
                g = b * GROUP_UNROLL + j
                eq = eq_ref[g]
                if prev_plane is not None:
                    eq = eq & (planes_ref[g, prev_plane] ^ ~prev_keep)
                    eq_ref[g] = eq
                ones = eq if cur_plane is None else eq & planes_ref[g, cur_plane]
                acc = acc + lax.population_count(ones)
            return acc
        acc = lax.fori_loop(0, n_blk, blk, jnp.zeros((8, BLOCK), jnp.int32))
        return jnp.sum(acc, axis=0, keepdims=True)

    def decide(bit, cnt_ones, thr_u, above):
        total = above + cnt_ones
        keep = jnp.where(total >= k_sel, -1, 0)
        return thr_u | (keep & bit), jnp.where(keep < 0, above, total), keep

    zeros = jnp.zeros((1, BLOCK), jnp.int32)
    state = decide(INT_MIN, sweep(0, None, None), zeros, zeros)

    def radix_step(it, state):
        thr_u, above, keep = state
        bit = lax.shift_left(jnp.int32(1), jnp.int32(31) - it)
        return decide(bit, sweep(it, it - 1, keep), thr_u, above)

    thr_u, above, keep = lax.fori_loop(1, 32, radix_step, state)
    cnt_thr = above + sweep(None, 31, keep)
    thr = thr_u ^ INT_MIN

    has_k = thr > INT_MIN
    thr = jnp.maximum(thr, INT_MIN + 1)

    jcut_ref[...] = jnp.full(jcut_ref.shape, 2 ** 31 - 1, jnp.int32)

    @pl.when(jnp.max(jnp.where(has_k, cnt_thr, 0)) > k_sel)
    def _():
        excess = jnp.where(has_k, count_ge(thr) - k_sel, 0)
        need = jnp.where(excess > 0, k_sel - count_ge(thr + 1), 1)
        rev_row = (seq - 1) - lax.broadcasted_iota(jnp.int32, (CNT_ROWS, BLOCK), 0)

        def tie_bisect(it, lo):
            cand = lo + lax.shift_left(jnp.int32(1), jnp.int32(seq.bit_length() - 2) - it)
            cnt = count_rows(lambda keys, off: (keys == thr) & (rev_row - off >= cand))
            return jnp.where(cnt >= need, cand, lo)

        rev = lax.fori_loop(0, seq.bit_length() - 1, tie_bisect, jnp.zeros((1, BLOCK), jnp.int32))
        jcut = jnp.where(excess > 0, (seq - 1) - rev, 2 ** 31 - 1)
        jcut_ref[...] = jnp.broadcast_to(jcut, jcut_ref.shape)

    jcut = jcut_ref[0:1, :]

    def emit(c, carry):
        off = pl.multiple_of(c * SEL_ROWS, SEL_ROWS)
        keys = keys_ref[pl.ds(off, SEL_ROWS), :]
        bar = jnp.where(off + k_row > jcut, thr + 1, thr)
        add = jnp.where(keys >= bar, 0.0, NEG)
        sel_ref[:, pl.ds(off, SEL_ROWS)] = jnp.concatenate(
            [add[j * BLOCK:(j + 1) * BLOCK].T for j in range(SEL_ROWS // BLOCK)], axis=1).astype(BF16)
        return carry

    _chunk_loop(emit, n_sel, SCORE_UNROLL)

    def fill(c, carry):
        off = pl.multiple_of(c * SEL_ROWS, SEL_ROWS)
        sel_ref[:, pl.ds(off, SEL_ROWS)] = jnp.full((BLOCK, SEL_ROWS), NEG, BF16)
        return carry

    lax.fori_loop(n_sel, seq // SEL_ROWS, fill, 0)


def _a_select(pf, batch, seq):
    t = pf.shape[0]
    nq = seq // BLOCK
    k_sel = min(TOPK_MAX, seq // 4)
    n_groups = -(-(seq // PLANE_ROWS) // GROUP_UNROLL) * GROUP_UNROLL
    nlane = IDX_HEADS * IDX_DIM
    r = np.arange(nlane)
    ph = np.zeros((nlane, IDX_HEADS * LANES), np.float32)
    plo = np.zeros((nlane, IDX_HEADS * LANES), np.float32)
    ph[r, (r // IDX_DIM) * LANES + r % IDX_DIM] = 1
    ph[r, (r // IDX_DIM) * LANES + 2 * IDX_DIM + r % IDX_DIM] = 1
    plo[r, (r // IDX_DIM) * LANES + IDX_DIM + r % IDX_DIM] = 1
    d = np.arange(IDX_DIM)
    pkh = np.zeros((LANES, LANES), np.float32)
    pkl = np.zeros((LANES, LANES), np.float32)
    pkh[d, d] = 1
    pkh[d, IDX_DIM + d] = 1
    pkl[d, 2 * IDX_DIM + d] = 1
    full = lambda shape: pl.BlockSpec(shape, lambda b, i: (0,) * len(shape))
    return pl.pallas_call(
        functools.partial(_a_select_kernel, seq=seq, k_sel=k_sel),
        grid=(batch, nq),
        in_specs=[pl.BlockSpec((BLOCK, 512), lambda b, i: (b * nq + i, 0)),
                  pl.BlockSpec((seq, LANES), lambda b, i: (b, nlane // LANES)),
                  full(ph.shape), full(plo.shape), full(pkh.shape), full(pkl.shape)],
        out_specs=pl.BlockSpec((BLOCK, seq), lambda b, i: (b * nq + i, 0)),
        out_shape=jax.ShapeDtypeStruct((t, seq), BF16),
        scratch_shapes=[pltpu.VMEM((seq, LANES), BF16), pltpu.VMEM((seq, LANES), jnp.int32),
                        pltpu.VMEM((n_groups, 32, 8, LANES), jnp.int32),
                        pltpu.VMEM((n_groups, 8, LANES), jnp.int32),
                        pltpu.VMEM((IDX_HEADS // 2, LANES, 2 * LANES), BF16),
                        pltpu.VMEM((8, LANES), jnp.int32)],
        compiler_params=_cparams(("parallel", "arbitrary")),
        name="a_select",
    )(pf, pf, jnp.asarray(ph, BF16), jnp.asarray(plo, BF16), jnp.asarray(pkh, BF16),
      jnp.asarray(pkl, BF16))


BAND_RBLK = 4


def _band_kernel(*refs, has_sink, want_lse, kv_div, rblk):
    if has_sink:
        sink_ref, refs = refs[0], refs[1:]
    q_ref, hm_ref, kp_ref, kc_ref, vp_ref, vc_ref, bias_ref, o_ref = refs[:8]
    first = pl.program_id(1) == 0
    lane = lax.broadcasted_iota(jnp.int32, (BLOCK, LANES), 1)
    col = lax.broadcasted_iota(jnp.int32, (BLOCK, 2 * BLOCK), 1)
    no_prev = jnp.logical_and(first, col < BLOCK)
    for p in range(BRANCH_WIDTH // LANES):
        kcol = (p // kv_div) * LANES
        kk = jnp.concatenate([kp_ref[:, kcol:kcol + LANES], kc_ref[:, kcol:kcol + LANES]], axis=0)
        vv = jnp.concatenate([vp_ref[:, kcol:kcol + LANES], vc_ref[:, kcol:kcol + LANES]], axis=0)
        for i in range(rblk):
            rows = slice(i * BLOCK, (i + 1) * BLOCK)
            q = q_ref[rows, p * LANES:(p + 1) * LANES]
            kblk = kk[i * BLOCK:(i + 2) * BLOCK]
            vblk = vv[i * BLOCK:(i + 2) * BLOCK]
            outs, lses = [], []
            for e in range(2):
                s = lax.dot_general(q * hm_ref[e], kblk, _CONTRACT_LANES, preferred_element_type=F32)
                s = s + bias_ref[2 * p + e]
                if i == 0:
                    s = jnp.where(no_prev, NEG, s)
                m = jnp.max(s, axis=1, keepdims=True)
                if has_sink:
                    sk = sink_ref[2 * p + e]
                    m = jnp.maximum(m, sk)
                pr = jnp.exp2(s - m)
                den = jnp.sum(pr, axis=1, keepdims=True)
                if has_sink:
                    den = den + jnp.exp2(sk - m)
                outs.append(jnp.dot(pr.astype(BF16), vblk, preferred_element_type=F32) / den)
                lses.append(m + jnp.log2(den))
            o_ref[rows, p * LANES:(p + 1) * LANES] = jnp.where(lane < HEAD_DIM, outs[0], outs[1]).astype(o_ref.dtype)
            if want_lse:
                refs[8][rows, p * LANES:(p + 1) * LANES] = jnp.where(lane < HEAD_DIM, lses[0], lses[1])


def _band_attn(q_arr, k_arr, v_arr, bias, nsub, sub_len, qcb, kcb, vcb, kv_w, sinks=None, want_lse=False):
    t = q_arr.shape[0]
    rblk = min(BAND_RBLK, sub_len // BLOCK)
    step_rows = rblk * BLOCK
    nbs = sub_len // step_rows
    kv_div = BRANCH_WIDTH // kv_w

    def cur(u, n):
        return u * nbs + n

    def prev(u, n):
        return jnp.where(n == 0, u * nbs * rblk, (u * nbs + n) * rblk - 1)

    in_specs = [
        pl.BlockSpec((step_rows, BRANCH_WIDTH), lambda u, n: (cur(u, n), qcb)),
        pl.BlockSpec((2, 1, LANES), lambda u, n: (0, 0, 0)),
        pl.BlockSpec((BLOCK, kv_w), lambda u, n: (prev(u, n), kcb)),
        pl.BlockSpec((step_rows, kv_w), lambda u, n: (cur(u, n), kcb)),
        pl.BlockSpec((BLOCK, kv_w), lambda u, n: (prev(u, n), vcb)),
        pl.BlockSpec((step_rows, kv_w), lambda u, n: (cur(u, n), vcb)),
        pl.BlockSpec(bias.shape, lambda u, n: (0, 0, 0)),
    ]
    args = [q_arr, _pair_masks(), k_arr, k_arr, v_arr, v_arr, bias]
    if sinks is not None:
        in_specs = [pl.BlockSpec(memory_space=pltpu.SMEM)] + in_specs
        args = [sinks] + args
    o_spec = pl.BlockSpec((step_rows, BRANCH_WIDTH), lambda u, n: (cur(u, n), 0))
    out_specs = [o_spec]
    out_shape = [jax.ShapeDtypeStruct((t, BRANCH_WIDTH), BF16)]
    if want_lse:
        out_specs.append(o_spec)
        out_shape.append(jax.ShapeDtypeStruct((t, BRANCH_WIDTH), F32))
    return pl.pallas_call(
        functools.partial(_band_kernel, has_sink=sinks is not None, want_lse=want_lse, kv_div=kv_div, rblk=rblk),
        grid=(nsub, nbs),
        in_specs=in_specs,
        out_specs=out_specs,
        out_shape=out_shape,
        compiler_params=_cparams(("parallel", "arbitrary")),
        name="band_attn",
    )(*args)


def _toeplitz(w, rows, cols):
    length = rows + cols - 1
    u = jnp.pad(w[..., ::-1], [(0, 0)] * (w.ndim - 1) + [(0, 1)])
    flat = jnp.broadcast_to(u[..., None, :], w.shape[:-1] + (rows, length + 1)).reshape(w.shape[:-1] + (-1,))
    skew = flat[..., :rows * length].reshape(w.shape[:-1] + (rows, length))
    return skew[..., rows - 1:rows - 1 + cols]


def _band_bias(table, step, max_dist):
    rel = BLOCK + np.arange(BLOCK)[:, None] - np.arange(2 * BLOCK)[None, :]
    rel_vec = np.arange(BLOCK - (2 * BLOCK - 1), 2 * BLOCK)
    bias = _toeplitz(table[_t5_bucket(jnp.asarray(rel_vec * step))].T.astype(F32), BLOCK, 2 * BLOCK)
    ok = (rel >= 0) & (rel <= max_dist)
    return jnp.where(jnp.asarray(ok)[None], bias, NEG)


def _merge_kernel(x_ref, ya_ref, yb0_ref, yb1_ref, yb2_ref, l0_ref, l1_ref, l2_ref, yc_ref, yd_ref,
                  za_ref, zb_ref, zc_ref, zd_ref, ga_ref, gb_ref, gc_ref, gd_ref, wb_ref, wo_ref, *rest):
    if len(rest) == 7:
        ng_ref, o_ref, xn_ref, y1_ref, y2_ref, s1_ref, s2_ref = rest
    else:
        ng_ref = xn_ref = None
        o_ref, y1_ref, y2_ref, s1_ref, s2_ref = rest
    ncol = BRANCH_WIDTH // LANES
    for src, dst in ((yb1_ref, y1_ref), (yb2_ref, y2_ref), (l1_ref, s1_ref), (l2_ref, s2_ref)):
        dil = src.shape[1]
        for r in range(dil):
            blk = src[0, r].astype(F32)
            for c in range(ncol):
                dst[c, pl.ds(r, src.shape[2], stride=dil), :] = blk[:, c * LANES:(c + 1) * LANES]
    wide = lambda ref: jnp.concatenate([ref[c] for c in range(ncol)], axis=1)
    l0, l1, l2 = l0_ref[0, 0], wide(s1_ref), wide(s2_ref)
    mx = jnp.maximum(jnp.maximum(l0, l1), l2)
    w0, w1, w2 = jnp.exp2(l0 - mx), jnp.exp2(l1 - mx), jnp.exp2(l2 - mx)
    yb = (w0 * yb0_ref[0, 0].astype(F32) + w1 * wide(y1_ref) + w2 * wide(y2_ref)) / (w0 + w1 + w2)
    ys = (ya_ref[...].astype(F32), yb, yc_ref[...].astype(F32), yd_ref[...].astype(F32))
    zs = (za_ref, zb_ref, zc_ref, zd_ref)
    gs = (ga_ref, gb_ref, gc_ref, gd_ref)
    merged = jnp.zeros(o_ref.shape, F32)
    for n in range(N_BRANCH):
        z = zs[n][...].astype(F32)
        u = (ys[n] * (z * jax.nn.sigmoid(z))).astype(BF16)
        proj = jnp.dot(u, wb_ref[n], preferred_element_type=F32)
        merged = merged + jax.nn.sigmoid(gs[n][...].astype(F32)) * proj
    x_new = x_ref[...] + jnp.dot(merged.astype(BF16), wo_ref[...], preferred_element_type=F32)
    o_ref[...] = x_new
    if xn_ref is not None:
        xn_ref[...] = _rms_bf16(x_new, ng_ref[...])


def _merge(x2, ya, yb, lb, yc, yd, pp, wb, wo, batch, seq, next_gain=None, tm=512):
    t = x2.shape[0]
    bw = BRANCH_WIDTH
    nsb = seq // tm
    row = lambda w, c: pl.BlockSpec((tm, w), lambda i: (i, c))

    def sub(dil):
        return pl.BlockSpec((1, dil, tm // dil, bw), lambda i: (i // nsb, 0, i % nsb, 0))

    dils = [dil for _, dil in B_GROUPS]
    yb = [a.reshape(batch, dil, seq // dil, bw) for a, dil in zip(yb, dils)]
    lb = [a.reshape(batch, dil, seq // dil, bw) for a, dil in zip(lb, dils)]
    in_specs = ([row(D_MODEL, 0), row(bw, 0)] + [sub(dil) for dil in dils] * 2 + [row(bw, 0), row(bw, 0)]
                + [row(bw, PP_AZ // bw), row(bw, PP_BZ // bw), row(bw, PP_CZ // bw), row(bw, PP_DZ // bw)]
                + [row(D_MODEL, PP_G // D_MODEL + n) for n in range(N_BRANCH)]
                + [pl.BlockSpec((N_BRANCH, bw, D_MODEL), lambda i: (0, 0, 0)),
                   pl.BlockSpec((D_MODEL, D_MODEL), lambda i: (0, 0))])
    args = [x2, ya, yb[0], yb[1], yb[2], lb[0], lb[1], lb[2], yc, yd, pp, pp, pp, pp, pp, pp, pp, pp, wb, wo]
    out_specs = [row(D_MODEL, 0)]
    out_shape = [jax.ShapeDtypeStruct((t, D_MODEL), F32)]
    if next_gain is not None:
        in_specs.append(pl.BlockSpec((1, D_MODEL), lambda i: (0, 0)))
        args.append(next_gain.reshape(1, D_MODEL))
        out_specs.append(row(D_MODEL, 0))
        out_shape.append(jax.ShapeDtypeStruct((t, D_MODEL), BF16))
    return pl.pallas_call(
        _merge_kernel,
        grid=(t // tm,),
        in_specs=in_specs,
        out_specs=out_specs,
        out_shape=out_shape,
        scratch_shapes=[pltpu.VMEM((bw // LANES, tm, LANES), F32)] * 4,
        compiler_params=_cparams(("parallel",)),
        name="merge",
    )(*args)


def _layer_weights(w_in, qk_gain_a, qk_gain_b, qk_gain_d):
    def cols(start, width):
        return w_in[:, :, start:start + width]

    n_bq = len(B_GROUPS) * B_HEADS * HEAD_DIM
    dk = cols(_D0 + 512, 128)
    dv = cols(_D0 + 640, 128)
    dup = lambda a: jnp.concatenate([a[:, :, :64], a[:, :, :64], a[:, :, 64:], a[:, :, 64:]], axis=-1)
    w_pn = jnp.concatenate([cols(_A0, 512), cols(_A0 + 512, 512), cols(_D0, 512), dup(dk), dup(dv)],
                           axis=-1).astype(BF16)
    w_pp = jnp.concatenate([cols(_G0, N_BRANCH * D_MODEL), cols(_A0 + 1024, 512), cols(_A0 + 1536, 512),
                            cols(_B0 + 3 * n_bq, 512), cols(_C0 + 416, 512), cols(_D0 + 768, 512)],
                           axis=-1).astype(BF16)
    w_bg = [jnp.concatenate([cols(_B0 + g * 512, 512), cols(_B0 + n_bq + g * 512, 512),
                             cols(_B0 + 2 * n_bq + g * 512, 512)], axis=-1).astype(BF16)
            for g in range(len(B_GROUPS))]
    depth = w_in.shape[0]
    zeros = lambda w: jnp.zeros((depth, D_MODEL, w), w_in.dtype)
    w_pf = jnp.concatenate([cols(_A0 + 2048, 296), zeros(216), cols(_C0, 416), zeros(96)], axis=-1).astype(BF16)

    scale = HEAD_DIM ** -0.5 * LOG2E
    tile = lambda g, reps: jnp.tile(g, (1, reps))
    hg = jnp.concatenate([tile(qk_gain_a[:, 0] * scale, 8), tile(qk_gain_a[:, 1], 8),
                          tile(qk_gain_d[:, 0] * scale, 8), tile(qk_gain_d[:, 1], 4),
                          jnp.ones((depth, 256), F32)], axis=-1)
    flag = jnp.concatenate([jnp.ones((PN_DV,), F32), jnp.zeros((PN_W - PN_DV,), F32)])
    hg_b = jnp.concatenate([tile(qk_gain_b[:, 0] * scale, 8), tile(qk_gain_b[:, 1], 8),
                            jnp.ones((depth, 512), F32)], axis=-1)
    flag_b = jnp.concatenate([jnp.ones((2 * BRANCH_WIDTH,), F32), jnp.zeros((BRANCH_WIDTH,), F32)])
    return w_pn, w_pp, w_bg, w_pf, hg, flag, hg_b, flag_b


def _c_weights(qk_gain_c, w_q_b, w_kv_b, seq):
    src_q, src_k, gain_idx, head, pe_src, rope_j, rope_half = _c_layout_tables()
    take = lambda w, src: jnp.where(jnp.asarray(src >= 0), jnp.take(w, jnp.asarray(np.maximum(src, 0)), axis=-1), 0.0)
    wq = take(w_q_b, src_q).astype(BF16)
    wk = take(w_kv_b, src_k).astype(BF16)
    v_src = np.array([h * (C_NOPE + C_V) + C_NOPE + c for h in range(C_HEADS) for c in range(C_V)])
    wv = jnp.take(w_kv_b, jnp.asarray(v_src), axis=-1).astype(BF16)
    ppe = np.zeros((LANES, C_QK_W), np.float32)
    ln = np.nonzero(pe_src >= 0)[0]
    ppe[pe_src[ln], ln] = 1
    hp = head[:C_PAIR]
    grp = ((hp[:, None] == hp[None, :]) & (hp[:, None] >= 0)).astype(np.float32)
    g_take = lambda g: jnp.where(jnp.asarray(gain_idx >= 0), jnp.take(g, jnp.asarray(np.maximum(gain_idx, 0)), axis=-1), 0.0)
    gq = g_take(qk_gain_c[:, 0]) * ((C_NOPE + C_ROPE) ** -0.5 * LOG2E)
    gk = g_take(qk_gain_c[:, 1])
    freq = ROPE_THETA ** (-jnp.arange(C_ROPE_HALF, dtype=F32) / C_ROPE_HALF)
    ang = jnp.arange(seq).astype(F32)[:, None] * freq[None, :]
    cos_j, sin_j = jnp.cos(ang), jnp.sin(ang)
    place = np.zeros((C_ROPE_HALF, C_PAIR), np.float32)
    lanes = np.nonzero(rope_j[:C_PAIR] >= 0)[0]
    place[rope_j[lanes], lanes] = 1
    is_rope = jnp.asarray(rope_j[:C_PAIR] >= 0)
    hi = lax.Precision.HIGHEST
    cos = jnp.where(is_rope, jnp.dot(cos_j, jnp.asarray(place), precision=hi), 1.0)
    sin = jnp.dot(sin_j, jnp.asarray(place), precision=hi)
    s1 = jnp.where(jnp.asarray(rope_half[:C_PAIR] == 0), -sin, 0.0)
    s2 = jnp.where(jnp.asarray(rope_half[:C_PAIR] == 1), sin, 0.0)
    return wq, wk, wv, jnp.asarray(ppe, BF16), jnp.asarray(grp, BF16), gq, gk, cos, s1, s2


def _a_bias_tiles(table):
    dist_vec = np.arange(-(BLOCK - 1), A_BIAS_TILES * BLOCK)
    vec = table[_t5_bucket(jnp.asarray(dist_vec))].T.astype(F32)
    windows = jnp.stack([vec[:, d * BLOCK:d * BLOCK + 2 * BLOCK - 1] for d in range(A_BIAS_TILES)], axis=1)
    return _toeplitz(windows, BLOCK, BLOCK)


def kernel(x, norm_gain, w_in, qk_gain_a, qk_gain_b, qk_gain_c, qk_gain_d, c_q_gain, c_kv_gain,
           w_q_b, w_kv_b, sinks, rel_bias, w_branch, w_out):
    batch, seq, d_model = x.shape
    depth = w_in.shape[0]
    t = batch * seq
    x2 = x.reshape(t, d_model)

    w_pn, w_pp, w_bg, w_pf, hg, flag, hg_b, flag_b = _layer_weights(w_in, qk_gain_a, qk_gain_b, qk_gain_d)
    wq, wk, wv, ppe, grp, gq, gk, cos, s1, s2 = _c_weights(qk_gain_c, w_q_b, w_kv_b, seq)
    wb = w_branch.astype(BF16)
    wo = w_out.astype(BF16)

    rel_bias2 = rel_bias * LOG2E
    bias_a = _a_bias_tiles(rel_bias2[:, :A_HEADS])
    bias_b = [_band_bias(rel_bias2[:, A_HEADS + g * B_HEADS:A_HEADS + (g + 1) * B_HEADS], dil, window // dil)
              for g, (window, dil) in enumerate(B_GROUPS)]
    bias_d = _band_bias(rel_bias2[:, N_BIAS_HEADS - D_HEADS:], 1, D_WINDOW - 1)

    xn = _norm(x2, norm_gain[0])
    for l in range(depth):
        pn = _proj(xn, w_pn[l], BF16, head_gain=hg[l], flag=flag)
        pp = _proj(xn, w_pp[l], BF16)
        pf = _proj(xn, w_pf[l], F32)

        sel = _a_select(pf, batch, seq)
        ya = _a_attn(pn, pp, sel, bias_a, batch, seq)

        yb, lb = [], []
        for g, (window, dil) in enumerate(B_GROUPS):
            bg = _proj(xn, w_bg[g][l], BF16, head_gain=hg_b[l], flag=flag_b, dil=dil, seq=seq)
            o, lse = _band_attn(bg, bg, bg, bias_b[g], batch * dil, seq // dil, 0, 1, 2, BRANCH_WIDTH,
                                want_lse=True)
            yb.append(o)
            lb.append(lse)

        qc, kc, vc = _c_prep(pf, c_q_gain[l].reshape(1, -1), c_kv_gain[l].reshape(1, -1), wq[l], wk[l], wv[l],
                             ppe, grp, gq[l].reshape(1, -1), gk[l].reshape(1, -1), cos, s1, s2, seq)
        yc = _c_attn(qc, kc, vc, batch, seq)

        (yd,) = _band_attn(pn, pn, pn, bias_d, batch, seq, PN_DQ // BRANCH_WIDTH, PN_DK // 256, PN_DV // 256,
                           256, sinks=sinks[l] * LOG2E)

        if l + 1 < depth:
            x2, xn = _merge(x2, ya, yb, lb, yc, yd, pp, wb[l], wo[l], batch, seq, next_gain=norm_gain[l + 1])
        else:
            (x2,) = _merge(x2, ya, yb, lb, yc, yd, pp, wb[l], wo[l], batch, seq)
    return x2.reshape(batch, seq, d_model)
```

```python
import functools
import math

import numpy as np
import jax
import jax.numpy as jnp
from jax import lax
from jax.experimental import pallas as pl
from jax.experimental.pallas import tpu as pltpu

F32 = jnp.float32
BF16 = jnp.bfloat16

D_MODEL = 1024
BLOCK = 128
HEAD_DIM = 64
N_BRANCH = 4
BRANCH_WIDTH = 512
EPS = 1e-6
A_HEADS = 8
IDX_HEADS = 8
IDX_DIM = 32
TOPK_MAX = 256
B_GROUPS = ((128, 1), (512, 4), (2048, 16))
B_HEADS = 8
C_HEADS = 8
C_NOPE = 64
C_ROPE = 32
C_V = 64
C_Q_LORA = 256
C_KV_LORA = 128
ROPE_THETA = 10000.0
D_HEADS = 8
D_KV_HEADS = 2
D_WINDOW = 128
NUM_BUCKETS = 32
MAX_DISTANCE = 2048
N_BIAS_HEADS = A_HEADS + len(B_GROUPS) * B_HEADS + D_HEADS

LANES = 128
NEG = -1e30
LOG2E = math.log2(math.e)
INT_MIN = -(2 ** 31)

_A0 = 0
_B0 = 2344
_C0 = 7464
_D0 = 8392
_G0 = 9672

PN_AQ, PN_AK, PN_DQ, PN_DK, PN_DV, PN_W = 0, 512, 1024, 1536, 1792, 2048
PP_G, PP_AV, PP_AZ, PP_BZ, PP_CZ, PP_DZ, PP_W = 0, 4096, 4608, 5120, 5632, 6144, 6656
PF_W = 1024
BG_W = 3 * BRANCH_WIDTH

A_BIAS_TILES = MAX_DISTANCE // BLOCK + 2


def _cparams(sem, vmem_mb=48):
    return pltpu.CompilerParams(dimension_semantics=sem, vmem_limit_bytes=vmem_mb * 1024 * 1024)


def _t5_bucket(dist):
    max_exact = NUM_BUCKETS // 2
    d = jnp.maximum(dist, 0)
    logd = jnp.log(jnp.maximum(d, 1).astype(F32) / max_exact)
    large = max_exact + (logd / math.log(MAX_DISTANCE / max_exact) * (NUM_BUCKETS - max_exact)).astype(jnp.int32)
    return jnp.where(d < max_exact, d, jnp.minimum(large, NUM_BUCKETS - 1))


def _rms_bf16(x, gain):
    return (x * lax.rsqrt(jnp.mean(x * x, axis=-1, keepdims=True) + EPS) * gain).astype(BF16)


def _norm_kernel(x_ref, g_ref, o_ref):
    o_ref[...] = _rms_bf16(x_ref[...], g_ref[...])


def _norm(x2, gain, tm=1024):
    t, d = x2.shape
    return pl.pallas_call(
        _norm_kernel,
        grid=(t // tm,),
        in_specs=[pl.BlockSpec((tm, d), lambda i: (i, 0)), pl.BlockSpec((1, d), lambda i: (0, 0))],
        out_specs=pl.BlockSpec((tm, d), lambda i: (i, 0)),
        out_shape=jax.ShapeDtypeStruct((t, d), BF16),
        compiler_params=_cparams(("parallel",)),
        name="norm",
    )(x2, gain.reshape(1, d))


def _proj_kernel(xn_ref, w_ref, *rest, norm, dil):
    rest = list(rest)
    res_ref = rest.pop(-1) if dil is not None and dil > 1 else None
    if norm:
        hg_ref, flag_ref, bd_ref, o_ref = rest
    else:
        (o_ref,) = rest

    def emit(h):
        if dil is None:
            o_ref[...] = h.astype(o_ref.dtype)
        elif dil == 1:
            o_ref[0, 0] = h.astype(o_ref.dtype)
        else:
            sub = h.shape[0] // dil
            for c in range(h.shape[1] // LANES):
                res_ref[c] = h[:, c * LANES:(c + 1) * LANES]
            for r in range(dil):
                o_ref[0, r] = jnp.concatenate(
                    [res_ref[c, pl.ds(r, sub, stride=dil), :] for c in range(h.shape[1] // LANES)],
                    axis=1).astype(o_ref.dtype)

    h = jnp.dot(xn_ref[...], w_ref[...], preferred_element_type=F32)
    if norm:
        ss = jnp.dot((h * h).astype(BF16), bd_ref[...], preferred_element_type=F32)
        scale = lax.rsqrt(ss * (1.0 / HEAD_DIM) + EPS) * hg_ref[...]
        h = h * jnp.where(flag_ref[...] > 0, scale, 1.0)
    emit(h)


def _proj(xn, w, out_dtype, head_gain=None, flag=None, dil=None, seq=None, tm=1024, tn=512):
    t, d = xn.shape
    n = w.shape[1]
    norm = head_gain is not None
    scratch = []
    if dil is None:
        out_spec = pl.BlockSpec((tm, tn), lambda i, j: (i, j))
        out_shape = jax.ShapeDtypeStruct((t, n), out_dtype)
        kdil = None
    else:
        nsb = seq // tm
        out_spec = pl.BlockSpec((1, dil, tm // dil, tn), lambda i, j: (i // nsb, 0, i % nsb, j))
        out_shape = jax.ShapeDtypeStruct((t // seq, dil, seq // dil, n), out_dtype)
        kdil = dil
        if dil > 1:
            scratch.append(pltpu.VMEM((tn // LANES, tm, LANES), F32))
    in_specs = [pl.BlockSpec((tm, d), lambda i, j: (i, 0)),
                pl.BlockSpec((d, tn), lambda i, j: (0, j))]
    args = [xn, w]
    if norm:
        lane = np.arange(tn)
        bd = jnp.asarray((lane[:, None] // HEAD_DIM == lane[None, :] // HEAD_DIM), BF16)
        in_specs += [pl.BlockSpec((1, tn), lambda i, j: (0, j)),
                     pl.BlockSpec((1, tn), lambda i, j: (0, j)),
                     pl.BlockSpec((tn, tn), lambda i, j: (0, 0))]
        args += [head_gain.reshape(1, n), flag.reshape(1, n), bd]
    out = pl.pallas_call(
        functools.partial(_proj_kernel, norm=norm, dil=kdil),
        grid=(t // tm, n // tn),
        in_specs=in_specs,
        out_specs=out_spec,
        out_shape=out_shape,
        scratch_shapes=scratch,
        compiler_params=_cparams(("parallel", "parallel")),
        name=("proj_norm" if norm else "proj") + ("" if dil is None else "_dil%d" % dil),
    )(*args)
    return out.reshape(t, n)


C_PAIR = 256
C_QK_W = (C_HEADS // 2) * C_PAIR
C_ROPE_HALF = C_ROPE // 2


def _c_lane(h, c):
    base = (h // 2) * C_PAIR
    e = h % 2
    if c < C_NOPE:
        return base + e * C_NOPE + c
    return base + 2 * C_NOPE + e * C_ROPE + (c - C_NOPE)


def _c_layout_tables():
    src_q = np.full(C_QK_W, -1, np.int64)
    src_k = np.full(C_QK_W, -1, np.int64)
    gain_idx = np.full(C_QK_W, -1, np.int64)
    head = np.full(C_QK_W, -1, np.int64)
    pe_src = np.full(C_QK_W, -1, np.int64)
    rope_j = np.full(C_QK_W, -1, np.int64)
    rope_half = np.zeros(C_QK_W, np.int64)
    for h in range(C_HEADS):
        for c in range(C_NOPE + C_ROPE):
            ln = _c_lane(h, c)
            src_q[ln] = h * (C_NOPE + C_ROPE) + c
            gain_idx[ln] = c
            head[ln] = h
            if c < C_NOPE:
                src_k[ln] = h * (C_NOPE + C_V) + c
            else:
                r = c - C_NOPE
                pe_src[ln] = r
                rope_j[ln] = r % C_ROPE_HALF
                rope_half[ln] = r // C_ROPE_HALF
    return src_q, src_k, gain_idx, head, pe_src, rope_j, rope_half


def _c_prep_kernel(pf_ref, cqg_ref, ckvg_ref, wq_ref, wk_ref, wv_ref, ppe_ref, grp_ref,
                   gq_ref, gk_ref, cos_ref, s1_ref, s2_ref, q_ref, k_ref, v_ref):
    blk = pf_ref[...]
    cq = blk[:, :C_Q_LORA]
    ckv = blk[:, C_Q_LORA:C_Q_LORA + C_KV_LORA]
    pe = blk[:, C_Q_LORA + C_KV_LORA:]

    def rms(v, g):
        return v * lax.rsqrt(jnp.mean(v * v, axis=-1, keepdims=True) + EPS) * g

    def head_norm_rope(raw, gain):
        sq = (raw * raw).astype(BF16)
        ss = jnp.concatenate([jnp.dot(sq[:, p * C_PAIR:(p + 1) * C_PAIR], grp_ref[...], preferred_element_type=F32)
                              for p in range(C_QK_W // C_PAIR)], axis=1)
        y = raw * lax.rsqrt(ss * (1.0 / (C_NOPE + C_ROPE)) + EPS) * gain
        up = pltpu.roll(y, C_QK_W - C_ROPE_HALF, 1)
        dn = pltpu.roll(y, C_ROPE_HALF, 1)
        wide = lambda ref: jnp.concatenate([ref[...]] * (C_QK_W // C_PAIR), axis=1)
        return y * wide(cos_ref) + up * wide(s1_ref) + dn * wide(s2_ref)

    cqn = rms(cq, cqg_ref[...]).astype(BF16)
    q_raw = jnp.dot(cqn, wq_ref[...], preferred_element_type=F32)
    q_ref[...] = head_norm_rope(q_raw, gq_ref[...]).astype(BF16)

    ckvn = rms(ckv, ckvg_ref[...]).astype(BF16)
    pe_hi = pe.astype(BF16)
    pe_lo = (pe - pe_hi.astype(F32)).astype(BF16)
    k_raw = (jnp.dot(ckvn, wk_ref[...], preferred_element_type=F32)
             + jnp.dot(pe_hi, ppe_ref[...], preferred_element_type=F32)
             + jnp.dot(pe_lo, ppe_ref[...], preferred_element_type=F32))
    k_ref[...] = head_norm_rope(k_raw, gk_ref[...]).astype(BF16)
    v_ref[...] = jnp.dot(ckvn, wv_ref[...], preferred_element_type=F32).astype(BF16)


def _c_prep(pf, cqg, ckvg, wq, wk, wv, ppe, grp, gq, gk, cos, s1, s2, seq, tm=512):
    t = pf.shape[0]
    nsb = seq // tm
    full = lambda shape: pl.BlockSpec(shape, lambda i: (0,) * len(shape))
    tab = pl.BlockSpec((tm, C_PAIR), lambda i: (i % nsb, 0))
    return pl.pallas_call(
        _c_prep_kernel,
        grid=(t // tm,),
        in_specs=[pl.BlockSpec((tm, 512), lambda i: (i, 1)),
                  full((1, C_Q_LORA)), full((1, C_KV_LORA)),
                  full((C_Q_LORA, C_QK_W)), full((C_KV_LORA, C_QK_W)), full((C_KV_LORA, BRANCH_WIDTH)),
                  full((LANES, C_QK_W)), full((C_PAIR, C_PAIR)),
                  full((1, C_QK_W)), full((1, C_QK_W)), tab, tab, tab],
        out_specs=[pl.BlockSpec((tm, C_QK_W), lambda i: (i, 0)),
                   pl.BlockSpec((tm, C_QK_W), lambda i: (i, 0)),
                   pl.BlockSpec((tm, BRANCH_WIDTH), lambda i: (i, 0))],
        out_shape=[jax.ShapeDtypeStruct((t, C_QK_W), BF16),
                   jax.ShapeDtypeStruct((t, C_QK_W), BF16),
                   jax.ShapeDtypeStruct((t, BRANCH_WIDTH), BF16)],
        compiler_params=_cparams(("parallel",)),
        name="c_prep",
    )(pf, cqg, ckvg, wq, wk, wv, ppe, grp, gq, gk, cos, s1, s2)


def _flash_update(e, s, v_ones, m_ref, acc_ref, rows=slice(None)):
    m_prev = m_ref[e, rows]
    m_new = jnp.maximum(m_prev, jnp.max(s, axis=1, keepdims=True))
    alpha = jnp.exp2(m_prev - m_new)
    p = jnp.exp2(s - jnp.concatenate([m_new] * (s.shape[1] // LANES), axis=1))
    acc_ref[e, rows] = (jnp.concatenate([alpha, alpha], axis=1) * acc_ref[e, rows]
                        + jnp.dot(p.astype(BF16), v_ones, preferred_element_type=F32))
    m_ref[e, rows] = m_new


def _flash_init(m_ref, acc_ref):
    m_ref[...] = jnp.full(m_ref.shape, NEG, F32)
    acc_ref[...] = jnp.zeros(acc_ref.shape, F32)


def _flash_finish(o_ref, acc_ref):
    lane = lax.broadcasted_iota(jnp.int32, o_ref.shape, 1)
    o0 = acc_ref[0, :, :LANES] / acc_ref[0, :, LANES:]
    o1 = acc_ref[1, :, :LANES] / acc_ref[1, :, LANES:]
    o_ref[...] = jnp.where(lane < HEAD_DIM, o0, o1).astype(o_ref.dtype)


def _with_ones(v_chunk):
    return jnp.concatenate([v_chunk, jnp.ones(v_chunk.shape, v_chunk.dtype)], axis=1)


_CONTRACT_LANES = (((1,), (1,)), ((), ()))


def _head_masks(width, ranges0, ranges1):
    m = np.zeros((2, 1, width), np.float32)
    for e, ranges in enumerate((ranges0, ranges1)):
        for lo, hi in ranges:
            m[e, 0, lo:hi] = 1
    return jnp.asarray(m, BF16)


def _pair_masks():
    return _head_masks(LANES, [(0, HEAD_DIM)], [(HEAD_DIM, LANES)])


def _transpose_keys(k_ref, kt_ref, rows=512):
    def body(c, carry):
        off = pl.multiple_of(c * rows, rows)
        kt_ref[:, pl.ds(off, rows)] = k_ref[pl.ds(off, rows), :].astype(F32).T.astype(kt_ref.dtype)
        return carry
    lax.fori_loop(0, k_ref.shape[0] // rows, body, 0)


def _chunk_loop(chunk, n, unroll, carry=None, slots=None):
    start = 0
    width = unroll
    while width >= 1:
        def body(g, carry, width=width, start=start):
            for j in range(width):
                carry = chunk(start + g * width + j, carry) if slots is None else \
                    chunk(start + g * width + j, carry, j)
            return carry
        groups = (n - start) // width
        carry = lax.fori_loop(0, groups, body, carry)
        start = start + groups * width
        width //= 2
    return carry


def _c_attn_kernel(q_ref, hm_ref, k_ref, v_ref, o_ref, kt_ref, m_ref, acc_ref, *, tq, tk, unroll):
    qi = pl.program_id(2)

    @pl.when(qi == 0)
    def _():
        _transpose_keys(k_ref, kt_ref)

    q = q_ref[...]
    qs = (q * hm_ref[0], q * hm_ref[1])
    _flash_init(m_ref, acc_ref)
    n_full = (qi * tq) // tk

    def chunk(c, row0=0, diagonal=False):
        off = pl.multiple_of(c * tk, tk)
        kc = kt_ref[:, pl.ds(off, tk)]
        vc = _with_ones(v_ref[pl.ds(off, tk), :])
        for e in range(2):
            if not diagonal:
                rows = slice(row0, tq)
                _flash_update(e, jnp.dot(qs[e][rows], kc, preferred_element_type=F32), vc, m_ref, acc_ref, rows)
                continue
            rows = slice(row0, row0 + tk)
            s = jnp.dot(qs[e][rows], kc, preferred_element_type=F32)
            s = jnp.where(lax.broadcasted_iota(jnp.int32, s.shape, 1) <= lax.broadcasted_iota(jnp.int32, s.shape, 0),
                          s, NEG)
            _flash_update(e, s, vc, m_ref, acc_ref, rows)
            if row0 + tk < tq:
                rows = slice(row0 + tk, tq)
                _flash_update(e, jnp.dot(qs[e][rows], kc, preferred_element_type=F32), vc, m_ref, acc_ref, rows)

    _chunk_loop(lambda c, _: chunk(c), n_full, unroll)
    for j in range(tq // tk):
        chunk(n_full + j, row0=j * tk, diagonal=True)
    _flash_finish(o_ref, acc_ref)


def _c_attn(qc, kc, vc, batch, seq, tq=1024, tk=512, unroll=4):
    t = qc.shape[0]
    nq = seq // tq
    npair = C_HEADS // 2
    return pl.pallas_call(
        functools.partial(_c_attn_kernel, tq=tq, tk=tk, unroll=unroll),
        grid=(batch, npair, nq),
        in_specs=[pl.BlockSpec((tq, C_PAIR), lambda b, p, i: (b * nq + i, p)),
                  pl.BlockSpec((2, 1, C_PAIR), lambda b, p, i: (0, 0, 0)),
                  pl.BlockSpec((seq, C_PAIR), lambda b, p, i: (b, p)),
                  pl.BlockSpec((seq, LANES), lambda b, p, i: (b, p))],
        out_specs=pl.BlockSpec((tq, LANES), lambda b, p, i: (b * nq + i, p)),
        out_shape=jax.ShapeDtypeStruct((t, BRANCH_WIDTH), BF16),
        scratch_shapes=[pltpu.VMEM((C_PAIR, seq), BF16),
                        pltpu.VMEM((2, tq, LANES), F32), pltpu.VMEM((2, tq, 2 * LANES), F32)],
        compiler_params=_cparams(("parallel", "parallel", "arbitrary")),
        name="c_attn",
    )(qc, _head_masks(C_PAIR, [(0, C_NOPE), (2 * C_NOPE, 2 * C_NOPE + C_ROPE)],
                      [(C_NOPE, 2 * C_NOPE), (2 * C_NOPE + C_ROPE, 2 * C_NOPE + 2 * C_ROPE)]), kc, vc)


def _a_attn_kernel(q_ref, hm_ref, k_ref, v_ref, sel_ref, bias_ref, o_ref, kt_ref, m_ref, acc_ref, msk_ref,
                   *, tq, tk, unroll):
    qi = pl.program_id(2)

    @pl.when(qi == 0)
    def _():
        _transpose_keys(k_ref, kt_ref)

    q = q_ref[...]
    qs = (q * hm_ref[0], q * hm_ref[1])
    _flash_init(m_ref, acc_ref)
    n_chunks = ((qi + 1) * tq + tk - 1) // tk

    def chunk(c, carry, slot):
        off = pl.multiple_of(c * tk, tk)
        kc = kt_ref[:, pl.ds(off, tk)]
        vc = _with_ones(v_ref[pl.ds(off, tk), :])
        msk_ref[slot] = sel_ref[:, pl.ds(off, tk)].astype(F32)
        for e in range(2):
            s = jnp.dot(qs[e], kc, preferred_element_type=F32)
            rows = []
            for i in range(tq // BLOCK):
                tiles = []
                for j in range(tk // BLOCK):
                    d = (qi * (tq // BLOCK) + i) - (c * (tk // BLOCK) + j)
                    d = jnp.clip(d, 0, A_BIAS_TILES - 1)
                    tiles.append(bias_ref[e, d])
                rows.append(jnp.concatenate(tiles, axis=1))
            s = s + jnp.concatenate(rows, axis=0) + msk_ref[slot]
            _flash_update(e, s, vc, m_ref, acc_ref)
        return carry

    _chunk_loop(chunk, n_chunks, unroll, slots=unroll)
    _flash_finish(o_ref, acc_ref)


def _a_attn(pn, pp, sel, bias_tiles, batch, seq, tq=512, tk=512, unroll=4):
    t = pn.shape[0]
    nq = seq // tq
    npair = A_HEADS // 2
    qcol, kcol, vcol = PN_AQ // LANES, PN_AK // LANES, PP_AV // LANES
    return pl.pallas_call(
        functools.partial(_a_attn_kernel, tq=tq, tk=tk, unroll=unroll),
        grid=(batch, npair, nq),
        in_specs=[pl.BlockSpec((tq, LANES), lambda b, p, i: (b * nq + i, qcol + p)),
                  pl.BlockSpec((2, 1, LANES), lambda b, p, i: (0, 0, 0)),
                  pl.BlockSpec((seq, LANES), lambda b, p, i: (b, kcol + p)),
                  pl.BlockSpec((seq, LANES), lambda b, p, i: (b, vcol + p)),
                  pl.BlockSpec((tq, seq), lambda b, p, i: (b * nq + i, 0)),
                  pl.BlockSpec((2, A_BIAS_TILES, BLOCK, BLOCK), lambda b, p, i: (p, 0, 0, 0))],
        out_specs=pl.BlockSpec((tq, LANES), lambda b, p, i: (b * nq + i, p)),
        out_shape=jax.ShapeDtypeStruct((t, BRANCH_WIDTH), BF16),
        scratch_shapes=[pltpu.VMEM((LANES, seq), BF16),
                        pltpu.VMEM((2, tq, LANES), F32), pltpu.VMEM((2, tq, 2 * LANES), F32),
                        pltpu.VMEM((unroll, tq, tk), F32)],
        compiler_params=_cparams(("parallel", "parallel", "arbitrary")),
        name="a_attn",
    )(pn, _pair_masks(), pn, pp, sel, bias_tiles)


def _sortable_key(score):
    bits = pltpu.bitcast(score, jnp.int32)
    return bits ^ ((bits >> 31) & jnp.int32(0x7FFFFFFF))


def _bit_transpose32(words):
    a = list(words)
    j, m = 16, 0x0000FFFF
    while j:
        for k in range(32):
            if k & j:
                continue
            t = (a[k] ^ lax.shift_right_logical(a[k + j], jnp.int32(j))) & m
            a[k] = a[k] ^ t
            a[k + j] = a[k + j] ^ lax.shift_left(t, jnp.int32(j))
        j >>= 1
        m ^= (m << j) & 0xFFFFFFFF
    return a


SEL_ROWS = 512
CNT_ROWS = SEL_ROWS
PLANE_ROWS = 256
SCORE_UNROLL = 2
COUNT_UNROLL = 1


def _a_select_kernel(qblk_ref, kblk_ref, ph_ref, pl_ref, pkh_ref, pkl_ref, sel_ref,
                     ikx_ref, keys_ref, planes_ref, eq_ref, iqt_ref, jcut_ref, *, seq, k_sel):
    qi = pl.program_id(1)
    n_sel = (qi * BLOCK + BLOCK + SEL_ROWS - 1) // SEL_ROWS
    n_cnt = n_sel

    @pl.when(qi == 0)
    def _():
        def prep(c, carry):
            off = pl.multiple_of(c * 512, 512)
            kv = kblk_ref[pl.ds(off, 512), :]
            hi = kv.astype(BF16)
            lo = (kv - hi.astype(F32)).astype(BF16)
            ikx_ref[pl.ds(off, 512), :] = (
                jnp.dot(hi, pkh_ref[...], preferred_element_type=F32)
                + jnp.dot(lo, pkl_ref[...], preferred_element_type=F32)).astype(BF16)
            return carry
        lax.fori_loop(0, seq // 512, prep, 0)

        def clear(g, carry):
            planes_ref[g] = jnp.zeros(planes_ref.shape[1:], jnp.int32)
            return carry
        lax.fori_loop(0, planes_ref.shape[0], clear, 0)

    qb = qblk_ref[...]
    iq = qb[:, :IDX_HEADS * IDX_DIM]
    iq_hi = iq.astype(BF16)
    iq_lo = (iq - iq_hi.astype(F32)).astype(BF16)
    iqx = (jnp.dot(iq_hi, ph_ref[...], preferred_element_type=F32)
           + jnp.dot(iq_lo, pl_ref[...], preferred_element_type=F32))
    for h in range(IDX_HEADS):
        iqt_ref[h // 2, :, (h % 2) * LANES:(h % 2 + 1) * LANES] = iqx[:, h * LANES:(h + 1) * LANES].T.astype(BF16)
    iw_t = qb[:, IDX_HEADS * IDX_DIM:IDX_HEADS * IDX_DIM + LANES].T
    iw_rows = [iw_t[IDX_DIM + h:IDX_DIM + h + 1, :] for h in range(IDX_HEADS)]
    q_pos = qi * BLOCK + lax.broadcasted_iota(jnp.int32, (SEL_ROWS, BLOCK), 1)
    k_row = lax.broadcasted_iota(jnp.int32, (SEL_ROWS, BLOCK), 0)

    def score_chunk(c, masked):
        off = pl.multiple_of(c * SEL_ROWS, SEL_ROWS)
        kx = ikx_ref[pl.ds(off, SEL_ROWS), :]
        sc = jnp.zeros((SEL_ROWS, BLOCK), F32)
        for hp in range(IDX_HEADS // 2):
            xx = jnp.dot(kx, iqt_ref[hp], preferred_element_type=F32)
            sc = (sc + jnp.maximum(xx[:, :LANES], 0.0) * iw_rows[2 * hp]
                  + jnp.maximum(xx[:, LANES:], 0.0) * iw_rows[2 * hp + 1])
        key = _sortable_key(sc + 0.0)
        if masked:
            key = jnp.where(off + k_row <= q_pos, key, INT_MIN)
        keys_ref[pl.ds(off, SEL_ROWS), :] = key
        u = key ^ INT_MIN
        for g in range(SEL_ROWS // PLANE_ROWS):
            words = [u[g * PLANE_ROWS + 8 * j:g * PLANE_ROWS + 8 * j + 8] for j in range(32)]
            for i, plane in enumerate(_bit_transpose32(words)):
                planes_ref[c * (SEL_ROWS // PLANE_ROWS) + g, i] = plane

    _chunk_loop(lambda c, _: score_chunk(c, False), n_sel - 1, SCORE_UNROLL)
    score_chunk(n_sel - 1, True)

    def count_rows(pred):
        def cnt_chunk(c, acc):
            off = pl.multiple_of(c * CNT_ROWS, CNT_ROWS)
            hit = jnp.where(pred(keys_ref[pl.ds(off, CNT_ROWS), :], off), 1, 0)
            return acc + jnp.sum(hit.reshape(CNT_ROWS // 8, 8, BLOCK), axis=0)
        acc = _chunk_loop(cnt_chunk, n_cnt, COUNT_UNROLL, jnp.zeros((8, BLOCK), jnp.int32))
        return jnp.sum(acc, axis=0, keepdims=True)

    def count_ge(cand):
        return count_rows(lambda keys, off: keys >= cand)

    n_grp = n_sel * (SEL_ROWS // PLANE_ROWS)
    all_groups = eq_ref.shape[0]
    for g in range(all_groups):
        eq_ref[g] = jnp.where(g < n_grp, jnp.full((8, BLOCK), -1, jnp.int32), jnp.zeros((8, BLOCK), jnp.int32))

    def sweep(cur_plane, prev_plane, prev_keep):
        counts = []
        for g in range(all_groups):
            eq = eq_ref[g]
            if prev_plane is not None:
                eq = eq & (planes_ref[g, prev_plane] ^ ~prev_keep)
                eq_ref[g] = eq
            ones = eq if cur_plane is None else eq & planes_ref[g, cur_plane]
            counts.append(lax.population_count(ones))
        while len(counts) > 1:
            counts = [a + b for a, b in zip(counts[::2], counts[1::2])] + counts[len(counts) & ~1:]
        return jnp.sum(counts[0], axis=0, keepdims=True)

    def decide(bit, cnt_ones, thr_u, above):
        total = above + cnt_ones
        keep = jnp.where(total >= k_sel, -1, 0)
        return thr_u | (keep & bit), jnp.where(keep < 0, above, total), keep

    zeros = jnp.zeros((1, BLOCK), jnp.int32)
    state = decide(INT_MIN, sweep(0, None, None), zeros, zeros)

    def radix_step(it, state):
        thr_u, above, keep = state
        bit = lax.shift_left(jnp.int32(1), jnp.int32(31) - it)
        return decide(bit, sweep(it, it - 1, keep), thr_u, above)

    thr_u, above, keep = lax.fori_loop(1, 32, radix_step, state)
    cnt_thr = above + sweep(None, 31, keep)
    thr = thr_u ^ INT_MIN

    has_k = thr > INT_MIN
    thr = jnp.maximum(thr, INT_MIN + 1)

    jcut_ref[...] = jnp.full(jcut_ref.shape, 2 ** 31 - 1, jnp.int32)

    @pl.when(jnp.max(jnp.where(has_k, cnt_thr, 0)) > k_sel)
    def _():
        excess = jnp.where(has_k, count_ge(thr) - k_sel, 0)
        need = jnp.where(excess > 0, k_sel - count_ge(thr + 1), 1)
        rev_row = (seq - 1) - lax.broadcasted_iota(jnp.int32, (CNT_ROWS, BLOCK), 0)

        def tie_bisect(it, lo):
            cand = lo + lax.shift_left(jnp.int32(1), jnp.int32(seq.bit_length() - 2) - it)
            cnt = count_rows(lambda keys, off: (keys == thr) & (rev_row - off >= cand))
            return jnp.where(cnt >= need, cand, lo)

        rev = lax.fori_loop(0, seq.bit_length() - 1, tie_bisect, jnp.zeros((1, BLOCK), jnp.int32))
        jcut = jnp.where(excess > 0, (seq - 1) - rev, 2 ** 31 - 1)
        jcut_ref[...] = jnp.broadcast_to(jcut, jcut_ref.shape)

    jcut = jcut_ref[0:1, :]

    def emit(c, carry):
        off = pl.multiple_of(c * SEL_ROWS, SEL_ROWS)
        keys = keys_ref[pl.ds(off, SEL_ROWS), :]
        bar = jnp.where(off + k_row > jcut, thr + 1, thr)
        add = jnp.where(keys >= bar, 0.0, NEG)
        sel_ref[:, pl.ds(off, SEL_ROWS)] = jnp.concatenate(
            [add[j * BLOCK:(j + 1) * BLOCK].T for j in range(SEL_ROWS // BLOCK)], axis=1).astype(BF16)
        return carry

    _chunk_loop(emit, n_sel, SCORE_UNROLL)

    def fill(c, carry):
        off = pl.multiple_of(c * SEL_ROWS, SEL_ROWS)
        sel_ref[:, pl.ds(off, SEL_ROWS)] = jnp.full((BLOCK, SEL_ROWS), NEG, BF16)
        return carry

    lax.fori_loop(n_sel, seq // SEL_ROWS, fill, 0)


def _a_select(pf, batch, seq):
    t = pf.shape[0]
    nq = seq // BLOCK
    k_sel = min(TOPK_MAX, seq // 4)
    n_groups = seq // PLANE_ROWS
    nlane = IDX_HEADS * IDX_DIM
    r = np.arange(nlane)
    ph = np.zeros((nlane, IDX_HEADS * LANES), np.float32)
    plo = np.zeros((nlane, IDX_HEADS * LANES), np.float32)
    ph[r, (r // IDX_DIM) * LANES + r % IDX_DIM] = 1
    ph[r, (r // IDX_DIM) * LANES + 2 * IDX_DIM + r % IDX_DIM] = 1
    plo[r, (r // IDX_DIM) * LANES + IDX_DIM + r % IDX_DIM] = 1
    d = np.arange(IDX_DIM)
    pkh = np.zeros((LANES, LANES), np.float32)
    pkl = np.zeros((LANES, LANES), np.float32)
    pkh[d, d] = 1
    pkh[d, IDX_DIM + d] = 1
    pkl[d, 2 * IDX_DIM + d] = 1
    full = lambda shape: pl.BlockSpec(shape, lambda b, i: (0,) * len(shape))
    return pl.pallas_call(
        functools.partial(_a_select_kernel, seq=seq, k_sel=k_sel),
        grid=(batch, nq),
        in_specs=[pl.BlockSpec((BLOCK, 512), lambda b, i: (b * nq + i, 0)),
                  pl.BlockSpec((seq, LANES), lambda b, i: (b, nlane // LANES)),
                  full(ph.shape), full(plo.shape), full(pkh.shape), full(pkl.shape)],
        out_specs=pl.BlockSpec((BLOCK, seq), lambda b, i: (b * nq + i, 0)),
        out_shape=jax.ShapeDtypeStruct((t, seq), BF16),
        scratch_shapes=[pltpu.VMEM((seq, LANES), BF16), pltpu.VMEM((seq, LANES), jnp.int32),
                        pltpu.VMEM((n_groups, 32, 8, LANES), jnp.int32),
                        pltpu.VMEM((n_groups, 8, LANES), jnp.int32),
                        pltpu.VMEM((IDX_HEADS // 2, LANES, 2 * LANES), BF16),
                        pltpu.VMEM((8, LANES), jnp.int32)],
        compiler_params=_cparams(("parallel", "arbitrary")),
        name="a_select",
    )(pf, pf, jnp.asarray(ph, BF16), jnp.asarray(plo, BF16), jnp.asarray(pkh, BF16),
      jnp.asarray(pkl, BF16))


BAND_RBLK = 4


def _band_kernel(*refs, has_sink, want_lse, kv_div, rblk):
    if has_sink:
        sink_ref, refs = refs[0], refs[1:]
    q_ref, hm_ref, kp_ref, kc_ref, vp_ref, vc_ref, bias_ref, o_ref = refs[:8]
    first = pl.program_id(1) == 0
    lane = lax.broadcasted_iota(jnp.int32, (BLOCK, LANES), 1)
    col = lax.broadcasted_iota(jnp.int32, (BLOCK, 2 * BLOCK), 1)
    no_prev = jnp.logical_and(first, col < BLOCK)
    for p in range(BRANCH_WIDTH // LANES):
        kcol = (p // kv_div) * LANES
        kk = jnp.concatenate([kp_ref[:, kcol:kcol + LANES], kc_ref[:, kcol:kcol + LANES]], axis=0)
        vv = jnp.concatenate([vp_ref[:, kcol:kcol + LANES], vc_ref[:, kcol:kcol + LANES]], axis=0)
        for i in range(rblk):
            rows = slice(i * BLOCK, (i + 1) * BLOCK)
            q = q_ref[rows, p * LANES:(p + 1) * LANES]
            kblk = kk[i * BLOCK:(i + 2) * BLOCK]
            vblk = vv[i * BLOCK:(i + 2) * BLOCK]
            outs, lses = [], []
            for e in range(2):
                s = lax.dot_general(q * hm_ref[e], kblk, _CONTRACT_LANES, preferred_element_type=F32)
                s = s + bias_ref[2 * p + e]
                if i == 0:
                    s = jnp.where(no_prev, NEG, s)
                m = jnp.max(s, axis=1, keepdims=True)
                if has_sink:
                    sk = sink_ref[2 * p + e]
                    m = jnp.maximum(m, sk)
                pr = jnp.exp2(s - m)
                den = jnp.sum(pr, axis=1, keepdims=True)
                if has_sink:
                    den = den + jnp.exp2(sk - m)
                outs.append(jnp.dot(pr.astype(BF16), vblk, preferred_element_type=F32) / den)
                lses.append(m + jnp.log2(den))
            o_ref[rows, p * LANES:(p + 1) * LANES] = jnp.where(lane < HEAD_DIM, outs[0], outs[1]).astype(o_ref.dtype)
            if want_lse:
                refs[8][rows, p * LANES:(p + 1) * LANES] = jnp.where(lane < HEAD_DIM, lses[0], lses[1])


def _band_attn(q_arr, k_arr, v_arr, bias, nsub, sub_len, qcb, kcb, vcb, kv_w, sinks=None, want_lse=False):
    t = q_arr.shape[0]
    rblk = min(BAND_RBLK, sub_len // BLOCK)
    step_rows = rblk * BLOCK
    nbs = sub_len // step_rows
    kv_div = BRANCH_WIDTH // kv_w

    def cur(u, n):
        return u * nbs + n

    def prev(u, n):
        return jnp.where(n == 0, u * nbs * rblk, (u * nbs + n) * rblk - 1)

    in_specs = [
        pl.BlockSpec((step_rows, BRANCH_WIDTH), lambda u, n: (cur(u, n), qcb)),
        pl.BlockSpec((2, 1, LANES), lambda u, n: (0, 0, 0)),
        pl.BlockSpec((BLOCK, kv_w), lambda u, n: (prev(u, n), kcb)),
        pl.BlockSpec((step_rows, kv_w), lambda u, n: (cur(u, n), kcb)),
        pl.BlockSpec((BLOCK, kv_w), lambda u, n: (prev(u, n), vcb)),
        pl.BlockSpec((step_rows, kv_w), lambda u, n: (cur(u, n), vcb)),
        pl.BlockSpec(bias.shape, lambda u, n: (0, 0, 0)),
    ]
    args = [q_arr, _pair_masks(), k_arr, k_arr, v_arr, v_arr, bias]
    if sinks is not None:
        in_specs = [pl.BlockSpec(memory_space=pltpu.SMEM)] + in_specs
        args = [sinks] + args
    o_spec = pl.BlockSpec((step_rows, BRANCH_WIDTH), lambda u, n: (cur(u, n), 0))
    out_specs = [o_spec]
    out_shape = [jax.ShapeDtypeStruct((t, BRANCH_WIDTH), BF16)]
    if want_lse:
        out_specs.append(o_spec)
        out_shape.append(jax.ShapeDtypeStruct((t, BRANCH_WIDTH), F32))
    return pl.pallas_call(
        functools.partial(_band_kernel, has_sink=sinks is not None, want_lse=want_lse, kv_div=kv_div, rblk=rblk),
        grid=(nsub, nbs),
        in_specs=in_specs,
        out_specs=out_specs,
        out_shape=out_shape,
        compiler_params=_cparams(("parallel", "arbitrary")),
        name="band_attn",
    )(*args)


def _toeplitz(w, rows, cols):
    length = rows + cols - 1
    u = jnp.pad(w[..., ::-1], [(0, 0)] * (w.ndim - 1) + [(0, 1)])
    flat = jnp.broadcast_to(u[..., None, :], w.shape[:-1] + (rows, length + 1)).reshape(w.shape[:-1] + (-1,))
    skew = flat[..., :rows * length].reshape(w.shape[:-1] + (rows, length))
    return skew[..., rows - 1:rows - 1 + cols]


def _band_bias(table, step, max_dist):
    rel = BLOCK + np.arange(BLOCK)[:, None] - np.arange(2 * BLOCK)[None, :]
    rel_vec = np.arange(BLOCK - (2 * BLOCK - 1), 2 * BLOCK)
    bias = _toeplitz(table[_t5_bucket(jnp.asarray(rel_vec * step))].T.astype(F32), BLOCK, 2 * BLOCK)
    ok = (rel >= 0) & (rel <= max_dist)
    return jnp.where(jnp.asarray(ok)[None], bias, NEG)


def _merge_kernel(x_ref, ya_ref, yb0_ref, yb1_ref, yb2_ref, l0_ref, l1_ref, l2_ref, yc_ref, yd_ref,
                  za_ref, zb_ref, zc_ref, zd_ref, ga_ref, gb_ref, gc_ref, gd_ref, wb_ref, wo_ref, *rest):
    if len(rest) == 7:
        ng_ref, o_ref, xn_ref, y1_ref, y2_ref, s1_ref, s2_ref = rest
    else:
        ng_ref = xn_ref = None
        o_ref, y1_ref, y2_ref, s1_ref, s2_ref = rest
    ncol = BRANCH_WIDTH // LANES
    for src, dst in ((yb1_ref, y1_ref), (yb2_ref, y2_ref), (l1_ref, s1_ref), (l2_ref, s2_ref)):
        dil = src.shape[1]
        for r in range(dil):
            blk = src[0, r].astype(F32)
            for c in range(ncol):
                dst[c, pl.ds(r, src.shape[2], stride=dil), :] = blk[:, c * LANES:(c + 1) * LANES]
    wide = lambda ref: jnp.concatenate([ref[c] for c in range(ncol)], axis=1)
    l0, l1, l2 = l0_ref[0, 0], wide(s1_ref), wide(s2_ref)
    mx = jnp.maximum(jnp.maximum(l0, l1), l2)
    w0, w1, w2 = jnp.exp2(l0 - mx), jnp.exp2(l1 - mx), jnp.exp2(l2 - mx)
    yb = (w0 * yb0_ref[0, 0].astype(F32) + w1 * wide(y1_ref) + w2 * wide(y2_ref)) / (w0 + w1 + w2)
    ys = (ya_ref[...].astype(F32), yb, yc_ref[...].astype(F32), yd_ref[...].astype(F32))
    zs = (za_ref, zb_ref, zc_ref, zd_ref)
    gs = (ga_ref, gb_ref, gc_ref, gd_ref)
    merged = jnp.zeros(o_ref.shape, F32)
    for n in range(N_BRANCH):
        z = zs[n][...].astype(F32)
        u = (ys[n] * (z * jax.nn.sigmoid(z))).astype(BF16)
        proj = jnp.dot(u, wb_ref[n], preferred_element_type=F32)
        merged = merged + jax.nn.sigmoid(gs[n][...].astype(F32)) * proj
    x_new = x_ref[...] + jnp.dot(merged.astype(BF16), wo_ref[...], preferred_element_type=F32)
    o_ref[...] = x_new
    if xn_ref is not None:
        xn_ref[...] = _rms_bf16(x_new, ng_ref[...])


def _merge(x2, ya, yb, lb, yc, yd, pp, wb, wo, batch, seq, next_gain=None, tm=512):
    t = x2.shape[0]
    bw = BRANCH_WIDTH
    nsb = seq // tm
    row = lambda w, c: pl.BlockSpec((tm, w), lambda i: (i, c))

    def sub(dil):
        return pl.BlockSpec((1, dil, tm // dil, bw), lambda i: (i // nsb, 0, i % nsb, 0))

    dils = [dil for _, dil in B_GROUPS]
    yb = [a.reshape(batch, dil, seq // dil, bw) for a, dil in zip(yb, dils)]
    lb = [a.reshape(batch, dil, seq // dil, bw) for a, dil in zip(lb, dils)]
    in_specs = ([row(D_MODEL, 0), row(bw, 0)] + [sub(dil) for dil in dils] * 2 + [row(bw, 0), row(bw, 0)]
                + [row(bw, PP_AZ // bw), row(bw, PP_BZ // bw), row(bw, PP_CZ // bw), row(bw, PP_DZ // bw)]
                + [row(D_MODEL, PP_G // D_MODEL + n) for n in range(N_BRANCH)]
                + [pl.BlockSpec((N_BRANCH, bw, D_MODEL), lambda i: (0, 0, 0)),
                   pl.BlockSpec((D_MODEL, D_MODEL), lambda i: (0, 0))])
    args = [x2, ya, yb[0], yb[1], yb[2], lb[0], lb[1], lb[2], yc, yd, pp, pp, pp, pp, pp, pp, pp, pp, wb, wo]
    out_specs = [row(D_MODEL, 0)]
    out_shape = [jax.ShapeDtypeStruct((t, D_MODEL), F32)]
    if next_gain is not None:
        in_specs.append(pl.BlockSpec((1, D_MODEL), lambda i: (0, 0)))
        args.append(next_gain.reshape(1, D_MODEL))
        out_specs.append(row(D_MODEL, 0))
        out_shape.append(jax.ShapeDtypeStruct((t, D_MODEL), BF16))
    return pl.pallas_call(
        _merge_kernel,
        grid=(t // tm,),
        in_specs=in_specs,
        out_specs=out_specs,
        out_shape=out_shape,
        scratch_shapes=[pltpu.VMEM((bw // LANES, tm, LANES), F32)] * 4,
        compiler_params=_cparams(("parallel",)),
        name="merge",
    )(*args)


def _layer_weights(w_in, qk_gain_a, qk_gain_b, qk_gain_d):
    def cols(start, width):
        return w_in[:, :, start:start + width]

    n_bq = len(B_GROUPS) * B_HEADS * HEAD_DIM
    dk = cols(_D0 + 512, 128)
    dv = cols(_D0 + 640, 128)
    dup = lambda a: jnp.concatenate([a[:, :, :64], a[:, :, :64], a[:, :, 64:], a[:, :, 64:]], axis=-1)
    w_pn = jnp.concatenate([cols(_A0, 512), cols(_A0 + 512, 512), cols(_D0, 512), dup(dk), dup(dv)],
                           axis=-1).astype(BF16)
    w_pp = jnp.concatenate([cols(_G0, N_BRANCH * D_MODEL), cols(_A0 + 1024, 512), cols(_A0 + 1536, 512),
                            cols(_B0 + 3 * n_bq, 512), cols(_C0 + 416, 512), cols(_D0 + 768, 512)],
                           axis=-1).astype(BF16)
    w_bg = [jnp.concatenate([cols(_B0 + g * 512, 512), cols(_B0 + n_bq + g * 512, 512),
                             cols(_B0 + 2 * n_bq + g * 512, 512)], axis=-1).astype(BF16)
            for g in range(len(B_GROUPS))]
    depth = w_in.shape[0]
    zeros = lambda w: jnp.zeros((depth, D_MODEL, w), w_in.dtype)
    w_pf = jnp.concatenate([cols(_A0 + 2048, 296), zeros(216), cols(_C0, 416), zeros(96)], axis=-1).astype(BF16)

    scale = HEAD_DIM ** -0.5 * LOG2E
    tile = lambda g, reps: jnp.tile(g, (1, reps))
    hg = jnp.concatenate([tile(qk_gain_a[:, 0] * scale, 8), tile(qk_gain_a[:, 1], 8),
                          tile(qk_gain_d[:, 0] * scale, 8), tile(qk_gain_d[:, 1], 4),
                          jnp.ones((depth, 256), F32)], axis=-1)
    flag = jnp.concatenate([jnp.ones((PN_DV,), F32), jnp.zeros((PN_W - PN_DV,), F32)])
    hg_b = jnp.concatenate([tile(qk_gain_b[:, 0] * scale, 8), tile(qk_gain_b[:, 1], 8),
                            jnp.ones((depth, 512), F32)], axis=-1)
    flag_b = jnp.concatenate([jnp.ones((2 * BRANCH_WIDTH,), F32), jnp.zeros((BRANCH_WIDTH,), F32)])
    return w_pn, w_pp, w_bg, w_pf, hg, flag, hg_b, flag_b


def _c_weights(qk_gain_c, w_q_b, w_kv_b, seq):
    src_q, src_k, gain_idx, head, pe_src, rope_j, rope_half = _c_layout_tables()
    take = lambda w, src: jnp.where(jnp.asarray(src >= 0), jnp.take(w, jnp.asarray(np.maximum(src, 0)), axis=-1), 0.0)
    wq = take(w_q_b, src_q).astype(BF16)
    wk = take(w_kv_b, src_k).astype(BF16)
    v_src = np.array([h * (C_NOPE + C_V) + C_NOPE + c for h in range(C_HEADS) for c in range(C_V)])
    wv = jnp.take(w_kv_b, jnp.asarray(v_src), axis=-1).astype(BF16)
    ppe = np.zeros((LANES, C_QK_W), np.float32)
    ln = np.nonzero(pe_src >= 0)[0]
    ppe[pe_src[ln], ln] = 1
    hp = head[:C_PAIR]
    grp = ((hp[:, None] == hp[None, :]) & (hp[:, None] >= 0)).astype(np.float32)
    g_take = lambda g: jnp.where(jnp.asarray(gain_idx >= 0), jnp.take(g, jnp.asarray(np.maximum(gain_idx, 0)), axis=-1), 0.0)
    gq = g_take(qk_gain_c[:, 0]) * ((C_NOPE + C_ROPE) ** -0.5 * LOG2E)
    gk = g_take(qk_gain_c[:, 1])
    freq = ROPE_THETA ** (-jnp.arange(C_ROPE_HALF, dtype=F32) / C_ROPE_HALF)
    ang = jnp.arange(seq).astype(F32)[:, None] * freq[None, :]
    cos_j, sin_j = jnp.cos(ang), jnp.sin(ang)
    place = np.zeros((C_ROPE_HALF, C_PAIR), np.float32)
    lanes = np.nonzero(rope_j[:C_PAIR] >= 0)[0]
    place[rope_j[lanes], lanes] = 1
    is_rope = jnp.asarray(rope_j[:C_PAIR] >= 0)
    hi = lax.Precision.HIGHEST
    cos = jnp.where(is_rope, jnp.dot(cos_j, jnp.asarray(place), precision=hi), 1.0)
    sin = jnp.dot(sin_j, jnp.asarray(place), precision=hi)
    s1 = jnp.where(jnp.asarray(rope_half[:C_PAIR] == 0), -sin, 0.0)
    s2 = jnp.where(jnp.asarray(rope_half[:C_PAIR] == 1), sin, 0.0)
    return wq, wk, wv, jnp.asarray(ppe, BF16), jnp.asarray(grp, BF16), gq, gk, cos, s1, s2


def _a_bias_tiles(table):
    dist_vec = np.arange(-(BLOCK - 1), A_BIAS_TILES * BLOCK)
    vec = table[_t5_bucket(jnp.asarray(dist_vec))].T.astype(F32)
    windows = jnp.stack([vec[:, d * BLOCK:d * BLOCK + 2 * BLOCK - 1] for d in range(A_BIAS_TILES)], axis=1)
    return _toeplitz(windows, BLOCK, BLOCK)


def kernel(x, norm_gain, w_in, qk_gain_a, qk_gain_b, qk_gain_c, qk_gain_d, c_q_gain, c_kv_gain,
           w_q_b, w_kv_b, sinks, rel_bias, w_branch, w_out):
    batch, seq, d_model = x.shape
    depth = w_in.shape[0]
    t = batch * seq
    x2 = x.reshape(t, d_model)

    w_pn, w_pp, w_bg, w_pf, hg, flag, hg_b, flag_b = _layer_weights(w_in, qk_gain_a, qk_gain_b, qk_gain_d)
    wq, wk, wv, ppe, grp, gq, gk, cos, s1, s2 = _c_weights(qk_gain_c, w_q_b, w_kv_b, seq)
    wb = w_branch.astype(BF16)
    wo = w_out.astype(BF16)

    rel_bias2 = rel_bias * LOG2E
    bias_a = _a_bias_tiles(rel_bias2[:, :A_HEADS])
    bias_b = [_band_bias(rel_bias2[:, A_HEADS + g * B_HEADS:A_HEADS + (g + 1) * B_HEADS], dil, window // dil)
              for g, (window, dil) in enumerate(B_GROUPS)]
    bias_d = _band_bias(rel_bias2[:, N_BIAS_HEADS - D_HEADS:], 1, D_WINDOW - 1)

    xn = _norm(x2, norm_gain[0])
    for l in range(depth):
        pn = _proj(xn, w_pn[l], BF16, head_gain=hg[l], flag=flag)
        pp = _proj(xn, w_pp[l], BF16)
        pf = _proj(xn, w_pf[l], F32)

        sel = _a_select(pf, batch, seq)
        ya = _a_attn(pn, pp, sel, bias_a, batch, seq)

        yb, lb = [], []
        for g, (window, dil) in enumerate(B_GROUPS):
            bg = _proj(xn, w_bg[g][l], BF16, head_gain=hg_b[l], flag=flag_b, dil=dil, seq=seq)
            o, lse = _band_attn(bg, bg, bg, bias_b[g], batch * dil, seq // dil, 0, 1, 2, BRANCH_WIDTH,
                                want_lse=True)
            yb.append(o)
            lb.append(lse)

        qc, kc, vc = _c_prep(pf, c_q_gain[l].reshape(1, -1), c_kv_gain[l].reshape(1, -1), wq[l], wk[l], wv[l],
                             ppe, grp, gq[l].reshape(1, -1), gk[l].reshape(1, -1), cos, s1, s2, seq)
        yc = _c_attn(qc, kc, vc, batch, seq)

        (yd,) = _band_attn(pn, pn, pn, bias_d, batch, seq, PN_DQ // BRANCH_WIDTH, PN_DK // 256, PN_DV // 256,
                           256, sinks=sinks[l] * LOG2E)

        if l + 1 < depth:
            x2, xn = _merge(x2, ya, yb, lb, yc, yd, pp, wb[l], wo[l], batch, seq, next_gain=norm_gain[l + 1])
        else:
            (x2,) = _merge(x2, ya, yb, lb, yc, yd, pp, wb[l], wo[l], batch, seq)
    return x2.reshape(batch, seq, d_model)
```

```python
import functools
import math

import numpy as np
import jax
import jax.numpy as jnp
from jax import lax
from jax.experimental import pallas as pl
from jax.experimental.pallas import tpu as pltpu

F32 = jnp.float32
BF16 = jnp.bfloat16

D_MODEL = 1024
BLOCK = 128
HEAD_DIM = 64
N_BRANCH = 4
BRANCH_WIDTH = 512
EPS = 1e-6
A_HEADS = 8
IDX_HEADS = 8
IDX_DIM = 32
TOPK_MAX = 256
B_GROUPS = ((128, 1), (512, 4), (2048, 16))
B_HEADS = 8
C_HEADS = 8
C_NOPE = 64
C_ROPE = 32
C_V = 64
C_Q_LORA = 256
C_KV_LORA = 128
ROPE_THETA = 10000.0
D_HEADS = 8
D_KV_HEADS = 2
D_WINDOW = 128
NUM_BUCKETS = 32
MAX_DISTANCE = 2048
N_BIAS_HEADS = A_HEADS + len(B_GROUPS) * B_HEADS + D_HEADS

LANES = 128
MXU_WIDTH = 256
NEG = -1e30
LOG2E = math.log2(math.e)
INT_MIN = -(2 ** 31)

_A0 = 0
_B0 = 2344
_C0 = 7464
_D0 = 8392
_G0 = 9672

PN_AQ, PN_AK, PN_DQ, PN_DK, PN_DV, PN_W = 0, 512, 1024, 1536, 1792, 2048
PP_G, PP_AV, PP_AZ, PP_BZ, PP_CZ, PP_DZ, PP_W = 0, 4096, 4608, 5120, 5632, 6144, 6656
PF_W = 1024
BG_W = 3 * BRANCH_WIDTH

A_BIAS_TILES = MAX_DISTANCE // BLOCK + 2


def _cparams(sem, vmem_mb=48):
    return pltpu.CompilerParams(dimension_semantics=sem, vmem_limit_bytes=vmem_mb * 1024 * 1024)


def _t5_bucket(dist):
    max_exact = NUM_BUCKETS // 2
    d = jnp.maximum(dist, 0)
    logd = jnp.log(jnp.maximum(d, 1).astype(F32) / max_exact)
    large = max_exact + (logd / math.log(MAX_DISTANCE / max_exact) * (NUM_BUCKETS - max_exact)).astype(jnp.int32)
    return jnp.where(d < max_exact, d, jnp.minimum(large, NUM_BUCKETS - 1))


def _rms_bf16(x, gain):
    return (x * lax.rsqrt(jnp.mean(x * x, axis=-1, keepdims=True) + EPS) * gain).astype(BF16)


def _norm_kernel(x_ref, g_ref, o_ref):
    o_ref[...] = _rms_bf16(x_ref[...], g_ref[...])


def _norm(x2, gain, tm=1024):
    t, d = x2.shape
    return pl.pallas_call(
        _norm_kernel,
        grid=(t // tm,),
        in_specs=[pl.BlockSpec((tm, d), lambda i: (i, 0)), pl.BlockSpec((1, d), lambda i: (0, 0))],
        out_specs=pl.BlockSpec((tm, d), lambda i: (i, 0)),
        out_shape=jax.ShapeDtypeStruct((t, d), BF16),
        compiler_params=_cparams(("parallel",)),
        name="norm",
    )(x2, gain.reshape(1, d))


def _proj_kernel(xn_ref, w_ref, *rest, norm, dil):
    rest = list(rest)
    res_ref = rest.pop(-1) if dil is not None and dil > 1 else None
    if norm:
        hg_ref, flag_ref, bd_ref, o_ref = rest
    else:
        (o_ref,) = rest

    def emit(h):
        if dil is None:
            o_ref[...] = h.astype(o_ref.dtype)
        elif dil == 1:
            o_ref[0, 0] = h.astype(o_ref.dtype)
        else:
            sub = h.shape[0] // dil
            for c in range(h.shape[1] // LANES):
                res_ref[c] = h[:, c * LANES:(c + 1) * LANES]
            for r in range(dil):
                o_ref[0, r] = jnp.concatenate(
                    [res_ref[c, pl.ds(r, sub, stride=dil), :] for c in range(h.shape[1] // LANES)],
                    axis=1).astype(o_ref.dtype)

    h = jnp.dot(xn_ref[...], w_ref[...], preferred_element_type=F32)
    if norm:
        sq = (h * h).astype(BF16)
        slab = bd_ref.shape[0]
        ss = jnp.concatenate([jnp.dot(sq[:, c:c + slab], bd_ref[...], preferred_element_type=F32)
                              for c in range(0, sq.shape[1], slab)], axis=1)
        scale = lax.rsqrt(ss * (1.0 / HEAD_DIM) + EPS) * hg_ref[...]
        h = h * jnp.where(flag_ref[...] > 0, scale, 1.0)
    emit(h)


def _proj(xn, w, out_dtype, head_gain=None, flag=None, dil=None, seq=None, tm=1024, tn=512):
    t, d = xn.shape
    n = w.shape[1]
    norm = head_gain is not None
    scratch = []
    if dil is None:
        out_spec = pl.BlockSpec((tm, tn), lambda i, j: (i, j))
        out_shape = jax.ShapeDtypeStruct((t, n), out_dtype)
        kdil = None
    else:
        nsb = seq // tm
        out_spec = pl.BlockSpec((1, dil, tm // dil, tn), lambda i, j: (i // nsb, 0, i % nsb, j))
        out_shape = jax.ShapeDtypeStruct((t // seq, dil, seq // dil, n), out_dtype)
        kdil = dil
        if dil > 1:
            scratch.append(pltpu.VMEM((tn // LANES, tm, LANES), F32))
    in_specs = [pl.BlockSpec((tm, d), lambda i, j: (i, 0)),
                pl.BlockSpec((d, tn), lambda i, j: (0, j))]
    args = [xn, w]
    if norm:
        lane = np.arange(MXU_WIDTH)
        bd = jnp.asarray((lane[:, None] // HEAD_DIM == lane[None, :] // HEAD_DIM), BF16)
        in_specs += [pl.BlockSpec((1, tn), lambda i, j: (0, j)),
                     pl.BlockSpec((1, tn), lambda i, j: (0, j)),
                     pl.BlockSpec((MXU_WIDTH, MXU_WIDTH), lambda i, j: (0, 0))]
        args += [head_gain.reshape(1, n), flag.reshape(1, n), bd]
    out = pl.pallas_call(
        functools.partial(_proj_kernel, norm=norm, dil=kdil),
        grid=(t // tm, n // tn),
        in_specs=in_specs,
        out_specs=out_spec,
        out_shape=out_shape,
        scratch_shapes=scratch,
        compiler_params=_cparams(("parallel", "parallel")),
        name=("proj_norm" if norm else "proj") + ("" if dil is None else "_dil%d" % dil),
    )(*args)
    return out.reshape(t, n)


C_PAIR = 256
C_QK_W = (C_HEADS // 2) * C_PAIR
C_ROPE_HALF = C_ROPE // 2


def _c_lane(h, c):
    base = (h // 2) * C_PAIR
    e = h % 2
    if c < C_NOPE:
        return base + e * C_NOPE + c
    return base + 2 * C_NOPE + e * C_ROPE + (c - C_NOPE)


def _c_layout_tables():
    src_q = np.full(C_QK_W, -1, np.int64)
    src_k = np.full(C_QK_W, -1, np.int64)
    gain_idx = np.full(C_QK_W, -1, np.int64)
    head = np.full(C_QK_W, -1, np.int64)
    pe_src = np.full(C_QK_W, -1, np.int64)
    rope_j = np.full(C_QK_W, -1, np.int64)
    rope_half = np.zeros(C_QK_W, np.int64)
    for h in range(C_HEADS):
        for c in range(C_NOPE + C_ROPE):
            ln = _c_lane(h, c)
            src_q[ln] = h * (C_NOPE + C_ROPE) + c
            gain_idx[ln] = c
            head[ln] = h
            if c < C_NOPE:
                src_k[ln] = h * (C_NOPE + C_V) + c
            else:
                r = c - C_NOPE
                pe_src[ln] = r
                rope_j[ln] = r % C_ROPE_HALF
                rope_half[ln] = r // C_ROPE_HALF
    return src_q, src_k, gain_idx, head, pe_src, rope_j, rope_half


def _c_prep_kernel(pf_ref, cqg_ref, ckvg_ref, wq_ref, wk_ref, wv_ref, ppe_ref, grp_ref,
                   gq_ref, gk_ref, cos_ref, s1_ref, s2_ref, q_ref, k_ref, v_ref):
    blk = pf_ref[...]
    cq = blk[:, :C_Q_LORA]
    ckv = blk[:, C_Q_LORA:C_Q_LORA + C_KV_LORA]
    pe = blk[:, C_Q_LORA + C_KV_LORA:]

    def rms(v, g):
        return v * lax.rsqrt(jnp.mean(v * v, axis=-1, keepdims=True) + EPS) * g

    def head_norm_rope(raw, gain):
        sq = (raw * raw).astype(BF16)
        ss = jnp.concatenate([jnp.dot(sq[:, p * C_PAIR:(p + 1) * C_PAIR], grp_ref[...], preferred_element_type=F32)
                              for p in range(C_QK_W // C_PAIR)], axis=1)
        y = raw * lax.rsqrt(ss * (1.0 / (C_NOPE + C_ROPE)) + EPS) * gain
        up = pltpu.roll(y, C_QK_W - C_ROPE_HALF, 1)
        dn = pltpu.roll(y, C_ROPE_HALF, 1)
        wide = lambda ref: jnp.concatenate([ref[...]] * (C_QK_W // C_PAIR), axis=1)
        return y * wide(cos_ref) + up * wide(s1_ref) + dn * wide(s2_ref)

    cqn = rms(cq, cqg_ref[...]).astype(BF16)
    q_raw = jnp.dot(cqn, wq_ref[...], preferred_element_type=F32)
    q_ref[...] = head_norm_rope(q_raw, gq_ref[...]).astype(BF16)

    ckvn = rms(ckv, ckvg_ref[...]).astype(BF16)
    pe_hi = pe.astype(BF16)
    pe_lo = (pe - pe_hi.astype(F32)).astype(BF16)
    k_raw = (jnp.dot(ckvn, wk_ref[...], preferred_element_type=F32)
             + jnp.dot(pe_hi, ppe_ref[...], preferred_element_type=F32)
             + jnp.dot(pe_lo, ppe_ref[...], preferred_element_type=F32))
    k_ref[...] = head_norm_rope(k_raw, gk_ref[...]).astype(BF16)
    v_ref[...] = jnp.dot(ckvn, wv_ref[...], preferred_element_type=F32).astype(BF16)


def _c_prep(pf, cqg, ckvg, wq, wk, wv, ppe, grp, gq, gk, cos, s1, s2, seq, tm=512):
    t = pf.shape[0]
    nsb = seq // tm
    full = lambda shape: pl.BlockSpec(shape, lambda i: (0,) * len(shape))
    tab = pl.BlockSpec((tm, C_PAIR), lambda i: (i % nsb, 0))
    return pl.pallas_call(
        _c_prep_kernel,
        grid=(t // tm,),
        in_specs=[pl.BlockSpec((tm, 512), lambda i: (i, 1)),
                  full((1, C_Q_LORA)), full((1, C_KV_LORA)),
                  full((C_Q_LORA, C_QK_W)), full((C_KV_LORA, C_QK_W)), full((C_KV_LORA, BRANCH_WIDTH)),
                  full((LANES, C_QK_W)), full((C_PAIR, C_PAIR)),
                  full((1, C_QK_W)), full((1, C_QK_W)), tab, tab, tab],
        out_specs=[pl.BlockSpec((tm, C_QK_W), lambda i: (i, 0)),
                   pl.BlockSpec((tm, C_QK_W), lambda i: (i, 0)),
                   pl.BlockSpec((tm, BRANCH_WIDTH), lambda i: (i, 0))],
        out_shape=[jax.ShapeDtypeStruct((t, C_QK_W), BF16),
                   jax.ShapeDtypeStruct((t, C_QK_W), BF16),
                   jax.ShapeDtypeStruct((t, BRANCH_WIDTH), BF16)],
        compiler_params=_cparams(("parallel",)),
        name="c_prep",
    )(pf, cqg, ckvg, wq, wk, wv, ppe, grp, gq, gk, cos, s1, s2)


def _flash_update(e, s, v_ones, m_ref, acc_ref, rows=slice(None)):
    m_prev = m_ref[e, rows]
    m_new = jnp.maximum(m_prev, jnp.max(s, axis=1, keepdims=True))
    alpha = jnp.exp2(m_prev - m_new)
    p = jnp.exp2(s - jnp.concatenate([m_new] * (s.shape[1] // LANES), axis=1))
    acc_ref[e, rows] = (jnp.concatenate([alpha, alpha], axis=1) * acc_ref[e, rows]
                        + jnp.dot(p.astype(BF16), v_ones, preferred_element_type=F32))
    m_ref[e, rows] = m_new


def _flash_init(m_ref, acc_ref):
    m_ref[...] = jnp.full(m_ref.shape, NEG, F32)
    acc_ref[...] = jnp.zeros(acc_ref.shape, F32)


def _flash_finish(o_ref, acc_ref):
    lane = lax.broadcasted_iota(jnp.int32, o_ref.shape, 1)
    o0 = acc_ref[0, :, :LANES] / acc_ref[0, :, LANES:]
    o1 = acc_ref[1, :, :LANES] / acc_ref[1, :, LANES:]
    o_ref[...] = jnp.where(lane < HEAD_DIM, o0, o1).astype(o_ref.dtype)


def _with_ones(v_chunk):
    return jnp.concatenate([v_chunk, jnp.ones(v_chunk.shape, v_chunk.dtype)], axis=1)


_CONTRACT_LANES = (((1,), (1,)), ((), ()))


def _head_masks(width, ranges0, ranges1):
    m = np.zeros((2, 1, width), np.float32)
    for e, ranges in enumerate((ranges0, ranges1)):
        for lo, hi in ranges:
            m[e, 0, lo:hi] = 1
    return jnp.asarray(m, BF16)


def _pair_masks():
    return _head_masks(LANES, [(0, HEAD_DIM)], [(HEAD_DIM, LANES)])


def _transpose_keys(k_ref, kt_ref, rows=512):
    def body(c, carry):
        off = pl.multiple_of(c * rows, rows)
        kt_ref[:, pl.ds(off, rows)] = k_ref[pl.ds(off, rows), :].astype(F32).T.astype(kt_ref.dtype)
        return carry
    lax.fori_loop(0, k_ref.shape[0] // rows, body, 0)


def _chunk_loop(chunk, n, unroll, carry=None, slots=None):
    start = 0
    width = unroll
    while width >= 1:
        def body(g, carry, width=width, start=start):
            for j in range(width):
                carry = chunk(start + g * width + j, carry) if slots is None else \
                    chunk(start + g * width + j, carry, j)
            return carry
        groups = (n - start) // width
        carry = lax.fori_loop(0, groups, body, carry)
        start = start + groups * width
        width //= 2
    return carry


def _c_attn_kernel(q_ref, hm_ref, k_ref, v_ref, o_ref, kt_ref, m_ref, acc_ref, *, tq, tk, unroll):
    qi = pl.program_id(2)

    @pl.when(qi == 0)
    def _():
        _transpose_keys(k_ref, kt_ref)

    q = q_ref[...]
    qs = (q * hm_ref[0], q * hm_ref[1])
    _flash_init(m_ref, acc_ref)
    n_full = (qi * tq) // tk

    def chunk(c, row0=0, diagonal=False):
        off = pl.multiple_of(c * tk, tk)
        kc = kt_ref[:, pl.ds(off, tk)]
        vc = _with_ones(v_ref[pl.ds(off, tk), :])
        for e in range(2):
            if not diagonal:
                rows = slice(row0, tq)
                _flash_update(e, jnp.dot(qs[e][rows], kc, preferred_element_type=F32), vc, m_ref, acc_ref, rows)
                continue
            rows = slice(row0, row0 + tk)
            s = jnp.dot(qs[e][rows], kc, preferred_element_type=F32)
            s = jnp.where(lax.broadcasted_iota(jnp.int32, s.shape, 1) <= lax.broadcasted_iota(jnp.int32, s.shape, 0),
                          s, NEG)
            _flash_update(e, s, vc, m_ref, acc_ref, rows)
            if row0 + tk < tq:
                rows = slice(row0 + tk, tq)
                _flash_update(e, jnp.dot(qs[e][rows], kc, preferred_element_type=F32), vc, m_ref, acc_ref, rows)

    _chunk_loop(lambda c, _: chunk(c), n_full, unroll)
    for j in range(tq // tk):
        chunk(n_full + j, row0=j * tk, diagonal=True)
    _flash_finish(o_ref, acc_ref)


def _c_attn(qc, kc, vc, batch, seq, tq=1024, tk=512, unroll=4):
    t = qc.shape[0]
    nq = seq // tq
    npair = C_HEADS // 2
    return pl.pallas_call(
        functools.partial(_c_attn_kernel, tq=tq, tk=tk, unroll=unroll),
        grid=(batch, npair, nq),
        in_specs=[pl.BlockSpec((tq, C_PAIR), lambda b, p, i: (b * nq + i, p)),
                  pl.BlockSpec((2, 1, C_PAIR), lambda b, p, i: (0, 0, 0)),
                  pl.BlockSpec((seq, C_PAIR), lambda b, p, i: (b, p)),
                  pl.BlockSpec((seq, LANES), lambda b, p, i: (b, p))],
        out_specs=pl.BlockSpec((tq, LANES), lambda b, p, i: (b * nq + i, p)),
        out_shape=jax.ShapeDtypeStruct((t, BRANCH_WIDTH), BF16),
        scratch_shapes=[pltpu.VMEM((C_PAIR, seq), BF16),
                        pltpu.VMEM((2, tq, LANES), F32), pltpu.VMEM((2, tq, 2 * LANES), F32)],
        compiler_params=_cparams(("parallel", "parallel", "arbitrary")),
        name="c_attn",
    )(qc, _head_masks(C_PAIR, [(0, C_NOPE), (2 * C_NOPE, 2 * C_NOPE + C_ROPE)],
                      [(C_NOPE, 2 * C_NOPE), (2 * C_NOPE + C_ROPE, 2 * C_NOPE + 2 * C_ROPE)]), kc, vc)


def _a_attn_kernel(q_ref, hm_ref, k_ref, v_ref, sel_ref, bias_ref, o_ref, kt_ref, m_ref, acc_ref, msk_ref,
                   *, tq, tk, unroll):
    qi = pl.program_id(2)

    @pl.when(qi == 0)
    def _():
        _transpose_keys(k_ref, kt_ref)

    q = q_ref[...]
    qs = (q * hm_ref[0], q * hm_ref[1])
    _flash_init(m_ref, acc_ref)
    n_chunks = ((qi + 1) * tq + tk - 1) // tk

    def chunk(c, carry, slot):
        off = pl.multiple_of(c * tk, tk)
        kc = kt_ref[:, pl.ds(off, tk)]
        vc = _with_ones(v_ref[pl.ds(off, tk), :])
        msk_ref[slot] = sel_ref[:, pl.ds(off, tk)].astype(F32)
        for e in range(2):
            s = jnp.dot(qs[e], kc, preferred_element_type=F32)
            rows = []
            for i in range(tq // BLOCK):
                tiles = []
                for j in range(tk // BLOCK):
                    d = (qi * (tq // BLOCK) + i) - (c * (tk // BLOCK) + j)
                    d = jnp.clip(d, 0, A_BIAS_TILES - 1)
                    tiles.append(bias_ref[e, d])
                rows.append(jnp.concatenate(tiles, axis=1))
            s = s + jnp.concatenate(rows, axis=0) + msk_ref[slot]
            _flash_update(e, s, vc, m_ref, acc_ref)
        return carry

    _chunk_loop(chunk, n_chunks, unroll, slots=unroll)
    _flash_finish(o_ref, acc_ref)


def _a_attn(pn, pp, sel, bias_tiles, batch, seq, tq=512, tk=512, unroll=4):
    t = pn.shape[0]
    nq = seq // tq
    npair = A_HEADS // 2
    qcol, kcol, vcol = PN_AQ // LANES, PN_AK // LANES, PP_AV // LANES
    return pl.pallas_call(
        functools.partial(_a_attn_kernel, tq=tq, tk=tk, unroll=unroll),
        grid=(batch, npair, nq),
        in_specs=[pl.BlockSpec((tq, LANES), lambda b, p, i: (b * nq + i, qcol + p)),
                  pl.BlockSpec((2, 1, LANES), lambda b, p, i: (0, 0, 0)),
                  pl.BlockSpec((seq, LANES), lambda b, p, i: (b, kcol + p)),
                  pl.BlockSpec((seq, LANES), lambda b, p, i: (b, vcol + p)),
                  pl.BlockSpec((tq, seq), lambda b, p, i: (b * nq + i, 0)),
                  pl.BlockSpec((2, A_BIAS_TILES, BLOCK, BLOCK), lambda b, p, i: (p, 0, 0, 0))],
        out_specs=pl.BlockSpec((tq, LANES), lambda b, p, i: (b * nq + i, p)),
        out_shape=jax.ShapeDtypeStruct((t, BRANCH_WIDTH), BF16),
        scratch_shapes=[pltpu.VMEM((LANES, seq), BF16),
                        pltpu.VMEM((2, tq, LANES), F32), pltpu.VMEM((2, tq, 2 * LANES), F32),
                        pltpu.VMEM((unroll, tq, tk), F32)],
        compiler_params=_cparams(("parallel", "parallel", "arbitrary")),
        name="a_attn",
    )(pn, _pair_masks(), pn, pp, sel, bias_tiles)


def _sortable_key(score):
    bits = pltpu.bitcast(score, jnp.int32)
    return bits ^ ((bits >> 31) & jnp.int32(0x7FFFFFFF))


def _bit_transpose32(words):
    a = list(words)
    j, m = 16, 0x0000FFFF
    while j:
        for k in range(32):
            if k & j:
                continue
            t = (a[k] ^ lax.shift_right_logical(a[k + j], jnp.int32(j))) & m
            a[k] = a[k] ^ t
            a[k + j] = a[k + j] ^ lax.shift_left(t, jnp.int32(j))
        j >>= 1
        m ^= (m << j) & 0xFFFFFFFF
    return a


SEL_ROWS = 512
CNT_ROWS = SEL_ROWS
PLANE_ROWS = 256
SCORE_UNROLL = 2
COUNT_UNROLL = 1


def _a_select_kernel(qblk_ref, kblk_ref, ph_ref, pl_ref, pkh_ref, pkl_ref, sel_ref,
                     ikx_ref, keys_ref, planes_ref, eq_ref, iqt_ref, jcut_ref, *, seq, k_sel):
    qi = pl.program_id(1)
    n_sel = (qi * BLOCK + BLOCK + SEL_ROWS - 1) // SEL_ROWS
    n_cnt = n_sel

    @pl.when(qi == 0)
    def _():
        def prep(c, carry):
            off = pl.multiple_of(c * 512, 512)
            kv = kblk_ref[pl.ds(off, 512), :]
            hi = kv.astype(BF16)
            lo = (kv - hi.astype(F32)).astype(BF16)
            ikx_ref[pl.ds(off, 512), :] = (
                jnp.dot(hi, pkh_ref[...], preferred_element_type=F32)
                + jnp.dot(lo, pkl_ref[...], preferred_element_type=F32)).astype(BF16)
            return carry
        lax.fori_loop(0, seq // 512, prep, 0)

        def clear(g, carry):
            planes_ref[g] = jnp.zeros(planes_ref.shape[1:], jnp.int32)
            return carry
        lax.fori_loop(0, planes_ref.shape[0], clear, 0)

    qb = qblk_ref[...]
    iq = qb[:, :IDX_HEADS * IDX_DIM]
    iq_hi = iq.astype(BF16)
    iq_lo = (iq - iq_hi.astype(F32)).astype(BF16)
    iqx = (jnp.dot(iq_hi, ph_ref[...], preferred_element_type=F32)
           + jnp.dot(iq_lo, pl_ref[...], preferred_element_type=F32))
    for h in range(IDX_HEADS):
        iqt_ref[h // 2, :, (h % 2) * LANES:(h % 2 + 1) * LANES] = iqx[:, h * LANES:(h + 1) * LANES].T.astype(BF16)
    iw_t = qb[:, IDX_HEADS * IDX_DIM:IDX_HEADS * IDX_DIM + LANES].T
    iw_rows = [iw_t[IDX_DIM + h:IDX_DIM + h + 1, :] for h in range(IDX_HEADS)]
    q_pos = qi * BLOCK + lax.broadcasted_iota(jnp.int32, (SEL_ROWS, BLOCK), 1)
    k_row = lax.broadcasted_iota(jnp.int32, (SEL_ROWS, BLOCK), 0)

    def score_chunk(c, masked):
        off = pl.multiple_of(c * SEL_ROWS, SEL_ROWS)
        kx = ikx_ref[pl.ds(off, SEL_ROWS), :]
        sc = jnp.zeros((SEL_ROWS, BLOCK), F32)
        for hp in range(IDX_HEADS // 2):
            xx = jnp.dot(kx, iqt_ref[hp], preferred_element_type=F32)
            sc = (sc + jnp.maximum(xx[:, :LANES], 0.0) * iw_rows[2 * hp]
                  + jnp.maximum(xx[:, LANES:], 0.0) * iw_rows[2 * hp + 1])
        key = _sortable_key(sc + 0.0)
        if masked:
            key = jnp.where(off + k_row <= q_pos, key, INT_MIN)
        keys_ref[pl.ds(off, SEL_ROWS), :] = key
        u = key ^ INT_MIN
        for g in range(SEL_ROWS // PLANE_ROWS):
            words = [u[g * PLANE_ROWS + 8 * j:g * PLANE_ROWS + 8 * j + 8] for j in range(32)]
            for i, plane in enumerate(_bit_transpose32(words)):
                planes_ref[c * (SEL_ROWS // PLANE_ROWS) + g, i] = plane

    _chunk_loop(lambda c, _: score_chunk(c, False), n_sel - 1, SCORE_UNROLL)
    score_chunk(n_sel - 1, True)

    def count_rows(pred):
        def cnt_chunk(c, acc):
            off = pl.multiple_of(c * CNT_ROWS, CNT_ROWS)
            hit = jnp.where(pred(keys_ref[pl.ds(off, CNT_ROWS), :], off), 1, 0)
            return acc + jnp.sum(hit.reshape(CNT_ROWS // 8, 8, BLOCK), axis=0)
        acc = _chunk_loop(cnt_chunk, n_cnt, COUNT_UNROLL, jnp.zeros((8, BLOCK), jnp.int32))
        return jnp.sum(acc, axis=0, keepdims=True)

    def count_ge(cand):
        return count_rows(lambda keys, off: keys >= cand)

    n_grp = n_sel * (SEL_ROWS // PLANE_ROWS)
    all_groups = eq_ref.shape[0]
    for g in range(all_groups):
        eq_ref[g] = jnp.where(g < n_grp, jnp.full((8, BLOCK), -1, jnp.int32), jnp.zeros((8, BLOCK), jnp.int32))

    def sweep(cur_plane, prev_plane, prev_keep):
        counts = []
        for g in range(all_groups):
            eq = eq_ref[g]
            if prev_plane is not None:
                eq = eq & (planes_ref[g, prev_plane] ^ ~prev_keep)
                eq_ref[g] = eq
            ones = eq if cur_plane is None else eq & planes_ref[g, cur_plane]
            counts.append(lax.population_count(ones))
        while len(counts) > 1:
            counts = [a + b for a, b in zip(counts[::2], counts[1::2])] + counts[len(counts) & ~1:]
        return jnp.sum(counts[0], axis=0, keepdims=True)

    def decide(bit, cnt_ones, thr_u, above):
        total = above + cnt_ones
        keep = jnp.where(total >= k_sel, -1, 0)
        return thr_u | (keep & bit), jnp.where(keep < 0, above, total), keep

    zeros = jnp.zeros((1, BLOCK), jnp.int32)
    state = decide(INT_MIN, sweep(0, None, None), zeros, zeros)

    def radix_step(it, state):
        thr_u, above, keep = state
        bit = lax.shift_left(jnp.int32(1), jnp.int32(31) - it)
        return decide(bit, sweep(it, it - 1, keep), thr_u, above)

    thr_u, above, keep = lax.fori_loop(1, 32, radix_step, state)
    cnt_thr = above + sweep(None, 31, keep)
    thr = thr_u ^ INT_MIN

    has_k = thr > INT_MIN
    thr = jnp.maximum(thr, INT_MIN + 1)

    jcut_ref[...] = jnp.full(jcut_ref.shape, 2 ** 31 - 1, jnp.int32)

    @pl.when(jnp.max(jnp.where(has_k, cnt_thr, 0)) > k_sel)
    def _():
        excess = jnp.where(has_k, count_ge(thr) - k_sel, 0)
        need = jnp.where(excess > 0, k_sel - count_ge(thr + 1), 1)
        rev_row = (seq - 1) - lax.broadcasted_iota(jnp.int32, (CNT_ROWS, BLOCK), 0)

        def tie_bisect(it, lo):
            cand = lo + lax.shift_left(jnp.int32(1), jnp.int32(seq.bit_length() - 2) - it)
            cnt = count_rows(lambda keys, off: (keys == thr) & (rev_row - off >= cand))
            return jnp.where(cnt >= need, cand, lo)

        rev = lax.fori_loop(0, seq.bit_length() - 1, tie_bisect, jnp.zeros((1, BLOCK), jnp.int32))
        jcut = jnp.where(excess > 0, (seq - 1) - rev, 2 ** 31 - 1)
        jcut_ref[...] = jnp.broadcast_to(jcut, jcut_ref.shape)

    jcut = jcut_ref[0:1, :]

    def emit(c, carry):
        off = pl.multiple_of(c * SEL_ROWS, SEL_ROWS)
        keys = keys_ref[pl.ds(off, SEL_ROWS), :]
        bar = jnp.where(off + k_row > jcut, thr + 1, thr)
        add = jnp.where(keys >= bar, 0.0, NEG)
        sel_ref[:, pl.ds(off, SEL_ROWS)] = jnp.concatenate(
            [add[j * BLOCK:(j + 1) * BLOCK].T for j in range(SEL_ROWS // BLOCK)], axis=1).astype(BF16)
        return carry

    _chunk_loop(emit, n_sel, SCORE_UNROLL)

    def fill(c, carry):
        off = pl.multiple_of(c * SEL_ROWS, SEL_ROWS)
        sel_ref[:, pl.ds(off, SEL_ROWS)] = jnp.full((BLOCK, SEL_ROWS), NEG, BF16)
        return carry

    lax.fori_loop(n_sel, seq // SEL_ROWS, fill, 0)


def _a_select(pf, batch, seq):
    t = pf.shape[0]
    nq = seq // BLOCK
    k_sel = min(TOPK_MAX, seq // 4)
    n_groups = seq // PLANE_ROWS
    nlane = IDX_HEADS * IDX_DIM
    r = np.arange(nlane)
    ph = np.zeros((nlane, IDX_HEADS * LANES), np.float32)
    plo = np.zeros((nlane, IDX_HEADS * LANES), np.float32)
    ph[r, (r // IDX_DIM) * LANES + r % IDX_DIM] = 1
    ph[r, (r // IDX_DIM) * LANES + 2 * IDX_DIM + r % IDX_DIM] = 1
    plo[r, (r // IDX_DIM) * LANES + IDX_DIM + r % IDX_DIM] = 1
    d = np.arange(IDX_DIM)
    pkh = np.zeros((LANES, LANES), np.float32)
    pkl = np.zeros((LANES, LANES), np.float32)
    pkh[d, d] = 1
    pkh[d, IDX_DIM + d] = 1
    pkl[d, 2 * IDX_DIM + d] = 1
    full = lambda shape: pl.BlockSpec(shape, lambda b, i: (0,) * len(shape))
    return pl.pallas_call(
        functools.partial(_a_select_kernel, seq=seq, k_sel=k_sel),
        grid=(batch, nq),
        in_specs=[pl.BlockSpec((BLOCK, 512), lambda b, i: (b * nq + i, 0)),
                  pl.BlockSpec((seq, LANES), lambda b, i: (b, nlane // LANES)),
                  full(ph.shape), full(plo.shape), full(pkh.shape), full(pkl.shape)],
        out_specs=pl.BlockSpec((BLOCK, seq), lambda b, i: (b * nq + i, 0)),
        out_shape=jax.ShapeDtypeStruct((t, seq), BF16),
        scratch_shapes=[pltpu.VMEM((seq, LANES), BF16), pltpu.VMEM((seq, LANES), jnp.int32),
                        pltpu.VMEM((n_groups, 32, 8, LANES), jnp.int32),
                        pltpu.VMEM((n_groups, 8, LANES), jnp.int32),
                        pltpu.VMEM((IDX_HEADS // 2, LANES, 2 * LANES), BF16),
                        pltpu.VMEM((8, LANES), jnp.int32)],
        compiler_params=_cparams(("parallel", "arbitrary")),
        name="a_select",
    )(pf, pf, jnp.asarray(ph, BF16), jnp.asarray(plo, BF16), jnp.asarray(pkh, BF16),
      jnp.asarray(pkl, BF16))


BAND_RBLK = 4


def _band_kernel(*refs, has_sink, want_lse, kv_div, rblk):
    if has_sink:
        sink_ref, refs = refs[0], refs[1:]
    q_ref, hm_ref, kp_ref, kc_ref, vp_ref, vc_ref, bias_ref, o_ref = refs[:8]
    first = pl.program_id(1) == 0
    lane = lax.broadcasted_iota(jnp.int32, (BLOCK, LANES), 1)
    col = lax.broadcasted_iota(jnp.int32, (BLOCK, 2 * BLOCK), 1)
    no_prev = jnp.logical_and(first, col < BLOCK)
    for p in range(BRANCH_WIDTH // LANES):
        kcol = (p // kv_div) * LANES
        kk = jnp.concatenate([kp_ref[:, kcol:kcol + LANES], kc_ref[:, kcol:kcol + LANES]], axis=0)
        vv = jnp.concatenate([vp_ref[:, kcol:kcol + LANES], vc_ref[:, kcol:kcol + LANES]], axis=0)
        for i in range(rblk):
            rows = slice(i * BLOCK, (i + 1) * BLOCK)
            q = q_ref[rows, p * LANES:(p + 1) * LANES]
            kblk = kk[i * BLOCK:(i + 2) * BLOCK]
            vblk = vv[i * BLOCK:(i + 2) * BLOCK]
            outs, lses = [], []
            for e in range(2):
                s = lax.dot_general(q * hm_ref[e], kblk, _CONTRACT_LANES, preferred_element_type=F32)
                s = s + bias_ref[2 * p + e]
                if i == 0:
                    s = jnp.where(no_prev, NEG, s)
                m = jnp.max(s, axis=1, keepdims=True)
                if has_sink:
                    sk = sink_ref[2 * p + e]
                    m = jnp.maximum(m, sk)
                pr = jnp.exp2(s - m)
                den = jnp.sum(pr, axis=1, keepdims=True)
                if has_sink:
                    den = den + jnp.exp2(sk - m)
                outs.append(jnp.dot(pr.astype(BF16), vblk, preferred_element_type=F32) / den)
                lses.append(m + jnp.log2(den))
            o_ref[rows, p * LANES:(p + 1) * LANES] = jnp.where(lane < HEAD_DIM, outs[0], outs[1]).astype(o_ref.dtype)
            if want_lse:
                refs[8][rows, p * LANES:(p + 1) * LANES] = jnp.where(lane < HEAD_DIM, lses[0], lses[1])


def _band_attn(q_arr, k_arr, v_arr, bias, nsub, sub_len, qcb, kcb, vcb, kv_w, sinks=None, want_lse=False):
    t = q_arr.shape[0]
    rblk = min(BAND_RBLK, sub_len // BLOCK)
    step_rows = rblk * BLOCK
    nbs = sub_len // step_rows
    kv_div = BRANCH_WIDTH // kv_w

    def cur(u, n):
        return u * nbs + n

    def prev(u, n):
        return jnp.where(n == 0, u * nbs * rblk, (u * nbs + n) * rblk - 1)

    in_specs = [
        pl.BlockSpec((step_rows, BRANCH_WIDTH), lambda u, n: (cur(u, n), qcb)),
        pl.BlockSpec((2, 1, LANES), lambda u, n: (0, 0, 0)),
        pl.BlockSpec((BLOCK, kv_w), lambda u, n: (prev(u, n), kcb)),
        pl.BlockSpec((step_rows, kv_w), lambda u, n: (cur(u, n), kcb)),
        pl.BlockSpec((BLOCK, kv_w), lambda u, n: (prev(u, n), vcb)),
        pl.BlockSpec((step_rows, kv_w), lambda u, n: (cur(u, n), vcb)),
        pl.BlockSpec(bias.shape, lambda u, n: (0, 0, 0)),
    ]
    args = [q_arr, _pair_masks(), k_arr, k_arr, v_arr, v_arr, bias]
    if sinks is not None:
        in_specs = [pl.BlockSpec(memory_space=pltpu.SMEM)] + in_specs
        args = [sinks] + args
    o_spec = pl.BlockSpec((step_rows, BRANCH_WIDTH), lambda u, n: (cur(u, n), 0))
    out_specs = [o_spec]
    out_shape = [jax.ShapeDtypeStruct((t, BRANCH_WIDTH), BF16)]
    if want_lse:
        out_specs.append(o_spec)
        out_shape.append(jax.ShapeDtypeStruct((t, BRANCH_WIDTH), F32))
    return pl.pallas_call(
        functools.partial(_band_kernel, has_sink=sinks is not None, want_lse=want_lse, kv_div=kv_div, rblk=rblk),
        grid=(nsub, nbs),
        in_specs=in_specs,
        out_specs=out_specs,
        out_shape=out_shape,
        compiler_params=_cparams(("parallel", "arbitrary")),
        name="band_attn",
    )(*args)


def _toeplitz(w, rows, cols):
    length = rows + cols - 1
    u = jnp.pad(w[..., ::-1], [(0, 0)] * (w.ndim - 1) + [(0, 1)])
    flat = jnp.broadcast_to(u[..., None, :], w.shape[:-1] + (rows, length + 1)).reshape(w.shape[:-1] + (-1,))
    skew = flat[..., :rows * length].reshape(w.shape[:-1] + (rows, length))
    return skew[..., rows - 1:rows - 1 + cols]


def _band_bias(table, step, max_dist):
    rel = BLOCK + np.arange(BLOCK)[:, None] - np.arange(2 * BLOCK)[None, :]
    rel_vec = np.arange(BLOCK - (2 * BLOCK - 1), 2 * BLOCK)
    bias = _toeplitz(table[_t5_bucket(jnp.asarray(rel_vec * step))].T.astype(F32), BLOCK, 2 * BLOCK)
    ok = (rel >= 0) & (rel <= max_dist)
    return jnp.where(jnp.asarray(ok)[None], bias, NEG)


def _merge_kernel(x_ref, ya_ref, yb0_ref, yb1_ref, yb2_ref, l0_ref, l1_ref, l2_ref, yc_ref, yd_ref,
                  za_ref, zb_ref, zc_ref, zd_ref, ga_ref, gb_ref, gc_ref, gd_ref, wb_ref, wo_ref, *rest):
    if len(rest) == 7:
        ng_ref, o_ref, xn_ref, y1_ref, y2_ref, s1_ref, s2_ref = rest
    else:
        ng_ref = xn_ref = None
        o_ref, y1_ref, y2_ref, s1_ref, s2_ref = rest
    ncol = BRANCH_WIDTH // LANES
    for src, dst in ((yb1_ref, y1_ref), (yb2_ref, y2_ref), (l1_ref, s1_ref), (l2_ref, s2_ref)):
        dil = src.shape[1]
        for r in range(dil):
            blk = src[0, r].astype(F32)
            for c in range(ncol):
                dst[c, pl.ds(r, src.shape[2], stride=dil), :] = blk[:, c * LANES:(c + 1) * LANES]
    wide = lambda ref: jnp.concatenate([ref[c] for c in range(ncol)], axis=1)
    l0, l1, l2 = l0_ref[0, 0], wide(s1_ref), wide(s2_ref)
    mx = jnp.maximum(jnp.maximum(l0, l1), l2)
    w0, w1, w2 = jnp.exp2(l0 - mx), jnp.exp2(l1 - mx), jnp.exp2(l2 - mx)
    yb = (w0 * yb0_ref[0, 0].astype(F32) + w1 * wide(y1_ref) + w2 * wide(y2_ref)) / (w0 + w1 + w2)
    ys = (ya_ref[...].astype(F32), yb, yc_ref[...].astype(F32), yd_ref[...].astype(F32))
    zs = (za_ref, zb_ref, zc_ref, zd_ref)
    gs = (ga_ref, gb_ref, gc_ref, gd_ref)
    merged = jnp.zeros(o_ref.shape, F32)
    for n in range(N_BRANCH):
        z = zs[n][...].astype(F32)
        u = (ys[n] * (z * jax.nn.sigmoid(z))).astype(BF16)
        proj = jnp.dot(u, wb_ref[n], preferred_element_type=F32)
        merged = merged + jax.nn.sigmoid(gs[n][...].astype(F32)) * proj
    x_new = x_ref[...] + jnp.dot(merged.astype(BF16), wo_ref[...], preferred_element_type=F32)
    o_ref[...] = x_new
    if xn_ref is not None:
        xn_ref[...] = _rms_bf16(x_new, ng_ref[...])


def _merge(x2, ya, yb, lb, yc, yd, pp, wb, wo, batch, seq, next_gain=None, tm=512):
    t = x2.shape[0]
    bw = BRANCH_WIDTH
    nsb = seq // tm
    row = lambda w, c: pl.BlockSpec((tm, w), lambda i: (i, c))

    def sub(dil):
        return pl.BlockSpec((1, dil, tm // dil, bw), lambda i: (i // nsb, 0, i % nsb, 0))

    dils = [dil for _, dil in B_GROUPS]
    yb = [a.reshape(batch, dil, seq // dil, bw) for a, dil in zip(yb, dils)]
    lb = [a.reshape(batch, dil, seq // dil, bw) for a, dil in zip(lb, dils)]
    in_specs = ([row(D_MODEL, 0), row(bw, 0)] + [sub(dil) for dil in dils] * 2 + [row(bw, 0), row(bw, 0)]
                + [row(bw, PP_AZ // bw), row(bw, PP_BZ // bw), row(bw, PP_CZ // bw), row(bw, PP_DZ // bw)]
                + [row(D_MODEL, PP_G // D_MODEL + n) for n in range(N_BRANCH)]
                + [pl.BlockSpec((N_BRANCH, bw, D_MODEL), lambda i: (0, 0, 0)),
                   pl.BlockSpec((D_MODEL, D_MODEL), lambda i: (0, 0))])
    args = [x2, ya, yb[0], yb[1], yb[2], lb[0], lb[1], lb[2], yc, yd, pp, pp, pp, pp, pp, pp, pp, pp, wb, wo]
    out_specs = [row(D_MODEL, 0)]
    out_shape = [jax.ShapeDtypeStruct((t, D_MODEL), F32)]
    if next_gain is not None:
        in_specs.append(pl.BlockSpec((1, D_MODEL), lambda i: (0, 0)))
        args.append(next_gain.reshape(1, D_MODEL))
        out_specs.append(row(D_MODEL, 0))
        out_shape.append(jax.ShapeDtypeStruct((t, D_MODEL), BF16))
    return pl.pallas_call(
        _merge_kernel,
        grid=(t // tm,),
        in_specs=in_specs,
        out_specs=out_specs,
        out_shape=out_shape,
        scratch_shapes=[pltpu.VMEM((bw // LANES, tm, LANES), F32)] * 4,
        compiler_params=_cparams(("parallel",)),
        name="merge",
    )(*args)


def _layer_weights(w_in, qk_gain_a, qk_gain_b, qk_gain_d):
    def cols(start, width):
        return w_in[:, :, start:start + width]

    n_bq = len(B_GROUPS) * B_HEADS * HEAD_DIM
    dk = cols(_D0 + 512, 128)
    dv = cols(_D0 + 640, 128)
    dup = lambda a: jnp.concatenate([a[:, :, :64], a[:, :, :64], a[:, :, 64:], a[:, :, 64:]], axis=-1)
    w_pn = jnp.concatenate([cols(_A0, 512), cols(_A0 + 512, 512), cols(_D0, 512), dup(dk), dup(dv)],
                           axis=-1).astype(BF16)
    w_pp = jnp.concatenate([cols(_G0, N_BRANCH * D_MODEL), cols(_A0 + 1024, 512), cols(_A0 + 1536, 512),
                            cols(_B0 + 3 * n_bq, 512), cols(_C0 + 416, 512), cols(_D0 + 768, 512)],
                           axis=-1).astype(BF16)
    w_bg = [jnp.concatenate([cols(_B0 + g * 512, 512), cols(_B0 + n_bq + g * 512, 512),
                             cols(_B0 + 2 * n_bq + g * 512, 512)], axis=-1).astype(BF16)
            for g in range(len(B_GROUPS))]
    depth = w_in.shape[0]
    zeros = lambda w: jnp.zeros((depth, D_MODEL, w), w_in.dtype)
    w_pf = jnp.concatenate([cols(_A0 + 2048, 296), zeros(216), cols(_C0, 416), zeros(96)], axis=-1).astype(BF16)

    scale = HEAD_DIM ** -0.5 * LOG2E
    tile = lambda g, reps: jnp.tile(g, (1, reps))
    hg = jnp.concatenate([tile(qk_gain_a[:, 0] * scale, 8), tile(qk_gain_a[:, 1], 8),
                          tile(qk_gain_d[:, 0] * scale, 8), tile(qk_gain_d[:, 1], 4),
                          jnp.ones((depth, 256), F32)], axis=-1)
    flag = jnp.concatenate([jnp.ones((PN_DV,), F32), jnp.zeros((PN_W - PN_DV,), F32)])
    hg_b = jnp.concatenate([tile(qk_gain_b[:, 0] * scale, 8), tile(qk_gain_b[:, 1], 8),
                            jnp.ones((depth, 512), F32)], axis=-1)
    flag_b = jnp.concatenate([jnp.ones((2 * BRANCH_WIDTH,), F32), jnp.zeros((BRANCH_WIDTH,), F32)])
    return w_pn, w_pp, w_bg, w_pf, hg, flag, hg_b, flag_b


def _c_weights(qk_gain_c, w_q_b, w_kv_b, seq):
    src_q, src_k, gain_idx, head, pe_src, rope_j, rope_half = _c_layout_tables()
    take = lambda w, src: jnp.where(jnp.asarray(src >= 0), jnp.take(w, jnp.asarray(np.maximum(src, 0)), axis=-1), 0.0)
    wq = take(w_q_b, src_q).astype(BF16)
    wk = take(w_kv_b, src_k).astype(BF16)
    v_src = np.array([h * (C_NOPE + C_V) + C_NOPE + c for h in range(C_HEADS) for c in range(C_V)])
    wv = jnp.take(w_kv_b, jnp.asarray(v_src), axis=-1).astype(BF16)
    ppe = np.zeros((LANES, C_QK_W), np.float32)
    ln = np.nonzero(pe_src >= 0)[0]
    ppe[pe_src[ln], ln] = 1
    hp = head[:C_PAIR]
    grp = ((hp[:, None] == hp[None, :]) & (hp[:, None] >= 0)).astype(np.float32)
    g_take = lambda g: jnp.where(jnp.asarray(gain_idx >= 0), jnp.take(g, jnp.asarray(np.maximum(gain_idx, 0)), axis=-1), 0.0)
    gq = g_take(qk_gain_c[:, 0]) * ((C_NOPE + C_ROPE) ** -0.5 * LOG2E)
    gk = g_take(qk_gain_c[:, 1])
    freq = ROPE_THETA ** (-jnp.arange(C_ROPE_HALF, dtype=F32) / C_ROPE_HALF)
    ang = jnp.arange(seq).astype(F32)[:, None] * freq[None, :]
    cos_j, sin_j = jnp.cos(ang), jnp.sin(ang)
    place = np.zeros((C_ROPE_HALF, C_PAIR), np.float32)
    lanes = np.nonzero(rope_j[:C_PAIR] >= 0)[0]
    place[rope_j[lanes], lanes] = 1
    is_rope = jnp.asarray(rope_j[:C_PAIR] >= 0)
    hi = lax.Precision.HIGHEST
    cos = jnp.where(is_rope, jnp.dot(cos_j, jnp.asarray(place), precision=hi), 1.0)
    sin = jnp.dot(sin_j, jnp.asarray(place), precision=hi)
    s1 = jnp.where(jnp.asarray(rope_half[:C_PAIR] == 0), -sin, 0.0)
    s2 = jnp.where(jnp.asarray(rope_half[:C_PAIR] == 1), sin, 0.0)
    return wq, wk, wv, jnp.asarray(ppe, BF16), jnp.asarray(grp, BF16), gq, gk, cos, s1, s2


def _a_bias_tiles(table):
    dist_vec = np.arange(-(BLOCK - 1), A_BIAS_TILES * BLOCK)
    vec = table[_t5_bucket(jnp.asarray(dist_vec))].T.astype(F32)
    windows = jnp.stack([vec[:, d * BLOCK:d * BLOCK + 2 * BLOCK - 1] for d in range(A_BIAS_TILES)], axis=1)
    return _toeplitz(windows, BLOCK, BLOCK)


def kernel(x, norm_gain, w_in, qk_gain_a, qk_gain_b, qk_gain_c, qk_gain_d, c_q_gain, c_kv_gain,
           w_q_b, w_kv_b, sinks, rel_bias, w_branch, w_out):
    batch, seq, d_model = x.shape
    depth = w_in.shape[0]
    t = batch * seq
    x2 = x.reshape(t, d_model)

    w_pn, w_pp, w_bg, w_pf, hg, flag, hg_b, flag_b = _layer_weights(w_in, qk_gain_a, qk_gain_b, qk_gain_d)
    wq, wk, wv, ppe, grp, gq, gk, cos, s1, s2 = _c_weights(qk_gain_c, w_q_b, w_kv_b, seq)
    wb = w_branch.astype(BF16)
    wo = w_out.astype(BF16)

    rel_bias2 = rel_bias * LOG2E
    bias_a = _a_bias_tiles(rel_bias2[:, :A_HEADS])
    bias_b = [_band_bias(rel_bias2[:, A_HEADS + g * B_HEADS:A_HEADS + (g + 1) * B_HEADS], dil, window // dil)
              for g, (window, dil) in enumerate(B_GROUPS)]
    bias_d = _band_bias(rel_bias2[:, N_BIAS_HEADS - D_HEADS:], 1, D_WINDOW - 1)

    xn = _norm(x2, norm_gain[0])
    for l in range(depth):
        pn = _proj(xn, w_pn[l], BF16, head_gain=hg[l], flag=flag)
        pp = _proj(xn, w_pp[l], BF16)
        pf = _proj(xn, w_pf[l], F32)

        sel = _a_select(pf, batch, seq)
        ya = _a_attn(pn, pp, sel, bias_a, batch, seq)

        yb, lb = [], []
        for g, (window, dil) in enumerate(B_GROUPS):
            bg = _proj(xn, w_bg[g][l], BF16, head_gain=hg_b[l], flag=flag_b, dil=dil, seq=seq)
            o, lse = _band_attn(bg, bg, bg, bias_b[g], batch * dil, seq // dil, 0, 1, 2, BRANCH_WIDTH,
                                want_lse=True)
            yb.append(o)
            lb.append(lse)

        qc, kc, vc = _c_prep(pf, c_q_gain[l].reshape(1, -1), c_kv_gain[l].reshape(1, -1), wq[l], wk[l], wv[l],
                             ppe, grp, gq[l].reshape(1, -1), gk[l].reshape(1, -1), cos, s1, s2, seq)
        yc = _c_attn(qc, kc, vc, batch, seq)

        (yd,) = _band_attn(pn, pn, pn, bias_d, batch, seq, PN_DQ // BRANCH_WIDTH, PN_DK // 256, PN_DV // 256,
                           256, sinks=sinks[l] * LOG2E)

        if l + 1 < depth:
            x2, xn = _merge(x2, ya, yb, lb, yc, yd, pp, wb[l], wo[l], batch, seq, next_gain=norm_gain[l + 1])
        else:
            (x2,) = _merge(x2, ya, yb, lb, yc, yd, pp, wb[l], wo[l], batch, seq)
    return x2.reshape(batch, seq, d_model)
```

```python
import functools
import math

import numpy as np
import jax
import jax.numpy as jnp
from jax import lax
from jax.experimental import pallas as pl
from jax.experimental.pallas import tpu as pltpu

F32 = jnp.float32
BF16 = jnp.bfloat16

D_MODEL = 1024
BLOCK = 128
HEAD_DIM = 64
N_BRANCH = 4
BRANCH_WIDTH = 512
EPS = 1e-6
A_HEADS = 8
IDX_HEADS = 8
IDX_DIM = 32
TOPK_MAX = 256
B_GROUPS = ((128, 1), (512, 4), (2048, 16))
B_HEADS = 8
C_HEADS = 8
C_NOPE = 64
C_ROPE = 32
C_V = 64
C_Q_LORA = 256
C_KV_LORA = 128
ROPE_THETA = 10000.0
D_HEADS = 8
D_KV_HEADS = 2
D_WINDOW = 128
NUM_BUCKETS = 32
MAX_DISTANCE = 2048
N_BIAS_HEADS = A_HEADS + len(B_GROUPS) * B_HEADS + D_HEADS

LANES = 128
MXU_WIDTH = 256
NEG = -1e30
LOG2E = math.log2(math.e)
INT_MIN = -(2 ** 31)

_A0 = 0
_B0 = 2344
_C0 = 7464
_D0 = 8392
_G0 = 9672

PN_AQ, PN_AK, PN_DQ, PN_DK, PN_DV, PN_W = 0, 512, 1024, 1536, 1792, 2048
PP_G, PP_AV, PP_AZ, PP_BZ, PP_CZ, PP_DZ, PP_W = 0, 4096, 4608, 5120, 5632, 6144, 6656
PF_W = 1024
BG_W = 3 * BRANCH_WIDTH

A_BIAS_TILES = MAX_DISTANCE // BLOCK + 2


def _cparams(sem, vmem_mb=48):
    return pltpu.CompilerParams(dimension_semantics=sem, vmem_limit_bytes=vmem_mb * 1024 * 1024)


def _t5_bucket(dist):
    max_exact = NUM_BUCKETS // 2
    d = jnp.maximum(dist, 0)
    logd = jnp.log(jnp.maximum(d, 1).astype(F32) / max_exact)
    large = max_exact + (logd / math.log(MAX_DISTANCE / max_exact) * (NUM_BUCKETS - max_exact)).astype(jnp.int32)
    return jnp.where(d < max_exact, d, jnp.minimum(large, NUM_BUCKETS - 1))


def _rms_bf16(x, gain):
    return (x * lax.rsqrt(jnp.mean(x * x, axis=-1, keepdims=True) + EPS) * gain).astype(BF16)


def _norm_kernel(x_ref, g_ref, o_ref):
    o_ref[...] = _rms_bf16(x_ref[...], g_ref[...])


def _norm(x2, gain, tm=1024):
    t, d = x2.shape
    return pl.pallas_call(
        _norm_kernel,
        grid=(t // tm,),
        in_specs=[pl.BlockSpec((tm, d), lambda i: (i, 0)), pl.BlockSpec((1, d), lambda i: (0, 0))],
        out_specs=pl.BlockSpec((tm, d), lambda i: (i, 0)),
        out_shape=jax.ShapeDtypeStruct((t, d), BF16),
        compiler_params=_cparams(("parallel",)),
        name="norm",
    )(x2, gain.reshape(1, d))


def _proj_kernel(xn_ref, w_ref, *rest, norm, dil):
    rest = list(rest)
    res_ref = rest.pop(-1) if dil is not None and dil > 1 else None
    if norm:
        hg_ref, flag_ref, bd_ref, o_ref = rest
    else:
        (o_ref,) = rest

    def emit(h):
        if dil is None:
            o_ref[...] = h.astype(o_ref.dtype)
        elif dil == 1:
            o_ref[0, 0] = h.astype(o_ref.dtype)
        else:
            sub = h.shape[0] // dil
            for c in range(h.shape[1] // LANES):
                res_ref[c] = h[:, c * LANES:(c + 1) * LANES]
            for r in range(dil):
                o_ref[0, r] = jnp.concatenate(
                    [res_ref[c, pl.ds(r, sub, stride=dil), :] for c in range(h.shape[1] // LANES)],
                    axis=1).astype(o_ref.dtype)

    h = jnp.dot(xn_ref[...], w_ref[...], preferred_element_type=F32)
    if norm:
        sq = (h * h).astype(BF16)
        slab = bd_ref.shape[0]
        ss = jnp.concatenate([jnp.dot(sq[:, c:c + slab], bd_ref[...], preferred_element_type=F32)
                              for c in range(0, sq.shape[1], slab)], axis=1)
        scale = lax.rsqrt(ss * (1.0 / HEAD_DIM) + EPS) * hg_ref[...]
        h = h * jnp.where(flag_ref[...] > 0, scale, 1.0)
    emit(h)


def _proj(xn, w, out_dtype, head_gain=None, flag=None, dil=None, seq=None, tm=1024, tn=512):
    t, d = xn.shape
    n = w.shape[1]
    norm = head_gain is not None
    scratch = []
    if dil is None:
        out_spec = pl.BlockSpec((tm, tn), lambda i, j: (i, j))
        out_shape = jax.ShapeDtypeStruct((t, n), out_dtype)
        kdil = None
    else:
        nsb = seq // tm
        out_spec = pl.BlockSpec((1, dil, tm // dil, tn), lambda i, j: (i // nsb, 0, i % nsb, j))
        out_shape = jax.ShapeDtypeStruct((t // seq, dil, seq // dil, n), out_dtype)
        kdil = dil
        if dil > 1:
            scratch.append(pltpu.VMEM((tn // LANES, tm, LANES), F32))
    in_specs = [pl.BlockSpec((tm, d), lambda i, j: (i, 0)),
                pl.BlockSpec((d, tn), lambda i, j: (0, j))]
    args = [xn, w]
    if norm:
        lane = np.arange(MXU_WIDTH)
        bd = jnp.asarray((lane[:, None] // HEAD_DIM == lane[None, :] // HEAD_DIM), BF16)
        in_specs += [pl.BlockSpec((1, tn), lambda i, j: (0, j)),
                     pl.BlockSpec((1, tn), lambda i, j: (0, j)),
                     pl.BlockSpec((MXU_WIDTH, MXU_WIDTH), lambda i, j: (0, 0))]
        args += [head_gain.reshape(1, n), flag.reshape(1, n), bd]
    out = pl.pallas_call(
        functools.partial(_proj_kernel, norm=norm, dil=kdil),
        grid=(t // tm, n // tn),
        in_specs=in_specs,
        out_specs=out_spec,
        out_shape=out_shape,
        scratch_shapes=scratch,
        compiler_params=_cparams(("parallel", "parallel")),
        name=("proj_norm" if norm else "proj") + ("" if dil is None else "_dil%d" % dil),
    )(*args)
    return out.reshape(t, n)


C_PAIR = 256
C_QK_W = (C_HEADS // 2) * C_PAIR
C_ROPE_HALF = C_ROPE // 2


def _c_lane(h, c):
    base = (h // 2) * C_PAIR
    e = h % 2
    if c < C_NOPE:
        return base + e * C_NOPE + c
    return base + 2 * C_NOPE + e * C_ROPE + (c - C_NOPE)


def _c_layout_tables():
    src_q = np.full(C_QK_W, -1, np.int64)
    src_k = np.full(C_QK_W, -1, np.int64)
    gain_idx = np.full(C_QK_W, -1, np.int64)
    head = np.full(C_QK_W, -1, np.int64)
    pe_src = np.full(C_QK_W, -1, np.int64)
    rope_j = np.full(C_QK_W, -1, np.int64)
    rope_half = np.zeros(C_QK_W, np.int64)
    for h in range(C_HEADS):
        for c in range(C_NOPE + C_ROPE):
            ln = _c_lane(h, c)
            src_q[ln] = h * (C_NOPE + C_ROPE) + c
            gain_idx[ln] = c
            head[ln] = h
            if c < C_NOPE:
                src_k[ln] = h * (C_NOPE + C_V) + c
            else:
                r = c - C_NOPE
                pe_src[ln] = r
                rope_j[ln] = r % C_ROPE_HALF
                rope_half[ln] = r // C_ROPE_HALF
    return src_q, src_k, gain_idx, head, pe_src, rope_j, rope_half


def _c_prep_kernel(pf_ref, cqg_ref, ckvg_ref, wq_ref, wk_ref, wv_ref, ppe_ref, grp_ref,
                   gq_ref, gk_ref, cos_ref, s1_ref, s2_ref, q_ref, k_ref, v_ref):
    blk = pf_ref[...]
    cq = blk[:, :C_Q_LORA]
    ckv = blk[:, C_Q_LORA:C_Q_LORA + C_KV_LORA]
    pe = blk[:, C_Q_LORA + C_KV_LORA:]

    def rms(v, g):
        return v * lax.rsqrt(jnp.mean(v * v, axis=-1, keepdims=True) + EPS) * g

    def head_norm_rope(raw, gain):
        sq = (raw * raw).astype(BF16)
        ss = jnp.concatenate([jnp.dot(sq[:, p * C_PAIR:(p + 1) * C_PAIR], grp_ref[...], preferred_element_type=F32)
                              for p in range(C_QK_W // C_PAIR)], axis=1)
        y = raw * lax.rsqrt(ss * (1.0 / (C_NOPE + C_ROPE)) + EPS) * gain
        up = pltpu.roll(y, C_QK_W - C_ROPE_HALF, 1)
        dn = pltpu.roll(y, C_ROPE_HALF, 1)
        wide = lambda ref: jnp.concatenate([ref[...]] * (C_QK_W // C_PAIR), axis=1)
        return y * wide(cos_ref) + up * wide(s1_ref) + dn * wide(s2_ref)

    cqn = rms(cq, cqg_ref[...]).astype(BF16)
    q_raw = jnp.dot(cqn, wq_ref[...], preferred_element_type=F32)
    q_ref[...] = head_norm_rope(q_raw, gq_ref[...]).astype(BF16)

    ckvn = rms(ckv, ckvg_ref[...]).astype(BF16)
    pe_hi = pe.astype(BF16)
    pe_lo = (pe - pe_hi.astype(F32)).astype(BF16)
    k_raw = (jnp.dot(ckvn, wk_ref[...], preferred_element_type=F32)
             + jnp.dot(pe_hi, ppe_ref[...], preferred_element_type=F32)
             + jnp.dot(pe_lo, ppe_ref[...], preferred_element_type=F32))
    k_ref[...] = head_norm_rope(k_raw, gk_ref[...]).astype(BF16)
    v_ref[...] = jnp.dot(ckvn, wv_ref[...], preferred_element_type=F32).astype(BF16)


def _c_prep(pf, cqg, ckvg, wq, wk, wv, ppe, grp, gq, gk, cos, s1, s2, seq, tm=512):
    t = pf.shape[0]
    nsb = seq // tm
    full = lambda shape: pl.BlockSpec(shape, lambda i: (0,) * len(shape))
    tab = pl.BlockSpec((tm, C_PAIR), lambda i: (i % nsb, 0))
    return pl.pallas_call(
        _c_prep_kernel,
        grid=(t // tm,),
        in_specs=[pl.BlockSpec((tm, 512), lambda i: (i, 1)),
                  full((1, C_Q_LORA)), full((1, C_KV_LORA)),
                  full((C_Q_LORA, C_QK_W)), full((C_KV_LORA, C_QK_W)), full((C_KV_LORA, BRANCH_WIDTH)),
                  full((LANES, C_QK_W)), full((C_PAIR, C_PAIR)),
                  full((1, C_QK_W)), full((1, C_QK_W)), tab, tab, tab],
        out_specs=[pl.BlockSpec((tm, C_QK_W), lambda i: (i, 0)),
                   pl.BlockSpec((tm, C_QK_W), lambda i: (i, 0)),
                   pl.BlockSpec((tm, BRANCH_WIDTH), lambda i: (i, 0))],
        out_shape=[jax.ShapeDtypeStruct((t, C_QK_W), BF16),
                   jax.ShapeDtypeStruct((t, C_QK_W), BF16),
                   jax.ShapeDtypeStruct((t, BRANCH_WIDTH), BF16)],
        compiler_params=_cparams(("parallel",)),
        name="c_prep",
    )(pf, cqg, ckvg, wq, wk, wv, ppe, grp, gq, gk, cos, s1, s2)


def _flash_update(e, s, v_ones, m_ref, acc_ref, rows=slice(None)):
    m_prev = m_ref[e, rows]
    m_new = jnp.maximum(m_prev, jnp.max(s, axis=1, keepdims=True))
    alpha = jnp.exp2(m_prev - m_new)
    p = jnp.exp2(s - jnp.concatenate([m_new] * (s.shape[1] // LANES), axis=1))
    acc_ref[e, rows] = (jnp.concatenate([alpha, alpha], axis=1) * acc_ref[e, rows]
                        + jnp.dot(p.astype(BF16), v_ones, preferred_element_type=F32))
    m_ref[e, rows] = m_new


def _flash_init(m_ref, acc_ref):
    m_ref[...] = jnp.full(m_ref.shape, NEG, F32)
    acc_ref[...] = jnp.zeros(acc_ref.shape, F32)


def _flash_finish(o_ref, acc_ref):
    lane = lax.broadcasted_iota(jnp.int32, o_ref.shape, 1)
    o0 = acc_ref[0, :, :LANES] / acc_ref[0, :, LANES:]
    o1 = acc_ref[1, :, :LANES] / acc_ref[1, :, LANES:]
    o_ref[...] = jnp.where(lane < HEAD_DIM, o0, o1).astype(o_ref.dtype)


def _with_ones(v_chunk):
    return jnp.concatenate([v_chunk, jnp.ones(v_chunk.shape, v_chunk.dtype)], axis=1)


_CONTRACT_LANES = (((1,), (1,)), ((), ()))


def _head_masks(width, ranges0, ranges1):
    m = np.zeros((2, 1, width), np.float32)
    for e, ranges in enumerate((ranges0, ranges1)):
        for lo, hi in ranges:
            m[e, 0, lo:hi] = 1
    return jnp.asarray(m, BF16)


def _pair_masks():
    return _head_masks(LANES, [(0, HEAD_DIM)], [(HEAD_DIM, LANES)])


def _transpose_keys(k_ref, kt_ref, rows=512):
    def body(c, carry):
        off = pl.multiple_of(c * rows, rows)
        kt_ref[:, pl.ds(off, rows)] = k_ref[pl.ds(off, rows), :].astype(F32).T.astype(kt_ref.dtype)
        return carry
    lax.fori_loop(0, k_ref.shape[0] // rows, body, 0)


def _chunk_loop(chunk, n, unroll, carry=None, slots=None):
    start = 0
    width = unroll
    while width >= 1:
        def body(g, carry, width=width, start=start):
            for j in range(width):
                carry = chunk(start + g * width + j, carry) if slots is None else \
                    chunk(start + g * width + j, carry, j)
            return carry
        groups = (n - start) // width
        carry = lax.fori_loop(0, groups, body, carry)
        start = start + groups * width
        width //= 2
    return carry


def _c_attn_kernel(q_ref, hm_ref, k_ref, v_ref, o_ref, kt_ref, m_ref, acc_ref, *, tq, tk, unroll):
    qi = pl.program_id(2)

    @pl.when(qi == 0)
    def _():
        _transpose_keys(k_ref, kt_ref)

    q = q_ref[...]
    qs = (q * hm_ref[0], q * hm_ref[1])
    _flash_init(m_ref, acc_ref)
    n_full = (qi * tq) // tk

    def chunk(c, row0=0, diagonal=False):
        off = pl.multiple_of(c * tk, tk)
        kc = kt_ref[:, pl.ds(off, tk)]
        vc = _with_ones(v_ref[pl.ds(off, tk), :])
        for e in range(2):
            if not diagonal:
                rows = slice(row0, tq)
                _flash_update(e, jnp.dot(qs[e][rows], kc, preferred_element_type=F32), vc, m_ref, acc_ref, rows)
                continue
            rows = slice(row0, row0 + tk)
            s = jnp.dot(qs[e][rows], kc, preferred_element_type=F32)
            s = jnp.where(lax.broadcasted_iota(jnp.int32, s.shape, 1) <= lax.broadcasted_iota(jnp.int32, s.shape, 0),
                          s, NEG)
            _flash_update(e, s, vc, m_ref, acc_ref, rows)
            if row0 + tk < tq:
                rows = slice(row0 + tk, tq)
                _flash_update(e, jnp.dot(qs[e][rows], kc, preferred_element_type=F32), vc, m_ref, acc_ref, rows)

    _chunk_loop(lambda c, _: chunk(c), n_full, unroll)
    for j in range(tq // tk):
        chunk(n_full + j, row0=j * tk, diagonal=True)
    _flash_finish(o_ref, acc_ref)


def _c_attn(qc, kc, vc, batch, seq, tq=1024, tk=512, unroll=4):
    t = qc.shape[0]
    nq = seq // tq
    npair = C_HEADS // 2
    return pl.pallas_call(
        functools.partial(_c_attn_kernel, tq=tq, tk=tk, unroll=unroll),
        grid=(batch, npair, nq),
        in_specs=[pl.BlockSpec((tq, C_PAIR), lambda b, p, i: (b * nq + i, p)),
                  pl.BlockSpec((2, 1, C_PAIR), lambda b, p, i: (0, 0, 0)),
                  pl.BlockSpec((seq, C_PAIR), lambda b, p, i: (b, p)),
                  pl.BlockSpec((seq, LANES), lambda b, p, i: (b, p))],
        out_specs=pl.BlockSpec((tq, LANES), lambda b, p, i: (b * nq + i, p)),
        out_shape=jax.ShapeDtypeStruct((t, BRANCH_WIDTH), BF16),
        scratch_shapes=[pltpu.VMEM((C_PAIR, seq), BF16),
                        pltpu.VMEM((2, tq, LANES), F32), pltpu.VMEM((2, tq, 2 * LANES), F32)],
        compiler_params=_cparams(("parallel", "parallel", "arbitrary")),
        name="c_attn",
    )(qc, _head_masks(C_PAIR, [(0, C_NOPE), (2 * C_NOPE, 2 * C_NOPE + C_ROPE)],
                      [(C_NOPE, 2 * C_NOPE), (2 * C_NOPE + C_ROPE, 2 * C_NOPE + 2 * C_ROPE)]), kc, vc)


def _a_attn_kernel(q_ref, hm_ref, k_ref, v_ref, sel_ref, bias_ref, o_ref, kt_ref, m_ref, acc_ref, msk_ref,
                   *, tq, tk, unroll):
    qi = pl.program_id(2)

    @pl.when(qi == 0)
    def _():
        _transpose_keys(k_ref, kt_ref)

    q = q_ref[...]
    qs = (q * hm_ref[0], q * hm_ref[1])
    _flash_init(m_ref, acc_ref)
    n_chunks = ((qi + 1) * tq + tk - 1) // tk

    def chunk(c, carry, slot):
        off = pl.multiple_of(c * tk, tk)
        kc = kt_ref[:, pl.ds(off, tk)]
        vc = _with_ones(v_ref[pl.ds(off, tk), :])
        msk_ref[slot] = sel_ref[:, pl.ds(off, tk)].astype(F32)
        for e in range(2):
            s = jnp.dot(qs[e], kc, preferred_element_type=F32)
            rows = []
            for i in range(tq // BLOCK):
                tiles = []
                for j in range(tk // BLOCK):
                    d = (qi * (tq // BLOCK) + i) - (c * (tk // BLOCK) + j)
                    d = jnp.clip(d, 0, A_BIAS_TILES - 1)
                    tiles.append(bias_ref[e, d])
                rows.append(jnp.concatenate(tiles, axis=1))
            s = s + jnp.concatenate(rows, axis=0) + msk_ref[slot]
            _flash_update(e, s, vc, m_ref, acc_ref)
        return carry

    _chunk_loop(chunk, n_chunks, unroll, slots=unroll)
    _flash_finish(o_ref, acc_ref)


def _a_attn(pn, pp, sel, bias_tiles, batch, seq, tq=512, tk=512, unroll=4):
    t = pn.shape[0]
    nq = seq // tq
    npair = A_HEADS // 2
    qcol, kcol, vcol = PN_AQ // LANES, PN_AK // LANES, PP_AV // LANES
    return pl.pallas_call(
        functools.partial(_a_attn_kernel, tq=tq, tk=tk, unroll=unroll),
        grid=(batch, npair, nq),
        in_specs=[pl.BlockSpec((tq, LANES), lambda b, p, i: (b * nq + i, qcol + p)),
                  pl.BlockSpec((2, 1, LANES), lambda b, p, i: (0, 0, 0)),
                  pl.BlockSpec((seq, LANES), lambda b, p, i: (b, kcol + p)),
                  pl.BlockSpec((seq, LANES), lambda b, p, i: (b, vcol + p)),
                  pl.BlockSpec((tq, seq), lambda b, p, i: (b * nq + i, 0)),
                  pl.BlockSpec((2, A_BIAS_TILES, BLOCK, BLOCK), lambda b, p, i: (p, 0, 0, 0))],
        out_specs=pl.BlockSpec((tq, LANES), lambda b, p, i: (b * nq + i, p)),
        out_shape=jax.ShapeDtypeStruct((t, BRANCH_WIDTH), BF16),
        scratch_shapes=[pltpu.VMEM((LANES, seq), BF16),
                        pltpu.VMEM((2, tq, LANES), F32), pltpu.VMEM((2, tq, 2 * LANES), F32),
                        pltpu.VMEM((unroll, tq, tk), F32)],
        compiler_params=_cparams(("parallel", "parallel", "arbitrary")),
        name="a_attn",
    )(pn, _pair_masks(), pn, pp, sel, bias_tiles)


def _sortable_key(score):
    bits = pltpu.bitcast(score, jnp.int32)
    return bits ^ ((bits >> 31) & jnp.int32(0x7FFFFFFF))


def _bit_transpose32(words):
    a = list(words)
    j, m = 16, 0x0000FFFF
    while j:
        for k in range(32):
            if k & j:
                continue
            t = (a[k] ^ lax.shift_right_logical(a[k + j], jnp.int32(j))) & m
            a[k] = a[k] ^ t
            a[k + j] = a[k + j] ^ lax.shift_left(t, jnp.int32(j))
        j >>= 1
        m ^= (m << j) & 0xFFFFFFFF
    return a


SEL_ROWS = 512
PLANE_ROWS = 256
SCORE_UNROLL = 2


def _a_select_kernel(qblk_ref, kblk_ref, ph_ref, pl_ref, pkh_ref, pkl_ref, sel_ref,
                     ikx_ref, keys_ref, planes_ref, eq_ref, iqt_ref, jcut_ref, *, seq, k_sel):
    qi = pl.program_id(1)
    n_sel = (qi * BLOCK + BLOCK + SEL_ROWS - 1) // SEL_ROWS

    @pl.when(qi == 0)
    def _():
        def prep(c, carry):
            off = pl.multiple_of(c * 512, 512)
            kv = kblk_ref[pl.ds(off, 512), :]
            hi = kv.astype(BF16)
            lo = (kv - hi.astype(F32)).astype(BF16)
            ikx_ref[pl.ds(off, 512), :] = (
                jnp.dot(hi, pkh_ref[...], preferred_element_type=F32)
                + jnp.dot(lo, pkl_ref[...], preferred_element_type=F32)).astype(BF16)
            return carry
        lax.fori_loop(0, seq // 512, prep, 0)

        def clear(g, carry):
            planes_ref[g] = jnp.zeros(planes_ref.shape[1:], jnp.int32)
            return carry
        lax.fori_loop(0, planes_ref.shape[0], clear, 0)

    qb = qblk_ref[...]
    iq = qb[:, :IDX_HEADS * IDX_DIM]
    iq_hi = iq.astype(BF16)
    iq_lo = (iq - iq_hi.astype(F32)).astype(BF16)
    iqx = (jnp.dot(iq_hi, ph_ref[...], preferred_element_type=F32)
           + jnp.dot(iq_lo, pl_ref[...], preferred_element_type=F32))
    for h in range(IDX_HEADS):
        iqt_ref[h // 2, :, (h % 2) * LANES:(h % 2 + 1) * LANES] = iqx[:, h * LANES:(h + 1) * LANES].T.astype(BF16)
    iw_t = qb[:, IDX_HEADS * IDX_DIM:IDX_HEADS * IDX_DIM + LANES].T
    iw_rows = [iw_t[IDX_DIM + h:IDX_DIM + h + 1, :] for h in range(IDX_HEADS)]
    q_pos = qi * BLOCK + lax.broadcasted_iota(jnp.int32, (SEL_ROWS, BLOCK), 1)
    k_row = lax.broadcasted_iota(jnp.int32, (SEL_ROWS, BLOCK), 0)

    def score_chunk(c, masked):
        off = pl.multiple_of(c * SEL_ROWS, SEL_ROWS)
        kx = ikx_ref[pl.ds(off, SEL_ROWS), :]
        sc = jnp.zeros((SEL_ROWS, BLOCK), F32)
        for hp in range(IDX_HEADS // 2):
            xx = jnp.dot(kx, iqt_ref[hp], preferred_element_type=F32)
            sc = (sc + jnp.maximum(xx[:, :LANES], 0.0) * iw_rows[2 * hp]
                  + jnp.maximum(xx[:, LANES:], 0.0) * iw_rows[2 * hp + 1])
        key = _sortable_key(sc + 0.0)
        if masked:
            key = jnp.where(off + k_row <= q_pos, key, INT_MIN)
        keys_ref[pl.ds(off, SEL_ROWS), :] = key
        u = key ^ INT_MIN
        for g in range(SEL_ROWS // PLANE_ROWS):
            words = [u[g * PLANE_ROWS + 8 * j:g * PLANE_ROWS + 8 * j + 8] for j in range(32)]
            for i, plane in enumerate(_bit_transpose32(words)):
                planes_ref[c * (SEL_ROWS // PLANE_ROWS) + g, i] = plane

    _chunk_loop(lambda c, _: score_chunk(c, False), n_sel - 1, SCORE_UNROLL)
    score_chunk(n_sel - 1, True)

    def per_query_total(counts):
        while len(counts) > 1:
            counts = [a + b for a, b in zip(counts[::2], counts[1::2])] + counts[len(counts) & ~1:]
        return jnp.sum(counts[0], axis=0, keepdims=True)

    n_grp = n_sel * (SEL_ROWS // PLANE_ROWS)
    all_groups = eq_ref.shape[0]
    for g in range(all_groups):
        eq_ref[g] = jnp.where(g < n_grp, jnp.full((8, BLOCK), -1, jnp.int32), jnp.zeros((8, BLOCK), jnp.int32))

    def sweep(cur_plane, prev_plane, prev_keep):
        counts = []
        for g in range(all_groups):
            eq = eq_ref[g]
            if prev_plane is not None:
                eq = eq & (planes_ref[g, prev_plane] ^ ~prev_keep)
                eq_ref[g] = eq
            ones = eq if cur_plane is None else eq & planes_ref[g, cur_plane]
            counts.append(lax.population_count(ones))
        return per_query_total(counts)

    def decide(bit, cnt_ones, thr_u, above):
        total = above + cnt_ones
        keep = jnp.where(total >= k_sel, -1, 0)
        return thr_u | (keep & bit), jnp.where(keep < 0, above, total), keep

    zeros = jnp.zeros((1, BLOCK), jnp.int32)
    state = decide(INT_MIN, sweep(0, None, None), zeros, zeros)

    def radix_step(it, state):
        thr_u, above, keep = state
        bit = lax.shift_left(jnp.int32(1), jnp.int32(31) - it)
        return decide(bit, sweep(it, it - 1, keep), thr_u, above)

    thr_u, above, keep = lax.fori_loop(1, 32, radix_step, state)
    cnt_thr = above + sweep(None, 31, keep)
    thr = thr_u ^ INT_MIN

    has_k = thr > INT_MIN
    thr = jnp.maximum(thr, INT_MIN + 1)

    jcut_ref[...] = jnp.full(jcut_ref.shape, 2 ** 31 - 1, jnp.int32)

    @pl.when(jnp.max(jnp.where(has_k, cnt_thr, 0)) > k_sel)
    def _():
        excess = jnp.where(has_k, cnt_thr - k_sel, 0)
        need = jnp.where(excess > 0, k_sel - above, 1)
        sub = lax.broadcasted_iota(jnp.int32, (8, BLOCK), 0)

        def ties_below(t):
            counts = []
            for g in range(all_groups):
                words = jnp.clip((t - g * PLANE_ROWS - sub + 7) >> 3, 0, 32)
                top = jnp.where(words > 0, lax.shift_left(jnp.int32(-1), 32 - words), 0)
                counts.append(lax.population_count(eq_ref[g] & top))
            return per_query_total(counts)

        def tie_bisect(it, t):
            cand = t + lax.shift_left(jnp.int32(1), jnp.int32(seq.bit_length() - 2) - it)
            return jnp.where(ties_below(cand) < need, cand, t)

        last = lax.fori_loop(0, seq.bit_length() - 1, tie_bisect, jnp.zeros((1, BLOCK), jnp.int32))
        jcut = jnp.where(excess > 0, last, 2 ** 31 - 1)
        jcut_ref[...] = jnp.broadcast_to(jcut, jcut_ref.shape)

    jcut = jcut_ref[0:1, :]

    def emit(c, carry):
        off = pl.multiple_of(c * SEL_ROWS, SEL_ROWS)
        keys = keys_ref[pl.ds(off, SEL_ROWS), :]
        bar = jnp.where(off + k_row > jcut, thr + 1, thr)
        add = jnp.where(keys >= bar, 0.0, NEG)
        sel_ref[:, pl.ds(off, SEL_ROWS)] = jnp.concatenate(
            [add[j * BLOCK:(j + 1) * BLOCK].T for j in range(SEL_ROWS // BLOCK)], axis=1).astype(BF16)
        return carry

    _chunk_loop(emit, n_sel, SCORE_UNROLL)

    def fill(c, carry):
        off = pl.multiple_of(c * SEL_ROWS, SEL_ROWS)
        sel_ref[:, pl.ds(off, SEL_ROWS)] = jnp.full((BLOCK, SEL_ROWS), NEG, BF16)
        return carry

    lax.fori_loop(n_sel, seq // SEL_ROWS, fill, 0)


def _a_select(pf, batch, seq):
    t = pf.shape[0]
    nq = seq // BLOCK
    k_sel = min(TOPK_MAX, seq // 4)
    n_groups = seq // PLANE_ROWS
    nlane = IDX_HEADS * IDX_DIM
    r = np.arange(nlane)
    ph = np.zeros((nlane, IDX_HEADS * LANES), np.float32)
    plo = np.zeros((nlane, IDX_HEADS * LANES), np.float32)
    ph[r, (r // IDX_DIM) * LANES + r % IDX_DIM] = 1
    ph[r, (r // IDX_DIM) * LANES + 2 * IDX_DIM + r % IDX_DIM] = 1
    plo[r, (r // IDX_DIM) * LANES + IDX_DIM + r % IDX_DIM] = 1
    d = np.arange(IDX_DIM)
    pkh = np.zeros((LANES, LANES), np.float32)
    pkl = np.zeros((LANES, LANES), np.float32)
    pkh[d, d] = 1
    pkh[d, IDX_DIM + d] = 1
    pkl[d, 2 * IDX_DIM + d] = 1
    full = lambda shape: pl.BlockSpec(shape, lambda b, i: (0,) * len(shape))
    return pl.pallas_call(
        functools.partial(_a_select_kernel, seq=seq, k_sel=k_sel),
        grid=(batch, nq),
        in_specs=[pl.BlockSpec((BLOCK, 512), lambda b, i: (b * nq + i, 0)),
                  pl.BlockSpec((seq, LANES), lambda b, i: (b, nlane // LANES)),
                  full(ph.shape), full(plo.shape), full(pkh.shape), full(pkl.shape)],
        out_specs=pl.BlockSpec((BLOCK, seq), lambda b, i: (b * nq + i, 0)),
        out_shape=jax.ShapeDtypeStruct((t, seq), BF16),
        scratch_shapes=[pltpu.VMEM((seq, LANES), BF16), pltpu.VMEM((seq, LANES), jnp.int32),
                        pltpu.VMEM((n_groups, 32, 8, LANES), jnp.int32),
                        pltpu.VMEM((n_groups, 8, LANES), jnp.int32),
                        pltpu.VMEM((IDX_HEADS // 2, LANES, 2 * LANES), BF16),
                        pltpu.VMEM((8, LANES), jnp.int32)],
        compiler_params=_cparams(("parallel", "arbitrary")),
        name="a_select",
    )(pf, pf, jnp.asarray(ph, BF16), jnp.asarray(plo, BF16), jnp.asarray(pkh, BF16),
      jnp.asarray(pkl, BF16))


BAND_RBLK = 4


def _band_kernel(*refs, has_sink, want_lse, kv_div, rblk):
    if has_sink:
        sink_ref, refs = refs[0], refs[1:]
    q_ref, hm_ref, kp_ref, kc_ref, vp_ref, vc_ref, bias_ref, o_ref = refs[:8]
    first = pl.program_id(1) == 0
    lane = lax.broadcasted_iota(jnp.int32, (BLOCK, LANES), 1)
    col = lax.broadcasted_iota(jnp.int32, (BLOCK, 2 * BLOCK), 1)
    no_prev = jnp.logical_and(first, col < BLOCK)
    for p in range(BRANCH_WIDTH // LANES):
        kcol = (p // kv_div) * LANES
        kk = jnp.concatenate([kp_ref[:, kcol:kcol + LANES], kc_ref[:, kcol:kcol + LANES]], axis=0)
        vv = jnp.concatenate([vp_ref[:, kcol:kcol + LANES], vc_ref[:, kcol:kcol + LANES]], axis=0)
        for i in range(rblk):
            rows = slice(i * BLOCK, (i + 1) * BLOCK)
            q = q_ref[rows, p * LANES:(p + 1) * LANES]
            kblk = kk[i * BLOCK:(i + 2) * BLOCK]
            vblk = vv[i * BLOCK:(i + 2) * BLOCK]
            outs, lses = [], []
            for e in range(2):
                s = lax.dot_general(q * hm_ref[e], kblk, _CONTRACT_LANES, preferred_element_type=F32)
                s = s + bias_ref[2 * p + e]
                if i == 0:
                    s = jnp.where(no_prev, NEG, s)
                m = jnp.max(s, axis=1, keepdims=True)
                if has_sink:
                    sk = sink_ref[2 * p + e]
                    m = jnp.maximum(m, sk)
                pr = jnp.exp2(s - m)
                den = jnp.sum(pr, axis=1, keepdims=True)
                if has_sink:
                    den = den + jnp.exp2(sk - m)
                outs.append(jnp.dot(pr.astype(BF16), vblk, preferred_element_type=F32) / den)
                lses.append(m + jnp.log2(den))
            o_ref[rows, p * LANES:(p + 1) * LANES] = jnp.where(lane < HEAD_DIM, outs[0], outs[1]).astype(o_ref.dtype)
            if want_lse:
                refs[8][rows, p * LANES:(p + 1) * LANES] = jnp.where(lane < HEAD_DIM, lses[0], lses[1])


def _band_attn(q_arr, k_arr, v_arr, bias, nsub, sub_len, qcb, kcb, vcb, kv_w, sinks=None, want_lse=False):
    t = q_arr.shape[0]
    rblk = min(BAND_RBLK, sub_len // BLOCK)
    step_rows = rblk * BLOCK
    nbs = sub_len // step_rows
    kv_div = BRANCH_WIDTH // kv_w

    def cur(u, n):
        return u * nbs + n

    def prev(u, n):
        return jnp.where(n == 0, u * nbs * rblk, (u * nbs + n) * rblk - 1)

    in_specs = [
        pl.BlockSpec((step_rows, BRANCH_WIDTH), lambda u, n: (cur(u, n), qcb)),
        pl.BlockSpec((2, 1, LANES), lambda u, n: (0, 0, 0)),
        pl.BlockSpec((BLOCK, kv_w), lambda u, n: (prev(u, n), kcb)),
        pl.BlockSpec((step_rows, kv_w), lambda u, n: (cur(u, n), kcb)),
        pl.BlockSpec((BLOCK, kv_w), lambda u, n: (prev(u, n), vcb)),
        pl.BlockSpec((step_rows, kv_w), lambda u, n: (cur(u, n), vcb)),
        pl.BlockSpec(bias.shape, lambda u, n: (0, 0, 0)),
    ]
    args = [q_arr, _pair_masks(), k_arr, k_arr, v_arr, v_arr, bias]
    if sinks is not None:
        in_specs = [pl.BlockSpec(memory_space=pltpu.SMEM)] + in_specs
        args = [sinks] + args
    o_spec = pl.BlockSpec((step_rows, BRANCH_WIDTH), lambda u, n: (cur(u, n), 0))
    out_specs = [o_spec]
    out_shape = [jax.ShapeDtypeStruct((t, BRANCH_WIDTH), BF16)]
    if want_lse:
        out_specs.append(o_spec)
        out_shape.append(jax.ShapeDtypeStruct((t, BRANCH_WIDTH), F32))
    return pl.pallas_call(
        functools.partial(_band_kernel, has_sink=sinks is not None, want_lse=want_lse, kv_div=kv_div, rblk=rblk),
        grid=(nsub, nbs),
        in_specs=in_specs,
        out_specs=out_specs,
        out_shape=out_shape,
        compiler_params=_cparams(("parallel", "arbitrary")),
        name="band_attn",
    )(*args)


def _toeplitz(w, rows, cols):
    length = rows + cols - 1
    u = jnp.pad(w[..., ::-1], [(0, 0)] * (w.ndim - 1) + [(0, 1)])
    flat = jnp.broadcast_to(u[..., None, :], w.shape[:-1] + (rows, length + 1)).reshape(w.shape[:-1] + (-1,))
    skew = flat[..., :rows * length].reshape(w.shape[:-1] + (rows, length))
    return skew[..., rows - 1:rows - 1 + cols]


def _band_bias(table, step, max_dist):
    rel = BLOCK + np.arange(BLOCK)[:, None] - np.arange(2 * BLOCK)[None, :]
    rel_vec = np.arange(BLOCK - (2 * BLOCK - 1), 2 * BLOCK)
    bias = _toeplitz(table[_t5_bucket(jnp.asarray(rel_vec * step))].T.astype(F32), BLOCK, 2 * BLOCK)
    ok = (rel >= 0) & (rel <= max_dist)
    return jnp.where(jnp.asarray(ok)[None], bias, NEG)


def _merge_kernel(x_ref, ya_ref, yb0_ref, yb1_ref, yb2_ref, l0_ref, l1_ref, l2_ref, yc_ref, yd_ref,
                  za_ref, zb_ref, zc_ref, zd_ref, ga_ref, gb_ref, gc_ref, gd_ref, wb_ref, wo_ref, *rest):
    if len(rest) == 7:
        ng_ref, o_ref, xn_ref, y1_ref, y2_ref, s1_ref, s2_ref = rest
    else:
        ng_ref = xn_ref = None
        o_ref, y1_ref, y2_ref, s1_ref, s2_ref = rest
    ncol = BRANCH_WIDTH // LANES
    for src, dst in ((yb1_ref, y1_ref), (yb2_ref, y2_ref), (l1_ref, s1_ref), (l2_ref, s2_ref)):
        dil = src.shape[1]
        for r in range(dil):
            blk = src[0, r].astype(F32)
            for c in range(ncol):
                dst[c, pl.ds(r, src.shape[2], stride=dil), :] = blk[:, c * LANES:(c + 1) * LANES]
    wide = lambda ref: jnp.concatenate([ref[c] for c in range(ncol)], axis=1)
    l0, l1, l2 = l0_ref[0, 0], wide(s1_ref), wide(s2_ref)
    mx = jnp.maximum(jnp.maximum(l0, l1), l2)
    w0, w1, w2 = jnp.exp2(l0 - mx), jnp.exp2(l1 - mx), jnp.exp2(l2 - mx)
    yb = (w0 * yb0_ref[0, 0].astype(F32) + w1 * wide(y1_ref) + w2 * wide(y2_ref)) / (w0 + w1 + w2)
    ys = (ya_ref[...].astype(F32), yb, yc_ref[...].astype(F32), yd_ref[...].astype(F32))
    zs = (za_ref, zb_ref, zc_ref, zd_ref)
    gs = (ga_ref, gb_ref, gc_ref, gd_ref)
    merged = jnp.zeros(o_ref.shape, F32)
    for n in range(N_BRANCH):
        z = zs[n][...].astype(F32)
        u = (ys[n] * (z * jax.nn.sigmoid(z))).astype(BF16)
        proj = jnp.dot(u, wb_ref[n], preferred_element_type=F32)
        merged = merged + jax.nn.sigmoid(gs[n][...].astype(F32)) * proj
    x_new = x_ref[...] + jnp.dot(merged.astype(BF16), wo_ref[...], preferred_element_type=F32)
    o_ref[...] = x_new
    if xn_ref is not None:
        xn_ref[...] = _rms_bf16(x_new, ng_ref[...])


def _merge(x2, ya, yb, lb, yc, yd, pp, wb, wo, batch, seq, next_gain=None, tm=512):
    t = x2.shape[0]
    bw = BRANCH_WIDTH
    nsb = seq // tm
    row = lambda w, c: pl.BlockSpec((tm, w), lambda i: (i, c))

    def sub(dil):
        return pl.BlockSpec((1, dil, tm // dil, bw), lambda i: (i // nsb, 0, i % nsb, 0))

    dils = [dil for _, dil in B_GROUPS]
    yb = [a.reshape(batch, dil, seq // dil, bw) for a, dil in zip(yb, dils)]
    lb = [a.reshape(batch, dil, seq // dil, bw) for a, dil in zip(lb, dils)]
    in_specs = ([row(D_MODEL, 0), row(bw, 0)] + [sub(dil) for dil in dils] * 2 + [row(bw, 0), row(bw, 0)]
                + [row(bw, PP_AZ // bw), row(bw, PP_BZ // bw), row(bw, PP_CZ // bw), row(bw, PP_DZ // bw)]
                + [row(D_MODEL, PP_G // D_MODEL + n) for n in range(N_BRANCH)]
                + [pl.BlockSpec((N_BRANCH, bw, D_MODEL), lambda i: (0, 0, 0)),
                   pl.BlockSpec((D_MODEL, D_MODEL), lambda i: (0, 0))])
    args = [x2, ya, yb[0], yb[1], yb[2], lb[0], lb[1], lb[2], yc, yd, pp, pp, pp, pp, pp, pp, pp, pp, wb, wo]
    out_specs = [row(D_MODEL, 0)]
    out_shape = [jax.ShapeDtypeStruct((t, D_MODEL), F32)]
    if next_gain is not None:
        in_specs.append(pl.BlockSpec((1, D_MODEL), lambda i: (0, 0)))
        args.append(next_gain.reshape(1, D_MODEL))
        out_specs.append(row(D_MODEL, 0))
        out_shape.append(jax.ShapeDtypeStruct((t, D_MODEL), BF16))
    return pl.pallas_call(
        _merge_kernel,
        grid=(t // tm,),
        in_specs=in_specs,
        out_specs=out_specs,
        out_shape=out_shape,
        scratch_shapes=[pltpu.VMEM((bw // LANES, tm, LANES), F32)] * 4,
        compiler_params=_cparams(("parallel",)),
        name="merge",
    )(*args)


def _layer_weights(w_in, qk_gain_a, qk_gain_b, qk_gain_d):
    def cols(start, width):
        return w_in[:, :, start:start + width]

    n_bq = len(B_GROUPS) * B_HEADS * HEAD_DIM
    dk = cols(_D0 + 512, 128)
    dv = cols(_D0 + 640, 128)
    dup = lambda a: jnp.concatenate([a[:, :, :64], a[:, :, :64], a[:, :, 64:], a[:, :, 64:]], axis=-1)
    w_pn = jnp.concatenate([cols(_A0, 512), cols(_A0 + 512, 512), cols(_D0, 512), dup(dk), dup(dv)],
                           axis=-1).astype(BF16)
    w_pp = jnp.concatenate([cols(_G0, N_BRANCH * D_MODEL), cols(_A0 + 1024, 512), cols(_A0 + 1536, 512),
                            cols(_B0 + 3 * n_bq, 512), cols(_C0 + 416, 512), cols(_D0 + 768, 512)],
                           axis=-1).astype(BF16)
    w_bg = [jnp.concatenate([cols(_B0 + g * 512, 512), cols(_B0 + n_bq + g * 512, 512),
                             cols(_B0 + 2 * n_bq + g * 512, 512)], axis=-1).astype(BF16)
            for g in range(len(B_GROUPS))]
    depth = w_in.shape[0]
    zeros = lambda w: jnp.zeros((depth, D_MODEL, w), w_in.dtype)
    w_pf = jnp.concatenate([cols(_A0 + 2048, 296), zeros(216), cols(_C0, 416), zeros(96)], axis=-1).astype(BF16)

    scale = HEAD_DIM ** -0.5 * LOG2E
    tile = lambda g, reps: jnp.tile(g, (1, reps))
    hg = jnp.concatenate([tile(qk_gain_a[:, 0] * scale, 8), tile(qk_gain_a[:, 1], 8),
                          tile(qk_gain_d[:, 0] * scale, 8), tile(qk_gain_d[:, 1], 4),
                          jnp.ones((depth, 256), F32)], axis=-1)
    flag = jnp.concatenate([jnp.ones((PN_DV,), F32), jnp.zeros((PN_W - PN_DV,), F32)])
    hg_b = jnp.concatenate([tile(qk_gain_b[:, 0] * scale, 8), tile(qk_gain_b[:, 1], 8),
                            jnp.ones((depth, 512), F32)], axis=-1)
    flag_b = jnp.concatenate([jnp.ones((2 * BRANCH_WIDTH,), F32), jnp.zeros((BRANCH_WIDTH,), F32)])
    return w_pn, w_pp, w_bg, w_pf, hg, flag, hg_b, flag_b


def _c_weights(qk_gain_c, w_q_b, w_kv_b, seq):
    src_q, src_k, gain_idx, head, pe_src, rope_j, rope_half = _c_layout_tables()
    take = lambda w, src: jnp.where(jnp.asarray(src >= 0), jnp.take(w, jnp.asarray(np.maximum(src, 0)), axis=-1), 0.0)
    wq = take(w_q_b, src_q).astype(BF16)
    wk = take(w_kv_b, src_k).astype(BF16)
    v_src = np.array([h * (C_NOPE + C_V) + C_NOPE + c for h in range(C_HEADS) for c in range(C_V)])
    wv = jnp.take(w_kv_b, jnp.asarray(v_src), axis=-1).astype(BF16)
    ppe = np.zeros((LANES, C_QK_W), np.float32)
    ln = np.nonzero(pe_src >= 0)[0]
    ppe[pe_src[ln], ln] = 1
    hp = head[:C_PAIR]
    grp = ((hp[:, None] == hp[None, :]) & (hp[:, None] >= 0)).astype(np.float32)
    g_take = lambda g: jnp.where(jnp.asarray(gain_idx >= 0), jnp.take(g, jnp.asarray(np.maximum(gain_idx, 0)), axis=-1), 0.0)
    gq = g_take(qk_gain_c[:, 0]) * ((C_NOPE + C_ROPE) ** -0.5 * LOG2E)
    gk = g_take(qk_gain_c[:, 1])
    freq = ROPE_THETA ** (-jnp.arange(C_ROPE_HALF, dtype=F32) / C_ROPE_HALF)
    ang = jnp.arange(seq).astype(F32)[:, None] * freq[None, :]
    cos_j, sin_j = jnp.cos(ang), jnp.sin(ang)
    place = np.zeros((C_ROPE_HALF, C_PAIR), np.float32)
    lanes = np.nonzero(rope_j[:C_PAIR] >= 0)[0]
    place[rope_j[lanes], lanes] = 1
    is_rope = jnp.asarray(rope_j[:C_PAIR] >= 0)
    hi = lax.Precision.HIGHEST
    cos = jnp.where(is_rope, jnp.dot(cos_j, jnp.asarray(place), precision=hi), 1.0)
    sin = jnp.dot(sin_j, jnp.asarray(place), precision=hi)
    s1 = jnp.where(jnp.asarray(rope_half[:C_PAIR] == 0), -sin, 0.0)
    s2 = jnp.where(jnp.asarray(rope_half[:C_PAIR] == 1), sin, 0.0)
    return wq, wk, wv, jnp.asarray(ppe, BF16), jnp.asarray(grp, BF16), gq, gk, cos, s1, s2


def _a_bias_tiles(table):
    dist_vec = np.arange(-(BLOCK - 1), A_BIAS_TILES * BLOCK)
    vec = table[_t5_bucket(jnp.asarray(dist_vec))].T.astype(F32)
    windows = jnp.stack([vec[:, d * BLOCK:d * BLOCK + 2 * BLOCK - 1] for d in range(A_BIAS_TILES)], axis=1)
    return _toeplitz(windows, BLOCK, BLOCK)


def kernel(x, norm_gain, w_in, qk_gain_a, qk_gain_b, qk_gain_c, qk_gain_d, c_q_gain, c_kv_gain,
           w_q_b, w_kv_b, sinks, rel_bias, w_branch, w_out):
    batch, seq, d_model = x.shape
    depth = w_in.shape[0]
    t = batch * seq
    x2 = x.reshape(t, d_model)

    w_pn, w_pp, w_bg, w_pf, hg, flag, hg_b, flag_b = _layer_weights(w_in, qk_gain_a, qk_gain_b, qk_gain_d)
    wq, wk, wv, ppe, grp, gq, gk, cos, s1, s2 = _c_weights(qk_gain_c, w_q_b, w_kv_b, seq)
    wb = w_branch.astype(BF16)
    wo = w_out.astype(BF16)

    rel_bias2 = rel_bias * LOG2E
    bias_a = _a_bias_tiles(rel_bias2[:, :A_HEADS])
    bias_b = [_band_bias(rel_bias2[:, A_HEADS + g * B_HEADS:A_HEADS + (g + 1) * B_HEADS], dil, window // dil)
              for g, (window, dil) in enumerate(B_GROUPS)]
    bias_d = _band_bias(rel_bias2[:, N_BIAS_HEADS - D_HEADS:], 1, D_WINDOW - 1)

    xn = _norm(x2, norm_gain[0])
    for l in range(depth):
        pn = _proj(xn, w_pn[l], BF16, head_gain=hg[l], flag=flag)
        pp = _proj(xn, w_pp[l], BF16)
        pf = _proj(xn, w_pf[l], F32)

        sel = _a_select(pf, batch, seq)
        ya = _a_attn(pn, pp, sel, bias_a, batch, seq)

        yb, lb = [], []
        for g, (window, dil) in enumerate(B_GROUPS):
            bg = _proj(xn, w_bg[g][l], BF16, head_gain=hg_b[l], flag=flag_b, dil=dil, seq=seq)
            o, lse = _band_attn(bg, bg, bg, bias_b[g], batch * dil, seq // dil, 0, 1, 2, BRANCH_WIDTH,
                                want_lse=True)
            yb.append(o)
            lb.append(lse)

        qc, kc, vc = _c_prep(pf, c_q_gain[l].reshape(1, -1), c_kv_gain[l].reshape(1, -1), wq[l], wk[l], wv[l],
                             ppe, grp, gq[l].reshape(1, -1), gk[l].reshape(1, -1), cos, s1, s2, seq)
        yc = _c_attn(qc, kc, vc, batch, seq)

        (yd,) = _band_attn(pn, pn, pn, bias_d, batch, seq, PN_DQ // BRANCH_WIDTH, PN_DK // 256, PN_DV // 256,
                           256, sinks=sinks[l] * LOG2E)

        if l + 1 < depth:
            x2, xn = _merge(x2, ya, yb, lb, yc, yd, pp, wb[l], wo[l], batch, seq, next_gain=norm_gain[l + 1])
        else:
            (x2,) = _merge(x2, ya, yb, lb, yc, yd, pp, wb[l], wo[l], batch, seq)
    return x2.reshape(batch, seq, d_model)
```

```python
import functools
import math

import numpy as np
import jax
import jax.numpy as jnp
from jax import lax
from jax.experimental import pallas as pl
from jax.experimental.pallas import tpu as pltpu

F32 = jnp.float32
BF16 = jnp.bfloat16

D_MODEL = 1024
BLOCK = 128
HEAD_DIM = 64
N_BRANCH = 4
BRANCH_WIDTH = 512
EPS = 1e-6
A_HEADS = 8
IDX_HEADS = 8
IDX_DIM = 32
TOPK_MAX = 256
B_GROUPS = ((128, 1), (512, 4), (2048, 16))
B_HEADS = 8
C_HEADS = 8
C_NOPE = 64
C_ROPE = 32
C_V = 64
C_Q_LORA = 256
C_KV_LORA = 128
ROPE_THETA = 10000.0
D_HEADS = 8
D_KV_HEADS = 2
D_WINDOW = 128
NUM_BUCKETS = 32
MAX_DISTANCE = 2048
N_BIAS_HEADS = A_HEADS + len(B_GROUPS) * B_HEADS + D_HEADS

LANES = 128
MXU_WIDTH = 256
NEG = -1e30
LOG2E = math.log2(math.e)
INT_MIN = -(2 ** 31)

_A0 = 0
_B0 = 2344
_C0 = 7464
_D0 = 8392
_G0 = 9672

PN_AQ, PN_AK, PN_DQ, PN_DK, PN_DV, PN_W = 0, 512, 1024, 1536, 1792, 2048
PP_G, PP_AV, PP_AZ, PP_BZ, PP_CZ, PP_DZ, PP_W = 0, 4096, 4608, 5120, 5632, 6144, 6656
PF_W = 1024
BG_W = 3 * BRANCH_WIDTH

A_BIAS_TILES = MAX_DISTANCE // BLOCK + 2


def _cparams(sem, vmem_mb=48):
    return pltpu.CompilerParams(dimension_semantics=sem, vmem_limit_bytes=vmem_mb * 1024 * 1024)


def _t5_bucket(dist):
    max_exact = NUM_BUCKETS // 2
    d = jnp.maximum(dist, 0)
    logd = jnp.log(jnp.maximum(d, 1).astype(F32) / max_exact)
    large = max_exact + (logd / math.log(MAX_DISTANCE / max_exact) * (NUM_BUCKETS - max_exact)).astype(jnp.int32)
    return jnp.where(d < max_exact, d, jnp.minimum(large, NUM_BUCKETS - 1))


def _rms_bf16(x, gain):
    return (x * lax.rsqrt(jnp.mean(x * x, axis=-1, keepdims=True) + EPS) * gain).astype(BF16)


def _norm_kernel(x_ref, g_ref, o_ref):
    o_ref[...] = _rms_bf16(x_ref[...], g_ref[...])


def _norm(x2, gain, tm=1024):
    t, d = x2.shape
    return pl.pallas_call(
        _norm_kernel,
        grid=(t // tm,),
        in_specs=[pl.BlockSpec((tm, d), lambda i: (i, 0)), pl.BlockSpec((1, d), lambda i: (0, 0))],
        out_specs=pl.BlockSpec((tm, d), lambda i: (i, 0)),
        out_shape=jax.ShapeDtypeStruct((t, d), BF16),
        compiler_params=_cparams(("parallel",)),
        name="norm",
    )(x2, gain.reshape(1, d))


def _proj_kernel(xn_ref, w_ref, *rest, norm, dil):
    rest = list(rest)
    res_ref = rest.pop(-1) if dil is not None and dil > 1 else None
    if norm:
        hg_ref, flag_ref, bd_ref, o_ref = rest
    else:
        (o_ref,) = rest

    def emit(h):
        if dil is None:
            o_ref[...] = h.astype(o_ref.dtype)
        elif dil == 1:
            o_ref[0, 0] = h.astype(o_ref.dtype)
        else:
            sub = h.shape[0] // dil
            for c in range(h.shape[1] // LANES):
                res_ref[c] = h[:, c * LANES:(c + 1) * LANES]
            for r in range(dil):
                o_ref[0, r] = jnp.concatenate(
                    [res_ref[c, pl.ds(r, sub, stride=dil), :] for c in range(h.shape[1] // LANES)],
                    axis=1).astype(o_ref.dtype)

    h = jnp.dot(xn_ref[...], w_ref[...], preferred_element_type=F32)
    if norm:
        sq = (h * h).astype(BF16)
        slab = bd_ref.shape[0]
        ss = jnp.concatenate([jnp.dot(sq[:, c:c + slab], bd_ref[...], preferred_element_type=F32)
                              for c in range(0, sq.shape[1], slab)], axis=1)
        scale = lax.rsqrt(ss * (1.0 / HEAD_DIM) + EPS) * hg_ref[...]
        h = h * jnp.where(flag_ref[...] > 0, scale, 1.0)
    emit(h)


def _proj(xn, w, out_dtype, head_gain=None, flag=None, dil=None, seq=None, tm=1024, tn=512):
    t, d = xn.shape
    n = w.shape[1]
    norm = head_gain is not None
    scratch = []
    if dil is None:
        out_spec = pl.BlockSpec((tm, tn), lambda i, j: (i, j))
        out_shape = jax.ShapeDtypeStruct((t, n), out_dtype)
        kdil = None
    else:
        nsb = seq // tm
        out_spec = pl.BlockSpec((1, dil, tm // dil, tn), lambda i, j: (i // nsb, 0, i % nsb, j))
        out_shape = jax.ShapeDtypeStruct((t // seq, dil, seq // dil, n), out_dtype)
        kdil = dil
        if dil > 1:
            scratch.append(pltpu.VMEM((tn // LANES, tm, LANES), F32))
    in_specs = [pl.BlockSpec((tm, d), lambda i, j: (i, 0)),
                pl.BlockSpec((d, tn), lambda i, j: (0, j))]
    args = [xn, w]
    if norm:
        lane = np.arange(MXU_WIDTH)
        bd = jnp.asarray((lane[:, None] // HEAD_DIM == lane[None, :] // HEAD_DIM), BF16)
        in_specs += [pl.BlockSpec((1, tn), lambda i, j: (0, j)),
                     pl.BlockSpec((1, tn), lambda i, j: (0, j)),
                     pl.BlockSpec((MXU_WIDTH, MXU_WIDTH), lambda i, j: (0, 0))]
        args += [head_gain.reshape(1, n), flag.reshape(1, n), bd]
    out = pl.pallas_call(
        functools.partial(_proj_kernel, norm=norm, dil=kdil),
        grid=(t // tm, n // tn),
        in_specs=in_specs,
        out_specs=out_spec,
        out_shape=out_shape,
        scratch_shapes=scratch,
        compiler_params=_cparams(("parallel", "parallel")),
        name=("proj_norm" if norm else "proj") + ("" if dil is None else "_dil%d" % dil),
    )(*args)
    return out.reshape(t, n)


C_PAIR = 256
C_QK_W = (C_HEADS // 2) * C_PAIR
C_ROPE_HALF = C_ROPE // 2


def _c_lane(h, c):
    base = (h // 2) * C_PAIR
    e = h % 2
    if c < C_NOPE:
        return base + e * C_NOPE + c
    return base + 2 * C_NOPE + e * C_ROPE + (c - C_NOPE)


def _c_layout_tables():
    src_q = np.full(C_QK_W, -1, np.int64)
    src_k = np.full(C_QK_W, -1, np.int64)
    gain_idx = np.full(C_QK_W, -1, np.int64)
    head = np.full(C_QK_W, -1, np.int64)
    pe_src = np.full(C_QK_W, -1, np.int64)
    rope_j = np.full(C_QK_W, -1, np.int64)
    rope_half = np.zeros(C_QK_W, np.int64)
    for h in range(C_HEADS):
        for c in range(C_NOPE + C_ROPE):
            ln = _c_lane(h, c)
            src_q[ln] = h * (C_NOPE + C_ROPE) + c
            gain_idx[ln] = c
            head[ln] = h
            if c < C_NOPE:
                src_k[ln] = h * (C_NOPE + C_V) + c
            else:
                r = c - C_NOPE
                pe_src[ln] = r
                rope_j[ln] = r % C_ROPE_HALF
                rope_half[ln] = r // C_ROPE_HALF
    return src_q, src_k, gain_idx, head, pe_src, rope_j, rope_half


def _c_prep_kernel(pf_ref, cqg_ref, ckvg_ref, wq_ref, wk_ref, wv_ref, ppe_ref, grp_ref,
                   gq_ref, gk_ref, cos_ref, s1_ref, s2_ref, q_ref, k_ref, v_ref):
    blk = pf_ref[...]
    cq = blk[:, :C_Q_LORA]
    ckv = blk[:, C_Q_LORA:C_Q_LORA + C_KV_LORA]
    pe = blk[:, C_Q_LORA + C_KV_LORA:]

    def rms(v, g):
        return v * lax.rsqrt(jnp.mean(v * v, axis=-1, keepdims=True) + EPS) * g

    def head_norm_rope(raw, gain):
        sq = (raw * raw).astype(BF16)
        ss = jnp.concatenate([jnp.dot(sq[:, p * C_PAIR:(p + 1) * C_PAIR], grp_ref[...], preferred_element_type=F32)
                              for p in range(C_QK_W // C_PAIR)], axis=1)
        y = raw * lax.rsqrt(ss * (1.0 / (C_NOPE + C_ROPE)) + EPS) * gain
        up = pltpu.roll(y, C_QK_W - C_ROPE_HALF, 1)
        dn = pltpu.roll(y, C_ROPE_HALF, 1)
        wide = lambda ref: jnp.concatenate([ref[...]] * (C_QK_W // C_PAIR), axis=1)
        return y * wide(cos_ref) + up * wide(s1_ref) + dn * wide(s2_ref)

    cqn = rms(cq, cqg_ref[...]).astype(BF16)
    q_raw = jnp.dot(cqn, wq_ref[...], preferred_element_type=F32)
    q_ref[...] = head_norm_rope(q_raw, gq_ref[...]).astype(BF16)

    ckvn = rms(ckv, ckvg_ref[...]).astype(BF16)
    pe_hi = pe.astype(BF16)
    pe_lo = (pe - pe_hi.astype(F32)).astype(BF16)
    k_raw = (jnp.dot(ckvn, wk_ref[...], preferred_element_type=F32)
             + jnp.dot(pe_hi, ppe_ref[...], preferred_element_type=F32)
             + jnp.dot(pe_lo, ppe_ref[...], preferred_element_type=F32))
    k_ref[...] = head_norm_rope(k_raw, gk_ref[...]).astype(BF16)
    v_ref[...] = jnp.dot(ckvn, wv_ref[...], preferred_element_type=F32).astype(BF16)


def _c_prep(pf, cqg, ckvg, wq, wk, wv, ppe, grp, gq, gk, cos, s1, s2, seq, tm=512):
    t = pf.shape[0]
    nsb = seq // tm
    full = lambda shape: pl.BlockSpec(shape, lambda i: (0,) * len(shape))
    tab = pl.BlockSpec((tm, C_PAIR), lambda i: (i % nsb, 0))
    return pl.pallas_call(
        _c_prep_kernel,
        grid=(t // tm,),
        in_specs=[pl.BlockSpec((tm, 512), lambda i: (i, 1)),
                  full((1, C_Q_LORA)), full((1, C_KV_LORA)),
                  full((C_Q_LORA, C_QK_W)), full((C_KV_LORA, C_QK_W)), full((C_KV_LORA, BRANCH_WIDTH)),
                  full((LANES, C_QK_W)), full((C_PAIR, C_PAIR)),
                  full((1, C_QK_W)), full((1, C_QK_W)), tab, tab, tab],
        out_specs=[pl.BlockSpec((tm, C_QK_W), lambda i: (i, 0)),
                   pl.BlockSpec((tm, C_QK_W), lambda i: (i, 0)),
                   pl.BlockSpec((tm, BRANCH_WIDTH), lambda i: (i, 0))],
        out_shape=[jax.ShapeDtypeStruct((t, C_QK_W), BF16),
                   jax.ShapeDtypeStruct((t, C_QK_W), BF16),
                   jax.ShapeDtypeStruct((t, BRANCH_WIDTH), BF16)],
        compiler_params=_cparams(("parallel",)),
        name="c_prep",
    )(pf, cqg, ckvg, wq, wk, wv, ppe, grp, gq, gk, cos, s1, s2)


def _flash_update(e, s, v_ones, m_ref, acc_ref, rows=slice(None)):
    m_prev = m_ref[e, rows]
    m_new = jnp.maximum(m_prev, jnp.max(s, axis=1, keepdims=True))
    alpha = jnp.exp2(m_prev - m_new)
    p = jnp.exp2(s - jnp.concatenate([m_new] * (s.shape[1] // LANES), axis=1))
    acc_ref[e, rows] = (jnp.concatenate([alpha, alpha], axis=1) * acc_ref[e, rows]
                        + jnp.dot(p.astype(BF16), v_ones, preferred_element_type=F32))
    m_ref[e, rows] = m_new


def _flash_init(m_ref, acc_ref):
    m_ref[...] = jnp.full(m_ref.shape, NEG, F32)
    acc_ref[...] = jnp.zeros(acc_ref.shape, F32)


def _flash_finish(o_ref, acc_ref):
    lane = lax.broadcasted_iota(jnp.int32, o_ref.shape, 1)
    o0 = acc_ref[0, :, :LANES] / acc_ref[0, :, LANES:]
    o1 = acc_ref[1, :, :LANES] / acc_ref[1, :, LANES:]
    o_ref[...] = jnp.where(lane < HEAD_DIM, o0, o1).astype(o_ref.dtype)


def _with_ones(v_chunk):
    return jnp.concatenate([v_chunk, jnp.ones(v_chunk.shape, v_chunk.dtype)], axis=1)


_CONTRACT_LANES = (((1,), (1,)), ((), ()))


def _head_masks(width, ranges0, ranges1):
    m = np.zeros((2, 1, width), np.float32)
    for e, ranges in enumerate((ranges0, ranges1)):
        for lo, hi in ranges:
            m[e, 0, lo:hi] = 1
    return jnp.asarray(m, BF16)


def _pair_masks():
    return _head_masks(LANES, [(0, HEAD_DIM)], [(HEAD_DIM, LANES)])


def _transpose_keys(k_ref, kt_ref, rows=512):
    def body(c, carry):
        off = pl.multiple_of(c * rows, rows)
        kt_ref[:, pl.ds(off, rows)] = k_ref[pl.ds(off, rows), :].astype(F32).T.astype(kt_ref.dtype)
        return carry
    lax.fori_loop(0, k_ref.shape[0] // rows, body, 0)


def _chunk_loop(chunk, n, unroll, carry=None, slots=None):
    start = 0
    width = unroll
    while width >= 1:
        def body(g, carry, width=width, start=start):
            for j in range(width):
                carry = chunk(start + g * width + j, carry) if slots is None else \
                    chunk(start + g * width + j, carry, j)
            return carry
        groups = (n - start) // width
        carry = lax.fori_loop(0, groups, body, carry)
        start = start + groups * width
        width //= 2
    return carry


def _c_attn_kernel(q_ref, hm_ref, k_ref, v_ref, o_ref, kt_ref, m_ref, acc_ref, *, tq, tk, unroll):
    qi = pl.program_id(2)

    @pl.when(qi == 0)
    def _():
        _transpose_keys(k_ref, kt_ref)

    q = q_ref[...]
    qs = (q * hm_ref[0], q * hm_ref[1])
    _flash_init(m_ref, acc_ref)
    n_full = (qi * tq) // tk

    def chunk(c, row0=0, diagonal=False):
        off = pl.multiple_of(c * tk, tk)
        kc = kt_ref[:, pl.ds(off, tk)]
        vc = _with_ones(v_ref[pl.ds(off, tk), :])
        for e in range(2):
            if not diagonal:
                rows = slice(row0, tq)
                _flash_update(e, jnp.dot(qs[e][rows], kc, preferred_element_type=F32), vc, m_ref, acc_ref, rows)
                continue
            rows = slice(row0, row0 + tk)
            s = jnp.dot(qs[e][rows], kc, preferred_element_type=F32)
            s = jnp.where(lax.broadcasted_iota(jnp.int32, s.shape, 1) <= lax.broadcasted_iota(jnp.int32, s.shape, 0),
                          s, NEG)
            _flash_update(e, s, vc, m_ref, acc_ref, rows)
            if row0 + tk < tq:
                rows = slice(row0 + tk, tq)
                _flash_update(e, jnp.dot(qs[e][rows], kc, preferred_element_type=F32), vc, m_ref, acc_ref, rows)

    _chunk_loop(lambda c, _: chunk(c), n_full, unroll)
    for j in range(tq // tk):
        chunk(n_full + j, row0=j * tk, diagonal=True)
    _flash_finish(o_ref, acc_ref)


def _c_attn(qc, kc, vc, batch, seq, tq=1024, tk=512, unroll=4):
    t = qc.shape[0]
    nq = seq // tq
    npair = C_HEADS // 2
    return pl.pallas_call(
        functools.partial(_c_attn_kernel, tq=tq, tk=tk, unroll=unroll),
        grid=(batch, npair, nq),
        in_specs=[pl.BlockSpec((tq, C_PAIR), lambda b, p, i: (b * nq + i, p)),
                  pl.BlockSpec((2, 1, C_PAIR), lambda b, p, i: (0, 0, 0)),
                  pl.BlockSpec((seq, C_PAIR), lambda b, p, i: (b, p)),
                  pl.BlockSpec((seq, LANES), lambda b, p, i: (b, p))],
        out_specs=pl.BlockSpec((tq, LANES), lambda b, p, i: (b * nq + i, p)),
        out_shape=jax.ShapeDtypeStruct((t, BRANCH_WIDTH), BF16),
        scratch_shapes=[pltpu.VMEM((C_PAIR, seq), BF16),
                        pltpu.VMEM((2, tq, LANES), F32), pltpu.VMEM((2, tq, 2 * LANES), F32)],
        compiler_params=_cparams(("parallel", "parallel", "arbitrary")),
        name="c_attn",
    )(qc, _head_masks(C_PAIR, [(0, C_NOPE), (2 * C_NOPE, 2 * C_NOPE + C_ROPE)],
                      [(C_NOPE, 2 * C_NOPE), (2 * C_NOPE + C_ROPE, 2 * C_NOPE + 2 * C_ROPE)]), kc, vc)


def _a_attn_kernel(q_ref, hm_ref, k_ref, v_ref, sel_ref, bias_ref, o_ref, kt_ref, m_ref, acc_ref, msk_ref,
                   *, tq, tk, unroll):
    qi = pl.program_id(2)

    @pl.when(qi == 0)
    def _():
        _transpose_keys(k_ref, kt_ref)

    q = q_ref[...]
    qs = (q * hm_ref[0], q * hm_ref[1])
    _flash_init(m_ref, acc_ref)
    n_chunks = ((qi + 1) * tq + tk - 1) // tk

    def chunk(c, carry, slot):
        off = pl.multiple_of(c * tk, tk)
        kc = kt_ref[:, pl.ds(off, tk)]
        vc = _with_ones(v_ref[pl.ds(off, tk), :])
        msk_ref[slot] = sel_ref[:, pl.ds(off, tk)].astype(F32)
        for e in range(2):
            s = jnp.dot(qs[e], kc, preferred_element_type=F32)
            rows = []
            for i in range(tq // BLOCK):
                tiles = []
                for j in range(tk // BLOCK):
                    d = (qi * (tq // BLOCK) + i) - (c * (tk // BLOCK) + j)
                    d = jnp.clip(d, 0, A_BIAS_TILES - 1)
                    tiles.append(bias_ref[e, d])
                rows.append(jnp.concatenate(tiles, axis=1))
            s = s + jnp.concatenate(rows, axis=0) + msk_ref[slot]
            _flash_update(e, s, vc, m_ref, acc_ref)
        return carry

    _chunk_loop(chunk, n_chunks, unroll, slots=unroll)
    _flash_finish(o_ref, acc_ref)


def _a_attn(pn, pp, sel, bias_tiles, batch, seq, tq=512, tk=512, unroll=4):
    t = pn.shape[0]
    nq = seq // tq
    npair = A_HEADS // 2
    qcol, kcol, vcol = PN_AQ // LANES, PN_AK // LANES, PP_AV // LANES
    return pl.pallas_call(
        functools.partial(_a_attn_kernel, tq=tq, tk=tk, unroll=unroll),
        grid=(batch, npair, nq),
        in_specs=[pl.BlockSpec((tq, LANES), lambda b, p, i: (b * nq + i, qcol + p)),
                  pl.BlockSpec((2, 1, LANES), lambda b, p, i: (0, 0, 0)),
                  pl.BlockSpec((seq, LANES), lambda b, p, i: (b, kcol + p)),
                  pl.BlockSpec((seq, LANES), lambda b, p, i: (b, vcol + p)),
                  pl.BlockSpec((tq, seq), lambda b, p, i: (b * nq + i, 0)),
                  pl.BlockSpec((2, A_BIAS_TILES, BLOCK, BLOCK), lambda b, p, i: (p, 0, 0, 0))],
        out_specs=pl.BlockSpec((tq, LANES), lambda b, p, i: (b * nq + i, p)),
        out_shape=jax.ShapeDtypeStruct((t, BRANCH_WIDTH), BF16),
        scratch_shapes=[pltpu.VMEM((LANES, seq), BF16),
                        pltpu.VMEM((2, tq, LANES), F32), pltpu.VMEM((2, tq, 2 * LANES), F32),
                        pltpu.VMEM((unroll, tq, tk), F32)],
        compiler_params=_cparams(("parallel", "parallel", "arbitrary")),
        name="a_attn",
    )(pn, _pair_masks(), pn, pp, sel, bias_tiles)


def _sortable_key(score):
    bits = pltpu.bitcast(score, jnp.int32)
    return bits ^ ((bits >> 31) & jnp.int32(0x7FFFFFFF))


def _bit_transpose32(words):
    a = list(words)
    j, m = 16, 0x0000FFFF
    while j:
        for k in range(32):
            if k & j:
                continue
            t = (a[k] ^ lax.shift_right_logical(a[k + j], jnp.int32(j))) & m
            a[k] = a[k] ^ t
            a[k + j] = a[k + j] ^ lax.shift_left(t, jnp.int32(j))
        j >>= 1
        m ^= (m << j) & 0xFFFFFFFF
    return a


SEL_ROWS = 512
PLANE_ROWS = 256
SCORE_UNROLL = 2


def _a_select_kernel(qblk_ref, kblk_ref, ph_ref, pl_ref, pkh_ref, pkl_ref, sel_ref,
                     ikx_ref, keys_ref, planes_ref, eq_ref, iqt_ref, jcut_ref, *, seq, k_sel):
    qi = pl.program_id(1)
    n_sel = (qi * BLOCK + BLOCK + SEL_ROWS - 1) // SEL_ROWS

    @pl.when(qi == 0)
    def _():
        def prep(c, carry):
            off = pl.multiple_of(c * 512, 512)
            kv = kblk_ref[pl.ds(off, 512), :]
            hi = kv.astype(BF16)
            lo = (kv - hi.astype(F32)).astype(BF16)
            ikx_ref[pl.ds(off, 512), :] = (
                jnp.dot(hi, pkh_ref[...], preferred_element_type=F32)
                + jnp.dot(lo, pkl_ref[...], preferred_element_type=F32)).astype(BF16)
            return carry
        lax.fori_loop(0, seq // 512, prep, 0)

        def clear(g, carry):
            planes_ref[g] = jnp.zeros(planes_ref.shape[1:], jnp.int32)
            return carry
        lax.fori_loop(0, planes_ref.shape[0], clear, 0)

    qb = qblk_ref[...]
    iq = qb[:, :IDX_HEADS * IDX_DIM]
    iq_hi = iq.astype(BF16)
    iq_lo = (iq - iq_hi.astype(F32)).astype(BF16)
    iqx = (jnp.dot(iq_hi, ph_ref[...], preferred_element_type=F32)
           + jnp.dot(iq_lo, pl_ref[...], preferred_element_type=F32))
    for h in range(IDX_HEADS):
        iqt_ref[h // 2, :, (h % 2) * LANES:(h % 2 + 1) * LANES] = iqx[:, h * LANES:(h + 1) * LANES].T.astype(BF16)
    iw_t = qb[:, IDX_HEADS * IDX_DIM:IDX_HEADS * IDX_DIM + LANES].T
    iw_rows = [iw_t[IDX_DIM + h:IDX_DIM + h + 1, :] for h in range(IDX_HEADS)]
    q_pos = qi * BLOCK + lax.broadcasted_iota(jnp.int32, (SEL_ROWS, BLOCK), 1)
    k_row = lax.broadcasted_iota(jnp.int32, (SEL_ROWS, BLOCK), 0)

    def score_chunk(c, masked):
        off = pl.multiple_of(c * SEL_ROWS, SEL_ROWS)
        kx = ikx_ref[pl.ds(off, SEL_ROWS), :]
        sc = jnp.zeros((SEL_ROWS, BLOCK), F32)
        for hp in range(IDX_HEADS // 2):
            xx = jnp.dot(kx, iqt_ref[hp], preferred_element_type=F32)
            sc = (sc + jnp.maximum(xx[:, :LANES], 0.0) * iw_rows[2 * hp]
                  + jnp.maximum(xx[:, LANES:], 0.0) * iw_rows[2 * hp + 1])
        key = _sortable_key(sc + 0.0)
        if masked:
            key = jnp.where(off + k_row <= q_pos, key, INT_MIN)
        keys_ref[pl.ds(off, SEL_ROWS), :] = key
        u = key ^ INT_MIN
        for g in range(SEL_ROWS // PLANE_ROWS):
            words = [u[g * PLANE_ROWS + 8 * j:g * PLANE_ROWS + 8 * j + 8] for j in range(32)]
            for i, plane in enumerate(_bit_transpose32(words)):
                planes_ref[c * (SEL_ROWS // PLANE_ROWS) + g, i] = plane

    _chunk_loop(lambda c, _: score_chunk(c, False), n_sel - 1, SCORE_UNROLL)
    score_chunk(n_sel - 1, True)

    def per_query_total(counts):
        while len(counts) > 1:
            counts = [a + b for a, b in zip(counts[::2], counts[1::2])] + counts[len(counts) & ~1:]
        return jnp.sum(counts[0], axis=0, keepdims=True)

    n_grp = n_sel * (SEL_ROWS // PLANE_ROWS)
    all_groups = eq_ref.shape[0]
    for g in range(all_groups):
        eq_ref[g] = jnp.where(g < n_grp, jnp.full((8, BLOCK), -1, jnp.int32), jnp.zeros((8, BLOCK), jnp.int32))

    def sweep(cur_plane, prev_plane, prev_keep):
        counts = []
        for g in range(all_groups):
            eq = eq_ref[g]
            if prev_plane is not None:
                eq = eq & (planes_ref[g, prev_plane] ^ ~prev_keep)
                eq_ref[g] = eq
            ones = eq if cur_plane is None else eq & planes_ref[g, cur_plane]
            counts.append(lax.population_count(ones))
        return per_query_total(counts)

    def decide(bit, cnt_ones, thr_u, above):
        total = above + cnt_ones
        keep = jnp.where(total >= k_sel, -1, 0)
        return thr_u | (keep & bit), jnp.where(keep < 0, above, total), keep

    zeros = jnp.zeros((1, BLOCK), jnp.int32)
    state = decide(INT_MIN, sweep(0, None, None), zeros, zeros)

    def radix_step(it, state):
        thr_u, above, keep = state
        bit = lax.shift_left(jnp.int32(1), jnp.int32(31) - it)
        return decide(bit, sweep(it, it - 1, keep), thr_u, above)

    thr_u, above, keep = lax.fori_loop(1, 32, radix_step, state)
    cnt_thr = above + sweep(None, 31, keep)
    thr = thr_u ^ INT_MIN

    has_k = thr > INT_MIN
    thr = jnp.maximum(thr, INT_MIN + 1)

    jcut_ref[...] = jnp.full(jcut_ref.shape, 2 ** 31 - 1, jnp.int32)

    @pl.when(jnp.max(jnp.where(has_k, cnt_thr, 0)) > k_sel)
    def _():
        excess = jnp.where(has_k, cnt_thr - k_sel, 0)
        need = jnp.where(excess > 0, k_sel - above, 1)
        sub = lax.broadcasted_iota(jnp.int32, (8, BLOCK), 0)

        def ties_below(t):
            counts = []
            for g in range(all_groups):
                words = jnp.clip((t - g * PLANE_ROWS - sub + 7) >> 3, 0, 32)
                top = jnp.where(words > 0, lax.shift_left(jnp.int32(-1), 32 - words), 0)
                counts.append(lax.population_count(eq_ref[g] & top))
            return per_query_total(counts)

        def tie_bisect(it, t):
            cand = t + lax.shift_left(jnp.int32(1), jnp.int32(seq.bit_length() - 2) - it)
            return jnp.where(ties_below(cand) < need, cand, t)

        last = lax.fori_loop(0, seq.bit_length() - 1, tie_bisect, jnp.zeros((1, BLOCK), jnp.int32))
        jcut = jnp.where(excess > 0, last, 2 ** 31 - 1)
        jcut_ref[...] = jnp.broadcast_to(jcut, jcut_ref.shape)

    jcut = jcut_ref[0:1, :]

    def emit(c, carry):
        off = pl.multiple_of(c * SEL_ROWS, SEL_ROWS)
        keys = keys_ref[pl.ds(off, SEL_ROWS), :]
        bar = jnp.where(off + k_row > jcut, thr + 1, thr)
        add = jnp.where(keys >= bar, 0.0, NEG)
        sel_ref[:, pl.ds(off, SEL_ROWS)] = jnp.concatenate(
            [add[j * BLOCK:(j + 1) * BLOCK].T for j in range(SEL_ROWS // BLOCK)], axis=1).astype(BF16)
        return carry

    _chunk_loop(emit, n_sel, SCORE_UNROLL)

    def fill(c, carry):
        off = pl.multiple_of(c * SEL_ROWS, SEL_ROWS)
        sel_ref[:, pl.ds(off, SEL_ROWS)] = jnp.full((BLOCK, SEL_ROWS), NEG, BF16)
        return carry

    lax.fori_loop(n_sel, seq // SEL_ROWS, fill, 0)


def _a_select(pf, batch, seq):
    t = pf.shape[0]
    nq = seq // BLOCK
    k_sel = min(TOPK_MAX, seq // 4)
    n_groups = seq // PLANE_ROWS
    nlane = IDX_HEADS * IDX_DIM
    r = np.arange(nlane)
    ph = np.zeros((nlane, IDX_HEADS * LANES), np.float32)
    plo = np.zeros((nlane, IDX_HEADS * LANES), np.float32)
    ph[r, (r // IDX_DIM) * LANES + r % IDX_DIM] = 1
    ph[r, (r // IDX_DIM) * LANES + 2 * IDX_DIM + r % IDX_DIM] = 1
    plo[r, (r // IDX_DIM) * LANES + IDX_DIM + r % IDX_DIM] = 1
    d = np.arange(IDX_DIM)
    pkh = np.zeros((LANES, LANES), np.float32)
    pkl = np.zeros((LANES, LANES), np.float32)
    pkh[d, d] = 1
    pkh[d, IDX_DIM + d] = 1
    pkl[d, 2 * IDX_DIM + d] = 1
    full = lambda shape: pl.BlockSpec(shape, lambda b, i: (0,) * len(shape))
    return pl.pallas_call(
        functools.partial(_a_select_kernel, seq=seq, k_sel=k_sel),
        grid=(batch, nq),
        in_specs=[pl.BlockSpec((BLOCK, 512), lambda b, i: (b * nq + i, 0)),
                  pl.BlockSpec((seq, LANES), lambda b, i: (b, nlane // LANES)),
                  full(ph.shape), full(plo.shape), full(pkh.shape), full(pkl.shape)],
        out_specs=pl.BlockSpec((BLOCK, seq), lambda b, i: (b * nq + i, 0)),
        out_shape=jax.ShapeDtypeStruct((t, seq), BF16),
        scratch_shapes=[pltpu.VMEM((seq, LANES), BF16), pltpu.VMEM((seq, LANES), jnp.int32),
                        pltpu.VMEM((n_groups, 32, 8, LANES), jnp.int32),
                        pltpu.VMEM((n_groups, 8, LANES), jnp.int32),
                        pltpu.VMEM((IDX_HEADS // 2, LANES, 2 * LANES), BF16),
                        pltpu.VMEM((8, LANES), jnp.int32)],
        compiler_params=_cparams(("parallel", "arbitrary")),
        name="a_select",
    )(pf, pf, jnp.asarray(ph, BF16), jnp.asarray(plo, BF16), jnp.asarray(pkh, BF16),
      jnp.asarray(pkl, BF16))


BAND_RBLK = 4


def _band_kernel(*refs, has_sink, want_lse, kv_div, rblk):
    if has_sink:
        sink_ref, refs = refs[0], refs[1:]
    q_ref, hm_ref, kp_ref, kc_ref, vp_ref, vc_ref, bias_ref, o_ref = refs[:8]
    first = pl.program_id(1) == 0
    lane = lax.broadcasted_iota(jnp.int32, (BLOCK, LANES), 1)
    col = lax.broadcasted_iota(jnp.int32, (BLOCK, 2 * BLOCK), 1)
    no_prev = jnp.logical_and(first, col < BLOCK)
    for p in range(BRANCH_WIDTH // LANES):
        kcol = (p // kv_div) * LANES
        kk = jnp.concatenate([kp_ref[:, kcol:kcol + LANES], kc_ref[:, kcol:kcol + LANES]], axis=0)
        vv = _with_ones(jnp.concatenate([vp_ref[:, kcol:kcol + LANES], vc_ref[:, kcol:kcol + LANES]], axis=0))
        for i in range(rblk):
            rows = slice(i * BLOCK, (i + 1) * BLOCK)
            q = q_ref[rows, p * LANES:(p + 1) * LANES]
            kblk = kk[i * BLOCK:(i + 2) * BLOCK]
            vblk = vv[i * BLOCK:(i + 2) * BLOCK]
            outs, lses = [], []
            for e in range(2):
                s = lax.dot_general(q * hm_ref[e], kblk, _CONTRACT_LANES, preferred_element_type=F32)
                s = s + bias_ref[2 * p + e]
                if i == 0:
                    s = jnp.where(no_prev, NEG, s)
                m = jnp.max(s, axis=1, keepdims=True)
                if has_sink:
                    sk = sink_ref[2 * p + e]
                    m = jnp.maximum(m, sk)
                pr = jnp.exp2(s - m)
                acc = jnp.dot(pr.astype(BF16), vblk, preferred_element_type=F32)
                den = acc[:, LANES:]
                if has_sink:
                    den = den + jnp.exp2(sk - m)
                outs.append(acc[:, :LANES] / den)
                lses.append(m + jnp.log2(den))
            o_ref[rows, p * LANES:(p + 1) * LANES] = jnp.where(lane < HEAD_DIM, outs[0], outs[1]).astype(o_ref.dtype)
            if want_lse:
                refs[8][rows, p * LANES:(p + 1) * LANES] = jnp.where(lane < HEAD_DIM, lses[0], lses[1])


def _band_attn(q_arr, k_arr, v_arr, bias, nsub, sub_len, qcb, kcb, vcb, kv_w, sinks=None, want_lse=False):
    t = q_arr.shape[0]
    rblk = min(BAND_RBLK, sub_len // BLOCK)
    step_rows = rblk * BLOCK
    nbs = sub_len // step_rows
    kv_div = BRANCH_WIDTH // kv_w

    def cur(u, n):
        return u * nbs + n

    def prev(u, n):
        return jnp.where(n == 0, u * nbs * rblk, (u * nbs + n) * rblk - 1)

    in_specs = [
        pl.BlockSpec((step_rows, BRANCH_WIDTH), lambda u, n: (cur(u, n), qcb)),
        pl.BlockSpec((2, 1, LANES), lambda u, n: (0, 0, 0)),
        pl.BlockSpec((BLOCK, kv_w), lambda u, n: (prev(u, n), kcb)),
        pl.BlockSpec((step_rows, kv_w), lambda u, n: (cur(u, n), kcb)),
        pl.BlockSpec((BLOCK, kv_w), lambda u, n: (prev(u, n), vcb)),
        pl.BlockSpec((step_rows, kv_w), lambda u, n: (cur(u, n), vcb)),
        pl.BlockSpec(bias.shape, lambda u, n: (0, 0, 0)),
    ]
    args = [q_arr, _pair_masks(), k_arr, k_arr, v_arr, v_arr, bias]
    if sinks is not None:
        in_specs = [pl.BlockSpec(memory_space=pltpu.SMEM)] + in_specs
        args = [sinks] + args
    o_spec = pl.BlockSpec((step_rows, BRANCH_WIDTH), lambda u, n: (cur(u, n), 0))
    out_specs = [o_spec]
    out_shape = [jax.ShapeDtypeStruct((t, BRANCH_WIDTH), BF16)]
    if want_lse:
        out_specs.append(o_spec)
        out_shape.append(jax.ShapeDtypeStruct((t, BRANCH_WIDTH), F32))
    return pl.pallas_call(
        functools.partial(_band_kernel, has_sink=sinks is not None, want_lse=want_lse, kv_div=kv_div, rblk=rblk),
        grid=(nsub, nbs),
        in_specs=in_specs,
        out_specs=out_specs,
        out_shape=out_shape,
        compiler_params=_cparams(("parallel", "arbitrary")),
        name="band_attn",
    )(*args)


def _toeplitz(w, rows, cols):
    length = rows + cols - 1
    u = jnp.pad(w[..., ::-1], [(0, 0)] * (w.ndim - 1) + [(0, 1)])
    flat = jnp.broadcast_to(u[..., None, :], w.shape[:-1] + (rows, length + 1)).reshape(w.shape[:-1] + (-1,))
    skew = flat[..., :rows * length].reshape(w.shape[:-1] + (rows, length))
    return skew[..., rows - 1:rows - 1 + cols]


def _band_bias(table, step, max_dist):
    rel = BLOCK + np.arange(BLOCK)[:, None] - np.arange(2 * BLOCK)[None, :]
    rel_vec = np.arange(BLOCK - (2 * BLOCK - 1), 2 * BLOCK)
    bias = _toeplitz(table[_t5_bucket(jnp.asarray(rel_vec * step))].T.astype(F32), BLOCK, 2 * BLOCK)
    ok = (rel >= 0) & (rel <= max_dist)
    return jnp.where(jnp.asarray(ok)[None], bias, NEG)


def _sigmoid(x):
    return 0.5 * jnp.tanh(0.5 * x) + 0.5


def _merge_kernel(x_ref, ya_ref, yb0_ref, yb1_ref, yb2_ref, l0_ref, l1_ref, l2_ref, yc_ref, yd_ref,
                  za_ref, zb_ref, zc_ref, zd_ref, ga_ref, gb_ref, gc_ref, gd_ref, wb_ref, wo_ref, *rest):
    if len(rest) == 7:
        ng_ref, o_ref, xn_ref, y1_ref, y2_ref, s1_ref, s2_ref = rest
    else:
        ng_ref = xn_ref = None
        o_ref, y1_ref, y2_ref, s1_ref, s2_ref = rest
    ncol = BRANCH_WIDTH // LANES
    for src, dst in ((yb1_ref, y1_ref), (yb2_ref, y2_ref), (l1_ref, s1_ref), (l2_ref, s2_ref)):
        dil = src.shape[1]
        for r in range(dil):
            blk = src[0, r].astype(F32)
            for c in range(ncol):
                dst[c, pl.ds(r, src.shape[2], stride=dil), :] = blk[:, c * LANES:(c + 1) * LANES]
    wide = lambda ref: jnp.concatenate([ref[c] for c in range(ncol)], axis=1)
    l0, l1, l2 = l0_ref[0, 0], wide(s1_ref), wide(s2_ref)
    mx = jnp.maximum(jnp.maximum(l0, l1), l2)
    w0, w1, w2 = jnp.exp2(l0 - mx), jnp.exp2(l1 - mx), jnp.exp2(l2 - mx)
    yb = (w0 * yb0_ref[0, 0].astype(F32) + w1 * wide(y1_ref) + w2 * wide(y2_ref)) / (w0 + w1 + w2)
    ys = (ya_ref[...].astype(F32), yb, yc_ref[...].astype(F32), yd_ref[...].astype(F32))
    zs = (za_ref, zb_ref, zc_ref, zd_ref)
    gs = (ga_ref, gb_ref, gc_ref, gd_ref)
    merged = jnp.zeros(o_ref.shape, F32)
    for n in range(N_BRANCH):
        z = zs[n][...].astype(F32)
        u = (ys[n] * (z * _sigmoid(z))).astype(BF16)
        proj = jnp.dot(u, wb_ref[n], preferred_element_type=F32)
        merged = merged + _sigmoid(gs[n][...].astype(F32)) * proj
    x_new = x_ref[...] + jnp.dot(merged.astype(BF16), wo_ref[...], preferred_element_type=F32)
    o_ref[...] = x_new
    if xn_ref is not None:
        xn_ref[...] = _rms_bf16(x_new, ng_ref[...])


def _merge(x2, ya, yb, lb, yc, yd, pp, wb, wo, batch, seq, next_gain=None, tm=512):
    t = x2.shape[0]
    bw = BRANCH_WIDTH
    nsb = seq // tm
    row = lambda w, c: pl.BlockSpec((tm, w), lambda i: (i, c))

    def sub(dil):
        return pl.BlockSpec((1, dil, tm // dil, bw), lambda i: (i // nsb, 0, i % nsb, 0))

    dils = [dil for _, dil in B_GROUPS]
    yb = [a.reshape(batch, dil, seq // dil, bw) for a, dil in zip(yb, dils)]
    lb = [a.reshape(batch, dil, seq // dil, bw) for a, dil in zip(lb, dils)]
    in_specs = ([row(D_MODEL, 0), row(bw, 0)] + [sub(dil) for dil in dils] * 2 + [row(bw, 0), row(bw, 0)]
                + [row(bw, PP_AZ // bw), row(bw, PP_BZ // bw), row(bw, PP_CZ // bw), row(bw, PP_DZ // bw)]
                + [row(D_MODEL, PP_G // D_MODEL + n) for n in range(N_BRANCH)]
                + [pl.BlockSpec((N_BRANCH, bw, D_MODEL), lambda i: (0, 0, 0)),
                   pl.BlockSpec((D_MODEL, D_MODEL), lambda i: (0, 0))])
    args = [x2, ya, yb[0], yb[1], yb[2], lb[0], lb[1], lb[2], yc, yd, pp, pp, pp, pp, pp, pp, pp, pp, wb, wo]
    out_specs = [row(D_MODEL, 0)]
    out_shape = [jax.ShapeDtypeStruct((t, D_MODEL), F32)]
    if next_gain is not None:
        in_specs.append(pl.BlockSpec((1, D_MODEL), lambda i: (0, 0)))
        args.append(next_gain.reshape(1, D_MODEL))
        out_specs.append(row(D_MODEL, 0))
        out_shape.append(jax.ShapeDtypeStruct((t, D_MODEL), BF16))
    return pl.pallas_call(
        _merge_kernel,
        grid=(t // tm,),
        in_specs=in_specs,
        out_specs=out_specs,
        out_shape=out_shape,
        scratch_shapes=[pltpu.VMEM((bw // LANES, tm, LANES), F32)] * 4,
        compiler_params=_cparams(("parallel",)),
        name="merge",
    )(*args)


def _layer_weights(w_in, qk_gain_a, qk_gain_b, qk_gain_d):
    def cols(start, width):
        return w_in[:, :, start:start + width]

    n_bq = len(B_GROUPS) * B_HEADS * HEAD_DIM
    dk = cols(_D0 + 512, 128)
    dv = cols(_D0 + 640, 128)
    dup = lambda a: jnp.concatenate([a[:, :, :64], a[:, :, :64], a[:, :, 64:], a[:, :, 64:]], axis=-1)
    w_pn = jnp.concatenate([cols(_A0, 512), cols(_A0 + 512, 512), cols(_D0, 512), dup(dk), dup(dv)],
                           axis=-1).astype(BF16)
    w_pp = jnp.concatenate([cols(_G0, N_BRANCH * D_MODEL), cols(_A0 + 1024, 512), cols(_A0 + 1536, 512),
                            cols(_B0 + 3 * n_bq, 512), cols(_C0 + 416, 512), cols(_D0 + 768, 512)],
                           axis=-1).astype(BF16)
    w_bg = [jnp.concatenate([cols(_B0 + g * 512, 512), cols(_B0 + n_bq + g * 512, 512),
                             cols(_B0 + 2 * n_bq + g * 512, 512)], axis=-1).astype(BF16)
            for g in range(len(B_GROUPS))]
    depth = w_in.shape[0]
    zeros = lambda w: jnp.zeros((depth, D_MODEL, w), w_in.dtype)
    w_pf = jnp.concatenate([cols(_A0 + 2048, 296), zeros(216), cols(_C0, 416), zeros(96)], axis=-1).astype(BF16)

    scale = HEAD_DIM ** -0.5 * LOG2E
    tile = lambda g, reps: jnp.tile(g, (1, reps))
    hg = jnp.concatenate([tile(qk_gain_a[:, 0] * scale, 8), tile(qk_gain_a[:, 1], 8),
                          tile(qk_gain_d[:, 0] * scale, 8), tile(qk_gain_d[:, 1], 4),
                          jnp.ones((depth, 256), F32)], axis=-1)
    flag = jnp.concatenate([jnp.ones((PN_DV,), F32), jnp.zeros((PN_W - PN_DV,), F32)])
    hg_b = jnp.concatenate([tile(qk_gain_b[:, 0] * scale, 8), tile(qk_gain_b[:, 1], 8),
                            jnp.ones((depth, 512), F32)], axis=-1)
    flag_b = jnp.concatenate([jnp.ones((2 * BRANCH_WIDTH,), F32), jnp.zeros((BRANCH_WIDTH,), F32)])
    return w_pn, w_pp, w_bg, w_pf, hg, flag, hg_b, flag_b


def _c_weights(qk_gain_c, w_q_b, w_kv_b, seq):
    src_q, src_k, gain_idx, head, pe_src, rope_j, rope_half = _c_layout_tables()
    take = lambda w, src: jnp.where(jnp.asarray(src >= 0), jnp.take(w, jnp.asarray(np.maximum(src, 0)), axis=-1), 0.0)
    wq = take(w_q_b, src_q).astype(BF16)
    wk = take(w_kv_b, src_k).astype(BF16)
    v_src = np.array([h * (C_NOPE + C_V) + C_NOPE + c for h in range(C_HEADS) for c in range(C_V)])
    wv = jnp.take(w_kv_b, jnp.asarray(v_src), axis=-1).astype(BF16)
    ppe = np.zeros((LANES, C_QK_W), np.float32)
    ln = np.nonzero(pe_src >= 0)[0]
    ppe[pe_src[ln], ln] = 1
    hp = head[:C_PAIR]
    grp = ((hp[:, None] == hp[None, :]) & (hp[:, None] >= 0)).astype(np.float32)
    g_take = lambda g: jnp.where(jnp.asarray(gain_idx >= 0), jnp.take(g, jnp.asarray(np.maximum(gain_idx, 0)), axis=-1), 0.0)
    gq = g_take(qk_gain_c[:, 0]) * ((C_NOPE + C_ROPE) ** -0.5 * LOG2E)
    gk = g_take(qk_gain_c[:, 1])
    freq = ROPE_THETA ** (-jnp.arange(C_ROPE_HALF, dtype=F32) / C_ROPE_HALF)
    ang = jnp.arange(seq).astype(F32)[:, None] * freq[None, :]
    cos_j, sin_j = jnp.cos(ang), jnp.sin(ang)
    place = np.zeros((C_ROPE_HALF, C_PAIR), np.float32)
    lanes = np.nonzero(rope_j[:C_PAIR] >= 0)[0]
    place[rope_j[lanes], lanes] = 1
    is_rope = jnp.asarray(rope_j[:C_PAIR] >= 0)
    hi = lax.Precision.HIGHEST
    cos = jnp.where(is_rope, jnp.dot(cos_j, jnp.asarray(place), precision=hi), 1.0)
    sin = jnp.dot(sin_j, jnp.asarray(place), precision=hi)
    s1 = jnp.where(jnp.asarray(rope_half[:C_PAIR] == 0), -sin, 0.0)
    s2 = jnp.where(jnp.asarray(rope_half[:C_PAIR] == 1), sin, 0.0)
    return wq, wk, wv, jnp.asarray(ppe, BF16), jnp.asarray(grp, BF16), gq, gk, cos, s1, s2


def _a_bias_tiles(table):
    dist_vec = np.arange(-(BLOCK - 1), A_BIAS_TILES * BLOCK)
    vec = table[_t5_bucket(jnp.asarray(dist_vec))].T.astype(F32)
    windows = jnp.stack([vec[:, d * BLOCK:d * BLOCK + 2 * BLOCK - 1] for d in range(A_BIAS_TILES)], axis=1)
    return _toeplitz(windows, BLOCK, BLOCK)


def kernel(x, norm_gain, w_in, qk_gain_a, qk_gain_b, qk_gain_c, qk_gain_d, c_q_gain, c_kv_gain,
           w_q_b, w_kv_b, sinks, rel_bias, w_branch, w_out):
    batch, seq, d_model = x.shape
    depth = w_in.shape[0]
    t = batch * seq
    x2 = x.reshape(t, d_model)

    w_pn, w_pp, w_bg, w_pf, hg, flag, hg_b, flag_b = _layer_weights(w_in, qk_gain_a, qk_gain_b, qk_gain_d)
    wq, wk, wv, ppe, grp, gq, gk, cos, s1, s2 = _c_weights(qk_gain_c, w_q_b, w_kv_b, seq)
    wb = w_branch.astype(BF16)
    wo = w_out.astype(BF16)

    rel_bias2 = rel_bias * LOG2E
    bias_a = _a_bias_tiles(rel_bias2[:, :A_HEADS])
    bias_b = [_band_bias(rel_bias2[:, A_HEADS + g * B_HEADS:A_HEADS + (g + 1) * B_HEADS], dil, window // dil)
              for g, (window, dil) in enumerate(B_GROUPS)]
    bias_d = _band_bias(rel_bias2[:, N_BIAS_HEADS - D_HEADS:], 1, D_WINDOW - 1)

    xn = _norm(x2, norm_gain[0])
    for l in range(depth):
        pn = _proj(xn, w_pn[l], BF16, head_gain=hg[l], flag=flag)
        pp = _proj(xn, w_pp[l], BF16)
        pf = _proj(xn, w_pf[l], F32)

        sel = _a_select(pf, batch, seq)
        ya = _a_attn(pn, pp, sel, bias_a, batch, seq)

        yb, lb = [], []
        for g, (window, dil) in enumerate(B_GROUPS):
            bg = _proj(xn, w_bg[g][l], BF16, head_gain=hg_b[l], flag=flag_b, dil=dil, seq=seq)
            o, lse = _band_attn(bg, bg, bg, bias_b[g], batch * dil, seq // dil, 0, 1, 2, BRANCH_WIDTH,
                                want_lse=True)
            yb.append(o)
            lb.append(lse)

        qc, kc, vc = _c_prep(pf, c_q_gain[l].reshape(1, -1), c_kv_gain[l].reshape(1, -1), wq[l], wk[l], wv[l],
                             ppe, grp, gq[l].reshape(1, -1), gk[l].reshape(1, -1), cos, s1, s2, seq)
        yc = _c_attn(qc, kc, vc, batch, seq)

        (yd,) = _band_attn(pn, pn, pn, bias_d, batch, seq, PN_DQ // BRANCH_WIDTH, PN_DK // 256, PN_DV // 256,
                           256, sinks=sinks[l] * LOG2E)

        if l + 1 < depth:
            x2, xn = _merge(x2, ya, yb, lb, yc, yd, pp, wb[l], wo[l], batch, seq, next_gain=norm_gain[l + 1])
        else:
            (x2,) = _merge(x2, ya, yb, lb, yc, yd, pp, wb[l], wo[l], batch, seq)
    return x2.reshape(batch, seq, d_model)
```

```python
import functools
import math

import numpy as np
import jax
import jax.numpy as jnp
from jax import lax
from jax.experimental import pallas as pl
from jax.experimental.pallas import tpu as pltpu

F32 = jnp.float32
BF16 = jnp.bfloat16

D_MODEL = 1024
BLOCK = 128
HEAD_DIM = 64
N_BRANCH = 4
BRANCH_WIDTH = 512
EPS = 1e-6
A_HEADS = 8
IDX_HEADS = 8
IDX_DIM = 32
TOPK_MAX = 256
B_GROUPS = ((128, 1), (512, 4), (2048, 16))
B_HEADS = 8
C_HEADS = 8
C_NOPE = 64
C_ROPE = 32
C_V = 64
C_Q_LORA = 256
C_KV_LORA = 128
ROPE_THETA = 10000.0
D_HEADS = 8
D_KV_HEADS = 2
D_WINDOW = 128
NUM_BUCKETS = 32
MAX_DISTANCE = 2048
N_BIAS_HEADS = A_HEADS + len(B_GROUPS) * B_HEADS + D_HEADS

LANES = 128
MXU_WIDTH = 256
NEG = -1e30
LOG2E = math.log2(math.e)
INT_MIN = -(2 ** 31)

DKV_DUP_W = 2 * D_KV_HEADS * HEAD_DIM
PN_AQ = 0
PN_AK = PN_AQ + A_HEADS * HEAD_DIM
PN_DQ = PN_AK + A_HEADS * HEAD_DIM
PN_DK = PN_DQ + D_HEADS * HEAD_DIM
PN_DV = PN_DK + DKV_DUP_W
PN_W = PN_DV + DKV_DUP_W
PP_G = 0
PP_AV = PP_G + N_BRANCH * D_MODEL
PP_AZ = PP_AV + A_HEADS * HEAD_DIM
PP_BZ = PP_AZ + BRANCH_WIDTH
PP_CZ = PP_BZ + BRANCH_WIDTH
PP_DZ = PP_CZ + BRANCH_WIDTH
PF_HALF = 512

A_BIAS_TILES = MAX_DISTANCE // BLOCK + 2


def _cparams(sem, vmem_mb=48):
    return pltpu.CompilerParams(dimension_semantics=sem, vmem_limit_bytes=vmem_mb * 1024 * 1024)


def _t5_bucket(dist):
    max_exact = NUM_BUCKETS // 2
    d = jnp.maximum(dist, 0)
    logd = jnp.log(jnp.maximum(d, 1).astype(F32) / max_exact)
    large = max_exact + (logd / math.log(MAX_DISTANCE / max_exact) * (NUM_BUCKETS - max_exact)).astype(jnp.int32)
    return jnp.where(d < max_exact, d, jnp.minimum(large, NUM_BUCKETS - 1))


def _rms_bf16(x, gain):
    return (x * lax.rsqrt(jnp.mean(x * x, axis=-1, keepdims=True) + EPS) * gain).astype(BF16)


def _norm_kernel(x_ref, g_ref, o_ref):
    o_ref[...] = _rms_bf16(x_ref[...], g_ref[...])


def _norm(x2, gain, tm=1024):
    t, d = x2.shape
    return pl.pallas_call(
        _norm_kernel,
        grid=(t // tm,),
        in_specs=[pl.BlockSpec((tm, d), lambda i: (i, 0)), pl.BlockSpec((1, d), lambda i: (0, 0))],
        out_specs=pl.BlockSpec((tm, d), lambda i: (i, 0)),
        out_shape=jax.ShapeDtypeStruct((t, d), BF16),
        compiler_params=_cparams(("parallel",)),
        name="norm",
    )(x2, gain.reshape(1, d))


def _proj_kernel(xn_ref, w_ref, *rest, norm, dil):
    rest = list(rest)
    res_ref = rest.pop(-1) if dil is not None and dil > 1 else None
    if norm:
        hg_ref, flag_ref, bd_ref, o_ref = rest
    else:
        (o_ref,) = rest

    def emit(h):
        if dil is None:
            o_ref[...] = h.astype(o_ref.dtype)
        elif dil == 1:
            o_ref[0, 0] = h.astype(o_ref.dtype)
        else:
            sub = h.shape[0] // dil
            for c in range(h.shape[1] // LANES):
                res_ref[c] = h[:, c * LANES:(c + 1) * LANES]
            for r in range(dil):
                o_ref[0, r] = jnp.concatenate(
                    [res_ref[c, pl.ds(r, sub, stride=dil), :] for c in range(h.shape[1] // LANES)],
                    axis=1).astype(o_ref.dtype)

    h = jnp.dot(xn_ref[...], w_ref[...], preferred_element_type=F32)
    if norm:
        sq = (h * h).astype(BF16)
        slab = bd_ref.shape[0]
        ss = jnp.concatenate([jnp.dot(sq[:, c:c + slab], bd_ref[...], preferred_element_type=F32)
                              for c in range(0, sq.shape[1], slab)], axis=1)
        scale = lax.rsqrt(ss * (1.0 / HEAD_DIM) + EPS) * hg_ref[...]
        h = h * jnp.where(flag_ref[...] > 0, scale, 1.0)
    emit(h)


def _proj(xn, w, out_dtype, head_gain=None, flag=None, dil=None, seq=None, tm=1024, tn=512):
    t, d = xn.shape
    n = w.shape[1]
    norm = head_gain is not None
    scratch = []
    if dil is None:
        out_spec = pl.BlockSpec((tm, tn), lambda i, j: (i, j))
        out_shape = jax.ShapeDtypeStruct((t, n), out_dtype)
        kdil = None
    else:
        nsb = seq // tm
        out_spec = pl.BlockSpec((1, dil, tm // dil, tn), lambda i, j: (i // nsb, 0, i % nsb, j))
        out_shape = jax.ShapeDtypeStruct((t // seq, dil, seq // dil, n), out_dtype)
        kdil = dil
        if dil > 1:
            scratch.append(pltpu.VMEM((tn // LANES, tm, LANES), F32))
    in_specs = [pl.BlockSpec((tm, d), lambda i, j: (i, 0)),
                pl.BlockSpec((d, tn), lambda i, j: (0, j))]
    args = [xn, w]
    if norm:
        lane = np.arange(MXU_WIDTH)
        bd = jnp.asarray((lane[:, None] // HEAD_DIM == lane[None, :] // HEAD_DIM), BF16)
        in_specs += [pl.BlockSpec((1, tn), lambda i, j: (0, j)),
                     pl.BlockSpec((1, tn), lambda i, j: (0, j)),
                     pl.BlockSpec((MXU_WIDTH, MXU_WIDTH), lambda i, j: (0, 0))]
        args += [head_gain.reshape(1, n), flag.reshape(1, n), bd]
    out = pl.pallas_call(
        functools.partial(_proj_kernel, norm=norm, dil=kdil),
        grid=(t // tm, n // tn),
        in_specs=in_specs,
        out_specs=out_spec,
        out_shape=out_shape,
        scratch_shapes=scratch,
        compiler_params=_cparams(("parallel", "parallel")),
        name=("proj_norm" if norm else "proj") + ("" if dil is None else "_dil%d" % dil),
    )(*args)
    return out.reshape(t, n)


C_PAIR = 256
C_QK_W = (C_HEADS // 2) * C_PAIR
C_ROPE_HALF = C_ROPE // 2


def _c_lane(h, c):
    base = (h // 2) * C_PAIR
    e = h % 2
    if c < C_NOPE:
        return base + e * C_NOPE + c
    return base + 2 * C_NOPE + e * C_ROPE + (c - C_NOPE)


def _c_layout_tables():
    src_q = np.full(C_QK_W, -1, np.int64)
    src_k = np.full(C_QK_W, -1, np.int64)
    gain_idx = np.full(C_QK_W, -1, np.int64)
    head = np.full(C_QK_W, -1, np.int64)
    pe_src = np.full(C_QK_W, -1, np.int64)
    rope_j = np.full(C_QK_W, -1, np.int64)
    rope_half = np.zeros(C_QK_W, np.int64)
    for h in range(C_HEADS):
        for c in range(C_NOPE + C_ROPE):
            ln = _c_lane(h, c)
            src_q[ln] = h * (C_NOPE + C_ROPE) + c
            gain_idx[ln] = c
            head[ln] = h
            if c < C_NOPE:
                src_k[ln] = h * (C_NOPE + C_V) + c
            else:
                r = c - C_NOPE
                pe_src[ln] = r
                rope_j[ln] = r % C_ROPE_HALF
                rope_half[ln] = r // C_ROPE_HALF
    return src_q, src_k, gain_idx, head, pe_src, rope_j, rope_half


def _c_prep_kernel(pf_ref, cqg_ref, ckvg_ref, wq_ref, wk_ref, wv_ref, ppe_ref, grp_ref,
                   gq_ref, gk_ref, cos_ref, s1_ref, s2_ref, q_ref, k_ref, v_ref):
    blk = pf_ref[...]
    cq = blk[:, :C_Q_LORA]
    ckv = blk[:, C_Q_LORA:C_Q_LORA + C_KV_LORA]
    pe = blk[:, C_Q_LORA + C_KV_LORA:]

    def rms(v, g):
        return v * lax.rsqrt(jnp.mean(v * v, axis=-1, keepdims=True) + EPS) * g

    def head_norm_rope(raw, gain):
        sq = (raw * raw).astype(BF16)
        ss = jnp.concatenate([jnp.dot(sq[:, p * C_PAIR:(p + 1) * C_PAIR], grp_ref[...], preferred_element_type=F32)
                              for p in range(C_QK_W // C_PAIR)], axis=1)
        y = raw * lax.rsqrt(ss * (1.0 / (C_NOPE + C_ROPE)) + EPS) * gain
        up = pltpu.roll(y, C_QK_W - C_ROPE_HALF, 1)
        dn = pltpu.roll(y, C_ROPE_HALF, 1)
        wide = lambda ref: jnp.concatenate([ref[...]] * (C_QK_W // C_PAIR), axis=1)
        return y * wide(cos_ref) + up * wide(s1_ref) + dn * wide(s2_ref)

    cqn = rms(cq, cqg_ref[...]).astype(BF16)
    q_raw = jnp.dot(cqn, wq_ref[...], preferred_element_type=F32)
    q_ref[...] = head_norm_rope(q_raw, gq_ref[...]).astype(BF16)

    ckvn = rms(ckv, ckvg_ref[...]).astype(BF16)
    pe_hi = pe.astype(BF16)
    pe_lo = (pe - pe_hi.astype(F32)).astype(BF16)
    k_raw = (jnp.dot(ckvn, wk_ref[...], preferred_element_type=F32)
             + jnp.dot(pe_hi, ppe_ref[...], preferred_element_type=F32)
             + jnp.dot(pe_lo, ppe_ref[...], preferred_element_type=F32))
    k_ref[...] = head_norm_rope(k_raw, gk_ref[...]).astype(BF16)
    v_ref[...] = jnp.dot(ckvn, wv_ref[...], preferred_element_type=F32).astype(BF16)


def _c_prep(pf, cqg, ckvg, wq, wk, wv, ppe, grp, gq, gk, cos, s1, s2, seq, tm=512):
    t = pf.shape[0]
    nsb = seq // tm
    full = lambda shape: pl.BlockSpec(shape, lambda i: (0,) * len(shape))
    tab = pl.BlockSpec((tm, C_PAIR), lambda i: (i % nsb, 0))
    return pl.pallas_call(
        _c_prep_kernel,
        grid=(t // tm,),
        in_specs=[pl.BlockSpec((tm, PF_HALF), lambda i: (i, 1)),
                  full((1, C_Q_LORA)), full((1, C_KV_LORA)),
                  full((C_Q_LORA, C_QK_W)), full((C_KV_LORA, C_QK_W)), full((C_KV_LORA, BRANCH_WIDTH)),
                  full((LANES, C_QK_W)), full((C_PAIR, C_PAIR)),
                  full((1, C_QK_W)), full((1, C_QK_W)), tab, tab, tab],
        out_specs=[pl.BlockSpec((tm, C_QK_W), lambda i: (i, 0)),
                   pl.BlockSpec((tm, C_QK_W), lambda i: (i, 0)),
                   pl.BlockSpec((tm, BRANCH_WIDTH), lambda i: (i, 0))],
        out_shape=[jax.ShapeDtypeStruct((t, C_QK_W), BF16),
                   jax.ShapeDtypeStruct((t, C_QK_W), BF16),
                   jax.ShapeDtypeStruct((t, BRANCH_WIDTH), BF16)],
        compiler_params=_cparams(("parallel",)),
        name="c_prep",
    )(pf, cqg, ckvg, wq, wk, wv, ppe, grp, gq, gk, cos, s1, s2)


def _flash_update(e, s, v_ones, m_ref, acc_ref, rows=slice(None)):
    m_prev = m_ref[e, rows]
    m_new = jnp.maximum(m_prev, jnp.max(s, axis=1, keepdims=True))
    alpha = jnp.exp2(m_prev - m_new)
    p = jnp.exp2(s - jnp.concatenate([m_new] * (s.shape[1] // LANES), axis=1))
    acc_ref[e, rows] = (jnp.concatenate([alpha, alpha], axis=1) * acc_ref[e, rows]
                        + jnp.dot(p.astype(BF16), v_ones, preferred_element_type=F32))
    m_ref[e, rows] = m_new


def _flash_init(m_ref, acc_ref):
    m_ref[...] = jnp.full(m_ref.shape, NEG, F32)
    acc_ref[...] = jnp.zeros(acc_ref.shape, F32)


def _flash_finish(o_ref, acc_ref):
    lane = lax.broadcasted_iota(jnp.int32, o_ref.shape, 1)
    o0 = acc_ref[0, :, :LANES] / acc_ref[0, :, LANES:]
    o1 = acc_ref[1, :, :LANES] / acc_ref[1, :, LANES:]
    o_ref[...] = jnp.where(lane < HEAD_DIM, o0, o1).astype(o_ref.dtype)


def _with_ones(v_chunk):
    return jnp.concatenate([v_chunk, jnp.ones(v_chunk.shape, v_chunk.dtype)], axis=1)


_CONTRACT_LANES = (((1,), (1,)), ((), ()))


def _head_masks(width, ranges0, ranges1):
    m = np.zeros((2, 1, width), np.float32)
    for e, ranges in enumerate((ranges0, ranges1)):
        for lo, hi in ranges:
            m[e, 0, lo:hi] = 1
    return jnp.asarray(m, BF16)


def _pair_masks():
    return _head_masks(LANES, [(0, HEAD_DIM)], [(HEAD_DIM, LANES)])


def _transpose_keys(k_ref, kt_ref, rows=512):
    def body(c, carry):
        off = pl.multiple_of(c * rows, rows)
        kt_ref[:, pl.ds(off, rows)] = k_ref[pl.ds(off, rows), :].astype(F32).T.astype(kt_ref.dtype)
        return carry
    lax.fori_loop(0, k_ref.shape[0] // rows, body, 0)


def _chunk_loop(chunk, n, unroll, carry=None, slots=None):
    start = 0
    width = unroll
    while width >= 1:
        def body(g, carry, width=width, start=start):
            for j in range(width):
                carry = chunk(start + g * width + j, carry) if slots is None else \
                    chunk(start + g * width + j, carry, j)
            return carry
        groups = (n - start) // width
        carry = lax.fori_loop(0, groups, body, carry)
        start = start + groups * width
        width //= 2
    return carry


def _c_attn_kernel(q_ref, hm_ref, k_ref, v_ref, o_ref, kt_ref, m_ref, acc_ref, *, tq, tk, unroll):
    qi = pl.program_id(2)

    @pl.when(qi == 0)
    def _():
        _transpose_keys(k_ref, kt_ref)

    q = q_ref[...]
    qs = (q * hm_ref[0], q * hm_ref[1])
    _flash_init(m_ref, acc_ref)
    n_full = (qi * tq) // tk

    def chunk(c, row0=0, diagonal=False):
        off = pl.multiple_of(c * tk, tk)
        kc = kt_ref[:, pl.ds(off, tk)]
        vc = _with_ones(v_ref[pl.ds(off, tk), :])
        for e in range(2):
            if not diagonal:
                rows = slice(row0, tq)
                _flash_update(e, jnp.dot(qs[e][rows], kc, preferred_element_type=F32), vc, m_ref, acc_ref, rows)
                continue
            rows = slice(row0, row0 + tk)
            s = jnp.dot(qs[e][rows], kc, preferred_element_type=F32)
            s = jnp.where(lax.broadcasted_iota(jnp.int32, s.shape, 1) <= lax.broadcasted_iota(jnp.int32, s.shape, 0),
                          s, NEG)
            _flash_update(e, s, vc, m_ref, acc_ref, rows)
            if row0 + tk < tq:
                rows = slice(row0 + tk, tq)
                _flash_update(e, jnp.dot(qs[e][rows], kc, preferred_element_type=F32), vc, m_ref, acc_ref, rows)

    _chunk_loop(lambda c, _: chunk(c), n_full, unroll)
    for j in range(tq // tk):
        chunk(n_full + j, row0=j * tk, diagonal=True)
    _flash_finish(o_ref, acc_ref)


def _c_attn(qc, kc, vc, batch, seq, tq=1024, tk=512, unroll=4):
    t = qc.shape[0]
    nq = seq // tq
    npair = C_HEADS // 2
    return pl.pallas_call(
        functools.partial(_c_attn_kernel, tq=tq, tk=tk, unroll=unroll),
        grid=(batch, npair, nq),
        in_specs=[pl.BlockSpec((tq, C_PAIR), lambda b, p, i: (b * nq + i, p)),
                  pl.BlockSpec((2, 1, C_PAIR), lambda b, p, i: (0, 0, 0)),
                  pl.BlockSpec((seq, C_PAIR), lambda b, p, i: (b, p)),
                  pl.BlockSpec((seq, LANES), lambda b, p, i: (b, p))],
        out_specs=pl.BlockSpec((tq, LANES), lambda b, p, i: (b * nq + i, p)),
        out_shape=jax.ShapeDtypeStruct((t, BRANCH_WIDTH), BF16),
        scratch_shapes=[pltpu.VMEM((C_PAIR, seq), BF16),
                        pltpu.VMEM((2, tq, LANES), F32), pltpu.VMEM((2, tq, 2 * LANES), F32)],
        compiler_params=_cparams(("parallel", "parallel", "arbitrary")),
        name="c_attn",
    )(qc, _head_masks(C_PAIR, [(0, C_NOPE), (2 * C_NOPE, 2 * C_NOPE + C_ROPE)],
                      [(C_NOPE, 2 * C_NOPE), (2 * C_NOPE + C_ROPE, 2 * C_NOPE + 2 * C_ROPE)]), kc, vc)


def _a_attn_kernel(q_ref, hm_ref, k_ref, v_ref, sel_ref, bias_ref, o_ref, kt_ref, m_ref, acc_ref, msk_ref,
                   *, tq, tk, unroll):
    qi = pl.program_id(2)

    @pl.when(qi == 0)
    def _():
        _transpose_keys(k_ref, kt_ref)

    q = q_ref[...]
    qs = (q * hm_ref[0], q * hm_ref[1])
    _flash_init(m_ref, acc_ref)
    n_chunks = ((qi + 1) * tq + tk - 1) // tk

    def chunk(c, carry, slot):
        off = pl.multiple_of(c * tk, tk)
        kc = kt_ref[:, pl.ds(off, tk)]
        vc = _with_ones(v_ref[pl.ds(off, tk), :])
        msk_ref[slot] = sel_ref[:, pl.ds(off, tk)].astype(F32)
        for e in range(2):
            s = jnp.dot(qs[e], kc, preferred_element_type=F32)
            rows = []
            for i in range(tq // BLOCK):
                tiles = []
                for j in range(tk // BLOCK):
                    d = (qi * (tq // BLOCK) + i) - (c * (tk // BLOCK) + j)
                    d = jnp.clip(d, 0, A_BIAS_TILES - 1)
                    tiles.append(bias_ref[e, d])
                rows.append(jnp.concatenate(tiles, axis=1))
            s = s + jnp.concatenate(rows, axis=0) + msk_ref[slot]
            _flash_update(e, s, vc, m_ref, acc_ref)
        return carry

    _chunk_loop(chunk, n_chunks, unroll, slots=unroll)
    _flash_finish(o_ref, acc_ref)


def _a_attn(pn, pp, sel, bias_tiles, batch, seq, tq=512, tk=512, unroll=4):
    t = pn.shape[0]
    nq = seq // tq
    npair = A_HEADS // 2
    qcol, kcol, vcol = PN_AQ // LANES, PN_AK // LANES, PP_AV // LANES
    return pl.pallas_call(
        functools.partial(_a_attn_kernel, tq=tq, tk=tk, unroll=unroll),
        grid=(batch, npair, nq),
        in_specs=[pl.BlockSpec((tq, LANES), lambda b, p, i: (b * nq + i, qcol + p)),
                  pl.BlockSpec((2, 1, LANES), lambda b, p, i: (0, 0, 0)),
                  pl.BlockSpec((seq, LANES), lambda b, p, i: (b, kcol + p)),
                  pl.BlockSpec((seq, LANES), lambda b, p, i: (b, vcol + p)),
                  pl.BlockSpec((tq, seq), lambda b, p, i: (b * nq + i, 0)),
                  pl.BlockSpec((2, A_BIAS_TILES, BLOCK, BLOCK), lambda b, p, i: (p, 0, 0, 0))],
        out_specs=pl.BlockSpec((tq, LANES), lambda b, p, i: (b * nq + i, p)),
        out_shape=jax.ShapeDtypeStruct((t, BRANCH_WIDTH), BF16),
        scratch_shapes=[pltpu.VMEM((LANES, seq), BF16),
                        pltpu.VMEM((2, tq, LANES), F32), pltpu.VMEM((2, tq, 2 * LANES), F32),
                        pltpu.VMEM((unroll, tq, tk), F32)],
        compiler_params=_cparams(("parallel", "parallel", "arbitrary")),
        name="a_attn",
    )(pn, _pair_masks(), pn, pp, sel, bias_tiles)


def _sortable_key(score):
    bits = pltpu.bitcast(score, jnp.int32)
    return bits ^ ((bits >> 31) & jnp.int32(0x7FFFFFFF))


def _bit_transpose32(words):
    a = list(words)
    j, m = 16, 0x0000FFFF
    while j:
        for k in range(32):
            if k & j:
                continue
            t = (a[k] ^ lax.shift_right_logical(a[k + j], jnp.int32(j))) & m
            a[k] = a[k] ^ t
            a[k + j] = a[k + j] ^ lax.shift_left(t, jnp.int32(j))
        j >>= 1
        m ^= (m << j) & 0xFFFFFFFF
    return a


SEL_ROWS = 512
PLANE_ROWS = 256
SCORE_UNROLL = 2


def _a_select_kernel(qblk_ref, kblk_ref, ph_ref, pl_ref, pkh_ref, pkl_ref, sel_ref,
                     ikx_ref, keys_ref, planes_ref, eq_ref, iqt_ref, jcut_ref, *, seq, k_sel):
    qi = pl.program_id(1)
    n_sel = (qi * BLOCK + BLOCK + SEL_ROWS - 1) // SEL_ROWS

    @pl.when(qi == 0)
    def _():
        def prep(c, carry):
            off = pl.multiple_of(c * SEL_ROWS, SEL_ROWS)
            kv = kblk_ref[pl.ds(off, SEL_ROWS), :]
            hi = kv.astype(BF16)
            lo = (kv - hi.astype(F32)).astype(BF16)
            ikx_ref[pl.ds(off, SEL_ROWS), :] = (
                jnp.dot(hi, pkh_ref[...], preferred_element_type=F32)
                + jnp.dot(lo, pkl_ref[...], preferred_element_type=F32)).astype(BF16)
            return carry
        lax.fori_loop(0, seq // SEL_ROWS, prep, 0)

        def clear(g, carry):
            planes_ref[g] = jnp.zeros(planes_ref.shape[1:], jnp.int32)
            return carry
        lax.fori_loop(0, planes_ref.shape[0], clear, 0)

    qb = qblk_ref[...]
    iq = qb[:, :IDX_HEADS * IDX_DIM]
    iq_hi = iq.astype(BF16)
    iq_lo = (iq - iq_hi.astype(F32)).astype(BF16)
    iqx = (jnp.dot(iq_hi, ph_ref[...], preferred_element_type=F32)
           + jnp.dot(iq_lo, pl_ref[...], preferred_element_type=F32))
    for h in range(IDX_HEADS):
        iqt_ref[h // 2, :, (h % 2) * LANES:(h % 2 + 1) * LANES] = iqx[:, h * LANES:(h + 1) * LANES].T.astype(BF16)
    iw_t = qb[:, IDX_HEADS * IDX_DIM:IDX_HEADS * IDX_DIM + LANES].T
    iw_rows = [iw_t[IDX_DIM + h:IDX_DIM + h + 1, :] for h in range(IDX_HEADS)]
    q_pos = qi * BLOCK + lax.broadcasted_iota(jnp.int32, (SEL_ROWS, BLOCK), 1)
    k_row = lax.broadcasted_iota(jnp.int32, (SEL_ROWS, BLOCK), 0)

    def score_chunk(c, masked):
        off = pl.multiple_of(c * SEL_ROWS, SEL_ROWS)
        kx = ikx_ref[pl.ds(off, SEL_ROWS), :]
        sc = jnp.zeros((SEL_ROWS, BLOCK), F32)
        for hp in range(IDX_HEADS // 2):
            xx = jnp.dot(kx, iqt_ref[hp], preferred_element_type=F32)
            sc = (sc + jnp.maximum(xx[:, :LANES], 0.0) * iw_rows[2 * hp]
                  + jnp.maximum(xx[:, LANES:], 0.0) * iw_rows[2 * hp + 1])
        key = _sortable_key(sc + 0.0)
        if masked:
            key = jnp.where(off + k_row <= q_pos, key, INT_MIN)
        keys_ref[pl.ds(off, SEL_ROWS), :] = key
        u = key ^ INT_MIN
        for g in range(SEL_ROWS // PLANE_ROWS):
            words = [u[g * PLANE_ROWS + 8 * j:g * PLANE_ROWS + 8 * j + 8] for j in range(32)]
            for i, plane in enumerate(_bit_transpose32(words)):
                planes_ref[c * (SEL_ROWS // PLANE_ROWS) + g, i] = plane

    _chunk_loop(lambda c, _: score_chunk(c, False), n_sel - 1, SCORE_UNROLL)
    score_chunk(n_sel - 1, True)

    def per_query_total(counts):
        while len(counts) > 1:
            counts = [a + b for a, b in zip(counts[::2], counts[1::2])] + counts[len(counts) & ~1:]
        return jnp.sum(counts[0], axis=0, keepdims=True)

    n_grp = n_sel * (SEL_ROWS // PLANE_ROWS)
    all_groups = eq_ref.shape[0]
    for g in range(all_groups):
        eq_ref[g] = jnp.where(g < n_grp, jnp.full((8, BLOCK), -1, jnp.int32), jnp.zeros((8, BLOCK), jnp.int32))

    def sweep(cur_plane, prev_plane, prev_keep):
        counts = []
        for g in range(all_groups):
            eq = eq_ref[g]
            if prev_plane is not None:
                eq = eq & (planes_ref[g, prev_plane] ^ ~prev_keep)
                eq_ref[g] = eq
            ones = eq if cur_plane is None else eq & planes_ref[g, cur_plane]
            counts.append(lax.population_count(ones))
        return per_query_total(counts)

    def decide(bit, cnt_ones, thr_u, above):
        total = above + cnt_ones
        keep = jnp.where(total >= k_sel, -1, 0)
        return thr_u | (keep & bit), jnp.where(keep < 0, above, total), keep

    zeros = jnp.zeros((1, BLOCK), jnp.int32)
    state = decide(INT_MIN, sweep(0, None, None), zeros, zeros)

    def radix_step(it, state):
        thr_u, above, keep = state
        bit = lax.shift_left(jnp.int32(1), jnp.int32(31) - it)
        return decide(bit, sweep(it, it - 1, keep), thr_u, above)

    thr_u, above, keep = lax.fori_loop(1, 32, radix_step, state)
    cnt_thr = above + sweep(None, 31, keep)
    thr = thr_u ^ INT_MIN

    has_k = thr > INT_MIN
    thr = jnp.maximum(thr, INT_MIN + 1)

    jcut_ref[...] = jnp.full(jcut_ref.shape, 2 ** 31 - 1, jnp.int32)

    @pl.when(jnp.max(jnp.where(has_k, cnt_thr, 0)) > k_sel)
    def _():
        excess = jnp.where(has_k, cnt_thr - k_sel, 0)
        need = jnp.where(excess > 0, k_sel - above, 1)
        sub = lax.broadcasted_iota(jnp.int32, (8, BLOCK), 0)

        def ties_below(t):
            counts = []
            for g in range(all_groups):
                words = jnp.clip((t - g * PLANE_ROWS - sub + 7) >> 3, 0, 32)
                top = jnp.where(words > 0, lax.shift_left(jnp.int32(-1), 32 - words), 0)
                counts.append(lax.population_count(eq_ref[g] & top))
            return per_query_total(counts)

        def tie_bisect(it, t):
            cand = t + lax.shift_left(jnp.int32(1), jnp.int32(seq.bit_length() - 2) - it)
            return jnp.where(ties_below(cand) < need, cand, t)

        last = lax.fori_loop(0, seq.bit_length() - 1, tie_bisect, jnp.zeros((1, BLOCK), jnp.int32))
        jcut = jnp.where(excess > 0, last, 2 ** 31 - 1)
        jcut_ref[...] = jnp.broadcast_to(jcut, jcut_ref.shape)

    jcut = jcut_ref[0:1, :]

    def emit(c, carry):
        off = pl.multiple_of(c * SEL_ROWS, SEL_ROWS)
        keys = keys_ref[pl.ds(off, SEL_ROWS), :]
        bar = jnp.where(off + k_row > jcut, thr + 1, thr)
        add = jnp.where(keys >= bar, 0.0, NEG)
        sel_ref[:, pl.ds(off, SEL_ROWS)] = jnp.concatenate(
            [add[j * BLOCK:(j + 1) * BLOCK].T for j in range(SEL_ROWS // BLOCK)], axis=1).astype(BF16)
        return carry

    _chunk_loop(emit, n_sel, SCORE_UNROLL)

    def fill(c, carry):
        off = pl.multiple_of(c * SEL_ROWS, SEL_ROWS)
        sel_ref[:, pl.ds(off, SEL_ROWS)] = jnp.full((BLOCK, SEL_ROWS), NEG, BF16)
        return carry

    lax.fori_loop(n_sel, seq // SEL_ROWS, fill, 0)


def _a_select(pf, batch, seq):
    t = pf.shape[0]
    nq = seq // BLOCK
    k_sel = min(TOPK_MAX, seq // 4)
    n_groups = seq // PLANE_ROWS
    nlane = IDX_HEADS * IDX_DIM
    r = np.arange(nlane)
    ph = np.zeros((nlane, IDX_HEADS * LANES), np.float32)
    plo = np.zeros((nlane, IDX_HEADS * LANES), np.float32)
    ph[r, (r // IDX_DIM) * LANES + r % IDX_DIM] = 1
    ph[r, (r // IDX_DIM) * LANES + 2 * IDX_DIM + r % IDX_DIM] = 1
    plo[r, (r // IDX_DIM) * LANES + IDX_DIM + r % IDX_DIM] = 1
    d = np.arange(IDX_DIM)
    pkh = np.zeros((LANES, LANES), np.float32)
    pkl = np.zeros((LANES, LANES), np.float32)
    pkh[d, d] = 1
    pkh[d, IDX_DIM + d] = 1
    pkl[d, 2 * IDX_DIM + d] = 1
    full = lambda shape: pl.BlockSpec(shape, lambda b, i: (0,) * len(shape))
    return pl.pallas_call(
        functools.partial(_a_select_kernel, seq=seq, k_sel=k_sel),
        grid=(batch, nq),
        in_specs=[pl.BlockSpec((BLOCK, PF_HALF), lambda b, i: (b * nq + i, 0)),
                  pl.BlockSpec((seq, LANES), lambda b, i: (b, nlane // LANES)),
                  full(ph.shape), full(plo.shape), full(pkh.shape), full(pkl.shape)],
        out_specs=pl.BlockSpec((BLOCK, seq), lambda b, i: (b * nq + i, 0)),
        out_shape=jax.ShapeDtypeStruct((t, seq), BF16),
        scratch_shapes=[pltpu.VMEM((seq, LANES), BF16), pltpu.VMEM((seq, LANES), jnp.int32),
                        pltpu.VMEM((n_groups, 32, 8, LANES), jnp.int32),
                        pltpu.VMEM((n_groups, 8, LANES), jnp.int32),
                        pltpu.VMEM((IDX_HEADS // 2, LANES, 2 * LANES), BF16),
                        pltpu.VMEM((8, LANES), jnp.int32)],
        compiler_params=_cparams(("parallel", "arbitrary")),
        name="a_select",
    )(pf, pf, jnp.asarray(ph, BF16), jnp.asarray(plo, BF16), jnp.asarray(pkh, BF16),
      jnp.asarray(pkl, BF16))


BAND_RBLK = 4


def _band_kernel(*refs, has_sink, want_lse, kv_div, rblk):
    if has_sink:
        sink_ref, refs = refs[0], refs[1:]
    q_ref, hm_ref, kp_ref, kc_ref, vp_ref, vc_ref, bias_ref, o_ref = refs[:8]
    first = pl.program_id(1) == 0
    lane = lax.broadcasted_iota(jnp.int32, (BLOCK, LANES), 1)
    col = lax.broadcasted_iota(jnp.int32, (BLOCK, 2 * BLOCK), 1)
    no_prev = jnp.logical_and(first, col < BLOCK)
    for p in range(BRANCH_WIDTH // LANES):
        kcol = (p // kv_div) * LANES
        kk = jnp.concatenate([kp_ref[:, kcol:kcol + LANES], kc_ref[:, kcol:kcol + LANES]], axis=0)
        vv = _with_ones(jnp.concatenate([vp_ref[:, kcol:kcol + LANES], vc_ref[:, kcol:kcol + LANES]], axis=0))
        for i in range(rblk):
            rows = slice(i * BLOCK, (i + 1) * BLOCK)
            q = q_ref[rows, p * LANES:(p + 1) * LANES]
            kblk = kk[i * BLOCK:(i + 2) * BLOCK]
            vblk = vv[i * BLOCK:(i + 2) * BLOCK]
            outs, lses = [], []
            for e in range(2):
                s = lax.dot_general(q * hm_ref[e], kblk, _CONTRACT_LANES, preferred_element_type=F32)
                s = s + bias_ref[2 * p + e]
                if i == 0:
                    s = jnp.where(no_prev, NEG, s)
                m = jnp.max(s, axis=1, keepdims=True)
                if has_sink:
                    sk = sink_ref[2 * p + e]
                    m = jnp.maximum(m, sk)
                pr = jnp.exp2(s - m)
                acc = jnp.dot(pr.astype(BF16), vblk, preferred_element_type=F32)
                den = acc[:, LANES:]
                if has_sink:
                    den = den + jnp.exp2(sk - m)
                outs.append(acc[:, :LANES] / den)
                lses.append(m + jnp.log2(den))
            o_ref[rows, p * LANES:(p + 1) * LANES] = jnp.where(lane < HEAD_DIM, outs[0], outs[1]).astype(o_ref.dtype)
            if want_lse:
                refs[8][rows, p * LANES:(p + 1) * LANES] = jnp.where(lane < HEAD_DIM, lses[0], lses[1])


def _band_attn(q_arr, k_arr, v_arr, bias, nsub, sub_len, qcb, kcb, vcb, kv_w, sinks=None, want_lse=False):
    t = q_arr.shape[0]
    rblk = min(BAND_RBLK, sub_len // BLOCK)
    step_rows = rblk * BLOCK
    nbs = sub_len // step_rows
    kv_div = BRANCH_WIDTH // kv_w

    def cur(u, n):
        return u * nbs + n

    def prev(u, n):
        return jnp.where(n == 0, u * nbs * rblk, (u * nbs + n) * rblk - 1)

    in_specs = [
        pl.BlockSpec((step_rows, BRANCH_WIDTH), lambda u, n: (cur(u, n), qcb)),
        pl.BlockSpec((2, 1, LANES), lambda u, n: (0, 0, 0)),
        pl.BlockSpec((BLOCK, kv_w), lambda u, n: (prev(u, n), kcb)),
        pl.BlockSpec((step_rows, kv_w), lambda u, n: (cur(u, n), kcb)),
        pl.BlockSpec((BLOCK, kv_w), lambda u, n: (prev(u, n), vcb)),
        pl.BlockSpec((step_rows, kv_w), lambda u, n: (cur(u, n), vcb)),
        pl.BlockSpec(bias.shape, lambda u, n: (0, 0, 0)),
    ]
    args = [q_arr, _pair_masks(), k_arr, k_arr, v_arr, v_arr, bias]
    if sinks is not None:
        in_specs = [pl.BlockSpec(memory_space=pltpu.SMEM)] + in_specs
        args = [sinks] + args
    o_spec = pl.BlockSpec((step_rows, BRANCH_WIDTH), lambda u, n: (cur(u, n), 0))
    out_specs = [o_spec]
    out_shape = [jax.ShapeDtypeStruct((t, BRANCH_WIDTH), BF16)]
    if want_lse:
        out_specs.append(o_spec)
        out_shape.append(jax.ShapeDtypeStruct((t, BRANCH_WIDTH), F32))
    return pl.pallas_call(
        functools.partial(_band_kernel, has_sink=sinks is not None, want_lse=want_lse, kv_div=kv_div, rblk=rblk),
        grid=(nsub, nbs),
        in_specs=in_specs,
        out_specs=out_specs,
        out_shape=out_shape,
        compiler_params=_cparams(("parallel", "arbitrary")),
        name="band_attn",
    )(*args)


def _toeplitz(w, rows, cols):
    length = rows + cols - 1
    u = jnp.pad(w[..., ::-1], [(0, 0)] * (w.ndim - 1) + [(0, 1)])
    flat = jnp.broadcast_to(u[..., None, :], w.shape[:-1] + (rows, length + 1)).reshape(w.shape[:-1] + (-1,))
    skew = flat[..., :rows * length].reshape(w.shape[:-1] + (rows, length))
    return skew[..., rows - 1:rows - 1 + cols]


def _band_bias(table, step, max_dist):
    rel = BLOCK + np.arange(BLOCK)[:, None] - np.arange(2 * BLOCK)[None, :]
    rel_vec = np.arange(BLOCK - (2 * BLOCK - 1), 2 * BLOCK)
    bias = _toeplitz(table[_t5_bucket(jnp.asarray(rel_vec * step))].T.astype(F32), BLOCK, 2 * BLOCK)
    ok = (rel >= 0) & (rel <= max_dist)
    return jnp.where(jnp.asarray(ok)[None], bias, NEG)


def _sigmoid(x):
    return 0.5 * jnp.tanh(0.5 * x) + 0.5


def _merge_kernel(x_ref, ya_ref, yb0_ref, yb1_ref, yb2_ref, l0_ref, l1_ref, l2_ref, yc_ref, yd_ref,
                  za_ref, zb_ref, zc_ref, zd_ref, ga_ref, gb_ref, gc_ref, gd_ref, wb_ref, wo_ref, *rest):
    if len(rest) == 7:
        ng_ref, o_ref, xn_ref, y1_ref, y2_ref, s1_ref, s2_ref = rest
    else:
        ng_ref = xn_ref = None
        o_ref, y1_ref, y2_ref, s1_ref, s2_ref = rest
    ncol = BRANCH_WIDTH // LANES
    for src, dst in ((yb1_ref, y1_ref), (yb2_ref, y2_ref), (l1_ref, s1_ref), (l2_ref, s2_ref)):
        dil = src.shape[1]
        for r in range(dil):
            blk = src[0, r].astype(F32)
            for c in range(ncol):
                dst[c, pl.ds(r, src.shape[2], stride=dil), :] = blk[:, c * LANES:(c + 1) * LANES]
    wide = lambda ref: jnp.concatenate([ref[c] for c in range(ncol)], axis=1)
    l0, l1, l2 = l0_ref[0, 0], wide(s1_ref), wide(s2_ref)
    mx = jnp.maximum(jnp.maximum(l0, l1), l2)
    w0, w1, w2 = jnp.exp2(l0 - mx), jnp.exp2(l1 - mx), jnp.exp2(l2 - mx)
    yb = (w0 * yb0_ref[0, 0].astype(F32) + w1 * wide(y1_ref) + w2 * wide(y2_ref)) / (w0 + w1 + w2)
    ys = (ya_ref[...].astype(F32), yb, yc_ref[...].astype(F32), yd_ref[...].astype(F32))
    zs = (za_ref, zb_ref, zc_ref, zd_ref)
    gs = (ga_ref, gb_ref, gc_ref, gd_ref)
    merged = jnp.zeros(o_ref.shape, F32)
    for n in range(N_BRANCH):
        z = zs[n][...].astype(F32)
        u = (ys[n] * (z * _sigmoid(z))).astype(BF16)
        proj = jnp.dot(u, wb_ref[n], preferred_element_type=F32)
        merged = merged + _sigmoid(gs[n][...].astype(F32)) * proj
    x_new = x_ref[...] + jnp.dot(merged.astype(BF16), wo_ref[...], preferred_element_type=F32)
    o_ref[...] = x_new
    if xn_ref is not None:
        xn_ref[...] = _rms_bf16(x_new, ng_ref[...])


def _merge(x2, ya, yb, lb, yc, yd, pp, wb, wo, batch, seq, next_gain=None, tm=512):
    t = x2.shape[0]
    bw = BRANCH_WIDTH
    nsb = seq // tm
    row = lambda w, c: pl.BlockSpec((tm, w), lambda i: (i, c))

    def sub(dil):
        return pl.BlockSpec((1, dil, tm // dil, bw), lambda i: (i // nsb, 0, i % nsb, 0))

    dils = [dil for _, dil in B_GROUPS]
    yb = [a.reshape(batch, dil, seq // dil, bw) for a, dil in zip(yb, dils)]
    lb = [a.reshape(batch, dil, seq // dil, bw) for a, dil in zip(lb, dils)]
    in_specs = ([row(D_MODEL, 0), row(bw, 0)] + [sub(dil) for dil in dils] * 2 + [row(bw, 0), row(bw, 0)]
                + [row(bw, PP_AZ // bw), row(bw, PP_BZ // bw), row(bw, PP_CZ // bw), row(bw, PP_DZ // bw)]
                + [row(D_MODEL, PP_G // D_MODEL + n) for n in range(N_BRANCH)]
                + [pl.BlockSpec((N_BRANCH, bw, D_MODEL), lambda i: (0, 0, 0)),
                   pl.BlockSpec((D_MODEL, D_MODEL), lambda i: (0, 0))])
    args = [x2, ya, yb[0], yb[1], yb[2], lb[0], lb[1], lb[2], yc, yd, pp, pp, pp, pp, pp, pp, pp, pp, wb, wo]
    out_specs = [row(D_MODEL, 0)]
    out_shape = [jax.ShapeDtypeStruct((t, D_MODEL), F32)]
    if next_gain is not None:
        in_specs.append(pl.BlockSpec((1, D_MODEL), lambda i: (0, 0)))
        args.append(next_gain.reshape(1, D_MODEL))
        out_specs.append(row(D_MODEL, 0))
        out_shape.append(jax.ShapeDtypeStruct((t, D_MODEL), BF16))
    return pl.pallas_call(
        _merge_kernel,
        grid=(t // tm,),
        in_specs=in_specs,
        out_specs=out_specs,
        out_shape=out_shape,
        scratch_shapes=[pltpu.VMEM((bw // LANES, tm, LANES), F32)] * 4,
        compiler_params=_cparams(("parallel",)),
        name="merge",
    )(*args)


def _w_in_columns():
    n_bq = len(B_GROUPS) * B_HEADS * HEAD_DIM
    groups = (
        (("a_q", A_HEADS * HEAD_DIM), ("a_k", A_HEADS * HEAD_DIM), ("a_v", A_HEADS * HEAD_DIM),
         ("a_z", BRANCH_WIDTH), ("a_iq", IDX_HEADS * IDX_DIM), ("a_ik", IDX_DIM), ("a_iw", IDX_HEADS)),
        (("b_q", n_bq), ("b_k", n_bq), ("b_v", n_bq), ("b_z", BRANCH_WIDTH)),
        (("c_q", C_Q_LORA), ("c_kv", C_KV_LORA), ("c_pe", C_ROPE), ("c_z", BRANCH_WIDTH)),
        (("d_q", D_HEADS * HEAD_DIM), ("d_k", D_KV_HEADS * HEAD_DIM), ("d_v", D_KV_HEADS * HEAD_DIM),
         ("d_z", BRANCH_WIDTH)),
        (("gates", N_BRANCH * D_MODEL),),
    )
    table, start = {}, 0
    for group in groups:
        for name, width in group:
            table[name] = (start, width)
            start += width
    return table


def _layer_weights(w_in, qk_gain_a, qk_gain_b, qk_gain_d):
    table = _w_in_columns()
    depth = w_in.shape[0]

    def cols(name, part=None):
        start, width = table[name]
        if part is not None:
            width //= part[1]
            start += part[0] * width
        return w_in[:, :, start:start + width]

    def dup_heads(a):
        return jnp.concatenate([a[..., :HEAD_DIM], a[..., :HEAD_DIM], a[..., HEAD_DIM:], a[..., HEAD_DIM:]], axis=-1)

    def zeros(width):
        return jnp.zeros((depth, D_MODEL, width), w_in.dtype)

    n_grp = len(B_GROUPS)
    w_pn = jnp.concatenate([cols("a_q"), cols("a_k"), cols("d_q"), dup_heads(cols("d_k")), dup_heads(cols("d_v"))],
                           axis=-1).astype(BF16)
    w_pp = jnp.concatenate([cols("gates"), cols("a_v"), cols("a_z"), cols("b_z"), cols("c_z"), cols("d_z")],
                           axis=-1).astype(BF16)
    w_bg = [jnp.concatenate([cols("b_q", (g, n_grp)), cols("b_k", (g, n_grp)), cols("b_v", (g, n_grp))],
                            axis=-1).astype(BF16) for g in range(n_grp)]
    idx = jnp.concatenate([cols("a_iq"), cols("a_ik"), cols("a_iw")], axis=-1)
    lora = jnp.concatenate([cols("c_q"), cols("c_kv"), cols("c_pe")], axis=-1)
    w_pf = jnp.concatenate([idx, zeros(PF_HALF - idx.shape[-1]), lora, zeros(PF_HALF - lora.shape[-1])],
                           axis=-1).astype(BF16)

    scale = HEAD_DIM ** -0.5 * LOG2E
    tile = lambda g, reps: jnp.tile(g, (1, reps))
    hg = jnp.concatenate([tile(qk_gain_a[:, 0] * scale, A_HEADS), tile(qk_gain_a[:, 1], A_HEADS),
                          tile(qk_gain_d[:, 0] * scale, D_HEADS), tile(qk_gain_d[:, 1], 2 * D_KV_HEADS),
                          jnp.ones((depth, PN_W - PN_DV), F32)], axis=-1)
    flag = jnp.concatenate([jnp.ones((PN_DV,), F32), jnp.zeros((PN_W - PN_DV,), F32)])
    hg_b = jnp.concatenate([tile(qk_gain_b[:, 0] * scale, B_HEADS), tile(qk_gain_b[:, 1], B_HEADS),
                            jnp.ones((depth, BRANCH_WIDTH), F32)], axis=-1)
    flag_b = jnp.concatenate([jnp.ones((2 * BRANCH_WIDTH,), F32), jnp.zeros((BRANCH_WIDTH,), F32)])
    return w_pn, w_pp, w_bg, w_pf, hg, flag, hg_b, flag_b


def _c_weights(qk_gain_c, w_q_b, w_kv_b, seq):
    src_q, src_k, gain_idx, head, pe_src, rope_j, rope_half = _c_layout_tables()
    take = lambda w, src: jnp.where(jnp.asarray(src >= 0), jnp.take(w, jnp.asarray(np.maximum(src, 0)), axis=-1), 0.0)
    wq = take(w_q_b, src_q).astype(BF16)
    wk = take(w_kv_b, src_k).astype(BF16)
    v_src = np.array([h * (C_NOPE + C_V) + C_NOPE + c for h in range(C_HEADS) for c in range(C_V)])
    wv = jnp.take(w_kv_b, jnp.asarray(v_src), axis=-1).astype(BF16)
    ppe = np.zeros((LANES, C_QK_W), np.float32)
    ln = np.nonzero(pe_src >= 0)[0]
    ppe[pe_src[ln], ln] = 1
    hp = head[:C_PAIR]
    grp = ((hp[:, None] == hp[None, :]) & (hp[:, None] >= 0)).astype(np.float32)
    g_take = lambda g: jnp.where(jnp.asarray(gain_idx >= 0), jnp.take(g, jnp.asarray(np.maximum(gain_idx, 0)), axis=-1), 0.0)
    gq = g_take(qk_gain_c[:, 0]) * ((C_NOPE + C_ROPE) ** -0.5 * LOG2E)
    gk = g_take(qk_gain_c[:, 1])
    freq = ROPE_THETA ** (-jnp.arange(C_ROPE_HALF, dtype=F32) / C_ROPE_HALF)
    ang = jnp.arange(seq).astype(F32)[:, None] * freq[None, :]
    cos_j, sin_j = jnp.cos(ang), jnp.sin(ang)
    place = np.zeros((C_ROPE_HALF, C_PAIR), np.float32)
    lanes = np.nonzero(rope_j[:C_PAIR] >= 0)[0]
    place[rope_j[lanes], lanes] = 1
    is_rope = jnp.asarray(rope_j[:C_PAIR] >= 0)
    hi = lax.Precision.HIGHEST
    cos = jnp.where(is_rope, jnp.dot(cos_j, jnp.asarray(place), precision=hi), 1.0)
    sin = jnp.dot(sin_j, jnp.asarray(place), precision=hi)
    s1 = jnp.where(jnp.asarray(rope_half[:C_PAIR] == 0), -sin, 0.0)
    s2 = jnp.where(jnp.asarray(rope_half[:C_PAIR] == 1), sin, 0.0)
    return wq, wk, wv, jnp.asarray(ppe, BF16), jnp.asarray(grp, BF16), gq, gk, cos, s1, s2


def _a_bias_tiles(table):
    dist_vec = np.arange(-(BLOCK - 1), A_BIAS_TILES * BLOCK)
    vec = table[_t5_bucket(jnp.asarray(dist_vec))].T.astype(F32)
    windows = jnp.stack([vec[:, d * BLOCK:d * BLOCK + 2 * BLOCK - 1] for d in range(A_BIAS_TILES)], axis=1)
    return _toeplitz(windows, BLOCK, BLOCK)


def kernel(x, norm_gain, w_in, qk_gain_a, qk_gain_b, qk_gain_c, qk_gain_d, c_q_gain, c_kv_gain,
           w_q_b, w_kv_b, sinks, rel_bias, w_branch, w_out):
    batch, seq, d_model = x.shape
    depth = w_in.shape[0]
    t = batch * seq
    x2 = x.reshape(t, d_model)

    w_pn, w_pp, w_bg, w_pf, hg, flag, hg_b, flag_b = _layer_weights(w_in, qk_gain_a, qk_gain_b, qk_gain_d)
    wq, wk, wv, ppe, grp, gq, gk, cos, s1, s2 = _c_weights(qk_gain_c, w_q_b, w_kv_b, seq)
    wb = w_branch.astype(BF16)
    wo = w_out.astype(BF16)

    rel_bias2 = rel_bias * LOG2E
    bias_a = _a_bias_tiles(rel_bias2[:, :A_HEADS])
    bias_b = [_band_bias(rel_bias2[:, A_HEADS + g * B_HEADS:A_HEADS + (g + 1) * B_HEADS], dil, window // dil)
              for g, (window, dil) in enumerate(B_GROUPS)]
    bias_d = _band_bias(rel_bias2[:, N_BIAS_HEADS - D_HEADS:], 1, D_WINDOW - 1)

    xn = _norm(x2, norm_gain[0])
    for l in range(depth):
        pn = _proj(xn, w_pn[l], BF16, head_gain=hg[l], flag=flag)
        pp = _proj(xn, w_pp[l], BF16)
        pf = _proj(xn, w_pf[l], F32)

        sel = _a_select(pf, batch, seq)
        ya = _a_attn(pn, pp, sel, bias_a, batch, seq)

        yb, lb = [], []
        for g, (window, dil) in enumerate(B_GROUPS):
            bg = _proj(xn, w_bg[g][l], BF16, head_gain=hg_b[l], flag=flag_b, dil=dil, seq=seq)
            o, lse = _band_attn(bg, bg, bg, bias_b[g], batch * dil, seq // dil, 0, 1, 2, BRANCH_WIDTH,
                                want_lse=True)
            yb.append(o)
            lb.append(lse)

        qc, kc, vc = _c_prep(pf, c_q_gain[l].reshape(1, -1), c_kv_gain[l].reshape(1, -1), wq[l], wk[l], wv[l],
                             ppe, grp, gq[l].reshape(1, -1), gk[l].reshape(1, -1), cos, s1, s2, seq)
        yc = _c_attn(qc, kc, vc, batch, seq)

        (yd,) = _band_attn(pn, pn, pn, bias_d, batch, seq, PN_DQ // BRANCH_WIDTH, PN_DK // DKV_DUP_W,
                           PN_DV // DKV_DUP_W, DKV_DUP_W, sinks=sinks[l] * LOG2E)

        if l + 1 < depth:
            x2, xn = _merge(x2, ya, yb, lb, yc, yd, pp, wb[l], wo[l], batch, seq, next_gain=norm_gain[l + 1])
        else:
            (x2,) = _merge(x2, ya, yb, lb, yc, yd, pp, wb[l], wo[l], batch, seq)
    return x2.reshape(batch, seq, d_model)
```

```python
import functools
import math

import numpy as np
import jax
import jax.numpy as jnp
from jax import lax
from jax.experimental import pallas as pl
from jax.experimental.pallas import tpu as pltpu

F32 = jnp.float32
BF16 = jnp.bfloat16

D_MODEL = 1024
BLOCK = 128
HEAD_DIM = 64
N_BRANCH = 4
BRANCH_WIDTH = 512
EPS = 1e-6
A_HEADS = 8
IDX_HEADS = 8
IDX_DIM = 32
TOPK_MAX = 256
B_GROUPS = ((128, 1), (512, 4), (2048, 16))
B_HEADS = 8
C_HEADS = 8
C_NOPE = 64
C_ROPE = 32
C_V = 64
C_Q_LORA = 256
C_KV_LORA = 128
ROPE_THETA = 10000.0
D_HEADS = 8
D_KV_HEADS = 2
D_WINDOW = 128
NUM_BUCKETS = 32
MAX_DISTANCE = 2048
N_BIAS_HEADS = A_HEADS + len(B_GROUPS) * B_HEADS + D_HEADS

LANES = 128
MXU_WIDTH = 256
NEG = -1e30
LOG2E = math.log2(math.e)
INT_MIN = -(2 ** 31)

DKV_DUP_W = 2 * D_KV_HEADS * HEAD_DIM
PN_AQ = 0
PN_AK = PN_AQ + A_HEADS * HEAD_DIM
PN_DQ = PN_AK + A_HEADS * HEAD_DIM
PN_DK = PN_DQ + D_HEADS * HEAD_DIM
PN_DV = PN_DK + DKV_DUP_W
PN_W = PN_DV + DKV_DUP_W
PP_G = 0
PP_AV = PP_G + N_BRANCH * D_MODEL
PP_AZ = PP_AV + A_HEADS * HEAD_DIM
PP_BZ = PP_AZ + BRANCH_WIDTH
PP_CZ = PP_BZ + BRANCH_WIDTH
PP_DZ = PP_CZ + BRANCH_WIDTH
PF_HALF = 512

A_BIAS_TILES = MAX_DISTANCE // BLOCK + 2


def _cparams(sem, vmem_mb=48):
    return pltpu.CompilerParams(dimension_semantics=sem, vmem_limit_bytes=vmem_mb * 1024 * 1024)


def _t5_bucket(dist):
    max_exact = NUM_BUCKETS // 2
    d = jnp.maximum(dist, 0)
    logd = jnp.log(jnp.maximum(d, 1).astype(F32) / max_exact)
    large = max_exact + (logd / math.log(MAX_DISTANCE / max_exact) * (NUM_BUCKETS - max_exact)).astype(jnp.int32)
    return jnp.where(d < max_exact, d, jnp.minimum(large, NUM_BUCKETS - 1))


def _rms_bf16(x, gain):
    return (x * lax.rsqrt(jnp.mean(x * x, axis=-1, keepdims=True) + EPS) * gain).astype(BF16)


def _norm_kernel(x_ref, g_ref, o_ref):
    o_ref[...] = _rms_bf16(x_ref[...], g_ref[...])


def _norm(x2, gain, tm=1024):
    t, d = x2.shape
    return pl.pallas_call(
        _norm_kernel,
        grid=(t // tm,),
        in_specs=[pl.BlockSpec((tm, d), lambda i: (i, 0)), pl.BlockSpec((1, d), lambda i: (0, 0))],
        out_specs=pl.BlockSpec((tm, d), lambda i: (i, 0)),
        out_shape=jax.ShapeDtypeStruct((t, d), BF16),
        compiler_params=_cparams(("parallel",)),
        name="norm",
    )(x2, gain.reshape(1, d))


def _proj_kernel(xn_ref, w_ref, *rest, norm, dil):
    rest = list(rest)
    res_ref = rest.pop(-1) if dil is not None and dil > 1 else None
    if norm:
        hg_ref, flag_ref, bd_ref, o_ref = rest
    else:
        (o_ref,) = rest

    def emit(h):
        if dil is None:
            o_ref[...] = h.astype(o_ref.dtype)
        elif dil == 1:
            o_ref[0, 0] = h.astype(o_ref.dtype)
        else:
            sub = h.shape[0] // dil
            for c in range(h.shape[1] // LANES):
                res_ref[c] = h[:, c * LANES:(c + 1) * LANES]
            for r in range(dil):
                o_ref[0, r] = jnp.concatenate(
                    [res_ref[c, pl.ds(r, sub, stride=dil), :] for c in range(h.shape[1] // LANES)],
                    axis=1).astype(o_ref.dtype)

    h = jnp.dot(xn_ref[...], w_ref[...], preferred_element_type=F32)
    if norm:
        sq = (h * h).astype(BF16)
        slab = bd_ref.shape[0]
        ss = jnp.concatenate([jnp.dot(sq[:, c:c + slab], bd_ref[...], preferred_element_type=F32)
                              for c in range(0, sq.shape[1], slab)], axis=1)
        scale = lax.rsqrt(ss * (1.0 / HEAD_DIM) + EPS) * hg_ref[...]
        h = h * jnp.where(flag_ref[...] > 0, scale, 1.0)
    emit(h)


def _proj(xn, w, out_dtype, head_gain=None, flag=None, dil=None, seq=None, tm=2048, tn=512):
    t, d = xn.shape
    n = w.shape[1]
    norm = head_gain is not None
    scratch = []
    if dil is None:
        out_spec = pl.BlockSpec((tm, tn), lambda i, j: (i, j))
        out_shape = jax.ShapeDtypeStruct((t, n), out_dtype)
        kdil = None
    else:
        nsb = seq // tm
        out_spec = pl.BlockSpec((1, dil, tm // dil, tn), lambda i, j: (i // nsb, 0, i % nsb, j))
        out_shape = jax.ShapeDtypeStruct((t // seq, dil, seq // dil, n), out_dtype)
        kdil = dil
        if dil > 1:
            scratch.append(pltpu.VMEM((tn // LANES, tm, LANES), F32))
    in_specs = [pl.BlockSpec((tm, d), lambda i, j: (i, 0)),
                pl.BlockSpec((d, tn), lambda i, j: (0, j))]
    args = [xn, w]
    if norm:
        lane = np.arange(MXU_WIDTH)
        bd = jnp.asarray((lane[:, None] // HEAD_DIM == lane[None, :] // HEAD_DIM), BF16)
        in_specs += [pl.BlockSpec((1, tn), lambda i, j: (0, j)),
                     pl.BlockSpec((1, tn), lambda i, j: (0, j)),
                     pl.BlockSpec((MXU_WIDTH, MXU_WIDTH), lambda i, j: (0, 0))]
        args += [head_gain.reshape(1, n), flag.reshape(1, n), bd]
    out = pl.pallas_call(
        functools.partial(_proj_kernel, norm=norm, dil=kdil),
        grid=(t // tm, n // tn),
        in_specs=in_specs,
        out_specs=out_spec,
        out_shape=out_shape,
        scratch_shapes=scratch,
        compiler_params=_cparams(("parallel", "parallel")),
        name=("proj_norm" if norm else "proj") + ("" if dil is None else "_dil%d" % dil),
    )(*args)
    return out.reshape(t, n)


C_PAIR = 256
C_QK_W = (C_HEADS // 2) * C_PAIR
C_ROPE_HALF = C_ROPE // 2


def _c_lane(h, c):
    base = (h // 2) * C_PAIR
    e = h % 2
    if c < C_NOPE:
        return base + e * C_NOPE + c
    return base + 2 * C_NOPE + e * C_ROPE + (c - C_NOPE)


def _c_layout_tables():
    src_q = np.full(C_QK_W, -1, np.int64)
    src_k = np.full(C_QK_W, -1, np.int64)
    gain_idx = np.full(C_QK_W, -1, np.int64)
    head = np.full(C_QK_W, -1, np.int64)
    pe_src = np.full(C_QK_W, -1, np.int64)
    rope_j = np.full(C_QK_W, -1, np.int64)
    rope_half = np.zeros(C_QK_W, np.int64)
    for h in range(C_HEADS):
        for c in range(C_NOPE + C_ROPE):
            ln = _c_lane(h, c)
            src_q[ln] = h * (C_NOPE + C_ROPE) + c
            gain_idx[ln] = c
            head[ln] = h
            if c < C_NOPE:
                src_k[ln] = h * (C_NOPE + C_V) + c
            else:
                r = c - C_NOPE
                pe_src[ln] = r
                rope_j[ln] = r % C_ROPE_HALF
                rope_half[ln] = r // C_ROPE_HALF
    return src_q, src_k, gain_idx, head, pe_src, rope_j, rope_half


def _c_prep_kernel(pf_ref, cqg_ref, ckvg_ref, wq_ref, wk_ref, wv_ref, ppe_ref, grp_ref,
                   gq_ref, gk_ref, cos_ref, s1_ref, s2_ref, q_ref, k_ref, v_ref):
    blk = pf_ref[...]
    cq = blk[:, :C_Q_LORA]
    ckv = blk[:, C_Q_LORA:C_Q_LORA + C_KV_LORA]
    pe = blk[:, C_Q_LORA + C_KV_LORA:]

    def rms(v, g):
        return v * lax.rsqrt(jnp.mean(v * v, axis=-1, keepdims=True) + EPS) * g

    def head_norm_rope(raw, gain):
        sq = (raw * raw).astype(BF16)
        ss = jnp.concatenate([jnp.dot(sq[:, p * C_PAIR:(p + 1) * C_PAIR], grp_ref[...], preferred_element_type=F32)
                              for p in range(C_QK_W // C_PAIR)], axis=1)
        y = raw * lax.rsqrt(ss * (1.0 / (C_NOPE + C_ROPE)) + EPS) * gain
        up = pltpu.roll(y, C_QK_W - C_ROPE_HALF, 1)
        dn = pltpu.roll(y, C_ROPE_HALF, 1)
        wide = lambda ref: jnp.concatenate([ref[...]] * (C_QK_W // C_PAIR), axis=1)
        return y * wide(cos_ref) + up * wide(s1_ref) + dn * wide(s2_ref)

    cqn = rms(cq, cqg_ref[...]).astype(BF16)
    q_raw = jnp.dot(cqn, wq_ref[...], preferred_element_type=F32)
    q_ref[...] = head_norm_rope(q_raw, gq_ref[...]).astype(BF16)

    ckvn = rms(ckv, ckvg_ref[...]).astype(BF16)
    pe_hi = pe.astype(BF16)
    pe_lo = (pe - pe_hi.astype(F32)).astype(BF16)
    k_raw = (jnp.dot(ckvn, wk_ref[...], preferred_element_type=F32)
             + jnp.dot(pe_hi, ppe_ref[...], preferred_element_type=F32)
             + jnp.dot(pe_lo, ppe_ref[...], preferred_element_type=F32))
    k_ref[...] = head_norm_rope(k_raw, gk_ref[...]).astype(BF16)
    v_ref[...] = jnp.dot(ckvn, wv_ref[...], preferred_element_type=F32).astype(BF16)


def _c_prep(pf, cqg, ckvg, wq, wk, wv, ppe, grp, gq, gk, cos, s1, s2, seq, tm=512):
    t = pf.shape[0]
    nsb = seq // tm
    full = lambda shape: pl.BlockSpec(shape, lambda i: (0,) * len(shape))
    tab = pl.BlockSpec((tm, C_PAIR), lambda i: (i % nsb, 0))
    return pl.pallas_call(
        _c_prep_kernel,
        grid=(t // tm,),
        in_specs=[pl.BlockSpec((tm, PF_HALF), lambda i: (i, 1)),
                  full((1, C_Q_LORA)), full((1, C_KV_LORA)),
                  full((C_Q_LORA, C_QK_W)), full((C_KV_LORA, C_QK_W)), full((C_KV_LORA, BRANCH_WIDTH)),
                  full((LANES, C_QK_W)), full((C_PAIR, C_PAIR)),
                  full((1, C_QK_W)), full((1, C_QK_W)), tab, tab, tab],
        out_specs=[pl.BlockSpec((tm, C_QK_W), lambda i: (i, 0)),
                   pl.BlockSpec((tm, C_QK_W), lambda i: (i, 0)),
                   pl.BlockSpec((tm, BRANCH_WIDTH), lambda i: (i, 0))],
        out_shape=[jax.ShapeDtypeStruct((t, C_QK_W), BF16),
                   jax.ShapeDtypeStruct((t, C_QK_W), BF16),
                   jax.ShapeDtypeStruct((t, BRANCH_WIDTH), BF16)],
        compiler_params=_cparams(("parallel",)),
        name="c_prep",
    )(pf, cqg, ckvg, wq, wk, wv, ppe, grp, gq, gk, cos, s1, s2)


def _flash_update(e, s, v_ones, m_ref, acc_ref, rows=slice(None)):
    m_prev = m_ref[e, rows]
    m_new = jnp.maximum(m_prev, jnp.max(s, axis=1, keepdims=True))
    alpha = jnp.exp2(m_prev - m_new)
    p = jnp.exp2(s - jnp.concatenate([m_new] * (s.shape[1] // LANES), axis=1))
    acc_ref[e, rows] = (jnp.concatenate([alpha, alpha], axis=1) * acc_ref[e, rows]
                        + jnp.dot(p.astype(BF16), v_ones, preferred_element_type=F32))
    m_ref[e, rows] = m_new


def _flash_init(m_ref, acc_ref):
    m_ref[...] = jnp.full(m_ref.shape, NEG, F32)
    acc_ref[...] = jnp.zeros(acc_ref.shape, F32)


def _flash_finish(o_ref, acc_ref):
    lane = lax.broadcasted_iota(jnp.int32, o_ref.shape, 1)
    o0 = acc_ref[0, :, :LANES] / acc_ref[0, :, LANES:]
    o1 = acc_ref[1, :, :LANES] / acc_ref[1, :, LANES:]
    o_ref[...] = jnp.where(lane < HEAD_DIM, o0, o1).astype(o_ref.dtype)


def _with_ones(v_chunk):
    return jnp.concatenate([v_chunk, jnp.ones(v_chunk.shape, v_chunk.dtype)], axis=1)


_CONTRACT_LANES = (((1,), (1,)), ((), ()))


def _head_masks(width, ranges0, ranges1):
    m = np.zeros((2, 1, width), np.float32)
    for e, ranges in enumerate((ranges0, ranges1)):
        for lo, hi in ranges:
            m[e, 0, lo:hi] = 1
    return jnp.asarray(m, BF16)


def _pair_masks():
    return _head_masks(LANES, [(0, HEAD_DIM)], [(HEAD_DIM, LANES)])


def _transpose_keys(k_ref, kt_ref, rows=512):
    def body(c, carry):
        off = pl.multiple_of(c * rows, rows)
        kt_ref[:, pl.ds(off, rows)] = k_ref[pl.ds(off, rows), :].astype(F32).T.astype(kt_ref.dtype)
        return carry
    lax.fori_loop(0, k_ref.shape[0] // rows, body, 0)


def _chunk_loop(chunk, n, unroll, carry=None, slots=None):
    start = 0
    width = unroll
    while width >= 1:
        def body(g, carry, width=width, start=start):
            for j in range(width):
                carry = chunk(start + g * width + j, carry) if slots is None else \
                    chunk(start + g * width + j, carry, j)
            return carry
        groups = (n - start) // width
        carry = lax.fori_loop(0, groups, body, carry)
        start = start + groups * width
        width //= 2
    return carry


def _c_attn_kernel(q_ref, hm_ref, k_ref, v_ref, o_ref, kt_ref, m_ref, acc_ref, *, tq, tk, unroll):
    qi = pl.program_id(2)

    @pl.when(qi == 0)
    def _():
        _transpose_keys(k_ref, kt_ref)

    q = q_ref[...]
    qs = (q * hm_ref[0], q * hm_ref[1])
    _flash_init(m_ref, acc_ref)
    n_full = (qi * tq) // tk

    def chunk(c, row0=0, diagonal=False):
        off = pl.multiple_of(c * tk, tk)
        kc = kt_ref[:, pl.ds(off, tk)]
        vc = _with_ones(v_ref[pl.ds(off, tk), :])
        for e in range(2):
            if not diagonal:
                rows = slice(row0, tq)
                _flash_update(e, jnp.dot(qs[e][rows], kc, preferred_element_type=F32), vc, m_ref, acc_ref, rows)
                continue
            rows = slice(row0, row0 + tk)
            s = jnp.dot(qs[e][rows], kc, preferred_element_type=F32)
            s = jnp.where(lax.broadcasted_iota(jnp.int32, s.shape, 1) <= lax.broadcasted_iota(jnp.int32, s.shape, 0),
                          s, NEG)
            _flash_update(e, s, vc, m_ref, acc_ref, rows)
            if row0 + tk < tq:
                rows = slice(row0 + tk, tq)
                _flash_update(e, jnp.dot(qs[e][rows], kc, preferred_element_type=F32), vc, m_ref, acc_ref, rows)

    _chunk_loop(lambda c, _: chunk(c), n_full, unroll)
    for j in range(tq // tk):
        chunk(n_full + j, row0=j * tk, diagonal=True)
    _flash_finish(o_ref, acc_ref)


def _c_attn(qc, kc, vc, batch, seq, tq=1024, tk=512, unroll=4):
    t = qc.shape[0]
    nq = seq // tq
    npair = C_HEADS // 2
    return pl.pallas_call(
        functools.partial(_c_attn_kernel, tq=tq, tk=tk, unroll=unroll),
        grid=(batch, npair, nq),
        in_specs=[pl.BlockSpec((tq, C_PAIR), lambda b, p, i: (b * nq + i, p)),
                  pl.BlockSpec((2, 1, C_PAIR), lambda b, p, i: (0, 0, 0)),
                  pl.BlockSpec((seq, C_PAIR), lambda b, p, i: (b, p)),
                  pl.BlockSpec((seq, LANES), lambda b, p, i: (b, p))],
        out_specs=pl.BlockSpec((tq, LANES), lambda b, p, i: (b * nq + i, p)),
        out_shape=jax.ShapeDtypeStruct((t, BRANCH_WIDTH), BF16),
        scratch_shapes=[pltpu.VMEM((C_PAIR, seq), BF16),
                        pltpu.VMEM((2, tq, LANES), F32), pltpu.VMEM((2, tq, 2 * LANES), F32)],
        compiler_params=_cparams(("parallel", "parallel", "arbitrary")),
        name="c_attn",
    )(qc, _head_masks(C_PAIR, [(0, C_NOPE), (2 * C_NOPE, 2 * C_NOPE + C_ROPE)],
                      [(C_NOPE, 2 * C_NOPE), (2 * C_NOPE + C_ROPE, 2 * C_NOPE + 2 * C_ROPE)]), kc, vc)


def _a_attn_kernel(q_ref, hm_ref, k_ref, v_ref, sel_ref, bias_ref, o_ref, kt_ref, m_ref, acc_ref, msk_ref,
                   *, tq, tk, unroll):
    qi = pl.program_id(2)

    @pl.when(qi == 0)
    def _():
        _transpose_keys(k_ref, kt_ref)

    q = q_ref[...]
    qs = (q * hm_ref[0], q * hm_ref[1])
    _flash_init(m_ref, acc_ref)
    n_chunks = ((qi + 1) * tq + tk - 1) // tk

    def chunk(c, carry, slot):
        off = pl.multiple_of(c * tk, tk)
        kc = kt_ref[:, pl.ds(off, tk)]
        vc = _with_ones(v_ref[pl.ds(off, tk), :])
        msk_ref[slot] = sel_ref[:, pl.ds(off, tk)].astype(F32)
        for e in range(2):
            s = jnp.dot(qs[e], kc, preferred_element_type=F32)
            rows = []
            for i in range(tq // BLOCK):
                tiles = []
                for j in range(tk // BLOCK):
                    d = (qi * (tq // BLOCK) + i) - (c * (tk // BLOCK) + j)
                    d = jnp.clip(d, 0, A_BIAS_TILES - 1)
                    tiles.append(bias_ref[e, d])
                rows.append(jnp.concatenate(tiles, axis=1))
            s = s + jnp.concatenate(rows, axis=0) + msk_ref[slot]
            _flash_update(e, s, vc, m_ref, acc_ref)
        return carry

    _chunk_loop(chunk, n_chunks, unroll, slots=unroll)
    _flash_finish(o_ref, acc_ref)


def _a_attn(pn, pp, sel, bias_tiles, batch, seq, tq=512, tk=512, unroll=4):
    t = pn.shape[0]
    nq = seq // tq
    npair = A_HEADS // 2
    qcol, kcol, vcol = PN_AQ // LANES, PN_AK // LANES, PP_AV // LANES
    return pl.pallas_call(
        functools.partial(_a_attn_kernel, tq=tq, tk=tk, unroll=unroll),
        grid=(batch, npair, nq),
        in_specs=[pl.BlockSpec((tq, LANES), lambda b, p, i: (b * nq + i, qcol + p)),
                  pl.BlockSpec((2, 1, LANES), lambda b, p, i: (0, 0, 0)),
                  pl.BlockSpec((seq, LANES), lambda b, p, i: (b, kcol + p)),
                  pl.BlockSpec((seq, LANES), lambda b, p, i: (b, vcol + p)),
                  pl.BlockSpec((tq, seq), lambda b, p, i: (b * nq + i, 0)),
                  pl.BlockSpec((2, A_BIAS_TILES, BLOCK, BLOCK), lambda b, p, i: (p, 0, 0, 0))],
        out_specs=pl.BlockSpec((tq, LANES), lambda b, p, i: (b * nq + i, p)),
        out_shape=jax.ShapeDtypeStruct((t, BRANCH_WIDTH), BF16),
        scratch_shapes=[pltpu.VMEM((LANES, seq), BF16),
                        pltpu.VMEM((2, tq, LANES), F32), pltpu.VMEM((2, tq, 2 * LANES), F32),
                        pltpu.VMEM((unroll, tq, tk), F32)],
        compiler_params=_cparams(("parallel", "parallel", "arbitrary")),
        name="a_attn",
    )(pn, _pair_masks(), pn, pp, sel, bias_tiles)


def _sortable_key(score):
    bits = pltpu.bitcast(score, jnp.int32)
    return bits ^ ((bits >> 31) & jnp.int32(0x7FFFFFFF))


def _bit_transpose32(words):
    a = list(words)
    j, m = 16, 0x0000FFFF
    while j:
        for k in range(32):
            if k & j:
                continue
            t = (a[k] ^ lax.shift_right_logical(a[k + j], jnp.int32(j))) & m
            a[k] = a[k] ^ t
            a[k + j] = a[k + j] ^ lax.shift_left(t, jnp.int32(j))
        j >>= 1
        m ^= (m << j) & 0xFFFFFFFF
    return a


SEL_ROWS = 512
PLANE_ROWS = 256
SCORE_UNROLL = 2


def _a_select_kernel(qblk_ref, kblk_ref, ph_ref, pl_ref, pkh_ref, pkl_ref, sel_ref,
                     ikx_ref, keys_ref, planes_ref, eq_ref, iqt_ref, jcut_ref, *, seq, k_sel):
    qi = pl.program_id(1)
    n_sel = (qi * BLOCK + BLOCK + SEL_ROWS - 1) // SEL_ROWS

    @pl.when(qi == 0)
    def _():
        def prep(c, carry):
            off = pl.multiple_of(c * SEL_ROWS, SEL_ROWS)
            kv = kblk_ref[pl.ds(off, SEL_ROWS), :]
            hi = kv.astype(BF16)
            lo = (kv - hi.astype(F32)).astype(BF16)
            ikx_ref[pl.ds(off, SEL_ROWS), :] = (
                jnp.dot(hi, pkh_ref[...], preferred_element_type=F32)
                + jnp.dot(lo, pkl_ref[...], preferred_element_type=F32)).astype(BF16)
            return carry
        lax.fori_loop(0, seq // SEL_ROWS, prep, 0)

        def clear(g, carry):
            planes_ref[g] = jnp.zeros(planes_ref.shape[1:], jnp.int32)
            return carry
        lax.fori_loop(0, planes_ref.shape[0], clear, 0)

    qb = qblk_ref[...]
    iq = qb[:, :IDX_HEADS * IDX_DIM]
    iq_hi = iq.astype(BF16)
    iq_lo = (iq - iq_hi.astype(F32)).astype(BF16)
    iqx = (jnp.dot(iq_hi, ph_ref[...], preferred_element_type=F32)
           + jnp.dot(iq_lo, pl_ref[...], preferred_element_type=F32))
    for h in range(IDX_HEADS):
        iqt_ref[h // 2, :, (h % 2) * LANES:(h % 2 + 1) * LANES] = iqx[:, h * LANES:(h + 1) * LANES].T.astype(BF16)
    iw_t = qb[:, IDX_HEADS * IDX_DIM:IDX_HEADS * IDX_DIM + LANES].T
    iw_rows = [iw_t[IDX_DIM + h:IDX_DIM + h + 1, :] for h in range(IDX_HEADS)]
    q_pos = qi * BLOCK + lax.broadcasted_iota(jnp.int32, (SEL_ROWS, BLOCK), 1)
    k_row = lax.broadcasted_iota(jnp.int32, (SEL_ROWS, BLOCK), 0)

    def score_chunk(c, masked):
        off = pl.multiple_of(c * SEL_ROWS, SEL_ROWS)
        kx = ikx_ref[pl.ds(off, SEL_ROWS), :]
        sc = jnp.zeros((SEL_ROWS, BLOCK), F32)
        for hp in range(IDX_HEADS // 2):
            xx = jnp.dot(kx, iqt_ref[hp], preferred_element_type=F32)
            sc = (sc + jnp.maximum(xx[:, :LANES], 0.0) * iw_rows[2 * hp]
                  + jnp.maximum(xx[:, LANES:], 0.0) * iw_rows[2 * hp + 1])
        key = _sortable_key(sc + 0.0)
        if masked:
            key = jnp.where(off + k_row <= q_pos, key, INT_MIN)
        keys_ref[pl.ds(off, SEL_ROWS), :] = key
        u = key ^ INT_MIN
        for g in range(SEL_ROWS // PLANE_ROWS):
            words = [u[g * PLANE_ROWS + 8 * j:g * PLANE_ROWS + 8 * j + 8] for j in range(32)]
            for i, plane in enumerate(_bit_transpose32(words)):
                planes_ref[c * (SEL_ROWS // PLANE_ROWS) + g, i] = plane

    _chunk_loop(lambda c, _: score_chunk(c, False), n_sel - 1, SCORE_UNROLL)
    score_chunk(n_sel - 1, True)

    def per_query_total(counts):
        while len(counts) > 1:
            counts = [a + b for a, b in zip(counts[::2], counts[1::2])] + counts[len(counts) & ~1:]
        return jnp.sum(counts[0], axis=0, keepdims=True)

    n_grp = n_sel * (SEL_ROWS // PLANE_ROWS)
    all_groups = eq_ref.shape[0]
    for g in range(all_groups):
        eq_ref[g] = jnp.where(g < n_grp, jnp.full((8, BLOCK), -1, jnp.int32), jnp.zeros((8, BLOCK), jnp.int32))

    def sweep(cur_plane, prev_plane, prev_keep):
        counts = []
        for g in range(all_groups):
            eq = eq_ref[g]
            if prev_plane is not None:
                eq = eq & (planes_ref[g, prev_plane] ^ ~prev_keep)
                eq_ref[g] = eq
            ones = eq if cur_plane is None else eq & planes_ref[g, cur_plane]
            counts.append(lax.population_count(ones))
        return per_query_total(counts)

    def decide(bit, cnt_ones, thr_u, above):
        total = above + cnt_ones
        keep = jnp.where(total >= k_sel, -1, 0)
        return thr_u | (keep & bit), jnp.where(keep < 0, above, total), keep

    zeros = jnp.zeros((1, BLOCK), jnp.int32)
    state = decide(INT_MIN, sweep(0, None, None), zeros, zeros)

    def radix_step(it, state):
        thr_u, above, keep = state
        bit = lax.shift_left(jnp.int32(1), jnp.int32(31) - it)
        return decide(bit, sweep(it, it - 1, keep), thr_u, above)

    thr_u, above, keep = lax.fori_loop(1, 32, radix_step, state)
    cnt_thr = above + sweep(None, 31, keep)
    thr = thr_u ^ INT_MIN

    has_k = thr > INT_MIN
    thr = jnp.maximum(thr, INT_MIN + 1)

    jcut_ref[...] = jnp.full(jcut_ref.shape, 2 ** 31 - 1, jnp.int32)

    @pl.when(jnp.max(jnp.where(has_k, cnt_thr, 0)) > k_sel)
    def _():
        excess = jnp.where(has_k, cnt_thr - k_sel, 0)
        need = jnp.where(excess > 0, k_sel - above, 1)
        sub = lax.broadcasted_iota(jnp.int32, (8, BLOCK), 0)

        def ties_below(t):
            counts = []
            for g in range(all_groups):
                words = jnp.clip((t - g * PLANE_ROWS - sub + 7) >> 3, 0, 32)
                top = jnp.where(words > 0, lax.shift_left(jnp.int32(-1), 32 - words), 0)
                counts.append(lax.population_count(eq_ref[g] & top))
            return per_query_total(counts)

        def tie_bisect(it, t):
            cand = t + lax.shift_left(jnp.int32(1), jnp.int32(seq.bit_length() - 2) - it)
            return jnp.where(ties_below(cand) < need, cand, t)

        last = lax.fori_loop(0, seq.bit_length() - 1, tie_bisect, jnp.zeros((1, BLOCK), jnp.int32))
        jcut = jnp.where(excess > 0, last, 2 ** 31 - 1)
        jcut_ref[...] = jnp.broadcast_to(jcut, jcut_ref.shape)

    jcut = jcut_ref[0:1, :]

    def emit(c, carry):
        off = pl.multiple_of(c * SEL_ROWS, SEL_ROWS)
        keys = keys_ref[pl.ds(off, SEL_ROWS), :]
        bar = jnp.where(off + k_row > jcut, thr + 1, thr)
        add = jnp.where(keys >= bar, 0.0, NEG)
        sel_ref[:, pl.ds(off, SEL_ROWS)] = jnp.concatenate(
            [add[j * BLOCK:(j + 1) * BLOCK].T for j in range(SEL_ROWS // BLOCK)], axis=1).astype(BF16)
        return carry

    _chunk_loop(emit, n_sel, SCORE_UNROLL)

    def fill(c, carry):
        off = pl.multiple_of(c * SEL_ROWS, SEL_ROWS)
        sel_ref[:, pl.ds(off, SEL_ROWS)] = jnp.full((BLOCK, SEL_ROWS), NEG, BF16)
        return carry

    lax.fori_loop(n_sel, seq // SEL_ROWS, fill, 0)


def _a_select(pf, batch, seq):
    t = pf.shape[0]
    nq = seq // BLOCK
    k_sel = min(TOPK_MAX, seq // 4)
    n_groups = seq // PLANE_ROWS
    nlane = IDX_HEADS * IDX_DIM
    r = np.arange(nlane)
    ph = np.zeros((nlane, IDX_HEADS * LANES), np.float32)
    plo = np.zeros((nlane, IDX_HEADS * LANES), np.float32)
    ph[r, (r // IDX_DIM) * LANES + r % IDX_DIM] = 1
    ph[r, (r // IDX_DIM) * LANES + 2 * IDX_DIM + r % IDX_DIM] = 1
    plo[r, (r // IDX_DIM) * LANES + IDX_DIM + r % IDX_DIM] = 1
    d = np.arange(IDX_DIM)
    pkh = np.zeros((LANES, LANES), np.float32)
    pkl = np.zeros((LANES, LANES), np.float32)
    pkh[d, d] = 1
    pkh[d, IDX_DIM + d] = 1
    pkl[d, 2 * IDX_DIM + d] = 1
    full = lambda shape: pl.BlockSpec(shape, lambda b, i: (0,) * len(shape))
    return pl.pallas_call(
        functools.partial(_a_select_kernel, seq=seq, k_sel=k_sel),
        grid=(batch, nq),
        in_specs=[pl.BlockSpec((BLOCK, PF_HALF), lambda b, i: (b * nq + i, 0)),
                  pl.BlockSpec((seq, LANES), lambda b, i: (b, nlane // LANES)),
                  full(ph.shape), full(plo.shape), full(pkh.shape), full(pkl.shape)],
        out_specs=pl.BlockSpec((BLOCK, seq), lambda b, i: (b * nq + i, 0)),
        out_shape=jax.ShapeDtypeStruct((t, seq), BF16),
        scratch_shapes=[pltpu.VMEM((seq, LANES), BF16), pltpu.VMEM((seq, LANES), jnp.int32),
                        pltpu.VMEM((n_groups, 32, 8, LANES), jnp.int32),
                        pltpu.VMEM((n_groups, 8, LANES), jnp.int32),
                        pltpu.VMEM((IDX_HEADS // 2, LANES, 2 * LANES), BF16),
                        pltpu.VMEM((8, LANES), jnp.int32)],
        compiler_params=_cparams(("parallel", "arbitrary")),
        name="a_select",
    )(pf, pf, jnp.asarray(ph, BF16), jnp.asarray(plo, BF16), jnp.asarray(pkh, BF16),
      jnp.asarray(pkl, BF16))


BAND_RBLK = 4


def _band_kernel(*refs, has_sink, want_lse, kv_div, rblk):
    if has_sink:
        sink_ref, refs = refs[0], refs[1:]
    q_ref, hm_ref, kp_ref, kc_ref, vp_ref, vc_ref, bias_ref, o_ref = refs[:8]
    first = pl.program_id(1) == 0
    lane = lax.broadcasted_iota(jnp.int32, (BLOCK, LANES), 1)
    col = lax.broadcasted_iota(jnp.int32, (BLOCK, 2 * BLOCK), 1)
    no_prev = jnp.logical_and(first, col < BLOCK)
    for p in range(BRANCH_WIDTH // LANES):
        kcol = (p // kv_div) * LANES
        kk = jnp.concatenate([kp_ref[:, kcol:kcol + LANES], kc_ref[:, kcol:kcol + LANES]], axis=0)
        vv = _with_ones(jnp.concatenate([vp_ref[:, kcol:kcol + LANES], vc_ref[:, kcol:kcol + LANES]], axis=0))
        for i in range(rblk):
            rows = slice(i * BLOCK, (i + 1) * BLOCK)
            q = q_ref[rows, p * LANES:(p + 1) * LANES]
            kblk = kk[i * BLOCK:(i + 2) * BLOCK]
            vblk = vv[i * BLOCK:(i + 2) * BLOCK]
            outs, lses = [], []
            for e in range(2):
                s = lax.dot_general(q * hm_ref[e], kblk, _CONTRACT_LANES, preferred_element_type=F32)
                s = s + bias_ref[2 * p + e]
                if i == 0:
                    s = jnp.where(no_prev, NEG, s)
                m = jnp.max(s, axis=1, keepdims=True)
                if has_sink:
                    sk = sink_ref[2 * p + e]
                    m = jnp.maximum(m, sk)
                pr = jnp.exp2(s - m)
                acc = jnp.dot(pr.astype(BF16), vblk, preferred_element_type=F32)
                den = acc[:, LANES:]
                if has_sink:
                    den = den + jnp.exp2(sk - m)
                outs.append(acc[:, :LANES] / den)
                lses.append(m + jnp.log2(den))
            o_ref[rows, p * LANES:(p + 1) * LANES] = jnp.where(lane < HEAD_DIM, outs[0], outs[1]).astype(o_ref.dtype)
            if want_lse:
                refs[8][rows, p * LANES:(p + 1) * LANES] = jnp.where(lane < HEAD_DIM, lses[0], lses[1])


def _band_attn(q_arr, k_arr, v_arr, bias, nsub, sub_len, qcb, kcb, vcb, kv_w, sinks=None, want_lse=False):
    t = q_arr.shape[0]
    rblk = min(BAND_RBLK, sub_len // BLOCK)
    step_rows = rblk * BLOCK
    nbs = sub_len // step_rows
    kv_div = BRANCH_WIDTH // kv_w

    def cur(u, n):
        return u * nbs + n

    def prev(u, n):
        return jnp.where(n == 0, u * nbs * rblk, (u * nbs + n) * rblk - 1)

    in_specs = [
        pl.BlockSpec((step_rows, BRANCH_WIDTH), lambda u, n: (cur(u, n), qcb)),
        pl.BlockSpec((2, 1, LANES), lambda u, n: (0, 0, 0)),
        pl.BlockSpec((BLOCK, kv_w), lambda u, n: (prev(u, n), kcb)),
        pl.BlockSpec((step_rows, kv_w), lambda u, n: (cur(u, n), kcb)),
        pl.BlockSpec((BLOCK, kv_w), lambda u, n: (prev(u, n), vcb)),
        pl.BlockSpec((step_rows, kv_w), lambda u, n: (cur(u, n), vcb)),
        pl.BlockSpec(bias.shape, lambda u, n: (0, 0, 0)),
    ]
    args = [q_arr, _pair_masks(), k_arr, k_arr, v_arr, v_arr, bias]
    if sinks is not None:
        in_specs = [pl.BlockSpec(memory_space=pltpu.SMEM)] + in_specs
        args = [sinks] + args
    o_spec = pl.BlockSpec((step_rows, BRANCH_WIDTH), lambda u, n: (cur(u, n), 0))
    out_specs = [o_spec]
    out_shape = [jax.ShapeDtypeStruct((t, BRANCH_WIDTH), BF16)]
    if want_lse:
        out_specs.append(o_spec)
        out_shape.append(jax.ShapeDtypeStruct((t, BRANCH_WIDTH), F32))
    return pl.pallas_call(
        functools.partial(_band_kernel, has_sink=sinks is not None, want_lse=want_lse, kv_div=kv_div, rblk=rblk),
        grid=(nsub, nbs),
        in_specs=in_specs,
        out_specs=out_specs,
        out_shape=out_shape,
        compiler_params=_cparams(("parallel", "arbitrary")),
        name="band_attn",
    )(*args)


def _toeplitz(w, rows, cols):
    length = rows + cols - 1
    u = jnp.pad(w[..., ::-1], [(0, 0)] * (w.ndim - 1) + [(0, 1)])
    flat = jnp.broadcast_to(u[..., None, :], w.shape[:-1] + (rows, length + 1)).reshape(w.shape[:-1] + (-1,))
    skew = flat[..., :rows * length].reshape(w.shape[:-1] + (rows, length))
    return skew[..., rows - 1:rows - 1 + cols]


def _band_bias(table, step, max_dist):
    rel = BLOCK + np.arange(BLOCK)[:, None] - np.arange(2 * BLOCK)[None, :]
    rel_vec = np.arange(BLOCK - (2 * BLOCK - 1), 2 * BLOCK)
    bias = _toeplitz(table[_t5_bucket(jnp.asarray(rel_vec * step))].T.astype(F32), BLOCK, 2 * BLOCK)
    ok = (rel >= 0) & (rel <= max_dist)
    return jnp.where(jnp.asarray(ok)[None], bias, NEG)


def _sigmoid(x):
    return 0.5 * jnp.tanh(0.5 * x) + 0.5


def _merge_kernel(x_ref, ya_ref, yb0_ref, yb1_ref, yb2_ref, l0_ref, l1_ref, l2_ref, yc_ref, yd_ref,
                  za_ref, zb_ref, zc_ref, zd_ref, ga_ref, gb_ref, gc_ref, gd_ref, wb_ref, wo_ref, *rest):
    if len(rest) == 7:
        ng_ref, o_ref, xn_ref, y1_ref, y2_ref, s1_ref, s2_ref = rest
    else:
        ng_ref = xn_ref = None
        o_ref, y1_ref, y2_ref, s1_ref, s2_ref = rest
    ncol = BRANCH_WIDTH // LANES
    for src, dst in ((yb1_ref, y1_ref), (yb2_ref, y2_ref), (l1_ref, s1_ref), (l2_ref, s2_ref)):
        dil = src.shape[1]
        for r in range(dil):
            blk = src[0, r].astype(F32)
            for c in range(ncol):
                dst[c, pl.ds(r, src.shape[2], stride=dil), :] = blk[:, c * LANES:(c + 1) * LANES]
    wide = lambda ref: jnp.concatenate([ref[c] for c in range(ncol)], axis=1)
    l0, l1, l2 = l0_ref[0, 0], wide(s1_ref), wide(s2_ref)
    mx = jnp.maximum(jnp.maximum(l0, l1), l2)
    w0, w1, w2 = jnp.exp2(l0 - mx), jnp.exp2(l1 - mx), jnp.exp2(l2 - mx)
    yb = (w0 * yb0_ref[0, 0].astype(F32) + w1 * wide(y1_ref) + w2 * wide(y2_ref)) / (w0 + w1 + w2)
    ys = (ya_ref[...].astype(F32), yb, yc_ref[...].astype(F32), yd_ref[...].astype(F32))
    zs = (za_ref, zb_ref, zc_ref, zd_ref)
    gs = (ga_ref, gb_ref, gc_ref, gd_ref)
    merged = jnp.zeros(o_ref.shape, F32)
    for n in range(N_BRANCH):
        z = zs[n][...].astype(F32)
        u = (ys[n] * (z * _sigmoid(z))).astype(BF16)
        proj = jnp.dot(u, wb_ref[n], preferred_element_type=F32)
        merged = merged + _sigmoid(gs[n][...].astype(F32)) * proj
    x_new = x_ref[...] + jnp.dot(merged.astype(BF16), wo_ref[...], preferred_element_type=F32)
    o_ref[...] = x_new
    if xn_ref is not None:
        xn_ref[...] = _rms_bf16(x_new, ng_ref[...])


def _merge(x2, ya, yb, lb, yc, yd, pp, wb, wo, batch, seq, next_gain=None, tm=512):
    t = x2.shape[0]
    bw = BRANCH_WIDTH
    nsb = seq // tm
    row = lambda w, c: pl.BlockSpec((tm, w), lambda i: (i, c))

    def sub(dil):
        return pl.BlockSpec((1, dil, tm // dil, bw), lambda i: (i // nsb, 0, i % nsb, 0))

    dils = [dil for _, dil in B_GROUPS]
    yb = [a.reshape(batch, dil, seq // dil, bw) for a, dil in zip(yb, dils)]
    lb = [a.reshape(batch, dil, seq // dil, bw) for a, dil in zip(lb, dils)]
    in_specs = ([row(D_MODEL, 0), row(bw, 0)] + [sub(dil) for dil in dils] * 2 + [row(bw, 0), row(bw, 0)]
                + [row(bw, PP_AZ // bw), row(bw, PP_BZ // bw), row(bw, PP_CZ // bw), row(bw, PP_DZ // bw)]
                + [row(D_MODEL, PP_G // D_MODEL + n) for n in range(N_BRANCH)]
                + [pl.BlockSpec((N_BRANCH, bw, D_MODEL), lambda i: (0, 0, 0)),
                   pl.BlockSpec((D_MODEL, D_MODEL), lambda i: (0, 0))])
    args = [x2, ya, yb[0], yb[1], yb[2], lb[0], lb[1], lb[2], yc, yd, pp, pp, pp, pp, pp, pp, pp, pp, wb, wo]
    out_specs = [row(D_MODEL, 0)]
    out_shape = [jax.ShapeDtypeStruct((t, D_MODEL), F32)]
    if next_gain is not None:
        in_specs.append(pl.BlockSpec((1, D_MODEL), lambda i: (0, 0)))
        args.append(next_gain.reshape(1, D_MODEL))
        out_specs.append(row(D_MODEL, 0))
        out_shape.append(jax.ShapeDtypeStruct((t, D_MODEL), BF16))
    return pl.pallas_call(
        _merge_kernel,
        grid=(t // tm,),
        in_specs=in_specs,
        out_specs=out_specs,
        out_shape=out_shape,
        scratch_shapes=[pltpu.VMEM((bw // LANES, tm, LANES), F32)] * 4,
        compiler_params=_cparams(("parallel",)),
        name="merge",
    )(*args)


def _w_in_columns():
    n_bq = len(B_GROUPS) * B_HEADS * HEAD_DIM
    groups = (
        (("a_q", A_HEADS * HEAD_DIM), ("a_k", A_HEADS * HEAD_DIM), ("a_v", A_HEADS * HEAD_DIM),
         ("a_z", BRANCH_WIDTH), ("a_iq", IDX_HEADS * IDX_DIM), ("a_ik", IDX_DIM), ("a_iw", IDX_HEADS)),
        (("b_q", n_bq), ("b_k", n_bq), ("b_v", n_bq), ("b_z", BRANCH_WIDTH)),
        (("c_q", C_Q_LORA), ("c_kv", C_KV_LORA), ("c_pe", C_ROPE), ("c_z", BRANCH_WIDTH)),
        (("d_q", D_HEADS * HEAD_DIM), ("d_k", D_KV_HEADS * HEAD_DIM), ("d_v", D_KV_HEADS * HEAD_DIM),
         ("d_z", BRANCH_WIDTH)),
        (("gates", N_BRANCH * D_MODEL),),
    )
    table, start = {}, 0
    for group in groups:
        for name, width in group:
            table[name] = (start, width)
            start += width
    return table


def _layer_weights(w_in, qk_gain_a, qk_gain_b, qk_gain_d):
    table = _w_in_columns()
    depth = w_in.shape[0]

    def cols(name, part=None):
        start, width = table[name]
        if part is not None:
            width //= part[1]
            start += part[0] * width
        return w_in[:, :, start:start + width]

    def dup_heads(a):
        return jnp.concatenate([a[..., :HEAD_DIM], a[..., :HEAD_DIM], a[..., HEAD_DIM:], a[..., HEAD_DIM:]], axis=-1)

    def zeros(width):
        return jnp.zeros((depth, D_MODEL, width), w_in.dtype)

    n_grp = len(B_GROUPS)
    w_pn = jnp.concatenate([cols("a_q"), cols("a_k"), cols("d_q"), dup_heads(cols("d_k")), dup_heads(cols("d_v"))],
                           axis=-1).astype(BF16)
    w_pp = jnp.concatenate([cols("gates"), cols("a_v"), cols("a_z"), cols("b_z"), cols("c_z"), cols("d_z")],
                           axis=-1).astype(BF16)
    w_bg = [jnp.concatenate([cols("b_q", (g, n_grp)), cols("b_k", (g, n_grp)), cols("b_v", (g, n_grp))],
                            axis=-1).astype(BF16) for g in range(n_grp)]
    idx = jnp.concatenate([cols("a_iq"), cols("a_ik"), cols("a_iw")], axis=-1)
    lora = jnp.concatenate([cols("c_q"), cols("c_kv"), cols("c_pe")], axis=-1)
    w_pf = jnp.concatenate([idx, zeros(PF_HALF - idx.shape[-1]), lora, zeros(PF_HALF - lora.shape[-1])],
                           axis=-1).astype(BF16)

    scale = HEAD_DIM ** -0.5 * LOG2E
    tile = lambda g, reps: jnp.tile(g, (1, reps))
    hg = jnp.concatenate([tile(qk_gain_a[:, 0] * scale, A_HEADS), tile(qk_gain_a[:, 1], A_HEADS),
                          tile(qk_gain_d[:, 0] * scale, D_HEADS), tile(qk_gain_d[:, 1], 2 * D_KV_HEADS),
                          jnp.ones((depth, PN_W - PN_DV), F32)], axis=-1)
    flag = jnp.concatenate([jnp.ones((PN_DV,), F32), jnp.zeros((PN_W - PN_DV,), F32)])
    hg_b = jnp.concatenate([tile(qk_gain_b[:, 0] * scale, B_HEADS), tile(qk_gain_b[:, 1], B_HEADS),
                            jnp.ones((depth, BRANCH_WIDTH), F32)], axis=-1)
    flag_b = jnp.concatenate([jnp.ones((2 * BRANCH_WIDTH,), F32), jnp.zeros((BRANCH_WIDTH,), F32)])
    return w_pn, w_pp, w_bg, w_pf, hg, flag, hg_b, flag_b


def _c_weights(qk_gain_c, w_q_b, w_kv_b, seq):
    src_q, src_k, gain_idx, head, pe_src, rope_j, rope_half = _c_layout_tables()
    take = lambda w, src: jnp.where(jnp.asarray(src >= 0), jnp.take(w, jnp.asarray(np.maximum(src, 0)), axis=-1), 0.0)
    wq = take(w_q_b, src_q).astype(BF16)
    wk = take(w_kv_b, src_k).astype(BF16)
    v_src = np.array([h * (C_NOPE + C_V) + C_NOPE + c for h in range(C_HEADS) for c in range(C_V)])
    wv = jnp.take(w_kv_b, jnp.asarray(v_src), axis=-1).astype(BF16)
    ppe = np.zeros((LANES, C_QK_W), np.float32)
    ln = np.nonzero(pe_src >= 0)[0]
    ppe[pe_src[ln], ln] = 1
    hp = head[:C_PAIR]
    grp = ((hp[:, None] == hp[None, :]) & (hp[:, None] >= 0)).astype(np.float32)
    g_take = lambda g: jnp.where(jnp.asarray(gain_idx >= 0), jnp.take(g, jnp.asarray(np.maximum(gain_idx, 0)), axis=-1), 0.0)
    gq = g_take(qk_gain_c[:, 0]) * ((C_NOPE + C_ROPE) ** -0.5 * LOG2E)
    gk = g_take(qk_gain_c[:, 1])
    freq = ROPE_THETA ** (-jnp.arange(C_ROPE_HALF, dtype=F32) / C_ROPE_HALF)
    ang = jnp.arange(seq).astype(F32)[:, None] * freq[None, :]
    cos_j, sin_j = jnp.cos(ang), jnp.sin(ang)
    place = np.zeros((C_ROPE_HALF, C_PAIR), np.float32)
    lanes = np.nonzero(rope_j[:C_PAIR] >= 0)[0]
    place[rope_j[lanes], lanes] = 1
    is_rope = jnp.asarray(rope_j[:C_PAIR] >= 0)
    hi = lax.Precision.HIGHEST
    cos = jnp.where(is_rope, jnp.dot(cos_j, jnp.asarray(place), precision=hi), 1.0)
    sin = jnp.dot(sin_j, jnp.asarray(place), precision=hi)
    s1 = jnp.where(jnp.asarray(rope_half[:C_PAIR] == 0), -sin, 0.0)
    s2 = jnp.where(jnp.asarray(rope_half[:C_PAIR] == 1), sin, 0.0)
    return wq, wk, wv, jnp.asarray(ppe, BF16), jnp.asarray(grp, BF16), gq, gk, cos, s1, s2


def _a_bias_tiles(table):
    dist_vec = np.arange(-(BLOCK - 1), A_BIAS_TILES * BLOCK)
    vec = table[_t5_bucket(jnp.asarray(dist_vec))].T.astype(F32)
    windows = jnp.stack([vec[:, d * BLOCK:d * BLOCK + 2 * BLOCK - 1] for d in range(A_BIAS_TILES)], axis=1)
    return _toeplitz(windows, BLOCK, BLOCK)


def kernel(x, norm_gain, w_in, qk_gain_a, qk_gain_b, qk_gain_c, qk_gain_d, c_q_gain, c_kv_gain,
           w_q_b, w_kv_b, sinks, rel_bias, w_branch, w_out):
    batch, seq, d_model = x.shape
    depth = w_in.shape[0]
    t = batch * seq
    x2 = x.reshape(t, d_model)

    w_pn, w_pp, w_bg, w_pf, hg, flag, hg_b, flag_b = _layer_weights(w_in, qk_gain_a, qk_gain_b, qk_gain_d)
    wq, wk, wv, ppe, grp, gq, gk, cos, s1, s2 = _c_weights(qk_gain_c, w_q_b, w_kv_b, seq)
    wb = w_branch.astype(BF16)
    wo = w_out.astype(BF16)

    rel_bias2 = rel_bias * LOG2E
    bias_a = _a_bias_tiles(rel_bias2[:, :A_HEADS])
    bias_b = [_band_bias(rel_bias2[:, A_HEADS + g * B_HEADS:A_HEADS + (g + 1) * B_HEADS], dil, window // dil)
              for g, (window, dil) in enumerate(B_GROUPS)]
    bias_d = _band_bias(rel_bias2[:, N_BIAS_HEADS - D_HEADS:], 1, D_WINDOW - 1)

    xn = _norm(x2, norm_gain[0])
    for l in range(depth):
        pn = _proj(xn, w_pn[l], BF16, head_gain=hg[l], flag=flag)
        pp = _proj(xn, w_pp[l], BF16)
        pf = _proj(xn, w_pf[l], F32)

        sel = _a_select(pf, batch, seq)
        ya = _a_attn(pn, pp, sel, bias_a, batch, seq)

        yb, lb = [], []
        for g, (window, dil) in enumerate(B_GROUPS):
            bg = _proj(xn, w_bg[g][l], BF16, head_gain=hg_b[l], flag=flag_b, dil=dil, seq=seq)
            o, lse = _band_attn(bg, bg, bg, bias_b[g], batch * dil, seq // dil, 0, 1, 2, BRANCH_WIDTH,
                                want_lse=True)
            yb.append(o)
            lb.append(lse)

        qc, kc, vc = _c_prep(pf, c_q_gain[l].reshape(1, -1), c_kv_gain[l].reshape(1, -1), wq[l], wk[l], wv[l],
                             ppe, grp, gq[l].reshape(1, -1), gk[l].reshape(1, -1), cos, s1, s2, seq)
        yc = _c_attn(qc, kc, vc, batch, seq)

        (yd,) = _band_attn(pn, pn, pn, bias_d, batch, seq, PN_DQ // BRANCH_WIDTH, PN_DK // DKV_DUP_W,
                           PN_DV // DKV_DUP_W, DKV_DUP_W, sinks=sinks[l] * LOG2E)

        if l + 1 < depth:
            x2, xn = _merge(x2, ya, yb, lb, yc, yd, pp, wb[l], wo[l], batch, seq, next_gain=norm_gain[l + 1])
        else:
            (x2,) = _merge(x2, ya, yb, lb, yc, yd, pp, wb[l], wo[l], batch, seq)
    return x2.reshape(batch, seq, d_model)
```

```python
import functools
import math

import numpy as np
import jax
import jax.numpy as jnp
from jax import lax
from jax.experimental import pallas as pl
from jax.experimental.pallas import tpu as pltpu

F32 = jnp.float32
BF16 = jnp.bfloat16

D_MODEL = 1024
BLOCK = 128
HEAD_DIM = 64
N_BRANCH = 4
BRANCH_WIDTH = 512
EPS = 1e-6
A_HEADS = 8
IDX_HEADS = 8
IDX_DIM = 32
TOPK_MAX = 256
B_GROUPS = ((128, 1), (512, 4), (2048, 16))
B_HEADS = 8
C_HEADS = 8
C_NOPE = 64
C_ROPE = 32
C_V = 64
C_Q_LORA = 256
C_KV_LORA = 128
ROPE_THETA = 10000.0
D_HEADS = 8
D_KV_HEADS = 2
D_WINDOW = 128
NUM_BUCKETS = 32
MAX_DISTANCE = 2048
N_BIAS_HEADS = A_HEADS + len(B_GROUPS) * B_HEADS + D_HEADS

LANES = 128
MXU_WIDTH = 256
NEG = -1e30
LOG2E = math.log2(math.e)
INT_MIN = -(2 ** 31)

DKV_DUP_W = 2 * D_KV_HEADS * HEAD_DIM
PN_AQ = 0
PN_AK = PN_AQ + A_HEADS * HEAD_DIM
PN_DQ = PN_AK + A_HEADS * HEAD_DIM
PN_DK = PN_DQ + D_HEADS * HEAD_DIM
PN_DV = PN_DK + DKV_DUP_W
PN_W = PN_DV + DKV_DUP_W
PP_G = 0
PP_AV = PP_G + N_BRANCH * D_MODEL
PP_AZ = PP_AV + A_HEADS * HEAD_DIM
PP_BZ = PP_AZ + BRANCH_WIDTH
PP_CZ = PP_BZ + BRANCH_WIDTH
PP_DZ = PP_CZ + BRANCH_WIDTH
PF_HALF = 512

A_BIAS_TILES = MAX_DISTANCE // BLOCK + 2


def _cparams(sem, vmem_mb=48):
    return pltpu.CompilerParams(dimension_semantics=sem, vmem_limit_bytes=vmem_mb * 1024 * 1024)


def _t5_bucket(dist):
    max_exact = NUM_BUCKETS // 2
    d = jnp.maximum(dist, 0)
    logd = jnp.log(jnp.maximum(d, 1).astype(F32) / max_exact)
    large = max_exact + (logd / math.log(MAX_DISTANCE / max_exact) * (NUM_BUCKETS - max_exact)).astype(jnp.int32)
    return jnp.where(d < max_exact, d, jnp.minimum(large, NUM_BUCKETS - 1))


def _rms_bf16(x, gain):
    return (x * lax.rsqrt(jnp.mean(x * x, axis=-1, keepdims=True) + EPS) * gain).astype(BF16)


def _norm_kernel(x_ref, g_ref, o_ref):
    o_ref[...] = _rms_bf16(x_ref[...], g_ref[...])


def _norm(x2, gain, tm=1024):
    t, d = x2.shape
    return pl.pallas_call(
        _norm_kernel,
        grid=(t // tm,),
        in_specs=[pl.BlockSpec((tm, d), lambda i: (i, 0)), pl.BlockSpec((1, d), lambda i: (0, 0))],
        out_specs=pl.BlockSpec((tm, d), lambda i: (i, 0)),
        out_shape=jax.ShapeDtypeStruct((t, d), BF16),
        compiler_params=_cparams(("parallel",)),
        name="norm",
    )(x2, gain.reshape(1, d))


def _proj_kernel(xn_ref, w_ref, *rest, norm, dil):
    rest = list(rest)
    res_ref = rest.pop(-1) if dil is not None and dil > 1 else None
    if norm:
        hg_ref, flag_ref, bd_ref, o_ref = rest
    else:
        (o_ref,) = rest

    def emit(h):
        if dil is None:
            o_ref[...] = h.astype(o_ref.dtype)
        elif dil == 1:
            o_ref[0, 0] = h.astype(o_ref.dtype)
        else:
            sub = h.shape[0] // dil
            for c in range(h.shape[1] // LANES):
                res_ref[c] = h[:, c * LANES:(c + 1) * LANES]
            for r in range(dil):
                o_ref[0, r] = jnp.concatenate(
                    [res_ref[c, pl.ds(r, sub, stride=dil), :] for c in range(h.shape[1] // LANES)],
                    axis=1).astype(o_ref.dtype)

    h = jnp.dot(xn_ref[...], w_ref[...], preferred_element_type=F32)
    if norm:
        sq = (h * h).astype(BF16)
        slab = bd_ref.shape[0]
        ss = jnp.concatenate([jnp.dot(sq[:, c:c + slab], bd_ref[...], preferred_element_type=F32)
                              for c in range(0, sq.shape[1], slab)], axis=1)
        scale = lax.rsqrt(ss * (1.0 / HEAD_DIM) + EPS) * hg_ref[...]
        h = h * jnp.where(flag_ref[...] > 0, scale, 1.0)
    emit(h)


def _proj(xn, w, out_dtype, head_gain=None, flag=None, dil=None, seq=None, tm=2048, tn=512):
    t, d = xn.shape
    n = w.shape[1]
    norm = head_gain is not None
    scratch = []
    if dil is None:
        out_spec = pl.BlockSpec((tm, tn), lambda i, j: (i, j))
        out_shape = jax.ShapeDtypeStruct((t, n), out_dtype)
        kdil = None
    else:
        nsb = seq // tm
        out_spec = pl.BlockSpec((1, dil, tm // dil, tn), lambda i, j: (i // nsb, 0, i % nsb, j))
        out_shape = jax.ShapeDtypeStruct((t // seq, dil, seq // dil, n), out_dtype)
        kdil = dil
        if dil > 1:
            scratch.append(pltpu.VMEM((tn // LANES, tm, LANES), F32))
    in_specs = [pl.BlockSpec((tm, d), lambda i, j: (i, 0)),
                pl.BlockSpec((d, tn), lambda i, j: (0, j))]
    args = [xn, w]
    if norm:
        lane = np.arange(MXU_WIDTH)
        bd = jnp.asarray((lane[:, None] // HEAD_DIM == lane[None, :] // HEAD_DIM), BF16)
        in_specs += [pl.BlockSpec((1, tn), lambda i, j: (0, j)),
                     pl.BlockSpec((1, tn), lambda i, j: (0, j)),
                     pl.BlockSpec((MXU_WIDTH, MXU_WIDTH), lambda i, j: (0, 0))]
        args += [head_gain.reshape(1, n), flag.reshape(1, n), bd]
    out = pl.pallas_call(
        functools.partial(_proj_kernel, norm=norm, dil=kdil),
        grid=(t // tm, n // tn),
        in_specs=in_specs,
        out_specs=out_spec,
        out_shape=out_shape,
        scratch_shapes=scratch,
        compiler_params=_cparams(("parallel", "parallel")),
        name=("proj_norm" if norm else "proj") + ("" if dil is None else "_dil%d" % dil),
    )(*args)
    return out.reshape(t, n)


C_PAIR = 256
C_QK_W = (C_HEADS // 2) * C_PAIR
C_ROPE_HALF = C_ROPE // 2


def _c_lane(h, c):
    base = (h // 2) * C_PAIR
    e = h % 2
    if c < C_NOPE:
        return base + e * C_NOPE + c
    return base + 2 * C_NOPE + e * C_ROPE + (c - C_NOPE)


def _c_layout_tables():
    src_q = np.full(C_QK_W, -1, np.int64)
    src_k = np.full(C_QK_W, -1, np.int64)
    gain_idx = np.full(C_QK_W, -1, np.int64)
    head = np.full(C_QK_W, -1, np.int64)
    pe_src = np.full(C_QK_W, -1, np.int64)
    rope_j = np.full(C_QK_W, -1, np.int64)
    rope_half = np.zeros(C_QK_W, np.int64)
    for h in range(C_HEADS):
        for c in range(C_NOPE + C_ROPE):
            ln = _c_lane(h, c)
            src_q[ln] = h * (C_NOPE + C_ROPE) + c
            gain_idx[ln] = c
            head[ln] = h
            if c < C_NOPE:
                src_k[ln] = h * (C_NOPE + C_V) + c
            else:
                r = c - C_NOPE
                pe_src[ln] = r
                rope_j[ln] = r % C_ROPE_HALF
                rope_half[ln] = r // C_ROPE_HALF
    return src_q, src_k, gain_idx, head, pe_src, rope_j, rope_half


def _c_prep_kernel(pf_ref, cqg_ref, ckvg_ref, wq_ref, wk_ref, wv_ref, ppe_ref, grp_ref,
                   gq_ref, gk_ref, cos_ref, s1_ref, s2_ref, q_ref, k_ref, v_ref):
    blk = pf_ref[...]
    cq = blk[:, :C_Q_LORA]
    ckv = blk[:, C_Q_LORA:C_Q_LORA + C_KV_LORA]
    pe = blk[:, C_Q_LORA + C_KV_LORA:]

    def rms(v, g):
        return v * lax.rsqrt(jnp.mean(v * v, axis=-1, keepdims=True) + EPS) * g

    def head_norm_rope(raw, gain):
        sq = (raw * raw).astype(BF16)
        ss = jnp.concatenate([jnp.dot(sq[:, p * C_PAIR:(p + 1) * C_PAIR], grp_ref[...], preferred_element_type=F32)
                              for p in range(C_QK_W // C_PAIR)], axis=1)
        y = raw * lax.rsqrt(ss * (1.0 / (C_NOPE + C_ROPE)) + EPS) * gain
        up = pltpu.roll(y, C_QK_W - C_ROPE_HALF, 1)
        dn = pltpu.roll(y, C_ROPE_HALF, 1)
        wide = lambda ref: jnp.concatenate([ref[...]] * (C_QK_W // C_PAIR), axis=1)
        return y * wide(cos_ref) + up * wide(s1_ref) + dn * wide(s2_ref)

    cqn = rms(cq, cqg_ref[...]).astype(BF16)
    q_raw = jnp.dot(cqn, wq_ref[...], preferred_element_type=F32)
    q_ref[...] = head_norm_rope(q_raw, gq_ref[...]).astype(BF16)

    ckvn = rms(ckv, ckvg_ref[...]).astype(BF16)
    pe_hi = pe.astype(BF16)
    pe_lo = (pe - pe_hi.astype(F32)).astype(BF16)
    k_raw = (jnp.dot(ckvn, wk_ref[...], preferred_element_type=F32)
             + jnp.dot(pe_hi, ppe_ref[...], preferred_element_type=F32)
             + jnp.dot(pe_lo, ppe_ref[...], preferred_element_type=F32))
    k_ref[...] = head_norm_rope(k_raw, gk_ref[...]).astype(BF16)
    v_ref[...] = jnp.dot(ckvn, wv_ref[...], preferred_element_type=F32).astype(BF16)


def _c_prep(pf, cqg, ckvg, wq, wk, wv, ppe, grp, gq, gk, cos, s1, s2, seq, tm=512):
    t = pf.shape[0]
    nsb = seq // tm
    full = lambda shape: pl.BlockSpec(shape, lambda i: (0,) * len(shape))
    tab = pl.BlockSpec((tm, C_PAIR), lambda i: (i % nsb, 0))
    return pl.pallas_call(
        _c_prep_kernel,
        grid=(t // tm,),
        in_specs=[pl.BlockSpec((tm, PF_HALF), lambda i: (i, 1)),
                  full((1, C_Q_LORA)), full((1, C_KV_LORA)),
                  full((C_Q_LORA, C_QK_W)), full((C_KV_LORA, C_QK_W)), full((C_KV_LORA, BRANCH_WIDTH)),
                  full((LANES, C_QK_W)), full((C_PAIR, C_PAIR)),
                  full((1, C_QK_W)), full((1, C_QK_W)), tab, tab, tab],
        out_specs=[pl.BlockSpec((tm, C_QK_W), lambda i: (i, 0)),
                   pl.BlockSpec((tm, C_QK_W), lambda i: (i, 0)),
                   pl.BlockSpec((tm, BRANCH_WIDTH), lambda i: (i, 0))],
        out_shape=[jax.ShapeDtypeStruct((t, C_QK_W), BF16),
                   jax.ShapeDtypeStruct((t, C_QK_W), BF16),
                   jax.ShapeDtypeStruct((t, BRANCH_WIDTH), BF16)],
        compiler_params=_cparams(("parallel",)),
        name="c_prep",
    )(pf, cqg, ckvg, wq, wk, wv, ppe, grp, gq, gk, cos, s1, s2)


def _flash_update(e, s, v_ones, m_ref, acc_ref, rows=slice(None)):
    m_prev = m_ref[e, rows]
    m_new = jnp.maximum(m_prev, jnp.max(s, axis=1, keepdims=True))
    alpha = jnp.exp2(m_prev - m_new)
    p = jnp.exp2(s - jnp.concatenate([m_new] * (s.shape[1] // LANES), axis=1))
    acc_ref[e, rows] = (jnp.concatenate([alpha, alpha], axis=1) * acc_ref[e, rows]
                        + jnp.dot(p.astype(BF16), v_ones, preferred_element_type=F32))
    m_ref[e, rows] = m_new


def _flash_init(m_ref, acc_ref):
    m_ref[...] = jnp.full(m_ref.shape, NEG, F32)
    acc_ref[...] = jnp.zeros(acc_ref.shape, F32)


def _flash_finish(o_ref, acc_ref):
    lane = lax.broadcasted_iota(jnp.int32, o_ref.shape, 1)
    o0 = acc_ref[0, :, :LANES] / acc_ref[0, :, LANES:]
    o1 = acc_ref[1, :, :LANES] / acc_ref[1, :, LANES:]
    o_ref[...] = jnp.where(lane < HEAD_DIM, o0, o1).astype(o_ref.dtype)


def _with_ones(v_chunk):
    return jnp.concatenate([v_chunk, jnp.ones(v_chunk.shape, v_chunk.dtype)], axis=1)


_CONTRACT_LANES = (((1,), (1,)), ((), ()))


def _head_masks(width, ranges0, ranges1):
    m = np.zeros((2, 1, width), np.float32)
    for e, ranges in enumerate((ranges0, ranges1)):
        for lo, hi in ranges:
            m[e, 0, lo:hi] = 1
    return jnp.asarray(m, BF16)


def _pair_masks():
    return _head_masks(LANES, [(0, HEAD_DIM)], [(HEAD_DIM, LANES)])


def _transpose_keys(k_ref, kt_ref, rows=512):
    def body(c, carry):
        off = pl.multiple_of(c * rows, rows)
        kt_ref[:, pl.ds(off, rows)] = k_ref[pl.ds(off, rows), :].astype(F32).T.astype(kt_ref.dtype)
        return carry
    lax.fori_loop(0, k_ref.shape[0] // rows, body, 0)


def _chunk_loop(chunk, n, unroll, carry=None, slots=None):
    start = 0
    width = unroll
    while width >= 1:
        def body(g, carry, width=width, start=start):
            for j in range(width):
                carry = chunk(start + g * width + j, carry) if slots is None else \
                    chunk(start + g * width + j, carry, j)
            return carry
        groups = (n - start) // width
        carry = lax.fori_loop(0, groups, body, carry)
        start = start + groups * width
        width //= 2
    return carry


def _c_attn_kernel(q_ref, hm_ref, k_ref, v_ref, o_ref, kt_ref, m_ref, acc_ref, *, tq, tk, unroll):
    qi = pl.program_id(2)

    @pl.when(qi == 0)
    def _():
        _transpose_keys(k_ref, kt_ref)

    q = q_ref[...]
    qs = (q * hm_ref[0], q * hm_ref[1])
    _flash_init(m_ref, acc_ref)
    n_full = (qi * tq) // tk

    def chunk(c, row0=0, diagonal=False):
        off = pl.multiple_of(c * tk, tk)
        kc = kt_ref[:, pl.ds(off, tk)]
        vc = _with_ones(v_ref[pl.ds(off, tk), :])
        for e in range(2):
            if not diagonal:
                rows = slice(row0, tq)
                _flash_update(e, jnp.dot(qs[e][rows], kc, preferred_element_type=F32), vc, m_ref, acc_ref, rows)
                continue
            rows = slice(row0, row0 + tk)
            s = jnp.dot(qs[e][rows], kc, preferred_element_type=F32)
            s = jnp.where(lax.broadcasted_iota(jnp.int32, s.shape, 1) <= lax.broadcasted_iota(jnp.int32, s.shape, 0),
                          s, NEG)
            _flash_update(e, s, vc, m_ref, acc_ref, rows)
            if row0 + tk < tq:
                rows = slice(row0 + tk, tq)
                _flash_update(e, jnp.dot(qs[e][rows], kc, preferred_element_type=F32), vc, m_ref, acc_ref, rows)

    _chunk_loop(lambda c, _: chunk(c), n_full, unroll)
    for j in range(tq // tk):
        chunk(n_full + j, row0=j * tk, diagonal=True)
    _flash_finish(o_ref, acc_ref)


def _c_attn(qc, kc, vc, batch, seq, tq=1024, tk=512, unroll=4):
    t = qc.shape[0]
    nq = seq // tq
    npair = C_HEADS // 2
    return pl.pallas_call(
        functools.partial(_c_attn_kernel, tq=tq, tk=tk, unroll=unroll),
        grid=(batch, npair, nq),
        in_specs=[pl.BlockSpec((tq, C_PAIR), lambda b, p, i: (b * nq + i, p)),
                  pl.BlockSpec((2, 1, C_PAIR), lambda b, p, i: (0, 0, 0)),
                  pl.BlockSpec((seq, C_PAIR), lambda b, p, i: (b, p)),
                  pl.BlockSpec((seq, LANES), lambda b, p, i: (b, p))],
        out_specs=pl.BlockSpec((tq, LANES), lambda b, p, i: (b * nq + i, p)),
        out_shape=jax.ShapeDtypeStruct((t, BRANCH_WIDTH), BF16),
        scratch_shapes=[pltpu.VMEM((C_PAIR, seq), BF16),
                        pltpu.VMEM((2, tq, LANES), F32), pltpu.VMEM((2, tq, 2 * LANES), F32)],
        compiler_params=_cparams(("parallel", "parallel", "arbitrary")),
        name="c_attn",
    )(qc, _head_masks(C_PAIR, [(0, C_NOPE), (2 * C_NOPE, 2 * C_NOPE + C_ROPE)],
                      [(C_NOPE, 2 * C_NOPE), (2 * C_NOPE + C_ROPE, 2 * C_NOPE + 2 * C_ROPE)]), kc, vc)


def _a_attn_kernel(q_ref, hm_ref, k_ref, v_ref, sel_ref, bias_ref, o_ref, kt_ref, m_ref, acc_ref, msk_ref,
                   *, tq, tk, unroll):
    qi = pl.program_id(2)

    @pl.when(qi == 0)
    def _():
        _transpose_keys(k_ref, kt_ref)

    q = q_ref[...]
    qs = (q * hm_ref[0], q * hm_ref[1])
    _flash_init(m_ref, acc_ref)
    n_chunks = ((qi + 1) * tq + tk - 1) // tk

    def chunk(c, carry, slot):
        off = pl.multiple_of(c * tk, tk)
        kc = kt_ref[:, pl.ds(off, tk)]
        vc = _with_ones(v_ref[pl.ds(off, tk), :])
        msk_ref[slot] = sel_ref[:, pl.ds(off, tk)].astype(F32)
        for e in range(2):
            s = jnp.dot(qs[e], kc, preferred_element_type=F32)
            rows = []
            for i in range(tq // BLOCK):
                tiles = []
                for j in range(tk // BLOCK):
                    d = (qi * (tq // BLOCK) + i) - (c * (tk // BLOCK) + j)
                    d = jnp.clip(d, 0, A_BIAS_TILES - 1)
                    tiles.append(bias_ref[e, d])
                rows.append(jnp.concatenate(tiles, axis=1))
            s = s + jnp.concatenate(rows, axis=0) + msk_ref[slot]
            _flash_update(e, s, vc, m_ref, acc_ref)
        return carry

    _chunk_loop(chunk, n_chunks, unroll, slots=unroll)
    _flash_finish(o_ref, acc_ref)


def _a_attn(pn, pp, sel, bias_tiles, batch, seq, tq=512, tk=512, unroll=8):
    t = pn.shape[0]
    nq = seq // tq
    npair = A_HEADS // 2
    qcol, kcol, vcol = PN_AQ // LANES, PN_AK // LANES, PP_AV // LANES
    return pl.pallas_call(
        functools.partial(_a_attn_kernel, tq=tq, tk=tk, unroll=unroll),
        grid=(batch, npair, nq),
        in_specs=[pl.BlockSpec((tq, LANES), lambda b, p, i: (b * nq + i, qcol + p)),
                  pl.BlockSpec((2, 1, LANES), lambda b, p, i: (0, 0, 0)),
                  pl.BlockSpec((seq, LANES), lambda b, p, i: (b, kcol + p)),
                  pl.BlockSpec((seq, LANES), lambda b, p, i: (b, vcol + p)),
                  pl.BlockSpec((tq, seq), lambda b, p, i: (b * nq + i, 0)),
                  pl.BlockSpec((2, A_BIAS_TILES, BLOCK, BLOCK), lambda b, p, i: (p, 0, 0, 0))],
        out_specs=pl.BlockSpec((tq, LANES), lambda b, p, i: (b * nq + i, p)),
        out_shape=jax.ShapeDtypeStruct((t, BRANCH_WIDTH), BF16),
        scratch_shapes=[pltpu.VMEM((LANES, seq), BF16),
                        pltpu.VMEM((2, tq, LANES), F32), pltpu.VMEM((2, tq, 2 * LANES), F32),
                        pltpu.VMEM((unroll, tq, tk), F32)],
        compiler_params=_cparams(("parallel", "parallel", "arbitrary")),
        name="a_attn",
    )(pn, _pair_masks(), pn, pp, sel, bias_tiles)


def _sortable_key(score):
    bits = pltpu.bitcast(score, jnp.int32)
    return bits ^ ((bits >> 31) & jnp.int32(0x7FFFFFFF))


def _bit_transpose32(words):
    a = list(words)
    j, m = 16, 0x0000FFFF
    while j:
        for k in range(32):
            if k & j:
                continue
            t = (a[k] ^ lax.shift_right_logical(a[k + j], jnp.int32(j))) & m
            a[k] = a[k] ^ t
            a[k + j] = a[k + j] ^ lax.shift_left(t, jnp.int32(j))
        j >>= 1
        m ^= (m << j) & 0xFFFFFFFF
    return a


SEL_ROWS = 512
PLANE_ROWS = 256
SCORE_UNROLL = 2


def _a_select_kernel(qblk_ref, kblk_ref, ph_ref, pl_ref, pkh_ref, pkl_ref, sel_ref,
                     ikx_ref, keys_ref, planes_ref, eq_ref, iqt_ref, jcut_ref, *, seq, k_sel):
    qi = pl.program_id(1)
    n_sel = (qi * BLOCK + BLOCK + SEL_ROWS - 1) // SEL_ROWS

    @pl.when(qi == 0)
    def _():
        def prep(c, carry):
            off = pl.multiple_of(c * SEL_ROWS, SEL_ROWS)
            kv = kblk_ref[pl.ds(off, SEL_ROWS), :]
            hi = kv.astype(BF16)
            lo = (kv - hi.astype(F32)).astype(BF16)
            ikx_ref[pl.ds(off, SEL_ROWS), :] = (
                jnp.dot(hi, pkh_ref[...], preferred_element_type=F32)
                + jnp.dot(lo, pkl_ref[...], preferred_element_type=F32)).astype(BF16)
            return carry
        lax.fori_loop(0, seq // SEL_ROWS, prep, 0)

        def clear(g, carry):
            planes_ref[g] = jnp.zeros(planes_ref.shape[1:], jnp.int32)
            return carry
        lax.fori_loop(0, planes_ref.shape[0], clear, 0)

    qb = qblk_ref[...]
    iq = qb[:, :IDX_HEADS * IDX_DIM]
    iq_hi = iq.astype(BF16)
    iq_lo = (iq - iq_hi.astype(F32)).astype(BF16)
    iqx = (jnp.dot(iq_hi, ph_ref[...], preferred_element_type=F32)
           + jnp.dot(iq_lo, pl_ref[...], preferred_element_type=F32))
    for h in range(IDX_HEADS):
        iqt_ref[h // 2, :, (h % 2) * LANES:(h % 2 + 1) * LANES] = iqx[:, h * LANES:(h + 1) * LANES].T.astype(BF16)
    iw_t = qb[:, IDX_HEADS * IDX_DIM:IDX_HEADS * IDX_DIM + LANES].T
    iw_rows = [iw_t[IDX_DIM + h:IDX_DIM + h + 1, :] for h in range(IDX_HEADS)]
    q_pos = qi * BLOCK + lax.broadcasted_iota(jnp.int32, (SEL_ROWS, BLOCK), 1)
    k_row = lax.broadcasted_iota(jnp.int32, (SEL_ROWS, BLOCK), 0)

    def score_chunk(c, masked):
        off = pl.multiple_of(c * SEL_ROWS, SEL_ROWS)
        kx = ikx_ref[pl.ds(off, SEL_ROWS), :]
        sc = jnp.zeros((SEL_ROWS, BLOCK), F32)
        for hp in range(IDX_HEADS // 2):
            xx = jnp.dot(kx, iqt_ref[hp], preferred_element_type=F32)
            sc = (sc + jnp.maximum(xx[:, :LANES], 0.0) * iw_rows[2 * hp]
                  + jnp.maximum(xx[:, LANES:], 0.0) * iw_rows[2 * hp + 1])
        key = _sortable_key(sc + 0.0)
        if masked:
            key = jnp.where(off + k_row <= q_pos, key, INT_MIN)
        keys_ref[pl.ds(off, SEL_ROWS), :] = key
        u = key ^ INT_MIN
        for g in range(SEL_ROWS // PLANE_ROWS):
            words = [u[g * PLANE_ROWS + 8 * j:g * PLANE_ROWS + 8 * j + 8] for j in range(32)]
            for i, plane in enumerate(_bit_transpose32(words)):
                planes_ref[c * (SEL_ROWS // PLANE_ROWS) + g, i] = plane

    _chunk_loop(lambda c, _: score_chunk(c, False), n_sel - 1, SCORE_UNROLL)
    score_chunk(n_sel - 1, True)

    def per_query_total(counts):
        while len(counts) > 1:
            counts = [a + b for a, b in zip(counts[::2], counts[1::2])] + counts[len(counts) & ~1:]
        return jnp.sum(counts[0], axis=0, keepdims=True)

    n_grp = n_sel * (SEL_ROWS // PLANE_ROWS)
    all_groups = eq_ref.shape[0]
    for g in range(all_groups):
        eq_ref[g] = jnp.where(g < n_grp, jnp.full((8, BLOCK), -1, jnp.int32), jnp.zeros((8, BLOCK), jnp.int32))

    def sweep(cur_plane, prev_plane, prev_keep):
        counts = []
        for g in range(all_groups):
            eq = eq_ref[g]
            if prev_plane is not None:
                eq = eq & (planes_ref[g, prev_plane] ^ ~prev_keep)
                eq_ref[g] = eq
            ones = eq if cur_plane is None else eq & planes_ref[g, cur_plane]
            counts.append(lax.population_count(ones))
        return per_query_total(counts)

    def decide(bit, cnt_ones, thr_u, above):
        total = above + cnt_ones
        keep = jnp.where(total >= k_sel, -1, 0)
        return thr_u | (keep & bit), jnp.where(keep < 0, above, total), keep

    zeros = jnp.zeros((1, BLOCK), jnp.int32)
    state = decide(INT_MIN, sweep(0, None, None), zeros, zeros)

    def radix_step(it, state):
        thr_u, above, keep = state
        bit = lax.shift_left(jnp.int32(1), jnp.int32(31) - it)
        return decide(bit, sweep(it, it - 1, keep), thr_u, above)

    thr_u, above, keep = lax.fori_loop(1, 32, radix_step, state)
    cnt_thr = above + sweep(None, 31, keep)
    thr = thr_u ^ INT_MIN

    has_k = thr > INT_MIN
    thr = jnp.maximum(thr, INT_MIN + 1)

    jcut_ref[...] = jnp.full(jcut_ref.shape, 2 ** 31 - 1, jnp.int32)

    @pl.when(jnp.max(jnp.where(has_k, cnt_thr, 0)) > k_sel)
    def _():
        excess = jnp.where(has_k, cnt_thr - k_sel, 0)
        need = jnp.where(excess > 0, k_sel - above, 1)
        sub = lax.broadcasted_iota(jnp.int32, (8, BLOCK), 0)

        def ties_below(t):
            counts = []
            for g in range(all_groups):
                words = jnp.clip((t - g * PLANE_ROWS - sub + 7) >> 3, 0, 32)
                top = jnp.where(words > 0, lax.shift_left(jnp.int32(-1), 32 - words), 0)
                counts.append(lax.population_count(eq_ref[g] & top))
            return per_query_total(counts)

        def tie_bisect(it, t):
            cand = t + lax.shift_left(jnp.int32(1), jnp.int32(seq.bit_length() - 2) - it)
            return jnp.where(ties_below(cand) < need, cand, t)

        last = lax.fori_loop(0, seq.bit_length() - 1, tie_bisect, jnp.zeros((1, BLOCK), jnp.int32))
        jcut = jnp.where(excess > 0, last, 2 ** 31 - 1)
        jcut_ref[...] = jnp.broadcast_to(jcut, jcut_ref.shape)

    jcut = jcut_ref[0:1, :]

    def emit(c, carry):
        off = pl.multiple_of(c * SEL_ROWS, SEL_ROWS)
        keys = keys_ref[pl.ds(off, SEL_ROWS), :]
        bar = jnp.where(off + k_row > jcut, thr + 1, thr)
        add = jnp.where(keys >= bar, 0.0, NEG)
        sel_ref[:, pl.ds(off, SEL_ROWS)] = jnp.concatenate(
            [add[j * BLOCK:(j + 1) * BLOCK].T for j in range(SEL_ROWS // BLOCK)], axis=1).astype(BF16)
        return carry

    _chunk_loop(emit, n_sel, SCORE_UNROLL)

    def fill(c, carry):
        off = pl.multiple_of(c * SEL_ROWS, SEL_ROWS)
        sel_ref[:, pl.ds(off, SEL_ROWS)] = jnp.full((BLOCK, SEL_ROWS), NEG, BF16)
        return carry

    lax.fori_loop(n_sel, seq // SEL_ROWS, fill, 0)


def _a_select(pf, batch, seq):
    t = pf.shape[0]
    nq = seq // BLOCK
    k_sel = min(TOPK_MAX, seq // 4)
    n_groups = seq // PLANE_ROWS
    nlane = IDX_HEADS * IDX_DIM
    r = np.arange(nlane)
    ph = np.zeros((nlane, IDX_HEADS * LANES), np.float32)
    plo = np.zeros((nlane, IDX_HEADS * LANES), np.float32)
    ph[r, (r // IDX_DIM) * LANES + r % IDX_DIM] = 1
    ph[r, (r // IDX_DIM) * LANES + 2 * IDX_DIM + r % IDX_DIM] = 1
    plo[r, (r // IDX_DIM) * LANES + IDX_DIM + r % IDX_DIM] = 1
    d = np.arange(IDX_DIM)
    pkh = np.zeros((LANES, LANES), np.float32)
    pkl = np.zeros((LANES, LANES), np.float32)
    pkh[d, d] = 1
    pkh[d, IDX_DIM + d] = 1
    pkl[d, 2 * IDX_DIM + d] = 1
    full = lambda shape: pl.BlockSpec(shape, lambda b, i: (0,) * len(shape))
    return pl.pallas_call(
        functools.partial(_a_select_kernel, seq=seq, k_sel=k_sel),
        grid=(batch, nq),
        in_specs=[pl.BlockSpec((BLOCK, PF_HALF), lambda b, i: (b * nq + i, 0)),
                  pl.BlockSpec((seq, LANES), lambda b, i: (b, nlane // LANES)),
                  full(ph.shape), full(plo.shape), full(pkh.shape), full(pkl.shape)],
        out_specs=pl.BlockSpec((BLOCK, seq), lambda b, i: (b * nq + i, 0)),
        out_shape=jax.ShapeDtypeStruct((t, seq), BF16),
        scratch_shapes=[pltpu.VMEM((seq, LANES), BF16), pltpu.VMEM((seq, LANES), jnp.int32),
                        pltpu.VMEM((n_groups, 32, 8, LANES), jnp.int32),
                        pltpu.VMEM((n_groups, 8, LANES), jnp.int32),
                        pltpu.VMEM((IDX_HEADS // 2, LANES, 2 * LANES), BF16),
                        pltpu.VMEM((8, LANES), jnp.int32)],
        compiler_params=_cparams(("parallel", "arbitrary")),
        name="a_select",
    )(pf, pf, jnp.asarray(ph, BF16), jnp.asarray(plo, BF16), jnp.asarray(pkh, BF16),
      jnp.asarray(pkl, BF16))


BAND_RBLK = 8


def _band_kernel(*refs, has_sink, want_lse, kv_div, rblk):
    if has_sink:
        sink_ref, refs = refs[0], refs[1:]
    q_ref, hm_ref, kp_ref, kc_ref, vp_ref, vc_ref, bias_ref, o_ref = refs[:8]
    first = pl.program_id(1) == 0
    lane = lax.broadcasted_iota(jnp.int32, (BLOCK, LANES), 1)
    col = lax.broadcasted_iota(jnp.int32, (BLOCK, 2 * BLOCK), 1)
    no_prev = jnp.logical_and(first, col < BLOCK)
    for p in range(BRANCH_WIDTH // LANES):
        kcol = (p // kv_div) * LANES
        kk = jnp.concatenate([kp_ref[:, kcol:kcol + LANES], kc_ref[:, kcol:kcol + LANES]], axis=0)
        vv = _with_ones(jnp.concatenate([vp_ref[:, kcol:kcol + LANES], vc_ref[:, kcol:kcol + LANES]], axis=0))
        for i in range(rblk):
            rows = slice(i * BLOCK, (i + 1) * BLOCK)
            q = q_ref[rows, p * LANES:(p + 1) * LANES]
            kblk = kk[i * BLOCK:(i + 2) * BLOCK]
            vblk = vv[i * BLOCK:(i + 2) * BLOCK]
            outs, lses = [], []
            for e in range(2):
                s = lax.dot_general(q * hm_ref[e], kblk, _CONTRACT_LANES, preferred_element_type=F32)
                s = s + bias_ref[2 * p + e]
                if i == 0:
                    s = jnp.where(no_prev, NEG, s)
                m = jnp.max(s, axis=1, keepdims=True)
                if has_sink:
                    sk = sink_ref[2 * p + e]
                    m = jnp.maximum(m, sk)
                pr = jnp.exp2(s - m)
                acc = jnp.dot(pr.astype(BF16), vblk, preferred_element_type=F32)
                den = acc[:, LANES:]
                if has_sink:
                    den = den + jnp.exp2(sk - m)
                outs.append(acc[:, :LANES] / den)
                lses.append(m + jnp.log2(den))
            o_ref[rows, p * LANES:(p + 1) * LANES] = jnp.where(lane < HEAD_DIM, outs[0], outs[1]).astype(o_ref.dtype)
            if want_lse:
                refs[8][rows, p * LANES:(p + 1) * LANES] = jnp.where(lane < HEAD_DIM, lses[0], lses[1])


def _band_attn(q_arr, k_arr, v_arr, bias, nsub, sub_len, qcb, kcb, vcb, kv_w, sinks=None, want_lse=False):
    t = q_arr.shape[0]
    rblk = min(BAND_RBLK, sub_len // BLOCK)
    step_rows = rblk * BLOCK
    nbs = sub_len // step_rows
    kv_div = BRANCH_WIDTH // kv_w

    def cur(u, n):
        return u * nbs + n

    def prev(u, n):
        return jnp.where(n == 0, u * nbs * rblk, (u * nbs + n) * rblk - 1)

    in_specs = [
        pl.BlockSpec((step_rows, BRANCH_WIDTH), lambda u, n: (cur(u, n), qcb)),
        pl.BlockSpec((2, 1, LANES), lambda u, n: (0, 0, 0)),
        pl.BlockSpec((BLOCK, kv_w), lambda u, n: (prev(u, n), kcb)),
        pl.BlockSpec((step_rows, kv_w), lambda u, n: (cur(u, n), kcb)),
        pl.BlockSpec((BLOCK, kv_w), lambda u, n: (prev(u, n), vcb)),
        pl.BlockSpec((step_rows, kv_w), lambda u, n: (cur(u, n), vcb)),
        pl.BlockSpec(bias.shape, lambda u, n: (0, 0, 0)),
    ]
    args = [q_arr, _pair_masks(), k_arr, k_arr, v_arr, v_arr, bias]
    if sinks is not None:
        in_specs = [pl.BlockSpec(memory_space=pltpu.SMEM)] + in_specs
        args = [sinks] + args
    o_spec = pl.BlockSpec((step_rows, BRANCH_WIDTH), lambda u, n: (cur(u, n), 0))
    out_specs = [o_spec]
    out_shape = [jax.ShapeDtypeStruct((t, BRANCH_WIDTH), BF16)]
    if want_lse:
        out_specs.append(o_spec)
        out_shape.append(jax.ShapeDtypeStruct((t, BRANCH_WIDTH), F32))
    return pl.pallas_call(
        functools.partial(_band_kernel, has_sink=sinks is not None, want_lse=want_lse, kv_div=kv_div, rblk=rblk),
        grid=(nsub, nbs),
        in_specs=in_specs,
        out_specs=out_specs,
        out_shape=out_shape,
        compiler_params=_cparams(("parallel", "arbitrary")),
        name="band_attn",
    )(*args)


def _toeplitz(w, rows, cols):
    length = rows + cols - 1
    u = jnp.pad(w[..., ::-1], [(0, 0)] * (w.ndim - 1) + [(0, 1)])
    flat = jnp.broadcast_to(u[..., None, :], w.shape[:-1] + (rows, length + 1)).reshape(w.shape[:-1] + (-1,))
    skew = flat[..., :rows * length].reshape(w.shape[:-1] + (rows, length))
    return skew[..., rows - 1:rows - 1 + cols]


def _band_bias(table, step, max_dist):
    rel = BLOCK + np.arange(BLOCK)[:, None] - np.arange(2 * BLOCK)[None, :]
    rel_vec = np.arange(BLOCK - (2 * BLOCK - 1), 2 * BLOCK)
    bias = _toeplitz(table[_t5_bucket(jnp.asarray(rel_vec * step))].T.astype(F32), BLOCK, 2 * BLOCK)
    ok = (rel >= 0) & (rel <= max_dist)
    return jnp.where(jnp.asarray(ok)[None], bias, NEG)


def _sigmoid(x):
    return 0.5 * jnp.tanh(0.5 * x) + 0.5


def _merge_kernel(x_ref, ya_ref, yb0_ref, yb1_ref, yb2_ref, l0_ref, l1_ref, l2_ref, yc_ref, yd_ref,
                  za_ref, zb_ref, zc_ref, zd_ref, ga_ref, gb_ref, gc_ref, gd_ref, wb_ref, wo_ref, *rest):
    if len(rest) == 7:
        ng_ref, o_ref, xn_ref, y1_ref, y2_ref, s1_ref, s2_ref = rest
    else:
        ng_ref = xn_ref = None
        o_ref, y1_ref, y2_ref, s1_ref, s2_ref = rest
    ncol = BRANCH_WIDTH // LANES
    for src, dst in ((yb1_ref, y1_ref), (yb2_ref, y2_ref), (l1_ref, s1_ref), (l2_ref, s2_ref)):
        dil = src.shape[1]
        for r in range(dil):
            blk = src[0, r].astype(F32)
            for c in range(ncol):
                dst[c, pl.ds(r, src.shape[2], stride=dil), :] = blk[:, c * LANES:(c + 1) * LANES]
    wide = lambda ref: jnp.concatenate([ref[c] for c in range(ncol)], axis=1)
    l0, l1, l2 = l0_ref[0, 0], wide(s1_ref), wide(s2_ref)
    mx = jnp.maximum(jnp.maximum(l0, l1), l2)
    w0, w1, w2 = jnp.exp2(l0 - mx), jnp.exp2(l1 - mx), jnp.exp2(l2 - mx)
    yb = (w0 * yb0_ref[0, 0].astype(F32) + w1 * wide(y1_ref) + w2 * wide(y2_ref)) / (w0 + w1 + w2)
    ys = (ya_ref[...].astype(F32), yb, yc_ref[...].astype(F32), yd_ref[...].astype(F32))
    zs = (za_ref, zb_ref, zc_ref, zd_ref)
    gs = (ga_ref, gb_ref, gc_ref, gd_ref)
    merged = jnp.zeros(o_ref.shape, F32)
    for n in range(N_BRANCH):
        z = zs[n][...].astype(F32)
        u = (ys[n] * (z * _sigmoid(z))).astype(BF16)
        proj = jnp.dot(u, wb_ref[n], preferred_element_type=F32)
        merged = merged + _sigmoid(gs[n][...].astype(F32)) * proj
    x_new = x_ref[...] + jnp.dot(merged.astype(BF16), wo_ref[...], preferred_element_type=F32)
    o_ref[...] = x_new
    if xn_ref is not None:
        xn_ref[...] = _rms_bf16(x_new, ng_ref[...])


def _merge(x2, ya, yb, lb, yc, yd, pp, wb, wo, batch, seq, next_gain=None, tm=512):
    t = x2.shape[0]
    bw = BRANCH_WIDTH
    nsb = seq // tm
    row = lambda w, c: pl.BlockSpec((tm, w), lambda i: (i, c))

    def sub(dil):
        return pl.BlockSpec((1, dil, tm // dil, bw), lambda i: (i // nsb, 0, i % nsb, 0))

    dils = [dil for _, dil in B_GROUPS]
    yb = [a.reshape(batch, dil, seq // dil, bw) for a, dil in zip(yb, dils)]
    lb = [a.reshape(batch, dil, seq // dil, bw) for a, dil in zip(lb, dils)]
    in_specs = ([row(D_MODEL, 0), row(bw, 0)] + [sub(dil) for dil in dils] * 2 + [row(bw, 0), row(bw, 0)]
                + [row(bw, PP_AZ // bw), row(bw, PP_BZ // bw), row(bw, PP_CZ // bw), row(bw, PP_DZ // bw)]
                + [row(D_MODEL, PP_G // D_MODEL + n) for n in range(N_BRANCH)]
                + [pl.BlockSpec((N_BRANCH, bw, D_MODEL), lambda i: (0, 0, 0)),
                   pl.BlockSpec((D_MODEL, D_MODEL), lambda i: (0, 0))])
    args = [x2, ya, yb[0], yb[1], yb[2], lb[0], lb[1], lb[2], yc, yd, pp, pp, pp, pp, pp, pp, pp, pp, wb, wo]
    out_specs = [row(D_MODEL, 0)]
    out_shape = [jax.ShapeDtypeStruct((t, D_MODEL), F32)]
    if next_gain is not None:
        in_specs.append(pl.BlockSpec((1, D_MODEL), lambda i: (0, 0)))
        args.append(next_gain.reshape(1, D_MODEL))
        out_specs.append(row(D_MODEL, 0))
        out_shape.append(jax.ShapeDtypeStruct((t, D_MODEL), BF16))
    return pl.pallas_call(
        _merge_kernel,
        grid=(t // tm,),
        in_specs=in_specs,
        out_specs=out_specs,
        out_shape=out_shape,
        scratch_shapes=[pltpu.VMEM((bw // LANES, tm, LANES), F32)] * 4,
        compiler_params=_cparams(("parallel",)),
        name="merge",
    )(*args)


def _w_in_columns():
    n_bq = len(B_GROUPS) * B_HEADS * HEAD_DIM
    groups = (
        (("a_q", A_HEADS * HEAD_DIM), ("a_k", A_HEADS * HEAD_DIM), ("a_v", A_HEADS * HEAD_DIM),
         ("a_z", BRANCH_WIDTH), ("a_iq", IDX_HEADS * IDX_DIM), ("a_ik", IDX_DIM), ("a_iw", IDX_HEADS)),
        (("b_q", n_bq), ("b_k", n_bq), ("b_v", n_bq), ("b_z", BRANCH_WIDTH)),
        (("c_q", C_Q_LORA), ("c_kv", C_KV_LORA), ("c_pe", C_ROPE), ("c_z", BRANCH_WIDTH)),
        (("d_q", D_HEADS * HEAD_DIM), ("d_k", D_KV_HEADS * HEAD_DIM), ("d_v", D_KV_HEADS * HEAD_DIM),
         ("d_z", BRANCH_WIDTH)),
        (("gates", N_BRANCH * D_MODEL),),
    )
    table, start = {}, 0
    for group in groups:
        for name, width in group:
            table[name] = (start, width)
            start += width
    return table


def _layer_weights(w_in, qk_gain_a, qk_gain_b, qk_gain_d):
    table = _w_in_columns()
    depth = w_in.shape[0]

    def cols(name, part=None):
        start, width = table[name]
        if part is not None:
            width //= part[1]
            start += part[0] * width
        return w_in[:, :, start:start + width]

    def dup_heads(a):
        return jnp.concatenate([a[..., :HEAD_DIM], a[..., :HEAD_DIM], a[..., HEAD_DIM:], a[..., HEAD_DIM:]], axis=-1)

    def zeros(width):
        return jnp.zeros((depth, D_MODEL, width), w_in.dtype)

    n_grp = len(B_GROUPS)
    w_pn = jnp.concatenate([cols("a_q"), cols("a_k"), cols("d_q"), dup_heads(cols("d_k")), dup_heads(cols("d_v"))],
                           axis=-1).astype(BF16)
    w_pp = jnp.concatenate([cols("gates"), cols("a_v"), cols("a_z"), cols("b_z"), cols("c_z"), cols("d_z")],
                           axis=-1).astype(BF16)
    w_bg = [jnp.concatenate([cols("b_q", (g, n_grp)), cols("b_k", (g, n_grp)), cols("b_v", (g, n_grp))],
                            axis=-1).astype(BF16) for g in range(n_grp)]
    idx = jnp.concatenate([cols("a_iq"), cols("a_ik"), cols("a_iw")], axis=-1)
    lora = jnp.concatenate([cols("c_q"), cols("c_kv"), cols("c_pe")], axis=-1)
    w_pf = jnp.concatenate([idx, zeros(PF_HALF - idx.shape[-1]), lora, zeros(PF_HALF - lora.shape[-1])],
                           axis=-1).astype(BF16)

    scale = HEAD_DIM ** -0.5 * LOG2E
    tile = lambda g, reps: jnp.tile(g, (1, reps))
    hg = jnp.concatenate([tile(qk_gain_a[:, 0] * scale, A_HEADS), tile(qk_gain_a[:, 1], A_HEADS),
                          tile(qk_gain_d[:, 0] * scale, D_HEADS), tile(qk_gain_d[:, 1], 2 * D_KV_HEADS),
                          jnp.ones((depth, PN_W - PN_DV), F32)], axis=-1)
    flag = jnp.concatenate([jnp.ones((PN_DV,), F32), jnp.zeros((PN_W - PN_DV,), F32)])
    hg_b = jnp.concatenate([tile(qk_gain_b[:, 0] * scale, B_HEADS), tile(qk_gain_b[:, 1], B_HEADS),
                            jnp.ones((depth, BRANCH_WIDTH), F32)], axis=-1)
    flag_b = jnp.concatenate([jnp.ones((2 * BRANCH_WIDTH,), F32), jnp.zeros((BRANCH_WIDTH,), F32)])
    return w_pn, w_pp, w_bg, w_pf, hg, flag, hg_b, flag_b


def _c_weights(qk_gain_c, w_q_b, w_kv_b, seq):
    src_q, src_k, gain_idx, head, pe_src, rope_j, rope_half = _c_layout_tables()
    take = lambda w, src: jnp.where(jnp.asarray(src >= 0), jnp.take(w, jnp.asarray(np.maximum(src, 0)), axis=-1), 0.0)
    wq = take(w_q_b, src_q).astype(BF16)
    wk = take(w_kv_b, src_k).astype(BF16)
    v_src = np.array([h * (C_NOPE + C_V) + C_NOPE + c for h in range(C_HEADS) for c in range(C_V)])
    wv = jnp.take(w_kv_b, jnp.asarray(v_src), axis=-1).astype(BF16)
    ppe = np.zeros((LANES, C_QK_W), np.float32)
    ln = np.nonzero(pe_src >= 0)[0]
    ppe[pe_src[ln], ln] = 1
    hp = head[:C_PAIR]
    grp = ((hp[:, None] == hp[None, :]) & (hp[:, None] >= 0)).astype(np.float32)
    g_take = lambda g: jnp.where(jnp.asarray(gain_idx >= 0), jnp.take(g, jnp.asarray(np.maximum(gain_idx, 0)), axis=-1), 0.0)
    gq = g_take(qk_gain_c[:, 0]) * ((C_NOPE + C_ROPE) ** -0.5 * LOG2E)
    gk = g_take(qk_gain_c[:, 1])
    freq = ROPE_THETA ** (-jnp.arange(C_ROPE_HALF, dtype=F32) / C_ROPE_HALF)
    ang = jnp.arange(seq).astype(F32)[:, None] * freq[None, :]
    cos_j, sin_j = jnp.cos(ang), jnp.sin(ang)
    place = np.zeros((C_ROPE_HALF, C_PAIR), np.float32)
    lanes = np.nonzero(rope_j[:C_PAIR] >= 0)[0]
    place[rope_j[lanes], lanes] = 1
    is_rope = jnp.asarray(rope_j[:C_PAIR] >= 0)
    hi = lax.Precision.HIGHEST
    cos = jnp.where(is_rope, jnp.dot(cos_j, jnp.asarray(place), precision=hi), 1.0)
    sin = jnp.dot(sin_j, jnp.asarray(place), precision=hi)
    s1 = jnp.where(jnp.asarray(rope_half[:C_PAIR] == 0), -sin, 0.0)
    s2 = jnp.where(jnp.asarray(rope_half[:C_PAIR] == 1), sin, 0.0)
    return wq, wk, wv, jnp.asarray(ppe, BF16), jnp.asarray(grp, BF16), gq, gk, cos, s1, s2


def _a_bias_tiles(table):
    dist_vec = np.arange(-(BLOCK - 1), A_BIAS_TILES * BLOCK)
    vec = table[_t5_bucket(jnp.asarray(dist_vec))].T.astype(F32)
    windows = jnp.stack([vec[:, d * BLOCK:d * BLOCK + 2 * BLOCK - 1] for d in range(A_BIAS_TILES)], axis=1)
    return _toeplitz(windows, BLOCK, BLOCK)


def kernel(x, norm_gain, w_in, qk_gain_a, qk_gain_b, qk_gain_c, qk_gain_d, c_q_gain, c_kv_gain,
           w_q_b, w_kv_b, sinks, rel_bias, w_branch, w_out):
    batch, seq, d_model = x.shape
    depth = w_in.shape[0]
    t = batch * seq
    x2 = x.reshape(t, d_model)

    w_pn, w_pp, w_bg, w_pf, hg, flag, hg_b, flag_b = _layer_weights(w_in, qk_gain_a, qk_gain_b, qk_gain_d)
    wq, wk, wv, ppe, grp, gq, gk, cos, s1, s2 = _c_weights(qk_gain_c, w_q_b, w_kv_b, seq)
    wb = w_branch.astype(BF16)
    wo = w_out.astype(BF16)

    rel_bias2 = rel_bias * LOG2E
    bias_a = _a_bias_tiles(rel_bias2[:, :A_HEADS])
    bias_b = [_band_bias(rel_bias2[:, A_HEADS + g * B_HEADS:A_HEADS + (g + 1) * B_HEADS], dil, window // dil)
              for g, (window, dil) in enumerate(B_GROUPS)]
    bias_d = _band_bias(rel_bias2[:, N_BIAS_HEADS - D_HEADS:], 1, D_WINDOW - 1)

    xn = _norm(x2, norm_gain[0])
    for l in range(depth):
        pn = _proj(xn, w_pn[l], BF16, head_gain=hg[l], flag=flag)
        pp = _proj(xn, w_pp[l], BF16)
        pf = _proj(xn, w_pf[l], F32)

        sel = _a_select(pf, batch, seq)
        ya = _a_attn(pn, pp, sel, bias_a, batch, seq)

        yb, lb = [], []
        for g, (window, dil) in enumerate(B_GROUPS):
            bg = _proj(xn, w_bg[g][l], BF16, head_gain=hg_b[l], flag=flag_b, dil=dil, seq=seq)
            o, lse = _band_attn(bg, bg, bg, bias_b[g], batch * dil, seq // dil, 0, 1, 2, BRANCH_WIDTH,
                                want_lse=True)
            yb.append(o)
            lb.append(lse)

        qc, kc, vc = _c_prep(pf, c_q_gain[l].reshape(1, -1), c_kv_gain[l].reshape(1, -1), wq[l], wk[l], wv[l],
                             ppe, grp, gq[l].reshape(1, -1), gk[l].reshape(1, -1), cos, s1, s2, seq)
        yc = _c_attn(qc, kc, vc, batch, seq)

        (yd,) = _band_attn(pn, pn, pn, bias_d, batch, seq, PN_DQ // BRANCH_WIDTH, PN_DK // DKV_DUP_W,
                           PN_DV // DKV_DUP_W, DKV_DUP_W, sinks=sinks[l] * LOG2E)

        if l + 1 < depth:
            x2, xn = _merge(x2, ya, yb, lb, yc, yd, pp, wb[l], wo[l], batch, seq, next_gain=norm_gain[l + 1])
        else:
            (x2,) = _merge(x2, ya, yb, lb, yc, yd, pp, wb[l], wo[l], batch, seq)
    return x2.reshape(batch, seq, d_model)
```

```python
import functools
import math

import numpy as np
import jax
import jax.numpy as jnp
from jax import lax
from jax.experimental import pallas as pl
from jax.experimental.pallas import tpu as pltpu

F32 = jnp.float32
BF16 = jnp.bfloat16

D_MODEL = 1024
BLOCK = 128
HEAD_DIM = 64
N_BRANCH = 4
BRANCH_WIDTH = 512
EPS = 1e-6
A_HEADS = 8
IDX_HEADS = 8
IDX_DIM = 32
TOPK_MAX = 256
B_GROUPS = ((128, 1), (512, 4), (2048, 16))
B_HEADS = 8
C_HEADS = 8
C_NOPE = 64
C_ROPE = 32
C_V = 64
C_Q_LORA = 256
C_KV_LORA = 128
ROPE_THETA = 10000.0
D_HEADS = 8
D_KV_HEADS = 2
D_WINDOW = 128
NUM_BUCKETS = 32
MAX_DISTANCE = 2048
N_BIAS_HEADS = A_HEADS + len(B_GROUPS) * B_HEADS + D_HEADS

LANES = 128
MXU_WIDTH = 256
NEG = -1e30
LOG2E = math.log2(math.e)
INT_MIN = -(2 ** 31)

DKV_DUP_W = 2 * D_KV_HEADS * HEAD_DIM
PN_AQ = 0
PN_AK = PN_AQ + A_HEADS * HEAD_DIM
PN_DQ = PN_AK + A_HEADS * HEAD_DIM
PN_DK = PN_DQ + D_HEADS * HEAD_DIM
PN_DV = PN_DK + DKV_DUP_W
PN_W = PN_DV + DKV_DUP_W
PP_G = 0
PP_AV = PP_G + N_BRANCH * D_MODEL
PP_AZ = PP_AV + A_HEADS * HEAD_DIM
PP_BZ = PP_AZ + BRANCH_WIDTH
PP_CZ = PP_BZ + BRANCH_WIDTH
PP_DZ = PP_CZ + BRANCH_WIDTH
PF_HALF = 512

A_BIAS_TILES = MAX_DISTANCE // BLOCK + 2


def _cparams(sem, vmem_mb=48):
    return pltpu.CompilerParams(dimension_semantics=sem, vmem_limit_bytes=vmem_mb * 1024 * 1024)


def _t5_bucket(dist):
    max_exact = NUM_BUCKETS // 2
    d = jnp.maximum(dist, 0)
    logd = jnp.log(jnp.maximum(d, 1).astype(F32) / max_exact)
    large = max_exact + (logd / math.log(MAX_DISTANCE / max_exact) * (NUM_BUCKETS - max_exact)).astype(jnp.int32)
    return jnp.where(d < max_exact, d, jnp.minimum(large, NUM_BUCKETS - 1))


def _rms_bf16(x, gain):
    return (x * lax.rsqrt(jnp.mean(x * x, axis=-1, keepdims=True) + EPS) * gain).astype(BF16)


def _norm_kernel(x_ref, g_ref, o_ref):
    o_ref[...] = _rms_bf16(x_ref[...], g_ref[...])


def _norm(x2, gain, tm=1024):
    t, d = x2.shape
    return pl.pallas_call(
        _norm_kernel,
        grid=(t // tm,),
        in_specs=[pl.BlockSpec((tm, d), lambda i: (i, 0)), pl.BlockSpec((1, d), lambda i: (0, 0))],
        out_specs=pl.BlockSpec((tm, d), lambda i: (i, 0)),
        out_shape=jax.ShapeDtypeStruct((t, d), BF16),
        compiler_params=_cparams(("parallel",)),
        name="norm",
    )(x2, gain.reshape(1, d))


def _proj_kernel(xn_ref, w_ref, *rest, norm, dil):
    rest = list(rest)
    res_ref = rest.pop(-1) if dil is not None and dil > 1 else None
    if norm:
        hg_ref, flag_ref, bd_ref, o_ref = rest
    else:
        (o_ref,) = rest

    def emit(h):
        if dil is None:
            o_ref[...] = h.astype(o_ref.dtype)
        elif dil == 1:
            o_ref[0, 0] = h.astype(o_ref.dtype)
        else:
            sub = h.shape[0] // dil
            for c in range(h.shape[1] // LANES):
                res_ref[c] = h[:, c * LANES:(c + 1) * LANES]
            for r in range(dil):
                o_ref[0, r] = jnp.concatenate(
                    [res_ref[c, pl.ds(r, sub, stride=dil), :] for c in range(h.shape[1] // LANES)],
                    axis=1).astype(o_ref.dtype)

    h = jnp.dot(xn_ref[...], w_ref[...], preferred_element_type=F32)
    if norm:
        sq = (h * h).astype(BF16)
        slab = bd_ref.shape[0]
        ss = jnp.concatenate([jnp.dot(sq[:, c:c + slab], bd_ref[...], preferred_element_type=F32)
                              for c in range(0, sq.shape[1], slab)], axis=1)
        scale = lax.rsqrt(ss * (1.0 / HEAD_DIM) + EPS) * hg_ref[...]
        h = h * jnp.where(flag_ref[...] > 0, scale, 1.0)
    emit(h)


def _proj(xn, w, out_dtype, head_gain=None, flag=None, dil=None, seq=None, tm=2048, tn=512):
    t, d = xn.shape
    n = w.shape[1]
    norm = head_gain is not None
    scratch = []
    if dil is None:
        out_spec = pl.BlockSpec((tm, tn), lambda i, j: (i, j))
        out_shape = jax.ShapeDtypeStruct((t, n), out_dtype)
        kdil = None
    else:
        nsb = seq // tm
        out_spec = pl.BlockSpec((1, dil, tm // dil, tn), lambda i, j: (i // nsb, 0, i % nsb, j))
        out_shape = jax.ShapeDtypeStruct((t // seq, dil, seq // dil, n), out_dtype)
        kdil = dil
        if dil > 1:
            scratch.append(pltpu.VMEM((tn // LANES, tm, LANES), F32))
    in_specs = [pl.BlockSpec((tm, d), lambda i, j: (i, 0)),
                pl.BlockSpec((d, tn), lambda i, j: (0, j))]
    args = [xn, w]
    if norm:
        lane = np.arange(MXU_WIDTH)
        bd = jnp.asarray((lane[:, None] // HEAD_DIM == lane[None, :] // HEAD_DIM), BF16)
        in_specs += [pl.BlockSpec((1, tn), lambda i, j: (0, j)),
                     pl.BlockSpec((1, tn), lambda i, j: (0, j)),
                     pl.BlockSpec((MXU_WIDTH, MXU_WIDTH), lambda i, j: (0, 0))]
        args += [head_gain.reshape(1, n), flag.reshape(1, n), bd]
    out = pl.pallas_call(
        functools.partial(_proj_kernel, norm=norm, dil=kdil),
        grid=(t // tm, n // tn),
        in_specs=in_specs,
        out_specs=out_spec,
        out_shape=out_shape,
        scratch_shapes=scratch,
        compiler_params=_cparams(("parallel", "parallel")),
        name=("proj_norm" if norm else "proj") + ("" if dil is None else "_dil%d" % dil),
    )(*args)
    return out.reshape(t, n)


C_PAIR = 256
C_QK_W = (C_HEADS // 2) * C_PAIR
C_ROPE_HALF = C_ROPE // 2


def _c_lane(h, c):
    base = (h // 2) * C_PAIR
    e = h % 2
    if c < C_NOPE:
        return base + e * C_NOPE + c
    return base + 2 * C_NOPE + e * C_ROPE + (c - C_NOPE)


def _c_layout_tables():
    src_q = np.full(C_QK_W, -1, np.int64)
    src_k = np.full(C_QK_W, -1, np.int64)
    gain_idx = np.full(C_QK_W, -1, np.int64)
    head = np.full(C_QK_W, -1, np.int64)
    pe_src = np.full(C_QK_W, -1, np.int64)
    rope_j = np.full(C_QK_W, -1, np.int64)
    rope_half = np.zeros(C_QK_W, np.int64)
    for h in range(C_HEADS):
        for c in range(C_NOPE + C_ROPE):
            ln = _c_lane(h, c)
            src_q[ln] = h * (C_NOPE + C_ROPE) + c
            gain_idx[ln] = c
            head[ln] = h
            if c < C_NOPE:
                src_k[ln] = h * (C_NOPE + C_V) + c
            else:
                r = c - C_NOPE
                pe_src[ln] = r
                rope_j[ln] = r % C_ROPE_HALF
                rope_half[ln] = r // C_ROPE_HALF
    return src_q, src_k, gain_idx, head, pe_src, rope_j, rope_half


def _c_prep_kernel(pf_ref, cqg_ref, ckvg_ref, wq_ref, wk_ref, wv_ref, ppe_ref, grp_ref,
                   gq_ref, gk_ref, cos_ref, s1_ref, s2_ref, q_ref, k_ref, v_ref):
    blk = pf_ref[...]
    cq = blk[:, :C_Q_LORA]
    ckv = blk[:, C_Q_LORA:C_Q_LORA + C_KV_LORA]
    pe = blk[:, C_Q_LORA + C_KV_LORA:]

    def rms(v, g):
        return v * lax.rsqrt(jnp.mean(v * v, axis=-1, keepdims=True) + EPS) * g

    def head_norm_rope(raw, gain):
        sq = (raw * raw).astype(BF16)
        ss = jnp.concatenate([jnp.dot(sq[:, p * C_PAIR:(p + 1) * C_PAIR], grp_ref[...], preferred_element_type=F32)
                              for p in range(C_QK_W // C_PAIR)], axis=1)
        y = raw * lax.rsqrt(ss * (1.0 / (C_NOPE + C_ROPE)) + EPS) * gain
        up = pltpu.roll(y, C_QK_W - C_ROPE_HALF, 1)
        dn = pltpu.roll(y, C_ROPE_HALF, 1)
        wide = lambda ref: jnp.concatenate([ref[...]] * (C_QK_W // C_PAIR), axis=1)
        return y * wide(cos_ref) + up * wide(s1_ref) + dn * wide(s2_ref)

    cqn = rms(cq, cqg_ref[...]).astype(BF16)
    q_raw = jnp.dot(cqn, wq_ref[...], preferred_element_type=F32)
    q_ref[...] = head_norm_rope(q_raw, gq_ref[...]).astype(BF16)

    ckvn = rms(ckv, ckvg_ref[...]).astype(BF16)
    pe_hi = pe.astype(BF16)
    pe_lo = (pe - pe_hi.astype(F32)).astype(BF16)
    k_raw = (jnp.dot(ckvn, wk_ref[...], preferred_element_type=F32)
             + jnp.dot(pe_hi, ppe_ref[...], preferred_element_type=F32)
             + jnp.dot(pe_lo, ppe_ref[...], preferred_element_type=F32))
    k_ref[...] = head_norm_rope(k_raw, gk_ref[...]).astype(BF16)
    v_ref[...] = jnp.dot(ckvn, wv_ref[...], preferred_element_type=F32).astype(BF16)


def _c_prep(pf, cqg, ckvg, wq, wk, wv, ppe, grp, gq, gk, cos, s1, s2, seq, tm=512):
    t = pf.shape[0]
    nsb = seq // tm
    full = lambda shape: pl.BlockSpec(shape, lambda i: (0,) * len(shape))
    tab = pl.BlockSpec((tm, C_PAIR), lambda i: (i % nsb, 0))
    return pl.pallas_call(
        _c_prep_kernel,
        grid=(t // tm,),
        in_specs=[pl.BlockSpec((tm, PF_HALF), lambda i: (i, 1)),
                  full((1, C_Q_LORA)), full((1, C_KV_LORA)),
                  full((C_Q_LORA, C_QK_W)), full((C_KV_LORA, C_QK_W)), full((C_KV_LORA, BRANCH_WIDTH)),
                  full((LANES, C_QK_W)), full((C_PAIR, C_PAIR)),
                  full((1, C_QK_W)), full((1, C_QK_W)), tab, tab, tab],
        out_specs=[pl.BlockSpec((tm, C_QK_W), lambda i: (i, 0)),
                   pl.BlockSpec((tm, C_QK_W), lambda i: (i, 0)),
                   pl.BlockSpec((tm, BRANCH_WIDTH), lambda i: (i, 0))],
        out_shape=[jax.ShapeDtypeStruct((t, C_QK_W), BF16),
                   jax.ShapeDtypeStruct((t, C_QK_W), BF16),
                   jax.ShapeDtypeStruct((t, BRANCH_WIDTH), BF16)],
        compiler_params=_cparams(("parallel",)),
        name="c_prep",
    )(pf, cqg, ckvg, wq, wk, wv, ppe, grp, gq, gk, cos, s1, s2)


def _flash_update(e, s, v_ones, m_ref, acc_ref, rows=slice(None)):
    m_prev = m_ref[e, rows]
    m_new = jnp.maximum(m_prev, jnp.max(s, axis=1, keepdims=True))
    alpha = jnp.exp2(m_prev - m_new)
    p = jnp.exp2(s - jnp.concatenate([m_new] * (s.shape[1] // LANES), axis=1))
    acc_ref[e, rows] = (jnp.concatenate([alpha, alpha], axis=1) * acc_ref[e, rows]
                        + jnp.dot(p.astype(BF16), v_ones, preferred_element_type=F32))
    m_ref[e, rows] = m_new


def _flash_init(m_ref, acc_ref):
    m_ref[...] = jnp.full(m_ref.shape, NEG, F32)
    acc_ref[...] = jnp.zeros(acc_ref.shape, F32)


def _flash_finish(o_ref, acc_ref):
    lane = lax.broadcasted_iota(jnp.int32, o_ref.shape, 1)
    o0 = acc_ref[0, :, :LANES] / acc_ref[0, :, LANES:]
    o1 = acc_ref[1, :, :LANES] / acc_ref[1, :, LANES:]
    o_ref[...] = jnp.where(lane < HEAD_DIM, o0, o1).astype(o_ref.dtype)


def _with_ones(v_chunk):
    return jnp.concatenate([v_chunk, jnp.ones(v_chunk.shape, v_chunk.dtype)], axis=1)


_CONTRACT_LANES = (((1,), (1,)), ((), ()))


def _head_masks(width, ranges0, ranges1):
    m = np.zeros((2, 1, width), np.float32)
    for e, ranges in enumerate((ranges0, ranges1)):
        for lo, hi in ranges:
            m[e, 0, lo:hi] = 1
    return jnp.asarray(m, BF16)


def _pair_masks():
    return _head_masks(LANES, [(0, HEAD_DIM)], [(HEAD_DIM, LANES)])


def _transpose_keys(k_ref, kt_ref, rows=512):
    def body(c, carry):
        off = pl.multiple_of(c * rows, rows)
        kt_ref[:, pl.ds(off, rows)] = k_ref[pl.ds(off, rows), :].astype(F32).T.astype(kt_ref.dtype)
        return carry
    lax.fori_loop(0, k_ref.shape[0] // rows, body, 0)


def _chunk_loop(chunk, n, unroll, carry=None, slots=None):
    start = 0
    width = unroll
    while width >= 1:
        def body(g, carry, width=width, start=start):
            for j in range(width):
                carry = chunk(start + g * width + j, carry) if slots is None else \
                    chunk(start + g * width + j, carry, j)
            return carry
        groups = (n - start) // width
        carry = lax.fori_loop(0, groups, body, carry)
        start = start + groups * width
        width //= 2
    return carry


def _c_attn_kernel(q_ref, hm_ref, k_ref, v_ref, o_ref, kt_ref, m_ref, acc_ref, *, tq, tk, unroll):
    qi = pl.program_id(2)

    @pl.when(qi == 0)
    def _():
        _transpose_keys(k_ref, kt_ref)

    q = q_ref[...]
    qs = (q * hm_ref[0], q * hm_ref[1])
    _flash_init(m_ref, acc_ref)
    n_full = (qi * tq) // tk

    def chunk(c, row0=0, diagonal=False):
        off = pl.multiple_of(c * tk, tk)
        kc = kt_ref[:, pl.ds(off, tk)]
        vc = _with_ones(v_ref[pl.ds(off, tk), :])
        for e in range(2):
            if not diagonal:
                rows = slice(row0, tq)
                _flash_update(e, jnp.dot(qs[e][rows], kc, preferred_element_type=F32), vc, m_ref, acc_ref, rows)
                continue
            rows = slice(row0, row0 + tk)
            s = jnp.dot(qs[e][rows], kc, preferred_element_type=F32)
            s = jnp.where(lax.broadcasted_iota(jnp.int32, s.shape, 1) <= lax.broadcasted_iota(jnp.int32, s.shape, 0),
                          s, NEG)
            _flash_update(e, s, vc, m_ref, acc_ref, rows)
            if row0 + tk < tq:
                rows = slice(row0 + tk, tq)
                _flash_update(e, jnp.dot(qs[e][rows], kc, preferred_element_type=F32), vc, m_ref, acc_ref, rows)

    _chunk_loop(lambda c, _: chunk(c), n_full, unroll)
    for j in range(tq // tk):
        chunk(n_full + j, row0=j * tk, diagonal=True)
    _flash_finish(o_ref, acc_ref)


def _c_attn(qc, kc, vc, batch, seq, tq=1024, tk=512, unroll=4):
    t = qc.shape[0]
    nq = seq // tq
    npair = C_HEADS // 2
    return pl.pallas_call(
        functools.partial(_c_attn_kernel, tq=tq, tk=tk, unroll=unroll),
        grid=(batch, npair, nq),
        in_specs=[pl.BlockSpec((tq, C_PAIR), lambda b, p, i: (b * nq + i, p)),
                  pl.BlockSpec((2, 1, C_PAIR), lambda b, p, i: (0, 0, 0)),
                  pl.BlockSpec((seq, C_PAIR), lambda b, p, i: (b, p)),
                  pl.BlockSpec((seq, LANES), lambda b, p, i: (b, p))],
        out_specs=pl.BlockSpec((tq, LANES), lambda b, p, i: (b * nq + i, p)),
        out_shape=jax.ShapeDtypeStruct((t, BRANCH_WIDTH), BF16),
        scratch_shapes=[pltpu.VMEM((C_PAIR, seq), BF16),
                        pltpu.VMEM((2, tq, LANES), F32), pltpu.VMEM((2, tq, 2 * LANES), F32)],
        compiler_params=_cparams(("parallel", "parallel", "arbitrary")),
        name="c_attn",
    )(qc, _head_masks(C_PAIR, [(0, C_NOPE), (2 * C_NOPE, 2 * C_NOPE + C_ROPE)],
                      [(C_NOPE, 2 * C_NOPE), (2 * C_NOPE + C_ROPE, 2 * C_NOPE + 2 * C_ROPE)]), kc, vc)


def _a_attn_kernel(q_ref, hm_ref, k_ref, v_ref, sel_ref, bias_ref, o_ref, kt_ref, m_ref, acc_ref, msk_ref,
                   *, tq, tk, unroll):
    qi = pl.program_id(2)

    @pl.when(qi == 0)
    def _():
        _transpose_keys(k_ref, kt_ref)

    q = q_ref[...]
    qs = (q * hm_ref[0], q * hm_ref[1])
    _flash_init(m_ref, acc_ref)
    n_chunks = ((qi + 1) * tq + tk - 1) // tk

    def chunk(c, carry, slot):
        off = pl.multiple_of(c * tk, tk)
        kc = kt_ref[:, pl.ds(off, tk)]
        vc = _with_ones(v_ref[pl.ds(off, tk), :])
        msk_ref[slot] = sel_ref[:, pl.ds(off, tk)].astype(F32)
        for e in range(2):
            s = jnp.dot(qs[e], kc, preferred_element_type=F32)
            rows = []
            for i in range(tq // BLOCK):
                tiles = []
                for j in range(tk // BLOCK):
                    d = (qi * (tq // BLOCK) + i) - (c * (tk // BLOCK) + j)
                    d = jnp.clip(d, 0, A_BIAS_TILES - 1)
                    tiles.append(bias_ref[e, d])
                rows.append(jnp.concatenate(tiles, axis=1))
            s = s + jnp.concatenate(rows, axis=0) + msk_ref[slot]
            _flash_update(e, s, vc, m_ref, acc_ref)
        return carry

    _chunk_loop(chunk, n_chunks, unroll, slots=unroll)
    _flash_finish(o_ref, acc_ref)


def _a_attn(pn, pp, sel, bias_tiles, batch, seq, tq=512, tk=512, unroll=8):
    t = pn.shape[0]
    nq = seq // tq
    npair = A_HEADS // 2
    qcol, kcol, vcol = PN_AQ // LANES, PN_AK // LANES, PP_AV // LANES
    return pl.pallas_call(
        functools.partial(_a_attn_kernel, tq=tq, tk=tk, unroll=unroll),
        grid=(batch, npair, nq),
        in_specs=[pl.BlockSpec((tq, LANES), lambda b, p, i: (b * nq + i, qcol + p)),
                  pl.BlockSpec((2, 1, LANES), lambda b, p, i: (0, 0, 0)),
                  pl.BlockSpec((seq, LANES), lambda b, p, i: (b, kcol + p)),
                  pl.BlockSpec((seq, LANES), lambda b, p, i: (b, vcol + p)),
                  pl.BlockSpec((tq, seq), lambda b, p, i: (b * nq + i, 0)),
                  pl.BlockSpec((2, A_BIAS_TILES, BLOCK, BLOCK), lambda b, p, i: (p, 0, 0, 0))],
        out_specs=pl.BlockSpec((tq, LANES), lambda b, p, i: (b * nq + i, p)),
        out_shape=jax.ShapeDtypeStruct((t, BRANCH_WIDTH), BF16),
        scratch_shapes=[pltpu.VMEM((LANES, seq), BF16),
                        pltpu.VMEM((2, tq, LANES), F32), pltpu.VMEM((2, tq, 2 * LANES), F32),
                        pltpu.VMEM((unroll, tq, tk), F32)],
        compiler_params=_cparams(("parallel", "parallel", "arbitrary")),
        name="a_attn",
    )(pn, _pair_masks(), pn, pp, sel, bias_tiles)


def _sortable_key(score):
    bits = pltpu.bitcast(score, jnp.int32)
    return bits ^ ((bits >> 31) & jnp.int32(0x7FFFFFFF))


def _bit_transpose32(words):
    a = list(words)
    j, m = 16, 0x0000FFFF
    while j:
        for k in range(32):
            if k & j:
                continue
            t = (a[k] ^ lax.shift_right_logical(a[k + j], jnp.int32(j))) & m
            a[k] = a[k] ^ t
            a[k + j] = a[k + j] ^ lax.shift_left(t, jnp.int32(j))
        j >>= 1
        m ^= (m << j) & 0xFFFFFFFF
    return a


SEL_ROWS = 512
PLANE_ROWS = 256
SCORE_UNROLL = 4


def _a_select_kernel(qblk_ref, kblk_ref, ph_ref, pl_ref, pkh_ref, pkl_ref, sel_ref,
                     ikx_ref, keys_ref, planes_ref, eq_ref, iqt_ref, jcut_ref, *, seq, k_sel):
    qi = pl.program_id(1)
    n_sel = (qi * BLOCK + BLOCK + SEL_ROWS - 1) // SEL_ROWS

    @pl.when(qi == 0)
    def _():
        def prep(c, carry):
            off = pl.multiple_of(c * SEL_ROWS, SEL_ROWS)
            kv = kblk_ref[pl.ds(off, SEL_ROWS), :]
            hi = kv.astype(BF16)
            lo = (kv - hi.astype(F32)).astype(BF16)
            ikx_ref[pl.ds(off, SEL_ROWS), :] = (
                jnp.dot(hi, pkh_ref[...], preferred_element_type=F32)
                + jnp.dot(lo, pkl_ref[...], preferred_element_type=F32)).astype(BF16)
            return carry
        lax.fori_loop(0, seq // SEL_ROWS, prep, 0)

        def clear(g, carry):
            planes_ref[g] = jnp.zeros(planes_ref.shape[1:], jnp.int32)
            return carry
        lax.fori_loop(0, planes_ref.shape[0], clear, 0)

    qb = qblk_ref[...]
    iq = qb[:, :IDX_HEADS * IDX_DIM]
    iq_hi = iq.astype(BF16)
    iq_lo = (iq - iq_hi.astype(F32)).astype(BF16)
    iqx = (jnp.dot(iq_hi, ph_ref[...], preferred_element_type=F32)
           + jnp.dot(iq_lo, pl_ref[...], preferred_element_type=F32))
    for h in range(IDX_HEADS):
        iqt_ref[h // 2, :, (h % 2) * LANES:(h % 2 + 1) * LANES] = iqx[:, h * LANES:(h + 1) * LANES].T.astype(BF16)
    iw_t = qb[:, IDX_HEADS * IDX_DIM:IDX_HEADS * IDX_DIM + LANES].T
    iw_rows = [iw_t[IDX_DIM + h:IDX_DIM + h + 1, :] for h in range(IDX_HEADS)]
    q_pos = qi * BLOCK + lax.broadcasted_iota(jnp.int32, (SEL_ROWS, BLOCK), 1)
    k_row = lax.broadcasted_iota(jnp.int32, (SEL_ROWS, BLOCK), 0)

    def score_chunk(c, masked):
        off = pl.multiple_of(c * SEL_ROWS, SEL_ROWS)
        kx = ikx_ref[pl.ds(off, SEL_ROWS), :]
        sc = jnp.zeros((SEL_ROWS, BLOCK), F32)
        for hp in range(IDX_HEADS // 2):
            xx = jnp.dot(kx, iqt_ref[hp], preferred_element_type=F32)
            sc = (sc + jnp.maximum(xx[:, :LANES], 0.0) * iw_rows[2 * hp]
                  + jnp.maximum(xx[:, LANES:], 0.0) * iw_rows[2 * hp + 1])
        key = _sortable_key(sc + 0.0)
        if masked:
            key = jnp.where(off + k_row <= q_pos, key, INT_MIN)
        keys_ref[pl.ds(off, SEL_ROWS), :] = key
        u = key ^ INT_MIN
        for g in range(SEL_ROWS // PLANE_ROWS):
            words = [u[g * PLANE_ROWS + 8 * j:g * PLANE_ROWS + 8 * j + 8] for j in range(32)]
            for i, plane in enumerate(_bit_transpose32(words)):
                planes_ref[c * (SEL_ROWS // PLANE_ROWS) + g, i] = plane

    _chunk_loop(lambda c, _: score_chunk(c, False), n_sel - 1, SCORE_UNROLL)
    score_chunk(n_sel - 1, True)

    def per_query_total(counts):
        while len(counts) > 1:
            counts = [a + b for a, b in zip(counts[::2], counts[1::2])] + counts[len(counts) & ~1:]
        return jnp.sum(counts[0], axis=0, keepdims=True)

    n_grp = n_sel * (SEL_ROWS // PLANE_ROWS)
    all_groups = eq_ref.shape[0]
    for g in range(all_groups):
        eq_ref[g] = jnp.where(g < n_grp, jnp.full((8, BLOCK), -1, jnp.int32), jnp.zeros((8, BLOCK), jnp.int32))

    def sweep(cur_plane, prev_plane, prev_keep):
        counts = []
        for g in range(all_groups):
            eq = eq_ref[g]
            if prev_plane is not None:
                eq = eq & (planes_ref[g, prev_plane] ^ ~prev_keep)
                eq_ref[g] = eq
            ones = eq if cur_plane is None else eq & planes_ref[g, cur_plane]
            counts.append(lax.population_count(ones))
        return per_query_total(counts)

    def decide(bit, cnt_ones, thr_u, above):
        total = above + cnt_ones
        keep = jnp.where(total >= k_sel, -1, 0)
        return thr_u | (keep & bit), jnp.where(keep < 0, above, total), keep

    zeros = jnp.zeros((1, BLOCK), jnp.int32)
    state = decide(INT_MIN, sweep(0, None, None), zeros, zeros)

    def radix_step(it, state):
        thr_u, above, keep = state
        bit = lax.shift_left(jnp.int32(1), jnp.int32(31) - it)
        return decide(bit, sweep(it, it - 1, keep), thr_u, above)

    thr_u, above, keep = lax.fori_loop(1, 32, radix_step, state)
    cnt_thr = above + sweep(None, 31, keep)
    thr = thr_u ^ INT_MIN

    has_k = thr > INT_MIN
    thr = jnp.maximum(thr, INT_MIN + 1)

    jcut_ref[...] = jnp.full(jcut_ref.shape, 2 ** 31 - 1, jnp.int32)

    @pl.when(jnp.max(jnp.where(has_k, cnt_thr, 0)) > k_sel)
    def _():
        excess = jnp.where(has_k, cnt_thr - k_sel, 0)
        need = jnp.where(excess > 0, k_sel - above, 1)
        sub = lax.broadcasted_iota(jnp.int32, (8, BLOCK), 0)

        def ties_below(t):
            counts = []
            for g in range(all_groups):
                words = jnp.clip((t - g * PLANE_ROWS - sub + 7) >> 3, 0, 32)
                top = jnp.where(words > 0, lax.shift_left(jnp.int32(-1), 32 - words), 0)
                counts.append(lax.population_count(eq_ref[g] & top))
            return per_query_total(counts)

        def tie_bisect(it, t):
            cand = t + lax.shift_left(jnp.int32(1), jnp.int32(seq.bit_length() - 2) - it)
            return jnp.where(ties_below(cand) < need, cand, t)

        last = lax.fori_loop(0, seq.bit_length() - 1, tie_bisect, jnp.zeros((1, BLOCK), jnp.int32))
        jcut = jnp.where(excess > 0, last, 2 ** 31 - 1)
        jcut_ref[...] = jnp.broadcast_to(jcut, jcut_ref.shape)

    jcut = jcut_ref[0:1, :]

    def emit(c, carry):
        off = pl.multiple_of(c * SEL_ROWS, SEL_ROWS)
        keys = keys_ref[pl.ds(off, SEL_ROWS), :]
        bar = jnp.where(off + k_row > jcut, thr + 1, thr)
        add = jnp.where(keys >= bar, 0.0, NEG)
        sel_ref[:, pl.ds(off, SEL_ROWS)] = jnp.concatenate(
            [add[j * BLOCK:(j + 1) * BLOCK].T for j in range(SEL_ROWS // BLOCK)], axis=1).astype(BF16)
        return carry

    _chunk_loop(emit, n_sel, SCORE_UNROLL)

    def fill(c, carry):
        off = pl.multiple_of(c * SEL_ROWS, SEL_ROWS)
        sel_ref[:, pl.ds(off, SEL_ROWS)] = jnp.full((BLOCK, SEL_ROWS), NEG, BF16)
        return carry

    lax.fori_loop(n_sel, seq // SEL_ROWS, fill, 0)


def _a_select(pf, batch, seq):
    t = pf.shape[0]
    nq = seq // BLOCK
    k_sel = min(TOPK_MAX, seq // 4)
    n_groups = seq // PLANE_ROWS
    nlane = IDX_HEADS * IDX_DIM
    r = np.arange(nlane)
    ph = np.zeros((nlane, IDX_HEADS * LANES), np.float32)
    plo = np.zeros((nlane, IDX_HEADS * LANES), np.float32)
    ph[r, (r // IDX_DIM) * LANES + r % IDX_DIM] = 1
    ph[r, (r // IDX_DIM) * LANES + 2 * IDX_DIM + r % IDX_DIM] = 1
    plo[r, (r // IDX_DIM) * LANES + IDX_DIM + r % IDX_DIM] = 1
    d = np.arange(IDX_DIM)
    pkh = np.zeros((LANES, LANES), np.float32)
    pkl = np.zeros((LANES, LANES), np.float32)
    pkh[d, d] = 1
    pkh[d, IDX_DIM + d] = 1
    pkl[d, 2 * IDX_DIM + d] = 1
    full = lambda shape: pl.BlockSpec(shape, lambda b, i: (0,) * len(shape))
    return pl.pallas_call(
        functools.partial(_a_select_kernel, seq=seq, k_sel=k_sel),
        grid=(batch, nq),
        in_specs=[pl.BlockSpec((BLOCK, PF_HALF), lambda b, i: (b * nq + i, 0)),
                  pl.BlockSpec((seq, LANES), lambda b, i: (b, nlane // LANES)),
                  full(ph.shape), full(plo.shape), full(pkh.shape), full(pkl.shape)],
        out_specs=pl.BlockSpec((BLOCK, seq), lambda b, i: (b * nq + i, 0)),
        out_shape=jax.ShapeDtypeStruct((t, seq), BF16),
        scratch_shapes=[pltpu.VMEM((seq, LANES), BF16), pltpu.VMEM((seq, LANES), jnp.int32),
                        pltpu.VMEM((n_groups, 32, 8, LANES), jnp.int32),
                        pltpu.VMEM((n_groups, 8, LANES), jnp.int32),
                        pltpu.VMEM((IDX_HEADS // 2, LANES, 2 * LANES), BF16),
                        pltpu.VMEM((8, LANES), jnp.int32)],
        compiler_params=_cparams(("parallel", "arbitrary")),
        name="a_select",
    )(pf, pf, jnp.asarray(ph, BF16), jnp.asarray(plo, BF16), jnp.asarray(pkh, BF16),
      jnp.asarray(pkl, BF16))


BAND_RBLK = 8


def _band_kernel(*refs, has_sink, want_lse, kv_div, rblk):
    if has_sink:
        sink_ref, refs = refs[0], refs[1:]
    q_ref, hm_ref, kp_ref, kc_ref, vp_ref, vc_ref, bias_ref, o_ref = refs[:8]
    first = pl.program_id(1) == 0
    lane = lax.broadcasted_iota(jnp.int32, (BLOCK, LANES), 1)
    col = lax.broadcasted_iota(jnp.int32, (BLOCK, 2 * BLOCK), 1)
    no_prev = jnp.logical_and(first, col < BLOCK)
    for p in range(BRANCH_WIDTH // LANES):
        kcol = (p // kv_div) * LANES
        kk = jnp.concatenate([kp_ref[:, kcol:kcol + LANES], kc_ref[:, kcol:kcol + LANES]], axis=0)
        vv = _with_ones(jnp.concatenate([vp_ref[:, kcol:kcol + LANES], vc_ref[:, kcol:kcol + LANES]], axis=0))
        for i in range(rblk):
            rows = slice(i * BLOCK, (i + 1) * BLOCK)
            q = q_ref[rows, p * LANES:(p + 1) * LANES]
            kblk = kk[i * BLOCK:(i + 2) * BLOCK]
            vblk = vv[i * BLOCK:(i + 2) * BLOCK]
            outs, lses = [], []
            for e in range(2):
                s = lax.dot_general(q * hm_ref[e], kblk, _CONTRACT_LANES, preferred_element_type=F32)
                s = s + bias_ref[2 * p + e]
                if i == 0:
                    s = jnp.where(no_prev, NEG, s)
                m = jnp.max(s, axis=1, keepdims=True)
                if has_sink:
                    sk = sink_ref[2 * p + e]
                    m = jnp.maximum(m, sk)
                pr = jnp.exp2(s - m)
                acc = jnp.dot(pr.astype(BF16), vblk, preferred_element_type=F32)
                den = acc[:, LANES:]
                if has_sink:
                    den = den + jnp.exp2(sk - m)
                outs.append(acc[:, :LANES] / den)
                lses.append(m + jnp.log2(den))
            o_ref[rows, p * LANES:(p + 1) * LANES] = jnp.where(lane < HEAD_DIM, outs[0], outs[1]).astype(o_ref.dtype)
            if want_lse:
                refs[8][rows, p * LANES:(p + 1) * LANES] = jnp.where(lane < HEAD_DIM, lses[0], lses[1])


def _band_attn(q_arr, k_arr, v_arr, bias, nsub, sub_len, qcb, kcb, vcb, kv_w, sinks=None, want_lse=False):
    t = q_arr.shape[0]
    rblk = min(BAND_RBLK, sub_len // BLOCK)
    step_rows = rblk * BLOCK
    nbs = sub_len // step_rows
    kv_div = BRANCH_WIDTH // kv_w

    def cur(u, n):
        return u * nbs + n

    def prev(u, n):
        return jnp.where(n == 0, u * nbs * rblk, (u * nbs + n) * rblk - 1)

    in_specs = [
        pl.BlockSpec((step_rows, BRANCH_WIDTH), lambda u, n: (cur(u, n), qcb)),
        pl.BlockSpec((2, 1, LANES), lambda u, n: (0, 0, 0)),
        pl.BlockSpec((BLOCK, kv_w), lambda u, n: (prev(u, n), kcb)),
        pl.BlockSpec((step_rows, kv_w), lambda u, n: (cur(u, n), kcb)),
        pl.BlockSpec((BLOCK, kv_w), lambda u, n: (prev(u, n), vcb)),
        pl.BlockSpec((step_rows, kv_w), lambda u, n: (cur(u, n), vcb)),
        pl.BlockSpec(bias.shape, lambda u, n: (0, 0, 0)),
    ]
    args = [q_arr, _pair_masks(), k_arr, k_arr, v_arr, v_arr, bias]
    if sinks is not None:
        in_specs = [pl.BlockSpec(memory_space=pltpu.SMEM)] + in_specs
        args = [sinks] + args
    o_spec = pl.BlockSpec((step_rows, BRANCH_WIDTH), lambda u, n: (cur(u, n), 0))
    out_specs = [o_spec]
    out_shape = [jax.ShapeDtypeStruct((t, BRANCH_WIDTH), BF16)]
    if want_lse:
        out_specs.append(o_spec)
        out_shape.append(jax.ShapeDtypeStruct((t, BRANCH_WIDTH), F32))
    return pl.pallas_call(
        functools.partial(_band_kernel, has_sink=sinks is not None, want_lse=want_lse, kv_div=kv_div, rblk=rblk),
        grid=(nsub, nbs),
        in_specs=in_specs,
        out_specs=out_specs,
        out_shape=out_shape,
        compiler_params=_cparams(("parallel", "arbitrary")),
        name="band_attn",
    )(*args)


def _toeplitz(w, rows, cols):
    length = rows + cols - 1
    u = jnp.pad(w[..., ::-1], [(0, 0)] * (w.ndim - 1) + [(0, 1)])
    flat = jnp.broadcast_to(u[..., None, :], w.shape[:-1] + (rows, length + 1)).reshape(w.shape[:-1] + (-1,))
    skew = flat[..., :rows * length].reshape(w.shape[:-1] + (rows, length))
    return skew[..., rows - 1:rows - 1 + cols]


def _band_bias(table, step, max_dist):
    rel = BLOCK + np.arange(BLOCK)[:, None] - np.arange(2 * BLOCK)[None, :]
    rel_vec = np.arange(BLOCK - (2 * BLOCK - 1), 2 * BLOCK)
    bias = _toeplitz(table[_t5_bucket(jnp.asarray(rel_vec * step))].T.astype(F32), BLOCK, 2 * BLOCK)
    ok = (rel >= 0) & (rel <= max_dist)
    return jnp.where(jnp.asarray(ok)[None], bias, NEG)


def _sigmoid(x):
    return 0.5 * jnp.tanh(0.5 * x) + 0.5


def _merge_kernel(x_ref, ya_ref, yb0_ref, yb1_ref, yb2_ref, l0_ref, l1_ref, l2_ref, yc_ref, yd_ref,
                  za_ref, zb_ref, zc_ref, zd_ref, ga_ref, gb_ref, gc_ref, gd_ref, wb_ref, wo_ref, *rest):
    if len(rest) == 7:
        ng_ref, o_ref, xn_ref, y1_ref, y2_ref, s1_ref, s2_ref = rest
    else:
        ng_ref = xn_ref = None
        o_ref, y1_ref, y2_ref, s1_ref, s2_ref = rest
    ncol = BRANCH_WIDTH // LANES
    for src, dst in ((yb1_ref, y1_ref), (yb2_ref, y2_ref), (l1_ref, s1_ref), (l2_ref, s2_ref)):
        dil = src.shape[1]
        for r in range(dil):
            blk = src[0, r].astype(F32)
            for c in range(ncol):
                dst[c, pl.ds(r, src.shape[2], stride=dil), :] = blk[:, c * LANES:(c + 1) * LANES]
    wide = lambda ref: jnp.concatenate([ref[c] for c in range(ncol)], axis=1)
    l0, l1, l2 = l0_ref[0, 0], wide(s1_ref), wide(s2_ref)
    mx = jnp.maximum(jnp.maximum(l0, l1), l2)
    w0, w1, w2 = jnp.exp2(l0 - mx), jnp.exp2(l1 - mx), jnp.exp2(l2 - mx)
    yb = (w0 * yb0_ref[0, 0].astype(F32) + w1 * wide(y1_ref) + w2 * wide(y2_ref)) / (w0 + w1 + w2)
    ys = (ya_ref[...].astype(F32), yb, yc_ref[...].astype(F32), yd_ref[...].astype(F32))
    zs = (za_ref, zb_ref, zc_ref, zd_ref)
    gs = (ga_ref, gb_ref, gc_ref, gd_ref)
    merged = jnp.zeros(o_ref.shape, F32)
    for n in range(N_BRANCH):
        z = zs[n][...].astype(F32)
        u = (ys[n] * (z * _sigmoid(z))).astype(BF16)
        proj = jnp.dot(u, wb_ref[n], preferred_element_type=F32)
        merged = merged + _sigmoid(gs[n][...].astype(F32)) * proj
    x_new = x_ref[...] + jnp.dot(merged.astype(BF16), wo_ref[...], preferred_element_type=F32)
    o_ref[...] = x_new
    if xn_ref is not None:
        xn_ref[...] = _rms_bf16(x_new, ng_ref[...])


def _merge(x2, ya, yb, lb, yc, yd, pp, wb, wo, batch, seq, next_gain=None, tm=512):
    t = x2.shape[0]
    bw = BRANCH_WIDTH
    nsb = seq // tm
    row = lambda w, c: pl.BlockSpec((tm, w), lambda i: (i, c))

    def sub(dil):
        return pl.BlockSpec((1, dil, tm // dil, bw), lambda i: (i // nsb, 0, i % nsb, 0))

    dils = [dil for _, dil in B_GROUPS]
    yb = [a.reshape(batch, dil, seq // dil, bw) for a, dil in zip(yb, dils)]
    lb = [a.reshape(batch, dil, seq // dil, bw) for a, dil in zip(lb, dils)]
    in_specs = ([row(D_MODEL, 0), row(bw, 0)] + [sub(dil) for dil in dils] * 2 + [row(bw, 0), row(bw, 0)]
                + [row(bw, PP_AZ // bw), row(bw, PP_BZ // bw), row(bw, PP_CZ // bw), row(bw, PP_DZ // bw)]
                + [row(D_MODEL, PP_G // D_MODEL + n) for n in range(N_BRANCH)]
                + [pl.BlockSpec((N_BRANCH, bw, D_MODEL), lambda i: (0, 0, 0)),
                   pl.BlockSpec((D_MODEL, D_MODEL), lambda i: (0, 0))])
    args = [x2, ya, yb[0], yb[1], yb[2], lb[0], lb[1], lb[2], yc, yd, pp, pp, pp, pp, pp, pp, pp, pp, wb, wo]
    out_specs = [row(D_MODEL, 0)]
    out_shape = [jax.ShapeDtypeStruct((t, D_MODEL), F32)]
    if next_gain is not None:
        in_specs.append(pl.BlockSpec((1, D_MODEL), lambda i: (0, 0)))
        args.append(next_gain.reshape(1, D_MODEL))
        out_specs.append(row(D_MODEL, 0))
        out_shape.append(jax.ShapeDtypeStruct((t, D_MODEL), BF16))
    return pl.pallas_call(
        _merge_kernel,
        grid=(t // tm,),
        in_specs=in_specs,
        out_specs=out_specs,
        out_shape=out_shape,
        scratch_shapes=[pltpu.VMEM((bw // LANES, tm, LANES), F32)] * 4,
        compiler_params=_cparams(("parallel",)),
        name="merge",
    )(*args)


def _w_in_columns():
    n_bq = len(B_GROUPS) * B_HEADS * HEAD_DIM
    groups = (
        (("a_q", A_HEADS * HEAD_DIM), ("a_k", A_HEADS * HEAD_DIM), ("a_v", A_HEADS * HEAD_DIM),
         ("a_z", BRANCH_WIDTH), ("a_iq", IDX_HEADS * IDX_DIM), ("a_ik", IDX_DIM), ("a_iw", IDX_HEADS)),
        (("b_q", n_bq), ("b_k", n_bq), ("b_v", n_bq), ("b_z", BRANCH_WIDTH)),
        (("c_q", C_Q_LORA), ("c_kv", C_KV_LORA), ("c_pe", C_ROPE), ("c_z", BRANCH_WIDTH)),
        (("d_q", D_HEADS * HEAD_DIM), ("d_k", D_KV_HEADS * HEAD_DIM), ("d_v", D_KV_HEADS * HEAD_DIM),
         ("d_z", BRANCH_WIDTH)),
        (("gates", N_BRANCH * D_MODEL),),
    )
    table, start = {}, 0
    for group in groups:
        for name, width in group:
            table[name] = (start, width)
            start += width
    return table


def _layer_weights(w_in, qk_gain_a, qk_gain_b, qk_gain_d):
    table = _w_in_columns()
    depth = w_in.shape[0]

    def cols(name, part=None):
        start, width = table[name]
        if part is not None:
            width //= part[1]
            start += part[0] * width
        return w_in[:, :, start:start + width]

    def dup_heads(a):
        return jnp.concatenate([a[..., :HEAD_DIM], a[..., :HEAD_DIM], a[..., HEAD_DIM:], a[..., HEAD_DIM:]], axis=-1)

    def zeros(width):
        return jnp.zeros((depth, D_MODEL, width), w_in.dtype)

    n_grp = len(B_GROUPS)
    w_pn = jnp.concatenate([cols("a_q"), cols("a_k"), cols("d_q"), dup_heads(cols("d_k")), dup_heads(cols("d_v"))],
                           axis=-1).astype(BF16)
    w_pp = jnp.concatenate([cols("gates"), cols("a_v"), cols("a_z"), cols("b_z"), cols("c_z"), cols("d_z")],
                           axis=-1).astype(BF16)
    w_bg = [jnp.concatenate([cols("b_q", (g, n_grp)), cols("b_k", (g, n_grp)), cols("b_v", (g, n_grp))],
                            axis=-1).astype(BF16) for g in range(n_grp)]
    idx = jnp.concatenate([cols("a_iq"), cols("a_ik"), cols("a_iw")], axis=-1)
    lora = jnp.concatenate([cols("c_q"), cols("c_kv"), cols("c_pe")], axis=-1)
    w_pf = jnp.concatenate([idx, zeros(PF_HALF - idx.shape[-1]), lora, zeros(PF_HALF - lora.shape[-1])],
                           axis=-1).astype(BF16)

    scale = HEAD_DIM ** -0.5 * LOG2E
    tile = lambda g, reps: jnp.tile(g, (1, reps))
    hg = jnp.concatenate([tile(qk_gain_a[:, 0] * scale, A_HEADS), tile(qk_gain_a[:, 1], A_HEADS),
                          tile(qk_gain_d[:, 0] * scale, D_HEADS), tile(qk_gain_d[:, 1], 2 * D_KV_HEADS),
                          jnp.ones((depth, PN_W - PN_DV), F32)], axis=-1)
    flag = jnp.concatenate([jnp.ones((PN_DV,), F32), jnp.zeros((PN_W - PN_DV,), F32)])
    hg_b = jnp.concatenate([tile(qk_gain_b[:, 0] * scale, B_HEADS), tile(qk_gain_b[:, 1], B_HEADS),
                            jnp.ones((depth, BRANCH_WIDTH), F32)], axis=-1)
    flag_b = jnp.concatenate([jnp.ones((2 * BRANCH_WIDTH,), F32), jnp.zeros((BRANCH_WIDTH,), F32)])
    return w_pn, w_pp, w_bg, w_pf, hg, flag, hg_b, flag_b


def _c_weights(qk_gain_c, w_q_b, w_kv_b, seq):
    src_q, src_k, gain_idx, head, pe_src, rope_j, rope_half = _c_layout_tables()
    take = lambda w, src: jnp.where(jnp.asarray(src >= 0), jnp.take(w, jnp.asarray(np.maximum(src, 0)), axis=-1), 0.0)
    wq = take(w_q_b, src_q).astype(BF16)
    wk = take(w_kv_b, src_k).astype(BF16)
    v_src = np.array([h * (C_NOPE + C_V) + C_NOPE + c for h in range(C_HEADS) for c in range(C_V)])
    wv = jnp.take(w_kv_b, jnp.asarray(v_src), axis=-1).astype(BF16)
    ppe = np.zeros((LANES, C_QK_W), np.float32)
    ln = np.nonzero(pe_src >= 0)[0]
    ppe[pe_src[ln], ln] = 1
    hp = head[:C_PAIR]
    grp = ((hp[:, None] == hp[None, :]) & (hp[:, None] >= 0)).astype(np.float32)
    g_take = lambda g: jnp.where(jnp.asarray(gain_idx >= 0), jnp.take(g, jnp.asarray(np.maximum(gain_idx, 0)), axis=-1), 0.0)
    gq = g_take(qk_gain_c[:, 0]) * ((C_NOPE + C_ROPE) ** -0.5 * LOG2E)
    gk = g_take(qk_gain_c[:, 1])
    freq = ROPE_THETA ** (-jnp.arange(C_ROPE_HALF, dtype=F32) / C_ROPE_HALF)
    ang = jnp.arange(seq).astype(F32)[:, None] * freq[None, :]
    cos_j, sin_j = jnp.cos(ang), jnp.sin(ang)
    place = np.zeros((C_ROPE_HALF, C_PAIR), np.float32)
    lanes = np.nonzero(rope_j[:C_PAIR] >= 0)[0]
    place[rope_j[lanes], lanes] = 1
    is_rope = jnp.asarray(rope_j[:C_PAIR] >= 0)
    hi = lax.Precision.HIGHEST
    cos = jnp.where(is_rope, jnp.dot(cos_j, jnp.asarray(place), precision=hi), 1.0)
    sin = jnp.dot(sin_j, jnp.asarray(place), precision=hi)
    s1 = jnp.where(jnp.asarray(rope_half[:C_PAIR] == 0), -sin, 0.0)
    s2 = jnp.where(jnp.asarray(rope_half[:C_PAIR] == 1), sin, 0.0)
    return wq, wk, wv, jnp.asarray(ppe, BF16), jnp.asarray(grp, BF16), gq, gk, cos, s1, s2


def _a_bias_tiles(table):
    dist_vec = np.arange(-(BLOCK - 1), A_BIAS_TILES * BLOCK)
    vec = table[_t5_bucket(jnp.asarray(dist_vec))].T.astype(F32)
    windows = jnp.stack([vec[:, d * BLOCK:d * BLOCK + 2 * BLOCK - 1] for d in range(A_BIAS_TILES)], axis=1)
    return _toeplitz(windows, BLOCK, BLOCK)


def kernel(x, norm_gain, w_in, qk_gain_a, qk_gain_b, qk_gain_c, qk_gain_d, c_q_gain, c_kv_gain,
           w_q_b, w_kv_b, sinks, rel_bias, w_branch, w_out):
    batch, seq, d_model = x.shape
    depth = w_in.shape[0]
    t = batch * seq
    x2 = x.reshape(t, d_model)

    w_pn, w_pp, w_bg, w_pf, hg, flag, hg_b, flag_b = _layer_weights(w_in, qk_gain_a, qk_gain_b, qk_gain_d)
    wq, wk, wv, ppe, grp, gq, gk, cos, s1, s2 = _c_weights(qk_gain_c, w_q_b, w_kv_b, seq)
    wb = w_branch.astype(BF16)
    wo = w_out.astype(BF16)

    rel_bias2 = rel_bias * LOG2E
    bias_a = _a_bias_tiles(rel_bias2[:, :A_HEADS])
    bias_b = [_band_bias(rel_bias2[:, A_HEADS + g * B_HEADS:A_HEADS + (g + 1) * B_HEADS], dil, window // dil)
              for g, (window, dil) in enumerate(B_GROUPS)]
    bias_d = _band_bias(rel_bias2[:, N_BIAS_HEADS - D_HEADS:], 1, D_WINDOW - 1)

    xn = _norm(x2, norm_gain[0])
    for l in range(depth):
        pn = _proj(xn, w_pn[l], BF16, head_gain=hg[l], flag=flag)
        pp = _proj(xn, w_pp[l], BF16)
        pf = _proj(xn, w_pf[l], F32)

        sel = _a_select(pf, batch, seq)
        ya = _a_attn(pn, pp, sel, bias_a, batch, seq)

        yb, lb = [], []
        for g, (window, dil) in enumerate(B_GROUPS):
            bg = _proj(xn, w_bg[g][l], BF16, head_gain=hg_b[l], flag=flag_b, dil=dil, seq=seq)
            o, lse = _band_attn(bg, bg, bg, bias_b[g], batch * dil, seq // dil, 0, 1, 2, BRANCH_WIDTH,
                                want_lse=True)
            yb.append(o)
            lb.append(lse)

        qc, kc, vc = _c_prep(pf, c_q_gain[l].reshape(1, -1), c_kv_gain[l].reshape(1, -1), wq[l], wk[l], wv[l],
                             ppe, grp, gq[l].reshape(1, -1), gk[l].reshape(1, -1), cos, s1, s2, seq)
        yc = _c_attn(qc, kc, vc, batch, seq)

        (yd,) = _band_attn(pn, pn, pn, bias_d, batch, seq, PN_DQ // BRANCH_WIDTH, PN_DK // DKV_DUP_W,
                           PN_DV // DKV_DUP_W, DKV_DUP_W, sinks=sinks[l] * LOG2E)

        if l + 1 < depth:
            x2, xn = _merge(x2, ya, yb, lb, yc, yd, pp, wb[l], wo[l], batch, seq, next_gain=norm_gain[l + 1])
        else:
            (x2,) = _merge(x2, ya, yb, lb, yc, yd, pp, wb[l], wo[l], batch, seq)
    return x2.reshape(batch, seq, d_model)
```
